```python
import math
import jax, jax.numpy as jnp
from jax import lax
import numpy as np


D_MODEL = 1024
BATCH = 8
SEQ = 8192
DEPTH = 1

MEM_TOKENS = 256
S5_WIDTH = D_MODEL // 4
S5_GROUP_CH = 16
S5_GROUPS = S5_WIDTH // S5_GROUP_CH
S5_STATE = 64
S5_MAX_RE = -1e-4
S5_DT_MIN = 1e-3
S5_DT_MAX = 1e-1
MLA_HEADS = 8
MLA_NOPE_DIM = 64
MLA_ROPE_DIM = 32
MLA_QK_DIM = MLA_NOPE_DIM + MLA_ROPE_DIM
MLA_V_DIM = 64
MLA_Q_RANK = D_MODEL // 4
MLA_KV_RANK = D_MODEL // 4
ROPE_THETA = 10000.0
Q_BLOCK = 128
XATTN_HEADS = 4
XATTN_HEAD_DIM = D_MODEL // XATTN_HEADS
MLP_HIDDEN = 4 * D_MODEL
LN_EPS = 1e-5
RMS_EPS = 1e-6
NEG_INF = -1e30
POS_OFFSET_MAX = 4096
DN_ALPHA = (2.0 * DEPTH) ** 0.25
DN_BETA = (8.0 * DEPTH) ** -0.25
IN_S5 = S5_WIDTH
IN_Q = MLA_Q_RANK
IN_KV = MLA_KV_RANK
IN_KR = MLA_ROPE_DIM
IN_GATE = 2 * D_MODEL
IN_WIDTH = IN_S5 + IN_Q + IN_KV + IN_KR + IN_GATE

kernel_name = "hybrid_s5_mla_gated_deepnorm_layer"


def layer_norm(x, g, b):
    xf = x.astype(jnp.float32)
    mu = jnp.mean(xf, axis=-1, keepdims=True)
    xc = xf - mu
    var = jnp.mean(xc * xc, axis=-1, keepdims=True)
    return (xc * lax.rsqrt(var + LN_EPS) * g.astype(jnp.float32) + b.astype(jnp.float32)).astype(x.dtype)


def rms_norm(x, g):
    xf = x.astype(jnp.float32)
    return (xf * lax.rsqrt(jnp.mean(xf * xf, axis=-1, keepdims=True) + RMS_EPS) * g.astype(jnp.float32)).astype(x.dtype)


def rope_tables(positions):
    inv = ROPE_THETA ** (-jnp.arange(0, MLA_ROPE_DIM, 2, dtype=jnp.float32) / MLA_ROPE_DIM)
    ang = positions.astype(jnp.float32)[..., None] * inv
    return jnp.cos(ang)[:, :, None, :], jnp.sin(ang)[:, :, None, :]


def apply_rope(x, cos, sin):
    xf = x.astype(jnp.float32)
    x1, x2 = jnp.split(xf, 2, axis=-1)
    return jnp.concatenate([x1 * cos - x2 * sin, x1 * sin + x2 * cos], axis=-1).astype(x.dtype)


def _complex_scan_op(e1, e2):
    a1r, a1i, b1r, b1i = e1
    a2r, a2i, b2r, b2i = e2
    ar = a1r * a2r - a1i * a2i
    ai = a1r * a2i + a1i * a2r
    br = a2r * b1r - a2i * b1i + b2r
    bi = a2r * b1i + a2i * b1r + b2i
    return (ar, ai, br, bi)


def s5_ssm(u, lam_re, lam_im, log_dt, b_re, b_im, c_re, c_im, d_skip):
    bsz, seq, _ = u.shape
    uf = u.astype(jnp.float32).reshape(bsz, seq, S5_GROUPS, S5_GROUP_CH)
    lr = jnp.minimum(lam_re.astype(jnp.float32), S5_MAX_RE)
    li = lam_im.astype(jnp.float32)
    dt = jnp.exp(log_dt.astype(jnp.float32))[:, None]
    mag = jnp.exp(lr * dt)
    ang = li * dt
    ab_re = mag * jnp.cos(ang)
    ab_im = mag * jnp.sin(ang)
    den = lr * lr + li * li
    nr = ab_re - 1.0
    f_re = ((nr * lr + ab_im * li) / den)[..., None]
    f_im = ((ab_im * lr - nr * li) / den)[..., None]
    br = b_re.astype(jnp.float32)
    bi = b_im.astype(jnp.float32)
    bb_re = f_re * br - f_im * bi
    bb_im = f_re * bi + f_im * br
    bu_re = jnp.einsum('bsgh,gph->bsgp', uf, bb_re)
    bu_im = jnp.einsum('bsgh,gph->bsgp', uf, bb_im)
    a_re = jnp.broadcast_to(ab_re[None, None], (1, seq, S5_GROUPS, S5_STATE))
    a_im = jnp.broadcast_to(ab_im[None, None], (1, seq, S5_GROUPS, S5_STATE))
    _, _, h_re, h_im = lax.associative_scan(_complex_scan_op, (a_re, a_im, bu_re, bu_im), axis=1)
    y = (jnp.einsum('bsgp,ghp->bsgh', h_re, c_re.astype(jnp.float32))
         - jnp.einsum('bsgp,ghp->bsgh', h_im, c_im.astype(jnp.float32)))
    y = y + d_skip.astype(jnp.float32).reshape(S5_GROUPS, S5_GROUP_CH) * uf
    return y.reshape(bsz, seq, S5_WIDTH)


def causal_block_attention(q, k, v):
    bsz, seq, heads, dqk = q.shape
    nblk = seq // Q_BLOCK
    scale = dqk ** -0.5
    qb = jnp.moveaxis(q.reshape(bsz, nblk, Q_BLOCK, heads, dqk), 1, 0)
    starts = jnp.arange(nblk, dtype=jnp.int32) * Q_BLOCK
    kpos = jnp.arange(seq, dtype=jnp.int32)

    def block(args):
        qi, start = args
        s = jnp.einsum('bqhd,bkhd->bhqk', qi, k).astype(jnp.float32) * scale
        qpos = start + jnp.arange(Q_BLOCK, dtype=jnp.int32)
        s = jnp.where(kpos[None, :] <= qpos[:, None], s, NEG_INF)
        p = jax.nn.softmax(s, axis=-1).astype(v.dtype)
        return jnp.einsum('bhqk,bkhd->bqhd', p, v)

    o = lax.map(block, (qb, starts))
    return jnp.moveaxis(o, 0, 1).reshape(bsz, seq, heads * v.shape[-1])


def hybrid_mixer(h, cos, sin, w_in, s5_lam_re, s5_lam_im, s5_log_dt, s5_b_re, s5_b_im,
                 s5_c_re, s5_c_im, s5_d, w_glu, q_norm_g, w_uq, kv_norm_g, w_ukv, w_oa, w_o):
    bsz, seq, _ = h.shape
    z = h @ w_in
    o1 = IN_S5
    o2 = o1 + IN_Q
    o3 = o2 + IN_KV
    o4 = o3 + IN_KR
    u = z[..., :o1]
    c_q = z[..., o1:o2]
    c_kv = z[..., o2:o3]
    k_r = z[..., o3:o4]
    gate = z[..., o4:]
    y = s5_ssm(u, s5_lam_re, s5_lam_im, s5_log_dt, s5_b_re, s5_b_im, s5_c_re, s5_c_im, s5_d).astype(h.dtype)
    y = jax.nn.gelu(y, approximate=False) @ w_glu
    s_out = y[..., :D_MODEL] * jax.nn.sigmoid(y[..., D_MODEL:])
    q = (rms_norm(c_q, q_norm_g) @ w_uq).reshape(bsz, seq, MLA_HEADS, MLA_QK_DIM)
    q = jnp.concatenate([q[..., :MLA_NOPE_DIM], apply_rope(q[..., MLA_NOPE_DIM:], cos, sin)], axis=-1)
    kv = (rms_norm(c_kv, kv_norm_g) @ w_ukv).reshape(bsz, seq, MLA_HEADS, MLA_NOPE_DIM + MLA_V_DIM)
    k_rope = apply_rope(k_r[:, :, None, :], cos, sin)
    k = jnp.concatenate([kv[..., :MLA_NOPE_DIM],
                         jnp.broadcast_to(k_rope, (bsz, seq, MLA_HEADS, MLA_ROPE_DIM))], axis=-1)
    v = kv[..., MLA_NOPE_DIM:]
    a_out = causal_block_attention(q, k, v) @ w_oa
    g_s = jax.nn.sigmoid(gate[..., :D_MODEL])
    g_a = jax.nn.sigmoid(gate[..., D_MODEL:])
    return (g_s * s_out + g_a * a_out) @ w_o


def memory_cross_attention(h, mem, w_xq, w_xk, w_xv, w_xo):
    bsz, seq, _ = h.shape
    m = mem.shape[1]
    q = (h @ w_xq).reshape(bsz, seq, XATTN_HEADS, XATTN_HEAD_DIM)
    k = (mem @ w_xk).reshape(bsz, m, XATTN_HEADS, XATTN_HEAD_DIM)
    v = (mem @ w_xv).reshape(bsz, m, XATTN_HEADS, XATTN_HEAD_DIM)
    s = jnp.einsum('bshd,bmhd->bhsm', q, k).astype(jnp.float32) * (XATTN_HEAD_DIM ** -0.5)
    p = jax.nn.softmax(s, axis=-1).astype(v.dtype)
    o = jnp.einsum('bhsm,bmhd->bshd', p, v).reshape(bsz, seq, D_MODEL)
    return o @ w_xo


def squared_relu_mlp(h, w_up, w_down):
    return jnp.square(jax.nn.relu(h @ w_up)) @ w_down


def _fwd_setup_inputs(seed: int = 0) -> dict:
    key = jax.random.key(seed)
    ks = jax.random.split(key, 40)
    f32 = jnp.float32

    def nrm(k, shape, scale):
        return jax.random.normal(k, shape, f32) * scale

    L = DEPTH
    G, P, H = S5_GROUPS, S5_STATE, S5_GROUP_CH
    n = jnp.arange(P, dtype=f32)
    positions = (jax.random.randint(ks[2], (BATCH, 1), 0, POS_OFFSET_MAX, dtype=jnp.int32)
                 + jnp.arange(SEQ, dtype=jnp.int32)[None, :])
    return {
        "x": nrm(ks[0], (BATCH, SEQ, D_MODEL), 1.0),
        "mem": nrm(ks[1], (BATCH, MEM_TOKENS, D_MODEL), 1.0),
        "positions": positions,
        "ln_in_g": 1.0 + nrm(ks[3], (D_MODEL,), 0.02),
        "ln_in_b": nrm(ks[4], (D_MODEL,), 0.02),
        "w_in": nrm(ks[5], (L, D_MODEL, IN_WIDTH), D_MODEL ** -0.5),
        "s5_lam_re": -0.5 + nrm(ks[6], (L, G, P), 0.01),
        "s5_lam_im": math.pi * n + nrm(ks[7], (L, G, P), 0.01),
        "s5_log_dt": jax.random.uniform(ks[8], (L, G), f32, math.log(S5_DT_MIN), math.log(S5_DT_MAX)),
        "s5_b_re": nrm(ks[9], (L, G, P, H), (2.0 * H) ** -0.5),
        "s5_b_im": nrm(ks[10], (L, G, P, H), (2.0 * H) ** -0.5),
        "s5_c_re": nrm(ks[11], (L, G, H, P), P ** -0.5),
        "s5_c_im": nrm(ks[12], (L, G, H, P), P ** -0.5),
        "s5_d": nrm(ks[13], (L, S5_WIDTH), 1.0),
        "w_glu": nrm(ks[14], (L, S5_WIDTH, 2 * D_MODEL), S5_WIDTH ** -0.5),
        "q_norm_g": 1.0 + nrm(ks[15], (L, MLA_Q_RANK), 0.02),
        "w_uq": nrm(ks[16], (L, MLA_Q_RANK, MLA_HEADS * MLA_QK_DIM), MLA_Q_RANK ** -0.5),
        "kv_norm_g": 1.0 + nrm(ks[17], (L, MLA_KV_RANK), 0.02),
        "w_ukv": nrm(ks[18], (L, MLA_KV_RANK, MLA_HEADS * (MLA_NOPE_DIM + MLA_V_DIM)), MLA_KV_RANK ** -0.5),
        "w_oa": nrm(ks[19], (L, MLA_HEADS * MLA_V_DIM, D_MODEL), (MLA_HEADS * MLA_V_DIM) ** -0.5),
        "w_o": nrm(ks[20], (L, D_MODEL, D_MODEL), DN_BETA * D_MODEL ** -0.5),
        "ln1_g": 1.0 + nrm(ks[21], (L, D_MODEL), 0.02),
        "ln1_b": nrm(ks[22], (L, D_MODEL), 0.02),
        "w_xq": nrm(ks[23], (L, D_MODEL, D_MODEL), D_MODEL ** -0.5),
        "w_xk": nrm(ks[24], (L, D_MODEL, D_MODEL), D_MODEL ** -0.5),
        "w_xv": nrm(ks[25], (L, D_MODEL, D_MODEL), DN_BETA * D_MODEL ** -0.5),
        "w_xo": nrm(ks[26], (L, D_MODEL, D_MODEL), DN_BETA * D_MODEL ** -0.5),
        "ln2_g": 1.0 + nrm(ks[27], (L, D_MODEL), 0.02),
        "ln2_b": nrm(ks[28], (L, D_MODEL), 0.02),
        "w_up": nrm(ks[29], (L, D_MODEL, MLP_HIDDEN), DN_BETA * D_MODEL ** -0.5),
        "w_down": nrm(ks[30], (L, MLP_HIDDEN, D_MODEL), DN_BETA * MLP_HIDDEN ** -0.5),
        "ln3_g": 1.0 + nrm(ks[31], (L, D_MODEL), 0.02),
        "ln3_b": nrm(ks[32], (L, D_MODEL), 0.02),
    }


def _fwd_reference(x, mem, positions, ln_in_g, ln_in_b, w_in, s5_lam_re, s5_lam_im, s5_log_dt,
              s5_b_re, s5_b_im, s5_c_re, s5_c_im, s5_d, w_glu, q_norm_g, w_uq, kv_norm_g, w_ukv,
              w_oa, w_o, ln1_g, ln1_b, w_xq, w_xk, w_xv, w_xo, ln2_g, ln2_b, w_up, w_down,
              ln3_g, ln3_b):
    cos, sin = rope_tables(positions)
    h = layer_norm(x, ln_in_g, ln_in_b)
    for l in range(DEPTH):
        mix = hybrid_mixer(h, cos, sin, w_in[l], s5_lam_re[l], s5_lam_im[l], s5_log_dt[l],
                           s5_b_re[l], s5_b_im[l], s5_c_re[l], s5_c_im[l], s5_d[l], w_glu[l],
                           q_norm_g[l], w_uq[l], kv_norm_g[l], w_ukv[l], w_oa[l], w_o[l])
        h = layer_norm(DN_ALPHA * h + mix, ln1_g[l], ln1_b[l])
        xa = memory_cross_attention(h, mem, w_xq[l], w_xk[l], w_xv[l], w_xo[l])
        h = layer_norm(DN_ALPHA * h + xa, ln2_g[l], ln2_b[l])
        ff = squared_relu_mlp(h, w_up[l], w_down[l])
        h = layer_norm(DN_ALPHA * h + ff, ln3_g[l], ln3_b[l])
    return h


import jax as _jax
import jax.numpy as _jnp

TWIN_FORMAT = 'train_step'
FWD_PARAMS = ['x', 'mem', 'positions', 'ln_in_g', 'ln_in_b', 'w_in', 's5_lam_re', 's5_lam_im', 's5_log_dt', 's5_b_re', 's5_b_im', 's5_c_re', 's5_c_im', 's5_d', 'w_glu', 'q_norm_g', 'w_uq', 'kv_norm_g', 'w_ukv', 'w_oa', 'w_o', 'ln1_g', 'ln1_b', 'w_xq', 'w_xk', 'w_xv', 'w_xo', 'ln2_g', 'ln2_b', 'w_up', 'w_down', 'ln3_g', 'ln3_b']
TWIN_WEIGHTS = ['ln_in_g', 'ln_in_b', 'w_in', 's5_lam_re', 's5_lam_im', 's5_log_dt', 's5_b_re', 's5_b_im', 's5_c_re', 's5_c_im', 's5_d', 'w_glu', 'q_norm_g', 'w_uq', 'kv_norm_g', 'w_ukv', 'w_oa', 'w_o', 'ln1_g', 'ln1_b', 'w_xq', 'w_xk', 'w_xv', 'w_xo', 'ln2_g', 'ln2_b', 'w_up', 'w_down', 'ln3_g', 'ln3_b']
TWIN_DIFF_INPUT = 'x'
TWIN_INPUTS = ['x', 'mem', 'positions', 'ln_in_g', 'ln_in_b', 'w_in', 's5_lam_re', 's5_lam_im', 's5_log_dt', 's5_b_re', 's5_b_im', 's5_c_re', 's5_c_im', 's5_d', 'w_glu', 'q_norm_g', 'w_uq', 'kv_norm_g', 'w_ukv', 'w_oa', 'w_o', 'ln1_g', 'ln1_b', 'w_xq', 'w_xk', 'w_xv', 'w_xo', 'ln2_g', 'ln2_b', 'w_up', 'w_down', 'ln3_g', 'ln3_b', 'loss_target', 'm_ln_in_g', 'm_ln_in_b', 'm_w_in', 'm_s5_lam_re', 'm_s5_lam_im', 'm_s5_log_dt', 'm_s5_b_re', 'm_s5_b_im', 'm_s5_c_re', 'm_s5_c_im', 'm_s5_d', 'm_w_glu', 'm_q_norm_g', 'm_w_uq', 'm_kv_norm_g', 'm_w_ukv', 'm_w_oa', 'm_w_o', 'm_ln1_g', 'm_ln1_b', 'm_w_xq', 'm_w_xk', 'm_w_xv', 'm_w_xo', 'm_ln2_g', 'm_ln2_b', 'm_w_up', 'm_w_down', 'm_ln3_g', 'm_ln3_b', 'v_ln_in_g', 'v_ln_in_b', 'v_w_in', 'v_s5_lam_re', 'v_s5_lam_im', 'v_s5_log_dt', 'v_s5_b_re', 'v_s5_b_im', 'v_s5_c_re', 'v_s5_c_im', 'v_s5_d', 'v_w_glu', 'v_q_norm_g', 'v_w_uq', 'v_kv_norm_g', 'v_w_ukv', 'v_w_oa', 'v_w_o', 'v_ln1_g', 'v_ln1_b', 'v_w_xq', 'v_w_xk', 'v_w_xv', 'v_w_xo', 'v_ln2_g', 'v_ln2_b', 'v_w_up', 'v_w_down', 'v_ln3_g', 'v_ln3_b']
TWIN_OUTPUTS = ['loss', 'grad_x', 'grad_ln_in_g', 'grad_ln_in_b', 'grad_w_in', 'grad_s5_lam_re', 'grad_s5_lam_im', 'grad_s5_log_dt', 'grad_s5_b_re', 'grad_s5_b_im', 'grad_s5_c_re', 'grad_s5_c_im', 'grad_s5_d', 'grad_w_glu', 'grad_q_norm_g', 'grad_w_uq', 'grad_kv_norm_g', 'grad_w_ukv', 'grad_w_oa', 'grad_w_o', 'grad_ln1_g', 'grad_ln1_b', 'grad_w_xq', 'grad_w_xk', 'grad_w_xv', 'grad_w_xo', 'grad_ln2_g', 'grad_ln2_b', 'grad_w_up', 'grad_w_down', 'grad_ln3_g', 'grad_ln3_b', 'delta_ln_in_g', 'delta_ln_in_b', 'delta_w_in', 'delta_s5_lam_re', 'delta_s5_lam_im', 'delta_s5_log_dt', 'delta_s5_b_re', 'delta_s5_b_im', 'delta_s5_c_re', 'delta_s5_c_im', 'delta_s5_d', 'delta_w_glu', 'delta_q_norm_g', 'delta_w_uq', 'delta_kv_norm_g', 'delta_w_ukv', 'delta_w_oa', 'delta_w_o', 'delta_ln1_g', 'delta_ln1_b', 'delta_w_xq', 'delta_w_xk', 'delta_w_xv', 'delta_w_xo', 'delta_ln2_g', 'delta_ln2_b', 'delta_w_up', 'delta_w_down', 'delta_ln3_g', 'delta_ln3_b', 'new_m_ln_in_g', 'new_m_ln_in_b', 'new_m_w_in', 'new_m_s5_lam_re', 'new_m_s5_lam_im', 'new_m_s5_log_dt', 'new_m_s5_b_re', 'new_m_s5_b_im', 'new_m_s5_c_re', 'new_m_s5_c_im', 'new_m_s5_d', 'new_m_w_glu', 'new_m_q_norm_g', 'new_m_w_uq', 'new_m_kv_norm_g', 'new_m_w_ukv', 'new_m_w_oa', 'new_m_w_o', 'new_m_ln1_g', 'new_m_ln1_b', 'new_m_w_xq', 'new_m_w_xk', 'new_m_w_xv', 'new_m_w_xo', 'new_m_ln2_g', 'new_m_ln2_b', 'new_m_w_up', 'new_m_w_down', 'new_m_ln3_g', 'new_m_ln3_b', 'new_v_ln_in_g', 'new_v_ln_in_b', 'new_v_w_in', 'new_v_s5_lam_re', 'new_v_s5_lam_im', 'new_v_s5_log_dt', 'new_v_s5_b_re', 'new_v_s5_b_im', 'new_v_s5_c_re', 'new_v_s5_c_im', 'new_v_s5_d', 'new_v_w_glu', 'new_v_q_norm_g', 'new_v_w_uq', 'new_v_kv_norm_g', 'new_v_w_ukv', 'new_v_w_oa', 'new_v_w_o', 'new_v_ln1_g', 'new_v_ln1_b', 'new_v_w_xq', 'new_v_w_xk', 'new_v_w_xv', 'new_v_w_xo', 'new_v_ln2_g', 'new_v_ln2_b', 'new_v_w_up', 'new_v_w_down', 'new_v_ln3_g', 'new_v_ln3_b']
TWIN_LEAF_KINDS = {'loss': 'loss', 'grad_x': 'grad_x', 'grad_ln_in_g': 'grad_w', 'grad_ln_in_b': 'grad_w', 'grad_w_in': 'grad_w', 'grad_s5_lam_re': 'grad_w', 'grad_s5_lam_im': 'grad_w', 'grad_s5_log_dt': 'grad_w', 'grad_s5_b_re': 'grad_w', 'grad_s5_b_im': 'grad_w', 'grad_s5_c_re': 'grad_w', 'grad_s5_c_im': 'grad_w', 'grad_s5_d': 'grad_w', 'grad_w_glu': 'grad_w', 'grad_q_norm_g': 'grad_w', 'grad_w_uq': 'grad_w', 'grad_kv_norm_g': 'grad_w', 'grad_w_ukv': 'grad_w', 'grad_w_oa': 'grad_w', 'grad_w_o': 'grad_w', 'grad_ln1_g': 'grad_w', 'grad_ln1_b': 'grad_w', 'grad_w_xq': 'grad_w', 'grad_w_xk': 'grad_w', 'grad_w_xv': 'grad_w', 'grad_w_xo': 'grad_w', 'grad_ln2_g': 'grad_w', 'grad_ln2_b': 'grad_w', 'grad_w_up': 'grad_w', 'grad_w_down': 'grad_w', 'grad_ln3_g': 'grad_w', 'grad_ln3_b': 'grad_w', 'delta_ln_in_g': 'delta_w', 'delta_ln_in_b': 'delta_w', 'delta_w_in': 'delta_w', 'delta_s5_lam_re': 'delta_w', 'delta_s5_lam_im': 'delta_w', 'delta_s5_log_dt': 'delta_w', 'delta_s5_b_re': 'delta_w', 'delta_s5_b_im': 'delta_w', 'delta_s5_c_re': 'delta_w', 'delta_s5_c_im': 'delta_w', 'delta_s5_d': 'delta_w', 'delta_w_glu': 'delta_w', 'delta_q_norm_g': 'delta_w', 'delta_w_uq': 'delta_w', 'delta_kv_norm_g': 'delta_w', 'delta_w_ukv': 'delta_w', 'delta_w_oa': 'delta_w', 'delta_w_o': 'delta_w', 'delta_ln1_g': 'delta_w', 'delta_ln1_b': 'delta_w', 'delta_w_xq': 'delta_w', 'delta_w_xk': 'delta_w', 'delta_w_xv': 'delta_w', 'delta_w_xo': 'delta_w', 'delta_ln2_g': 'delta_w', 'delta_ln2_b': 'delta_w', 'delta_w_up': 'delta_w', 'delta_w_down': 'delta_w', 'delta_ln3_g': 'delta_w', 'delta_ln3_b': 'delta_w', 'new_m_ln_in_g': 'new_m', 'new_m_ln_in_b': 'new_m', 'new_m_w_in': 'new_m', 'new_m_s5_lam_re': 'new_m', 'new_m_s5_lam_im': 'new_m', 'new_m_s5_log_dt': 'new_m', 'new_m_s5_b_re': 'new_m', 'new_m_s5_b_im': 'new_m', 'new_m_s5_c_re': 'new_m', 'new_m_s5_c_im': 'new_m', 'new_m_s5_d': 'new_m', 'new_m_w_glu': 'new_m', 'new_m_q_norm_g': 'new_m', 'new_m_w_uq': 'new_m', 'new_m_kv_norm_g': 'new_m', 'new_m_w_ukv': 'new_m', 'new_m_w_oa': 'new_m', 'new_m_w_o': 'new_m', 'new_m_ln1_g': 'new_m', 'new_m_ln1_b': 'new_m', 'new_m_w_xq': 'new_m', 'new_m_w_xk': 'new_m', 'new_m_w_xv': 'new_m', 'new_m_w_xo': 'new_m', 'new_m_ln2_g': 'new_m', 'new_m_ln2_b': 'new_m', 'new_m_w_up': 'new_m', 'new_m_w_down': 'new_m', 'new_m_ln3_g': 'new_m', 'new_m_ln3_b': 'new_m', 'new_v_ln_in_g': 'new_v', 'new_v_ln_in_b': 'new_v', 'new_v_w_in': 'new_v', 'new_v_s5_lam_re': 'new_v', 'new_v_s5_lam_im': 'new_v', 'new_v_s5_log_dt': 'new_v', 'new_v_s5_b_re': 'new_v', 'new_v_s5_b_im': 'new_v', 'new_v_s5_c_re': 'new_v', 'new_v_s5_c_im': 'new_v', 'new_v_s5_d': 'new_v', 'new_v_w_glu': 'new_v', 'new_v_q_norm_g': 'new_v', 'new_v_w_uq': 'new_v', 'new_v_kv_norm_g': 'new_v', 'new_v_w_ukv': 'new_v', 'new_v_w_oa': 'new_v', 'new_v_w_o': 'new_v', 'new_v_ln1_g': 'new_v', 'new_v_ln1_b': 'new_v', 'new_v_w_xq': 'new_v', 'new_v_w_xk': 'new_v', 'new_v_w_xv': 'new_v', 'new_v_w_xo': 'new_v', 'new_v_ln2_g': 'new_v', 'new_v_ln2_b': 'new_v', 'new_v_w_up': 'new_v', 'new_v_w_down': 'new_v', 'new_v_ln3_g': 'new_v', 'new_v_ln3_b': 'new_v'}


def _forward(args):
    return _fwd_reference(*[args[k] for k in FWD_PARAMS])


def _output_shape():
    out = _jax.eval_shape(lambda: _forward(_fwd_setup_inputs(0)))
    return out.shape, out.dtype

N_MICROBATCH = 1
ADAM_LR = 0.001
ADAM_B1 = 0.9
ADAM_B2 = 0.999
ADAM_EPS = 1e-08
ADAM_WD = 0.01
ADAM_STEP = 10
PER_EXAMPLE_BATCH_AXIS = {'x': 0, 'mem': 0, 'positions': 0, 'loss_target': 0}
SHARED_INPUTS = []
_WEIGHT_DTYPES = {'ln_in_g': _jnp.float32, 'ln_in_b': _jnp.float32, 'w_in': _jnp.float32, 's5_lam_re': _jnp.float32, 's5_lam_im': _jnp.float32, 's5_log_dt': _jnp.float32, 's5_b_re': _jnp.float32, 's5_b_im': _jnp.float32, 's5_c_re': _jnp.float32, 's5_c_im': _jnp.float32, 's5_d': _jnp.float32, 'w_glu': _jnp.float32, 'q_norm_g': _jnp.float32, 'w_uq': _jnp.float32, 'kv_norm_g': _jnp.float32, 'w_ukv': _jnp.float32, 'w_oa': _jnp.float32, 'w_o': _jnp.float32, 'ln1_g': _jnp.float32, 'ln1_b': _jnp.float32, 'w_xq': _jnp.float32, 'w_xk': _jnp.float32, 'w_xv': _jnp.float32, 'w_xo': _jnp.float32, 'ln2_g': _jnp.float32, 'ln2_b': _jnp.float32, 'w_up': _jnp.float32, 'w_down': _jnp.float32, 'ln3_g': _jnp.float32, 'ln3_b': _jnp.float32}
MOMENT_SCALE = {'ln_in_g': 1.744407e+00, 'ln_in_b': 7.753244e-01, 'w_in': 2.178345e-02, 's5_lam_re': 4.047231e-03, 's5_lam_im': 4.580301e-03, 's5_log_dt': 2.669360e+00, 's5_b_re': 2.352255e-03, 's5_b_im': 2.226709e-03, 's5_c_re': 3.127310e-03, 's5_c_im': 3.206330e-03, 's5_d': 8.997964e-02, 'w_glu': 2.610487e-02, 'q_norm_g': 2.494471e-02, 'w_uq': 1.408088e-02, 'kv_norm_g': 3.785365e-02, 'w_ukv': 1.797152e-02, 'w_oa': 1.475772e-02, 'w_o': 5.007329e-02, 'ln1_g': 1.773603e+00, 'ln1_b': 7.351714e-01, 'w_xq': 7.821186e-03, 'w_xk': 7.859110e-03, 'w_xv': 1.514021e-02, 'w_xo': 1.515631e-02, 'ln2_g': 1.779077e+00, 'ln2_b': 7.372264e-01, 'w_up': 5.158884e-02, 'w_down': 1.210016e-01, 'ln3_g': 6.405614e+01, 'ln3_b': 5.934436e+00}


def _to_microbatches(a, axis):
    t = _jnp.moveaxis(a, axis, 0)
    t = t.reshape((N_MICROBATCH, t.shape[0] // N_MICROBATCH) + t.shape[1:])
    return _jnp.moveaxis(t, 1, axis + 1)


def setup_inputs(seed: int = 0) -> dict:
    inp = _fwd_setup_inputs(seed)
    key = _jax.random.fold_in(_jax.random.key(seed), 7919)
    shape, _ = _output_shape()
    out = dict(inp)
    out["loss_target"] = _jax.random.normal(_jax.random.fold_in(key, 0), shape, _jnp.float32)
    for i, name in enumerate(TWIN_WEIGHTS):
        w = inp[name].astype(_jnp.float32)
        if MOMENT_SCALE is None:
            s = _jnp.sqrt(_jnp.mean(_jnp.square(w)) + 1e-30)
        else:
            s = MOMENT_SCALE[name]
        km, kv = _jax.random.split(_jax.random.fold_in(key, i + 1))
        out[name] = w
        out["m_" + name] = s * _jax.random.normal(km, w.shape, _jnp.float32)
        out["v_" + name] = (s * s) * _jax.random.uniform(kv, w.shape, _jnp.float32, 0.5, 1.5)
    if N_MICROBATCH > 1:
        for name, axis in PER_EXAMPLE_BATCH_AXIS.items():
            out[name] = _to_microbatches(out[name], axis)
    return {'x': out['x'], 'mem': out['mem'], 'positions': out['positions'], 'ln_in_g': out['ln_in_g'], 'ln_in_b': out['ln_in_b'], 'w_in': out['w_in'], 's5_lam_re': out['s5_lam_re'], 's5_lam_im': out['s5_lam_im'], 's5_log_dt': out['s5_log_dt'], 's5_b_re': out['s5_b_re'], 's5_b_im': out['s5_b_im'], 's5_c_re': out['s5_c_re'], 's5_c_im': out['s5_c_im'], 's5_d': out['s5_d'], 'w_glu': out['w_glu'], 'q_norm_g': out['q_norm_g'], 'w_uq': out['w_uq'], 'kv_norm_g': out['kv_norm_g'], 'w_ukv': out['w_ukv'], 'w_oa': out['w_oa'], 'w_o': out['w_o'], 'ln1_g': out['ln1_g'], 'ln1_b': out['ln1_b'], 'w_xq': out['w_xq'], 'w_xk': out['w_xk'], 'w_xv': out['w_xv'], 'w_xo': out['w_xo'], 'ln2_g': out['ln2_g'], 'ln2_b': out['ln2_b'], 'w_up': out['w_up'], 'w_down': out['w_down'], 'ln3_g': out['ln3_g'], 'ln3_b': out['ln3_b'], 'loss_target': out['loss_target'], 'm_ln_in_g': out['m_ln_in_g'], 'm_ln_in_b': out['m_ln_in_b'], 'm_w_in': out['m_w_in'], 'm_s5_lam_re': out['m_s5_lam_re'], 'm_s5_lam_im': out['m_s5_lam_im'], 'm_s5_log_dt': out['m_s5_log_dt'], 'm_s5_b_re': out['m_s5_b_re'], 'm_s5_b_im': out['m_s5_b_im'], 'm_s5_c_re': out['m_s5_c_re'], 'm_s5_c_im': out['m_s5_c_im'], 'm_s5_d': out['m_s5_d'], 'm_w_glu': out['m_w_glu'], 'm_q_norm_g': out['m_q_norm_g'], 'm_w_uq': out['m_w_uq'], 'm_kv_norm_g': out['m_kv_norm_g'], 'm_w_ukv': out['m_w_ukv'], 'm_w_oa': out['m_w_oa'], 'm_w_o': out['m_w_o'], 'm_ln1_g': out['m_ln1_g'], 'm_ln1_b': out['m_ln1_b'], 'm_w_xq': out['m_w_xq'], 'm_w_xk': out['m_w_xk'], 'm_w_xv': out['m_w_xv'], 'm_w_xo': out['m_w_xo'], 'm_ln2_g': out['m_ln2_g'], 'm_ln2_b': out['m_ln2_b'], 'm_w_up': out['m_w_up'], 'm_w_down': out['m_w_down'], 'm_ln3_g': out['m_ln3_g'], 'm_ln3_b': out['m_ln3_b'], 'v_ln_in_g': out['v_ln_in_g'], 'v_ln_in_b': out['v_ln_in_b'], 'v_w_in': out['v_w_in'], 'v_s5_lam_re': out['v_s5_lam_re'], 'v_s5_lam_im': out['v_s5_lam_im'], 'v_s5_log_dt': out['v_s5_log_dt'], 'v_s5_b_re': out['v_s5_b_re'], 'v_s5_b_im': out['v_s5_b_im'], 'v_s5_c_re': out['v_s5_c_re'], 'v_s5_c_im': out['v_s5_c_im'], 'v_s5_d': out['v_s5_d'], 'v_w_glu': out['v_w_glu'], 'v_q_norm_g': out['v_q_norm_g'], 'v_w_uq': out['v_w_uq'], 'v_kv_norm_g': out['v_kv_norm_g'], 'v_w_ukv': out['v_w_ukv'], 'v_w_oa': out['v_w_oa'], 'v_w_o': out['v_w_o'], 'v_ln1_g': out['v_ln1_g'], 'v_ln1_b': out['v_ln1_b'], 'v_w_xq': out['v_w_xq'], 'v_w_xk': out['v_w_xk'], 'v_w_xv': out['v_w_xv'], 'v_w_xo': out['v_w_xo'], 'v_ln2_g': out['v_ln2_g'], 'v_ln2_b': out['v_ln2_b'], 'v_w_up': out['v_w_up'], 'v_w_down': out['v_w_down'], 'v_ln3_g': out['v_ln3_g'], 'v_ln3_b': out['v_ln3_b']}


def _loss(weights, diff, rest, loss_target):
    with _jax.named_scope("forward"):
        args = {**rest, TWIN_DIFF_INPUT: diff, **{k: w.astype(_WEIGHT_DTYPES[k]) for k, w in weights.items()}}
        y = _forward(args)
    with _jax.named_scope("loss_head"):
        err = _jnp.square(y.astype(_jnp.float32) - loss_target)
        return 0.5 * _jnp.sum(_jnp.mean(err, axis=-1)) if err.ndim else 0.5 * err


def _adamw(w, g, m, v):
    m = ADAM_B1 * m + (1.0 - ADAM_B1) * g
    v = ADAM_B2 * v + (1.0 - ADAM_B2) * _jnp.square(g)
    m_hat = m / (1.0 - ADAM_B1 ** ADAM_STEP)
    v_hat = v / (1.0 - ADAM_B2 ** ADAM_STEP)
    delta = -ADAM_LR * (m_hat / (_jnp.sqrt(v_hat) + ADAM_EPS) + ADAM_WD * w)
    return delta, m, v


def reference(x, mem, positions, ln_in_g, ln_in_b, w_in, s5_lam_re, s5_lam_im, s5_log_dt, s5_b_re, s5_b_im, s5_c_re, s5_c_im, s5_d, w_glu, q_norm_g, w_uq, kv_norm_g, w_ukv, w_oa, w_o, ln1_g, ln1_b, w_xq, w_xk, w_xv, w_xo, ln2_g, ln2_b, w_up, w_down, ln3_g, ln3_b, loss_target, m_ln_in_g, m_ln_in_b, m_w_in, m_s5_lam_re, m_s5_lam_im, m_s5_log_dt, m_s5_b_re, m_s5_b_im, m_s5_c_re, m_s5_c_im, m_s5_d, m_w_glu, m_q_norm_g, m_w_uq, m_kv_norm_g, m_w_ukv, m_w_oa, m_w_o, m_ln1_g, m_ln1_b, m_w_xq, m_w_xk, m_w_xv, m_w_xo, m_ln2_g, m_ln2_b, m_w_up, m_w_down, m_ln3_g, m_ln3_b, v_ln_in_g, v_ln_in_b, v_w_in, v_s5_lam_re, v_s5_lam_im, v_s5_log_dt, v_s5_b_re, v_s5_b_im, v_s5_c_re, v_s5_c_im, v_s5_d, v_w_glu, v_q_norm_g, v_w_uq, v_kv_norm_g, v_w_ukv, v_w_oa, v_w_o, v_ln1_g, v_ln1_b, v_w_xq, v_w_xk, v_w_xv, v_w_xo, v_ln2_g, v_ln2_b, v_w_up, v_w_down, v_ln3_g, v_ln3_b):
    given = dict(x=x, mem=mem, positions=positions, ln_in_g=ln_in_g, ln_in_b=ln_in_b, w_in=w_in, s5_lam_re=s5_lam_re, s5_lam_im=s5_lam_im, s5_log_dt=s5_log_dt, s5_b_re=s5_b_re, s5_b_im=s5_b_im, s5_c_re=s5_c_re, s5_c_im=s5_c_im, s5_d=s5_d, w_glu=w_glu, q_norm_g=q_norm_g, w_uq=w_uq, kv_norm_g=kv_norm_g, w_ukv=w_ukv, w_oa=w_oa, w_o=w_o, ln1_g=ln1_g, ln1_b=ln1_b, w_xq=w_xq, w_xk=w_xk, w_xv=w_xv, w_xo=w_xo, ln2_g=ln2_g, ln2_b=ln2_b, w_up=w_up, w_down=w_down, ln3_g=ln3_g, ln3_b=ln3_b, loss_target=loss_target, m_ln_in_g=m_ln_in_g, m_ln_in_b=m_ln_in_b, m_w_in=m_w_in, m_s5_lam_re=m_s5_lam_re, m_s5_lam_im=m_s5_lam_im, m_s5_log_dt=m_s5_log_dt, m_s5_b_re=m_s5_b_re, m_s5_b_im=m_s5_b_im, m_s5_c_re=m_s5_c_re, m_s5_c_im=m_s5_c_im, m_s5_d=m_s5_d, m_w_glu=m_w_glu, m_q_norm_g=m_q_norm_g, m_w_uq=m_w_uq, m_kv_norm_g=m_kv_norm_g, m_w_ukv=m_w_ukv, m_w_oa=m_w_oa, m_w_o=m_w_o, m_ln1_g=m_ln1_g, m_ln1_b=m_ln1_b, m_w_xq=m_w_xq, m_w_xk=m_w_xk, m_w_xv=m_w_xv, m_w_xo=m_w_xo, m_ln2_g=m_ln2_g, m_ln2_b=m_ln2_b, m_w_up=m_w_up, m_w_down=m_w_down, m_ln3_g=m_ln3_g, m_ln3_b=m_ln3_b, v_ln_in_g=v_ln_in_g, v_ln_in_b=v_ln_in_b, v_w_in=v_w_in, v_s5_lam_re=v_s5_lam_re, v_s5_lam_im=v_s5_lam_im, v_s5_log_dt=v_s5_log_dt, v_s5_b_re=v_s5_b_re, v_s5_b_im=v_s5_b_im, v_s5_c_re=v_s5_c_re, v_s5_c_im=v_s5_c_im, v_s5_d=v_s5_d, v_w_glu=v_w_glu, v_q_norm_g=v_q_norm_g, v_w_uq=v_w_uq, v_kv_norm_g=v_kv_norm_g, v_w_ukv=v_w_ukv, v_w_oa=v_w_oa, v_w_o=v_w_o, v_ln1_g=v_ln1_g, v_ln1_b=v_ln1_b, v_w_xq=v_w_xq, v_w_xk=v_w_xk, v_w_xv=v_w_xv, v_w_xo=v_w_xo, v_ln2_g=v_ln2_g, v_ln2_b=v_ln2_b, v_w_up=v_w_up, v_w_down=v_w_down, v_ln3_g=v_ln3_g, v_ln3_b=v_ln3_b)
    weights = {n: given[n] for n in TWIN_WEIGHTS}
    shared = {n: given[n] for n in SHARED_INPUTS}
    per_example = {n: given[n] for n in ['x', 'mem', 'positions']}
    grad_fn = _jax.value_and_grad(_loss, argnums=(0, 1))

    def one_microbatch(ex, loss_target):
        ex = dict(ex)
        diff = ex.pop(TWIN_DIFF_INPUT)
        return grad_fn(weights, diff, {**shared, **ex}, loss_target)

    if N_MICROBATCH == 1:
        loss, (grad_w, grad_x) = one_microbatch(per_example, given["loss_target"])
    else:
        def body(carry, xs):
            loss_sum, grad_sum = carry
            l_k, (gw_k, gx_k) = one_microbatch(xs[0], xs[1])
            with _jax.named_scope("update"):
                return (loss_sum + l_k, _jax.tree.map(_jnp.add, grad_sum, gw_k)), gx_k

        init = (_jnp.zeros((), _jnp.float32), _jax.tree.map(_jnp.zeros_like, weights))
        (loss, grad_w), grad_x = _jax.lax.scan(body, init, (per_example, given["loss_target"]))
    with _jax.named_scope("update"):
        delta_w, new_m, new_v = {}, {}, {}
        for n in TWIN_WEIGHTS:
            delta_w[n], new_m[n], new_v[n] = _adamw(weights[n], grad_w[n], given["m_" + n], given["v_" + n])
    return (loss, grad_x, *[grad_w[n] for n in TWIN_WEIGHTS], *[delta_w[n] for n in TWIN_WEIGHTS],
            *[new_m[n] for n in TWIN_WEIGHTS], *[new_v[n] for n in TWIN_WEIGHTS])
```

```python
import functools
import math

import jax
import jax.numpy as jnp
from jax import lax
from jax.experimental import pallas as pl
from jax.experimental.pallas import tpu as pltpu

F32 = jnp.float32
BF16 = jnp.bfloat16
MESH = pl.DeviceIdType.MESH

D_MODEL = 1024
S5_W = 256
S5_G = 16
S5_H = 16
S5_P = 64
S5_N = S5_G * S5_P
S5_MAX_RE = -1e-4
HEADS = 8
NOPE = 64
ROPE = 32
QK = NOPE + ROPE
VD = 64
HP = 128
XH = 4
XD = 256
LN_EPS = 1e-5
RMS_EPS = 1e-6
NEG_INF = -1e30
DN_ALPHA = 2.0 ** 0.25
ROPE_THETA = 10000.0
ADAM_LR, ADAM_B1, ADAM_B2, ADAM_EPS, ADAM_WD, ADAM_STEP = 0.001, 0.9, 0.999, 1e-08, 0.01, 10

Z_GS, Z_GA, Z_U, Z_CQ, Z_CKV, Z_KR, Z_W = 0, 1024, 2048, 2304, 2560, 2816, 2944
SCAN_CHUNK = 256

BIG = ("w_in", "w_glu", "w_uq", "w_ukv", "w_oa", "w_up", "w_o", "w_xq", "w_xk", "w_xv", "w_xo", "w_down")
COL_SHARDED = ("w_in", "w_glu", "w_uq", "w_ukv", "w_oa", "w_up")
SMALL = ("ln_in_g", "ln_in_b", "s5_lam_re", "s5_lam_im", "s5_log_dt", "s5_b_re", "s5_b_im", "s5_c_re", "s5_c_im",
         "s5_d", "q_norm_g", "kv_norm_g", "ln1_g", "ln1_b", "ln2_g", "ln2_b", "ln3_g", "ln3_b")
WEIGHTS = ("ln_in_g", "ln_in_b", "w_in", "s5_lam_re", "s5_lam_im", "s5_log_dt", "s5_b_re", "s5_b_im", "s5_c_re",
           "s5_c_im", "s5_d", "w_glu", "q_norm_g", "w_uq", "kv_norm_g", "w_ukv", "w_oa", "w_o", "ln1_g", "ln1_b",
           "w_xq", "w_xk", "w_xv", "w_xo", "ln2_g", "ln2_b", "w_up", "w_down", "ln3_g", "ln3_b")
PACK_ALIGN = 2 * 16 * 1024


def _row_tile(n, cap=512):
    best = n
    for t in range(8, min(n, cap) + 1, 8):
        if n % t == 0:
            best = t
    return best


def _pick(n, cap):
    best = None
    for t in range(128, min(n, cap) + 1, 128):
        if n % t == 0:
            best = t
    return best if best is not None and (best >= 512 or best == n) else n


def _mm(a, b, name, mode="nn", *, a_off=0, b_off=0, m=None, n=None, act=None, epi=None, extras=(),
        out_dtype=F32, tm=512, tn=1024, tk=1024):
    if mode == "nn":
        M, (K, N) = a.shape[0], b.shape
    elif mode == "nt":
        M, (N, K) = a.shape[0], b.shape
    else:
        K, M, N = a.shape[0], m, n
    if mode == "tn":
        tm, tn, tk = _pick(M, tm), _pick(N, tn), min(tk, K)
    else:
        tm, tn, tk = min(tm, M), _pick(N, tn), _pick(K, tk)
    assert M % tm == 0 and N % tn == 0 and K % tk == 0, (name, M, N, K, tm, tn, tk)
    nk = K // tk
    if mode == "tn":
        assert a_off % tm == 0 and b_off % tn == 0
        ao, bo = a_off // tm, b_off // tn
        a_spec = pl.BlockSpec((tk, tm), lambda i, j, k: (k, i + ao))
        b_spec = pl.BlockSpec((tk, tn), lambda i, j, k: (k, j + bo))
        dims = (((0,), (0,)), ((), ()))
    else:
        assert a_off % tk == 0
        ao = a_off // tk
        a_spec = pl.BlockSpec((tm, tk), lambda i, j, k: (i, k + ao))
        if mode == "nn":
            b_spec = pl.BlockSpec((tk, tn), lambda i, j, k: (k, j))
            dims = (((1,), (0,)), ((), ()))
        else:
            b_spec = pl.BlockSpec((tn, tk), lambda i, j, k: (j, k))
            dims = (((1,), (1,)), ((), ()))
    e_spec = pl.BlockSpec((tm, tn), lambda i, j, k: (i, j))
    n_extra = len(extras)

    def body(*refs):
        a_ref, b_ref = refs[0], refs[1]
        e_refs = refs[2:2 + n_extra]
        o_ref = refs[2 + n_extra]
        av = a_ref[...]
        if act is not None:
            av = act(av.astype(F32))
        p = lax.dot_general(av.astype(BF16), b_ref[...].astype(BF16), dims, preferred_element_type=F32)

        def finish(r):
            if epi is not None:
                r = epi(r, *[e[...] for e in e_refs])
            o_ref[...] = r.astype(o_ref.dtype)

        if nk == 1:
            finish(p)
        else:
            acc = refs[3 + n_extra]
            k = pl.program_id(2)

            @pl.when(k == 0)
            def _():
                acc[...] = p

            @pl.when(k > 0)
            def _():
                acc[...] += p

            @pl.when(k == nk - 1)
            def _():
                finish(acc[...])

    return pl.pallas_call(
        body, name=name, grid=(M // tm, N // tn, nk),
        in_specs=[a_spec, b_spec] + [e_spec] * n_extra, out_specs=e_spec,
        out_shape=jax.ShapeDtypeStruct((M, N), out_dtype),
        scratch_shapes=[pltpu.VMEM((tm, tn), F32)] if nk > 1 else [],
        compiler_params=pltpu.CompilerParams(dimension_semantics=("parallel", "parallel", "arbitrary")),
    )(a, b, *extras)


def _rowwise(fn, name, rows, pars, outs, accs=(), tm=256):
    T = rows[0][0].shape[0]
    tm = min(tm, T)
    assert T % tm == 0
    n_in, n_out = len(rows) + len(pars), len(outs)
    in_specs = []
    for arr, off, w in rows:
        assert off % w == 0 and arr.shape[0] == T, name
        in_specs.append(pl.BlockSpec((tm, w), functools.partial(lambda i, o: (i, o), o=off // w)))
    for p in pars:
        in_specs.append(pl.BlockSpec(p.shape, lambda i: (0, 0)))
    out_specs = [pl.BlockSpec((tm, w), lambda i: (i, 0)) for w, _ in outs]
    out_specs += [pl.BlockSpec(s, lambda i: (0, 0)) for s in accs]
    out_shape = [jax.ShapeDtypeStruct((T, w), dt) for w, dt in outs] + [jax.ShapeDtypeStruct(s, F32) for s in accs]

    def body(*refs):
        res = fn(*[r[...] for r in refs[:n_in]])
        o_refs = refs[n_in:]
        for r, v in zip(o_refs[:n_out], res[:n_out]):
            r[...] = v.astype(r.dtype)
        if accs:
            i = pl.program_id(0)

            @pl.when(i == 0)
            def _():
                for r in o_refs[n_out:]:
                    r[...] = jnp.zeros_like(r)

            for r, v in zip(o_refs[n_out:], res[n_out:]):
                r[...] += v

    res = pl.pallas_call(
        body, name=name, grid=(T // tm,), in_specs=in_specs, out_specs=out_specs, out_shape=out_shape,
        compiler_params=pltpu.CompilerParams(dimension_semantics=("arbitrary",)),
    )(*[r[0] for r in rows], *pars)
    return res


def _whole(arr):
    return (arr, 0, arr.shape[1])


def _ln(x, g, b):
    mu = jnp.mean(x, axis=-1, keepdims=True)
    xc = x - mu
    var = jnp.mean(xc * xc, axis=-1, keepdims=True)
    return xc * lax.rsqrt(var + LN_EPS) * g + b


def _ln_res(h, r, g, b):
    return _ln(DN_ALPHA * h + r, g, b)


def _rms(x, g):
    return x * lax.rsqrt(jnp.mean(x * x, axis=-1, keepdims=True) + RMS_EPS) * g


def _gelu_skip(yc, u, d):
    y = yc + d * u
    return 0.5 * y * (1.0 + lax.erf(y * (1.0 / math.sqrt(2.0))))


def _gate_mix(gs, ga, y2a, y2b, aout):
    return jax.nn.sigmoid(gs) * (y2a * jax.nn.sigmoid(y2b)) + jax.nn.sigmoid(ga) * aout


def _relu2(a):
    r = jnp.maximum(a, 0.0)
    return r * r


def _tile8(t):
    return jnp.concatenate([t] * HEADS, axis=1)


def _rope(x, cos, s1, s2):
    w = x.shape[1]
    return x * cos + pltpu.roll(x, ROPE // 2, 1) * s1 + pltpu.roll(x, w - ROPE // 2, 1) * s2


def _rope_t(dy, cos, s1, s2):
    w = dy.shape[1]
    return dy * cos + pltpu.roll(dy * s1, w - ROPE // 2, 1) + pltpu.roll(dy * s2, ROPE // 2, 1)


def _rope_lanes(shape):
    lane = lax.broadcasted_iota(jnp.int32, shape, 1) % HP
    return (lane >= NOPE) & (lane < QK)


def _causal_mask(i, j, bq, bk):
    qpos = i * bq + lax.broadcasted_iota(jnp.int32, (bq, bk), 0)
    kpos = j * bk + lax.broadcasted_iota(jnp.int32, (bq, bk), 1)
    return kpos <= qpos


def _flash_fwd(q, k, v, name, bq=512):
    T = q.shape[0]
    bq = min(bq, T)
    nq = T // bq
    scale = QK ** -0.5

    def body(q_ref, k_ref, v_ref, o_ref, lse_ref, m_sc, l_sc, acc_sc):
        i, j = pl.program_id(1), pl.program_id(2)

        @pl.when(j == 0)
        def _():
            m_sc[...] = jnp.full_like(m_sc, NEG_INF)
            l_sc[...] = jnp.zeros_like(l_sc)
            acc_sc[...] = jnp.zeros_like(acc_sc)

        @pl.when(j <= i)
        def _():
            s = lax.dot_general(q_ref[...], k_ref[...], (((1,), (1,)), ((), ())), preferred_element_type=F32) * scale
            s = jnp.where(_causal_mask(i, j, bq, bq), s, NEG_INF)
            m_prev = m_sc[...]
            m_new = jnp.maximum(m_prev, jnp.max(s, axis=1, keepdims=True))
            alpha = jnp.exp(m_prev - m_new)
            p = jnp.exp(s - m_new)
            l_sc[...] = alpha * l_sc[...] + jnp.sum(p, axis=1, keepdims=True)
            acc_sc[...] = alpha * acc_sc[...] + jnp.dot(p.astype(BF16), v_ref[...], preferred_element_type=F32)
            m_sc[...] = m_new

        @pl.when(j == nq - 1)
        def _():
            o_ref[...] = acc_sc[...] / l_sc[...]
            lse_ref[0] = m_sc[...] + jnp.log(l_sc[...])

    qs = pl.BlockSpec((bq, HP), lambda h, i, j: (i, h))
    ks = pl.BlockSpec((bq, HP), lambda h, i, j: (jnp.minimum(j, i), h))
    return pl.pallas_call(
        body, name=name, grid=(HEADS, nq, nq), in_specs=[qs, ks, ks],
        out_specs=[qs, pl.BlockSpec((1, bq, 1), lambda h, i, j: (h, i, 0))],
        out_shape=[jax.ShapeDtypeStruct((T, HEADS * HP), F32), jax.ShapeDtypeStruct((HEADS, T, 1), F32)],
        scratch_shapes=[pltpu.VMEM((bq, 1), F32), pltpu.VMEM((bq, 1), F32), pltpu.VMEM((bq, HP), F32)],
        compiler_params=pltpu.CompilerParams(dimension_semantics=("parallel", "parallel", "arbitrary")),
    )(q, k, v)


def _flash_p_ds(q_ref, k_ref, v_ref, o_ref, do_ref, lse_ref, i, j, bq, scale):
    s = lax.dot_general(q_ref[...], k_ref[...], (((1,), (1,)), ((), ())), preferred_element_type=F32) * scale
    p = jnp.where(_causal_mask(i, j, bq, bq), jnp.exp(s - lse_ref[0]), 0.0)
    do = do_ref[...]
    delta = jnp.sum(do * o_ref[...], axis=1, keepdims=True)
    dp = lax.dot_general(do.astype(BF16), v_ref[...], (((1,), (1,)), ((), ())), preferred_element_type=F32)
    ds = p * (dp - delta) * scale
    return p, ds, do


def _flash_bwd_dkv(q, k, v, o, do, lse, name, bq=512):
    T = q.shape[0]
    bq = min(bq, T)
    nq = T // bq
    scale = QK ** -0.5

    def body(q_ref, k_ref, v_ref, o_ref, do_ref, lse_ref, dk_ref, dv_ref):
        j, i = pl.program_id(1), pl.program_id(2)

        @pl.when(i == 0)
        def _():
            dk_ref[...] = jnp.zeros_like(dk_ref)
            dv_ref[...] = jnp.zeros_like(dv_ref)

        @pl.when(i >= j)
        def _():
            p, ds, do = _flash_p_ds(q_ref, k_ref, v_ref, o_ref, do_ref, lse_ref, i, j, bq, scale)
            tdims = (((0,), (0,)), ((), ()))
            dv_ref[...] += lax.dot_general(p.astype(BF16), do.astype(BF16), tdims, preferred_element_type=F32)
            dk_ref[...] += lax.dot_general(ds.astype(BF16), q_ref[...], tdims, preferred_element_type=F32)

    qs = pl.BlockSpec((bq, HP), lambda h, j, i: (jnp.maximum(i, j), h))
    ks = pl.BlockSpec((bq, HP), lambda h, j, i: (j, h))
    ls = pl.BlockSpec((1, bq, 1), lambda h, j, i: (h, jnp.maximum(i, j), 0))
    return pl.pallas_call(
        body, name=name, grid=(HEADS, nq, nq), in_specs=[qs, ks, ks, qs, qs, ls], out_specs=[ks, ks],
        out_shape=[jax.ShapeDtypeStruct((T, HEADS * HP), F32)] * 2,
        compiler_params=pltpu.CompilerParams(dimension_semantics=("parallel", "parallel", "arbitrary")),
    )(q, k, v, o, do, lse)


def _flash_bwd_dq(q, k, v, o, do, lse, name, bq=512):
    T = q.shape[0]
    bq = min(bq, T)
    nq = T // bq
    scale = QK ** -0.5

    def body(q_ref, k_ref, v_ref, o_ref, do_ref, lse_ref, dq_ref):
        i, j = pl.program_id(1), pl.program_id(2)

        @pl.when(j == 0)
        def _():
            dq_ref[...] = jnp.zeros_like(dq_ref)

        @pl.when(j <= i)
        def _():
            _, ds, _ = _flash_p_ds(q_ref, k_ref, v_ref, o_ref, do_ref, lse_ref, i, j, bq, scale)
            dq_ref[...] += jnp.dot(ds.astype(BF16), k_ref[...], preferred_element_type=F32)

    qs = pl.BlockSpec((bq, HP), lambda h, i, j: (i, h))
    ks = pl.BlockSpec((bq, HP), lambda h, i, j: (jnp.minimum(j, i), h))
    ls = pl.BlockSpec((1, bq, 1), lambda h, i, j: (h, i, 0))
    return pl.pallas_call(
        body, name=name, grid=(HEADS, nq, nq), in_specs=[qs, ks, ks, qs, qs, ls], out_specs=qs,
        out_shape=jax.ShapeDtypeStruct((T, HEADS * HP), F32),
        compiler_params=pltpu.CompilerParams(dimension_semantics=("parallel", "parallel", "arbitrary")),
    )(q, k, v, o, do, lse)


def _xattn_heads(q, k, v):
    scale = XD ** -0.5
    ps = []
    for h in range(XH):
        sl = slice(h * XD, (h + 1) * XD)
        s = lax.dot_general(q[:, sl].astype(BF16), k[:, sl].astype(BF16), (((1,), (1,)), ((), ())),
                            preferred_element_type=F32) * scale
        e = jnp.exp(s - jnp.max(s, axis=1, keepdims=True))
        ps.append(e / jnp.sum(e, axis=1, keepdims=True))
    return ps


def _xattn_fwd_fn(q, k, v):
    ps = _xattn_heads(q, k, v)
    o = [jnp.dot(p.astype(BF16), v[:, h * XD:(h + 1) * XD].astype(BF16), preferred_element_type=F32)
         for h, p in enumerate(ps)]
    return (jnp.concatenate(o, axis=1),)


def _xattn_bwd_fn(q, do, k, v):
    scale = XD ** -0.5
    ps = _xattn_heads(q, k, v)
    dqs, dks, dvs = [], [], []
    tdims = (((0,), (0,)), ((), ()))
    for h, p in enumerate(ps):
        sl = slice(h * XD, (h + 1) * XD)
        doh = do[:, sl].astype(BF16)
        dvs.append(lax.dot_general(p.astype(BF16), doh, tdims, preferred_element_type=F32))
        dp = lax.dot_general(doh, v[:, sl].astype(BF16), (((1,), (1,)), ((), ())), preferred_element_type=F32)
        ds = (p * (dp - jnp.sum(dp * p, axis=1, keepdims=True)) * scale).astype(BF16)
        dqs.append(jnp.dot(ds, k[:, sl].astype(BF16), preferred_element_type=F32))
        dks.append(lax.dot_general(ds, q[:, sl].astype(BF16), tdims, preferred_element_type=F32))
    return jnp.concatenate(dqs, axis=1), jnp.concatenate(dks, axis=1), jnp.concatenate(dvs, axis=1)


def _cmul_add(xr, xi, ar, ai, sr, si):
    return xr + ar * sr - ai * si, xi + ar * si + ai * sr


def _chunk_scan(xr, xi, astep_ref, reverse):
    tc = xr.shape[0]
    rows = lax.broadcasted_iota(jnp.int32, xr.shape, 0)
    for k in range(int(math.log2(tc))):
        s = 1 << k
        ar, ai = astep_ref[k:k + 1, :S5_N], astep_ref[k:k + 1, S5_N:]
        if reverse:
            keep, sh = rows < tc - s, tc - s
        else:
            keep, sh = rows >= s, s
        sr = jnp.where(keep, pltpu.roll(xr, sh, 0), 0.0)
        si = jnp.where(keep, pltpu.roll(xi, sh, 0), 0.0)
        xr, xi = _cmul_add(xr, xi, ar, ai, sr, si)
    return xr, xi


def _s5_scan_fwd(bu, apow, astep, name):
    T = bu.shape[0]
    tc = apow.shape[0]
    assert T % tc == 0

    def body(x_ref, apow_ref, astep_ref, h_ref, carry):
        @pl.when(pl.program_id(0) == 0)
        def _():
            carry[...] = jnp.zeros_like(carry)

        xr, xi = _chunk_scan(x_ref[:, :S5_N], x_ref[:, S5_N:], astep_ref, False)
        hr, hi = _cmul_add(xr, xi, apow_ref[:, :S5_N], apow_ref[:, S5_N:], carry[0:1, :S5_N], carry[0:1, S5_N:])
        h_ref[:, :S5_N] = hr
        h_ref[:, S5_N:] = hi
        carry[0:1, :S5_N] = hr[tc - 1:tc]
        carry[0:1, S5_N:] = hi[tc - 1:tc]

    blk = pl.BlockSpec((tc, 2 * S5_N), lambda i: (i, 0))
    return pl.pallas_call(
        body, name=name, grid=(T // tc,),
        in_specs=[blk, pl.BlockSpec(apow.shape, lambda i: (0, 0)), pl.BlockSpec(astep.shape, lambda i: (0, 0))],
        out_specs=blk, out_shape=jax.ShapeDtypeStruct((T, 2 * S5_N), F32),
        scratch_shapes=[pltpu.VMEM((8, 2 * S5_N), F32)],
        compiler_params=pltpu.CompilerParams(dimension_semantics=("arbitrary",)),
    )(bu, apow, astep)


def _s5_scan_bwd(dh, h, apow_rev, astep_conj, name):
    T = dh.shape[0]
    tc = apow_rev.shape[0]
    nc = T // tc

    def body(x_ref, h_ref, hp_ref, apow_ref, astep_ref, lam_ref, da_ref, carry):
        g = pl.program_id(0)

        @pl.when(g == 0)
        def _():
            carry[...] = jnp.zeros_like(carry)
            da_ref[...] = jnp.zeros_like(da_ref)

        xr, xi = _chunk_scan(x_ref[:, :S5_N], x_ref[:, S5_N:], astep_ref, True)
        lr, li = _cmul_add(xr, xi, apow_ref[:, :S5_N], apow_ref[:, S5_N:], carry[0:1, :S5_N], carry[0:1, S5_N:])
        lam_ref[:, :S5_N] = lr
        lam_ref[:, S5_N:] = li
        carry[0:1, :S5_N] = lr[0:1]
        carry[0:1, S5_N:] = li[0:1]
        rows = lax.broadcasted_iota(jnp.int32, (tc, S5_N), 0)
        first = jnp.where(g == nc - 1, 0.0, 1.0)
        pr = jnp.where(rows == 0, hp_ref[7:8, :S5_N] * first, pltpu.roll(h_ref[:, :S5_N], 1, 0))
        pi = jnp.where(rows == 0, hp_ref[7:8, S5_N:] * first, pltpu.roll(h_ref[:, S5_N:], 1, 0))
        da_ref[0:1, :S5_N] += jnp.sum(lr * pr + li * pi, axis=0, keepdims=True)
        da_ref[0:1, S5_N:] += jnp.sum(li * pr - lr * pi, axis=0, keepdims=True)

    blk = pl.BlockSpec((tc, 2 * S5_N), lambda g: (nc - 1 - g, 0))
    prev = pl.BlockSpec((8, 2 * S5_N), lambda g: (jnp.maximum((nc - 1 - g) * (tc // 8) - 1, 0), 0))
    return pl.pallas_call(
        body, name=name, grid=(nc,),
        in_specs=[blk, blk, prev, pl.BlockSpec(apow_rev.shape, lambda g: (0, 0)),
                  pl.BlockSpec(astep_conj.shape, lambda g: (0, 0))],
        out_specs=[blk, pl.BlockSpec((1, 2 * S5_N), lambda g: (0, 0))],
        out_shape=[jax.ShapeDtypeStruct((T, 2 * S5_N), F32), jax.ShapeDtypeStruct((1, 2 * S5_N), F32)],
        scratch_shapes=[pltpu.VMEM((8, 2 * S5_N), F32)],
        compiler_params=pltpu.CompilerParams(dimension_semantics=("arbitrary",)),
    )(dh, h, h, apow_rev, astep_conj)


def _s5_discretize(lam_re, lam_im, log_dt, b_re, b_im):
    lr = jnp.minimum(lam_re, S5_MAX_RE)
    li = lam_im
    dt = jnp.exp(log_dt)[:, None]
    mag = jnp.exp(lr * dt)
    ang = li * dt
    ab_re = mag * jnp.cos(ang)
    ab_im = mag * jnp.sin(ang)
    den = lr * lr + li * li
    nr = ab_re - 1.0
    f_re = ((nr * lr + ab_im * li) / den)[..., None]
    f_im = ((ab_im * lr - nr * li) / den)[..., None]
    return ab_re, ab_im, f_re * b_re - f_im * b_im, f_re * b_im + f_im * b_re


def _cpow_table(ar, ai, n):
    def op(e1, e2):
        return e1[0] * e2[0] - e1[1] * e2[1], e1[0] * e2[1] + e1[1] * e2[0]
    pr, pi = lax.associative_scan(op, (jnp.broadcast_to(ar, (n, S5_N)), jnp.broadcast_to(ai, (n, S5_N))), axis=0)
    return pr, pi


def _adamw_fn(w, g, m, v):
    m = ADAM_B1 * m + (1.0 - ADAM_B1) * g
    v = ADAM_B2 * v + (1.0 - ADAM_B2) * (g * g)
    m_hat = m / (1.0 - ADAM_B1 ** ADAM_STEP)
    v_hat = v / (1.0 - ADAM_B2 ** ADAM_STEP)
    delta = -ADAM_LR * (m_hat / (jnp.sqrt(v_hat) + ADAM_EPS) + ADAM_WD * w)
    return delta, m, v


def _adamw(w, g, m, v, name):
    wd = w.shape[1]
    return _rowwise(_adamw_fn, name, [_whole(w), _whole(g), _whole(m), _whole(v)], [], [(wd, F32)] * 3,
                    tm=_row_tile(w.shape[0]))


def _me():
    return lax.axis_index("x"), lax.axis_index("y"), lax.axis_index("c")


def _chip_of(j):
    return j // 2, j % 2


_HBM = pl.BlockSpec(memory_space=pl.ANY)


def _gather_weights(shard):
    _, R, W = shard.shape

    def body(s_ref, out_ref, send_sems, recv_sems, local_sem):
        x, y, c = _me()
        mine = 2 * x + y
        sib = (x, y, 1 - c)
        chips = [(1 - x, y), (x, 1 - y), (1 - x, 1 - y)]

        def rcopy(kk, src, chip_idx, half, to):
            return pltpu.make_async_remote_copy(src_ref=src, dst_ref=out_ref.at[chip_idx, half],
                                                send_sem=send_sems.at[kk], recv_sem=recv_sems.at[kk],
                                                device_id=to, device_id_type=MESH)

        own = [pltpu.make_async_copy(s_ref.at[hf], out_ref.at[mine, hf], local_sem.at[hf]) for hf in range(2)]
        for cp in own:
            cp.start()
        first = [rcopy(j, s_ref.at[c], mine, c, (*chip, c)) for j, chip in enumerate(chips)]
        for cp in first:
            cp.start()
        passed = []
        for j, (px, py) in enumerate(chips):
            src_chip = 2 * px + py
            rcopy(j, s_ref.at[c], src_chip, c, (x, y, c)).wait_recv()
            fwd = rcopy(3 + j, out_ref.at[src_chip, c], src_chip, c, sib)
            fwd.start()
            passed.append(fwd)
        for j, (px, py) in enumerate(chips):
            rcopy(3 + j, s_ref.at[c], 2 * px + py, 1 - c, (x, y, c)).wait_recv()
        for cp in first + passed:
            cp.wait_send()
        for cp in own:
            cp.wait()

    return pl.pallas_call(
        body, name="gather_weights", in_specs=[_HBM], out_specs=_HBM,
        out_shape=jax.ShapeDtypeStruct((4, 2, R, W), shard.dtype),
        scratch_shapes=[pltpu.SemaphoreType.DMA((6,)), pltpu.SemaphoreType.DMA((6,)), pltpu.SemaphoreType.DMA((2,))],
    )(shard)


def _pair_exchange(g):
    _, _, R, W = g.shape

    def body(g_ref, out_ref, send_sem, recv_sem):
        x, y, c = _me()
        cp = pltpu.make_async_remote_copy(src_ref=g_ref.at[:, 1 - c], dst_ref=out_ref, send_sem=send_sem,
                                          recv_sem=recv_sem, device_id=(x, y, 1 - c), device_id_type=MESH)
        cp.start()
        cp.wait()

    return pl.pallas_call(
        body, name="grad_pair_exchange", in_specs=[_HBM], out_specs=_HBM,
        out_shape=jax.ShapeDtypeStruct((4, R, W), g.dtype),
        scratch_shapes=[pltpu.SemaphoreType.DMA(()), pltpu.SemaphoreType.DMA(())],
    )(g)


def _pair_add(g, recv, cidx):
    _, _, R, W = g.shape
    tr = _row_tile(R)

    def body(c_ref, a_ref, b_ref, o_ref):
        o_ref[...] = a_ref[...] + b_ref[...]

    gs = pltpu.PrefetchScalarGridSpec(
        num_scalar_prefetch=1, grid=(4, R // tr),
        in_specs=[pl.BlockSpec((None, None, tr, W), lambda j, r, c: (j, c[0], r, 0)),
                  pl.BlockSpec((None, tr, W), lambda j, r, c: (j, r, 0))],
        out_specs=pl.BlockSpec((None, tr, W), lambda j, r, c: (j, r, 0)))
    return pl.pallas_call(body, name="grad_pair_add", grid_spec=gs,
                          out_shape=jax.ShapeDtypeStruct((4, R, W), F32))(cidx, g, recv)


def _chip_exchange(p):
    _, R, W = p.shape

    def body(p_ref, out_ref, send_sems, recv_sems, local_sem):
        x, y, c = _me()
        mine = 2 * x + y
        chips = [(1 - x, y), (x, 1 - y), (1 - x, 1 - y)]
        own = pltpu.make_async_copy(p_ref.at[mine], out_ref.at[mine], local_sem)
        own.start()
        cps = []
        for j, (px, py) in enumerate(chips):
            cp = pltpu.make_async_remote_copy(src_ref=p_ref.at[2 * px + py], dst_ref=out_ref.at[mine],
                                              send_sem=send_sems.at[j], recv_sem=recv_sems.at[j],
                                              device_id=(px, py, c), device_id_type=MESH)
            cp.start()
            cps.append(cp)
        for j, (px, py) in enumerate(chips):
            pltpu.make_async_remote_copy(src_ref=p_ref.at[mine], dst_ref=out_ref.at[2 * px + py],
                                         send_sem=send_sems.at[j], recv_sem=recv_sems.at[j],
                                         device_id=(px, py, c), device_id_type=MESH).wait_recv()
        for cp in cps:
            cp.wait_send()
        own.wait()

    return pl.pallas_call(
        body, name="grad_chip_exchange", in_specs=[_HBM], out_specs=_HBM,
        out_shape=jax.ShapeDtypeStruct((4, R, W), p.dtype),
        scratch_shapes=[pltpu.SemaphoreType.DMA((3,)), pltpu.SemaphoreType.DMA((3,)), pltpu.SemaphoreType.DMA(())],
    )(p)


def _chip_sum(q):
    _, R, W = q.shape
    tr = _row_tile(R)

    def body(q_ref, o_ref):
        o_ref[...] = ((q_ref[0] + q_ref[1]) + q_ref[2]) + q_ref[3]

    return pl.pallas_call(body, name="grad_chip_sum", grid=(R // tr,),
                          in_specs=[pl.BlockSpec((4, tr, W), lambda r: (0, r, 0))],
                          out_specs=pl.BlockSpec((tr, W), lambda r: (r, 0)),
                          out_shape=jax.ShapeDtypeStruct((R, W), F32))(q)


def _pair_share(r):
    R, W = r.shape

    def body(r_ref, out_ref, send_sem, recv_sem, local_sem):
        x, y, c = _me()
        own = pltpu.make_async_copy(r_ref, out_ref.at[c], local_sem)
        own.start()
        cp = pltpu.make_async_remote_copy(src_ref=r_ref, dst_ref=out_ref.at[c], send_sem=send_sem, recv_sem=recv_sem,
                                          device_id=(x, y, 1 - c), device_id_type=MESH)
        cp.start()
        pltpu.make_async_remote_copy(src_ref=r_ref, dst_ref=out_ref.at[1 - c], send_sem=send_sem, recv_sem=recv_sem,
                                     device_id=(x, y, 1 - c), device_id_type=MESH).wait_recv()
        cp.wait_send()
        own.wait()

    return pl.pallas_call(
        body, name="grad_pair_share", in_specs=[_HBM], out_specs=_HBM,
        out_shape=jax.ShapeDtypeStruct((2, R, W), r.dtype),
        scratch_shapes=[pltpu.SemaphoreType.DMA(()), pltpu.SemaphoreType.DMA(()), pltpu.SemaphoreType.DMA(())],
    )(r)


def _allreduce_small(vec):
    R, W = vec.shape

    def body(v_ref, out_ref, buf, send_sems, recv_sems):
        x, y, c = _me()
        me = 4 * x + 2 * y + c
        buf[me] = v_ref[...]
        cps = []
        for kk in range(1, 8):
            peer = (x ^ (kk >> 2), y ^ ((kk >> 1) & 1), c ^ (kk & 1))
            cp = pltpu.make_async_remote_copy(src_ref=v_ref, dst_ref=buf.at[me], send_sem=send_sems.at[kk - 1],
                                              recv_sem=recv_sems.at[kk - 1], device_id=peer, device_id_type=MESH)
            cp.start()
            cps.append(cp)
        for kk in range(1, 8):
            peer = (x ^ (kk >> 2), y ^ ((kk >> 1) & 1), c ^ (kk & 1))
            pltpu.make_async_remote_copy(src_ref=v_ref, dst_ref=buf.at[me ^ kk], send_sem=send_sems.at[kk - 1],
                                         recv_sem=recv_sems.at[kk - 1], device_id=peer, device_id_type=MESH).wait_recv()
        for cp in cps:
            cp.wait_send()
        acc = buf[0]
        for d in range(1, 8):
            acc = acc + buf[d]
        out_ref[...] = acc

    vm = pl.BlockSpec(memory_space=pltpu.VMEM)
    return pl.pallas_call(
        body, name="allreduce_small", in_specs=[vm], out_specs=vm, out_shape=jax.ShapeDtypeStruct((R, W), F32),
        scratch_shapes=[pltpu.VMEM((8, R, W), F32), pltpu.SemaphoreType.DMA((7,)), pltpu.SemaphoreType.DMA((7,))],
    )(vec)


def _pack_flat(parts, align, dtype):
    flat = jnp.concatenate([p.reshape(-1).astype(dtype) for p in parts])
    n = flat.shape[0]
    pad = (-n) % align
    return jnp.pad(flat, (0, pad)) if pad else flat


def _unpack_flat(flat, shapes):
    out, off = [], 0
    for s in shapes:
        n = math.prod(s)
        out.append(flat[off:off + n].reshape(s))
        off += n
    return out


def _full_from_shards(name, sh):
    if name in COL_SHARDED:
        return jnp.transpose(sh, (1, 0, 2)).reshape(sh.shape[1], 4 * sh.shape[2])
    return sh.reshape(4 * sh.shape[1], sh.shape[2])


def _shards_from_full(name, full):
    if name in COL_SHARDED:
        r, cc = full.shape[0], full.shape[1] // 4
        return jnp.transpose(full.reshape(r, 4, cc), (1, 0, 2)).reshape(4, r * cc)
    return full.reshape(4, -1)


def _pad_heads(w, width):
    k = w.shape[0]
    return jnp.pad(w.reshape(k, HEADS, width), ((0, 0), (0, 0), (0, HP - width))).reshape(k, HEADS * HP)


def _unpad_heads(w, width):
    k = w.shape[0]
    return w.reshape(k, HEADS, HP)[:, :, :width].reshape(k, HEADS * width)


def _blockdiag(t):
    g, a, b = t.shape
    return jnp.einsum("gab,gk->gakb", t, jnp.eye(g, dtype=t.dtype)).reshape(g * a, g * b)


def _blockdiag_t(m, a, b):
    g = m.shape[0] // a
    return jnp.einsum("gagb->gab", m.reshape(g, a, g, b))


def _local_step(x, mem, positions, target, W, sp):
    T = x.shape[0]
    row = lambda v: v.reshape(1, -1)

    ab_re, ab_im, bb_re, bb_im = _s5_discretize(sp["s5_lam_re"], sp["s5_lam_im"], sp["s5_log_dt"], sp["s5_b_re"], sp["s5_b_im"])
    ar, ai = ab_re.reshape(1, S5_N), ab_im.reshape(1, S5_N)
    tc = min(SCAN_CHUNK, T)
    pr, pi = _cpow_table(ar, ai, tc)
    apow = jnp.concatenate([pr, pi], axis=1)
    apow_rev = jnp.concatenate([pr[::-1], -pi[::-1]], axis=1)
    nst = int(math.log2(tc))
    idx = jnp.array([(1 << k) - 1 for k in range(nst)])
    astep = jnp.concatenate([pr[idx], pi[idx]], axis=1)
    astep_conj = jnp.concatenate([pr[idx], -pi[idx]], axis=1)
    padr = (-nst) % 8
    if padr:
        astep = jnp.pad(astep, ((0, padr), (0, 0)))
        astep_conj = jnp.pad(astep_conj, ((0, padr), (0, 0)))
    b_st = jnp.concatenate([_blockdiag(jnp.swapaxes(bb_re, 1, 2)), _blockdiag(jnp.swapaxes(bb_im, 1, 2))], axis=1)
    c_st = jnp.concatenate([_blockdiag(jnp.swapaxes(sp["s5_c_re"], 1, 2)),
                            -_blockdiag(jnp.swapaxes(sp["s5_c_im"], 1, 2))], axis=0)
    inv = ROPE_THETA ** (-jnp.arange(0, ROPE, 2, dtype=F32) / ROPE)
    ang = positions.astype(F32)[:, None] * inv
    cs, sn = jnp.cos(ang), jnp.sin(ang)
    one, zero = jnp.ones((T, NOPE), F32), jnp.zeros((T, NOPE), F32)
    z16, tail1, tail0 = jnp.zeros((T, ROPE // 2), F32), jnp.ones((T, HP - QK), F32), jnp.zeros((T, HP - QK), F32)
    t_cos = jnp.concatenate([one, cs, cs, tail1], axis=1)
    t_s1 = jnp.concatenate([zero, z16, sn, tail0], axis=1)
    t_s2 = jnp.concatenate([zero, -sn, z16, tail0], axis=1)

    w_in = W["w_in"]
    w_in_re = jnp.concatenate([w_in[:, 800:], w_in[:, :768], w_in[:, 768:800],
                               jnp.zeros((D_MODEL, HP - ROPE), w_in.dtype)], axis=1)
    w_uq_p = _pad_heads(W["w_uq"], QK)
    wkv = W["w_ukv"].reshape(S5_W, HEADS, NOPE + VD)
    w_uk_p = _pad_heads(wkv[:, :, :NOPE].reshape(S5_W, HEADS * NOPE), NOPE)
    w_uv_p = _pad_heads(wkv[:, :, NOPE:].reshape(S5_W, HEADS * VD), VD)
    w_oa_p = jnp.pad(W["w_oa"].reshape(HEADS, VD, D_MODEL), ((0, 0), (0, HP - VD), (0, 0))).reshape(HEADS * HP, D_MODEL)

    g_in, b_in = row(sp["ln_in_g"]), row(sp["ln_in_b"])
    g1, b1, g2, b2, g3, b3 = (row(sp[k]) for k in ("ln1_g", "ln1_b", "ln2_g", "ln2_b", "ln3_g", "ln3_b"))
    gq, gkv, d_skip = row(sp["q_norm_g"]), row(sp["kv_norm_g"]), row(sp["s5_d"])

    (h0,) = _rowwise(lambda a, g, b: (_ln(a, g, b),), "ln_in_fwd", [_whole(x)], [g_in, b_in], [(D_MODEL, F32)])
    z = _mm(h0, w_in_re, "in_proj", tn=Z_W)
    bu = _mm(z, b_st, "s5_bu", a_off=Z_U)
    hs = _s5_scan_fwd(bu, apow, astep, "s5_scan_fwd")
    yc = _mm(hs, c_st, "s5_c")
    (yg,) = _rowwise(lambda a, u, d: (_gelu_skip(a, u, d),), "s5_gelu_fwd", [_whole(yc), (z, Z_U, S5_W)], [d_skip], [(S5_W, BF16)])
    y2 = _mm(yg, W["w_glu"], "glu_proj", tn=2048)
    cqn, ckvn = _rowwise(lambda a, b, ga, gb: (_rms(a, ga), _rms(b, gb)), "mla_norm_fwd",
                         [(z, Z_CQ, S5_W), (z, Z_CKV, S5_W)], [gq, gkv], [(S5_W, BF16)] * 2)
    q_raw = _mm(cqn, w_uq_p, "mla_uq")
    k_raw = _mm(ckvn, w_uk_p, "mla_uk")
    v = _mm(ckvn, w_uv_p, "mla_uv", out_dtype=BF16)

    def rope_fwd(qr, kr_raw, krz, tc_, t1, t2):
        q = _rope(qr, _tile8(tc_), _tile8(t1), _tile8(t2))
        kr = _rope(pltpu.roll(krz, NOPE, 1), tc_, t1, t2)
        return q, kr_raw + _tile8(kr)

    q, k = _rowwise(rope_fwd, "mla_rope_fwd", [_whole(q_raw), _whole(k_raw), (z, Z_KR, HP), _whole(t_cos), _whole(t_s1), _whole(t_s2)],
                    [], [(HEADS * HP, BF16)] * 2)
    o, lse = _flash_fwd(q, k, v, "mla_attn_fwd")
    a_out = _mm(o, w_oa_p, "mla_oa")
    (mixin,) = _rowwise(lambda *a: (_gate_mix(*a),), "gate_mix_fwd",
                        [(z, Z_GS, D_MODEL), (z, Z_GA, D_MODEL), (y2, 0, D_MODEL), (y2, D_MODEL, D_MODEL), _whole(a_out)],
                        [], [(D_MODEL, BF16)])
    mix = _mm(mixin, W["w_o"], "mix_o")
    (h1,) = _rowwise(lambda h, r, g, b: (_ln_res(h, r, g, b),), "ln1_fwd", [_whole(h0), _whole(mix)], [g1, b1], [(D_MODEL, F32)])
    xq = _mm(h1, W["w_xq"], "x_q")
    xk = _mm(mem, W["w_xk"], "x_k")
    xv = _mm(mem, W["w_xv"], "x_v")
    (xo,) = _rowwise(_xattn_fwd_fn, "xattn_fwd", [_whole(xq)], [xk, xv], [(D_MODEL, BF16)])
    xa = _mm(xo, W["w_xo"], "x_o")
    (h2,) = _rowwise(lambda h, r, g, b: (_ln_res(h, r, g, b),), "ln2_fwd", [_whole(h1), _whole(xa)], [g2, b2], [(D_MODEL, F32)])
    a_up = _mm(h2, W["w_up"], "mlp_up")
    ff = _mm(a_up, W["w_down"], "mlp_down", act=_relu2)

    def loss_fn(h, r, tgt, g, b):
        def f(r_, g_, b_):
            e = _ln_res(h, r_, g_, b_) - tgt
            return 0.5 * jnp.sum(jnp.mean(e * e, axis=-1))
        lv, (dr_, dg_, db_) = jax.value_and_grad(f, argnums=(0, 1, 2))(r, g, b)
        return dr_, dg_, db_, jnp.broadcast_to(lv, (1, 128))

    dr3, dg3, db3, lossv = _rowwise(loss_fn, "ln3_loss_bwd", [_whole(h2), _whole(ff), _whole(target)], [g3, b3],
                                    [(D_MODEL, F32)], [(1, D_MODEL), (1, D_MODEL), (1, 128)])
    gW, gs = {}, {"ln3_g": dg3, "ln3_b": db3}

    da = _mm(dr3, W["w_down"], "mlp_down_bwd_a", "nt", epi=lambda acc, a: acc * (2.0 * jnp.maximum(a, 0.0)), extras=(a_up,))
    gW["w_down"] = _mm(a_up, dr3, "mlp_down_bwd_w", "tn", m=4 * D_MODEL, n=D_MODEL, act=_relu2)
    gW["w_up"] = _mm(h2, da, "mlp_up_bwd_w", "tn", m=D_MODEL, n=4 * D_MODEL)
    dh2 = _mm(da, W["w_up"], "mlp_up_bwd_a", "nt", epi=lambda acc, e: acc + DN_ALPHA * e, extras=(dr3,))

    def ln_res_bwd(h, r, dy, g, b):
        _, vjp = jax.vjp(_ln_res, h, r, g, b)
        _, dr_, dg_, db_ = vjp(dy)
        return dr_, dg_, db_

    dr2, gs["ln2_g"], gs["ln2_b"] = _rowwise(ln_res_bwd, "ln2_bwd", [_whole(h1), _whole(xa), _whole(dh2)], [g2, b2],
                                             [(D_MODEL, F32)], [(1, D_MODEL)] * 2)
    dxo = _mm(dr2, W["w_xo"], "x_o_bwd_a", "nt")
    gW["w_xo"] = _mm(xo, dr2, "x_o_bwd_w", "tn", m=D_MODEL, n=D_MODEL)
    dxq, dxk, dxv = _rowwise(_xattn_bwd_fn, "xattn_bwd", [_whole(xq), _whole(dxo)], [xk, xv], [(D_MODEL, F32)],
                             [(xk.shape[0], D_MODEL)] * 2)
    gW["w_xq"] = _mm(h1, dxq, "x_q_bwd_w", "tn", m=D_MODEL, n=D_MODEL)
    gW["w_xk"] = _mm(mem, dxk, "x_k_bwd_w", "tn", m=D_MODEL, n=D_MODEL)
    gW["w_xv"] = _mm(mem, dxv, "x_v_bwd_w", "tn", m=D_MODEL, n=D_MODEL)
    dh1 = _mm(dxq, W["w_xq"], "x_q_bwd_a", "nt", epi=lambda acc, e: acc + DN_ALPHA * e, extras=(dr2,))

    dr1, gs["ln1_g"], gs["ln1_b"] = _rowwise(ln_res_bwd, "ln1_bwd", [_whole(h0), _whole(mix), _whole(dh1)], [g1, b1],
                                             [(D_MODEL, F32)], [(1, D_MODEL)] * 2)
    dmixin = _mm(dr1, W["w_o"], "mix_o_bwd_a", "nt")
    gW["w_o"] = _mm(mixin, dr1, "mix_o_bwd_w", "tn", m=D_MODEL, n=D_MODEL)

    def gate_mix_bwd(gs_, ga_, ya, yb, ao, dy):
        _, vjp = jax.vjp(_gate_mix, gs_, ga_, ya, yb, ao)
        return vjp(dy)

    dgs, dga, dy2a, dy2b, da_out = _rowwise(
        gate_mix_bwd, "gate_mix_bwd",
        [(z, Z_GS, D_MODEL), (z, Z_GA, D_MODEL), (y2, 0, D_MODEL), (y2, D_MODEL, D_MODEL), _whole(a_out), _whole(dmixin)],
        [], [(D_MODEL, F32)] * 5)

    do = _mm(da_out, w_oa_p, "mla_oa_bwd_a", "nt")
    g_oa_p = _mm(o, da_out, "mla_oa_bwd_w", "tn", m=HEADS * HP, n=D_MODEL)
    gW["w_oa"] = g_oa_p.reshape(HEADS, HP, D_MODEL)[:, :VD].reshape(HEADS * VD, D_MODEL)
    dk, dv = _flash_bwd_dkv(q, k, v, o, do, lse, "mla_attn_bwd_kv")
    dq = _flash_bwd_dq(q, k, v, o, do, lse, "mla_attn_bwd_q")

    def rope_bwd(dq_, dk_, tc_, t1, t2):
        dqr = _rope_t(dq_, _tile8(tc_), _tile8(t1), _tile8(t2))
        dkr = dk_[:, 0:HP]
        for hh in range(1, HEADS):
            dkr = dkr + dk_[:, hh * HP:(hh + 1) * HP]
        dkr = _rope_t(jnp.where(_rope_lanes(dkr.shape), dkr, 0.0), tc_, t1, t2)
        dkr = pltpu.roll(dkr, NOPE, 1)
        lane = lax.broadcasted_iota(jnp.int32, dkr.shape, 1)
        return dqr, jnp.where(lane < ROPE, dkr, 0.0)

    dq_raw, dkr = _rowwise(rope_bwd, "mla_rope_bwd", [_whole(dq), _whole(dk), _whole(t_cos), _whole(t_s1), _whole(t_s2)], [],
                           [(HEADS * HP, F32), (HP, F32)])
    dcqn = _mm(dq_raw, w_uq_p, "mla_uq_bwd_a", "nt")
    gW["w_uq"] = _unpad_heads(_mm(cqn, dq_raw, "mla_uq_bwd_w", "tn", m=S5_W, n=HEADS * HP), QK)
    dckvn_k = _mm(dk, w_uk_p, "mla_uk_bwd_a", "nt")
    dckvn = _mm(dv, w_uv_p, "mla_uv_bwd_a", "nt", epi=lambda acc, e: acc + e, extras=(dckvn_k,))
    g_uk = _unpad_heads(_mm(ckvn, dk, "mla_uk_bwd_w", "tn", m=S5_W, n=HEADS * HP), NOPE).reshape(S5_W, HEADS, NOPE)
    g_uv = _unpad_heads(_mm(ckvn, dv, "mla_uv_bwd_w", "tn", m=S5_W, n=HEADS * HP), VD).reshape(S5_W, HEADS, VD)
    gW["w_ukv"] = jnp.concatenate([g_uk, g_uv], axis=2).reshape(S5_W, HEADS * (NOPE + VD))

    def norm_bwd(a, b, da_, db_, ga, gb):
        _, vja = jax.vjp(_rms, a, ga)
        _, vjb = jax.vjp(_rms, b, gb)
        dxa, dga_ = vja(da_)
        dxb, dgb_ = vjb(db_)
        return dxa, dxb, dga_, dgb_

    dcq, dckv, gs["q_norm_g"], gs["kv_norm_g"] = _rowwise(
        norm_bwd, "mla_norm_bwd", [(z, Z_CQ, S5_W), (z, Z_CKV, S5_W), _whole(dcqn), _whole(dckvn)], [gq, gkv],
        [(S5_W, F32)] * 2, [(1, S5_W)] * 2)

    dy2 = jnp.concatenate([dy2a, dy2b], axis=1)
    dyg = _mm(dy2, W["w_glu"], "glu_bwd_a", "nt")
    gW["w_glu"] = _mm(yg, dy2, "glu_bwd_w", "tn", m=S5_W, n=2 * D_MODEL)

    def gelu_bwd(a, u, dy, d):
        _, vjp = jax.vjp(_gelu_skip, a, u, d)
        return vjp(dy)

    dyc, du_skip, gs["s5_d"] = _rowwise(gelu_bwd, "s5_gelu_bwd", [_whole(yc), (z, Z_U, S5_W), _whole(dyg)], [d_skip],
                                        [(S5_W, F32)] * 2, [(1, S5_W)])
    g_cst = _mm(hs, dyc, "s5_c_bwd_w", "tn", m=2 * S5_N, n=S5_W)
    dhs = _mm(dyc, c_st, "s5_c_bwd_a", "nt")
    lam, d_ab = _s5_scan_bwd(dhs, hs, apow_rev, astep_conj, "s5_scan_bwd")
    g_bst = _mm(z, lam, "s5_bu_bwd_w", "tn", a_off=Z_U, m=S5_W, n=2 * S5_N)
    du = _mm(lam, b_st, "s5_bu_bwd_a", "nt", epi=lambda acc, e: acc + e, extras=(du_skip,))
    gs["s5_ab"] = d_ab
    gs["s5_bb_re"] = jnp.swapaxes(_blockdiag_t(g_bst[:, :S5_N], S5_H, S5_P), 1, 2)
    gs["s5_bb_im"] = jnp.swapaxes(_blockdiag_t(g_bst[:, S5_N:], S5_H, S5_P), 1, 2)
    gs["s5_c_re"] = jnp.swapaxes(_blockdiag_t(g_cst[:S5_N], S5_P, S5_H), 1, 2)
    gs["s5_c_im"] = -jnp.swapaxes(_blockdiag_t(g_cst[S5_N:], S5_P, S5_H), 1, 2)

    dz = jnp.concatenate([dgs, dga, du, dcq, dckv, dkr], axis=1)
    g_in_re = _mm(h0, dz, "in_proj_bwd_w", "tn", m=D_MODEL, n=Z_W, tn=Z_W, tk=512)
    gW["w_in"] = jnp.concatenate([g_in_re[:, Z_U:Z_KR], g_in_re[:, Z_KR:Z_KR + ROPE], g_in_re[:, :Z_U]], axis=1)
    dh0 = _mm(dz, w_in_re, "in_proj_bwd_a", "nt", tk=Z_W, epi=lambda acc, e: acc + DN_ALPHA * e, extras=(dr1,))

    def ln_bwd(a, dy, g, b):
        _, vjp = jax.vjp(_ln, a, g, b)
        return vjp(dy)

    dx, gs["ln_in_g"], gs["ln_in_b"] = _rowwise(ln_bwd, "ln_in_bwd", [_whole(x), _whole(dh0)], [g_in, b_in],
                                                [(D_MODEL, F32)], [(1, D_MODEL)] * 2)
    return lossv[:, :1], dx, gW, gs


_RAW_SMALL = (("loss", (1, 1)), ("ln_in_g", (1, D_MODEL)), ("ln_in_b", (1, D_MODEL)), ("ln1_g", (1, D_MODEL)),
              ("ln1_b", (1, D_MODEL)), ("ln2_g", (1, D_MODEL)), ("ln2_b", (1, D_MODEL)), ("ln3_g", (1, D_MODEL)),
              ("ln3_b", (1, D_MODEL)), ("q_norm_g", (1, S5_W)), ("kv_norm_g", (1, S5_W)), ("s5_d", (1, S5_W)),
              ("s5_ab", (1, 2 * S5_N)), ("s5_bb_re", (S5_G, S5_P, S5_H)), ("s5_bb_im", (S5_G, S5_P, S5_H)),
              ("s5_c_re", (S5_G, S5_H, S5_P)), ("s5_c_im", (S5_G, S5_H, S5_P)))


def kernel(x, mem, positions, ln_in_g, ln_in_b, w_in, s5_lam_re, s5_lam_im, s5_log_dt, s5_b_re, s5_b_im, s5_c_re, s5_c_im, s5_d, w_glu, q_norm_g, w_uq, kv_norm_g, w_ukv, w_oa, w_o, ln1_g, ln1_b, w_xq, w_xk, w_xv, w_xo, ln2_g, ln2_b, w_up, w_down, ln3_g, ln3_b, loss_target, m_ln_in_g, m_ln_in_b, m_w_in, m_s5_lam_re, m_s5_lam_im, m_s5_log_dt, m_s5_b_re, m_s5_b_im, m_s5_c_re, m_s5_c_im, m_s5_d, m_w_glu, m_q_norm_g, m_w_uq, m_kv_norm_g, m_w_ukv, m_w_oa, m_w_o, m_ln1_g, m_ln1_b, m_w_xq, m_w_xk, m_w_xv, m_w_xo, m_ln2_g, m_ln2_b, m_w_up, m_w_down, m_ln3_g, m_ln3_b, v_ln_in_g, v_ln_in_b, v_w_in, v_s5_lam_re, v_s5_lam_im, v_s5_log_dt, v_s5_b_re, v_s5_b_im, v_s5_c_re, v_s5_c_im, v_s5_d, v_w_glu, v_q_norm_g, v_w_uq, v_kv_norm_g, v_w_ukv, v_w_oa, v_w_o, v_ln1_g, v_ln1_b, v_w_xq, v_w_xk, v_w_xv, v_w_xo, v_ln2_g, v_ln2_b, v_w_up, v_w_down, v_ln3_g, v_ln3_b):
    a = dict(locals())
    wts = {n: a[n] for n in WEIGHTS}
    ms = {n: a["m_" + n] for n in WEIGHTS}
    vs = {n: a["v_" + n] for n in WEIGHTS}
    cidx = lax.axis_index("c").astype(jnp.int32).reshape(1)

    shard2d = {n: wts[n].reshape(wts[n].shape[-2], wts[n].shape[-1]) for n in BIG}
    flat = _pack_flat([shard2d[n] for n in BIG], PACK_ALIGN, BF16)
    rows_half = flat.shape[0] // (2 * 1024)
    gathered = _gather_weights(flat.reshape(2, rows_half, 1024)).reshape(4, -1)
    W, off = {}, 0
    for n in BIG:
        r, cc = shard2d[n].shape
        W[n] = _full_from_shards(n, gathered[:, off:off + r * cc].reshape(4, r, cc))
        off += r * cc

    sp = {n: wts[n] for n in SMALL}
    sp_local = {n: (sp[n][0] if sp[n].ndim > 1 else sp[n]) for n in SMALL}
    lossv, dx, gW, gs = _local_step(x[0], mem[0], positions[0], loss_target[0], W, sp_local)

    gflat = jnp.concatenate([_shards_from_full(n, gW[n]) for n in BIG], axis=1)
    pad = (-gflat.shape[1]) % PACK_ALIGN
    gflat = jnp.pad(gflat, ((0, 0), (0, pad))).reshape(4, 2, rows_half, 1024)
    recv = _pair_exchange(gflat)
    psum = _pair_add(gflat, recv, cidx)
    slots = _chip_exchange(psum)
    red = _chip_sum(slots)
    gsh_flat = _pair_share(red).reshape(-1)
    g_out, off = {}, 0
    for n in BIG:
        r, cc = shard2d[n].shape
        g_out[n] = gsh_flat[off:off + r * cc].reshape(r, cc)
        off += r * cc

    gs["loss"] = lossv
    raw = _pack_flat([gs[n].reshape(s) for n, s in _RAW_SMALL], 8 * 1024, F32)
    raw = _allreduce_small(raw.reshape(-1, 1024)).reshape(-1)
    rs = dict(zip([n for n, _ in _RAW_SMALL], _unpack_flat(raw, [s for _, s in _RAW_SMALL])))
    loss = rs["loss"].reshape(())
    _, disc_vjp = jax.vjp(_s5_discretize, sp_local["s5_lam_re"], sp_local["s5_lam_im"], sp_local["s5_log_dt"],
                          sp_local["s5_b_re"], sp_local["s5_b_im"])
    d_ab = rs["s5_ab"].reshape(2, S5_G, S5_P)
    g_lre, g_lim, g_ldt, g_bre, g_bim = disc_vjp((d_ab[0], d_ab[1], rs["s5_bb_re"], rs["s5_bb_im"]))
    small_g = {"s5_lam_re": g_lre, "s5_lam_im": g_lim, "s5_log_dt": g_ldt, "s5_b_re": g_bre, "s5_b_im": g_bim,
               "s5_c_re": rs["s5_c_re"], "s5_c_im": rs["s5_c_im"]}
    for n in ("ln_in_g", "ln_in_b", "ln1_g", "ln1_b", "ln2_g", "ln2_b", "ln3_g", "ln3_b", "q_norm_g", "kv_norm_g", "s5_d"):
        small_g[n] = rs[n]

    grads, deltas, new_m, new_v = {}, {}, {}, {}
    for n in BIG:
        d_, m_, v_ = _adamw(shard2d[n], g_out[n], ms[n].reshape(shard2d[n].shape), vs[n].reshape(shard2d[n].shape), "adamw_" + n)
        grads[n] = g_out[n].reshape(wts[n].shape)
        deltas[n], new_m[n], new_v[n] = (t.reshape(wts[n].shape) for t in (d_, m_, v_))
    packs = [_pack_flat([t[n] for n in SMALL], 8 * 1024, F32).reshape(-1, 1024)
             for t in (wts, {n: small_g[n].reshape(wts[n].shape) for n in SMALL}, ms, vs)]
    sd, sm, sv = _adamw(*packs, "adamw_small")
    shapes = [wts[n].shape for n in SMALL]
    for n, d_, m_, v_ in zip(SMALL, _unpack_flat(sd.reshape(-1), shapes), _unpack_flat(sm.reshape(-1), shapes),
                             _unpack_flat(sv.reshape(-1), shapes)):
        grads[n] = small_g[n].reshape(wts[n].shape)
        deltas[n], new_m[n], new_v[n] = d_, m_, v_

    return (loss, dx[None], *[grads[n] for n in WEIGHTS], *[deltas[n] for n in WEIGHTS],
            *[new_m[n] for n in WEIGHTS], *[new_v[n] for n in WEIGHTS])
```

```python
import functools
import math

import jax
import jax.numpy as jnp
from jax import lax
from jax.experimental import pallas as pl
from jax.experimental.pallas import tpu as pltpu

F32 = jnp.float32
BF16 = jnp.bfloat16
MESH = pl.DeviceIdType.MESH

D_MODEL = 1024
S5_W = 256
S5_G = 16
S5_H = 16
S5_P = 64
S5_N = S5_G * S5_P
S5_MAX_RE = -1e-4
HEADS = 8
NOPE = 64
ROPE = 32
QK = NOPE + ROPE
VD = 64
HP = 128
XH = 4
XD = 256
LN_EPS = 1e-5
RMS_EPS = 1e-6
NEG_INF = -1e30
DN_ALPHA = 2.0 ** 0.25
ROPE_THETA = 10000.0
ADAM_LR, ADAM_B1, ADAM_B2, ADAM_EPS, ADAM_WD, ADAM_STEP = 0.001, 0.9, 0.999, 1e-08, 0.01, 10

Z_GS, Z_GA, Z_U, Z_CQ, Z_CKV, Z_KR, Z_W = 0, 1024, 2048, 2304, 2560, 2816, 2944
SCAN_CHUNK = 256

BIG = ("w_in", "w_glu", "w_uq", "w_ukv", "w_oa", "w_up", "w_o", "w_xq", "w_xk", "w_xv", "w_xo", "w_down")
COL_SHARDED = ("w_in", "w_glu", "w_uq", "w_ukv", "w_oa", "w_up")
SMALL = ("ln_in_g", "ln_in_b", "s5_lam_re", "s5_lam_im", "s5_log_dt", "s5_b_re", "s5_b_im", "s5_c_re", "s5_c_im",
         "s5_d", "q_norm_g", "kv_norm_g", "ln1_g", "ln1_b", "ln2_g", "ln2_b", "ln3_g", "ln3_b")
WEIGHTS = ("ln_in_g", "ln_in_b", "w_in", "s5_lam_re", "s5_lam_im", "s5_log_dt", "s5_b_re", "s5_b_im", "s5_c_re",
           "s5_c_im", "s5_d", "w_glu", "q_norm_g", "w_uq", "kv_norm_g", "w_ukv", "w_oa", "w_o", "ln1_g", "ln1_b",
           "w_xq", "w_xk", "w_xv", "w_xo", "ln2_g", "ln2_b", "w_up", "w_down", "ln3_g", "ln3_b")
PACK_ALIGN = 2 * 16 * 1024


def _row_tile(n, cap=512):
    best = n
    for t in range(8, min(n, cap) + 1, 8):
        if n % t == 0:
            best = t
    return best


def _pick(n, cap):
    best = None
    for t in range(128, min(n, cap) + 1, 128):
        if n % t == 0:
            best = t
    return best if best is not None and (best >= 512 or best == n) else n


def _mm(a, b, name, mode="nn", *, a_off=0, b_off=0, m=None, n=None, act=None, epi=None, extras=(),
        out_dtype=F32, tm=512, tn=1024, tk=1024):
    if mode == "nn":
        M, (K, N) = a.shape[0], b.shape
    elif mode == "nt":
        M, (N, K) = a.shape[0], b.shape
    else:
        K, M, N = a.shape[0], m, n
    if mode == "tn":
        tm, tn, tk = _pick(M, tm), _pick(N, tn), min(tk, K)
    else:
        tm, tn, tk = min(tm, M), _pick(N, tn), _pick(K, tk)
    assert M % tm == 0 and N % tn == 0 and K % tk == 0, (name, M, N, K, tm, tn, tk)
    nk = K // tk
    if mode == "tn":
        assert a_off % tm == 0 and b_off % tn == 0
        ao, bo = a_off // tm, b_off // tn
        a_spec = pl.BlockSpec((tk, tm), lambda i, j, k: (k, i + ao))
        b_spec = pl.BlockSpec((tk, tn), lambda i, j, k: (k, j + bo))
        dims = (((0,), (0,)), ((), ()))
    else:
        assert a_off % tk == 0
        ao = a_off // tk
        a_spec = pl.BlockSpec((tm, tk), lambda i, j, k: (i, k + ao))
        if mode == "nn":
            b_spec = pl.BlockSpec((tk, tn), lambda i, j, k: (k, j))
            dims = (((1,), (0,)), ((), ()))
        else:
            b_spec = pl.BlockSpec((tn, tk), lambda i, j, k: (j, k))
            dims = (((1,), (1,)), ((), ()))
    e_spec = pl.BlockSpec((tm, tn), lambda i, j, k: (i, j))
    n_extra = len(extras)

    def body(*refs):
        a_ref, b_ref = refs[0], refs[1]
        e_refs = refs[2:2 + n_extra]
        o_ref = refs[2 + n_extra]
        av = a_ref[...]
        if act is not None:
            av = act(av.astype(F32))
        p = lax.dot_general(av.astype(BF16), b_ref[...].astype(BF16), dims, preferred_element_type=F32)

        def finish(r):
            if epi is not None:
                r = epi(r, *[e[...] for e in e_refs])
            o_ref[...] = r.astype(o_ref.dtype)

        if nk == 1:
            finish(p)
        else:
            acc = refs[3 + n_extra]
            k = pl.program_id(2)

            @pl.when(k == 0)
            def _():
                acc[...] = p

            @pl.when(k > 0)
            def _():
                acc[...] += p

            @pl.when(k == nk - 1)
            def _():
                finish(acc[...])

    return pl.pallas_call(
        body, name=name, grid=(M // tm, N // tn, nk),
        in_specs=[a_spec, b_spec] + [e_spec] * n_extra, out_specs=e_spec,
        out_shape=jax.ShapeDtypeStruct((M, N), out_dtype),
        scratch_shapes=[pltpu.VMEM((tm, tn), F32)] if nk > 1 else [],
        compiler_params=pltpu.CompilerParams(dimension_semantics=("parallel", "parallel", "arbitrary")),
    )(a, b, *extras)


def _rowwise(fn, name, rows, pars, outs, accs=(), tm=256):
    T = rows[0][0].shape[0]
    tm = min(tm, T)
    assert T % tm == 0
    n_in, n_out = len(rows) + len(pars), len(outs)
    in_specs = []
    for arr, off, w in rows:
        assert off % w == 0 and arr.shape[0] == T, name
        in_specs.append(pl.BlockSpec((tm, w), functools.partial(lambda i, o: (i, o), o=off // w)))
    for p in pars:
        in_specs.append(pl.BlockSpec(p.shape, lambda i: (0, 0)))
    out_specs = [pl.BlockSpec((tm, w), lambda i: (i, 0)) for w, _ in outs]
    out_specs += [pl.BlockSpec(s, lambda i: (0, 0)) for s in accs]
    out_shape = [jax.ShapeDtypeStruct((T, w), dt) for w, dt in outs] + [jax.ShapeDtypeStruct(s, F32) for s in accs]

    def body(*refs):
        res = fn(*[r[...] for r in refs[:n_in]])
        o_refs = refs[n_in:]
        for r, v in zip(o_refs[:n_out], res[:n_out]):
            r[...] = v.astype(r.dtype)
        if accs:
            i = pl.program_id(0)

            @pl.when(i == 0)
            def _():
                for r in o_refs[n_out:]:
                    r[...] = jnp.zeros_like(r)

            for r, v in zip(o_refs[n_out:], res[n_out:]):
                r[...] += v

    res = pl.pallas_call(
        body, name=name, grid=(T // tm,), in_specs=in_specs, out_specs=out_specs, out_shape=out_shape,
        compiler_params=pltpu.CompilerParams(dimension_semantics=("arbitrary",)),
    )(*[r[0] for r in rows], *pars)
    return res


def _whole(arr):
    return (arr, 0, arr.shape[1])


def _ln(x, g, b):
    mu = jnp.mean(x, axis=-1, keepdims=True)
    xc = x - mu
    var = jnp.mean(xc * xc, axis=-1, keepdims=True)
    return xc * lax.rsqrt(var + LN_EPS) * g + b


def _ln_res(h, r, g, b):
    return _ln(DN_ALPHA * h + r, g, b)


def _rms(x, g):
    return x * lax.rsqrt(jnp.mean(x * x, axis=-1, keepdims=True) + RMS_EPS) * g


def _gelu_skip(yc, u, d):
    y = yc + d * u
    return 0.5 * y * (1.0 + lax.erf(y * (1.0 / math.sqrt(2.0))))


def _gate_mix(gs, ga, y2a, y2b, aout):
    return jax.nn.sigmoid(gs) * (y2a * jax.nn.sigmoid(y2b)) + jax.nn.sigmoid(ga) * aout


def _relu2(a):
    r = jnp.maximum(a, 0.0)
    return r * r


def _tile8(t):
    return jnp.concatenate([t] * HEADS, axis=1)


def _rope(x, cos, s1, s2):
    w = x.shape[1]
    return x * cos + pltpu.roll(x, ROPE // 2, 1) * s1 + pltpu.roll(x, w - ROPE // 2, 1) * s2


def _rope_t(dy, cos, s1, s2):
    w = dy.shape[1]
    return dy * cos + pltpu.roll(dy * s1, w - ROPE // 2, 1) + pltpu.roll(dy * s2, ROPE // 2, 1)


def _rope_lanes(shape):
    lane = lax.broadcasted_iota(jnp.int32, shape, 1) % HP
    return (lane >= NOPE) & (lane < QK)


ATT_HB = 2
ATT_SCALE = QK ** -0.5
LOG2E = 1.4426950408889634
LN2 = 0.6931471805599453
_NT = (((1,), (1,)), ((), ()))
_TN = (((0,), (0,)), ((), ()))


def _flash_fwd(q, k, v, name, bq=512):
    T = q.shape[0]
    bq = min(bq, T)
    nq = T // bq
    hb = ATT_HB

    def body(q_ref, k_ref, v_ref, o_ref, lse_ref, m_sc, l_sc, acc_sc):
        i, j = pl.program_id(1), pl.program_id(2)

        @pl.when(j == 0)
        def _():
            m_sc[...] = jnp.full_like(m_sc, NEG_INF)
            l_sc[...] = jnp.zeros_like(l_sc)
            acc_sc[...] = jnp.zeros_like(acc_sc)

        def step(masked):
            for hh in range(hb):
                sl = slice(hh * HP, (hh + 1) * HP)
                s = lax.dot_general(q_ref[:, sl], k_ref[:, sl], _NT, preferred_element_type=F32) * (ATT_SCALE * LOG2E)
                if masked:
                    row = lax.broadcasted_iota(jnp.int32, (bq, bq), 0)
                    col = lax.broadcasted_iota(jnp.int32, (bq, bq), 1)
                    s = jnp.where(col <= row, s, NEG_INF)
                m_prev = m_sc[hh]
                m_new = jnp.maximum(m_prev, jnp.max(s, axis=1, keepdims=True))
                alpha = jnp.exp2(m_prev - m_new)
                p = jnp.exp2(s - jnp.concatenate([m_new] * (bq // HP), axis=1))
                l_sc[hh] = alpha * l_sc[hh] + jnp.sum(p, axis=1, keepdims=True)
                acc_sc[hh] = alpha * acc_sc[hh] + jnp.dot(p.astype(BF16), v_ref[:, sl], preferred_element_type=F32)
                m_sc[hh] = m_new

        @pl.when(j < i)
        def _():
            step(False)

        @pl.when(j == i)
        def _():
            step(True)

        @pl.when(j == nq - 1)
        def _():
            for hh in range(hb):
                o_ref[:, hh * HP:(hh + 1) * HP] = acc_sc[hh] / l_sc[hh]
                lse_ref[hh] = (m_sc[hh][:, :1] + jnp.log2(l_sc[hh][:, :1])) * LN2

    qs = pl.BlockSpec((bq, hb * HP), lambda h, i, j: (i, h))
    ks = pl.BlockSpec((bq, hb * HP), lambda h, i, j: (jnp.minimum(j, i), h))
    return pl.pallas_call(
        body, name=name, grid=(HEADS // hb, nq, nq), in_specs=[qs, ks, ks],
        out_specs=[qs, pl.BlockSpec((hb, bq, 1), lambda h, i, j: (h, i, 0))],
        out_shape=[jax.ShapeDtypeStruct((T, HEADS * HP), F32), jax.ShapeDtypeStruct((HEADS, T, 1), F32)],
        scratch_shapes=[pltpu.VMEM((hb, bq, HP), F32)] * 3,
        compiler_params=pltpu.CompilerParams(dimension_semantics=("parallel", "parallel", "arbitrary")),
    )(q, k, v)


def _attn_delta_fn(do, o):
    prod = do * o
    cols = [jnp.sum(prod[:, h * HP:(h + 1) * HP], axis=1, keepdims=True) for h in range(HEADS)]
    return jnp.concatenate(cols, axis=1), do


def _flash_bwd(q, k, v, do, lse_row, delta_row, name, bq=512):
    T = q.shape[0]
    bq = min(bq, T)
    nq = T // bq
    hb = ATT_HB

    def body(q_ref, k_ref, v_ref, do_ref, lse_ref, dl_ref, dq_ref, dk_ref, dv_ref):
        j, i = pl.program_id(1), pl.program_id(2)

        @pl.when((i == 0) & (j == 0))
        def _():
            dq_ref[...] = jnp.zeros_like(dq_ref)

        @pl.when(i == 0)
        def _():
            dk_ref[...] = jnp.zeros_like(dk_ref)
            dv_ref[...] = jnp.zeros_like(dv_ref)

        def step(masked):
            rows = pl.ds(pl.multiple_of(i * bq, bq), bq)
            for hh in range(hb):
                sl = slice(hh * HP, (hh + 1) * HP)
                qh, kh, doh = q_ref[:, sl], k_ref[:, sl], do_ref[:, sl]
                st = lax.dot_general(kh, qh, _NT, preferred_element_type=F32) * (ATT_SCALE * LOG2E)
                pt = jnp.exp2(st - lse_ref[hh] * LOG2E)
                if masked:
                    krow = lax.broadcasted_iota(jnp.int32, (bq, bq), 0)
                    qcol = lax.broadcasted_iota(jnp.int32, (bq, bq), 1)
                    pt = jnp.where(krow <= qcol, pt, 0.0)
                dv_ref[:, sl] += jnp.dot(pt.astype(BF16), doh, preferred_element_type=F32)
                dpt = lax.dot_general(v_ref[:, sl], doh, _NT, preferred_element_type=F32)
                dst = (pt * (dpt - dl_ref[hh])).astype(BF16)
                dk_ref[:, sl] += jnp.dot(dst, qh, preferred_element_type=F32)
                dq_ref[rows, sl] += lax.dot_general(dst, kh, _TN, preferred_element_type=F32)

        @pl.when(i > j)
        def _():
            step(False)

        @pl.when(i == j)
        def _():
            step(True)

        @pl.when(i == nq - 1)
        def _():
            dk_ref[...] *= ATT_SCALE

        @pl.when((i == nq - 1) & (j == nq - 1))
        def _():
            dq_ref[...] *= ATT_SCALE

    qs = pl.BlockSpec((bq, hb * HP), lambda h, j, i: (jnp.maximum(i, j), h))
    ks = pl.BlockSpec((bq, hb * HP), lambda h, j, i: (j, h))
    rs = pl.BlockSpec((hb, 1, bq), lambda h, j, i: (h, 0, jnp.maximum(i, j)))
    full = pl.BlockSpec((T, hb * HP), lambda h, j, i: (0, h))
    return pl.pallas_call(
        body, name=name, grid=(HEADS // hb, nq, nq), in_specs=[qs, ks, ks, qs, rs, rs], out_specs=[full, ks, ks],
        out_shape=[jax.ShapeDtypeStruct((T, HEADS * HP), F32)] * 3,
        compiler_params=pltpu.CompilerParams(dimension_semantics=("parallel", "arbitrary", "arbitrary")),
    )(q, k, v, do, lse_row, delta_row)


def _xattn_heads(q, k, v):
    scale = XD ** -0.5
    ps = []
    for h in range(XH):
        sl = slice(h * XD, (h + 1) * XD)
        s = lax.dot_general(q[:, sl].astype(BF16), k[:, sl].astype(BF16), (((1,), (1,)), ((), ())),
                            preferred_element_type=F32) * scale
        e = jnp.exp(s - jnp.max(s, axis=1, keepdims=True))
        ps.append(e / jnp.sum(e, axis=1, keepdims=True))
    return ps


def _xattn_fwd_fn(q, k, v):
    ps = _xattn_heads(q, k, v)
    o = [jnp.dot(p.astype(BF16), v[:, h * XD:(h + 1) * XD].astype(BF16), preferred_element_type=F32)
         for h, p in enumerate(ps)]
    return (jnp.concatenate(o, axis=1),)


def _xattn_bwd_fn(q, do, k, v):
    scale = XD ** -0.5
    ps = _xattn_heads(q, k, v)
    dqs, dks, dvs = [], [], []
    tdims = (((0,), (0,)), ((), ()))
    for h, p in enumerate(ps):
        sl = slice(h * XD, (h + 1) * XD)
        doh = do[:, sl].astype(BF16)
        dvs.append(lax.dot_general(p.astype(BF16), doh, tdims, preferred_element_type=F32))
        dp = lax.dot_general(doh, v[:, sl].astype(BF16), (((1,), (1,)), ((), ())), preferred_element_type=F32)
        ds = (p * (dp - jnp.sum(dp * p, axis=1, keepdims=True)) * scale).astype(BF16)
        dqs.append(jnp.dot(ds, k[:, sl].astype(BF16), preferred_element_type=F32))
        dks.append(lax.dot_general(ds, q[:, sl].astype(BF16), tdims, preferred_element_type=F32))
    return jnp.concatenate(dqs, axis=1), jnp.concatenate(dks, axis=1), jnp.concatenate(dvs, axis=1)


def _cmul_add(xr, xi, ar, ai, sr, si):
    return xr + ar * sr - ai * si, xi + ar * si + ai * sr


def _chunk_scan(xr, xi, astep_ref, reverse):
    tc = xr.shape[0]
    rows = lax.broadcasted_iota(jnp.int32, xr.shape, 0)
    for k in range(int(math.log2(tc))):
        s = 1 << k
        ar, ai = astep_ref[k:k + 1, :S5_N], astep_ref[k:k + 1, S5_N:]
        if reverse:
            keep, sh = rows < tc - s, tc - s
        else:
            keep, sh = rows >= s, s
        sr = jnp.where(keep, pltpu.roll(xr, sh, 0), 0.0)
        si = jnp.where(keep, pltpu.roll(xi, sh, 0), 0.0)
        xr, xi = _cmul_add(xr, xi, ar, ai, sr, si)
    return xr, xi


def _s5_scan_fwd(bu, apow, astep, name):
    T = bu.shape[0]
    tc = apow.shape[0]
    assert T % tc == 0

    def body(x_ref, apow_ref, astep_ref, h_ref, carry):
        @pl.when(pl.program_id(0) == 0)
        def _():
            carry[...] = jnp.zeros_like(carry)

        xr, xi = _chunk_scan(x_ref[:, :S5_N], x_ref[:, S5_N:], astep_ref, False)
        hr, hi = _cmul_add(xr, xi, apow_ref[:, :S5_N], apow_ref[:, S5_N:], carry[0:1, :S5_N], carry[0:1, S5_N:])
        h_ref[:, :S5_N] = hr
        h_ref[:, S5_N:] = hi
        carry[0:1, :S5_N] = hr[tc - 1:tc]
        carry[0:1, S5_N:] = hi[tc - 1:tc]

    blk = pl.BlockSpec((tc, 2 * S5_N), lambda i: (i, 0))
    return pl.pallas_call(
        body, name=name, grid=(T // tc,),
        in_specs=[blk, pl.BlockSpec(apow.shape, lambda i: (0, 0)), pl.BlockSpec(astep.shape, lambda i: (0, 0))],
        out_specs=blk, out_shape=jax.ShapeDtypeStruct((T, 2 * S5_N), F32),
        scratch_shapes=[pltpu.VMEM((8, 2 * S5_N), F32)],
        compiler_params=pltpu.CompilerParams(dimension_semantics=("arbitrary",)),
    )(bu, apow, astep)


def _s5_scan_bwd(dh, h, apow_rev, astep_conj, name):
    T = dh.shape[0]
    tc = apow_rev.shape[0]
    nc = T // tc

    def body(x_ref, h_ref, hp_ref, apow_ref, astep_ref, lam_ref, da_ref, carry):
        g = pl.program_id(0)

        @pl.when(g == 0)
        def _():
            carry[...] = jnp.zeros_like(carry)
            da_ref[...] = jnp.zeros_like(da_ref)

        xr, xi = _chunk_scan(x_ref[:, :S5_N], x_ref[:, S5_N:], astep_ref, True)
        lr, li = _cmul_add(xr, xi, apow_ref[:, :S5_N], apow_ref[:, S5_N:], carry[0:1, :S5_N], carry[0:1, S5_N:])
        lam_ref[:, :S5_N] = lr
        lam_ref[:, S5_N:] = li
        carry[0:1, :S5_N] = lr[0:1]
        carry[0:1, S5_N:] = li[0:1]
        rows = lax.broadcasted_iota(jnp.int32, (tc, S5_N), 0)
        first = jnp.where(g == nc - 1, 0.0, 1.0)
        pr = jnp.where(rows == 0, hp_ref[7:8, :S5_N] * first, pltpu.roll(h_ref[:, :S5_N], 1, 0))
        pi = jnp.where(rows == 0, hp_ref[7:8, S5_N:] * first, pltpu.roll(h_ref[:, S5_N:], 1, 0))
        da_ref[0:1, :S5_N] += jnp.sum(lr * pr + li * pi, axis=0, keepdims=True)
        da_ref[0:1, S5_N:] += jnp.sum(li * pr - lr * pi, axis=0, keepdims=True)

    blk = pl.BlockSpec((tc, 2 * S5_N), lambda g: (nc - 1 - g, 0))
    prev = pl.BlockSpec((8, 2 * S5_N), lambda g: (jnp.maximum((nc - 1 - g) * (tc // 8) - 1, 0), 0))
    return pl.pallas_call(
        body, name=name, grid=(nc,),
        in_specs=[blk, blk, prev, pl.BlockSpec(apow_rev.shape, lambda g: (0, 0)),
                  pl.BlockSpec(astep_conj.shape, lambda g: (0, 0))],
        out_specs=[blk, pl.BlockSpec((1, 2 * S5_N), lambda g: (0, 0))],
        out_shape=[jax.ShapeDtypeStruct((T, 2 * S5_N), F32), jax.ShapeDtypeStruct((1, 2 * S5_N), F32)],
        scratch_shapes=[pltpu.VMEM((8, 2 * S5_N), F32)],
        compiler_params=pltpu.CompilerParams(dimension_semantics=("arbitrary",)),
    )(dh, h, h, apow_rev, astep_conj)


def _s5_discretize(lam_re, lam_im, log_dt, b_re, b_im):
    lr = jnp.minimum(lam_re, S5_MAX_RE)
    li = lam_im
    dt = jnp.exp(log_dt)[:, None]
    mag = jnp.exp(lr * dt)
    ang = li * dt
    ab_re = mag * jnp.cos(ang)
    ab_im = mag * jnp.sin(ang)
    den = lr * lr + li * li
    nr = ab_re - 1.0
    f_re = ((nr * lr + ab_im * li) / den)[..., None]
    f_im = ((ab_im * lr - nr * li) / den)[..., None]
    return ab_re, ab_im, f_re * b_re - f_im * b_im, f_re * b_im + f_im * b_re


def _cpow_table(ar, ai, n):
    def op(e1, e2):
        return e1[0] * e2[0] - e1[1] * e2[1], e1[0] * e2[1] + e1[1] * e2[0]
    pr, pi = lax.associative_scan(op, (jnp.broadcast_to(ar, (n, S5_N)), jnp.broadcast_to(ai, (n, S5_N))), axis=0)
    return pr, pi


def _adamw_fn(w, g, m, v):
    m = ADAM_B1 * m + (1.0 - ADAM_B1) * g
    v = ADAM_B2 * v + (1.0 - ADAM_B2) * (g * g)
    m_hat = m / (1.0 - ADAM_B1 ** ADAM_STEP)
    v_hat = v / (1.0 - ADAM_B2 ** ADAM_STEP)
    delta = -ADAM_LR * (m_hat / (jnp.sqrt(v_hat) + ADAM_EPS) + ADAM_WD * w)
    return delta, m, v


def _adamw(w, g, m, v, name):
    wd = w.shape[1]
    return _rowwise(_adamw_fn, name, [_whole(w), _whole(g), _whole(m), _whole(v)], [], [(wd, F32)] * 3,
                    tm=_row_tile(w.shape[0]))


def _me():
    return lax.axis_index("x"), lax.axis_index("y"), lax.axis_index("c")


def _chip_of(j):
    return j // 2, j % 2


_HBM = pl.BlockSpec(memory_space=pl.ANY)


def _gather_weights(shard):
    _, R, W = shard.shape

    def body(s_ref, out_ref, send_sems, recv_sems, local_sem):
        x, y, c = _me()
        mine = 2 * x + y
        sib = (x, y, 1 - c)
        chips = [(1 - x, y), (x, 1 - y), (1 - x, 1 - y)]

        def rcopy(kk, src, chip_idx, half, to):
            return pltpu.make_async_remote_copy(src_ref=src, dst_ref=out_ref.at[chip_idx, half],
                                                send_sem=send_sems.at[kk], recv_sem=recv_sems.at[kk],
                                                device_id=to, device_id_type=MESH)

        own = [pltpu.make_async_copy(s_ref.at[hf], out_ref.at[mine, hf], local_sem.at[hf]) for hf in range(2)]
        for cp in own:
            cp.start()
        first = [rcopy(j, s_ref.at[c], mine, c, (*chip, c)) for j, chip in enumerate(chips)]
        for cp in first:
            cp.start()
        passed = []
        for j, (px, py) in enumerate(chips):
            src_chip = 2 * px + py
            rcopy(j, s_ref.at[c], src_chip, c, (x, y, c)).wait_recv()
            fwd = rcopy(3 + j, out_ref.at[src_chip, c], src_chip, c, sib)
            fwd.start()
            passed.append(fwd)
        for j, (px, py) in enumerate(chips):
            rcopy(3 + j, s_ref.at[c], 2 * px + py, 1 - c, (x, y, c)).wait_recv()
        for cp in first + passed:
            cp.wait_send()
        for cp in own:
            cp.wait()

    return pl.pallas_call(
        body, name="gather_weights", in_specs=[_HBM], out_specs=_HBM,
        out_shape=jax.ShapeDtypeStruct((4, 2, R, W), shard.dtype),
        scratch_shapes=[pltpu.SemaphoreType.DMA((6,)), pltpu.SemaphoreType.DMA((6,)), pltpu.SemaphoreType.DMA((2,))],
    )(shard)


def _pair_exchange(g):
    _, _, R, W = g.shape

    def body(g_ref, out_ref, send_sem, recv_sem):
        x, y, c = _me()
        cp = pltpu.make_async_remote_copy(src_ref=g_ref.at[:, 1 - c], dst_ref=out_ref, send_sem=send_sem,
                                          recv_sem=recv_sem, device_id=(x, y, 1 - c), device_id_type=MESH)
        cp.start()
        cp.wait()

    return pl.pallas_call(
        body, name="grad_pair_exchange", in_specs=[_HBM], out_specs=_HBM,
        out_shape=jax.ShapeDtypeStruct((4, R, W), g.dtype),
        scratch_shapes=[pltpu.SemaphoreType.DMA(()), pltpu.SemaphoreType.DMA(())],
    )(g)


def _pair_add(g, recv, cidx):
    _, _, R, W = g.shape
    tr = _row_tile(R)

    def body(c_ref, a_ref, b_ref, o_ref):
        o_ref[...] = a_ref[...] + b_ref[...]

    gs = pltpu.PrefetchScalarGridSpec(
        num_scalar_prefetch=1, grid=(4, R // tr),
        in_specs=[pl.BlockSpec((None, None, tr, W), lambda j, r, c: (j, c[0], r, 0)),
                  pl.BlockSpec((None, tr, W), lambda j, r, c: (j, r, 0))],
        out_specs=pl.BlockSpec((None, tr, W), lambda j, r, c: (j, r, 0)))
    return pl.pallas_call(body, name="grad_pair_add", grid_spec=gs,
                          out_shape=jax.ShapeDtypeStruct((4, R, W), F32))(cidx, g, recv)


def _chip_exchange(p):
    _, R, W = p.shape

    def body(p_ref, out_ref, send_sems, recv_sems, local_sem):
        x, y, c = _me()
        mine = 2 * x + y
        chips = [(1 - x, y), (x, 1 - y), (1 - x, 1 - y)]
        own = pltpu.make_async_copy(p_ref.at[mine], out_ref.at[mine], local_sem)
        own.start()
        cps = []
        for j, (px, py) in enumerate(chips):
            cp = pltpu.make_async_remote_copy(src_ref=p_ref.at[2 * px + py], dst_ref=out_ref.at[mine],
                                              send_sem=send_sems.at[j], recv_sem=recv_sems.at[j],
                                              device_id=(px, py, c), device_id_type=MESH)
            cp.start()
            cps.append(cp)
        for j, (px, py) in enumerate(chips):
            pltpu.make_async_remote_copy(src_ref=p_ref.at[mine], dst_ref=out_ref.at[2 * px + py],
                                         send_sem=send_sems.at[j], recv_sem=recv_sems.at[j],
                                         device_id=(px, py, c), device_id_type=MESH).wait_recv()
        for cp in cps:
            cp.wait_send()
        own.wait()

    return pl.pallas_call(
        body, name="grad_chip_exchange", in_specs=[_HBM], out_specs=_HBM,
        out_shape=jax.ShapeDtypeStruct((4, R, W), p.dtype),
        scratch_shapes=[pltpu.SemaphoreType.DMA((3,)), pltpu.SemaphoreType.DMA((3,)), pltpu.SemaphoreType.DMA(())],
    )(p)


def _chip_sum(q):
    _, R, W = q.shape
    tr = _row_tile(R)

    def body(q_ref, o_ref):
        o_ref[...] = ((q_ref[0] + q_ref[1]) + q_ref[2]) + q_ref[3]

    return pl.pallas_call(body, name="grad_chip_sum", grid=(R // tr,),
                          in_specs=[pl.BlockSpec((4, tr, W), lambda r: (0, r, 0))],
                          out_specs=pl.BlockSpec((tr, W), lambda r: (r, 0)),
                          out_shape=jax.ShapeDtypeStruct((R, W), F32))(q)


def _pair_share(r):
    R, W = r.shape

    def body(r_ref, out_ref, send_sem, recv_sem, local_sem):
        x, y, c = _me()
        own = pltpu.make_async_copy(r_ref, out_ref.at[c], local_sem)
        own.start()
        cp = pltpu.make_async_remote_copy(src_ref=r_ref, dst_ref=out_ref.at[c], send_sem=send_sem, recv_sem=recv_sem,
                                          device_id=(x, y, 1 - c), device_id_type=MESH)
        cp.start()
        pltpu.make_async_remote_copy(src_ref=r_ref, dst_ref=out_ref.at[1 - c], send_sem=send_sem, recv_sem=recv_sem,
                                     device_id=(x, y, 1 - c), device_id_type=MESH).wait_recv()
        cp.wait_send()
        own.wait()

    return pl.pallas_call(
        body, name="grad_pair_share", in_specs=[_HBM], out_specs=_HBM,
        out_shape=jax.ShapeDtypeStruct((2, R, W), r.dtype),
        scratch_shapes=[pltpu.SemaphoreType.DMA(()), pltpu.SemaphoreType.DMA(()), pltpu.SemaphoreType.DMA(())],
    )(r)


def _allreduce_small(vec):
    R, W = vec.shape

    def body(v_ref, out_ref, buf, send_sems, recv_sems):
        x, y, c = _me()
        me = 4 * x + 2 * y + c
        buf[me] = v_ref[...]
        cps = []
        for kk in range(1, 8):
            peer = (x ^ (kk >> 2), y ^ ((kk >> 1) & 1), c ^ (kk & 1))
            cp = pltpu.make_async_remote_copy(src_ref=v_ref, dst_ref=buf.at[me], send_sem=send_sems.at[kk - 1],
                                              recv_sem=recv_sems.at[kk - 1], device_id=peer, device_id_type=MESH)
            cp.start()
            cps.append(cp)
        for kk in range(1, 8):
            peer = (x ^ (kk >> 2), y ^ ((kk >> 1) & 1), c ^ (kk & 1))
            pltpu.make_async_remote_copy(src_ref=v_ref, dst_ref=buf.at[me ^ kk], send_sem=send_sems.at[kk - 1],
                                         recv_sem=recv_sems.at[kk - 1], device_id=peer, device_id_type=MESH).wait_recv()
        for cp in cps:
            cp.wait_send()
        acc = buf[0]
        for d in range(1, 8):
            acc = acc + buf[d]
        out_ref[...] = acc

    vm = pl.BlockSpec(memory_space=pltpu.VMEM)
    return pl.pallas_call(
        body, name="allreduce_small", in_specs=[vm], out_specs=vm, out_shape=jax.ShapeDtypeStruct((R, W), F32),
        scratch_shapes=[pltpu.VMEM((8, R, W), F32), pltpu.SemaphoreType.DMA((7,)), pltpu.SemaphoreType.DMA((7,))],
    )(vec)


def _pack_flat(parts, align, dtype):
    flat = jnp.concatenate([p.reshape(-1).astype(dtype) for p in parts])
    n = flat.shape[0]
    pad = (-n) % align
    return jnp.pad(flat, (0, pad)) if pad else flat


def _unpack_flat(flat, shapes):
    out, off = [], 0
    for s in shapes:
        n = math.prod(s)
        out.append(flat[off:off + n].reshape(s))
        off += n
    return out


def _full_from_shards(name, sh):
    if name in COL_SHARDED:
        return jnp.transpose(sh, (1, 0, 2)).reshape(sh.shape[1], 4 * sh.shape[2])
    return sh.reshape(4 * sh.shape[1], sh.shape[2])


def _shards_from_full(name, full):
    if name in COL_SHARDED:
        r, cc = full.shape[0], full.shape[1] // 4
        return jnp.transpose(full.reshape(r, 4, cc), (1, 0, 2)).reshape(4, r * cc)
    return full.reshape(4, -1)


def _pad_heads(w, width):
    k = w.shape[0]
    return jnp.pad(w.reshape(k, HEADS, width), ((0, 0), (0, 0), (0, HP - width))).reshape(k, HEADS * HP)


def _unpad_heads(w, width):
    k = w.shape[0]
    return w.reshape(k, HEADS, HP)[:, :, :width].reshape(k, HEADS * width)


def _blockdiag(t):
    g, a, b = t.shape
    return jnp.einsum("gab,gk->gakb", t, jnp.eye(g, dtype=t.dtype)).reshape(g * a, g * b)


def _blockdiag_t(m, a, b):
    g = m.shape[0] // a
    return jnp.einsum("gagb->gab", m.reshape(g, a, g, b))


def _local_step(x, mem, positions, target, W, sp):
    T = x.shape[0]
    row = lambda v: v.reshape(1, -1)

    ab_re, ab_im, bb_re, bb_im = _s5_discretize(sp["s5_lam_re"], sp["s5_lam_im"], sp["s5_log_dt"], sp["s5_b_re"], sp["s5_b_im"])
    ar, ai = ab_re.reshape(1, S5_N), ab_im.reshape(1, S5_N)
    tc = min(SCAN_CHUNK, T)
    pr, pi = _cpow_table(ar, ai, tc)
    apow = jnp.concatenate([pr, pi], axis=1)
    apow_rev = jnp.concatenate([pr[::-1], -pi[::-1]], axis=1)
    nst = int(math.log2(tc))
    idx = jnp.array([(1 << k) - 1 for k in range(nst)])
    astep = jnp.concatenate([pr[idx], pi[idx]], axis=1)
    astep_conj = jnp.concatenate([pr[idx], -pi[idx]], axis=1)
    padr = (-nst) % 8
    if padr:
        astep = jnp.pad(astep, ((0, padr), (0, 0)))
        astep_conj = jnp.pad(astep_conj, ((0, padr), (0, 0)))
    b_st = jnp.concatenate([_blockdiag(jnp.swapaxes(bb_re, 1, 2)), _blockdiag(jnp.swapaxes(bb_im, 1, 2))], axis=1)
    c_st = jnp.concatenate([_blockdiag(jnp.swapaxes(sp["s5_c_re"], 1, 2)),
                            -_blockdiag(jnp.swapaxes(sp["s5_c_im"], 1, 2))], axis=0)
    inv = ROPE_THETA ** (-jnp.arange(0, ROPE, 2, dtype=F32) / ROPE)
    ang = positions.astype(F32)[:, None] * inv
    cs, sn = jnp.cos(ang), jnp.sin(ang)
    one, zero = jnp.ones((T, NOPE), F32), jnp.zeros((T, NOPE), F32)
    z16, tail1, tail0 = jnp.zeros((T, ROPE // 2), F32), jnp.ones((T, HP - QK), F32), jnp.zeros((T, HP - QK), F32)
    t_cos = jnp.concatenate([one, cs, cs, tail1], axis=1)
    t_s1 = jnp.concatenate([zero, z16, sn, tail0], axis=1)
    t_s2 = jnp.concatenate([zero, -sn, z16, tail0], axis=1)

    w_in = W["w_in"]
    w_in_re = jnp.concatenate([w_in[:, 800:], w_in[:, :768], w_in[:, 768:800],
                               jnp.zeros((D_MODEL, HP - ROPE), w_in.dtype)], axis=1)
    w_uq_p = _pad_heads(W["w_uq"], QK)
    wkv = W["w_ukv"].reshape(S5_W, HEADS, NOPE + VD)
    w_uk_p = _pad_heads(wkv[:, :, :NOPE].reshape(S5_W, HEADS * NOPE), NOPE)
    w_uv_p = _pad_heads(wkv[:, :, NOPE:].reshape(S5_W, HEADS * VD), VD)
    w_oa_p = jnp.pad(W["w_oa"].reshape(HEADS, VD, D_MODEL), ((0, 0), (0, HP - VD), (0, 0))).reshape(HEADS * HP, D_MODEL)

    g_in, b_in = row(sp["ln_in_g"]), row(sp["ln_in_b"])
    g1, b1, g2, b2, g3, b3 = (row(sp[k]) for k in ("ln1_g", "ln1_b", "ln2_g", "ln2_b", "ln3_g", "ln3_b"))
    gq, gkv, d_skip = row(sp["q_norm_g"]), row(sp["kv_norm_g"]), row(sp["s5_d"])

    (h0,) = _rowwise(lambda a, g, b: (_ln(a, g, b),), "ln_in_fwd", [_whole(x)], [g_in, b_in], [(D_MODEL, F32)])
    z = _mm(h0, w_in_re, "in_proj", tn=Z_W)
    bu = _mm(z, b_st, "s5_bu", a_off=Z_U)
    hs = _s5_scan_fwd(bu, apow, astep, "s5_scan_fwd")
    yc = _mm(hs, c_st, "s5_c")
    (yg,) = _rowwise(lambda a, u, d: (_gelu_skip(a, u, d),), "s5_gelu_fwd", [_whole(yc), (z, Z_U, S5_W)], [d_skip], [(S5_W, BF16)])
    y2 = _mm(yg, W["w_glu"], "glu_proj", tn=2048)
    cqn, ckvn = _rowwise(lambda a, b, ga, gb: (_rms(a, ga), _rms(b, gb)), "mla_norm_fwd",
                         [(z, Z_CQ, S5_W), (z, Z_CKV, S5_W)], [gq, gkv], [(S5_W, BF16)] * 2)
    q_raw = _mm(cqn, w_uq_p, "mla_uq")
    k_raw = _mm(ckvn, w_uk_p, "mla_uk")
    v = _mm(ckvn, w_uv_p, "mla_uv", out_dtype=BF16)

    def rope_fwd(qr, kr_raw, krz, tc_, t1, t2):
        q = _rope(qr, _tile8(tc_), _tile8(t1), _tile8(t2))
        kr = _rope(pltpu.roll(krz, NOPE, 1), tc_, t1, t2)
        return q, kr_raw + _tile8(kr)

    q, k = _rowwise(rope_fwd, "mla_rope_fwd", [_whole(q_raw), _whole(k_raw), (z, Z_KR, HP), _whole(t_cos), _whole(t_s1), _whole(t_s2)],
                    [], [(HEADS * HP, BF16)] * 2)
    o, lse = _flash_fwd(q, k, v, "mla_attn_fwd")
    a_out = _mm(o, w_oa_p, "mla_oa")
    (mixin,) = _rowwise(lambda *a: (_gate_mix(*a),), "gate_mix_fwd",
                        [(z, Z_GS, D_MODEL), (z, Z_GA, D_MODEL), (y2, 0, D_MODEL), (y2, D_MODEL, D_MODEL), _whole(a_out)],
                        [], [(D_MODEL, BF16)])
    mix = _mm(mixin, W["w_o"], "mix_o")
    (h1,) = _rowwise(lambda h, r, g, b: (_ln_res(h, r, g, b),), "ln1_fwd", [_whole(h0), _whole(mix)], [g1, b1], [(D_MODEL, F32)])
    xq = _mm(h1, W["w_xq"], "x_q")
    xk = _mm(mem, W["w_xk"], "x_k")
    xv = _mm(mem, W["w_xv"], "x_v")
    (xo,) = _rowwise(_xattn_fwd_fn, "xattn_fwd", [_whole(xq)], [xk, xv], [(D_MODEL, BF16)])
    xa = _mm(xo, W["w_xo"], "x_o")
    (h2,) = _rowwise(lambda h, r, g, b: (_ln_res(h, r, g, b),), "ln2_fwd", [_whole(h1), _whole(xa)], [g2, b2], [(D_MODEL, F32)])
    a_up = _mm(h2, W["w_up"], "mlp_up")
    ff = _mm(a_up, W["w_down"], "mlp_down", act=_relu2)

    def loss_fn(h, r, tgt, g, b):
        def f(r_, g_, b_):
            e = _ln_res(h, r_, g_, b_) - tgt
            return 0.5 * jnp.sum(jnp.mean(e * e, axis=-1))
        lv, (dr_, dg_, db_) = jax.value_and_grad(f, argnums=(0, 1, 2))(r, g, b)
        return dr_, dg_, db_, jnp.broadcast_to(lv, (1, 128))

    dr3, dg3, db3, lossv = _rowwise(loss_fn, "ln3_loss_bwd", [_whole(h2), _whole(ff), _whole(target)], [g3, b3],
                                    [(D_MODEL, F32)], [(1, D_MODEL), (1, D_MODEL), (1, 128)])
    gW, gs = {}, {"ln3_g": dg3, "ln3_b": db3}

    da = _mm(dr3, W["w_down"], "mlp_down_bwd_a", "nt", epi=lambda acc, a: acc * (2.0 * jnp.maximum(a, 0.0)), extras=(a_up,))
    gW["w_down"] = _mm(a_up, dr3, "mlp_down_bwd_w", "tn", m=4 * D_MODEL, n=D_MODEL, act=_relu2)
    gW["w_up"] = _mm(h2, da, "mlp_up_bwd_w", "tn", m=D_MODEL, n=4 * D_MODEL)
    dh2 = _mm(da, W["w_up"], "mlp_up_bwd_a", "nt", epi=lambda acc, e: acc + DN_ALPHA * e, extras=(dr3,))

    def ln_res_bwd(h, r, dy, g, b):
        _, vjp = jax.vjp(_ln_res, h, r, g, b)
        _, dr_, dg_, db_ = vjp(dy)
        return dr_, dg_, db_

    dr2, gs["ln2_g"], gs["ln2_b"] = _rowwise(ln_res_bwd, "ln2_bwd", [_whole(h1), _whole(xa), _whole(dh2)], [g2, b2],
                                             [(D_MODEL, F32)], [(1, D_MODEL)] * 2)
    dxo = _mm(dr2, W["w_xo"], "x_o_bwd_a", "nt")
    gW["w_xo"] = _mm(xo, dr2, "x_o_bwd_w", "tn", m=D_MODEL, n=D_MODEL)
    dxq, dxk, dxv = _rowwise(_xattn_bwd_fn, "xattn_bwd", [_whole(xq), _whole(dxo)], [xk, xv], [(D_MODEL, F32)],
                             [(xk.shape[0], D_MODEL)] * 2)
    gW["w_xq"] = _mm(h1, dxq, "x_q_bwd_w", "tn", m=D_MODEL, n=D_MODEL)
    gW["w_xk"] = _mm(mem, dxk, "x_k_bwd_w", "tn", m=D_MODEL, n=D_MODEL)
    gW["w_xv"] = _mm(mem, dxv, "x_v_bwd_w", "tn", m=D_MODEL, n=D_MODEL)
    dh1 = _mm(dxq, W["w_xq"], "x_q_bwd_a", "nt", epi=lambda acc, e: acc + DN_ALPHA * e, extras=(dr2,))

    dr1, gs["ln1_g"], gs["ln1_b"] = _rowwise(ln_res_bwd, "ln1_bwd", [_whole(h0), _whole(mix), _whole(dh1)], [g1, b1],
                                             [(D_MODEL, F32)], [(1, D_MODEL)] * 2)
    dmixin = _mm(dr1, W["w_o"], "mix_o_bwd_a", "nt")
    gW["w_o"] = _mm(mixin, dr1, "mix_o_bwd_w", "tn", m=D_MODEL, n=D_MODEL)

    def gate_mix_bwd(gs_, ga_, ya, yb, ao, dy):
        _, vjp = jax.vjp(_gate_mix, gs_, ga_, ya, yb, ao)
        return vjp(dy)

    dgs, dga, dy2a, dy2b, da_out = _rowwise(
        gate_mix_bwd, "gate_mix_bwd",
        [(z, Z_GS, D_MODEL), (z, Z_GA, D_MODEL), (y2, 0, D_MODEL), (y2, D_MODEL, D_MODEL), _whole(a_out), _whole(dmixin)],
        [], [(D_MODEL, F32)] * 5)

    do = _mm(da_out, w_oa_p, "mla_oa_bwd_a", "nt")
    g_oa_p = _mm(o, da_out, "mla_oa_bwd_w", "tn", m=HEADS * HP, n=D_MODEL)
    gW["w_oa"] = g_oa_p.reshape(HEADS, HP, D_MODEL)[:, :VD].reshape(HEADS * VD, D_MODEL)
    delta, do16 = _rowwise(_attn_delta_fn, "mla_attn_delta", [_whole(do), _whole(o)], [], [(HEADS, F32), (HEADS * HP, BF16)])
    dq, dk, dv = _flash_bwd(q, k, v, do16, lse.reshape(HEADS, 1, T), delta.T.reshape(HEADS, 1, T), "mla_attn_bwd")

    def rope_bwd(dq_, dk_, tc_, t1, t2):
        dqr = _rope_t(dq_, _tile8(tc_), _tile8(t1), _tile8(t2))
        dkr = dk_[:, 0:HP]
        for hh in range(1, HEADS):
            dkr = dkr + dk_[:, hh * HP:(hh + 1) * HP]
        dkr = _rope_t(jnp.where(_rope_lanes(dkr.shape), dkr, 0.0), tc_, t1, t2)
        dkr = pltpu.roll(dkr, NOPE, 1)
        lane = lax.broadcasted_iota(jnp.int32, dkr.shape, 1)
        return dqr, jnp.where(lane < ROPE, dkr, 0.0)

    dq_raw, dkr = _rowwise(rope_bwd, "mla_rope_bwd", [_whole(dq), _whole(dk), _whole(t_cos), _whole(t_s1), _whole(t_s2)], [],
                           [(HEADS * HP, F32), (HP, F32)])
    dcqn = _mm(dq_raw, w_uq_p, "mla_uq_bwd_a", "nt")
    gW["w_uq"] = _unpad_heads(_mm(cqn, dq_raw, "mla_uq_bwd_w", "tn", m=S5_W, n=HEADS * HP), QK)
    dckvn_k = _mm(dk, w_uk_p, "mla_uk_bwd_a", "nt")
    dckvn = _mm(dv, w_uv_p, "mla_uv_bwd_a", "nt", epi=lambda acc, e: acc + e, extras=(dckvn_k,))
    g_uk = _unpad_heads(_mm(ckvn, dk, "mla_uk_bwd_w", "tn", m=S5_W, n=HEADS * HP), NOPE).reshape(S5_W, HEADS, NOPE)
    g_uv = _unpad_heads(_mm(ckvn, dv, "mla_uv_bwd_w", "tn", m=S5_W, n=HEADS * HP), VD).reshape(S5_W, HEADS, VD)
    gW["w_ukv"] = jnp.concatenate([g_uk, g_uv], axis=2).reshape(S5_W, HEADS * (NOPE + VD))

    def norm_bwd(a, b, da_, db_, ga, gb):
        _, vja = jax.vjp(_rms, a, ga)
        _, vjb = jax.vjp(_rms, b, gb)
        dxa, dga_ = vja(da_)
        dxb, dgb_ = vjb(db_)
        return dxa, dxb, dga_, dgb_

    dcq, dckv, gs["q_norm_g"], gs["kv_norm_g"] = _rowwise(
        norm_bwd, "mla_norm_bwd", [(z, Z_CQ, S5_W), (z, Z_CKV, S5_W), _whole(dcqn), _whole(dckvn)], [gq, gkv],
        [(S5_W, F32)] * 2, [(1, S5_W)] * 2)

    dy2 = jnp.concatenate([dy2a, dy2b], axis=1)
    dyg = _mm(dy2, W["w_glu"], "glu_bwd_a", "nt")
    gW["w_glu"] = _mm(yg, dy2, "glu_bwd_w", "tn", m=S5_W, n=2 * D_MODEL)

    def gelu_bwd(a, u, dy, d):
        _, vjp = jax.vjp(_gelu_skip, a, u, d)
        return vjp(dy)

    dyc, du_skip, gs["s5_d"] = _rowwise(gelu_bwd, "s5_gelu_bwd", [_whole(yc), (z, Z_U, S5_W), _whole(dyg)], [d_skip],
                                        [(S5_W, F32)] * 2, [(1, S5_W)])
    g_cst = _mm(hs, dyc, "s5_c_bwd_w", "tn", m=2 * S5_N, n=S5_W)
    dhs = _mm(dyc, c_st, "s5_c_bwd_a", "nt")
    lam, d_ab = _s5_scan_bwd(dhs, hs, apow_rev, astep_conj, "s5_scan_bwd")
    g_bst = _mm(z, lam, "s5_bu_bwd_w", "tn", a_off=Z_U, m=S5_W, n=2 * S5_N)
    du = _mm(lam, b_st, "s5_bu_bwd_a", "nt", epi=lambda acc, e: acc + e, extras=(du_skip,))
    gs["s5_ab"] = d_ab
    gs["s5_bb_re"] = jnp.swapaxes(_blockdiag_t(g_bst[:, :S5_N], S5_H, S5_P), 1, 2)
    gs["s5_bb_im"] = jnp.swapaxes(_blockdiag_t(g_bst[:, S5_N:], S5_H, S5_P), 1, 2)
    gs["s5_c_re"] = jnp.swapaxes(_blockdiag_t(g_cst[:S5_N], S5_P, S5_H), 1, 2)
    gs["s5_c_im"] = -jnp.swapaxes(_blockdiag_t(g_cst[S5_N:], S5_P, S5_H), 1, 2)

    dz = jnp.concatenate([dgs, dga, du, dcq, dckv, dkr], axis=1)
    g_in_re = _mm(h0, dz, "in_proj_bwd_w", "tn", m=D_MODEL, n=Z_W, tn=Z_W, tk=512)
    gW["w_in"] = jnp.concatenate([g_in_re[:, Z_U:Z_KR], g_in_re[:, Z_KR:Z_KR + ROPE], g_in_re[:, :Z_U]], axis=1)
    dh0 = _mm(dz, w_in_re, "in_proj_bwd_a", "nt", tk=Z_W, epi=lambda acc, e: acc + DN_ALPHA * e, extras=(dr1,))

    def ln_bwd(a, dy, g, b):
        _, vjp = jax.vjp(_ln, a, g, b)
        return vjp(dy)

    dx, gs["ln_in_g"], gs["ln_in_b"] = _rowwise(ln_bwd, "ln_in_bwd", [_whole(x), _whole(dh0)], [g_in, b_in],
                                                [(D_MODEL, F32)], [(1, D_MODEL)] * 2)
    return lossv[:, :1], dx, gW, gs


_RAW_SMALL = (("loss", (1, 1)), ("ln_in_g", (1, D_MODEL)), ("ln_in_b", (1, D_MODEL)), ("ln1_g", (1, D_MODEL)),
              ("ln1_b", (1, D_MODEL)), ("ln2_g", (1, D_MODEL)), ("ln2_b", (1, D_MODEL)), ("ln3_g", (1, D_MODEL)),
              ("ln3_b", (1, D_MODEL)), ("q_norm_g", (1, S5_W)), ("kv_norm_g", (1, S5_W)), ("s5_d", (1, S5_W)),
              ("s5_ab", (1, 2 * S5_N)), ("s5_bb_re", (S5_G, S5_P, S5_H)), ("s5_bb_im", (S5_G, S5_P, S5_H)),
              ("s5_c_re", (S5_G, S5_H, S5_P)), ("s5_c_im", (S5_G, S5_H, S5_P)))


def kernel(x, mem, positions, ln_in_g, ln_in_b, w_in, s5_lam_re, s5_lam_im, s5_log_dt, s5_b_re, s5_b_im, s5_c_re, s5_c_im, s5_d, w_glu, q_norm_g, w_uq, kv_norm_g, w_ukv, w_oa, w_o, ln1_g, ln1_b, w_xq, w_xk, w_xv, w_xo, ln2_g, ln2_b, w_up, w_down, ln3_g, ln3_b, loss_target, m_ln_in_g, m_ln_in_b, m_w_in, m_s5_lam_re, m_s5_lam_im, m_s5_log_dt, m_s5_b_re, m_s5_b_im, m_s5_c_re, m_s5_c_im, m_s5_d, m_w_glu, m_q_norm_g, m_w_uq, m_kv_norm_g, m_w_ukv, m_w_oa, m_w_o, m_ln1_g, m_ln1_b, m_w_xq, m_w_xk, m_w_xv, m_w_xo, m_ln2_g, m_ln2_b, m_w_up, m_w_down, m_ln3_g, m_ln3_b, v_ln_in_g, v_ln_in_b, v_w_in, v_s5_lam_re, v_s5_lam_im, v_s5_log_dt, v_s5_b_re, v_s5_b_im, v_s5_c_re, v_s5_c_im, v_s5_d, v_w_glu, v_q_norm_g, v_w_uq, v_kv_norm_g, v_w_ukv, v_w_oa, v_w_o, v_ln1_g, v_ln1_b, v_w_xq, v_w_xk, v_w_xv, v_w_xo, v_ln2_g, v_ln2_b, v_w_up, v_w_down, v_ln3_g, v_ln3_b):
    a = dict(locals())
    wts = {n: a[n] for n in WEIGHTS}
    ms = {n: a["m_" + n] for n in WEIGHTS}
    vs = {n: a["v_" + n] for n in WEIGHTS}
    cidx = lax.axis_index("c").astype(jnp.int32).reshape(1)

    shard2d = {n: wts[n].reshape(wts[n].shape[-2], wts[n].shape[-1]) for n in BIG}
    flat = _pack_flat([shard2d[n] for n in BIG], PACK_ALIGN, BF16)
    rows_half = flat.shape[0] // (2 * 1024)
    gathered = _gather_weights(flat.reshape(2, rows_half, 1024)).reshape(4, -1)
    W, off = {}, 0
    for n in BIG:
        r, cc = shard2d[n].shape
        W[n] = _full_from_shards(n, gathered[:, off:off + r * cc].reshape(4, r, cc))
        off += r * cc

    sp = {n: wts[n] for n in SMALL}
    sp_local = {n: (sp[n][0] if sp[n].ndim > 1 else sp[n]) for n in SMALL}
    lossv, dx, gW, gs = _local_step(x[0], mem[0], positions[0], loss_target[0], W, sp_local)

    gflat = jnp.concatenate([_shards_from_full(n, gW[n]) for n in BIG], axis=1)
    pad = (-gflat.shape[1]) % PACK_ALIGN
    gflat = jnp.pad(gflat, ((0, 0), (0, pad))).reshape(4, 2, rows_half, 1024)
    recv = _pair_exchange(gflat)
    psum = _pair_add(gflat, recv, cidx)
    slots = _chip_exchange(psum)
    red = _chip_sum(slots)
    gsh_flat = _pair_share(red).reshape(-1)
    g_out, off = {}, 0
    for n in BIG:
        r, cc = shard2d[n].shape
        g_out[n] = gsh_flat[off:off + r * cc].reshape(r, cc)
        off += r * cc

    gs["loss"] = lossv
    raw = _pack_flat([gs[n].reshape(s) for n, s in _RAW_SMALL], 8 * 1024, F32)
    raw = _allreduce_small(raw.reshape(-1, 1024)).reshape(-1)
    rs = dict(zip([n for n, _ in _RAW_SMALL], _unpack_flat(raw, [s for _, s in _RAW_SMALL])))
    loss = rs["loss"].reshape(())
    _, disc_vjp = jax.vjp(_s5_discretize, sp_local["s5_lam_re"], sp_local["s5_lam_im"], sp_local["s5_log_dt"],
                          sp_local["s5_b_re"], sp_local["s5_b_im"])
    d_ab = rs["s5_ab"].reshape(2, S5_G, S5_P)
    g_lre, g_lim, g_ldt, g_bre, g_bim = disc_vjp((d_ab[0], d_ab[1], rs["s5_bb_re"], rs["s5_bb_im"]))
    small_g = {"s5_lam_re": g_lre, "s5_lam_im": g_lim, "s5_log_dt": g_ldt, "s5_b_re": g_bre, "s5_b_im": g_bim,
               "s5_c_re": rs["s5_c_re"], "s5_c_im": rs["s5_c_im"]}
    for n in ("ln_in_g", "ln_in_b", "ln1_g", "ln1_b", "ln2_g", "ln2_b", "ln3_g", "ln3_b", "q_norm_g", "kv_norm_g", "s5_d"):
        small_g[n] = rs[n]

    grads, deltas, new_m, new_v = {}, {}, {}, {}
    for n in BIG:
        d_, m_, v_ = _adamw(shard2d[n], g_out[n], ms[n].reshape(shard2d[n].shape), vs[n].reshape(shard2d[n].shape), "adamw_" + n)
        grads[n] = g_out[n].reshape(wts[n].shape)
        deltas[n], new_m[n], new_v[n] = (t.reshape(wts[n].shape) for t in (d_, m_, v_))
    packs = [_pack_flat([t[n] for n in SMALL], 8 * 1024, F32).reshape(-1, 1024)
             for t in (wts, {n: small_g[n].reshape(wts[n].shape) for n in SMALL}, ms, vs)]
    sd, sm, sv = _adamw(*packs, "adamw_small")
    shapes = [wts[n].shape for n in SMALL]
    for n, d_, m_, v_ in zip(SMALL, _unpack_flat(sd.reshape(-1), shapes), _unpack_flat(sm.reshape(-1), shapes),
                             _unpack_flat(sv.reshape(-1), shapes)):
        grads[n] = small_g[n].reshape(wts[n].shape)
        deltas[n], new_m[n], new_v[n] = d_, m_, v_

    return (loss, dx[None], *[grads[n] for n in WEIGHTS], *[deltas[n] for n in WEIGHTS],
            *[new_m[n] for n in WEIGHTS], *[new_v[n] for n in WEIGHTS])
```

```python
import functools
import math

import jax
import jax.numpy as jnp
from jax import lax
from jax.experimental import pallas as pl
from jax.experimental.pallas import tpu as pltpu

F32 = jnp.float32
BF16 = jnp.bfloat16
MESH = pl.DeviceIdType.MESH

D_MODEL = 1024
S5_W = 256
S5_G = 16
S5_H = 16
S5_P = 64
S5_N = S5_G * S5_P
S5_MAX_RE = -1e-4
HEADS = 8
NOPE = 64
ROPE = 32
QK = NOPE + ROPE
VD = 64
HP = 128
XH = 4
XD = 256
LN_EPS = 1e-5
RMS_EPS = 1e-6
NEG_INF = -1e30
DN_ALPHA = 2.0 ** 0.25
ROPE_THETA = 10000.0
ADAM_LR, ADAM_B1, ADAM_B2, ADAM_EPS, ADAM_WD, ADAM_STEP = 0.001, 0.9, 0.999, 1e-08, 0.01, 10

Z_GS, Z_GA, Z_U, Z_CQ, Z_CKV, Z_KR, Z_W = 0, 1024, 2048, 2304, 2560, 2816, 2944
SCAN_CHUNK = 256

BIG = ("w_in", "w_glu", "w_uq", "w_ukv", "w_oa", "w_up", "w_o", "w_xq", "w_xk", "w_xv", "w_xo", "w_down")
COL_SHARDED = ("w_in", "w_glu", "w_uq", "w_ukv", "w_oa", "w_up")
SMALL = ("ln_in_g", "ln_in_b", "s5_lam_re", "s5_lam_im", "s5_log_dt", "s5_b_re", "s5_b_im", "s5_c_re", "s5_c_im",
         "s5_d", "q_norm_g", "kv_norm_g", "ln1_g", "ln1_b", "ln2_g", "ln2_b", "ln3_g", "ln3_b")
WEIGHTS = ("ln_in_g", "ln_in_b", "w_in", "s5_lam_re", "s5_lam_im", "s5_log_dt", "s5_b_re", "s5_b_im", "s5_c_re",
           "s5_c_im", "s5_d", "w_glu", "q_norm_g", "w_uq", "kv_norm_g", "w_ukv", "w_oa", "w_o", "ln1_g", "ln1_b",
           "w_xq", "w_xk", "w_xv", "w_xo", "ln2_g", "ln2_b", "w_up", "w_down", "ln3_g", "ln3_b")
PACK_ROWS = 2 * 16


def _row_tile(n, cap=512):
    best = n
    for t in range(8, min(n, cap) + 1, 8):
        if n % t == 0:
            best = t
    return best


def _pick(n, cap):
    best = None
    for t in range(128, min(n, cap) + 1, 128):
        if n % t == 0:
            best = t
    return best if best is not None and (best >= 512 or best == n) else n


def _mm(a, b, name, mode="nn", *, a_off=0, b_off=0, m=None, n=None, act=None, epi=None, extras=(),
        out_dtype=F32, tm=1024, tn=1024, tk=None):
    if mode == "nn":
        M, (K, N) = a.shape[0], b.shape
    elif mode == "nt":
        M, (N, K) = a.shape[0], b.shape
    else:
        K, M, N = a.shape[0], m, n
    if mode == "tn":
        tm, tn, tk = _pick(M, tm), _pick(N, tn), min(tk or 512, K)
    else:
        tm, tn, tk = min(tm, M), _pick(N, tn), _pick(K, tk or 1024)
    assert M % tm == 0 and N % tn == 0 and K % tk == 0, (name, M, N, K, tm, tn, tk)
    nk = K // tk
    if mode == "tn":
        assert a_off % tm == 0 and b_off % tn == 0
        ao, bo = a_off // tm, b_off // tn
        a_spec = pl.BlockSpec((tk, tm), lambda i, j, k: (k, i + ao))
        b_spec = pl.BlockSpec((tk, tn), lambda i, j, k: (k, j + bo))
        dims = (((0,), (0,)), ((), ()))
    else:
        assert a_off % tk == 0
        ao = a_off // tk
        a_spec = pl.BlockSpec((tm, tk), lambda i, j, k: (i, k + ao))
        if mode == "nn":
            b_spec = pl.BlockSpec((tk, tn), lambda i, j, k: (k, j))
            dims = (((1,), (0,)), ((), ()))
        else:
            b_spec = pl.BlockSpec((tn, tk), lambda i, j, k: (j, k))
            dims = (((1,), (1,)), ((), ()))
    e_spec = pl.BlockSpec((tm, tn), lambda i, j, k: (i, j))
    n_extra = len(extras)

    def body(*refs):
        a_ref, b_ref = refs[0], refs[1]
        e_refs = refs[2:2 + n_extra]
        o_ref = refs[2 + n_extra]
        av = a_ref[...]
        if act is not None:
            av = act(av.astype(F32))
        p = lax.dot_general(av.astype(BF16), b_ref[...].astype(BF16), dims, preferred_element_type=F32)

        def finish(r):
            if epi is not None:
                r = epi(r, *[e[...] for e in e_refs])
            o_ref[...] = r.astype(o_ref.dtype)

        if nk == 1:
            finish(p)
        else:
            acc = refs[3 + n_extra]
            k = pl.program_id(2)

            @pl.when(k == 0)
            def _():
                acc[...] = p

            @pl.when(k > 0)
            def _():
                acc[...] += p

            @pl.when(k == nk - 1)
            def _():
                finish(acc[...])

    return pl.pallas_call(
        body, name=name, grid=(M // tm, N // tn, nk),
        in_specs=[a_spec, b_spec] + [e_spec] * n_extra, out_specs=e_spec,
        out_shape=jax.ShapeDtypeStruct((M, N), out_dtype),
        scratch_shapes=[pltpu.VMEM((tm, tn), F32)] if nk > 1 else [],
        compiler_params=pltpu.CompilerParams(dimension_semantics=("parallel", "parallel", "arbitrary")),
    )(a, b, *extras)


def _rowwise(fn, name, rows, pars, outs, accs=(), tm=256):
    T = rows[0][0].shape[0]
    tm = min(tm, T)
    assert T % tm == 0
    n_in, n_out = len(rows) + len(pars), len(outs)
    in_specs = []
    for arr, off, w in rows:
        assert off % w == 0 and arr.shape[0] == T, name
        in_specs.append(pl.BlockSpec((tm, w), functools.partial(lambda i, o: (i, o), o=off // w)))
    for p in pars:
        in_specs.append(pl.BlockSpec(p.shape, lambda i: (0, 0)))
    out_specs = [pl.BlockSpec((tm, w), lambda i: (i, 0)) for w, _ in outs]
    out_specs += [pl.BlockSpec(s, lambda i: (0, 0)) for s in accs]
    out_shape = [jax.ShapeDtypeStruct((T, w), dt) for w, dt in outs] + [jax.ShapeDtypeStruct(s, F32) for s in accs]

    def body(*refs):
        res = fn(*[r[...] for r in refs[:n_in]])
        o_refs = refs[n_in:]
        for r, v in zip(o_refs[:n_out], res[:n_out]):
            r[...] = v.astype(r.dtype)
        if accs:
            i = pl.program_id(0)

            @pl.when(i == 0)
            def _():
                for r in o_refs[n_out:]:
                    r[...] = jnp.zeros_like(r)

            for r, v in zip(o_refs[n_out:], res[n_out:]):
                r[...] += v

    res = pl.pallas_call(
        body, name=name, grid=(T // tm,), in_specs=in_specs, out_specs=out_specs, out_shape=out_shape,
        compiler_params=pltpu.CompilerParams(dimension_semantics=("arbitrary",)),
    )(*[r[0] for r in rows], *pars)
    return res


def _whole(arr):
    return (arr, 0, arr.shape[1])


def _ln(x, g, b):
    mu = jnp.mean(x, axis=-1, keepdims=True)
    xc = x - mu
    var = jnp.mean(xc * xc, axis=-1, keepdims=True)
    return xc * lax.rsqrt(var + LN_EPS) * g + b


def _ln_res(h, r, g, b):
    return _ln(DN_ALPHA * h + r, g, b)


def _rms(x, g):
    return x * lax.rsqrt(jnp.mean(x * x, axis=-1, keepdims=True) + RMS_EPS) * g


def _gelu_skip(yc, u, d):
    y = yc + d * u
    return 0.5 * y * (1.0 + lax.erf(y * (1.0 / math.sqrt(2.0))))


def _gate_mix(gs, ga, y2a, y2b, aout):
    return jax.nn.sigmoid(gs) * (y2a * jax.nn.sigmoid(y2b)) + jax.nn.sigmoid(ga) * aout


def _relu2(a):
    r = jnp.maximum(a, 0.0)
    return r * r


def _tile8(t):
    return jnp.concatenate([t] * HEADS, axis=1)


def _rope(x, cos, s1, s2):
    w = x.shape[1]
    return x * cos + pltpu.roll(x, ROPE // 2, 1) * s1 + pltpu.roll(x, w - ROPE // 2, 1) * s2


def _rope_t(dy, cos, s1, s2):
    w = dy.shape[1]
    return dy * cos + pltpu.roll(dy * s1, w - ROPE // 2, 1) + pltpu.roll(dy * s2, ROPE // 2, 1)


def _rope_lanes(shape):
    lane = lax.broadcasted_iota(jnp.int32, shape, 1) % HP
    return (lane >= NOPE) & (lane < QK)


ATT_HB = 2
ATT_SCALE = QK ** -0.5
LOG2E = 1.4426950408889634
LN2 = 0.6931471805599453
_NT = (((1,), (1,)), ((), ()))
_TN = (((0,), (0,)), ((), ()))


def _flash_fwd(q, k, v, name, bq=512):
    T = q.shape[0]
    bq = min(bq, T)
    nq = T // bq
    hb = ATT_HB

    def body(q_ref, k_ref, v_ref, o_ref, lse_ref, m_sc, l_sc, acc_sc):
        i, j = pl.program_id(1), pl.program_id(2)

        @pl.when(j == 0)
        def _():
            m_sc[...] = jnp.full_like(m_sc, NEG_INF)
            l_sc[...] = jnp.zeros_like(l_sc)
            acc_sc[...] = jnp.zeros_like(acc_sc)

        def step(masked):
            for hh in range(hb):
                sl = slice(hh * HP, (hh + 1) * HP)
                s = lax.dot_general(q_ref[:, sl], k_ref[:, sl], _NT, preferred_element_type=F32) * (ATT_SCALE * LOG2E)
                if masked:
                    row = lax.broadcasted_iota(jnp.int32, (bq, bq), 0)
                    col = lax.broadcasted_iota(jnp.int32, (bq, bq), 1)
                    s = jnp.where(col <= row, s, NEG_INF)
                m_prev = m_sc[hh]
                m_new = jnp.maximum(m_prev, jnp.max(s, axis=1, keepdims=True))
                alpha = jnp.exp2(m_prev - m_new)
                p = jnp.exp2(s - jnp.concatenate([m_new] * (bq // HP), axis=1))
                l_sc[hh] = alpha * l_sc[hh] + jnp.sum(p, axis=1, keepdims=True)
                acc_sc[hh] = alpha * acc_sc[hh] + jnp.dot(p.astype(BF16), v_ref[:, sl], preferred_element_type=F32)
                m_sc[hh] = m_new

        @pl.when(j < i)
        def _():
            step(False)

        @pl.when(j == i)
        def _():
            step(True)

        @pl.when(j == nq - 1)
        def _():
            for hh in range(hb):
                o_ref[:, hh * HP:(hh + 1) * HP] = acc_sc[hh] / l_sc[hh]
                lse_ref[hh] = (m_sc[hh][:, :1] + jnp.log2(l_sc[hh][:, :1])) * LN2

    qs = pl.BlockSpec((bq, hb * HP), lambda h, i, j: (i, h))
    ks = pl.BlockSpec((bq, hb * HP), lambda h, i, j: (jnp.minimum(j, i), h))
    return pl.pallas_call(
        body, name=name, grid=(HEADS // hb, nq, nq), in_specs=[qs, ks, ks],
        out_specs=[qs, pl.BlockSpec((hb, bq, 1), lambda h, i, j: (h, i, 0))],
        out_shape=[jax.ShapeDtypeStruct((T, HEADS * HP), F32), jax.ShapeDtypeStruct((HEADS, T, 1), F32)],
        scratch_shapes=[pltpu.VMEM((hb, bq, HP), F32)] * 3,
        compiler_params=pltpu.CompilerParams(dimension_semantics=("parallel", "parallel", "arbitrary")),
    )(q, k, v)


def _attn_delta_fn(do, o):
    prod = do * o
    cols = [jnp.sum(prod[:, h * HP:(h + 1) * HP], axis=1, keepdims=True) for h in range(HEADS)]
    return jnp.concatenate(cols, axis=1), do


def _flash_bwd(q, k, v, do, lse_row, delta_row, name, bq=512):
    T = q.shape[0]
    bq = min(bq, T)
    nq = T // bq
    hb = ATT_HB

    def body(q_ref, k_ref, v_ref, do_ref, lse_ref, dl_ref, dq_ref, dk_ref, dv_ref):
        j, i = pl.program_id(1), pl.program_id(2)

        @pl.when((i == 0) & (j == 0))
        def _():
            dq_ref[...] = jnp.zeros_like(dq_ref)

        @pl.when(i == 0)
        def _():
            dk_ref[...] = jnp.zeros_like(dk_ref)
            dv_ref[...] = jnp.zeros_like(dv_ref)

        def step(masked):
            rows = pl.ds(pl.multiple_of(i * bq, bq), bq)
            for hh in range(hb):
                sl = slice(hh * HP, (hh + 1) * HP)
                qh, kh, doh = q_ref[:, sl], k_ref[:, sl], do_ref[:, sl]
                st = lax.dot_general(kh, qh, _NT, preferred_element_type=F32) * (ATT_SCALE * LOG2E)
                pt = jnp.exp2(st - lse_ref[hh] * LOG2E)
                if masked:
                    krow = lax.broadcasted_iota(jnp.int32, (bq, bq), 0)
                    qcol = lax.broadcasted_iota(jnp.int32, (bq, bq), 1)
                    pt = jnp.where(krow <= qcol, pt, 0.0)
                dv_ref[:, sl] += jnp.dot(pt.astype(BF16), doh, preferred_element_type=F32)
                dpt = lax.dot_general(v_ref[:, sl], doh, _NT, preferred_element_type=F32)
                dst = (pt * (dpt - dl_ref[hh])).astype(BF16)
                dk_ref[:, sl] += jnp.dot(dst, qh, preferred_element_type=F32)
                dq_ref[rows, sl] += lax.dot_general(dst, kh, _TN, preferred_element_type=F32)

        @pl.when(i > j)
        def _():
            step(False)

        @pl.when(i == j)
        def _():
            step(True)

        @pl.when(i == nq - 1)
        def _():
            dk_ref[...] *= ATT_SCALE

        @pl.when((i == nq - 1) & (j == nq - 1))
        def _():
            dq_ref[...] *= ATT_SCALE

    qs = pl.BlockSpec((bq, hb * HP), lambda h, j, i: (jnp.maximum(i, j), h))
    ks = pl.BlockSpec((bq, hb * HP), lambda h, j, i: (j, h))
    rs = pl.BlockSpec((hb, 1, bq), lambda h, j, i: (h, 0, jnp.maximum(i, j)))
    full = pl.BlockSpec((T, hb * HP), lambda h, j, i: (0, h))
    return pl.pallas_call(
        body, name=name, grid=(HEADS // hb, nq, nq), in_specs=[qs, ks, ks, qs, rs, rs], out_specs=[full, ks, ks],
        out_shape=[jax.ShapeDtypeStruct((T, HEADS * HP), F32)] * 3,
        compiler_params=pltpu.CompilerParams(dimension_semantics=("parallel", "arbitrary", "arbitrary")),
    )(q, k, v, do, lse_row, delta_row)


def _xattn_heads(q, k, v):
    scale = XD ** -0.5
    ps = []
    for h in range(XH):
        sl = slice(h * XD, (h + 1) * XD)
        s = lax.dot_general(q[:, sl].astype(BF16), k[:, sl].astype(BF16), (((1,), (1,)), ((), ())),
                            preferred_element_type=F32) * scale
        e = jnp.exp(s - jnp.max(s, axis=1, keepdims=True))
        ps.append(e / jnp.sum(e, axis=1, keepdims=True))
    return ps


def _xattn_fwd_fn(q, k, v):
    ps = _xattn_heads(q, k, v)
    o = [jnp.dot(p.astype(BF16), v[:, h * XD:(h + 1) * XD].astype(BF16), preferred_element_type=F32)
         for h, p in enumerate(ps)]
    return (jnp.concatenate(o, axis=1),)


def _xattn_bwd_fn(q, do, k, v):
    scale = XD ** -0.5
    ps = _xattn_heads(q, k, v)
    dqs, dks, dvs = [], [], []
    tdims = (((0,), (0,)), ((), ()))
    for h, p in enumerate(ps):
        sl = slice(h * XD, (h + 1) * XD)
        doh = do[:, sl].astype(BF16)
        dvs.append(lax.dot_general(p.astype(BF16), doh, tdims, preferred_element_type=F32))
        dp = lax.dot_general(doh, v[:, sl].astype(BF16), (((1,), (1,)), ((), ())), preferred_element_type=F32)
        ds = (p * (dp - jnp.sum(dp * p, axis=1, keepdims=True)) * scale).astype(BF16)
        dqs.append(jnp.dot(ds, k[:, sl].astype(BF16), preferred_element_type=F32))
        dks.append(lax.dot_general(ds, q[:, sl].astype(BF16), tdims, preferred_element_type=F32))
    return jnp.concatenate(dqs, axis=1), jnp.concatenate(dks, axis=1), jnp.concatenate(dvs, axis=1)


def _cmul_add(xr, xi, ar, ai, sr, si):
    return xr + ar * sr - ai * si, xi + ar * si + ai * sr


def _chunk_scan(xr, xi, astep_ref, reverse):
    tc = xr.shape[0]
    rows = lax.broadcasted_iota(jnp.int32, xr.shape, 0)
    for k in range(int(math.log2(tc))):
        s = 1 << k
        ar, ai = astep_ref[k:k + 1, :S5_N], astep_ref[k:k + 1, S5_N:]
        if reverse:
            keep, sh = rows < tc - s, tc - s
        else:
            keep, sh = rows >= s, s
        sr = jnp.where(keep, pltpu.roll(xr, sh, 0), 0.0)
        si = jnp.where(keep, pltpu.roll(xi, sh, 0), 0.0)
        xr, xi = _cmul_add(xr, xi, ar, ai, sr, si)
    return xr, xi


def _s5_scan_fwd(bu, apow, astep, name):
    T = bu.shape[0]
    tc = apow.shape[0]
    assert T % tc == 0

    def body(x_ref, apow_ref, astep_ref, h_ref, carry):
        @pl.when(pl.program_id(0) == 0)
        def _():
            carry[...] = jnp.zeros_like(carry)

        xr, xi = _chunk_scan(x_ref[:, :S5_N], x_ref[:, S5_N:], astep_ref, False)
        hr, hi = _cmul_add(xr, xi, apow_ref[:, :S5_N], apow_ref[:, S5_N:], carry[0:1, :S5_N], carry[0:1, S5_N:])
        h_ref[:, :S5_N] = hr
        h_ref[:, S5_N:] = hi
        carry[0:1, :S5_N] = hr[tc - 1:tc]
        carry[0:1, S5_N:] = hi[tc - 1:tc]

    blk = pl.BlockSpec((tc, 2 * S5_N), lambda i: (i, 0))
    return pl.pallas_call(
        body, name=name, grid=(T // tc,),
        in_specs=[blk, pl.BlockSpec(apow.shape, lambda i: (0, 0)), pl.BlockSpec(astep.shape, lambda i: (0, 0))],
        out_specs=blk, out_shape=jax.ShapeDtypeStruct((T, 2 * S5_N), F32),
        scratch_shapes=[pltpu.VMEM((8, 2 * S5_N), F32)],
        compiler_params=pltpu.CompilerParams(dimension_semantics=("arbitrary",)),
    )(bu, apow, astep)


def _s5_scan_bwd(dh, h, apow_rev, astep_conj, name):
    T = dh.shape[0]
    tc = apow_rev.shape[0]
    nc = T // tc

    def body(x_ref, h_ref, hp_ref, apow_ref, astep_ref, lam_ref, da_ref, carry):
        g = pl.program_id(0)

        @pl.when(g == 0)
        def _():
            carry[...] = jnp.zeros_like(carry)
            da_ref[...] = jnp.zeros_like(da_ref)

        xr, xi = _chunk_scan(x_ref[:, :S5_N], x_ref[:, S5_N:], astep_ref, True)
        lr, li = _cmul_add(xr, xi, apow_ref[:, :S5_N], apow_ref[:, S5_N:], carry[0:1, :S5_N], carry[0:1, S5_N:])
        lam_ref[:, :S5_N] = lr.astype(lam_ref.dtype)
        lam_ref[:, S5_N:] = li.astype(lam_ref.dtype)
        carry[0:1, :S5_N] = lr[0:1]
        carry[0:1, S5_N:] = li[0:1]
        rows = lax.broadcasted_iota(jnp.int32, (tc, S5_N), 0)
        first = jnp.where(g == nc - 1, 0.0, 1.0)
        pr = jnp.where(rows == 0, hp_ref[7:8, :S5_N] * first, pltpu.roll(h_ref[:, :S5_N], 1, 0))
        pi = jnp.where(rows == 0, hp_ref[7:8, S5_N:] * first, pltpu.roll(h_ref[:, S5_N:], 1, 0))
        da_ref[0:1, :S5_N] += jnp.sum(lr * pr + li * pi, axis=0, keepdims=True)
        da_ref[0:1, S5_N:] += jnp.sum(li * pr - lr * pi, axis=0, keepdims=True)

    blk = pl.BlockSpec((tc, 2 * S5_N), lambda g: (nc - 1 - g, 0))
    prev = pl.BlockSpec((8, 2 * S5_N), lambda g: (jnp.maximum((nc - 1 - g) * (tc // 8) - 1, 0), 0))
    return pl.pallas_call(
        body, name=name, grid=(nc,),
        in_specs=[blk, blk, prev, pl.BlockSpec(apow_rev.shape, lambda g: (0, 0)),
                  pl.BlockSpec(astep_conj.shape, lambda g: (0, 0))],
        out_specs=[blk, pl.BlockSpec((1, 2 * S5_N), lambda g: (0, 0))],
        out_shape=[jax.ShapeDtypeStruct((T, 2 * S5_N), BF16), jax.ShapeDtypeStruct((1, 2 * S5_N), F32)],
        scratch_shapes=[pltpu.VMEM((8, 2 * S5_N), F32)],
        compiler_params=pltpu.CompilerParams(dimension_semantics=("arbitrary",)),
    )(dh, h, h, apow_rev, astep_conj)


def _s5_discretize(lam_re, lam_im, log_dt, b_re, b_im):
    lr = jnp.minimum(lam_re, S5_MAX_RE)
    li = lam_im
    dt = jnp.exp(log_dt)[:, None]
    mag = jnp.exp(lr * dt)
    ang = li * dt
    ab_re = mag * jnp.cos(ang)
    ab_im = mag * jnp.sin(ang)
    den = lr * lr + li * li
    nr = ab_re - 1.0
    f_re = ((nr * lr + ab_im * li) / den)[..., None]
    f_im = ((ab_im * lr - nr * li) / den)[..., None]
    return ab_re, ab_im, f_re * b_re - f_im * b_im, f_re * b_im + f_im * b_re


def _cpow_table(ar, ai, n):
    def op(e1, e2):
        return e1[0] * e2[0] - e1[1] * e2[1], e1[0] * e2[1] + e1[1] * e2[0]
    pr, pi = lax.associative_scan(op, (jnp.broadcast_to(ar, (n, S5_N)), jnp.broadcast_to(ai, (n, S5_N))), axis=0)
    return pr, pi


def _adamw_fn(w, g, m, v):
    m = ADAM_B1 * m + (1.0 - ADAM_B1) * g
    v = ADAM_B2 * v + (1.0 - ADAM_B2) * (g * g)
    m_hat = m / (1.0 - ADAM_B1 ** ADAM_STEP)
    v_hat = v / (1.0 - ADAM_B2 ** ADAM_STEP)
    delta = -ADAM_LR * (m_hat / (jnp.sqrt(v_hat) + ADAM_EPS) + ADAM_WD * w)
    return delta, m, v


def _adamw(w, g, m, v, name):
    wd = w.shape[1]
    return _rowwise(_adamw_fn, name, [_whole(w), _whole(g), _whole(m), _whole(v)], [], [(wd, F32)] * 3,
                    tm=_row_tile(w.shape[0]))


def _me():
    return lax.axis_index("x"), lax.axis_index("y"), lax.axis_index("c")


def _chip_of(j):
    return j // 2, j % 2


_HBM = pl.BlockSpec(memory_space=pl.ANY)


def _gather_weights(shard):
    _, R, W = shard.shape

    def body(s_ref, out_ref, send_sems, recv_sems):
        x, y, c = _me()
        mine = 2 * x + y
        sib = (x, y, 1 - c)
        chips = [(1 - x, y), (x, 1 - y), (1 - x, 1 - y)]

        def rcopy(kk, src, chip_idx, half, to):
            return pltpu.make_async_remote_copy(src_ref=src, dst_ref=out_ref.at[chip_idx, half],
                                                send_sem=send_sems.at[kk], recv_sem=recv_sems.at[kk],
                                                device_id=to, device_id_type=MESH)

        first = [rcopy(j, s_ref.at[c], mine, c, (*chip, c)) for j, chip in enumerate(chips)]
        for cp in first:
            cp.start()
        passed = []
        for j, (px, py) in enumerate(chips):
            src_chip = 2 * px + py
            rcopy(j, s_ref.at[c], src_chip, c, (x, y, c)).wait_recv()
            fwd = rcopy(3 + j, out_ref.at[src_chip, c], src_chip, c, sib)
            fwd.start()
            passed.append(fwd)
        for j, (px, py) in enumerate(chips):
            rcopy(3 + j, s_ref.at[c], 2 * px + py, 1 - c, (x, y, c)).wait_recv()
        for cp in first + passed:
            cp.wait_send()

    return pl.pallas_call(
        body, name="gather_weights", in_specs=[_HBM], out_specs=_HBM,
        out_shape=jax.ShapeDtypeStruct((4, 2, R, W), shard.dtype),
        scratch_shapes=[pltpu.SemaphoreType.DMA((6,)), pltpu.SemaphoreType.DMA((6,))],
    )(shard)


def _pair_exchange(g):
    _, _, R, W = g.shape

    def body(g_ref, out_ref, send_sem, recv_sem):
        x, y, c = _me()
        cp = pltpu.make_async_remote_copy(src_ref=g_ref.at[:, 1 - c], dst_ref=out_ref, send_sem=send_sem,
                                          recv_sem=recv_sem, device_id=(x, y, 1 - c), device_id_type=MESH)
        cp.start()
        cp.wait()

    return pl.pallas_call(
        body, name="grad_pair_exchange", in_specs=[_HBM], out_specs=_HBM,
        out_shape=jax.ShapeDtypeStruct((4, R, W), g.dtype),
        scratch_shapes=[pltpu.SemaphoreType.DMA(()), pltpu.SemaphoreType.DMA(())],
    )(g)


def _pair_add(g, recv, cidx):
    _, _, R, W = g.shape
    tr = _row_tile(R)

    def body(c_ref, a_ref, b_ref, o_ref):
        o_ref[...] = (a_ref[...] + b_ref[...]).astype(o_ref.dtype)

    gs = pltpu.PrefetchScalarGridSpec(
        num_scalar_prefetch=1, grid=(4, R // tr),
        in_specs=[pl.BlockSpec((None, None, tr, W), lambda j, r, c: (j, c[0], r, 0)),
                  pl.BlockSpec((None, tr, W), lambda j, r, c: (j, r, 0))],
        out_specs=pl.BlockSpec((None, tr, W), lambda j, r, c: (j, r, 0)))
    return pl.pallas_call(body, name="grad_pair_add", grid_spec=gs,
                          out_shape=jax.ShapeDtypeStruct((4, R, W), BF16))(cidx, g, recv)


def _chip_exchange(p):
    _, R, W = p.shape

    def body(p_ref, out_ref, send_sems, recv_sems):
        x, y, c = _me()
        mine = 2 * x + y
        chips = [(1 - x, y), (x, 1 - y), (1 - x, 1 - y)]
        cps = []
        for j, (px, py) in enumerate(chips):
            cp = pltpu.make_async_remote_copy(src_ref=p_ref.at[2 * px + py], dst_ref=out_ref.at[mine],
                                              send_sem=send_sems.at[j], recv_sem=recv_sems.at[j],
                                              device_id=(px, py, c), device_id_type=MESH)
            cp.start()
            cps.append(cp)
        for j, (px, py) in enumerate(chips):
            pltpu.make_async_remote_copy(src_ref=p_ref.at[mine], dst_ref=out_ref.at[2 * px + py],
                                         send_sem=send_sems.at[j], recv_sem=recv_sems.at[j],
                                         device_id=(px, py, c), device_id_type=MESH).wait_recv()
        for cp in cps:
            cp.wait_send()

    return pl.pallas_call(
        body, name="grad_chip_exchange", in_specs=[_HBM], out_specs=_HBM,
        out_shape=jax.ShapeDtypeStruct((4, R, W), p.dtype),
        scratch_shapes=[pltpu.SemaphoreType.DMA((3,)), pltpu.SemaphoreType.DMA((3,))],
    )(p)


def _chip_sum(q):
    _, R, W = q.shape
    tr = _row_tile(R)

    def body(q_ref, o_ref):
        f = lambda t: q_ref[t].astype(F32)
        o_ref[...] = ((f(0) + f(1)) + f(2)) + f(3)

    return pl.pallas_call(body, name="grad_chip_sum", grid=(R // tr,),
                          in_specs=[pl.BlockSpec((4, tr, W), lambda r: (0, r, 0))],
                          out_specs=pl.BlockSpec((tr, W), lambda r: (r, 0)),
                          out_shape=jax.ShapeDtypeStruct((R, W), F32))(q)


def _pair_share(r):
    R, W = r.shape

    def body(r_ref, out_ref, send_sem, recv_sem):
        x, y, c = _me()
        cp = pltpu.make_async_remote_copy(src_ref=r_ref, dst_ref=out_ref, send_sem=send_sem, recv_sem=recv_sem,
                                          device_id=(x, y, 1 - c), device_id_type=MESH)
        cp.start()
        cp.wait()

    return pl.pallas_call(
        body, name="grad_pair_share", in_specs=[_HBM], out_specs=_HBM,
        out_shape=jax.ShapeDtypeStruct((R, W), r.dtype),
        scratch_shapes=[pltpu.SemaphoreType.DMA(()), pltpu.SemaphoreType.DMA(())],
    )(r)


def _allreduce_small(vec):
    R, W = vec.shape

    def body(v_ref, out_ref, buf, send_sems, recv_sems):
        x, y, c = _me()
        me = 4 * x + 2 * y + c
        buf[me] = v_ref[...]
        cps = []
        for kk in range(1, 8):
            peer = (x ^ (kk >> 2), y ^ ((kk >> 1) & 1), c ^ (kk & 1))
            cp = pltpu.make_async_remote_copy(src_ref=v_ref, dst_ref=buf.at[me], send_sem=send_sems.at[kk - 1],
                                              recv_sem=recv_sems.at[kk - 1], device_id=peer, device_id_type=MESH)
            cp.start()
            cps.append(cp)
        for kk in range(1, 8):
            peer = (x ^ (kk >> 2), y ^ ((kk >> 1) & 1), c ^ (kk & 1))
            pltpu.make_async_remote_copy(src_ref=v_ref, dst_ref=buf.at[me ^ kk], send_sem=send_sems.at[kk - 1],
                                         recv_sem=recv_sems.at[kk - 1], device_id=peer, device_id_type=MESH).wait_recv()
        for cp in cps:
            cp.wait_send()
        acc = buf[0]
        for d in range(1, 8):
            acc = acc + buf[d]
        out_ref[...] = acc

    vm = pl.BlockSpec(memory_space=pltpu.VMEM)
    return pl.pallas_call(
        body, name="allreduce_small", in_specs=[vm], out_specs=vm, out_shape=jax.ShapeDtypeStruct((R, W), F32),
        scratch_shapes=[pltpu.VMEM((8, R, W), F32), pltpu.SemaphoreType.DMA((7,)), pltpu.SemaphoreType.DMA((7,))],
    )(vec)


def _pack_flat(parts, align, dtype):
    flat = jnp.concatenate([p.reshape(-1).astype(dtype) for p in parts])
    n = flat.shape[0]
    pad = (-n) % align
    return jnp.pad(flat, (0, pad)) if pad else flat


def _unpack_flat(flat, shapes):
    out, off = [], 0
    for s in shapes:
        n = math.prod(s)
        out.append(flat[off:off + n].reshape(s))
        off += n
    return out


def _full_from_shards(name, sh):
    if name in COL_SHARDED:
        return jnp.transpose(sh, (1, 0, 2)).reshape(sh.shape[1], 4 * sh.shape[2])
    return sh.reshape(4 * sh.shape[1], sh.shape[2])


def _shards_from_full(name, full):
    if name in COL_SHARDED:
        r, cc = full.shape[0], full.shape[1] // 4
        return jnp.transpose(full.reshape(r, 4, cc), (1, 0, 2)).reshape(4, r * cc)
    return full.reshape(4, -1)


def _pad_heads(w, width):
    k = w.shape[0]
    return jnp.pad(w.reshape(k, HEADS, width), ((0, 0), (0, 0), (0, HP - width))).reshape(k, HEADS * HP)


def _unpad_heads(w, width):
    k = w.shape[0]
    return w.reshape(k, HEADS, HP)[:, :, :width].reshape(k, HEADS * width)


def _blockdiag(t):
    g, a, b = t.shape
    return jnp.einsum("gab,gk->gakb", t, jnp.eye(g, dtype=t.dtype)).reshape(g * a, g * b)


def _blockdiag_t(m, a, b):
    g = m.shape[0] // a
    return jnp.einsum("gagb->gab", m.reshape(g, a, g, b))


def _local_step(x, mem, positions, target, W, sp):
    T = x.shape[0]
    row = lambda v: v.reshape(1, -1)

    ab_re, ab_im, bb_re, bb_im = _s5_discretize(sp["s5_lam_re"], sp["s5_lam_im"], sp["s5_log_dt"], sp["s5_b_re"], sp["s5_b_im"])
    ar, ai = ab_re.reshape(1, S5_N), ab_im.reshape(1, S5_N)
    tc = min(SCAN_CHUNK, T)
    pr, pi = _cpow_table(ar, ai, tc)
    apow = jnp.concatenate([pr, pi], axis=1)
    apow_rev = jnp.concatenate([pr[::-1], -pi[::-1]], axis=1)
    nst = int(math.log2(tc))
    idx = jnp.array([(1 << k) - 1 for k in range(nst)])
    astep = jnp.concatenate([pr[idx], pi[idx]], axis=1)
    astep_conj = jnp.concatenate([pr[idx], -pi[idx]], axis=1)
    padr = (-nst) % 8
    if padr:
        astep = jnp.pad(astep, ((0, padr), (0, 0)))
        astep_conj = jnp.pad(astep_conj, ((0, padr), (0, 0)))
    b_st = jnp.concatenate([_blockdiag(jnp.swapaxes(bb_re, 1, 2)), _blockdiag(jnp.swapaxes(bb_im, 1, 2))], axis=1)
    c_st = jnp.concatenate([_blockdiag(jnp.swapaxes(sp["s5_c_re"], 1, 2)),
                            -_blockdiag(jnp.swapaxes(sp["s5_c_im"], 1, 2))], axis=0)
    inv = ROPE_THETA ** (-jnp.arange(0, ROPE, 2, dtype=F32) / ROPE)
    ang = positions.astype(F32)[:, None] * inv
    cs, sn = jnp.cos(ang), jnp.sin(ang)
    one, zero = jnp.ones((T, NOPE), F32), jnp.zeros((T, NOPE), F32)
    z16, tail1, tail0 = jnp.zeros((T, ROPE // 2), F32), jnp.ones((T, HP - QK), F32), jnp.zeros((T, HP - QK), F32)
    t_cos = jnp.concatenate([one, cs, cs, tail1], axis=1)
    t_s1 = jnp.concatenate([zero, z16, sn, tail0], axis=1)
    t_s2 = jnp.concatenate([zero, -sn, z16, tail0], axis=1)

    w_in = W["w_in"]
    w_in_re = jnp.concatenate([w_in[:, 800:], w_in[:, :768], w_in[:, 768:800],
                               jnp.zeros((D_MODEL, HP - ROPE), w_in.dtype)], axis=1)
    w_uq_p = _pad_heads(W["w_uq"], QK)
    wkv = W["w_ukv"].reshape(S5_W, HEADS, NOPE + VD)
    w_uk_p = _pad_heads(wkv[:, :, :NOPE].reshape(S5_W, HEADS * NOPE), NOPE)
    w_uv_p = _pad_heads(wkv[:, :, NOPE:].reshape(S5_W, HEADS * VD), VD)
    w_oa_p = jnp.pad(W["w_oa"].reshape(HEADS, VD, D_MODEL), ((0, 0), (0, HP - VD), (0, 0))).reshape(HEADS * HP, D_MODEL)

    g_in, b_in = row(sp["ln_in_g"]), row(sp["ln_in_b"])
    g1, b1, g2, b2, g3, b3 = (row(sp[k]) for k in ("ln1_g", "ln1_b", "ln2_g", "ln2_b", "ln3_g", "ln3_b"))
    gq, gkv, d_skip = row(sp["q_norm_g"]), row(sp["kv_norm_g"]), row(sp["s5_d"])

    (h0,) = _rowwise(lambda a, g, b: (_ln(a, g, b),), "ln_in_fwd", [_whole(x)], [g_in, b_in], [(D_MODEL, F32)])
    z = _mm(h0, w_in_re, "in_proj", tm=512, tn=Z_W)
    bu = _mm(z, b_st, "s5_bu", a_off=Z_U)
    hs = _s5_scan_fwd(bu, apow, astep, "s5_scan_fwd")
    yc = _mm(hs, c_st, "s5_c")
    (yg,) = _rowwise(lambda a, u, d: (_gelu_skip(a, u, d),), "s5_gelu_fwd", [_whole(yc), (z, Z_U, S5_W)], [d_skip], [(S5_W, BF16)])
    y2 = _mm(yg, W["w_glu"], "glu_proj", tn=2048)
    cqn, ckvn = _rowwise(lambda a, b, ga, gb: (_rms(a, ga), _rms(b, gb)), "mla_norm_fwd",
                         [(z, Z_CQ, S5_W), (z, Z_CKV, S5_W)], [gq, gkv], [(S5_W, BF16)] * 2)
    q_raw = _mm(cqn, w_uq_p, "mla_uq")
    k_raw = _mm(ckvn, w_uk_p, "mla_uk")
    v = _mm(ckvn, w_uv_p, "mla_uv", out_dtype=BF16)

    def rope_fwd(qr, kr_raw, krz, tc_, t1, t2):
        q = _rope(qr, _tile8(tc_), _tile8(t1), _tile8(t2))
        kr = _rope(pltpu.roll(krz, NOPE, 1), tc_, t1, t2)
        return q, kr_raw + _tile8(kr)

    q, k = _rowwise(rope_fwd, "mla_rope_fwd", [_whole(q_raw), _whole(k_raw), (z, Z_KR, HP), _whole(t_cos), _whole(t_s1), _whole(t_s2)],
                    [], [(HEADS * HP, BF16)] * 2)
    o, lse = _flash_fwd(q, k, v, "mla_attn_fwd")
    a_out = _mm(o, w_oa_p, "mla_oa")
    (mixin,) = _rowwise(lambda *a: (_gate_mix(*a),), "gate_mix_fwd",
                        [(z, Z_GS, D_MODEL), (z, Z_GA, D_MODEL), (y2, 0, D_MODEL), (y2, D_MODEL, D_MODEL), _whole(a_out)],
                        [], [(D_MODEL, BF16)])
    mix = _mm(mixin, W["w_o"], "mix_o")
    (h1,) = _rowwise(lambda h, r, g, b: (_ln_res(h, r, g, b),), "ln1_fwd", [_whole(h0), _whole(mix)], [g1, b1], [(D_MODEL, F32)])
    xq = _mm(h1, W["w_xq"], "x_q", out_dtype=BF16)
    xk = _mm(mem, W["w_xk"], "x_k", out_dtype=BF16)
    xv = _mm(mem, W["w_xv"], "x_v", out_dtype=BF16)
    (xo,) = _rowwise(_xattn_fwd_fn, "xattn_fwd", [_whole(xq)], [xk, xv], [(D_MODEL, BF16)])
    xa = _mm(xo, W["w_xo"], "x_o")
    (h2,) = _rowwise(lambda h, r, g, b: (_ln_res(h, r, g, b),), "ln2_fwd", [_whole(h1), _whole(xa)], [g2, b2], [(D_MODEL, F32)])
    a_up = _mm(h2, W["w_up"], "mlp_up")
    ff = _mm(a_up, W["w_down"], "mlp_down", act=_relu2)

    def loss_fn(h, r, tgt, g, b):
        def f(r_, g_, b_):
            e = _ln_res(h, r_, g_, b_) - tgt
            return 0.5 * jnp.sum(jnp.mean(e * e, axis=-1))
        lv, (dr_, dg_, db_) = jax.value_and_grad(f, argnums=(0, 1, 2))(r, g, b)
        return dr_, dg_, db_, jnp.broadcast_to(lv, (1, 128))

    dr3, dg3, db3, lossv = _rowwise(loss_fn, "ln3_loss_bwd", [_whole(h2), _whole(ff), _whole(target)], [g3, b3],
                                    [(D_MODEL, F32)], [(1, D_MODEL), (1, D_MODEL), (1, 128)])
    gW, gs = {}, {"ln3_g": dg3, "ln3_b": db3}

    da = _mm(dr3, W["w_down"], "mlp_down_bwd_a", "nt", epi=lambda acc, a: acc * (2.0 * jnp.maximum(a, 0.0)), extras=(a_up,),
             out_dtype=BF16)
    gW["w_down"] = _mm(a_up, dr3, "mlp_down_bwd_w", "tn", m=4 * D_MODEL, n=D_MODEL, act=_relu2)
    gW["w_up"] = _mm(h2, da, "mlp_up_bwd_w", "tn", m=D_MODEL, n=4 * D_MODEL)
    dh2 = _mm(da, W["w_up"], "mlp_up_bwd_a", "nt", epi=lambda acc, e: acc + DN_ALPHA * e, extras=(dr3,))

    def ln_res_bwd(h, r, dy, g, b):
        _, vjp = jax.vjp(_ln_res, h, r, g, b)
        _, dr_, dg_, db_ = vjp(dy)
        return dr_, dg_, db_

    dr2, gs["ln2_g"], gs["ln2_b"] = _rowwise(ln_res_bwd, "ln2_bwd", [_whole(h1), _whole(xa), _whole(dh2)], [g2, b2],
                                             [(D_MODEL, F32)], [(1, D_MODEL)] * 2)
    dxo = _mm(dr2, W["w_xo"], "x_o_bwd_a", "nt", out_dtype=BF16)
    gW["w_xo"] = _mm(xo, dr2, "x_o_bwd_w", "tn", m=D_MODEL, n=D_MODEL)
    dxq, dxk, dxv = _rowwise(_xattn_bwd_fn, "xattn_bwd", [_whole(xq), _whole(dxo)], [xk, xv], [(D_MODEL, BF16)],
                             [(xk.shape[0], D_MODEL)] * 2)
    gW["w_xq"] = _mm(h1, dxq, "x_q_bwd_w", "tn", m=D_MODEL, n=D_MODEL)
    gW["w_xk"] = _mm(mem, dxk, "x_k_bwd_w", "tn", m=D_MODEL, n=D_MODEL)
    gW["w_xv"] = _mm(mem, dxv, "x_v_bwd_w", "tn", m=D_MODEL, n=D_MODEL)
    dh1 = _mm(dxq, W["w_xq"], "x_q_bwd_a", "nt", epi=lambda acc, e: acc + DN_ALPHA * e, extras=(dr2,))

    dr1, gs["ln1_g"], gs["ln1_b"] = _rowwise(ln_res_bwd, "ln1_bwd", [_whole(h0), _whole(mix), _whole(dh1)], [g1, b1],
                                             [(D_MODEL, F32)], [(1, D_MODEL)] * 2)
    dmixin = _mm(dr1, W["w_o"], "mix_o_bwd_a", "nt")
    gW["w_o"] = _mm(mixin, dr1, "mix_o_bwd_w", "tn", m=D_MODEL, n=D_MODEL)

    def gate_mix_bwd(gs_, ga_, ya, yb, ao, dy):
        _, vjp = jax.vjp(_gate_mix, gs_, ga_, ya, yb, ao)
        return vjp(dy)

    dgs, dga, dy2a, dy2b, da_out = _rowwise(
        gate_mix_bwd, "gate_mix_bwd",
        [(z, Z_GS, D_MODEL), (z, Z_GA, D_MODEL), (y2, 0, D_MODEL), (y2, D_MODEL, D_MODEL), _whole(a_out), _whole(dmixin)],
        [], [(D_MODEL, BF16)] * 5)

    do = _mm(da_out, w_oa_p, "mla_oa_bwd_a", "nt")
    g_oa_p = _mm(o, da_out, "mla_oa_bwd_w", "tn", m=HEADS * HP, n=D_MODEL)
    gW["w_oa"] = g_oa_p.reshape(HEADS, HP, D_MODEL)[:, :VD].reshape(HEADS * VD, D_MODEL)
    delta, do16 = _rowwise(_attn_delta_fn, "mla_attn_delta", [_whole(do), _whole(o)], [], [(HEADS, F32), (HEADS * HP, BF16)])
    dq, dk, dv = _flash_bwd(q, k, v, do16, lse.reshape(HEADS, 1, T), delta.T.reshape(HEADS, 1, T), "mla_attn_bwd")

    def rope_bwd(dq_, dk_, tc_, t1, t2):
        dqr = _rope_t(dq_, _tile8(tc_), _tile8(t1), _tile8(t2))
        dkr = dk_[:, 0:HP]
        for hh in range(1, HEADS):
            dkr = dkr + dk_[:, hh * HP:(hh + 1) * HP]
        dkr = _rope_t(jnp.where(_rope_lanes(dkr.shape), dkr, 0.0), tc_, t1, t2)
        dkr = pltpu.roll(dkr, NOPE, 1)
        lane = lax.broadcasted_iota(jnp.int32, dkr.shape, 1)
        return dqr, jnp.where(lane < ROPE, dkr, 0.0)

    dq_raw, dkr = _rowwise(rope_bwd, "mla_rope_bwd", [_whole(dq), _whole(dk), _whole(t_cos), _whole(t_s1), _whole(t_s2)], [],
                           [(HEADS * HP, BF16), (HP, BF16)])
    dcqn = _mm(dq_raw, w_uq_p, "mla_uq_bwd_a", "nt")
    gW["w_uq"] = _unpad_heads(_mm(cqn, dq_raw, "mla_uq_bwd_w", "tn", m=S5_W, n=HEADS * HP), QK)
    dckvn_k = _mm(dk, w_uk_p, "mla_uk_bwd_a", "nt")
    dckvn = _mm(dv, w_uv_p, "mla_uv_bwd_a", "nt", epi=lambda acc, e: acc + e, extras=(dckvn_k,))
    g_uk = _unpad_heads(_mm(ckvn, dk, "mla_uk_bwd_w", "tn", m=S5_W, n=HEADS * HP), NOPE).reshape(S5_W, HEADS, NOPE)
    g_uv = _unpad_heads(_mm(ckvn, dv, "mla_uv_bwd_w", "tn", m=S5_W, n=HEADS * HP), VD).reshape(S5_W, HEADS, VD)
    gW["w_ukv"] = jnp.concatenate([g_uk, g_uv], axis=2).reshape(S5_W, HEADS * (NOPE + VD))

    def norm_bwd(a, b, da_, db_, ga, gb):
        _, vja = jax.vjp(_rms, a, ga)
        _, vjb = jax.vjp(_rms, b, gb)
        dxa, dga_ = vja(da_)
        dxb, dgb_ = vjb(db_)
        return dxa, dxb, dga_, dgb_

    dcq, dckv, gs["q_norm_g"], gs["kv_norm_g"] = _rowwise(
        norm_bwd, "mla_norm_bwd", [(z, Z_CQ, S5_W), (z, Z_CKV, S5_W), _whole(dcqn), _whole(dckvn)], [gq, gkv],
        [(S5_W, BF16)] * 2, [(1, S5_W)] * 2)

    dy2 = jnp.concatenate([dy2a, dy2b], axis=1)
    dyg = _mm(dy2, W["w_glu"], "glu_bwd_a", "nt")
    gW["w_glu"] = _mm(yg, dy2, "glu_bwd_w", "tn", m=S5_W, n=2 * D_MODEL)

    def gelu_bwd(a, u, dy, d):
        _, vjp = jax.vjp(_gelu_skip, a, u, d)
        return vjp(dy)

    dyc, du_skip, gs["s5_d"] = _rowwise(gelu_bwd, "s5_gelu_bwd", [_whole(yc), (z, Z_U, S5_W), _whole(dyg)], [d_skip],
                                        [(S5_W, BF16), (S5_W, F32)], [(1, S5_W)])
    g_cst = _mm(hs, dyc, "s5_c_bwd_w", "tn", m=2 * S5_N, n=S5_W)
    dhs = _mm(dyc, c_st, "s5_c_bwd_a", "nt")
    lam, d_ab = _s5_scan_bwd(dhs, hs, apow_rev, astep_conj, "s5_scan_bwd")
    g_bst = _mm(z, lam, "s5_bu_bwd_w", "tn", a_off=Z_U, m=S5_W, n=2 * S5_N)
    du = _mm(lam, b_st, "s5_bu_bwd_a", "nt", epi=lambda acc, e: acc + e, extras=(du_skip,), out_dtype=BF16)
    gs["s5_ab"] = d_ab
    gs["s5_bb_re"] = jnp.swapaxes(_blockdiag_t(g_bst[:, :S5_N], S5_H, S5_P), 1, 2)
    gs["s5_bb_im"] = jnp.swapaxes(_blockdiag_t(g_bst[:, S5_N:], S5_H, S5_P), 1, 2)
    gs["s5_c_re"] = jnp.swapaxes(_blockdiag_t(g_cst[:S5_N], S5_P, S5_H), 1, 2)
    gs["s5_c_im"] = -jnp.swapaxes(_blockdiag_t(g_cst[S5_N:], S5_P, S5_H), 1, 2)

    dz = jnp.concatenate([dgs, dga, du, dcq, dckv, dkr], axis=1)
    g_in_re = _mm(h0, dz, "in_proj_bwd_w", "tn", m=D_MODEL, n=Z_W, tm=512, tn=Z_W)
    gW["w_in"] = jnp.concatenate([g_in_re[:, Z_U:Z_KR], g_in_re[:, Z_KR:Z_KR + ROPE], g_in_re[:, :Z_U]], axis=1)
    dh0 = _mm(dz, w_in_re, "in_proj_bwd_a", "nt", tm=512, tk=Z_W, epi=lambda acc, e: acc + DN_ALPHA * e, extras=(dr1,))

    def ln_bwd(a, dy, g, b):
        _, vjp = jax.vjp(_ln, a, g, b)
        return vjp(dy)

    dx, gs["ln_in_g"], gs["ln_in_b"] = _rowwise(ln_bwd, "ln_in_bwd", [_whole(x), _whole(dh0)], [g_in, b_in],
                                                [(D_MODEL, F32)], [(1, D_MODEL)] * 2)
    return lossv[:, :1], dx, gW, gs


_RAW_SMALL = (("loss", (1, 1)), ("ln_in_g", (1, D_MODEL)), ("ln_in_b", (1, D_MODEL)), ("ln1_g", (1, D_MODEL)),
              ("ln1_b", (1, D_MODEL)), ("ln2_g", (1, D_MODEL)), ("ln2_b", (1, D_MODEL)), ("ln3_g", (1, D_MODEL)),
              ("ln3_b", (1, D_MODEL)), ("q_norm_g", (1, S5_W)), ("kv_norm_g", (1, S5_W)), ("s5_d", (1, S5_W)),
              ("s5_ab", (1, 2 * S5_N)), ("s5_bb_re", (S5_G, S5_P, S5_H)), ("s5_bb_im", (S5_G, S5_P, S5_H)),
              ("s5_c_re", (S5_G, S5_H, S5_P)), ("s5_c_im", (S5_G, S5_H, S5_P)))


def kernel(x, mem, positions, ln_in_g, ln_in_b, w_in, s5_lam_re, s5_lam_im, s5_log_dt, s5_b_re, s5_b_im, s5_c_re, s5_c_im, s5_d, w_glu, q_norm_g, w_uq, kv_norm_g, w_ukv, w_oa, w_o, ln1_g, ln1_b, w_xq, w_xk, w_xv, w_xo, ln2_g, ln2_b, w_up, w_down, ln3_g, ln3_b, loss_target, m_ln_in_g, m_ln_in_b, m_w_in, m_s5_lam_re, m_s5_lam_im, m_s5_log_dt, m_s5_b_re, m_s5_b_im, m_s5_c_re, m_s5_c_im, m_s5_d, m_w_glu, m_q_norm_g, m_w_uq, m_kv_norm_g, m_w_ukv, m_w_oa, m_w_o, m_ln1_g, m_ln1_b, m_w_xq, m_w_xk, m_w_xv, m_w_xo, m_ln2_g, m_ln2_b, m_w_up, m_w_down, m_ln3_g, m_ln3_b, v_ln_in_g, v_ln_in_b, v_w_in, v_s5_lam_re, v_s5_lam_im, v_s5_log_dt, v_s5_b_re, v_s5_b_im, v_s5_c_re, v_s5_c_im, v_s5_d, v_w_glu, v_q_norm_g, v_w_uq, v_kv_norm_g, v_w_ukv, v_w_oa, v_w_o, v_ln1_g, v_ln1_b, v_w_xq, v_w_xk, v_w_xv, v_w_xo, v_ln2_g, v_ln2_b, v_w_up, v_w_down, v_ln3_g, v_ln3_b):
    a = dict(locals())
    wts = {n: a[n] for n in WEIGHTS}
    ms = {n: a["m_" + n] for n in WEIGHTS}
    vs = {n: a["v_" + n] for n in WEIGHTS}
    cidx = lax.axis_index("c").astype(jnp.int32).reshape(1)

    shard2d = {n: wts[n].reshape(wts[n].shape[-2], wts[n].shape[-1]) for n in BIG}
    nrow = {n: shard2d[n].size // 1024 for n in BIG}
    pad_rows = (-sum(nrow.values())) % PACK_ROWS
    rows_half = (sum(nrow.values()) + pad_rows) // 2
    mine = 2 * lax.axis_index("x") + lax.axis_index("y")
    packed = jnp.concatenate([shard2d[n].astype(BF16).reshape(nrow[n], 1024) for n in BIG]
                             + [jnp.zeros((pad_rows, 1024), BF16)], axis=0).reshape(1, 2, rows_half, 1024)
    gathered = lax.dynamic_update_slice(_gather_weights(packed[0]), packed, (mine, 0, 0, 0))
    gathered = gathered.reshape(4, 2 * rows_half, 1024)
    W, off = {}, 0
    for n in BIG:
        r, cc = shard2d[n].shape
        piece = lax.optimization_barrier(gathered[:, off:off + nrow[n]])
        W[n] = _full_from_shards(n, piece.reshape(4, r, cc))
        off += nrow[n]

    sp = {n: wts[n] for n in SMALL}
    sp_local = {n: (sp[n][0] if sp[n].ndim > 1 else sp[n]) for n in SMALL}
    lossv, dx, gW, gs = _local_step(x[0], mem[0], positions[0], loss_target[0], W, sp_local)

    gflat = jnp.concatenate([_shards_from_full(n, gW[n]).reshape(4, nrow[n], 1024) for n in BIG]
                            + [jnp.zeros((4, pad_rows, 1024), F32)], axis=1).reshape(4, 2, rows_half, 1024)
    recv = _pair_exchange(gflat)
    psum = _pair_add(gflat, recv, cidx)
    own = lax.dynamic_slice(psum, (mine, 0, 0), (1, rows_half, 1024))
    slots = lax.dynamic_update_slice(_chip_exchange(psum), own, (mine, 0, 0))
    red = _chip_sum(slots)
    other = _pair_share(red)
    is_mine = (jnp.arange(2) == lax.axis_index("c"))[:, None, None]
    gsh = jnp.where(is_mine, red[None], other[None]).reshape(2 * rows_half, 1024)
    g_out, off = {}, 0
    for n in BIG:
        g_out[n] = lax.optimization_barrier(gsh[off:off + nrow[n]]).reshape(shard2d[n].shape)
        off += nrow[n]

    gs["loss"] = lossv
    raw = _pack_flat([gs[n].reshape(s) for n, s in _RAW_SMALL], 8 * 1024, F32)
    raw = _allreduce_small(raw.reshape(-1, 1024)).reshape(-1)
    rs = dict(zip([n for n, _ in _RAW_SMALL], _unpack_flat(raw, [s for _, s in _RAW_SMALL])))
    loss = rs["loss"].reshape(())
    _, disc_vjp = jax.vjp(_s5_discretize, sp_local["s5_lam_re"], sp_local["s5_lam_im"], sp_local["s5_log_dt"],
                          sp_local["s5_b_re"], sp_local["s5_b_im"])
    d_ab = rs["s5_ab"].reshape(2, S5_G, S5_P)
    g_lre, g_lim, g_ldt, g_bre, g_bim = disc_vjp((d_ab[0], d_ab[1], rs["s5_bb_re"], rs["s5_bb_im"]))
    small_g = {"s5_lam_re": g_lre, "s5_lam_im": g_lim, "s5_log_dt": g_ldt, "s5_b_re": g_bre, "s5_b_im": g_bim,
               "s5_c_re": rs["s5_c_re"], "s5_c_im": rs["s5_c_im"]}
    for n in ("ln_in_g", "ln_in_b", "ln1_g", "ln1_b", "ln2_g", "ln2_b", "ln3_g", "ln3_b", "q_norm_g", "kv_norm_g", "s5_d"):
        small_g[n] = rs[n]

    grads, deltas, new_m, new_v = {}, {}, {}, {}
    for n in BIG:
        d_, m_, v_ = _adamw(shard2d[n], g_out[n], ms[n].reshape(shard2d[n].shape), vs[n].reshape(shard2d[n].shape), "adamw_" + n)
        grads[n] = g_out[n].reshape(wts[n].shape)
        deltas[n], new_m[n], new_v[n] = (t.reshape(wts[n].shape) for t in (d_, m_, v_))
    packs = [_pack_flat([t[n] for n in SMALL], 8 * 1024, F32).reshape(-1, 1024)
             for t in (wts, {n: small_g[n].reshape(wts[n].shape) for n in SMALL}, ms, vs)]
    sd, sm, sv = _adamw(*packs, "adamw_small")
    shapes = [wts[n].shape for n in SMALL]
    for n, d_, m_, v_ in zip(SMALL, _unpack_flat(sd.reshape(-1), shapes), _unpack_flat(sm.reshape(-1), shapes),
                             _unpack_flat(sv.reshape(-1), shapes)):
        grads[n] = small_g[n].reshape(wts[n].shape)
        deltas[n], new_m[n], new_v[n] = d_, m_, v_

    return (loss, dx[None], *[grads[n] for n in WEIGHTS], *[deltas[n] for n in WEIGHTS],
            *[new_m[n] for n in WEIGHTS], *[new_v[n] for n in WEIGHTS])
```

```python
import functools
import math

import jax
import jax.numpy as jnp
from jax import lax
from jax.experimental import pallas as pl
from jax.experimental.pallas import tpu as pltpu

F32 = jnp.float32
BF16 = jnp.bfloat16
MESH = pl.DeviceIdType.MESH

D_MODEL = 1024
S5_W = 256
S5_G = 16
S5_H = 16
S5_P = 64
S5_N = S5_G * S5_P
S5_MAX_RE = -1e-4
HEADS = 8
NOPE = 64
ROPE = 32
QK = NOPE + ROPE
VD = 64
HP = 128
XH = 4
XD = 256
LN_EPS = 1e-5
RMS_EPS = 1e-6
NEG_INF = -1e30
DN_ALPHA = 2.0 ** 0.25
ROPE_THETA = 10000.0
ADAM_LR, ADAM_B1, ADAM_B2, ADAM_EPS, ADAM_WD, ADAM_STEP = 0.001, 0.9, 0.999, 1e-08, 0.01, 10

Z_GS, Z_GA, Z_U, Z_CQ, Z_CKV, Z_KR, Z_W = 0, 1024, 2048, 2304, 2560, 2816, 2944
SCAN_CHUNK = 256

BIG = ("w_in", "w_glu", "w_uq", "w_ukv", "w_oa", "w_up", "w_o", "w_xq", "w_xk", "w_xv", "w_xo", "w_down")
COL_SHARDED = ("w_in", "w_glu", "w_uq", "w_ukv", "w_oa", "w_up")
SMALL = ("ln_in_g", "ln_in_b", "s5_lam_re", "s5_lam_im", "s5_log_dt", "s5_b_re", "s5_b_im", "s5_c_re", "s5_c_im",
         "s5_d", "q_norm_g", "kv_norm_g", "ln1_g", "ln1_b", "ln2_g", "ln2_b", "ln3_g", "ln3_b")
WEIGHTS = ("ln_in_g", "ln_in_b", "w_in", "s5_lam_re", "s5_lam_im", "s5_log_dt", "s5_b_re", "s5_b_im", "s5_c_re",
           "s5_c_im", "s5_d", "w_glu", "q_norm_g", "w_uq", "kv_norm_g", "w_ukv", "w_oa", "w_o", "ln1_g", "ln1_b",
           "w_xq", "w_xk", "w_xv", "w_xo", "ln2_g", "ln2_b", "w_up", "w_down", "ln3_g", "ln3_b")
PACK_ROWS = 2 * 16


def _row_tile(n, cap=512):
    best = n
    for t in range(8, min(n, cap) + 1, 8):
        if n % t == 0:
            best = t
    return best


def _pick(n, cap):
    best = None
    for t in range(128, min(n, cap) + 1, 128):
        if n % t == 0:
            best = t
    return best if best is not None and (best >= 512 or best == n) else n


def _mm(a, b, name, mode="nn", *, a_off=0, b_off=0, m=None, n=None, act=None, epi=None, extras=(),
        out_dtype=F32, tm=1024, tn=1024, tk=None):
    if mode == "nn":
        M, (K, N) = a.shape[0], b.shape
    elif mode == "nt":
        M, (N, K) = a.shape[0], b.shape
    else:
        K, M, N = a.shape[0], m, n
    if mode == "tn":
        tm, tn, tk = _pick(M, tm), _pick(N, tn), min(tk or 512, K)
    else:
        tm, tn, tk = min(tm, M), _pick(N, tn), _pick(K, tk or 1024)
    assert M % tm == 0 and N % tn == 0 and K % tk == 0, (name, M, N, K, tm, tn, tk)
    nk = K // tk
    if mode == "tn":
        assert a_off % tm == 0 and b_off % tn == 0
        ao, bo = a_off // tm, b_off // tn
        a_spec = pl.BlockSpec((tk, tm), lambda i, j, k: (k, i + ao))
        b_spec = pl.BlockSpec((tk, tn), lambda i, j, k: (k, j + bo))
        dims = (((0,), (0,)), ((), ()))
    else:
        assert a_off % tk == 0
        ao = a_off // tk
        a_spec = pl.BlockSpec((tm, tk), lambda i, j, k: (i, k + ao))
        if mode == "nn":
            b_spec = pl.BlockSpec((tk, tn), lambda i, j, k: (k, j))
            dims = (((1,), (0,)), ((), ()))
        else:
            b_spec = pl.BlockSpec((tn, tk), lambda i, j, k: (j, k))
            dims = (((1,), (1,)), ((), ()))
    e_spec = pl.BlockSpec((tm, tn), lambda i, j, k: (i, j))
    n_extra = len(extras)

    def body(*refs):
        a_ref, b_ref = refs[0], refs[1]
        e_refs = refs[2:2 + n_extra]
        o_ref = refs[2 + n_extra]
        av = a_ref[...]
        if act is not None:
            av = act(av.astype(F32))
        p = lax.dot_general(av.astype(BF16), b_ref[...].astype(BF16), dims, preferred_element_type=F32)

        def finish(r):
            if epi is not None:
                r = epi(r, *[e[...] for e in e_refs])
            o_ref[...] = r.astype(o_ref.dtype)

        if nk == 1:
            finish(p)
        else:
            acc = refs[3 + n_extra]
            k = pl.program_id(2)

            @pl.when(k == 0)
            def _():
                acc[...] = p

            @pl.when(k > 0)
            def _():
                acc[...] += p

            @pl.when(k == nk - 1)
            def _():
                finish(acc[...])

    return pl.pallas_call(
        body, name=name, grid=(M // tm, N // tn, nk),
        in_specs=[a_spec, b_spec] + [e_spec] * n_extra, out_specs=e_spec,
        out_shape=jax.ShapeDtypeStruct((M, N), out_dtype),
        scratch_shapes=[pltpu.VMEM((tm, tn), F32)] if nk > 1 else [],
        compiler_params=pltpu.CompilerParams(dimension_semantics=("parallel", "parallel", "arbitrary")),
    )(a, b, *extras)


def _rowwise(fn, name, rows, pars, outs, accs=(), tm=256):
    T = rows[0][0].shape[0]
    tm = min(tm, T)
    assert T % tm == 0
    n_in, n_out = len(rows) + len(pars), len(outs)
    in_specs = []
    for arr, off, w in rows:
        assert off % w == 0 and arr.shape[0] == T, name
        in_specs.append(pl.BlockSpec((tm, w), functools.partial(lambda i, o: (i, o), o=off // w)))
    for p in pars:
        in_specs.append(pl.BlockSpec(p.shape, lambda i: (0, 0)))
    out_specs = [pl.BlockSpec((tm, w), lambda i: (i, 0)) for w, _ in outs]
    out_specs += [pl.BlockSpec(s, lambda i: (0, 0)) for s in accs]
    out_shape = [jax.ShapeDtypeStruct((T, w), dt) for w, dt in outs] + [jax.ShapeDtypeStruct(s, F32) for s in accs]

    def body(*refs):
        res = fn(*[r[...] for r in refs[:n_in]])
        o_refs = refs[n_in:]
        for r, v in zip(o_refs[:n_out], res[:n_out]):
            r[...] = v.astype(r.dtype)
        if accs:
            i = pl.program_id(0)

            @pl.when(i == 0)
            def _():
                for r in o_refs[n_out:]:
                    r[...] = jnp.zeros_like(r)

            for r, v in zip(o_refs[n_out:], res[n_out:]):
                r[...] += v

    res = pl.pallas_call(
        body, name=name, grid=(T // tm,), in_specs=in_specs, out_specs=out_specs, out_shape=out_shape,
        compiler_params=pltpu.CompilerParams(dimension_semantics=("arbitrary",)),
    )(*[r[0] for r in rows], *pars)
    return res


def _whole(arr):
    return (arr, 0, arr.shape[1])


def _ln(x, g, b):
    mu = jnp.mean(x, axis=-1, keepdims=True)
    xc = x - mu
    var = jnp.mean(xc * xc, axis=-1, keepdims=True)
    return xc * lax.rsqrt(var + LN_EPS) * g + b


def _ln_res(h, r, g, b):
    return _ln(DN_ALPHA * h + r, g, b)


def _rms(x, g):
    return x * lax.rsqrt(jnp.mean(x * x, axis=-1, keepdims=True) + RMS_EPS) * g


def _gelu_skip(yc, u, d):
    y = yc + d * u
    return 0.5 * y * (1.0 + lax.erf(y * (1.0 / math.sqrt(2.0))))


def _gate_mix(gs, ga, y2a, y2b, aout):
    return jax.nn.sigmoid(gs) * (y2a * jax.nn.sigmoid(y2b)) + jax.nn.sigmoid(ga) * aout


def _relu2(a):
    r = jnp.maximum(a, 0.0)
    return r * r


def _tile8(t):
    return jnp.concatenate([t] * HEADS, axis=1)


def _rope(x, cos, s1, s2):
    w = x.shape[1]
    return x * cos + pltpu.roll(x, ROPE // 2, 1) * s1 + pltpu.roll(x, w - ROPE // 2, 1) * s2


def _rope_t(dy, cos, s1, s2):
    w = dy.shape[1]
    return dy * cos + pltpu.roll(dy * s1, w - ROPE // 2, 1) + pltpu.roll(dy * s2, ROPE // 2, 1)


def _rope_lanes(shape):
    lane = lax.broadcasted_iota(jnp.int32, shape, 1) % HP
    return (lane >= NOPE) & (lane < QK)


ATT_HB = 2
ATT_SCALE = QK ** -0.5
LOG2E = 1.4426950408889634
LN2 = 0.6931471805599453
_NT = (((1,), (1,)), ((), ()))
_TN = (((0,), (0,)), ((), ()))


def _tri_tables(n, by_row):
    if by_row:
        pairs = [(i, j) for i in range(n) for j in range(i + 1)]
    else:
        pairs = [(i, j) for j in range(n) for i in range(j, n)]
    return jnp.array([p[0] for p in pairs], jnp.int32), jnp.array([p[1] for p in pairs], jnp.int32)


def _flash_fwd(q, k, v, name, bq=512):
    T = q.shape[0]
    bq = min(bq, T)
    nq = T // bq
    hb = ATT_HB

    i_tab, j_tab = _tri_tables(nq, by_row=True)

    def body(it_ref, jt_ref, q_ref, k_ref, v_ref, o_ref, lse_ref, m_sc, l_sc, acc_sc):
        t = pl.program_id(1)
        i, j = it_ref[t], jt_ref[t]

        @pl.when(j == 0)
        def _():
            m_sc[...] = jnp.full_like(m_sc, NEG_INF)
            l_sc[...] = jnp.zeros_like(l_sc)
            acc_sc[...] = jnp.zeros_like(acc_sc)

        def step(masked):
            for hh in range(hb):
                sl = slice(hh * HP, (hh + 1) * HP)
                s = lax.dot_general(q_ref[:, sl], k_ref[:, sl], _NT, preferred_element_type=F32) * (ATT_SCALE * LOG2E)
                if masked:
                    row = lax.broadcasted_iota(jnp.int32, (bq, bq), 0)
                    col = lax.broadcasted_iota(jnp.int32, (bq, bq), 1)
                    s = jnp.where(col <= row, s, NEG_INF)
                m_prev = m_sc[hh]
                m_new = jnp.maximum(m_prev, jnp.max(s, axis=1, keepdims=True))
                alpha = jnp.exp2(m_prev - m_new)
                p = jnp.exp2(s - jnp.concatenate([m_new] * (bq // HP), axis=1))
                l_sc[hh] = alpha * l_sc[hh] + jnp.sum(p, axis=1, keepdims=True)
                acc_sc[hh] = alpha * acc_sc[hh] + jnp.dot(p.astype(BF16), v_ref[:, sl], preferred_element_type=F32)
                m_sc[hh] = m_new

        @pl.when(j < i)
        def _():
            step(False)

        @pl.when(j == i)
        def _():
            step(True)
            for hh in range(hb):
                o_ref[:, hh * HP:(hh + 1) * HP] = acc_sc[hh] / l_sc[hh]
                lse_ref[hh] = (m_sc[hh][:, :1] + jnp.log2(l_sc[hh][:, :1])) * LN2

    qs = pl.BlockSpec((bq, hb * HP), lambda h, t, it, jt: (it[t], h))
    ks = pl.BlockSpec((bq, hb * HP), lambda h, t, it, jt: (jt[t], h))
    gs = pltpu.PrefetchScalarGridSpec(
        num_scalar_prefetch=2, grid=(HEADS // hb, i_tab.shape[0]), in_specs=[qs, ks, ks],
        out_specs=[qs, pl.BlockSpec((hb, bq, 1), lambda h, t, it, jt: (h, it[t], 0))],
        scratch_shapes=[pltpu.VMEM((hb, bq, HP), F32)] * 3)
    return pl.pallas_call(
        body, name=name, grid_spec=gs,
        out_shape=[jax.ShapeDtypeStruct((T, HEADS * HP), F32), jax.ShapeDtypeStruct((HEADS, T, 1), F32)],
        compiler_params=pltpu.CompilerParams(dimension_semantics=("parallel", "arbitrary")),
    )(i_tab, j_tab, q, k, v)


def _attn_delta_fn(do, o):
    prod = do * o
    cols = [jnp.sum(prod[:, h * HP:(h + 1) * HP], axis=1, keepdims=True) for h in range(HEADS)]
    return jnp.concatenate(cols, axis=1), do


def _flash_bwd(q, k, v, do, lse_row, delta_row, name, bq=512):
    T = q.shape[0]
    bq = min(bq, T)
    nq = T // bq
    hb = ATT_HB

    i_tab, j_tab = _tri_tables(nq, by_row=False)
    n_blk = i_tab.shape[0]

    def body(it_ref, jt_ref, q_ref, k_ref, v_ref, do_ref, lse_ref, dl_ref, dq_ref, dk_ref, dv_ref):
        t = pl.program_id(1)
        i, j = it_ref[t], jt_ref[t]

        @pl.when(t == 0)
        def _():
            dq_ref[...] = jnp.zeros_like(dq_ref)

        @pl.when(i == j)
        def _():
            dk_ref[...] = jnp.zeros_like(dk_ref)
            dv_ref[...] = jnp.zeros_like(dv_ref)

        def step(masked):
            rows = pl.ds(pl.multiple_of(i * bq, bq), bq)
            for hh in range(hb):
                sl = slice(hh * HP, (hh + 1) * HP)
                qh, kh, doh = q_ref[:, sl], k_ref[:, sl], do_ref[:, sl]
                st = lax.dot_general(kh, qh, _NT, preferred_element_type=F32) * (ATT_SCALE * LOG2E)
                pt = jnp.exp2(st - lse_ref[hh] * LOG2E)
                if masked:
                    krow = lax.broadcasted_iota(jnp.int32, (bq, bq), 0)
                    qcol = lax.broadcasted_iota(jnp.int32, (bq, bq), 1)
                    pt = jnp.where(krow <= qcol, pt, 0.0)
                dv_ref[:, sl] += jnp.dot(pt.astype(BF16), doh, preferred_element_type=F32)
                dpt = lax.dot_general(v_ref[:, sl], doh, _NT, preferred_element_type=F32)
                dst = (pt * (dpt - dl_ref[hh])).astype(BF16)
                dk_ref[:, sl] += jnp.dot(dst, qh, preferred_element_type=F32)
                dq_ref[rows, sl] += lax.dot_general(dst, kh, _TN, preferred_element_type=F32)

        @pl.when(i > j)
        def _():
            step(False)

        @pl.when(i == j)
        def _():
            step(True)

        @pl.when(i == nq - 1)
        def _():
            dk_ref[...] *= ATT_SCALE

        @pl.when(t == n_blk - 1)
        def _():
            dq_ref[...] *= ATT_SCALE

    qs = pl.BlockSpec((bq, hb * HP), lambda h, t, it, jt: (it[t], h))
    ks = pl.BlockSpec((bq, hb * HP), lambda h, t, it, jt: (jt[t], h))
    rs = pl.BlockSpec((hb, 1, bq), lambda h, t, it, jt: (h, 0, it[t]))
    full = pl.BlockSpec((T, hb * HP), lambda h, t, it, jt: (0, h))
    gs = pltpu.PrefetchScalarGridSpec(num_scalar_prefetch=2, grid=(HEADS // hb, n_blk),
                                      in_specs=[qs, ks, ks, qs, rs, rs], out_specs=[full, ks, ks])
    return pl.pallas_call(
        body, name=name, grid_spec=gs, out_shape=[jax.ShapeDtypeStruct((T, HEADS * HP), F32)] * 3,
        compiler_params=pltpu.CompilerParams(dimension_semantics=("parallel", "arbitrary")),
    )(i_tab, j_tab, q, k, v, do, lse_row, delta_row)


def _xattn_heads(q, k, v):
    scale = XD ** -0.5
    ps = []
    for h in range(XH):
        sl = slice(h * XD, (h + 1) * XD)
        s = lax.dot_general(q[:, sl].astype(BF16), k[:, sl].astype(BF16), (((1,), (1,)), ((), ())),
                            preferred_element_type=F32) * scale
        e = jnp.exp(s - jnp.max(s, axis=1, keepdims=True))
        ps.append(e / jnp.sum(e, axis=1, keepdims=True))
    return ps


def _xattn_fwd_fn(q, k, v):
    ps = _xattn_heads(q, k, v)
    o = [jnp.dot(p.astype(BF16), v[:, h * XD:(h + 1) * XD].astype(BF16), preferred_element_type=F32)
         for h, p in enumerate(ps)]
    return (jnp.concatenate(o, axis=1),)


def _xattn_bwd_fn(q, do, k, v):
    scale = XD ** -0.5
    ps = _xattn_heads(q, k, v)
    dqs, dks, dvs = [], [], []
    tdims = (((0,), (0,)), ((), ()))
    for h, p in enumerate(ps):
        sl = slice(h * XD, (h + 1) * XD)
        doh = do[:, sl].astype(BF16)
        dvs.append(lax.dot_general(p.astype(BF16), doh, tdims, preferred_element_type=F32))
        dp = lax.dot_general(doh, v[:, sl].astype(BF16), (((1,), (1,)), ((), ())), preferred_element_type=F32)
        ds = (p * (dp - jnp.sum(dp * p, axis=1, keepdims=True)) * scale).astype(BF16)
        dqs.append(jnp.dot(ds, k[:, sl].astype(BF16), preferred_element_type=F32))
        dks.append(lax.dot_general(ds, q[:, sl].astype(BF16), tdims, preferred_element_type=F32))
    return jnp.concatenate(dqs, axis=1), jnp.concatenate(dks, axis=1), jnp.concatenate(dvs, axis=1)


def _cmul_add(xr, xi, ar, ai, sr, si):
    return xr + ar * sr - ai * si, xi + ar * si + ai * sr


def _chunk_scan(xr, xi, astep_ref, reverse):
    tc = xr.shape[0]
    rows = lax.broadcasted_iota(jnp.int32, xr.shape, 0)
    for k in range(int(math.log2(tc))):
        s = 1 << k
        ar, ai = astep_ref[k:k + 1, :S5_N], astep_ref[k:k + 1, S5_N:]
        if reverse:
            keep, sh = rows < tc - s, tc - s
        else:
            keep, sh = rows >= s, s
        sr = jnp.where(keep, pltpu.roll(xr, sh, 0), 0.0)
        si = jnp.where(keep, pltpu.roll(xi, sh, 0), 0.0)
        xr, xi = _cmul_add(xr, xi, ar, ai, sr, si)
    return xr, xi


def _s5_scan_fwd(bu, apow, astep, name):
    T = bu.shape[0]
    tc = apow.shape[0]
    assert T % tc == 0

    def body(x_ref, apow_ref, astep_ref, h_ref, carry):
        @pl.when(pl.program_id(0) == 0)
        def _():
            carry[...] = jnp.zeros_like(carry)

        xr, xi = _chunk_scan(x_ref[:, :S5_N], x_ref[:, S5_N:], astep_ref, False)
        hr, hi = _cmul_add(xr, xi, apow_ref[:, :S5_N], apow_ref[:, S5_N:], carry[0:1, :S5_N], carry[0:1, S5_N:])
        h_ref[:, :S5_N] = hr
        h_ref[:, S5_N:] = hi
        carry[0:1, :S5_N] = hr[tc - 1:tc]
        carry[0:1, S5_N:] = hi[tc - 1:tc]

    blk = pl.BlockSpec((tc, 2 * S5_N), lambda i: (i, 0))
    return pl.pallas_call(
        body, name=name, grid=(T // tc,),
        in_specs=[blk, pl.BlockSpec(apow.shape, lambda i: (0, 0)), pl.BlockSpec(astep.shape, lambda i: (0, 0))],
        out_specs=blk, out_shape=jax.ShapeDtypeStruct((T, 2 * S5_N), F32),
        scratch_shapes=[pltpu.VMEM((8, 2 * S5_N), F32)],
        compiler_params=pltpu.CompilerParams(dimension_semantics=("arbitrary",)),
    )(bu, apow, astep)


def _s5_scan_bwd(dh, h, apow_rev, astep_conj, name):
    T = dh.shape[0]
    tc = apow_rev.shape[0]
    nc = T // tc

    def body(x_ref, h_ref, hp_ref, apow_ref, astep_ref, lam_ref, da_ref, carry):
        g = pl.program_id(0)

        @pl.when(g == 0)
        def _():
            carry[...] = jnp.zeros_like(carry)
            da_ref[...] = jnp.zeros_like(da_ref)

        xr, xi = _chunk_scan(x_ref[:, :S5_N], x_ref[:, S5_N:], astep_ref, True)
        lr, li = _cmul_add(xr, xi, apow_ref[:, :S5_N], apow_ref[:, S5_N:], carry[0:1, :S5_N], carry[0:1, S5_N:])
        lam_ref[:, :S5_N] = lr.astype(lam_ref.dtype)
        lam_ref[:, S5_N:] = li.astype(lam_ref.dtype)
        carry[0:1, :S5_N] = lr[0:1]
        carry[0:1, S5_N:] = li[0:1]
        rows = lax.broadcasted_iota(jnp.int32, (tc, S5_N), 0)
        first = jnp.where(g == nc - 1, 0.0, 1.0)
        pr = jnp.where(rows == 0, hp_ref[7:8, :S5_N] * first, pltpu.roll(h_ref[:, :S5_N], 1, 0))
        pi = jnp.where(rows == 0, hp_ref[7:8, S5_N:] * first, pltpu.roll(h_ref[:, S5_N:], 1, 0))
        da_ref[0:1, :S5_N] += jnp.sum(lr * pr + li * pi, axis=0, keepdims=True)
        da_ref[0:1, S5_N:] += jnp.sum(li * pr - lr * pi, axis=0, keepdims=True)

    blk = pl.BlockSpec((tc, 2 * S5_N), lambda g: (nc - 1 - g, 0))
    prev = pl.BlockSpec((8, 2 * S5_N), lambda g: (jnp.maximum((nc - 1 - g) * (tc // 8) - 1, 0), 0))
    return pl.pallas_call(
        body, name=name, grid=(nc,),
        in_specs=[blk, blk, prev, pl.BlockSpec(apow_rev.shape, lambda g: (0, 0)),
                  pl.BlockSpec(astep_conj.shape, lambda g: (0, 0))],
        out_specs=[blk, pl.BlockSpec((1, 2 * S5_N), lambda g: (0, 0))],
        out_shape=[jax.ShapeDtypeStruct((T, 2 * S5_N), BF16), jax.ShapeDtypeStruct((1, 2 * S5_N), F32)],
        scratch_shapes=[pltpu.VMEM((8, 2 * S5_N), F32)],
        compiler_params=pltpu.CompilerParams(dimension_semantics=("arbitrary",)),
    )(dh, h, h, apow_rev, astep_conj)


def _s5_discretize(lam_re, lam_im, log_dt, b_re, b_im):
    lr = jnp.minimum(lam_re, S5_MAX_RE)
    li = lam_im
    dt = jnp.exp(log_dt)[:, None]
    mag = jnp.exp(lr * dt)
    ang = li * dt
    ab_re = mag * jnp.cos(ang)
    ab_im = mag * jnp.sin(ang)
    den = lr * lr + li * li
    nr = ab_re - 1.0
    f_re = ((nr * lr + ab_im * li) / den)[..., None]
    f_im = ((ab_im * lr - nr * li) / den)[..., None]
    return ab_re, ab_im, f_re * b_re - f_im * b_im, f_re * b_im + f_im * b_re


def _cpow_table(ar, ai, n):
    def op(e1, e2):
        return e1[0] * e2[0] - e1[1] * e2[1], e1[0] * e2[1] + e1[1] * e2[0]
    pr, pi = lax.associative_scan(op, (jnp.broadcast_to(ar, (n, S5_N)), jnp.broadcast_to(ai, (n, S5_N))), axis=0)
    return pr, pi


def _adamw_fn(w, g, m, v):
    m = ADAM_B1 * m + (1.0 - ADAM_B1) * g
    v = ADAM_B2 * v + (1.0 - ADAM_B2) * (g * g)
    m_hat = m / (1.0 - ADAM_B1 ** ADAM_STEP)
    v_hat = v / (1.0 - ADAM_B2 ** ADAM_STEP)
    delta = -ADAM_LR * (m_hat / (jnp.sqrt(v_hat) + ADAM_EPS) + ADAM_WD * w)
    return delta, m, v


def _adamw(w, g, m, v, name):
    wd = w.shape[1]
    return _rowwise(_adamw_fn, name, [_whole(w), _whole(g), _whole(m), _whole(v)], [], [(wd, F32)] * 3,
                    tm=_row_tile(w.shape[0]))


def _me():
    return lax.axis_index("x"), lax.axis_index("y"), lax.axis_index("c")


def _chip_of(j):
    return j // 2, j % 2


_HBM = pl.BlockSpec(memory_space=pl.ANY)


def _gather_weights(shard):
    _, R, W = shard.shape

    def body(s_ref, out_ref, send_sems, recv_sems):
        x, y, c = _me()
        mine = 2 * x + y
        sib = (x, y, 1 - c)
        chips = [(1 - x, y), (x, 1 - y), (1 - x, 1 - y)]

        def rcopy(kk, src, chip_idx, half, to):
            return pltpu.make_async_remote_copy(src_ref=src, dst_ref=out_ref.at[chip_idx, half],
                                                send_sem=send_sems.at[kk], recv_sem=recv_sems.at[kk],
                                                device_id=to, device_id_type=MESH)

        first = [rcopy(j, s_ref.at[c], mine, c, (*chip, c)) for j, chip in enumerate(chips)]
        for cp in first:
            cp.start()
        passed = []
        for j, (px, py) in enumerate(chips):
            src_chip = 2 * px + py
            rcopy(j, s_ref.at[c], src_chip, c, (x, y, c)).wait_recv()
            fwd = rcopy(3 + j, out_ref.at[src_chip, c], src_chip, c, sib)
            fwd.start()
            passed.append(fwd)
        for j, (px, py) in enumerate(chips):
            rcopy(3 + j, s_ref.at[c], 2 * px + py, 1 - c, (x, y, c)).wait_recv()
        for cp in first + passed:
            cp.wait_send()

    return pl.pallas_call(
        body, name="gather_weights", in_specs=[_HBM], out_specs=_HBM,
        out_shape=jax.ShapeDtypeStruct((4, 2, R, W), shard.dtype),
        scratch_shapes=[pltpu.SemaphoreType.DMA((6,)), pltpu.SemaphoreType.DMA((6,))],
    )(shard)


def _pair_exchange(g):
    _, _, R, W = g.shape

    def body(g_ref, out_ref, send_sem, recv_sem):
        x, y, c = _me()
        cp = pltpu.make_async_remote_copy(src_ref=g_ref.at[:, 1 - c], dst_ref=out_ref, send_sem=send_sem,
                                          recv_sem=recv_sem, device_id=(x, y, 1 - c), device_id_type=MESH)
        cp.start()
        cp.wait()

    return pl.pallas_call(
        body, name="grad_pair_exchange", in_specs=[_HBM], out_specs=_HBM,
        out_shape=jax.ShapeDtypeStruct((4, R, W), g.dtype),
        scratch_shapes=[pltpu.SemaphoreType.DMA(()), pltpu.SemaphoreType.DMA(())],
    )(g)


def _pair_add(g, recv, cidx):
    _, _, R, W = g.shape
    tr = _row_tile(R)

    def body(c_ref, a_ref, b_ref, o_ref):
        o_ref[...] = (a_ref[...] + b_ref[...]).astype(o_ref.dtype)

    gs = pltpu.PrefetchScalarGridSpec(
        num_scalar_prefetch=1, grid=(4, R // tr),
        in_specs=[pl.BlockSpec((None, None, tr, W), lambda j, r, c: (j, c[0], r, 0)),
                  pl.BlockSpec((None, tr, W), lambda j, r, c: (j, r, 0))],
        out_specs=pl.BlockSpec((None, tr, W), lambda j, r, c: (j, r, 0)))
    return pl.pallas_call(body, name="grad_pair_add", grid_spec=gs,
                          out_shape=jax.ShapeDtypeStruct((4, R, W), BF16))(cidx, g, recv)


def _chip_exchange(p):
    _, R, W = p.shape

    def body(p_ref, out_ref, send_sems, recv_sems):
        x, y, c = _me()
        mine = 2 * x + y
        chips = [(1 - x, y), (x, 1 - y), (1 - x, 1 - y)]
        cps = []
        for j, (px, py) in enumerate(chips):
            cp = pltpu.make_async_remote_copy(src_ref=p_ref.at[2 * px + py], dst_ref=out_ref.at[mine],
                                              send_sem=send_sems.at[j], recv_sem=recv_sems.at[j],
                                              device_id=(px, py, c), device_id_type=MESH)
            cp.start()
            cps.append(cp)
        for j, (px, py) in enumerate(chips):
            pltpu.make_async_remote_copy(src_ref=p_ref.at[mine], dst_ref=out_ref.at[2 * px + py],
                                         send_sem=send_sems.at[j], recv_sem=recv_sems.at[j],
                                         device_id=(px, py, c), device_id_type=MESH).wait_recv()
        for cp in cps:
            cp.wait_send()

    return pl.pallas_call(
        body, name="grad_chip_exchange", in_specs=[_HBM], out_specs=_HBM,
        out_shape=jax.ShapeDtypeStruct((4, R, W), p.dtype),
        scratch_shapes=[pltpu.SemaphoreType.DMA((3,)), pltpu.SemaphoreType.DMA((3,))],
    )(p)


def _chip_sum(q):
    _, R, W = q.shape
    tr = _row_tile(R)

    def body(q_ref, o_ref):
        f = lambda t: q_ref[t].astype(F32)
        o_ref[...] = ((f(0) + f(1)) + f(2)) + f(3)

    return pl.pallas_call(body, name="grad_chip_sum", grid=(R // tr,),
                          in_specs=[pl.BlockSpec((4, tr, W), lambda r: (0, r, 0))],
                          out_specs=pl.BlockSpec((tr, W), lambda r: (r, 0)),
                          out_shape=jax.ShapeDtypeStruct((R, W), F32))(q)


def _pair_share(r):
    R, W = r.shape

    def body(r_ref, out_ref, send_sem, recv_sem):
        x, y, c = _me()
        cp = pltpu.make_async_remote_copy(src_ref=r_ref, dst_ref=out_ref, send_sem=send_sem, recv_sem=recv_sem,
                                          device_id=(x, y, 1 - c), device_id_type=MESH)
        cp.start()
        cp.wait()

    return pl.pallas_call(
        body, name="grad_pair_share", in_specs=[_HBM], out_specs=_HBM,
        out_shape=jax.ShapeDtypeStruct((R, W), r.dtype),
        scratch_shapes=[pltpu.SemaphoreType.DMA(()), pltpu.SemaphoreType.DMA(())],
    )(r)


def _allreduce_small(vec):
    R, W = vec.shape

    def body(v_ref, out_ref, buf, send_sems, recv_sems):
        x, y, c = _me()
        me = 4 * x + 2 * y + c
        buf[me] = v_ref[...]
        cps = []
        for kk in range(1, 8):
            peer = (x ^ (kk >> 2), y ^ ((kk >> 1) & 1), c ^ (kk & 1))
            cp = pltpu.make_async_remote_copy(src_ref=v_ref, dst_ref=buf.at[me], send_sem=send_sems.at[kk - 1],
                                              recv_sem=recv_sems.at[kk - 1], device_id=peer, device_id_type=MESH)
            cp.start()
            cps.append(cp)
        for kk in range(1, 8):
            peer = (x ^ (kk >> 2), y ^ ((kk >> 1) & 1), c ^ (kk & 1))
            pltpu.make_async_remote_copy(src_ref=v_ref, dst_ref=buf.at[me ^ kk], send_sem=send_sems.at[kk - 1],
                                         recv_sem=recv_sems.at[kk - 1], device_id=peer, device_id_type=MESH).wait_recv()
        for cp in cps:
            cp.wait_send()
        acc = buf[0]
        for d in range(1, 8):
            acc = acc + buf[d]
        out_ref[...] = acc

    vm = pl.BlockSpec(memory_space=pltpu.VMEM)
    return pl.pallas_call(
        body, name="allreduce_small", in_specs=[vm], out_specs=vm, out_shape=jax.ShapeDtypeStruct((R, W), F32),
        scratch_shapes=[pltpu.VMEM((8, R, W), F32), pltpu.SemaphoreType.DMA((7,)), pltpu.SemaphoreType.DMA((7,))],
    )(vec)


def _pack_flat(parts, align, dtype):
    flat = jnp.concatenate([p.reshape(-1).astype(dtype) for p in parts])
    n = flat.shape[0]
    pad = (-n) % align
    return jnp.pad(flat, (0, pad)) if pad else flat


def _unpack_flat(flat, shapes):
    out, off = [], 0
    for s in shapes:
        n = math.prod(s)
        out.append(flat[off:off + n].reshape(s))
        off += n
    return out


def _full_from_shards(name, sh):
    if name in COL_SHARDED:
        return jnp.transpose(sh, (1, 0, 2)).reshape(sh.shape[1], 4 * sh.shape[2])
    return sh.reshape(4 * sh.shape[1], sh.shape[2])


def _shards_from_full(name, full):
    if name in COL_SHARDED:
        r, cc = full.shape[0], full.shape[1] // 4
        return jnp.transpose(full.reshape(r, 4, cc), (1, 0, 2)).reshape(4, r * cc)
    return full.reshape(4, -1)


def _pad_heads(w, width):
    k = w.shape[0]
    return jnp.pad(w.reshape(k, HEADS, width), ((0, 0), (0, 0), (0, HP - width))).reshape(k, HEADS * HP)


def _unpad_heads(w, width):
    k = w.shape[0]
    return w.reshape(k, HEADS, HP)[:, :, :width].reshape(k, HEADS * width)


def _blockdiag(t):
    g, a, b = t.shape
    return jnp.einsum("gab,gk->gakb", t, jnp.eye(g, dtype=t.dtype)).reshape(g * a, g * b)


def _blockdiag_t(m, a, b):
    g = m.shape[0] // a
    return jnp.einsum("gagb->gab", m.reshape(g, a, g, b))


def _local_step(x, mem, positions, target, W, sp):
    T = x.shape[0]
    row = lambda v: v.reshape(1, -1)

    ab_re, ab_im, bb_re, bb_im = _s5_discretize(sp["s5_lam_re"], sp["s5_lam_im"], sp["s5_log_dt"], sp["s5_b_re"], sp["s5_b_im"])
    ar, ai = ab_re.reshape(1, S5_N), ab_im.reshape(1, S5_N)
    tc = min(SCAN_CHUNK, T)
    pr, pi = _cpow_table(ar, ai, tc)
    apow = jnp.concatenate([pr, pi], axis=1)
    apow_rev = jnp.concatenate([pr[::-1], -pi[::-1]], axis=1)
    nst = int(math.log2(tc))
    idx = jnp.array([(1 << k) - 1 for k in range(nst)])
    astep = jnp.concatenate([pr[idx], pi[idx]], axis=1)
    astep_conj = jnp.concatenate([pr[idx], -pi[idx]], axis=1)
    padr = (-nst) % 8
    if padr:
        astep = jnp.pad(astep, ((0, padr), (0, 0)))
        astep_conj = jnp.pad(astep_conj, ((0, padr), (0, 0)))
    b_st = jnp.concatenate([_blockdiag(jnp.swapaxes(bb_re, 1, 2)), _blockdiag(jnp.swapaxes(bb_im, 1, 2))], axis=1)
    c_st = jnp.concatenate([_blockdiag(jnp.swapaxes(sp["s5_c_re"], 1, 2)),
                            -_blockdiag(jnp.swapaxes(sp["s5_c_im"], 1, 2))], axis=0)
    inv = ROPE_THETA ** (-jnp.arange(0, ROPE, 2, dtype=F32) / ROPE)
    ang = positions.astype(F32)[:, None] * inv
    cs, sn = jnp.cos(ang), jnp.sin(ang)
    one, zero = jnp.ones((T, NOPE), F32), jnp.zeros((T, NOPE), F32)
    z16, tail1, tail0 = jnp.zeros((T, ROPE // 2), F32), jnp.ones((T, HP - QK), F32), jnp.zeros((T, HP - QK), F32)
    t_cos = jnp.concatenate([one, cs, cs, tail1], axis=1)
    t_s1 = jnp.concatenate([zero, z16, sn, tail0], axis=1)
    t_s2 = jnp.concatenate([zero, -sn, z16, tail0], axis=1)

    w_in = W["w_in"]
    w_in_re = jnp.concatenate([w_in[:, 800:], w_in[:, :768], w_in[:, 768:800],
                               jnp.zeros((D_MODEL, HP - ROPE), w_in.dtype)], axis=1)
    w_uq_p = _pad_heads(W["w_uq"], QK)
    wkv = W["w_ukv"].reshape(S5_W, HEADS, NOPE + VD)
    w_uk_p = _pad_heads(wkv[:, :, :NOPE].reshape(S5_W, HEADS * NOPE), NOPE)
    w_uv_p = _pad_heads(wkv[:, :, NOPE:].reshape(S5_W, HEADS * VD), VD)
    w_oa_p = jnp.pad(W["w_oa"].reshape(HEADS, VD, D_MODEL), ((0, 0), (0, HP - VD), (0, 0))).reshape(HEADS * HP, D_MODEL)

    g_in, b_in = row(sp["ln_in_g"]), row(sp["ln_in_b"])
    g1, b1, g2, b2, g3, b3 = (row(sp[k]) for k in ("ln1_g", "ln1_b", "ln2_g", "ln2_b", "ln3_g", "ln3_b"))
    gq, gkv, d_skip = row(sp["q_norm_g"]), row(sp["kv_norm_g"]), row(sp["s5_d"])

    (h0,) = _rowwise(lambda a, g, b: (_ln(a, g, b),), "ln_in_fwd", [_whole(x)], [g_in, b_in], [(D_MODEL, F32)])
    z = _mm(h0, w_in_re, "in_proj", tm=512, tn=Z_W)
    bu = _mm(z, b_st, "s5_bu", a_off=Z_U)
    hs = _s5_scan_fwd(bu, apow, astep, "s5_scan_fwd")
    yc = _mm(hs, c_st, "s5_c")
    (yg,) = _rowwise(lambda a, u, d: (_gelu_skip(a, u, d),), "s5_gelu_fwd", [_whole(yc), (z, Z_U, S5_W)], [d_skip], [(S5_W, BF16)])
    y2 = _mm(yg, W["w_glu"], "glu_proj", tn=2048)
    cqn, ckvn = _rowwise(lambda a, b, ga, gb: (_rms(a, ga), _rms(b, gb)), "mla_norm_fwd",
                         [(z, Z_CQ, S5_W), (z, Z_CKV, S5_W)], [gq, gkv], [(S5_W, BF16)] * 2)
    q_raw = _mm(cqn, w_uq_p, "mla_uq")
    k_raw = _mm(ckvn, w_uk_p, "mla_uk")
    v = _mm(ckvn, w_uv_p, "mla_uv", out_dtype=BF16)

    def rope_fwd(qr, kr_raw, krz, tc_, t1, t2):
        q = _rope(qr, _tile8(tc_), _tile8(t1), _tile8(t2))
        kr = _rope(pltpu.roll(krz, NOPE, 1), tc_, t1, t2)
        return q, kr_raw + _tile8(kr)

    q, k = _rowwise(rope_fwd, "mla_rope_fwd", [_whole(q_raw), _whole(k_raw), (z, Z_KR, HP), _whole(t_cos), _whole(t_s1), _whole(t_s2)],
                    [], [(HEADS * HP, BF16)] * 2)
    o, lse = _flash_fwd(q, k, v, "mla_attn_fwd")
    a_out = _mm(o, w_oa_p, "mla_oa")
    (mixin,) = _rowwise(lambda *a: (_gate_mix(*a),), "gate_mix_fwd",
                        [(z, Z_GS, D_MODEL), (z, Z_GA, D_MODEL), (y2, 0, D_MODEL), (y2, D_MODEL, D_MODEL), _whole(a_out)],
                        [], [(D_MODEL, BF16)])
    mix = _mm(mixin, W["w_o"], "mix_o")
    (h1,) = _rowwise(lambda h, r, g, b: (_ln_res(h, r, g, b),), "ln1_fwd", [_whole(h0), _whole(mix)], [g1, b1], [(D_MODEL, F32)])
    xq = _mm(h1, W["w_xq"], "x_q", out_dtype=BF16)
    xk = _mm(mem, W["w_xk"], "x_k", out_dtype=BF16)
    xv = _mm(mem, W["w_xv"], "x_v", out_dtype=BF16)
    (xo,) = _rowwise(_xattn_fwd_fn, "xattn_fwd", [_whole(xq)], [xk, xv], [(D_MODEL, BF16)])
    xa = _mm(xo, W["w_xo"], "x_o")
    (h2,) = _rowwise(lambda h, r, g, b: (_ln_res(h, r, g, b),), "ln2_fwd", [_whole(h1), _whole(xa)], [g2, b2], [(D_MODEL, F32)])
    a_up = _mm(h2, W["w_up"], "mlp_up", out_dtype=BF16)
    ff = _mm(a_up, W["w_down"], "mlp_down", act=_relu2)

    def loss_fn(h, r, tgt, g, b):
        def f(r_, g_, b_):
            e = _ln_res(h, r_, g_, b_) - tgt
            return 0.5 * jnp.sum(jnp.mean(e * e, axis=-1))
        lv, (dr_, dg_, db_) = jax.value_and_grad(f, argnums=(0, 1, 2))(r, g, b)
        return dr_, dg_, db_, jnp.broadcast_to(lv, (1, 128))

    dr3, dg3, db3, lossv = _rowwise(loss_fn, "ln3_loss_bwd", [_whole(h2), _whole(ff), _whole(target)], [g3, b3],
                                    [(D_MODEL, F32)], [(1, D_MODEL), (1, D_MODEL), (1, 128)])
    gW, gs = {}, {"ln3_g": dg3, "ln3_b": db3}

    da = _mm(dr3, W["w_down"], "mlp_down_bwd_a", "nt", epi=lambda acc, a: acc * (2.0 * jnp.maximum(a, 0.0)), extras=(a_up,),
             out_dtype=BF16)
    gW["w_down"] = _mm(a_up, dr3, "mlp_down_bwd_w", "tn", m=4 * D_MODEL, n=D_MODEL, act=_relu2)
    gW["w_up"] = _mm(h2, da, "mlp_up_bwd_w", "tn", m=D_MODEL, n=4 * D_MODEL)
    dh2 = _mm(da, W["w_up"], "mlp_up_bwd_a", "nt", epi=lambda acc, e: acc + DN_ALPHA * e, extras=(dr3,))

    def ln_res_bwd(h, r, dy, g, b):
        _, vjp = jax.vjp(_ln_res, h, r, g, b)
        _, dr_, dg_, db_ = vjp(dy)
        return dr_, dg_, db_

    dr2, gs["ln2_g"], gs["ln2_b"] = _rowwise(ln_res_bwd, "ln2_bwd", [_whole(h1), _whole(xa), _whole(dh2)], [g2, b2],
                                             [(D_MODEL, F32)], [(1, D_MODEL)] * 2)
    dxo = _mm(dr2, W["w_xo"], "x_o_bwd_a", "nt", out_dtype=BF16)
    gW["w_xo"] = _mm(xo, dr2, "x_o_bwd_w", "tn", m=D_MODEL, n=D_MODEL)
    dxq, dxk, dxv = _rowwise(_xattn_bwd_fn, "xattn_bwd", [_whole(xq), _whole(dxo)], [xk, xv], [(D_MODEL, BF16)],
                             [(xk.shape[0], D_MODEL)] * 2)
    gW["w_xq"] = _mm(h1, dxq, "x_q_bwd_w", "tn", m=D_MODEL, n=D_MODEL)
    gW["w_xk"] = _mm(mem, dxk, "x_k_bwd_w", "tn", m=D_MODEL, n=D_MODEL)
    gW["w_xv"] = _mm(mem, dxv, "x_v_bwd_w", "tn", m=D_MODEL, n=D_MODEL)
    dh1 = _mm(dxq, W["w_xq"], "x_q_bwd_a", "nt", epi=lambda acc, e: acc + DN_ALPHA * e, extras=(dr2,))

    dr1, gs["ln1_g"], gs["ln1_b"] = _rowwise(ln_res_bwd, "ln1_bwd", [_whole(h0), _whole(mix), _whole(dh1)], [g1, b1],
                                             [(D_MODEL, F32)], [(1, D_MODEL)] * 2)
    dmixin = _mm(dr1, W["w_o"], "mix_o_bwd_a", "nt")
    gW["w_o"] = _mm(mixin, dr1, "mix_o_bwd_w", "tn", m=D_MODEL, n=D_MODEL)

    def gate_mix_bwd(gs_, ga_, ya, yb, ao, dy):
        _, vjp = jax.vjp(_gate_mix, gs_, ga_, ya, yb, ao)
        return vjp(dy)

    dgs, dga, dy2a, dy2b, da_out = _rowwise(
        gate_mix_bwd, "gate_mix_bwd",
        [(z, Z_GS, D_MODEL), (z, Z_GA, D_MODEL), (y2, 0, D_MODEL), (y2, D_MODEL, D_MODEL), _whole(a_out), _whole(dmixin)],
        [], [(D_MODEL, BF16)] * 5)

    do = _mm(da_out, w_oa_p, "mla_oa_bwd_a", "nt")
    g_oa_p = _mm(o, da_out, "mla_oa_bwd_w", "tn", m=HEADS * HP, n=D_MODEL)
    gW["w_oa"] = g_oa_p.reshape(HEADS, HP, D_MODEL)[:, :VD].reshape(HEADS * VD, D_MODEL)
    delta, do16 = _rowwise(_attn_delta_fn, "mla_attn_delta", [_whole(do), _whole(o)], [], [(HEADS, F32), (HEADS * HP, BF16)])
    dq, dk, dv = _flash_bwd(q, k, v, do16, lse.reshape(HEADS, 1, T), delta.T.reshape(HEADS, 1, T), "mla_attn_bwd")

    def rope_bwd(dq_, dk_, tc_, t1, t2):
        dqr = _rope_t(dq_, _tile8(tc_), _tile8(t1), _tile8(t2))
        dkr = dk_[:, 0:HP]
        for hh in range(1, HEADS):
            dkr = dkr + dk_[:, hh * HP:(hh + 1) * HP]
        dkr = _rope_t(jnp.where(_rope_lanes(dkr.shape), dkr, 0.0), tc_, t1, t2)
        dkr = pltpu.roll(dkr, NOPE, 1)
        lane = lax.broadcasted_iota(jnp.int32, dkr.shape, 1)
        return dqr, jnp.where(lane < ROPE, dkr, 0.0)

    dq_raw, dkr = _rowwise(rope_bwd, "mla_rope_bwd", [_whole(dq), _whole(dk), _whole(t_cos), _whole(t_s1), _whole(t_s2)], [],
                           [(HEADS * HP, BF16), (HP, BF16)])
    dcqn = _mm(dq_raw, w_uq_p, "mla_uq_bwd_a", "nt")
    gW["w_uq"] = _unpad_heads(_mm(cqn, dq_raw, "mla_uq_bwd_w", "tn", m=S5_W, n=HEADS * HP), QK)
    dckvn_k = _mm(dk, w_uk_p, "mla_uk_bwd_a", "nt")
    dckvn = _mm(dv, w_uv_p, "mla_uv_bwd_a", "nt", epi=lambda acc, e: acc + e, extras=(dckvn_k,))
    g_uk = _unpad_heads(_mm(ckvn, dk, "mla_uk_bwd_w", "tn", m=S5_W, n=HEADS * HP), NOPE).reshape(S5_W, HEADS, NOPE)
    g_uv = _unpad_heads(_mm(ckvn, dv, "mla_uv_bwd_w", "tn", m=S5_W, n=HEADS * HP), VD).reshape(S5_W, HEADS, VD)
    gW["w_ukv"] = jnp.concatenate([g_uk, g_uv], axis=2).reshape(S5_W, HEADS * (NOPE + VD))

    def norm_bwd(a, b, da_, db_, ga, gb):
        _, vja = jax.vjp(_rms, a, ga)
        _, vjb = jax.vjp(_rms, b, gb)
        dxa, dga_ = vja(da_)
        dxb, dgb_ = vjb(db_)
        return dxa, dxb, dga_, dgb_

    dcq, dckv, gs["q_norm_g"], gs["kv_norm_g"] = _rowwise(
        norm_bwd, "mla_norm_bwd", [(z, Z_CQ, S5_W), (z, Z_CKV, S5_W), _whole(dcqn), _whole(dckvn)], [gq, gkv],
        [(S5_W, BF16)] * 2, [(1, S5_W)] * 2)

    dy2 = jnp.concatenate([dy2a, dy2b], axis=1)
    dyg = _mm(dy2, W["w_glu"], "glu_bwd_a", "nt")
    gW["w_glu"] = _mm(yg, dy2, "glu_bwd_w", "tn", m=S5_W, n=2 * D_MODEL)

    def gelu_bwd(a, u, dy, d):
        _, vjp = jax.vjp(_gelu_skip, a, u, d)
        return vjp(dy)

    dyc, du_skip, gs["s5_d"] = _rowwise(gelu_bwd, "s5_gelu_bwd", [_whole(yc), (z, Z_U, S5_W), _whole(dyg)], [d_skip],
                                        [(S5_W, BF16), (S5_W, F32)], [(1, S5_W)])
    g_cst = _mm(hs, dyc, "s5_c_bwd_w", "tn", m=2 * S5_N, n=S5_W)
    dhs = _mm(dyc, c_st, "s5_c_bwd_a", "nt")
    lam, d_ab = _s5_scan_bwd(dhs, hs, apow_rev, astep_conj, "s5_scan_bwd")
    g_bst = _mm(z, lam, "s5_bu_bwd_w", "tn", a_off=Z_U, m=S5_W, n=2 * S5_N)
    du = _mm(lam, b_st, "s5_bu_bwd_a", "nt", epi=lambda acc, e: acc + e, extras=(du_skip,), out_dtype=BF16)
    gs["s5_ab"] = d_ab
    gs["s5_bb_re"] = jnp.swapaxes(_blockdiag_t(g_bst[:, :S5_N], S5_H, S5_P), 1, 2)
    gs["s5_bb_im"] = jnp.swapaxes(_blockdiag_t(g_bst[:, S5_N:], S5_H, S5_P), 1, 2)
    gs["s5_c_re"] = jnp.swapaxes(_blockdiag_t(g_cst[:S5_N], S5_P, S5_H), 1, 2)
    gs["s5_c_im"] = -jnp.swapaxes(_blockdiag_t(g_cst[S5_N:], S5_P, S5_H), 1, 2)

    dz = jnp.concatenate([dgs, dga, du, dcq, dckv, dkr], axis=1)
    g_in_re = _mm(h0, dz, "in_proj_bwd_w", "tn", m=D_MODEL, n=Z_W, tm=512, tn=Z_W)
    gW["w_in"] = jnp.concatenate([g_in_re[:, Z_U:Z_KR], g_in_re[:, Z_KR:Z_KR + ROPE], g_in_re[:, :Z_U]], axis=1)
    dh0 = _mm(dz, w_in_re, "in_proj_bwd_a", "nt", tm=512, tk=Z_W, epi=lambda acc, e: acc + DN_ALPHA * e, extras=(dr1,))

    def ln_bwd(a, dy, g, b):
        _, vjp = jax.vjp(_ln, a, g, b)
        return vjp(dy)

    dx, gs["ln_in_g"], gs["ln_in_b"] = _rowwise(ln_bwd, "ln_in_bwd", [_whole(x), _whole(dh0)], [g_in, b_in],
                                                [(D_MODEL, F32)], [(1, D_MODEL)] * 2)
    return lossv[:, :1], dx, gW, gs


_RAW_SMALL = (("loss", (1, 1)), ("ln_in_g", (1, D_MODEL)), ("ln_in_b", (1, D_MODEL)), ("ln1_g", (1, D_MODEL)),
              ("ln1_b", (1, D_MODEL)), ("ln2_g", (1, D_MODEL)), ("ln2_b", (1, D_MODEL)), ("ln3_g", (1, D_MODEL)),
              ("ln3_b", (1, D_MODEL)), ("q_norm_g", (1, S5_W)), ("kv_norm_g", (1, S5_W)), ("s5_d", (1, S5_W)),
              ("s5_ab", (1, 2 * S5_N)), ("s5_bb_re", (S5_G, S5_P, S5_H)), ("s5_bb_im", (S5_G, S5_P, S5_H)),
              ("s5_c_re", (S5_G, S5_H, S5_P)), ("s5_c_im", (S5_G, S5_H, S5_P)))


def kernel(x, mem, positions, ln_in_g, ln_in_b, w_in, s5_lam_re, s5_lam_im, s5_log_dt, s5_b_re, s5_b_im, s5_c_re, s5_c_im, s5_d, w_glu, q_norm_g, w_uq, kv_norm_g, w_ukv, w_oa, w_o, ln1_g, ln1_b, w_xq, w_xk, w_xv, w_xo, ln2_g, ln2_b, w_up, w_down, ln3_g, ln3_b, loss_target, m_ln_in_g, m_ln_in_b, m_w_in, m_s5_lam_re, m_s5_lam_im, m_s5_log_dt, m_s5_b_re, m_s5_b_im, m_s5_c_re, m_s5_c_im, m_s5_d, m_w_glu, m_q_norm_g, m_w_uq, m_kv_norm_g, m_w_ukv, m_w_oa, m_w_o, m_ln1_g, m_ln1_b, m_w_xq, m_w_xk, m_w_xv, m_w_xo, m_ln2_g, m_ln2_b, m_w_up, m_w_down, m_ln3_g, m_ln3_b, v_ln_in_g, v_ln_in_b, v_w_in, v_s5_lam_re, v_s5_lam_im, v_s5_log_dt, v_s5_b_re, v_s5_b_im, v_s5_c_re, v_s5_c_im, v_s5_d, v_w_glu, v_q_norm_g, v_w_uq, v_kv_norm_g, v_w_ukv, v_w_oa, v_w_o, v_ln1_g, v_ln1_b, v_w_xq, v_w_xk, v_w_xv, v_w_xo, v_ln2_g, v_ln2_b, v_w_up, v_w_down, v_ln3_g, v_ln3_b):
    a = dict(locals())
    wts = {n: a[n] for n in WEIGHTS}
    ms = {n: a["m_" + n] for n in WEIGHTS}
    vs = {n: a["v_" + n] for n in WEIGHTS}
    cidx = lax.axis_index("c").astype(jnp.int32).reshape(1)

    shard2d = {n: wts[n].reshape(wts[n].shape[-2], wts[n].shape[-1]) for n in BIG}
    nrow = {n: shard2d[n].size // 1024 for n in BIG}
    pad_rows = (-sum(nrow.values())) % PACK_ROWS
    rows_half = (sum(nrow.values()) + pad_rows) // 2
    mine = 2 * lax.axis_index("x") + lax.axis_index("y")
    packed = jnp.concatenate([shard2d[n].astype(BF16).reshape(nrow[n], 1024) for n in BIG]
                             + [jnp.zeros((pad_rows, 1024), BF16)], axis=0).reshape(1, 2, rows_half, 1024)
    gathered = lax.dynamic_update_slice(_gather_weights(packed[0]), packed, (mine, 0, 0, 0))
    gathered = gathered.reshape(4, 2 * rows_half, 1024)
    W, off = {}, 0
    for n in BIG:
        r, cc = shard2d[n].shape
        piece = lax.optimization_barrier(gathered[:, off:off + nrow[n]])
        W[n] = _full_from_shards(n, piece.reshape(4, r, cc))
        off += nrow[n]

    sp = {n: wts[n] for n in SMALL}
    sp_local = {n: (sp[n][0] if sp[n].ndim > 1 else sp[n]) for n in SMALL}
    lossv, dx, gW, gs = _local_step(x[0], mem[0], positions[0], loss_target[0], W, sp_local)

    gflat = jnp.concatenate([_shards_from_full(n, gW[n]).reshape(4, nrow[n], 1024) for n in BIG]
                            + [jnp.zeros((4, pad_rows, 1024), F32)], axis=1).reshape(4, 2, rows_half, 1024)
    recv = _pair_exchange(gflat)
    psum = _pair_add(gflat, recv, cidx)
    own = lax.dynamic_slice(psum, (mine, 0, 0), (1, rows_half, 1024))
    slots = lax.dynamic_update_slice(_chip_exchange(psum), own, (mine, 0, 0))
    red = _chip_sum(slots)
    other = _pair_share(red)
    is_mine = (jnp.arange(2) == lax.axis_index("c"))[:, None, None]
    gsh = jnp.where(is_mine, red[None], other[None]).reshape(2 * rows_half, 1024)
    g_out, off = {}, 0
    for n in BIG:
        g_out[n] = lax.optimization_barrier(gsh[off:off + nrow[n]]).reshape(shard2d[n].shape)
        off += nrow[n]

    gs["loss"] = lossv
    raw = _pack_flat([gs[n].reshape(s) for n, s in _RAW_SMALL], 8 * 1024, F32)
    raw = _allreduce_small(raw.reshape(-1, 1024)).reshape(-1)
    rs = dict(zip([n for n, _ in _RAW_SMALL], _unpack_flat(raw, [s for _, s in _RAW_SMALL])))
    loss = rs["loss"].reshape(())
    _, disc_vjp = jax.vjp(_s5_discretize, sp_local["s5_lam_re"], sp_local["s5_lam_im"], sp_local["s5_log_dt"],
                          sp_local["s5_b_re"], sp_local["s5_b_im"])
    d_ab = rs["s5_ab"].reshape(2, S5_G, S5_P)
    g_lre, g_lim, g_ldt, g_bre, g_bim = disc_vjp((d_ab[0], d_ab[1], rs["s5_bb_re"], rs["s5_bb_im"]))
    small_g = {"s5_lam_re": g_lre, "s5_lam_im": g_lim, "s5_log_dt": g_ldt, "s5_b_re": g_bre, "s5_b_im": g_bim,
               "s5_c_re": rs["s5_c_re"], "s5_c_im": rs["s5_c_im"]}
    for n in ("ln_in_g", "ln_in_b", "ln1_g", "ln1_b", "ln2_g", "ln2_b", "ln3_g", "ln3_b", "q_norm_g", "kv_norm_g", "s5_d"):
        small_g[n] = rs[n]

    grads, deltas, new_m, new_v = {}, {}, {}, {}
    for n in BIG:
        d_, m_, v_ = _adamw(shard2d[n], g_out[n], ms[n].reshape(shard2d[n].shape), vs[n].reshape(shard2d[n].shape), "adamw_" + n)
        grads[n] = g_out[n].reshape(wts[n].shape)
        deltas[n], new_m[n], new_v[n] = (t.reshape(wts[n].shape) for t in (d_, m_, v_))
    packs = [_pack_flat([t[n] for n in SMALL], 8 * 1024, F32).reshape(-1, 1024)
             for t in (wts, {n: small_g[n].reshape(wts[n].shape) for n in SMALL}, ms, vs)]
    sd, sm, sv = _adamw(*packs, "adamw_small")
    shapes = [wts[n].shape for n in SMALL]
    for n, d_, m_, v_ in zip(SMALL, _unpack_flat(sd.reshape(-1), shapes), _unpack_flat(sm.reshape(-1), shapes),
                             _unpack_flat(sv.reshape(-1), shapes)):
        grads[n] = small_g[n].reshape(wts[n].shape)
        deltas[n], new_m[n], new_v[n] = d_, m_, v_

    return (loss, dx[None], *[grads[n] for n in WEIGHTS], *[deltas[n] for n in WEIGHTS],
            *[new_m[n] for n in WEIGHTS], *[new_v[n] for n in WEIGHTS])
```

```python
import functools
import math

import jax
import jax.numpy as jnp
from jax import lax
from jax.experimental import pallas as pl
from jax.experimental.pallas import tpu as pltpu

F32 = jnp.float32
BF16 = jnp.bfloat16
MESH = pl.DeviceIdType.MESH

D_MODEL = 1024
S5_W = 256
S5_G = 16
S5_H = 16
S5_P = 64
S5_N = S5_G * S5_P
S5_MAX_RE = -1e-4
HEADS = 8
NOPE = 64
ROPE = 32
QK = NOPE + ROPE
VD = 64
HP = 128
XH = 4
XD = 256
LN_EPS = 1e-5
RMS_EPS = 1e-6
NEG_INF = -1e30
DN_ALPHA = 2.0 ** 0.25
ROPE_THETA = 10000.0
ADAM_LR, ADAM_B1, ADAM_B2, ADAM_EPS, ADAM_WD, ADAM_STEP = 0.001, 0.9, 0.999, 1e-08, 0.01, 10

Z_GS, Z_GA, Z_U, Z_CQ, Z_CKV, Z_KR, Z_W = 0, 1024, 2048, 2304, 2560, 2816, 2944
SCAN_CHUNK = 256

BIG = ("w_in", "w_glu", "w_uq", "w_ukv", "w_oa", "w_up", "w_o", "w_xq", "w_xk", "w_xv", "w_xo", "w_down")
COL_SHARDED = ("w_in", "w_glu", "w_uq", "w_ukv", "w_oa", "w_up")
SMALL = ("ln_in_g", "ln_in_b", "s5_lam_re", "s5_lam_im", "s5_log_dt", "s5_b_re", "s5_b_im", "s5_c_re", "s5_c_im",
         "s5_d", "q_norm_g", "kv_norm_g", "ln1_g", "ln1_b", "ln2_g", "ln2_b", "ln3_g", "ln3_b")
WEIGHTS = ("ln_in_g", "ln_in_b", "w_in", "s5_lam_re", "s5_lam_im", "s5_log_dt", "s5_b_re", "s5_b_im", "s5_c_re",
           "s5_c_im", "s5_d", "w_glu", "q_norm_g", "w_uq", "kv_norm_g", "w_ukv", "w_oa", "w_o", "ln1_g", "ln1_b",
           "w_xq", "w_xk", "w_xv", "w_xo", "ln2_g", "ln2_b", "w_up", "w_down", "ln3_g", "ln3_b")
PACK_ROWS = 2 * 16


def _row_tile(n, cap=512):
    best = n
    for t in range(8, min(n, cap) + 1, 8):
        if n % t == 0:
            best = t
    return best


def _pick(n, cap):
    best = None
    for t in range(128, min(n, cap) + 1, 128):
        if n % t == 0:
            best = t
    return best if best is not None and (best >= 512 or best == n) else n


def _mm(a, b, name, mode="nn", *, a_off=0, b_off=0, m=None, n=None, act=None, epi=None, extras=(),
        out_dtype=F32, tm=1024, tn=1024, tk=None):
    if mode == "nn":
        M, (K, N) = a.shape[0], b.shape
    elif mode == "nt":
        M, (N, K) = a.shape[0], b.shape
    else:
        K, M, N = a.shape[0], m, n
    if mode == "tn":
        tm, tn, tk = _pick(M, tm), _pick(N, tn), min(tk or 512, K)
    else:
        tm, tn, tk = min(tm, M), _pick(N, tn), _pick(K, tk or 1024)
    assert M % tm == 0 and N % tn == 0 and K % tk == 0, (name, M, N, K, tm, tn, tk)
    nk = K // tk
    if mode == "tn":
        assert a_off % tm == 0 and b_off % tn == 0
        ao, bo = a_off // tm, b_off // tn
        a_spec = pl.BlockSpec((tk, tm), lambda i, j, k: (k, i + ao))
        b_spec = pl.BlockSpec((tk, tn), lambda i, j, k: (k, j + bo))
        dims = (((0,), (0,)), ((), ()))
    else:
        assert a_off % tk == 0
        ao = a_off // tk
        a_spec = pl.BlockSpec((tm, tk), lambda i, j, k: (i, k + ao))
        if mode == "nn":
            b_spec = pl.BlockSpec((tk, tn), lambda i, j, k: (k, j))
            dims = (((1,), (0,)), ((), ()))
        else:
            b_spec = pl.BlockSpec((tn, tk), lambda i, j, k: (j, k))
            dims = (((1,), (1,)), ((), ()))
    e_spec = pl.BlockSpec((tm, tn), lambda i, j, k: (i, j))
    n_extra = len(extras)

    def body(*refs):
        a_ref, b_ref = refs[0], refs[1]
        e_refs = refs[2:2 + n_extra]
        o_ref = refs[2 + n_extra]
        av = a_ref[...]
        if act is not None:
            av = act(av.astype(F32))
        p = lax.dot_general(av.astype(BF16), b_ref[...].astype(BF16), dims, preferred_element_type=F32)

        def finish(r):
            if epi is not None:
                r = epi(r, *[e[...] for e in e_refs])
            o_ref[...] = r.astype(o_ref.dtype)

        if nk == 1:
            finish(p)
        else:
            acc = refs[3 + n_extra]
            k = pl.program_id(2)

            @pl.when(k == 0)
            def _():
                acc[...] = p

            @pl.when(k > 0)
            def _():
                acc[...] += p

            @pl.when(k == nk - 1)
            def _():
                finish(acc[...])

    return pl.pallas_call(
        body, name=name, grid=(M // tm, N // tn, nk),
        in_specs=[a_spec, b_spec] + [e_spec] * n_extra, out_specs=e_spec,
        out_shape=jax.ShapeDtypeStruct((M, N), out_dtype),
        scratch_shapes=[pltpu.VMEM((tm, tn), F32)] if nk > 1 else [],
        compiler_params=pltpu.CompilerParams(dimension_semantics=("parallel", "parallel", "arbitrary")),
    )(a, b, *extras)


def _rowwise(fn, name, rows, pars, outs, accs=(), tm=256):
    T = rows[0][0].shape[0]
    tm = min(tm, T)
    assert T % tm == 0
    n_in, n_out = len(rows) + len(pars), len(outs)
    in_specs = []
    for arr, off, w in rows:
        assert off % w == 0 and arr.shape[0] == T, name
        in_specs.append(pl.BlockSpec((tm, w), functools.partial(lambda i, o: (i, o), o=off // w)))
    for p in pars:
        in_specs.append(pl.BlockSpec(p.shape, lambda i: (0, 0)))
    out_specs = [pl.BlockSpec((tm, w), lambda i: (i, 0)) for w, _ in outs]
    out_specs += [pl.BlockSpec(s, lambda i: (0, 0)) for s in accs]
    out_shape = [jax.ShapeDtypeStruct((T, w), dt) for w, dt in outs] + [jax.ShapeDtypeStruct(s, F32) for s in accs]

    def body(*refs):
        res = fn(*[r[...] for r in refs[:n_in]])
        o_refs = refs[n_in:]
        for r, v in zip(o_refs[:n_out], res[:n_out]):
            r[...] = v.astype(r.dtype)
        if accs:
            i = pl.program_id(0)

            @pl.when(i == 0)
            def _():
                for r in o_refs[n_out:]:
                    r[...] = jnp.zeros_like(r)

            for r, v in zip(o_refs[n_out:], res[n_out:]):
                r[...] += v

    res = pl.pallas_call(
        body, name=name, grid=(T // tm,), in_specs=in_specs, out_specs=out_specs, out_shape=out_shape,
        compiler_params=pltpu.CompilerParams(dimension_semantics=("arbitrary",)),
    )(*[r[0] for r in rows], *pars)
    return res


def _whole(arr):
    return (arr, 0, arr.shape[1])


def _ln(x, g, b):
    mu = jnp.mean(x, axis=-1, keepdims=True)
    xc = x - mu
    var = jnp.mean(xc * xc, axis=-1, keepdims=True)
    return xc * lax.rsqrt(var + LN_EPS) * g + b


def _ln_res(h, r, g, b):
    return _ln(DN_ALPHA * h + r, g, b)


def _rms(x, g):
    return x * lax.rsqrt(jnp.mean(x * x, axis=-1, keepdims=True) + RMS_EPS) * g


def _gelu_skip(yc, u, d):
    y = yc + d * u
    return 0.5 * y * (1.0 + lax.erf(y * (1.0 / math.sqrt(2.0))))


def _gate_mix(gs, ga, y2a, y2b, aout):
    return jax.nn.sigmoid(gs) * (y2a * jax.nn.sigmoid(y2b)) + jax.nn.sigmoid(ga) * aout


def _relu2(a):
    r = jnp.maximum(a, 0.0)
    return r * r


def _tile8(t):
    return jnp.concatenate([t] * HEADS, axis=1)


def _rope(x, cos, s1, s2):
    w = x.shape[1]
    return x * cos + pltpu.roll(x, ROPE // 2, 1) * s1 + pltpu.roll(x, w - ROPE // 2, 1) * s2


def _rope_t(dy, cos, s1, s2):
    w = dy.shape[1]
    return dy * cos + pltpu.roll(dy * s1, w - ROPE // 2, 1) + pltpu.roll(dy * s2, ROPE // 2, 1)


def _rope_lanes(shape):
    lane = lax.broadcasted_iota(jnp.int32, shape, 1) % HP
    return (lane >= NOPE) & (lane < QK)


ATT_HB = 2
ATT_SCALE = QK ** -0.5
LOG2E = 1.4426950408889634
LN2 = 0.6931471805599453
_NT = (((1,), (1,)), ((), ()))
_TN = (((0,), (0,)), ((), ()))


def _tri_tables(n, by_row):
    if by_row:
        pairs = [(i, j) for i in range(n) for j in range(i + 1)]
    else:
        pairs = [(i, j) for j in range(n) for i in range(j, n)]
    return jnp.array([p[0] for p in pairs], jnp.int32), jnp.array([p[1] for p in pairs], jnp.int32)


def _flash_fwd(q, k, v, name, bq=512):
    T = q.shape[0]
    bq = min(bq, T)
    nq = T // bq
    hb = ATT_HB

    i_tab, j_tab = _tri_tables(nq, by_row=True)

    def body(it_ref, jt_ref, q_ref, k_ref, v_ref, o_ref, lse_ref, m_sc, l_sc, acc_sc):
        t = pl.program_id(1)
        i, j = it_ref[t], jt_ref[t]

        @pl.when(j == 0)
        def _():
            m_sc[...] = jnp.full_like(m_sc, NEG_INF)
            l_sc[...] = jnp.zeros_like(l_sc)
            acc_sc[...] = jnp.zeros_like(acc_sc)

        def step(masked):
            for hh in range(hb):
                sl = slice(hh * HP, (hh + 1) * HP)
                s = lax.dot_general(q_ref[:, sl], k_ref[:, sl], _NT, preferred_element_type=F32) * (ATT_SCALE * LOG2E)
                if masked:
                    row = lax.broadcasted_iota(jnp.int32, (bq, bq), 0)
                    col = lax.broadcasted_iota(jnp.int32, (bq, bq), 1)
                    s = jnp.where(col <= row, s, NEG_INF)
                m_prev = m_sc[hh]
                m_new = jnp.maximum(m_prev, jnp.max(s, axis=1, keepdims=True))
                alpha = jnp.exp2(m_prev - m_new)
                p = jnp.exp2(s - jnp.concatenate([m_new] * (bq // HP), axis=1))
                l_sc[hh] = alpha * l_sc[hh] + jnp.sum(p, axis=1, keepdims=True)
                acc_sc[hh] = alpha * acc_sc[hh] + jnp.dot(p.astype(BF16), v_ref[:, sl], preferred_element_type=F32)
                m_sc[hh] = m_new

        @pl.when(j < i)
        def _():
            step(False)

        @pl.when(j == i)
        def _():
            step(True)
            for hh in range(hb):
                o_ref[:, hh * HP:(hh + 1) * HP] = acc_sc[hh] / l_sc[hh]
                lse_ref[hh] = (m_sc[hh][:, :1] + jnp.log2(l_sc[hh][:, :1])) * LN2

    qs = pl.BlockSpec((bq, hb * HP), lambda h, t, it, jt: (it[t], h))
    ks = pl.BlockSpec((bq, hb * HP), lambda h, t, it, jt: (jt[t], h))
    gs = pltpu.PrefetchScalarGridSpec(
        num_scalar_prefetch=2, grid=(HEADS // hb, i_tab.shape[0]), in_specs=[qs, ks, ks],
        out_specs=[qs, pl.BlockSpec((hb, bq, 1), lambda h, t, it, jt: (h, it[t], 0))],
        scratch_shapes=[pltpu.VMEM((hb, bq, HP), F32)] * 3)
    return pl.pallas_call(
        body, name=name, grid_spec=gs,
        out_shape=[jax.ShapeDtypeStruct((T, HEADS * HP), F32), jax.ShapeDtypeStruct((HEADS, T, 1), F32)],
        compiler_params=pltpu.CompilerParams(dimension_semantics=("parallel", "arbitrary")),
    )(i_tab, j_tab, q, k, v)


def _attn_delta_fn(do, o):
    prod = do * o
    cols = [jnp.sum(prod[:, h * HP:(h + 1) * HP], axis=1, keepdims=True) for h in range(HEADS)]
    return jnp.concatenate(cols, axis=1), do


def _flash_bwd(q, k, v, do, lse_row, delta_row, name, bq=512):
    T = q.shape[0]
    bq = min(bq, T)
    nq = T // bq
    hb = ATT_HB

    i_tab, j_tab = _tri_tables(nq, by_row=False)
    n_blk = i_tab.shape[0]

    def body(it_ref, jt_ref, q_ref, k_ref, v_ref, do_ref, lse_ref, dl_ref, dq_ref, dk_ref, dv_ref):
        t = pl.program_id(1)
        i, j = it_ref[t], jt_ref[t]

        @pl.when(t == 0)
        def _():
            dq_ref[...] = jnp.zeros_like(dq_ref)

        @pl.when(i == j)
        def _():
            dk_ref[...] = jnp.zeros_like(dk_ref)
            dv_ref[...] = jnp.zeros_like(dv_ref)

        def step(masked):
            rows = pl.ds(pl.multiple_of(i * bq, bq), bq)
            for hh in range(hb):
                sl = slice(hh * HP, (hh + 1) * HP)
                qh, kh, doh = q_ref[:, sl], k_ref[:, sl], do_ref[:, sl]
                st = lax.dot_general(kh, qh, _NT, preferred_element_type=F32) * (ATT_SCALE * LOG2E)
                pt = jnp.exp2(st - lse_ref[hh] * LOG2E)
                if masked:
                    krow = lax.broadcasted_iota(jnp.int32, (bq, bq), 0)
                    qcol = lax.broadcasted_iota(jnp.int32, (bq, bq), 1)
                    pt = jnp.where(krow <= qcol, pt, 0.0)
                dv_ref[:, sl] += jnp.dot(pt.astype(BF16), doh, preferred_element_type=F32)
                dpt = lax.dot_general(v_ref[:, sl], doh, _NT, preferred_element_type=F32)
                dst = (pt * (dpt - dl_ref[hh])).astype(BF16)
                dk_ref[:, sl] += jnp.dot(dst, qh, preferred_element_type=F32)
                dq_ref[rows, sl] += lax.dot_general(dst, kh, _TN, preferred_element_type=F32)

        @pl.when(i > j)
        def _():
            step(False)

        @pl.when(i == j)
        def _():
            step(True)

        @pl.when(i == nq - 1)
        def _():
            dk_ref[...] *= ATT_SCALE

        @pl.when(t == n_blk - 1)
        def _():
            dq_ref[...] *= ATT_SCALE

    qs = pl.BlockSpec((bq, hb * HP), lambda h, t, it, jt: (it[t], h))
    ks = pl.BlockSpec((bq, hb * HP), lambda h, t, it, jt: (jt[t], h))
    rs = pl.BlockSpec((hb, 1, bq), lambda h, t, it, jt: (h, 0, it[t]))
    full = pl.BlockSpec((T, hb * HP), lambda h, t, it, jt: (0, h))
    gs = pltpu.PrefetchScalarGridSpec(num_scalar_prefetch=2, grid=(HEADS // hb, n_blk),
                                      in_specs=[qs, ks, ks, qs, rs, rs], out_specs=[full, ks, ks])
    return pl.pallas_call(
        body, name=name, grid_spec=gs, out_shape=[jax.ShapeDtypeStruct((T, HEADS * HP), F32)] * 3,
        compiler_params=pltpu.CompilerParams(dimension_semantics=("parallel", "arbitrary")),
    )(i_tab, j_tab, q, k, v, do, lse_row, delta_row)


def _xattn_heads(q, k, v):
    scale = XD ** -0.5
    ps = []
    for h in range(XH):
        sl = slice(h * XD, (h + 1) * XD)
        s = lax.dot_general(q[:, sl].astype(BF16), k[:, sl].astype(BF16), (((1,), (1,)), ((), ())),
                            preferred_element_type=F32) * scale
        e = jnp.exp(s - jnp.max(s, axis=1, keepdims=True))
        ps.append(e / jnp.sum(e, axis=1, keepdims=True))
    return ps


def _xattn_fwd_fn(q, k, v):
    ps = _xattn_heads(q, k, v)
    o = [jnp.dot(p.astype(BF16), v[:, h * XD:(h + 1) * XD].astype(BF16), preferred_element_type=F32)
         for h, p in enumerate(ps)]
    return (jnp.concatenate(o, axis=1),)


def _xattn_bwd_fn(q, do, k, v):
    scale = XD ** -0.5
    ps = _xattn_heads(q, k, v)
    dqs, dks, dvs = [], [], []
    tdims = (((0,), (0,)), ((), ()))
    for h, p in enumerate(ps):
        sl = slice(h * XD, (h + 1) * XD)
        doh = do[:, sl].astype(BF16)
        dvs.append(lax.dot_general(p.astype(BF16), doh, tdims, preferred_element_type=F32))
        dp = lax.dot_general(doh, v[:, sl].astype(BF16), (((1,), (1,)), ((), ())), preferred_element_type=F32)
        ds = (p * (dp - jnp.sum(dp * p, axis=1, keepdims=True)) * scale).astype(BF16)
        dqs.append(jnp.dot(ds, k[:, sl].astype(BF16), preferred_element_type=F32))
        dks.append(lax.dot_general(ds, q[:, sl].astype(BF16), tdims, preferred_element_type=F32))
    return jnp.concatenate(dqs, axis=1), jnp.concatenate(dks, axis=1), jnp.concatenate(dvs, axis=1)


S5_SUB = 8


def _cmul_add(xr, xi, ar, ai, sr, si):
    return xr + ar * sr - ai * si, xi + ar * si + ai * sr


def _s5_tables(ar, ai, reverse):
    pows = [(ar, ai)]
    for _ in range(S5_SUB - 1):
        pr, pi = pows[-1]
        pows.append((pr * ar - pi * ai, pr * ai + pi * ar))
    cr = jnp.concatenate([p[0] for p in pows], axis=0)
    ci = jnp.concatenate([p[1] for p in pows], axis=0)
    if reverse:
        cr, ci = cr[::-1], ci[::-1]
    t = jnp.arange(S5_SUB)[:, None]
    blocks = [jnp.concatenate([cr, ci], axis=1)]
    for s in (1, 2, 4):
        keep = (t < S5_SUB - s) if reverse else (t >= s)
        sr, si = pows[s - 1]
        blocks.append(jnp.concatenate([jnp.where(keep, sr, 0.0), jnp.where(keep, si, 0.0)], axis=1))
    return jnp.concatenate(blocks, axis=0)


def _sub_scan(xr, xi, tab_ref, reverse):
    for k, s in enumerate((1, 2, 4)):
        blk = slice(S5_SUB * (k + 1), S5_SUB * (k + 2))
        sh = S5_SUB - s if reverse else s
        xr, xi = _cmul_add(xr, xi, tab_ref[blk, :S5_N], tab_ref[blk, S5_N:], pltpu.roll(xr, sh, 0), pltpu.roll(xi, sh, 0))
    return xr, xi


def _s5_fwd(z, b_st, c_st, d_skip, tab, name, tc=512):
    T = z.shape[0]
    tc = min(tc, T)
    nsub = tc // S5_SUB

    def body(u_ref, bst_ref, cst_ref, d_ref, tab_ref, h_ref, yc_ref, yg_ref, carry, x_sc):
        @pl.when(pl.program_id(0) == 0)
        def _():
            carry[...] = jnp.zeros_like(carry)

        u = u_ref[...]
        x_sc[...] = jnp.dot(u.astype(BF16), bst_ref[...], preferred_element_type=F32)

        def sub(b, c):
            r = pl.ds(pl.multiple_of(b * S5_SUB, S5_SUB), S5_SUB)
            xr, xi = _sub_scan(x_sc[r, :S5_N], x_sc[r, S5_N:], tab_ref, False)
            hr, hi = _cmul_add(xr, xi, tab_ref[0:S5_SUB, :S5_N], tab_ref[0:S5_SUB, S5_N:], c[0], c[1])
            x_sc[r, :S5_N] = hr
            x_sc[r, S5_N:] = hi
            return hr[S5_SUB - 1:S5_SUB], hi[S5_SUB - 1:S5_SUB]

        cr, ci = lax.fori_loop(0, nsub, sub, (carry[0:1, :S5_N], carry[0:1, S5_N:]))
        carry[0:1, :S5_N] = cr
        carry[0:1, S5_N:] = ci
        h = x_sc[...]
        h_ref[...] = h
        yc = jnp.dot(h.astype(BF16), cst_ref[...], preferred_element_type=F32)
        yc_ref[...] = yc
        yg_ref[...] = _gelu_skip(yc, u, d_ref[...]).astype(yg_ref.dtype)

    whole = lambda a: pl.BlockSpec(a.shape, lambda i: (0, 0))
    row = lambda w: pl.BlockSpec((tc, w), lambda i: (i, 0))
    return pl.pallas_call(
        body, name=name, grid=(T // tc,),
        in_specs=[pl.BlockSpec((tc, S5_W), lambda i: (i, Z_U // S5_W)), whole(b_st), whole(c_st), whole(d_skip), whole(tab)],
        out_specs=[row(2 * S5_N), row(S5_W), row(S5_W)],
        out_shape=[jax.ShapeDtypeStruct((T, 2 * S5_N), F32), jax.ShapeDtypeStruct((T, S5_W), F32),
                   jax.ShapeDtypeStruct((T, S5_W), BF16)],
        scratch_shapes=[pltpu.VMEM((8, 2 * S5_N), F32), pltpu.VMEM((tc, 2 * S5_N), F32)],
        compiler_params=pltpu.CompilerParams(dimension_semantics=("arbitrary",)),
    )(z, b_st, c_st, d_skip, tab)


def _s5_bwd(dyg, yc, z, h, b_st, c_st, d_skip, tab, name, tc=512):
    T = z.shape[0]
    tc = min(tc, T)
    nc, nsub = T // tc, tc // S5_SUB

    def body(dyg_ref, yc_ref, u_ref, h_ref, hp_ref, bst_ref, cst_ref, d_ref, tab_ref,
             du_ref, da_ref, db_ref, dct_ref, dd_ref, carry, x_sc):
        g = pl.program_id(0)

        @pl.when(g == 0)
        def _():
            carry[...] = jnp.zeros_like(carry)
            for r in (da_ref, db_ref, dct_ref, dd_ref):
                r[...] = jnp.zeros_like(r)

        u = u_ref[...]
        _, vjp = jax.vjp(_gelu_skip, yc_ref[...], u, d_ref[...])
        dyc, du_skip, dd = vjp(dyg_ref[...])
        dd_ref[...] += dd
        dyc16 = dyc.astype(BF16)
        dct_ref[...] += lax.dot_general(dyc16, h_ref[...].astype(BF16), _TN, preferred_element_type=F32)
        x_sc[...] = lax.dot_general(dyc16, cst_ref[...], _NT, preferred_element_type=F32)
        first = jnp.where(g == nc - 1, 0.0, 1.0)
        row0 = lax.broadcasted_iota(jnp.int32, (S5_SUB, S5_N), 0) == 0

        def sub(k, c):
            cr, ci, acc_r, acc_i = c
            b = nsub - 1 - k
            r = pl.ds(pl.multiple_of(b * S5_SUB, S5_SUB), S5_SUB)
            xr, xi = _sub_scan(x_sc[r, :S5_N], x_sc[r, S5_N:], tab_ref, True)
            lr, li = _cmul_add(xr, xi, tab_ref[0:S5_SUB, :S5_N], tab_ref[0:S5_SUB, S5_N:], cr, ci)
            x_sc[r, :S5_N] = lr
            x_sc[r, S5_N:] = li
            rp = pl.ds(pl.multiple_of(jnp.maximum(b - 1, 0) * S5_SUB, S5_SUB), S5_SUB)
            last_r = jnp.where(b == 0, hp_ref[S5_SUB - 1:S5_SUB, :S5_N] * first, h_ref[rp, :S5_N][S5_SUB - 1:S5_SUB])
            last_i = jnp.where(b == 0, hp_ref[S5_SUB - 1:S5_SUB, S5_N:] * first, h_ref[rp, S5_N:][S5_SUB - 1:S5_SUB])
            pr = jnp.where(row0, last_r, pltpu.roll(h_ref[r, :S5_N], 1, 0))
            pi = jnp.where(row0, last_i, pltpu.roll(h_ref[r, S5_N:], 1, 0))
            return lr[0:1], li[0:1], acc_r + (lr * pr + li * pi), acc_i + (li * pr - lr * pi)

        zero = jnp.zeros((S5_SUB, S5_N), F32)
        cr, ci, acc_r, acc_i = lax.fori_loop(0, nsub, sub, (carry[0:1, :S5_N], carry[0:1, S5_N:], zero, zero))
        carry[0:1, :S5_N] = cr
        carry[0:1, S5_N:] = ci
        da_ref[0:1, :S5_N] += jnp.sum(acc_r, axis=0, keepdims=True)
        da_ref[0:1, S5_N:] += jnp.sum(acc_i, axis=0, keepdims=True)
        lam16 = x_sc[...].astype(BF16)
        db_ref[...] += lax.dot_general(u.astype(BF16), lam16, _TN, preferred_element_type=F32)
        du = lax.dot_general(lam16, bst_ref[...], _NT, preferred_element_type=F32) + du_skip
        du_ref[...] = du.astype(du_ref.dtype)

    whole = lambda a: pl.BlockSpec(a.shape, lambda g: (0, 0))
    row = lambda w, off=0: pl.BlockSpec((tc, w), lambda g: (nc - 1 - g, off))
    prev = pl.BlockSpec((S5_SUB, 2 * S5_N), lambda g: (jnp.maximum((nc - 1 - g) * nsub - 1, 0), 0))
    acc = lambda s: pl.BlockSpec(s, lambda g: (0, 0))
    return pl.pallas_call(
        body, name=name, grid=(nc,),
        in_specs=[row(S5_W), row(S5_W), row(S5_W, Z_U // S5_W), row(2 * S5_N), prev, whole(b_st), whole(c_st),
                  whole(d_skip), whole(tab)],
        out_specs=[row(S5_W), acc((1, 2 * S5_N)), acc((S5_W, 2 * S5_N)), acc((S5_W, 2 * S5_N)), acc((1, S5_W))],
        out_shape=[jax.ShapeDtypeStruct((T, S5_W), BF16), jax.ShapeDtypeStruct((1, 2 * S5_N), F32),
                   jax.ShapeDtypeStruct((S5_W, 2 * S5_N), F32), jax.ShapeDtypeStruct((S5_W, 2 * S5_N), F32),
                   jax.ShapeDtypeStruct((1, S5_W), F32)],
        scratch_shapes=[pltpu.VMEM((8, 2 * S5_N), F32), pltpu.VMEM((tc, 2 * S5_N), F32)],
        compiler_params=pltpu.CompilerParams(dimension_semantics=("arbitrary",)),
    )(dyg, yc, z, h, h, b_st, c_st, d_skip, tab)


def _s5_discretize(lam_re, lam_im, log_dt, b_re, b_im):
    lr = jnp.minimum(lam_re, S5_MAX_RE)
    li = lam_im
    dt = jnp.exp(log_dt)[:, None]
    mag = jnp.exp(lr * dt)
    ang = li * dt
    ab_re = mag * jnp.cos(ang)
    ab_im = mag * jnp.sin(ang)
    den = lr * lr + li * li
    nr = ab_re - 1.0
    f_re = ((nr * lr + ab_im * li) / den)[..., None]
    f_im = ((ab_im * lr - nr * li) / den)[..., None]
    return ab_re, ab_im, f_re * b_re - f_im * b_im, f_re * b_im + f_im * b_re


def _adamw_fn(w, g, m, v):
    m = ADAM_B1 * m + (1.0 - ADAM_B1) * g
    v = ADAM_B2 * v + (1.0 - ADAM_B2) * (g * g)
    m_hat = m / (1.0 - ADAM_B1 ** ADAM_STEP)
    v_hat = v / (1.0 - ADAM_B2 ** ADAM_STEP)
    delta = -ADAM_LR * (m_hat / (jnp.sqrt(v_hat) + ADAM_EPS) + ADAM_WD * w)
    return delta, m, v


def _adamw(w, g, m, v, name):
    wd = w.shape[1]
    return _rowwise(_adamw_fn, name, [_whole(w), _whole(g), _whole(m), _whole(v)], [], [(wd, F32)] * 3,
                    tm=_row_tile(w.shape[0]))


def _me():
    return lax.axis_index("x"), lax.axis_index("y"), lax.axis_index("c")


def _chip_of(j):
    return j // 2, j % 2


_HBM = pl.BlockSpec(memory_space=pl.ANY)


def _gather_weights(shard):
    _, R, W = shard.shape

    def body(s_ref, out_ref, send_sems, recv_sems):
        x, y, c = _me()
        mine = 2 * x + y
        sib = (x, y, 1 - c)
        chips = [(1 - x, y), (x, 1 - y), (1 - x, 1 - y)]

        def rcopy(kk, src, chip_idx, half, to):
            return pltpu.make_async_remote_copy(src_ref=src, dst_ref=out_ref.at[chip_idx, half],
                                                send_sem=send_sems.at[kk], recv_sem=recv_sems.at[kk],
                                                device_id=to, device_id_type=MESH)

        first = [rcopy(j, s_ref.at[c], mine, c, (*chip, c)) for j, chip in enumerate(chips)]
        for cp in first:
            cp.start()
        passed = []
        for j, (px, py) in enumerate(chips):
            src_chip = 2 * px + py
            rcopy(j, s_ref.at[c], src_chip, c, (x, y, c)).wait_recv()
            fwd = rcopy(3 + j, out_ref.at[src_chip, c], src_chip, c, sib)
            fwd.start()
            passed.append(fwd)
        for j, (px, py) in enumerate(chips):
            rcopy(3 + j, s_ref.at[c], 2 * px + py, 1 - c, (x, y, c)).wait_recv()
        for cp in first + passed:
            cp.wait_send()

    return pl.pallas_call(
        body, name="gather_weights", in_specs=[_HBM], out_specs=_HBM,
        out_shape=jax.ShapeDtypeStruct((4, 2, R, W), shard.dtype),
        scratch_shapes=[pltpu.SemaphoreType.DMA((6,)), pltpu.SemaphoreType.DMA((6,))],
    )(shard)


def _pair_exchange(g):
    _, _, R, W = g.shape

    def body(g_ref, out_ref, send_sem, recv_sem):
        x, y, c = _me()
        cp = pltpu.make_async_remote_copy(src_ref=g_ref.at[:, 1 - c], dst_ref=out_ref, send_sem=send_sem,
                                          recv_sem=recv_sem, device_id=(x, y, 1 - c), device_id_type=MESH)
        cp.start()
        cp.wait()

    return pl.pallas_call(
        body, name="grad_pair_exchange", in_specs=[_HBM], out_specs=_HBM,
        out_shape=jax.ShapeDtypeStruct((4, R, W), g.dtype),
        scratch_shapes=[pltpu.SemaphoreType.DMA(()), pltpu.SemaphoreType.DMA(())],
    )(g)


def _pair_add(g, recv, cidx):
    _, _, R, W = g.shape
    tr = _row_tile(R)

    def body(c_ref, a_ref, b_ref, o_ref):
        o_ref[...] = (a_ref[...] + b_ref[...]).astype(o_ref.dtype)

    gs = pltpu.PrefetchScalarGridSpec(
        num_scalar_prefetch=1, grid=(4, R // tr),
        in_specs=[pl.BlockSpec((None, None, tr, W), lambda j, r, c: (j, c[0], r, 0)),
                  pl.BlockSpec((None, tr, W), lambda j, r, c: (j, r, 0))],
        out_specs=pl.BlockSpec((None, tr, W), lambda j, r, c: (j, r, 0)))
    return pl.pallas_call(body, name="grad_pair_add", grid_spec=gs,
                          out_shape=jax.ShapeDtypeStruct((4, R, W), BF16))(cidx, g, recv)


def _chip_exchange(p):
    _, R, W = p.shape

    def body(p_ref, out_ref, send_sems, recv_sems):
        x, y, c = _me()
        mine = 2 * x + y
        chips = [(1 - x, y), (x, 1 - y), (1 - x, 1 - y)]
        cps = []
        for j, (px, py) in enumerate(chips):
            cp = pltpu.make_async_remote_copy(src_ref=p_ref.at[2 * px + py], dst_ref=out_ref.at[mine],
                                              send_sem=send_sems.at[j], recv_sem=recv_sems.at[j],
                                              device_id=(px, py, c), device_id_type=MESH)
            cp.start()
            cps.append(cp)
        for j, (px, py) in enumerate(chips):
            pltpu.make_async_remote_copy(src_ref=p_ref.at[mine], dst_ref=out_ref.at[2 * px + py],
                                         send_sem=send_sems.at[j], recv_sem=recv_sems.at[j],
                                         device_id=(px, py, c), device_id_type=MESH).wait_recv()
        for cp in cps:
            cp.wait_send()

    return pl.pallas_call(
        body, name="grad_chip_exchange", in_specs=[_HBM], out_specs=_HBM,
        out_shape=jax.ShapeDtypeStruct((4, R, W), p.dtype),
        scratch_shapes=[pltpu.SemaphoreType.DMA((3,)), pltpu.SemaphoreType.DMA((3,))],
    )(p)


def _chip_sum(q):
    _, R, W = q.shape
    tr = _row_tile(R)

    def body(q_ref, o_ref):
        f = lambda t: q_ref[t].astype(F32)
        o_ref[...] = ((f(0) + f(1)) + f(2)) + f(3)

    return pl.pallas_call(body, name="grad_chip_sum", grid=(R // tr,),
                          in_specs=[pl.BlockSpec((4, tr, W), lambda r: (0, r, 0))],
                          out_specs=pl.BlockSpec((tr, W), lambda r: (r, 0)),
                          out_shape=jax.ShapeDtypeStruct((R, W), F32))(q)


def _pair_share(r):
    R, W = r.shape

    def body(r_ref, out_ref, send_sem, recv_sem):
        x, y, c = _me()
        cp = pltpu.make_async_remote_copy(src_ref=r_ref, dst_ref=out_ref, send_sem=send_sem, recv_sem=recv_sem,
                                          device_id=(x, y, 1 - c), device_id_type=MESH)
        cp.start()
        cp.wait()

    return pl.pallas_call(
        body, name="grad_pair_share", in_specs=[_HBM], out_specs=_HBM,
        out_shape=jax.ShapeDtypeStruct((R, W), r.dtype),
        scratch_shapes=[pltpu.SemaphoreType.DMA(()), pltpu.SemaphoreType.DMA(())],
    )(r)


def _allreduce_small(vec):
    R, W = vec.shape

    def body(v_ref, out_ref, buf, send_sems, recv_sems):
        x, y, c = _me()
        me = 4 * x + 2 * y + c
        buf[me] = v_ref[...]
        cps = []
        for kk in range(1, 8):
            peer = (x ^ (kk >> 2), y ^ ((kk >> 1) & 1), c ^ (kk & 1))
            cp = pltpu.make_async_remote_copy(src_ref=v_ref, dst_ref=buf.at[me], send_sem=send_sems.at[kk - 1],
                                              recv_sem=recv_sems.at[kk - 1], device_id=peer, device_id_type=MESH)
            cp.start()
            cps.append(cp)
        for kk in range(1, 8):
            peer = (x ^ (kk >> 2), y ^ ((kk >> 1) & 1), c ^ (kk & 1))
            pltpu.make_async_remote_copy(src_ref=v_ref, dst_ref=buf.at[me ^ kk], send_sem=send_sems.at[kk - 1],
                                         recv_sem=recv_sems.at[kk - 1], device_id=peer, device_id_type=MESH).wait_recv()
        for cp in cps:
            cp.wait_send()
        acc = buf[0]
        for d in range(1, 8):
            acc = acc + buf[d]
        out_ref[...] = acc

    vm = pl.BlockSpec(memory_space=pltpu.VMEM)
    return pl.pallas_call(
        body, name="allreduce_small", in_specs=[vm], out_specs=vm, out_shape=jax.ShapeDtypeStruct((R, W), F32),
        scratch_shapes=[pltpu.VMEM((8, R, W), F32), pltpu.SemaphoreType.DMA((7,)), pltpu.SemaphoreType.DMA((7,))],
    )(vec)


def _pack_flat(parts, align, dtype):
    flat = jnp.concatenate([p.reshape(-1).astype(dtype) for p in parts])
    n = flat.shape[0]
    pad = (-n) % align
    return jnp.pad(flat, (0, pad)) if pad else flat


def _unpack_flat(flat, shapes):
    out, off = [], 0
    for s in shapes:
        n = math.prod(s)
        out.append(flat[off:off + n].reshape(s))
        off += n
    return out


def _full_from_shards(name, sh):
    if name in COL_SHARDED:
        return jnp.transpose(sh, (1, 0, 2)).reshape(sh.shape[1], 4 * sh.shape[2])
    return sh.reshape(4 * sh.shape[1], sh.shape[2])


def _shards_from_full(name, full):
    if name in COL_SHARDED:
        r, cc = full.shape[0], full.shape[1] // 4
        return jnp.transpose(full.reshape(r, 4, cc), (1, 0, 2)).reshape(4, r * cc)
    return full.reshape(4, -1)


def _pad_heads(w, width):
    k = w.shape[0]
    return jnp.pad(w.reshape(k, HEADS, width), ((0, 0), (0, 0), (0, HP - width))).reshape(k, HEADS * HP)


def _unpad_heads(w, width):
    k = w.shape[0]
    return w.reshape(k, HEADS, HP)[:, :, :width].reshape(k, HEADS * width)


def _blockdiag(t):
    g, a, b = t.shape
    return jnp.einsum("gab,gk->gakb", t, jnp.eye(g, dtype=t.dtype)).reshape(g * a, g * b)


def _blockdiag_t(m, a, b):
    g = m.shape[0] // a
    return jnp.einsum("gagb->gab", m.reshape(g, a, g, b))


def _local_step(x, mem, positions, target, W, sp):
    T = x.shape[0]
    row = lambda v: v.reshape(1, -1)

    ab_re, ab_im, bb_re, bb_im = _s5_discretize(sp["s5_lam_re"], sp["s5_lam_im"], sp["s5_log_dt"], sp["s5_b_re"], sp["s5_b_im"])
    ar, ai = ab_re.reshape(1, S5_N), ab_im.reshape(1, S5_N)
    tab_fwd, tab_rev = _s5_tables(ar, ai, False), _s5_tables(ar, -ai, True)
    b_st = jnp.concatenate([_blockdiag(jnp.swapaxes(bb_re, 1, 2)), _blockdiag(jnp.swapaxes(bb_im, 1, 2))],
                           axis=1).astype(BF16)
    c_st = jnp.concatenate([_blockdiag(jnp.swapaxes(sp["s5_c_re"], 1, 2)),
                            -_blockdiag(jnp.swapaxes(sp["s5_c_im"], 1, 2))], axis=0).astype(BF16)
    inv = ROPE_THETA ** (-jnp.arange(0, ROPE, 2, dtype=F32) / ROPE)
    ang = positions.astype(F32)[:, None] * inv
    cs, sn = jnp.cos(ang), jnp.sin(ang)
    one, zero = jnp.ones((T, NOPE), F32), jnp.zeros((T, NOPE), F32)
    z16, tail1, tail0 = jnp.zeros((T, ROPE // 2), F32), jnp.ones((T, HP - QK), F32), jnp.zeros((T, HP - QK), F32)
    t_cos = jnp.concatenate([one, cs, cs, tail1], axis=1)
    t_s1 = jnp.concatenate([zero, z16, sn, tail0], axis=1)
    t_s2 = jnp.concatenate([zero, -sn, z16, tail0], axis=1)

    w_in = W["w_in"]
    w_in_re = jnp.concatenate([w_in[:, 800:], w_in[:, :768], w_in[:, 768:800],
                               jnp.zeros((D_MODEL, HP - ROPE), w_in.dtype)], axis=1)
    w_uq_p = _pad_heads(W["w_uq"], QK)
    wkv = W["w_ukv"].reshape(S5_W, HEADS, NOPE + VD)
    w_uk_p = _pad_heads(wkv[:, :, :NOPE].reshape(S5_W, HEADS * NOPE), NOPE)
    w_uv_p = _pad_heads(wkv[:, :, NOPE:].reshape(S5_W, HEADS * VD), VD)
    w_oa_p = jnp.pad(W["w_oa"].reshape(HEADS, VD, D_MODEL), ((0, 0), (0, HP - VD), (0, 0))).reshape(HEADS * HP, D_MODEL)

    g_in, b_in = row(sp["ln_in_g"]), row(sp["ln_in_b"])
    g1, b1, g2, b2, g3, b3 = (row(sp[k]) for k in ("ln1_g", "ln1_b", "ln2_g", "ln2_b", "ln3_g", "ln3_b"))
    gq, gkv, d_skip = row(sp["q_norm_g"]), row(sp["kv_norm_g"]), row(sp["s5_d"])

    (h0,) = _rowwise(lambda a, g, b: (_ln(a, g, b),), "ln_in_fwd", [_whole(x)], [g_in, b_in], [(D_MODEL, F32)])
    z = _mm(h0, w_in_re, "in_proj", tm=512, tn=Z_W)
    hs, yc, yg = _s5_fwd(z, b_st, c_st, d_skip, tab_fwd, "s5_fwd")
    y2 = _mm(yg, W["w_glu"], "glu_proj", tn=2048)
    cqn, ckvn = _rowwise(lambda a, b, ga, gb: (_rms(a, ga), _rms(b, gb)), "mla_norm_fwd",
                         [(z, Z_CQ, S5_W), (z, Z_CKV, S5_W)], [gq, gkv], [(S5_W, BF16)] * 2)
    q_raw = _mm(cqn, w_uq_p, "mla_uq")
    k_raw = _mm(ckvn, w_uk_p, "mla_uk")
    v = _mm(ckvn, w_uv_p, "mla_uv", out_dtype=BF16)

    def rope_fwd(qr, kr_raw, krz, tc_, t1, t2):
        q = _rope(qr, _tile8(tc_), _tile8(t1), _tile8(t2))
        kr = _rope(pltpu.roll(krz, NOPE, 1), tc_, t1, t2)
        return q, kr_raw + _tile8(kr)

    q, k = _rowwise(rope_fwd, "mla_rope_fwd", [_whole(q_raw), _whole(k_raw), (z, Z_KR, HP), _whole(t_cos), _whole(t_s1), _whole(t_s2)],
                    [], [(HEADS * HP, BF16)] * 2)
    o, lse = _flash_fwd(q, k, v, "mla_attn_fwd")
    a_out = _mm(o, w_oa_p, "mla_oa")
    (mixin,) = _rowwise(lambda *a: (_gate_mix(*a),), "gate_mix_fwd",
                        [(z, Z_GS, D_MODEL), (z, Z_GA, D_MODEL), (y2, 0, D_MODEL), (y2, D_MODEL, D_MODEL), _whole(a_out)],
                        [], [(D_MODEL, BF16)])
    mix = _mm(mixin, W["w_o"], "mix_o")
    (h1,) = _rowwise(lambda h, r, g, b: (_ln_res(h, r, g, b),), "ln1_fwd", [_whole(h0), _whole(mix)], [g1, b1], [(D_MODEL, F32)])
    xq = _mm(h1, W["w_xq"], "x_q", out_dtype=BF16)
    xk = _mm(mem, W["w_xk"], "x_k", out_dtype=BF16)
    xv = _mm(mem, W["w_xv"], "x_v", out_dtype=BF16)
    (xo,) = _rowwise(_xattn_fwd_fn, "xattn_fwd", [_whole(xq)], [xk, xv], [(D_MODEL, BF16)])
    xa = _mm(xo, W["w_xo"], "x_o")
    (h2,) = _rowwise(lambda h, r, g, b: (_ln_res(h, r, g, b),), "ln2_fwd", [_whole(h1), _whole(xa)], [g2, b2], [(D_MODEL, F32)])
    a_up = _mm(h2, W["w_up"], "mlp_up", out_dtype=BF16)
    ff = _mm(a_up, W["w_down"], "mlp_down", act=_relu2)

    def loss_fn(h, r, tgt, g, b):
        def f(r_, g_, b_):
            e = _ln_res(h, r_, g_, b_) - tgt
            return 0.5 * jnp.sum(jnp.mean(e * e, axis=-1))
        lv, (dr_, dg_, db_) = jax.value_and_grad(f, argnums=(0, 1, 2))(r, g, b)
        return dr_, dg_, db_, jnp.broadcast_to(lv, (1, 128))

    dr3, dg3, db3, lossv = _rowwise(loss_fn, "ln3_loss_bwd", [_whole(h2), _whole(ff), _whole(target)], [g3, b3],
                                    [(D_MODEL, F32)], [(1, D_MODEL), (1, D_MODEL), (1, 128)])
    gW, gs = {}, {"ln3_g": dg3, "ln3_b": db3}

    da = _mm(dr3, W["w_down"], "mlp_down_bwd_a", "nt", epi=lambda acc, a: acc * (2.0 * jnp.maximum(a, 0.0)), extras=(a_up,),
             out_dtype=BF16)
    gW["w_down"] = _mm(a_up, dr3, "mlp_down_bwd_w", "tn", m=4 * D_MODEL, n=D_MODEL, act=_relu2)
    gW["w_up"] = _mm(h2, da, "mlp_up_bwd_w", "tn", m=D_MODEL, n=4 * D_MODEL)
    dh2 = _mm(da, W["w_up"], "mlp_up_bwd_a", "nt", epi=lambda acc, e: acc + DN_ALPHA * e, extras=(dr3,))

    def ln_res_bwd(h, r, dy, g, b):
        _, vjp = jax.vjp(_ln_res, h, r, g, b)
        _, dr_, dg_, db_ = vjp(dy)
        return dr_, dg_, db_

    dr2, gs["ln2_g"], gs["ln2_b"] = _rowwise(ln_res_bwd, "ln2_bwd", [_whole(h1), _whole(xa), _whole(dh2)], [g2, b2],
                                             [(D_MODEL, F32)], [(1, D_MODEL)] * 2)
    dxo = _mm(dr2, W["w_xo"], "x_o_bwd_a", "nt", out_dtype=BF16)
    gW["w_xo"] = _mm(xo, dr2, "x_o_bwd_w", "tn", m=D_MODEL, n=D_MODEL)
    dxq, dxk, dxv = _rowwise(_xattn_bwd_fn, "xattn_bwd", [_whole(xq), _whole(dxo)], [xk, xv], [(D_MODEL, BF16)],
                             [(xk.shape[0], D_MODEL)] * 2)
    gW["w_xq"] = _mm(h1, dxq, "x_q_bwd_w", "tn", m=D_MODEL, n=D_MODEL)
    gW["w_xk"] = _mm(mem, dxk, "x_k_bwd_w", "tn", m=D_MODEL, n=D_MODEL)
    gW["w_xv"] = _mm(mem, dxv, "x_v_bwd_w", "tn", m=D_MODEL, n=D_MODEL)
    dh1 = _mm(dxq, W["w_xq"], "x_q_bwd_a", "nt", epi=lambda acc, e: acc + DN_ALPHA * e, extras=(dr2,))

    dr1, gs["ln1_g"], gs["ln1_b"] = _rowwise(ln_res_bwd, "ln1_bwd", [_whole(h0), _whole(mix), _whole(dh1)], [g1, b1],
                                             [(D_MODEL, F32)], [(1, D_MODEL)] * 2)
    dmixin = _mm(dr1, W["w_o"], "mix_o_bwd_a", "nt")
    gW["w_o"] = _mm(mixin, dr1, "mix_o_bwd_w", "tn", m=D_MODEL, n=D_MODEL)

    def gate_mix_bwd(gs_, ga_, ya, yb, ao, dy):
        _, vjp = jax.vjp(_gate_mix, gs_, ga_, ya, yb, ao)
        return vjp(dy)

    dgs, dga, dy2a, dy2b, da_out = _rowwise(
        gate_mix_bwd, "gate_mix_bwd",
        [(z, Z_GS, D_MODEL), (z, Z_GA, D_MODEL), (y2, 0, D_MODEL), (y2, D_MODEL, D_MODEL), _whole(a_out), _whole(dmixin)],
        [], [(D_MODEL, BF16)] * 5)

    do = _mm(da_out, w_oa_p, "mla_oa_bwd_a", "nt")
    g_oa_p = _mm(o, da_out, "mla_oa_bwd_w", "tn", m=HEADS * HP, n=D_MODEL)
    gW["w_oa"] = g_oa_p.reshape(HEADS, HP, D_MODEL)[:, :VD].reshape(HEADS * VD, D_MODEL)
    delta, do16 = _rowwise(_attn_delta_fn, "mla_attn_delta", [_whole(do), _whole(o)], [], [(HEADS, F32), (HEADS * HP, BF16)])
    dq, dk, dv = _flash_bwd(q, k, v, do16, lse.reshape(HEADS, 1, T), delta.T.reshape(HEADS, 1, T), "mla_attn_bwd")

    def rope_bwd(dq_, dk_, tc_, t1, t2):
        dqr = _rope_t(dq_, _tile8(tc_), _tile8(t1), _tile8(t2))
        dkr = dk_[:, 0:HP]
        for hh in range(1, HEADS):
            dkr = dkr + dk_[:, hh * HP:(hh + 1) * HP]
        dkr = _rope_t(jnp.where(_rope_lanes(dkr.shape), dkr, 0.0), tc_, t1, t2)
        dkr = pltpu.roll(dkr, NOPE, 1)
        lane = lax.broadcasted_iota(jnp.int32, dkr.shape, 1)
        return dqr, jnp.where(lane < ROPE, dkr, 0.0)

    dq_raw, dkr = _rowwise(rope_bwd, "mla_rope_bwd", [_whole(dq), _whole(dk), _whole(t_cos), _whole(t_s1), _whole(t_s2)], [],
                           [(HEADS * HP, BF16), (HP, BF16)])
    dcqn = _mm(dq_raw, w_uq_p, "mla_uq_bwd_a", "nt")
    gW["w_uq"] = _unpad_heads(_mm(cqn, dq_raw, "mla_uq_bwd_w", "tn", m=S5_W, n=HEADS * HP), QK)
    dckvn_k = _mm(dk, w_uk_p, "mla_uk_bwd_a", "nt")
    dckvn = _mm(dv, w_uv_p, "mla_uv_bwd_a", "nt", epi=lambda acc, e: acc + e, extras=(dckvn_k,))
    g_uk = _unpad_heads(_mm(ckvn, dk, "mla_uk_bwd_w", "tn", m=S5_W, n=HEADS * HP), NOPE).reshape(S5_W, HEADS, NOPE)
    g_uv = _unpad_heads(_mm(ckvn, dv, "mla_uv_bwd_w", "tn", m=S5_W, n=HEADS * HP), VD).reshape(S5_W, HEADS, VD)
    gW["w_ukv"] = jnp.concatenate([g_uk, g_uv], axis=2).reshape(S5_W, HEADS * (NOPE + VD))

    def norm_bwd(a, b, da_, db_, ga, gb):
        _, vja = jax.vjp(_rms, a, ga)
        _, vjb = jax.vjp(_rms, b, gb)
        dxa, dga_ = vja(da_)
        dxb, dgb_ = vjb(db_)
        return dxa, dxb, dga_, dgb_

    dcq, dckv, gs["q_norm_g"], gs["kv_norm_g"] = _rowwise(
        norm_bwd, "mla_norm_bwd", [(z, Z_CQ, S5_W), (z, Z_CKV, S5_W), _whole(dcqn), _whole(dckvn)], [gq, gkv],
        [(S5_W, BF16)] * 2, [(1, S5_W)] * 2)

    dy2 = jnp.concatenate([dy2a, dy2b], axis=1)
    dyg = _mm(dy2, W["w_glu"], "glu_bwd_a", "nt")
    gW["w_glu"] = _mm(yg, dy2, "glu_bwd_w", "tn", m=S5_W, n=2 * D_MODEL)

    du, gs["s5_ab"], g_bst, g_cst_t, gs["s5_d"] = _s5_bwd(dyg, yc, z, hs, b_st, c_st, d_skip, tab_rev, "s5_bwd")
    gs["s5_bb_re"] = jnp.swapaxes(_blockdiag_t(g_bst[:, :S5_N], S5_H, S5_P), 1, 2)
    gs["s5_bb_im"] = jnp.swapaxes(_blockdiag_t(g_bst[:, S5_N:], S5_H, S5_P), 1, 2)
    gs["s5_c_re"] = _blockdiag_t(g_cst_t[:, :S5_N], S5_H, S5_P)
    gs["s5_c_im"] = -_blockdiag_t(g_cst_t[:, S5_N:], S5_H, S5_P)

    dz = jnp.concatenate([dgs, dga, du, dcq, dckv, dkr], axis=1)
    g_in_re = _mm(h0, dz, "in_proj_bwd_w", "tn", m=D_MODEL, n=Z_W, tm=512, tn=Z_W)
    gW["w_in"] = jnp.concatenate([g_in_re[:, Z_U:Z_KR], g_in_re[:, Z_KR:Z_KR + ROPE], g_in_re[:, :Z_U]], axis=1)
    dh0 = _mm(dz, w_in_re, "in_proj_bwd_a", "nt", tm=512, tk=Z_W, epi=lambda acc, e: acc + DN_ALPHA * e, extras=(dr1,))

    def ln_bwd(a, dy, g, b):
        _, vjp = jax.vjp(_ln, a, g, b)
        return vjp(dy)

    dx, gs["ln_in_g"], gs["ln_in_b"] = _rowwise(ln_bwd, "ln_in_bwd", [_whole(x), _whole(dh0)], [g_in, b_in],
                                                [(D_MODEL, F32)], [(1, D_MODEL)] * 2)
    return lossv[:, :1], dx, gW, gs


_RAW_SMALL = (("loss", (1, 1)), ("ln_in_g", (1, D_MODEL)), ("ln_in_b", (1, D_MODEL)), ("ln1_g", (1, D_MODEL)),
              ("ln1_b", (1, D_MODEL)), ("ln2_g", (1, D_MODEL)), ("ln2_b", (1, D_MODEL)), ("ln3_g", (1, D_MODEL)),
              ("ln3_b", (1, D_MODEL)), ("q_norm_g", (1, S5_W)), ("kv_norm_g", (1, S5_W)), ("s5_d", (1, S5_W)),
              ("s5_ab", (1, 2 * S5_N)), ("s5_bb_re", (S5_G, S5_P, S5_H)), ("s5_bb_im", (S5_G, S5_P, S5_H)),
              ("s5_c_re", (S5_G, S5_H, S5_P)), ("s5_c_im", (S5_G, S5_H, S5_P)))


def kernel(x, mem, positions, ln_in_g, ln_in_b, w_in, s5_lam_re, s5_lam_im, s5_log_dt, s5_b_re, s5_b_im, s5_c_re, s5_c_im, s5_d, w_glu, q_norm_g, w_uq, kv_norm_g, w_ukv, w_oa, w_o, ln1_g, ln1_b, w_xq, w_xk, w_xv, w_xo, ln2_g, ln2_b, w_up, w_down, ln3_g, ln3_b, loss_target, m_ln_in_g, m_ln_in_b, m_w_in, m_s5_lam_re, m_s5_lam_im, m_s5_log_dt, m_s5_b_re, m_s5_b_im, m_s5_c_re, m_s5_c_im, m_s5_d, m_w_glu, m_q_norm_g, m_w_uq, m_kv_norm_g, m_w_ukv, m_w_oa, m_w_o, m_ln1_g, m_ln1_b, m_w_xq, m_w_xk, m_w_xv, m_w_xo, m_ln2_g, m_ln2_b, m_w_up, m_w_down, m_ln3_g, m_ln3_b, v_ln_in_g, v_ln_in_b, v_w_in, v_s5_lam_re, v_s5_lam_im, v_s5_log_dt, v_s5_b_re, v_s5_b_im, v_s5_c_re, v_s5_c_im, v_s5_d, v_w_glu, v_q_norm_g, v_w_uq, v_kv_norm_g, v_w_ukv, v_w_oa, v_w_o, v_ln1_g, v_ln1_b, v_w_xq, v_w_xk, v_w_xv, v_w_xo, v_ln2_g, v_ln2_b, v_w_up, v_w_down, v_ln3_g, v_ln3_b):
    a = dict(locals())
    wts = {n: a[n] for n in WEIGHTS}
    ms = {n: a["m_" + n] for n in WEIGHTS}
    vs = {n: a["v_" + n] for n in WEIGHTS}
    cidx = lax.axis_index("c").astype(jnp.int32).reshape(1)

    shard2d = {n: wts[n].reshape(wts[n].shape[-2], wts[n].shape[-1]) for n in BIG}
    nrow = {n: shard2d[n].size // 1024 for n in BIG}
    pad_rows = (-sum(nrow.values())) % PACK_ROWS
    rows_half = (sum(nrow.values()) + pad_rows) // 2
    mine = 2 * lax.axis_index("x") + lax.axis_index("y")
    packed = jnp.concatenate([shard2d[n].astype(BF16).reshape(nrow[n], 1024) for n in BIG]
                             + [jnp.zeros((pad_rows, 1024), BF16)], axis=0).reshape(1, 2, rows_half, 1024)
    gathered = lax.dynamic_update_slice(_gather_weights(packed[0]), packed, (mine, 0, 0, 0))
    gathered = gathered.reshape(4, 2 * rows_half, 1024)
    W, off = {}, 0
    for n in BIG:
        r, cc = shard2d[n].shape
        piece = lax.optimization_barrier(gathered[:, off:off + nrow[n]])
        W[n] = _full_from_shards(n, piece.reshape(4, r, cc))
        off += nrow[n]

    sp = {n: wts[n] for n in SMALL}
    sp_local = {n: (sp[n][0] if sp[n].ndim > 1 else sp[n]) for n in SMALL}
    lossv, dx, gW, gs = _local_step(x[0], mem[0], positions[0], loss_target[0], W, sp_local)

    gflat = jnp.concatenate([_shards_from_full(n, gW[n]).reshape(4, nrow[n], 1024) for n in BIG]
                            + [jnp.zeros((4, pad_rows, 1024), F32)], axis=1).reshape(4, 2, rows_half, 1024)
    recv = _pair_exchange(gflat)
    psum = _pair_add(gflat, recv, cidx)
    own = lax.dynamic_slice(psum, (mine, 0, 0), (1, rows_half, 1024))
    slots = lax.dynamic_update_slice(_chip_exchange(psum), own, (mine, 0, 0))
    red = _chip_sum(slots)
    other = _pair_share(red)
    is_mine = (jnp.arange(2) == lax.axis_index("c"))[:, None, None]
    gsh = jnp.where(is_mine, red[None], other[None]).reshape(2 * rows_half, 1024)
    g_out, off = {}, 0
    for n in BIG:
        g_out[n] = lax.optimization_barrier(gsh[off:off + nrow[n]]).reshape(shard2d[n].shape)
        off += nrow[n]

    gs["loss"] = lossv
    raw = _pack_flat([gs[n].reshape(s) for n, s in _RAW_SMALL], 8 * 1024, F32)
    raw = _allreduce_small(raw.reshape(-1, 1024)).reshape(-1)
    rs = dict(zip([n for n, _ in _RAW_SMALL], _unpack_flat(raw, [s for _, s in _RAW_SMALL])))
    loss = rs["loss"].reshape(())
    _, disc_vjp = jax.vjp(_s5_discretize, sp_local["s5_lam_re"], sp_local["s5_lam_im"], sp_local["s5_log_dt"],
                          sp_local["s5_b_re"], sp_local["s5_b_im"])
    d_ab = rs["s5_ab"].reshape(2, S5_G, S5_P)
    g_lre, g_lim, g_ldt, g_bre, g_bim = disc_vjp((d_ab[0], d_ab[1], rs["s5_bb_re"], rs["s5_bb_im"]))
    small_g = {"s5_lam_re": g_lre, "s5_lam_im": g_lim, "s5_log_dt": g_ldt, "s5_b_re": g_bre, "s5_b_im": g_bim,
               "s5_c_re": rs["s5_c_re"], "s5_c_im": rs["s5_c_im"]}
    for n in ("ln_in_g", "ln_in_b", "ln1_g", "ln1_b", "ln2_g", "ln2_b", "ln3_g", "ln3_b", "q_norm_g", "kv_norm_g", "s5_d"):
        small_g[n] = rs[n]

    grads, deltas, new_m, new_v = {}, {}, {}, {}
    for n in BIG:
        d_, m_, v_ = _adamw(shard2d[n], g_out[n], ms[n].reshape(shard2d[n].shape), vs[n].reshape(shard2d[n].shape), "adamw_" + n)
        grads[n] = g_out[n].reshape(wts[n].shape)
        deltas[n], new_m[n], new_v[n] = (t.reshape(wts[n].shape) for t in (d_, m_, v_))
    packs = [_pack_flat([t[n] for n in SMALL], 8 * 1024, F32).reshape(-1, 1024)
             for t in (wts, {n: small_g[n].reshape(wts[n].shape) for n in SMALL}, ms, vs)]
    sd, sm, sv = _adamw(*packs, "adamw_small")
    shapes = [wts[n].shape for n in SMALL]
    for n, d_, m_, v_ in zip(SMALL, _unpack_flat(sd.reshape(-1), shapes), _unpack_flat(sm.reshape(-1), shapes),
                             _unpack_flat(sv.reshape(-1), shapes)):
        grads[n] = small_g[n].reshape(wts[n].shape)
        deltas[n], new_m[n], new_v[n] = d_, m_, v_

    return (loss, dx[None], *[grads[n] for n in WEIGHTS], *[deltas[n] for n in WEIGHTS],
            *[new_m[n] for n in WEIGHTS], *[new_v[n] for n in WEIGHTS])
```

```python
import functools
import math

import jax
import jax.numpy as jnp
from jax import lax
from jax.experimental import pallas as pl
from jax.experimental.pallas import tpu as pltpu

F32 = jnp.float32
BF16 = jnp.bfloat16
MESH = pl.DeviceIdType.MESH

D_MODEL = 1024
S5_W = 256
S5_G = 16
S5_H = 16
S5_P = 64
S5_N = S5_G * S5_P
S5_MAX_RE = -1e-4
HEADS = 8
NOPE = 64
ROPE = 32
QK = NOPE + ROPE
VD = 64
HP = 128
XH = 4
XD = 256
LN_EPS = 1e-5
RMS_EPS = 1e-6
NEG_INF = -1e30
DN_ALPHA = 2.0 ** 0.25
ROPE_THETA = 10000.0
ADAM_LR, ADAM_B1, ADAM_B2, ADAM_EPS, ADAM_WD, ADAM_STEP = 0.001, 0.9, 0.999, 1e-08, 0.01, 10

Z_GS, Z_GA, Z_U, Z_CQ, Z_CKV, Z_KR, Z_W = 0, 1024, 2048, 2304, 2560, 2816, 2944
SCAN_CHUNK = 256

BIG = ("w_in", "w_glu", "w_uq", "w_ukv", "w_oa", "w_up", "w_o", "w_xq", "w_xk", "w_xv", "w_xo", "w_down")
COL_SHARDED = ("w_in", "w_glu", "w_uq", "w_ukv", "w_oa", "w_up")
SMALL = ("ln_in_g", "ln_in_b", "s5_lam_re", "s5_lam_im", "s5_log_dt", "s5_b_re", "s5_b_im", "s5_c_re", "s5_c_im",
         "s5_d", "q_norm_g", "kv_norm_g", "ln1_g", "ln1_b", "ln2_g", "ln2_b", "ln3_g", "ln3_b")
WEIGHTS = ("ln_in_g", "ln_in_b", "w_in", "s5_lam_re", "s5_lam_im", "s5_log_dt", "s5_b_re", "s5_b_im", "s5_c_re",
           "s5_c_im", "s5_d", "w_glu", "q_norm_g", "w_uq", "kv_norm_g", "w_ukv", "w_oa", "w_o", "ln1_g", "ln1_b",
           "w_xq", "w_xk", "w_xv", "w_xo", "ln2_g", "ln2_b", "w_up", "w_down", "ln3_g", "ln3_b")
PACK_ROWS = 2 * 16


def _row_tile(n, cap=512):
    best = n
    for t in range(8, min(n, cap) + 1, 8):
        if n % t == 0:
            best = t
    return best


def _pick(n, cap):
    best = None
    for t in range(128, min(n, cap) + 1, 128):
        if n % t == 0:
            best = t
    return best if best is not None and (best >= 512 or best == n) else n


def _mm(a, b, name, mode="nn", *, a_off=0, b_off=0, m=None, n=None, act=None, epi=None, extras=(),
        out_dtype=F32, tm=1024, tn=1024, tk=None):
    if mode == "nn":
        M, (K, N) = a.shape[0], b.shape
    elif mode == "nt":
        M, (N, K) = a.shape[0], b.shape
    else:
        K, M, N = a.shape[0], m, n
    if mode == "tn":
        tm, tn, tk = _pick(M, tm), _pick(N, tn), min(tk or 512, K)
    else:
        tm, tn, tk = min(tm, M), _pick(N, tn), _pick(K, tk or 1024)
    assert M % tm == 0 and N % tn == 0 and K % tk == 0, (name, M, N, K, tm, tn, tk)
    nk = K // tk
    if mode == "tn":
        assert a_off % tm == 0 and b_off % tn == 0
        ao, bo = a_off // tm, b_off // tn
        a_spec = pl.BlockSpec((tk, tm), lambda i, j, k: (k, i + ao))
        b_spec = pl.BlockSpec((tk, tn), lambda i, j, k: (k, j + bo))
        dims = (((0,), (0,)), ((), ()))
    else:
        assert a_off % tk == 0
        ao = a_off // tk
        a_spec = pl.BlockSpec((tm, tk), lambda i, j, k: (i, k + ao))
        if mode == "nn":
            b_spec = pl.BlockSpec((tk, tn), lambda i, j, k: (k, j))
            dims = (((1,), (0,)), ((), ()))
        else:
            b_spec = pl.BlockSpec((tn, tk), lambda i, j, k: (j, k))
            dims = (((1,), (1,)), ((), ()))
    e_spec = pl.BlockSpec((tm, tn), lambda i, j, k: (i, j))
    n_extra = len(extras)

    def body(*refs):
        a_ref, b_ref = refs[0], refs[1]
        e_refs = refs[2:2 + n_extra]
        o_ref = refs[2 + n_extra]
        av = a_ref[...]
        if act is not None:
            av = act(av.astype(F32))
        p = lax.dot_general(av.astype(BF16), b_ref[...].astype(BF16), dims, preferred_element_type=F32)

        def finish(r):
            if epi is not None:
                r = epi(r, *[e[...] for e in e_refs])
            o_ref[...] = r.astype(o_ref.dtype)

        if nk == 1:
            finish(p)
        else:
            acc = refs[3 + n_extra]
            k = pl.program_id(2)

            @pl.when(k == 0)
            def _():
                acc[...] = p

            @pl.when(k > 0)
            def _():
                acc[...] += p

            @pl.when(k == nk - 1)
            def _():
                finish(acc[...])

    return pl.pallas_call(
        body, name=name, grid=(M // tm, N // tn, nk),
        in_specs=[a_spec, b_spec] + [e_spec] * n_extra, out_specs=e_spec,
        out_shape=jax.ShapeDtypeStruct((M, N), out_dtype),
        scratch_shapes=[pltpu.VMEM((tm, tn), F32)] if nk > 1 else [],
        compiler_params=pltpu.CompilerParams(dimension_semantics=("parallel", "parallel", "arbitrary")),
    )(a, b, *extras)


def _rowwise(fn, name, rows, pars, outs, accs=(), tm=256, upcast=True):
    T = rows[0][0].shape[0]
    tm = min(tm, T)
    assert T % tm == 0
    n_in, n_out = len(rows) + len(pars), len(outs)
    in_specs = []
    for arr, off, w in rows:
        assert off % w == 0 and arr.shape[0] == T, name
        in_specs.append(pl.BlockSpec((tm, w), functools.partial(lambda i, o: (i, o), o=off // w)))
    for p in pars:
        in_specs.append(pl.BlockSpec(p.shape, lambda i: (0, 0)))
    out_specs = [pl.BlockSpec((tm, w), lambda i: (i, 0)) for w, _ in outs]
    out_specs += [pl.BlockSpec(s, lambda i: (0, 0)) for s in accs]
    out_shape = [jax.ShapeDtypeStruct((T, w), dt) for w, dt in outs] + [jax.ShapeDtypeStruct(s, F32) for s in accs]

    def body(*refs):
        res = fn(*[r[...].astype(F32) if upcast else r[...] for r in refs[:n_in]])
        o_refs = refs[n_in:]
        for r, v in zip(o_refs[:n_out], res[:n_out]):
            r[...] = v.astype(r.dtype)
        if accs:
            i = pl.program_id(0)

            @pl.when(i == 0)
            def _():
                for r in o_refs[n_out:]:
                    r[...] = jnp.zeros_like(r)

            for r, v in zip(o_refs[n_out:], res[n_out:]):
                r[...] += v

    res = pl.pallas_call(
        body, name=name, grid=(T // tm,), in_specs=in_specs, out_specs=out_specs, out_shape=out_shape,
        compiler_params=pltpu.CompilerParams(dimension_semantics=("arbitrary",)),
    )(*[r[0] for r in rows], *pars)
    return res


def _whole(arr):
    return (arr, 0, arr.shape[1])


def _ln(x, g, b):
    mu = jnp.mean(x, axis=-1, keepdims=True)
    xc = x - mu
    var = jnp.mean(xc * xc, axis=-1, keepdims=True)
    return xc * lax.rsqrt(var + LN_EPS) * g + b


def _ln_res(h, r, g, b):
    return _ln(DN_ALPHA * h + r, g, b)


def _rms(x, g):
    return x * lax.rsqrt(jnp.mean(x * x, axis=-1, keepdims=True) + RMS_EPS) * g


def _gelu_skip(yc, u, d):
    y = yc + d * u
    return 0.5 * y * (1.0 + lax.erf(y * (1.0 / math.sqrt(2.0))))


def _gate_mix(gs, ga, y2a, y2b, aout):
    return jax.nn.sigmoid(gs) * (y2a * jax.nn.sigmoid(y2b)) + jax.nn.sigmoid(ga) * aout


def _relu2(a):
    r = jnp.maximum(a, 0.0)
    return r * r


def _tile8(t):
    return jnp.concatenate([t] * HEADS, axis=1)


def _rope(x, cos, s1, s2):
    w = x.shape[1]
    return x * cos + pltpu.roll(x, ROPE // 2, 1) * s1 + pltpu.roll(x, w - ROPE // 2, 1) * s2


def _rope_t(dy, cos, s1, s2):
    w = dy.shape[1]
    return dy * cos + pltpu.roll(dy * s1, w - ROPE // 2, 1) + pltpu.roll(dy * s2, ROPE // 2, 1)


def _rope_lanes(shape):
    lane = lax.broadcasted_iota(jnp.int32, shape, 1) % HP
    return (lane >= NOPE) & (lane < QK)


ATT_HB = 2
ATT_SCALE = QK ** -0.5
LOG2E = 1.4426950408889634
LN2 = 0.6931471805599453
_NT = (((1,), (1,)), ((), ()))
_TN = (((0,), (0,)), ((), ()))


def _tri_tables(n, by_row):
    if by_row:
        pairs = [(i, j) for i in range(n) for j in range(i + 1)]
    else:
        pairs = [(i, j) for j in range(n) for i in range(j, n)]
    return jnp.array([p[0] for p in pairs], jnp.int32), jnp.array([p[1] for p in pairs], jnp.int32)


def _flash_fwd(q, k, v, name, bq=512):
    T = q.shape[0]
    bq = min(bq, T)
    nq = T // bq
    hb = ATT_HB

    i_tab, j_tab = _tri_tables(nq, by_row=True)

    def body(it_ref, jt_ref, q_ref, k_ref, v_ref, o_ref, lse_ref, m_sc, l_sc, acc_sc):
        t = pl.program_id(1)
        i, j = it_ref[t], jt_ref[t]

        @pl.when(j == 0)
        def _():
            m_sc[...] = jnp.full_like(m_sc, NEG_INF)
            l_sc[...] = jnp.zeros_like(l_sc)
            acc_sc[...] = jnp.zeros_like(acc_sc)

        def step(masked):
            for hh in range(hb):
                sl = slice(hh * HP, (hh + 1) * HP)
                s = lax.dot_general(q_ref[:, sl], k_ref[:, sl], _NT, preferred_element_type=F32) * (ATT_SCALE * LOG2E)
                if masked:
                    row = lax.broadcasted_iota(jnp.int32, (bq, bq), 0)
                    col = lax.broadcasted_iota(jnp.int32, (bq, bq), 1)
                    s = jnp.where(col <= row, s, NEG_INF)
                m_prev = m_sc[hh]
                m_new = jnp.maximum(m_prev, jnp.max(s, axis=1, keepdims=True))
                alpha = jnp.exp2(m_prev - m_new)
                p = jnp.exp2(s - jnp.concatenate([m_new] * (bq // HP), axis=1))
                l_sc[hh] = alpha * l_sc[hh] + jnp.sum(p, axis=1, keepdims=True)
                acc_sc[hh] = alpha * acc_sc[hh] + jnp.dot(p.astype(BF16), v_ref[:, sl], preferred_element_type=F32)
                m_sc[hh] = m_new

        @pl.when(j < i)
        def _():
            step(False)

        @pl.when(j == i)
        def _():
            step(True)
            for hh in range(hb):
                o_ref[:, hh * HP:(hh + 1) * HP] = acc_sc[hh] / l_sc[hh]
                lse_ref[hh] = jnp.transpose((m_sc[hh] + jnp.log2(l_sc[hh])) * LN2)[0:1, :]

    qs = pl.BlockSpec((bq, hb * HP), lambda h, t, it, jt: (it[t], h))
    ks = pl.BlockSpec((bq, hb * HP), lambda h, t, it, jt: (jt[t], h))
    gs = pltpu.PrefetchScalarGridSpec(
        num_scalar_prefetch=2, grid=(HEADS // hb, i_tab.shape[0]), in_specs=[qs, ks, ks],
        out_specs=[qs, pl.BlockSpec((hb, 1, bq), lambda h, t, it, jt: (h, 0, it[t]))],
        scratch_shapes=[pltpu.VMEM((hb, bq, HP), F32)] * 3)
    return pl.pallas_call(
        body, name=name, grid_spec=gs,
        out_shape=[jax.ShapeDtypeStruct((T, HEADS * HP), F32), jax.ShapeDtypeStruct((HEADS, 1, T), F32)],
        compiler_params=pltpu.CompilerParams(dimension_semantics=("parallel", "arbitrary")),
    )(i_tab, j_tab, q, k, v)


def _attn_delta_fn(do, o):
    prod = do * o
    cols = [jnp.sum(prod[:, h * HP:(h + 1) * HP], axis=1, keepdims=True) for h in range(HEADS)]
    return jnp.concatenate(cols, axis=1), do


def _flash_bwd(q, k, v, do, lse_row, delta_row, name, bq=512):
    T = q.shape[0]
    bq = min(bq, T)
    nq = T // bq
    hb = ATT_HB

    i_tab, j_tab = _tri_tables(nq, by_row=False)
    n_blk = i_tab.shape[0]

    def body(it_ref, jt_ref, q_ref, k_ref, v_ref, do_ref, lse_ref, dl_ref, dq_ref, dk_ref, dv_ref):
        t = pl.program_id(1)
        i, j = it_ref[t], jt_ref[t]

        @pl.when(t == 0)
        def _():
            dq_ref[...] = jnp.zeros_like(dq_ref)

        @pl.when(i == j)
        def _():
            dk_ref[...] = jnp.zeros_like(dk_ref)
            dv_ref[...] = jnp.zeros_like(dv_ref)

        def step(masked):
            rows = pl.ds(pl.multiple_of(i * bq, bq), bq)
            for hh in range(hb):
                sl = slice(hh * HP, (hh + 1) * HP)
                qh, kh, doh = q_ref[:, sl], k_ref[:, sl], do_ref[:, sl]
                st = lax.dot_general(kh, qh, _NT, preferred_element_type=F32) * (ATT_SCALE * LOG2E)
                pt = jnp.exp2(st - lse_ref[hh] * LOG2E)
                if masked:
                    krow = lax.broadcasted_iota(jnp.int32, (bq, bq), 0)
                    qcol = lax.broadcasted_iota(jnp.int32, (bq, bq), 1)
                    pt = jnp.where(krow <= qcol, pt, 0.0)
                dv_ref[:, sl] += jnp.dot(pt.astype(BF16), doh, preferred_element_type=F32)
                dpt = lax.dot_general(v_ref[:, sl], doh, _NT, preferred_element_type=F32)
                dst = (pt * (dpt - dl_ref[hh])).astype(BF16)
                dk_ref[:, sl] += jnp.dot(dst, qh, preferred_element_type=F32)
                dq_ref[rows, sl] += lax.dot_general(dst, kh, _TN, preferred_element_type=F32)

        @pl.when(i > j)
        def _():
            step(False)

        @pl.when(i == j)
        def _():
            step(True)

        @pl.when(i == nq - 1)
        def _():
            dk_ref[...] *= ATT_SCALE

        @pl.when(t == n_blk - 1)
        def _():
            dq_ref[...] *= ATT_SCALE

    qs = pl.BlockSpec((bq, hb * HP), lambda h, t, it, jt: (it[t], h))
    ks = pl.BlockSpec((bq, hb * HP), lambda h, t, it, jt: (jt[t], h))
    rs = pl.BlockSpec((hb, 1, bq), lambda h, t, it, jt: (h, 0, it[t]))
    full = pl.BlockSpec((T, hb * HP), lambda h, t, it, jt: (0, h))
    gs = pltpu.PrefetchScalarGridSpec(num_scalar_prefetch=2, grid=(HEADS // hb, n_blk),
                                      in_specs=[qs, ks, ks, qs, rs, rs], out_specs=[full, ks, ks])
    return pl.pallas_call(
        body, name=name, grid_spec=gs, out_shape=[jax.ShapeDtypeStruct((T, HEADS * HP), F32)] * 3,
        compiler_params=pltpu.CompilerParams(dimension_semantics=("parallel", "arbitrary")),
    )(i_tab, j_tab, q, k, v, do, lse_row, delta_row)


def _xattn_heads(q, k, v):
    scale = XD ** -0.5
    ps = []
    for h in range(XH):
        sl = slice(h * XD, (h + 1) * XD)
        s = lax.dot_general(q[:, sl].astype(BF16), k[:, sl].astype(BF16), (((1,), (1,)), ((), ())),
                            preferred_element_type=F32) * scale
        e = jnp.exp(s - jnp.max(s, axis=1, keepdims=True))
        ps.append(e / jnp.sum(e, axis=1, keepdims=True))
    return ps


def _xattn_fwd_fn(q, k, v):
    ps = _xattn_heads(q, k, v)
    o = [jnp.dot(p.astype(BF16), v[:, h * XD:(h + 1) * XD].astype(BF16), preferred_element_type=F32)
         for h, p in enumerate(ps)]
    return (jnp.concatenate(o, axis=1),)


def _xattn_bwd_fn(q, do, k, v):
    scale = XD ** -0.5
    ps = _xattn_heads(q, k, v)
    dqs, dks, dvs = [], [], []
    tdims = (((0,), (0,)), ((), ()))
    for h, p in enumerate(ps):
        sl = slice(h * XD, (h + 1) * XD)
        doh = do[:, sl].astype(BF16)
        dvs.append(lax.dot_general(p.astype(BF16), doh, tdims, preferred_element_type=F32))
        dp = lax.dot_general(doh, v[:, sl].astype(BF16), (((1,), (1,)), ((), ())), preferred_element_type=F32)
        ds = (p * (dp - jnp.sum(dp * p, axis=1, keepdims=True)) * scale).astype(BF16)
        dqs.append(jnp.dot(ds, k[:, sl].astype(BF16), preferred_element_type=F32))
        dks.append(lax.dot_general(ds, q[:, sl].astype(BF16), tdims, preferred_element_type=F32))
    return jnp.concatenate(dqs, axis=1), jnp.concatenate(dks, axis=1), jnp.concatenate(dvs, axis=1)


S5_SUB = 8


def _cmul_add(xr, xi, ar, ai, sr, si):
    return xr + ar * sr - ai * si, xi + ar * si + ai * sr


def _s5_tables(ar, ai, reverse):
    pows = [(ar, ai)]
    for _ in range(S5_SUB - 1):
        pr, pi = pows[-1]
        pows.append((pr * ar - pi * ai, pr * ai + pi * ar))
    cr = jnp.concatenate([p[0] for p in pows], axis=0)
    ci = jnp.concatenate([p[1] for p in pows], axis=0)
    if reverse:
        cr, ci = cr[::-1], ci[::-1]
    t = jnp.arange(S5_SUB)[:, None]
    blocks = [jnp.concatenate([cr, ci], axis=1)]
    for s in (1, 2, 4):
        keep = (t < S5_SUB - s) if reverse else (t >= s)
        sr, si = pows[s - 1]
        blocks.append(jnp.concatenate([jnp.where(keep, sr, 0.0), jnp.where(keep, si, 0.0)], axis=1))
    return jnp.concatenate(blocks, axis=0)


def _sub_scan(xr, xi, tab_ref, reverse):
    for k, s in enumerate((1, 2, 4)):
        blk = slice(S5_SUB * (k + 1), S5_SUB * (k + 2))
        sh = S5_SUB - s if reverse else s
        xr, xi = _cmul_add(xr, xi, tab_ref[blk, :S5_N], tab_ref[blk, S5_N:], pltpu.roll(xr, sh, 0), pltpu.roll(xi, sh, 0))
    return xr, xi


def _s5_fwd(z, b_st, c_st, d_skip, tab, name, tc=512):
    T = z.shape[0]
    tc = min(tc, T)
    nsub = tc // S5_SUB

    def body(u_ref, bst_ref, cst_ref, d_ref, tab_ref, h_ref, yc_ref, yg_ref, carry, x_sc):
        @pl.when(pl.program_id(0) == 0)
        def _():
            carry[...] = jnp.zeros_like(carry)

        u = u_ref[...].astype(F32)
        x_sc[...] = jnp.dot(u.astype(BF16), bst_ref[...], preferred_element_type=F32)

        def sub(b, c):
            r = pl.ds(pl.multiple_of(b * S5_SUB, S5_SUB), S5_SUB)
            xr, xi = _sub_scan(x_sc[r, :S5_N], x_sc[r, S5_N:], tab_ref, False)
            hr, hi = _cmul_add(xr, xi, tab_ref[0:S5_SUB, :S5_N], tab_ref[0:S5_SUB, S5_N:], c[0], c[1])
            x_sc[r, :S5_N] = hr
            x_sc[r, S5_N:] = hi
            return hr[S5_SUB - 1:S5_SUB], hi[S5_SUB - 1:S5_SUB]

        cr, ci = lax.fori_loop(0, nsub, sub, (carry[0:1, :S5_N], carry[0:1, S5_N:]))
        carry[0:1, :S5_N] = cr
        carry[0:1, S5_N:] = ci
        h = x_sc[...]
        h_ref[...] = h
        yc = jnp.dot(h.astype(BF16), cst_ref[...], preferred_element_type=F32)
        yc_ref[...] = yc
        yg_ref[...] = _gelu_skip(yc, u, d_ref[...]).astype(yg_ref.dtype)

    whole = lambda a: pl.BlockSpec(a.shape, lambda i: (0, 0))
    row = lambda w: pl.BlockSpec((tc, w), lambda i: (i, 0))
    return pl.pallas_call(
        body, name=name, grid=(T // tc,),
        in_specs=[pl.BlockSpec((tc, S5_W), lambda i: (i, Z_U // S5_W)), whole(b_st), whole(c_st), whole(d_skip), whole(tab)],
        out_specs=[row(2 * S5_N), row(S5_W), row(S5_W)],
        out_shape=[jax.ShapeDtypeStruct((T, 2 * S5_N), F32), jax.ShapeDtypeStruct((T, S5_W), F32),
                   jax.ShapeDtypeStruct((T, S5_W), BF16)],
        scratch_shapes=[pltpu.VMEM((8, 2 * S5_N), F32), pltpu.VMEM((tc, 2 * S5_N), F32)],
        compiler_params=pltpu.CompilerParams(dimension_semantics=("arbitrary",)),
    )(z, b_st, c_st, d_skip, tab)


def _s5_bwd(dyg, yc, z, h, b_st, c_st, d_skip, tab, name, tc=512):
    T = z.shape[0]
    tc = min(tc, T)
    nc, nsub = T // tc, tc // S5_SUB

    def body(dyg_ref, yc_ref, u_ref, h_ref, hp_ref, bst_ref, cst_ref, d_ref, tab_ref,
             du_ref, da_ref, db_ref, dct_ref, dd_ref, carry, x_sc):
        g = pl.program_id(0)

        @pl.when(g == 0)
        def _():
            carry[...] = jnp.zeros_like(carry)
            for r in (da_ref, db_ref, dct_ref, dd_ref):
                r[...] = jnp.zeros_like(r)

        u = u_ref[...].astype(F32)
        _, vjp = jax.vjp(_gelu_skip, yc_ref[...], u, d_ref[...])
        dyc, du_skip, dd = vjp(dyg_ref[...])
        dd_ref[...] += dd
        dyc16 = dyc.astype(BF16)
        dct_ref[...] += lax.dot_general(dyc16, h_ref[...].astype(BF16), _TN, preferred_element_type=F32)
        x_sc[...] = lax.dot_general(dyc16, cst_ref[...], _NT, preferred_element_type=F32)
        first = jnp.where(g == nc - 1, 0.0, 1.0)
        row0 = lax.broadcasted_iota(jnp.int32, (S5_SUB, S5_N), 0) == 0

        def sub(k, c):
            cr, ci, acc_r, acc_i = c
            b = nsub - 1 - k
            r = pl.ds(pl.multiple_of(b * S5_SUB, S5_SUB), S5_SUB)
            xr, xi = _sub_scan(x_sc[r, :S5_N], x_sc[r, S5_N:], tab_ref, True)
            lr, li = _cmul_add(xr, xi, tab_ref[0:S5_SUB, :S5_N], tab_ref[0:S5_SUB, S5_N:], cr, ci)
            x_sc[r, :S5_N] = lr
            x_sc[r, S5_N:] = li
            rp = pl.ds(pl.multiple_of(jnp.maximum(b - 1, 0) * S5_SUB, S5_SUB), S5_SUB)
            last_r = jnp.where(b == 0, hp_ref[S5_SUB - 1:S5_SUB, :S5_N] * first, h_ref[rp, :S5_N][S5_SUB - 1:S5_SUB])
            last_i = jnp.where(b == 0, hp_ref[S5_SUB - 1:S5_SUB, S5_N:] * first, h_ref[rp, S5_N:][S5_SUB - 1:S5_SUB])
            pr = jnp.where(row0, last_r, pltpu.roll(h_ref[r, :S5_N], 1, 0))
            pi = jnp.where(row0, last_i, pltpu.roll(h_ref[r, S5_N:], 1, 0))
            return lr[0:1], li[0:1], acc_r + (lr * pr + li * pi), acc_i + (li * pr - lr * pi)

        zero = jnp.zeros((S5_SUB, S5_N), F32)
        cr, ci, acc_r, acc_i = lax.fori_loop(0, nsub, sub, (carry[0:1, :S5_N], carry[0:1, S5_N:], zero, zero))
        carry[0:1, :S5_N] = cr
        carry[0:1, S5_N:] = ci
        da_ref[0:1, :S5_N] += jnp.sum(acc_r, axis=0, keepdims=True)
        da_ref[0:1, S5_N:] += jnp.sum(acc_i, axis=0, keepdims=True)
        lam16 = x_sc[...].astype(BF16)
        db_ref[...] += lax.dot_general(u.astype(BF16), lam16, _TN, preferred_element_type=F32)
        du = lax.dot_general(lam16, bst_ref[...], _NT, preferred_element_type=F32) + du_skip
        du_ref[...] = du.astype(du_ref.dtype)

    whole = lambda a: pl.BlockSpec(a.shape, lambda g: (0, 0))
    row = lambda w, off=0: pl.BlockSpec((tc, w), lambda g: (nc - 1 - g, off))
    prev = pl.BlockSpec((S5_SUB, 2 * S5_N), lambda g: (jnp.maximum((nc - 1 - g) * nsub - 1, 0), 0))
    acc = lambda s: pl.BlockSpec(s, lambda g: (0, 0))
    return pl.pallas_call(
        body, name=name, grid=(nc,),
        in_specs=[row(S5_W), row(S5_W), row(S5_W, Z_U // S5_W), row(2 * S5_N), prev, whole(b_st), whole(c_st),
                  whole(d_skip), whole(tab)],
        out_specs=[row(S5_W), acc((1, 2 * S5_N)), acc((S5_W, 2 * S5_N)), acc((S5_W, 2 * S5_N)), acc((1, S5_W))],
        out_shape=[jax.ShapeDtypeStruct((T, S5_W), BF16), jax.ShapeDtypeStruct((1, 2 * S5_N), F32),
                   jax.ShapeDtypeStruct((S5_W, 2 * S5_N), F32), jax.ShapeDtypeStruct((S5_W, 2 * S5_N), F32),
                   jax.ShapeDtypeStruct((1, S5_W), F32)],
        scratch_shapes=[pltpu.VMEM((8, 2 * S5_N), F32), pltpu.VMEM((tc, 2 * S5_N), F32)],
        compiler_params=pltpu.CompilerParams(dimension_semantics=("arbitrary",)),
    )(dyg, yc, z, h, h, b_st, c_st, d_skip, tab)


def _s5_discretize(lam_re, lam_im, log_dt, b_re, b_im):
    lr = jnp.minimum(lam_re, S5_MAX_RE)
    li = lam_im
    dt = jnp.exp(log_dt)[:, None]
    mag = jnp.exp(lr * dt)
    ang = li * dt
    ab_re = mag * jnp.cos(ang)
    ab_im = mag * jnp.sin(ang)
    den = lr * lr + li * li
    nr = ab_re - 1.0
    f_re = ((nr * lr + ab_im * li) / den)[..., None]
    f_im = ((ab_im * lr - nr * li) / den)[..., None]
    return ab_re, ab_im, f_re * b_re - f_im * b_im, f_re * b_im + f_im * b_re


def _adamw_fn(w, g, m, v):
    m = ADAM_B1 * m + (1.0 - ADAM_B1) * g
    v = ADAM_B2 * v + (1.0 - ADAM_B2) * (g * g)
    m_hat = m / (1.0 - ADAM_B1 ** ADAM_STEP)
    v_hat = v / (1.0 - ADAM_B2 ** ADAM_STEP)
    delta = -ADAM_LR * (m_hat / (jnp.sqrt(v_hat) + ADAM_EPS) + ADAM_WD * w)
    return delta, m, v


def _adamw(w, g, m, v, name):
    wd = w.shape[1]
    return _rowwise(_adamw_fn, name, [_whole(w), _whole(g), _whole(m), _whole(v)], [], [(wd, F32)] * 3,
                    tm=_row_tile(w.shape[0]))


def _me():
    return lax.axis_index("x"), lax.axis_index("y"), lax.axis_index("c")


def _chip_of(j):
    return j // 2, j % 2


_HBM = pl.BlockSpec(memory_space=pl.ANY)


def _gather_weights(shard):
    _, R, W = shard.shape

    def body(s_ref, out_ref, send_sems, recv_sems):
        x, y, c = _me()
        mine = 2 * x + y
        sib = (x, y, 1 - c)
        chips = [(1 - x, y), (x, 1 - y), (1 - x, 1 - y)]

        def rcopy(kk, src, chip_idx, half, to):
            return pltpu.make_async_remote_copy(src_ref=src, dst_ref=out_ref.at[chip_idx, half],
                                                send_sem=send_sems.at[kk], recv_sem=recv_sems.at[kk],
                                                device_id=to, device_id_type=MESH)

        first = [rcopy(j, s_ref.at[c], mine, c, (*chip, c)) for j, chip in enumerate(chips)]
        for cp in first:
            cp.start()
        passed = []
        for j, (px, py) in enumerate(chips):
            src_chip = 2 * px + py
            rcopy(j, s_ref.at[c], src_chip, c, (x, y, c)).wait_recv()
            fwd = rcopy(3 + j, out_ref.at[src_chip, c], src_chip, c, sib)
            fwd.start()
            passed.append(fwd)
        for j, (px, py) in enumerate(chips):
            rcopy(3 + j, s_ref.at[c], 2 * px + py, 1 - c, (x, y, c)).wait_recv()
        for cp in first + passed:
            cp.wait_send()

    return pl.pallas_call(
        body, name="gather_weights", in_specs=[_HBM], out_specs=_HBM,
        out_shape=jax.ShapeDtypeStruct((4, 2, R, W), shard.dtype),
        scratch_shapes=[pltpu.SemaphoreType.DMA((6,)), pltpu.SemaphoreType.DMA((6,))],
    )(shard)


def _pair_exchange(g):
    _, _, R, W = g.shape

    def body(g_ref, out_ref, send_sem, recv_sem):
        x, y, c = _me()
        cp = pltpu.make_async_remote_copy(src_ref=g_ref.at[:, 1 - c], dst_ref=out_ref, send_sem=send_sem,
                                          recv_sem=recv_sem, device_id=(x, y, 1 - c), device_id_type=MESH)
        cp.start()
        cp.wait()

    return pl.pallas_call(
        body, name="grad_pair_exchange", in_specs=[_HBM], out_specs=_HBM,
        out_shape=jax.ShapeDtypeStruct((4, R, W), g.dtype),
        scratch_shapes=[pltpu.SemaphoreType.DMA(()), pltpu.SemaphoreType.DMA(())],
    )(g)


def _pair_add(g, recv, cidx):
    _, _, R, W = g.shape
    tr = _row_tile(R)

    def body(c_ref, a_ref, b_ref, o_ref):
        o_ref[...] = (a_ref[...] + b_ref[...]).astype(o_ref.dtype)

    gs = pltpu.PrefetchScalarGridSpec(
        num_scalar_prefetch=1, grid=(4, R // tr),
        in_specs=[pl.BlockSpec((None, None, tr, W), lambda j, r, c: (j, c[0], r, 0)),
                  pl.BlockSpec((None, tr, W), lambda j, r, c: (j, r, 0))],
        out_specs=pl.BlockSpec((None, tr, W), lambda j, r, c: (j, r, 0)))
    return pl.pallas_call(body, name="grad_pair_add", grid_spec=gs,
                          out_shape=jax.ShapeDtypeStruct((4, R, W), BF16))(cidx, g, recv)


def _chip_exchange(p):
    _, R, W = p.shape

    def body(p_ref, out_ref, send_sems, recv_sems):
        x, y, c = _me()
        mine = 2 * x + y
        chips = [(1 - x, y), (x, 1 - y), (1 - x, 1 - y)]
        cps = []
        for j, (px, py) in enumerate(chips):
            cp = pltpu.make_async_remote_copy(src_ref=p_ref.at[2 * px + py], dst_ref=out_ref.at[mine],
                                              send_sem=send_sems.at[j], recv_sem=recv_sems.at[j],
                                              device_id=(px, py, c), device_id_type=MESH)
            cp.start()
            cps.append(cp)
        for j, (px, py) in enumerate(chips):
            pltpu.make_async_remote_copy(src_ref=p_ref.at[mine], dst_ref=out_ref.at[2 * px + py],
                                         send_sem=send_sems.at[j], recv_sem=recv_sems.at[j],
                                         device_id=(px, py, c), device_id_type=MESH).wait_recv()
        for cp in cps:
            cp.wait_send()

    return pl.pallas_call(
        body, name="grad_chip_exchange", in_specs=[_HBM], out_specs=_HBM,
        out_shape=jax.ShapeDtypeStruct((4, R, W), p.dtype),
        scratch_shapes=[pltpu.SemaphoreType.DMA((3,)), pltpu.SemaphoreType.DMA((3,))],
    )(p)


def _chip_sum(q):
    _, R, W = q.shape
    tr = _row_tile(R)

    def body(q_ref, o_ref):
        f = lambda t: q_ref[t].astype(F32)
        o_ref[...] = ((f(0) + f(1)) + f(2)) + f(3)

    return pl.pallas_call(body, name="grad_chip_sum", grid=(R // tr,),
                          in_specs=[pl.BlockSpec((4, tr, W), lambda r: (0, r, 0))],
                          out_specs=pl.BlockSpec((tr, W), lambda r: (r, 0)),
                          out_shape=jax.ShapeDtypeStruct((R, W), F32))(q)


def _pair_share(r):
    R, W = r.shape

    def body(r_ref, out_ref, send_sem, recv_sem):
        x, y, c = _me()
        cp = pltpu.make_async_remote_copy(src_ref=r_ref, dst_ref=out_ref, send_sem=send_sem, recv_sem=recv_sem,
                                          device_id=(x, y, 1 - c), device_id_type=MESH)
        cp.start()
        cp.wait()

    return pl.pallas_call(
        body, name="grad_pair_share", in_specs=[_HBM], out_specs=_HBM,
        out_shape=jax.ShapeDtypeStruct((R, W), r.dtype),
        scratch_shapes=[pltpu.SemaphoreType.DMA(()), pltpu.SemaphoreType.DMA(())],
    )(r)


def _allreduce_small(vec):
    R, W = vec.shape

    def body(v_ref, out_ref, buf, send_sems, recv_sems):
        x, y, c = _me()
        me = 4 * x + 2 * y + c
        buf[me] = v_ref[...]
        cps = []
        for kk in range(1, 8):
            peer = (x ^ (kk >> 2), y ^ ((kk >> 1) & 1), c ^ (kk & 1))
            cp = pltpu.make_async_remote_copy(src_ref=v_ref, dst_ref=buf.at[me], send_sem=send_sems.at[kk - 1],
                                              recv_sem=recv_sems.at[kk - 1], device_id=peer, device_id_type=MESH)
            cp.start()
            cps.append(cp)
        for kk in range(1, 8):
            peer = (x ^ (kk >> 2), y ^ ((kk >> 1) & 1), c ^ (kk & 1))
            pltpu.make_async_remote_copy(src_ref=v_ref, dst_ref=buf.at[me ^ kk], send_sem=send_sems.at[kk - 1],
                                         recv_sem=recv_sems.at[kk - 1], device_id=peer, device_id_type=MESH).wait_recv()
        for cp in cps:
            cp.wait_send()
        acc = buf[0]
        for d in range(1, 8):
            acc = acc + buf[d]
        out_ref[...] = acc

    vm = pl.BlockSpec(memory_space=pltpu.VMEM)
    return pl.pallas_call(
        body, name="allreduce_small", in_specs=[vm], out_specs=vm, out_shape=jax.ShapeDtypeStruct((R, W), F32),
        scratch_shapes=[pltpu.VMEM((8, R, W), F32), pltpu.SemaphoreType.DMA((7,)), pltpu.SemaphoreType.DMA((7,))],
    )(vec)


def _pack_flat(parts, align, dtype):
    flat = jnp.concatenate([p.reshape(-1).astype(dtype) for p in parts])
    n = flat.shape[0]
    pad = (-n) % align
    return jnp.pad(flat, (0, pad)) if pad else flat


def _unpack_flat(flat, shapes):
    out, off = [], 0
    for s in shapes:
        n = math.prod(s)
        out.append(flat[off:off + n].reshape(s))
        off += n
    return out


def _full_from_shards(name, sh):
    if name in COL_SHARDED:
        return jnp.transpose(sh, (1, 0, 2)).reshape(sh.shape[1], 4 * sh.shape[2])
    return sh.reshape(4 * sh.shape[1], sh.shape[2])


def _shards_from_full(name, full):
    if name in COL_SHARDED:
        r, cc = full.shape[0], full.shape[1] // 4
        return jnp.transpose(full.reshape(r, 4, cc), (1, 0, 2)).reshape(4, r * cc)
    return full.reshape(4, -1)


def _pad_heads(w, width):
    k = w.shape[0]
    return jnp.pad(w.reshape(k, HEADS, width), ((0, 0), (0, 0), (0, HP - width))).reshape(k, HEADS * HP)


def _unpad_heads(w, width):
    k = w.shape[0]
    return w.reshape(k, HEADS, HP)[:, :, :width].reshape(k, HEADS * width)


def _blockdiag(t):
    g, a, b = t.shape
    return jnp.einsum("gab,gk->gakb", t, jnp.eye(g, dtype=t.dtype)).reshape(g * a, g * b)


def _blockdiag_t(m, a, b):
    g = m.shape[0] // a
    return jnp.einsum("gagb->gab", m.reshape(g, a, g, b))


def _local_step(x, mem, positions, target, W, sp):
    T = x.shape[0]
    row = lambda v: v.reshape(1, -1)

    ab_re, ab_im, bb_re, bb_im = _s5_discretize(sp["s5_lam_re"], sp["s5_lam_im"], sp["s5_log_dt"], sp["s5_b_re"], sp["s5_b_im"])
    ar, ai = ab_re.reshape(1, S5_N), ab_im.reshape(1, S5_N)
    tab_fwd, tab_rev = _s5_tables(ar, ai, False), _s5_tables(ar, -ai, True)
    b_st = jnp.concatenate([_blockdiag(jnp.swapaxes(bb_re, 1, 2)), _blockdiag(jnp.swapaxes(bb_im, 1, 2))],
                           axis=1).astype(BF16)
    c_st = jnp.concatenate([_blockdiag(jnp.swapaxes(sp["s5_c_re"], 1, 2)),
                            -_blockdiag(jnp.swapaxes(sp["s5_c_im"], 1, 2))], axis=0).astype(BF16)
    inv = ROPE_THETA ** (-jnp.arange(0, ROPE, 2, dtype=F32) / ROPE)
    ang = positions.astype(F32)[:, None] * inv
    cs, sn = jnp.cos(ang), jnp.sin(ang)
    one, zero = jnp.ones((T, NOPE), F32), jnp.zeros((T, NOPE), F32)
    z16, tail1, tail0 = jnp.zeros((T, ROPE // 2), F32), jnp.ones((T, HP - QK), F32), jnp.zeros((T, HP - QK), F32)
    t_cos = jnp.concatenate([one, cs, cs, tail1], axis=1)
    t_s1 = jnp.concatenate([zero, z16, sn, tail0], axis=1)
    t_s2 = jnp.concatenate([zero, -sn, z16, tail0], axis=1)

    w_in = W["w_in"]
    w_in_re = jnp.concatenate([w_in[:, 800:], w_in[:, :768], w_in[:, 768:800],
                               jnp.zeros((D_MODEL, HP - ROPE), w_in.dtype)], axis=1)
    w_uq_p = _pad_heads(W["w_uq"], QK)
    wkv = W["w_ukv"].reshape(S5_W, HEADS, NOPE + VD)
    w_uk_p = _pad_heads(wkv[:, :, :NOPE].reshape(S5_W, HEADS * NOPE), NOPE)
    w_uv_p = _pad_heads(wkv[:, :, NOPE:].reshape(S5_W, HEADS * VD), VD)
    w_oa_p = jnp.pad(W["w_oa"].reshape(HEADS, VD, D_MODEL), ((0, 0), (0, HP - VD), (0, 0))).reshape(HEADS * HP, D_MODEL)

    g_in, b_in = row(sp["ln_in_g"]), row(sp["ln_in_b"])
    g1, b1, g2, b2, g3, b3 = (row(sp[k]) for k in ("ln1_g", "ln1_b", "ln2_g", "ln2_b", "ln3_g", "ln3_b"))
    gq, gkv, d_skip = row(sp["q_norm_g"]), row(sp["kv_norm_g"]), row(sp["s5_d"])

    (h0,) = _rowwise(lambda a, g, b: (_ln(a, g, b),), "ln_in_fwd", [_whole(x)], [g_in, b_in], [(D_MODEL, F32)])
    z = _mm(h0, w_in_re, "in_proj", tm=512, tn=Z_W, out_dtype=BF16)
    hs, yc, yg = _s5_fwd(z, b_st, c_st, d_skip, tab_fwd, "s5_fwd")
    y2 = _mm(yg, W["w_glu"], "glu_proj", tn=2048, out_dtype=BF16)
    cqn, ckvn = _rowwise(lambda a, b, ga, gb: (_rms(a, ga), _rms(b, gb)), "mla_norm_fwd",
                         [(z, Z_CQ, S5_W), (z, Z_CKV, S5_W)], [gq, gkv], [(S5_W, BF16)] * 2)
    q_raw = _mm(cqn, w_uq_p, "mla_uq")
    k_raw = _mm(ckvn, w_uk_p, "mla_uk")
    v = _mm(ckvn, w_uv_p, "mla_uv", out_dtype=BF16)

    def rope_fwd(qr, kr_raw, krz, tc_, t1, t2):
        q = _rope(qr, _tile8(tc_), _tile8(t1), _tile8(t2))
        kr = _rope(pltpu.roll(krz, NOPE, 1), tc_, t1, t2)
        return q, kr_raw + _tile8(kr)

    q, k = _rowwise(rope_fwd, "mla_rope_fwd", [_whole(q_raw), _whole(k_raw), (z, Z_KR, HP), _whole(t_cos), _whole(t_s1), _whole(t_s2)],
                    [], [(HEADS * HP, BF16)] * 2)
    o, lse = _flash_fwd(q, k, v, "mla_attn_fwd")
    a_out = _mm(o, w_oa_p, "mla_oa", out_dtype=BF16)
    (mixin,) = _rowwise(lambda *a: (_gate_mix(*a),), "gate_mix_fwd",
                        [(z, Z_GS, D_MODEL), (z, Z_GA, D_MODEL), (y2, 0, D_MODEL), (y2, D_MODEL, D_MODEL), _whole(a_out)],
                        [], [(D_MODEL, BF16)])
    mix = _mm(mixin, W["w_o"], "mix_o")
    (h1,) = _rowwise(lambda h, r, g, b: (_ln_res(h, r, g, b),), "ln1_fwd", [_whole(h0), _whole(mix)], [g1, b1], [(D_MODEL, F32)])
    xq = _mm(h1, W["w_xq"], "x_q", out_dtype=BF16)
    xk = _mm(mem, W["w_xk"], "x_k", out_dtype=BF16)
    xv = _mm(mem, W["w_xv"], "x_v", out_dtype=BF16)
    (xo,) = _rowwise(_xattn_fwd_fn, "xattn_fwd", [_whole(xq)], [xk, xv], [(D_MODEL, BF16)], tm=1024, upcast=False)
    xa = _mm(xo, W["w_xo"], "x_o")
    (h2,) = _rowwise(lambda h, r, g, b: (_ln_res(h, r, g, b),), "ln2_fwd", [_whole(h1), _whole(xa)], [g2, b2], [(D_MODEL, F32)])
    a_up = _mm(h2, W["w_up"], "mlp_up", out_dtype=BF16)
    ff = _mm(a_up, W["w_down"], "mlp_down", act=_relu2)

    def loss_fn(h, r, tgt, g, b):
        def f(r_, g_, b_):
            e = _ln_res(h, r_, g_, b_) - tgt
            return 0.5 * jnp.sum(jnp.mean(e * e, axis=-1))
        lv, (dr_, dg_, db_) = jax.value_and_grad(f, argnums=(0, 1, 2))(r, g, b)
        return dr_, dg_, db_, jnp.broadcast_to(lv, (1, 128))

    dr3, dg3, db3, lossv = _rowwise(loss_fn, "ln3_loss_bwd", [_whole(h2), _whole(ff), _whole(target)], [g3, b3],
                                    [(D_MODEL, F32)], [(1, D_MODEL), (1, D_MODEL), (1, 128)])
    gW, gs = {}, {"ln3_g": dg3, "ln3_b": db3}

    da = _mm(dr3, W["w_down"], "mlp_down_bwd_a", "nt", epi=lambda acc, a: acc * (2.0 * jnp.maximum(a, 0.0)), extras=(a_up,),
             out_dtype=BF16)
    gW["w_down"] = _mm(a_up, dr3, "mlp_down_bwd_w", "tn", m=4 * D_MODEL, n=D_MODEL, act=_relu2)
    gW["w_up"] = _mm(h2, da, "mlp_up_bwd_w", "tn", m=D_MODEL, n=4 * D_MODEL)
    dh2 = _mm(da, W["w_up"], "mlp_up_bwd_a", "nt", epi=lambda acc, e: acc + DN_ALPHA * e, extras=(dr3,))

    def ln_res_bwd(h, r, dy, g, b):
        _, vjp = jax.vjp(_ln_res, h, r, g, b)
        _, dr_, dg_, db_ = vjp(dy)
        return dr_, dg_, db_

    dr2, gs["ln2_g"], gs["ln2_b"] = _rowwise(ln_res_bwd, "ln2_bwd", [_whole(h1), _whole(xa), _whole(dh2)], [g2, b2],
                                             [(D_MODEL, F32)], [(1, D_MODEL)] * 2)
    dxo = _mm(dr2, W["w_xo"], "x_o_bwd_a", "nt", out_dtype=BF16)
    gW["w_xo"] = _mm(xo, dr2, "x_o_bwd_w", "tn", m=D_MODEL, n=D_MODEL)
    dxq, dxk, dxv = _rowwise(_xattn_bwd_fn, "xattn_bwd", [_whole(xq), _whole(dxo)], [xk, xv], [(D_MODEL, BF16)],
                             [(xk.shape[0], D_MODEL)] * 2, tm=1024, upcast=False)
    gW["w_xq"] = _mm(h1, dxq, "x_q_bwd_w", "tn", m=D_MODEL, n=D_MODEL)
    gW["w_xk"] = _mm(mem, dxk, "x_k_bwd_w", "tn", m=D_MODEL, n=D_MODEL)
    gW["w_xv"] = _mm(mem, dxv, "x_v_bwd_w", "tn", m=D_MODEL, n=D_MODEL)
    dh1 = _mm(dxq, W["w_xq"], "x_q_bwd_a", "nt", epi=lambda acc, e: acc + DN_ALPHA * e, extras=(dr2,))

    dr1, gs["ln1_g"], gs["ln1_b"] = _rowwise(ln_res_bwd, "ln1_bwd", [_whole(h0), _whole(mix), _whole(dh1)], [g1, b1],
                                             [(D_MODEL, F32)], [(1, D_MODEL)] * 2)
    dmixin = _mm(dr1, W["w_o"], "mix_o_bwd_a", "nt")
    gW["w_o"] = _mm(mixin, dr1, "mix_o_bwd_w", "tn", m=D_MODEL, n=D_MODEL)

    def gate_mix_bwd(gs_, ga_, ya, yb, ao, dy):
        _, vjp = jax.vjp(_gate_mix, gs_, ga_, ya, yb, ao)
        return vjp(dy)

    dgs, dga, dy2a, dy2b, da_out = _rowwise(
        gate_mix_bwd, "gate_mix_bwd",
        [(z, Z_GS, D_MODEL), (z, Z_GA, D_MODEL), (y2, 0, D_MODEL), (y2, D_MODEL, D_MODEL), _whole(a_out), _whole(dmixin)],
        [], [(D_MODEL, BF16)] * 5)

    do = _mm(da_out, w_oa_p, "mla_oa_bwd_a", "nt")
    g_oa_p = _mm(o, da_out, "mla_oa_bwd_w", "tn", m=HEADS * HP, n=D_MODEL)
    gW["w_oa"] = g_oa_p.reshape(HEADS, HP, D_MODEL)[:, :VD].reshape(HEADS * VD, D_MODEL)
    delta, do16 = _rowwise(_attn_delta_fn, "mla_attn_delta", [_whole(do), _whole(o)], [], [(HEADS, F32), (HEADS * HP, BF16)])
    dq, dk, dv = _flash_bwd(q, k, v, do16, lse, delta.T.reshape(HEADS, 1, T), "mla_attn_bwd")

    def rope_bwd(dq_, dk_, tc_, t1, t2):
        dqr = _rope_t(dq_, _tile8(tc_), _tile8(t1), _tile8(t2))
        dkr = dk_[:, 0:HP]
        for hh in range(1, HEADS):
            dkr = dkr + dk_[:, hh * HP:(hh + 1) * HP]
        dkr = _rope_t(jnp.where(_rope_lanes(dkr.shape), dkr, 0.0), tc_, t1, t2)
        dkr = pltpu.roll(dkr, NOPE, 1)
        lane = lax.broadcasted_iota(jnp.int32, dkr.shape, 1)
        return dqr, jnp.where(lane < ROPE, dkr, 0.0)

    dq_raw, dkr = _rowwise(rope_bwd, "mla_rope_bwd", [_whole(dq), _whole(dk), _whole(t_cos), _whole(t_s1), _whole(t_s2)], [],
                           [(HEADS * HP, BF16), (HP, BF16)])
    dcqn = _mm(dq_raw, w_uq_p, "mla_uq_bwd_a", "nt")
    gW["w_uq"] = _unpad_heads(_mm(cqn, dq_raw, "mla_uq_bwd_w", "tn", m=S5_W, n=HEADS * HP), QK)
    dckvn_k = _mm(dk, w_uk_p, "mla_uk_bwd_a", "nt")
    dckvn = _mm(dv, w_uv_p, "mla_uv_bwd_a", "nt", epi=lambda acc, e: acc + e, extras=(dckvn_k,))
    g_uk = _unpad_heads(_mm(ckvn, dk, "mla_uk_bwd_w", "tn", m=S5_W, n=HEADS * HP), NOPE).reshape(S5_W, HEADS, NOPE)
    g_uv = _unpad_heads(_mm(ckvn, dv, "mla_uv_bwd_w", "tn", m=S5_W, n=HEADS * HP), VD).reshape(S5_W, HEADS, VD)
    gW["w_ukv"] = jnp.concatenate([g_uk, g_uv], axis=2).reshape(S5_W, HEADS * (NOPE + VD))

    def norm_bwd(a, b, da_, db_, ga, gb):
        _, vja = jax.vjp(_rms, a, ga)
        _, vjb = jax.vjp(_rms, b, gb)
        dxa, dga_ = vja(da_)
        dxb, dgb_ = vjb(db_)
        return dxa, dxb, dga_, dgb_

    dcq, dckv, gs["q_norm_g"], gs["kv_norm_g"] = _rowwise(
        norm_bwd, "mla_norm_bwd", [(z, Z_CQ, S5_W), (z, Z_CKV, S5_W), _whole(dcqn), _whole(dckvn)], [gq, gkv],
        [(S5_W, BF16)] * 2, [(1, S5_W)] * 2)

    dy2 = jnp.concatenate([dy2a, dy2b], axis=1)
    dyg = _mm(dy2, W["w_glu"], "glu_bwd_a", "nt")
    gW["w_glu"] = _mm(yg, dy2, "glu_bwd_w", "tn", m=S5_W, n=2 * D_MODEL)

    du, gs["s5_ab"], g_bst, g_cst_t, gs["s5_d"] = _s5_bwd(dyg, yc, z, hs, b_st, c_st, d_skip, tab_rev, "s5_bwd")
    gs["s5_bb_re"] = jnp.swapaxes(_blockdiag_t(g_bst[:, :S5_N], S5_H, S5_P), 1, 2)
    gs["s5_bb_im"] = jnp.swapaxes(_blockdiag_t(g_bst[:, S5_N:], S5_H, S5_P), 1, 2)
    gs["s5_c_re"] = _blockdiag_t(g_cst_t[:, :S5_N], S5_H, S5_P)
    gs["s5_c_im"] = -_blockdiag_t(g_cst_t[:, S5_N:], S5_H, S5_P)

    dz = jnp.concatenate([dgs, dga, du, dcq, dckv, dkr], axis=1)
    g_in_re = _mm(h0, dz, "in_proj_bwd_w", "tn", m=D_MODEL, n=Z_W, tm=512, tn=Z_W)
    gW["w_in"] = jnp.concatenate([g_in_re[:, Z_U:Z_KR], g_in_re[:, Z_KR:Z_KR + ROPE], g_in_re[:, :Z_U]], axis=1)
    dh0 = _mm(dz, w_in_re, "in_proj_bwd_a", "nt", tm=512, tk=Z_W, epi=lambda acc, e: acc + DN_ALPHA * e, extras=(dr1,))

    def ln_bwd(a, dy, g, b):
        _, vjp = jax.vjp(_ln, a, g, b)
        return vjp(dy)

    dx, gs["ln_in_g"], gs["ln_in_b"] = _rowwise(ln_bwd, "ln_in_bwd", [_whole(x), _whole(dh0)], [g_in, b_in],
                                                [(D_MODEL, F32)], [(1, D_MODEL)] * 2)
    return lossv[:, :1], dx, gW, gs


_RAW_SMALL = (("loss", (1, 1)), ("ln_in_g", (1, D_MODEL)), ("ln_in_b", (1, D_MODEL)), ("ln1_g", (1, D_MODEL)),
              ("ln1_b", (1, D_MODEL)), ("ln2_g", (1, D_MODEL)), ("ln2_b", (1, D_MODEL)), ("ln3_g", (1, D_MODEL)),
              ("ln3_b", (1, D_MODEL)), ("q_norm_g", (1, S5_W)), ("kv_norm_g", (1, S5_W)), ("s5_d", (1, S5_W)),
              ("s5_ab", (1, 2 * S5_N)), ("s5_bb_re", (S5_G, S5_P, S5_H)), ("s5_bb_im", (S5_G, S5_P, S5_H)),
              ("s5_c_re", (S5_G, S5_H, S5_P)), ("s5_c_im", (S5_G, S5_H, S5_P)))


def kernel(x, mem, positions, ln_in_g, ln_in_b, w_in, s5_lam_re, s5_lam_im, s5_log_dt, s5_b_re, s5_b_im, s5_c_re, s5_c_im, s5_d, w_glu, q_norm_g, w_uq, kv_norm_g, w_ukv, w_oa, w_o, ln1_g, ln1_b, w_xq, w_xk, w_xv, w_xo, ln2_g, ln2_b, w_up, w_down, ln3_g, ln3_b, loss_target, m_ln_in_g, m_ln_in_b, m_w_in, m_s5_lam_re, m_s5_lam_im, m_s5_log_dt, m_s5_b_re, m_s5_b_im, m_s5_c_re, m_s5_c_im, m_s5_d, m_w_glu, m_q_norm_g, m_w_uq, m_kv_norm_g, m_w_ukv, m_w_oa, m_w_o, m_ln1_g, m_ln1_b, m_w_xq, m_w_xk, m_w_xv, m_w_xo, m_ln2_g, m_ln2_b, m_w_up, m_w_down, m_ln3_g, m_ln3_b, v_ln_in_g, v_ln_in_b, v_w_in, v_s5_lam_re, v_s5_lam_im, v_s5_log_dt, v_s5_b_re, v_s5_b_im, v_s5_c_re, v_s5_c_im, v_s5_d, v_w_glu, v_q_norm_g, v_w_uq, v_kv_norm_g, v_w_ukv, v_w_oa, v_w_o, v_ln1_g, v_ln1_b, v_w_xq, v_w_xk, v_w_xv, v_w_xo, v_ln2_g, v_ln2_b, v_w_up, v_w_down, v_ln3_g, v_ln3_b):
    a = dict(locals())
    wts = {n: a[n] for n in WEIGHTS}
    ms = {n: a["m_" + n] for n in WEIGHTS}
    vs = {n: a["v_" + n] for n in WEIGHTS}
    cidx = lax.axis_index("c").astype(jnp.int32).reshape(1)

    shard2d = {n: wts[n].reshape(wts[n].shape[-2], wts[n].shape[-1]) for n in BIG}
    nrow = {n: shard2d[n].size // 1024 for n in BIG}
    pad_rows = (-sum(nrow.values())) % PACK_ROWS
    rows_half = (sum(nrow.values()) + pad_rows) // 2
    mine = 2 * lax.axis_index("x") + lax.axis_index("y")
    packed = jnp.concatenate([shard2d[n].astype(BF16).reshape(nrow[n], 1024) for n in BIG]
                             + [jnp.zeros((pad_rows, 1024), BF16)], axis=0).reshape(1, 2, rows_half, 1024)
    gathered = lax.dynamic_update_slice(_gather_weights(packed[0]), packed, (mine, 0, 0, 0))
    gathered = gathered.reshape(4, 2 * rows_half, 1024)
    W, off = {}, 0
    for n in BIG:
        r, cc = shard2d[n].shape
        piece = lax.optimization_barrier(gathered[:, off:off + nrow[n]])
        W[n] = _full_from_shards(n, piece.reshape(4, r, cc))
        off += nrow[n]

    sp = {n: wts[n] for n in SMALL}
    sp_local = {n: (sp[n][0] if sp[n].ndim > 1 else sp[n]) for n in SMALL}
    lossv, dx, gW, gs = _local_step(x[0], mem[0], positions[0], loss_target[0], W, sp_local)

    gflat = jnp.concatenate([_shards_from_full(n, gW[n]).reshape(4, nrow[n], 1024) for n in BIG]
                            + [jnp.zeros((4, pad_rows, 1024), F32)], axis=1).reshape(4, 2, rows_half, 1024)
    recv = _pair_exchange(gflat)
    psum = _pair_add(gflat, recv, cidx)
    own = lax.dynamic_slice(psum, (mine, 0, 0), (1, rows_half, 1024))
    slots = lax.dynamic_update_slice(_chip_exchange(psum), own, (mine, 0, 0))
    red = _chip_sum(slots)
    other = _pair_share(red)
    is_mine = (jnp.arange(2) == lax.axis_index("c"))[:, None, None]
    gsh = jnp.where(is_mine, red[None], other[None]).reshape(2 * rows_half, 1024)
    g_out, off = {}, 0
    for n in BIG:
        g_out[n] = lax.optimization_barrier(gsh[off:off + nrow[n]]).reshape(shard2d[n].shape)
        off += nrow[n]

    gs["loss"] = lossv
    raw = _pack_flat([gs[n].reshape(s) for n, s in _RAW_SMALL], 8 * 1024, F32)
    raw = _allreduce_small(raw.reshape(-1, 1024)).reshape(-1)
    rs = dict(zip([n for n, _ in _RAW_SMALL], _unpack_flat(raw, [s for _, s in _RAW_SMALL])))
    loss = rs["loss"].reshape(())
    _, disc_vjp = jax.vjp(_s5_discretize, sp_local["s5_lam_re"], sp_local["s5_lam_im"], sp_local["s5_log_dt"],
                          sp_local["s5_b_re"], sp_local["s5_b_im"])
    d_ab = rs["s5_ab"].reshape(2, S5_G, S5_P)
    g_lre, g_lim, g_ldt, g_bre, g_bim = disc_vjp((d_ab[0], d_ab[1], rs["s5_bb_re"], rs["s5_bb_im"]))
    small_g = {"s5_lam_re": g_lre, "s5_lam_im": g_lim, "s5_log_dt": g_ldt, "s5_b_re": g_bre, "s5_b_im": g_bim,
               "s5_c_re": rs["s5_c_re"], "s5_c_im": rs["s5_c_im"]}
    for n in ("ln_in_g", "ln_in_b", "ln1_g", "ln1_b", "ln2_g", "ln2_b", "ln3_g", "ln3_b", "q_norm_g", "kv_norm_g", "s5_d"):
        small_g[n] = rs[n]

    grads, deltas, new_m, new_v = {}, {}, {}, {}
    for n in BIG:
        d_, m_, v_ = _adamw(shard2d[n], g_out[n], ms[n].reshape(shard2d[n].shape), vs[n].reshape(shard2d[n].shape), "adamw_" + n)
        grads[n] = g_out[n].reshape(wts[n].shape)
        deltas[n], new_m[n], new_v[n] = (t.reshape(wts[n].shape) for t in (d_, m_, v_))
    packs = [_pack_flat([t[n] for n in SMALL], 8 * 1024, F32).reshape(-1, 1024)
             for t in (wts, {n: small_g[n].reshape(wts[n].shape) for n in SMALL}, ms, vs)]
    sd, sm, sv = _adamw(*packs, "adamw_small")
    shapes = [wts[n].shape for n in SMALL]
    for n, d_, m_, v_ in zip(SMALL, _unpack_flat(sd.reshape(-1), shapes), _unpack_flat(sm.reshape(-1), shapes),
                             _unpack_flat(sv.reshape(-1), shapes)):
        grads[n] = small_g[n].reshape(wts[n].shape)
        deltas[n], new_m[n], new_v[n] = d_, m_, v_

    return (loss, dx[None], *[grads[n] for n in WEIGHTS], *[deltas[n] for n in WEIGHTS],
            *[new_m[n] for n in WEIGHTS], *[new_v[n] for n in WEIGHTS])
```

```python
import functools
import math

import jax
import jax.numpy as jnp
from jax import lax
from jax.experimental import pallas as pl
from jax.experimental.pallas import tpu as pltpu

F32 = jnp.float32
BF16 = jnp.bfloat16
MESH = pl.DeviceIdType.MESH

D_MODEL = 1024
S5_W = 256
S5_G = 16
S5_H = 16
S5_P = 64
S5_N = S5_G * S5_P
S5_MAX_RE = -1e-4
HEADS = 8
NOPE = 64
ROPE = 32
QK = NOPE + ROPE
VD = 64
HP = 128
XH = 4
XD = 256
LN_EPS = 1e-5
RMS_EPS = 1e-6
NEG_INF = -1e30
DN_ALPHA = 2.0 ** 0.25
ROPE_THETA = 10000.0
ADAM_LR, ADAM_B1, ADAM_B2, ADAM_EPS, ADAM_WD, ADAM_STEP = 0.001, 0.9, 0.999, 1e-08, 0.01, 10

Z_GS, Z_GA, Z_U, Z_CQ, Z_CKV, Z_KR, Z_W = 0, 1024, 2048, 2304, 2560, 2816, 2944
SCAN_CHUNK = 256

BIG_EARLY = ("w_in", "w_glu", "w_uq", "w_ukv", "w_oa", "w_o")
BIG_LATE = ("w_up", "w_xq", "w_xk", "w_xv", "w_xo", "w_down")
BIG = BIG_EARLY + BIG_LATE
COL_SHARDED = ("w_in", "w_glu", "w_uq", "w_ukv", "w_oa", "w_up")
SMALL = ("ln_in_g", "ln_in_b", "s5_lam_re", "s5_lam_im", "s5_log_dt", "s5_b_re", "s5_b_im", "s5_c_re", "s5_c_im",
         "s5_d", "q_norm_g", "kv_norm_g", "ln1_g", "ln1_b", "ln2_g", "ln2_b", "ln3_g", "ln3_b")
WEIGHTS = ("ln_in_g", "ln_in_b", "w_in", "s5_lam_re", "s5_lam_im", "s5_log_dt", "s5_b_re", "s5_b_im", "s5_c_re",
           "s5_c_im", "s5_d", "w_glu", "q_norm_g", "w_uq", "kv_norm_g", "w_ukv", "w_oa", "w_o", "ln1_g", "ln1_b",
           "w_xq", "w_xk", "w_xv", "w_xo", "ln2_g", "ln2_b", "w_up", "w_down", "ln3_g", "ln3_b")
PACK_ROWS = 2 * 16


def _row_tile(n, cap=512):
    best = n
    for t in range(8, min(n, cap) + 1, 8):
        if n % t == 0:
            best = t
    return best


def _pick(n, cap):
    best = None
    for t in range(128, min(n, cap) + 1, 128):
        if n % t == 0:
            best = t
    return best if best is not None and (best >= 512 or best == n) else n


def _mm(a, b, name, mode="nn", *, a_off=0, b_off=0, m=None, n=None, act=None, epi=None, extras=(),
        out_dtype=F32, tm=1024, tn=1024, tk=None):
    if mode == "nn":
        M, (K, N) = a.shape[0], b.shape
    elif mode == "nt":
        M, (N, K) = a.shape[0], b.shape
    else:
        K, M, N = a.shape[0], m, n
    if mode == "tn":
        tm, tn, tk = _pick(M, tm), _pick(N, tn), min(tk or 512, K)
    else:
        tm, tn, tk = min(tm, M), _pick(N, tn), _pick(K, tk or 1024)
    assert M % tm == 0 and N % tn == 0 and K % tk == 0, (name, M, N, K, tm, tn, tk)
    nk = K // tk
    if mode == "tn":
        assert a_off % tm == 0 and b_off % tn == 0
        ao, bo = a_off // tm, b_off // tn
        a_spec = pl.BlockSpec((tk, tm), lambda i, j, k: (k, i + ao))
        b_spec = pl.BlockSpec((tk, tn), lambda i, j, k: (k, j + bo))
        dims = (((0,), (0,)), ((), ()))
    else:
        assert a_off % tk == 0
        ao = a_off // tk
        a_spec = pl.BlockSpec((tm, tk), lambda i, j, k: (i, k + ao))
        if mode == "nn":
            b_spec = pl.BlockSpec((tk, tn), lambda i, j, k: (k, j))
            dims = (((1,), (0,)), ((), ()))
        else:
            b_spec = pl.BlockSpec((tn, tk), lambda i, j, k: (j, k))
            dims = (((1,), (1,)), ((), ()))
    e_spec = pl.BlockSpec((tm, tn), lambda i, j, k: (i, j))
    n_extra = len(extras)

    def body(*refs):
        a_ref, b_ref = refs[0], refs[1]
        e_refs = refs[2:2 + n_extra]
        o_ref = refs[2 + n_extra]
        av = a_ref[...]
        if act is not None:
            av = act(av.astype(F32))
        p = lax.dot_general(av.astype(BF16), b_ref[...].astype(BF16), dims, preferred_element_type=F32)

        def finish(r):
            if epi is not None:
                r = epi(r, *[e[...] for e in e_refs])
            o_ref[...] = r.astype(o_ref.dtype)

        if nk == 1:
            finish(p)
        else:
            acc = refs[3 + n_extra]
            k = pl.program_id(2)

            @pl.when(k == 0)
            def _():
                acc[...] = p

            @pl.when(k > 0)
            def _():
                acc[...] += p

            @pl.when(k == nk - 1)
            def _():
                finish(acc[...])

    return pl.pallas_call(
        body, name=name, grid=(M // tm, N // tn, nk),
        in_specs=[a_spec, b_spec] + [e_spec] * n_extra, out_specs=e_spec,
        out_shape=jax.ShapeDtypeStruct((M, N), out_dtype),
        scratch_shapes=[pltpu.VMEM((tm, tn), F32)] if nk > 1 else [],
        compiler_params=pltpu.CompilerParams(dimension_semantics=("parallel", "parallel", "arbitrary")),
    )(a, b, *extras)


def _rowwise(fn, name, rows, pars, outs, accs=(), tm=256, upcast=True):
    T = rows[0][0].shape[0]
    tm = min(tm, T)
    assert T % tm == 0
    n_in, n_out = len(rows) + len(pars), len(outs)
    in_specs = []
    for arr, off, w in rows:
        assert off % w == 0 and arr.shape[0] == T, name
        in_specs.append(pl.BlockSpec((tm, w), functools.partial(lambda i, o: (i, o), o=off // w)))
    for p in pars:
        in_specs.append(pl.BlockSpec(p.shape, lambda i: (0, 0)))
    out_specs = [pl.BlockSpec((tm, w), lambda i: (i, 0)) for w, _ in outs]
    out_specs += [pl.BlockSpec(s, lambda i: (0, 0)) for s in accs]
    out_shape = [jax.ShapeDtypeStruct((T, w), dt) for w, dt in outs] + [jax.ShapeDtypeStruct(s, F32) for s in accs]

    def body(*refs):
        res = fn(*[r[...].astype(F32) if upcast else r[...] for r in refs[:n_in]])
        o_refs = refs[n_in:]
        for r, v in zip(o_refs[:n_out], res[:n_out]):
            r[...] = v.astype(r.dtype)
        if accs:
            i = pl.program_id(0)

            @pl.when(i == 0)
            def _():
                for r in o_refs[n_out:]:
                    r[...] = jnp.zeros_like(r)

            for r, v in zip(o_refs[n_out:], res[n_out:]):
                r[...] += v

    res = pl.pallas_call(
        body, name=name, grid=(T // tm,), in_specs=in_specs, out_specs=out_specs, out_shape=out_shape,
        compiler_params=pltpu.CompilerParams(dimension_semantics=("arbitrary",)),
    )(*[r[0] for r in rows], *pars)
    return res


def _whole(arr):
    return (arr, 0, arr.shape[1])


def _ln(x, g, b):
    mu = jnp.mean(x, axis=-1, keepdims=True)
    xc = x - mu
    var = jnp.mean(xc * xc, axis=-1, keepdims=True)
    return xc * lax.rsqrt(var + LN_EPS) * g + b


def _ln_res(h, r, g, b):
    return _ln(DN_ALPHA * h + r, g, b)


def _rms(x, g):
    return x * lax.rsqrt(jnp.mean(x * x, axis=-1, keepdims=True) + RMS_EPS) * g


def _gelu_skip(yc, u, d):
    y = yc + d * u
    return 0.5 * y * (1.0 + lax.erf(y * (1.0 / math.sqrt(2.0))))


def _gate_mix(gs, ga, y2a, y2b, aout):
    return jax.nn.sigmoid(gs) * (y2a * jax.nn.sigmoid(y2b)) + jax.nn.sigmoid(ga) * aout


def _relu2(a):
    r = jnp.maximum(a, 0.0)
    return r * r


def _tile8(t):
    return jnp.concatenate([t] * HEADS, axis=1)


def _rope(x, cos, s1, s2):
    w = x.shape[1]
    return x * cos + pltpu.roll(x, ROPE // 2, 1) * s1 + pltpu.roll(x, w - ROPE // 2, 1) * s2


def _rope_t(dy, cos, s1, s2):
    w = dy.shape[1]
    return dy * cos + pltpu.roll(dy * s1, w - ROPE // 2, 1) + pltpu.roll(dy * s2, ROPE // 2, 1)


def _rope_lanes(shape):
    lane = lax.broadcasted_iota(jnp.int32, shape, 1) % HP
    return (lane >= NOPE) & (lane < QK)


ATT_HB = 2
ATT_SCALE = QK ** -0.5
LOG2E = 1.4426950408889634
LN2 = 0.6931471805599453
_NT = (((1,), (1,)), ((), ()))
_TN = (((0,), (0,)), ((), ()))


def _carry_specs(carry):
    if carry is None:
        return [], [], [], []
    arr = carry[0]
    return ([_HBM], [_HBM], [jax.ShapeDtypeStruct((4,) + arr.shape[1:], arr.dtype)],
            [pltpu.SemaphoreType.DMA((3,)), pltpu.SemaphoreType.DMA((3,))])


def _carry_start(exchange, when):
    @pl.when(when)
    def _():
        for cp in exchange(False):
            cp.start()


def _carry_wait(exchange, when):
    @pl.when(when)
    def _():
        for cp in exchange(True):
            cp.wait_recv()
        for cp in exchange(False):
            cp.wait_send()


def _tri_tables(n, by_row):
    if by_row:
        pairs = [(i, j) for i in range(n) for j in range(i + 1)]
    else:
        pairs = [(i, j) for j in range(n) for i in range(j, n)]
    return jnp.array([p[0] for p in pairs], jnp.int32), jnp.array([p[1] for p in pairs], jnp.int32)


def _flash_fwd(q, k, v, name, bq=512, carry=None):
    T = q.shape[0]
    bq = min(bq, T)
    nq = T // bq
    hb = ATT_HB
    n_carry = 0 if carry is None else 1

    i_tab, j_tab = _tri_tables(nq, by_row=True)

    def body(it_ref, jt_ref, q_ref, k_ref, v_ref, *rest):
        o_ref, lse_ref = rest[n_carry:n_carry + 2]
        m_sc, l_sc, acc_sc = rest[2 * n_carry + 2:2 * n_carry + 5]
        t = pl.program_id(1)
        i, j = it_ref[t], jt_ref[t]
        if carry is not None:
            exchange = functools.partial(_carried_exchange, rest[0], rest[3], rest[-2], rest[-1], carry[1])
            _carry_start(exchange, (pl.program_id(0) == 0) & (t == 0))

        @pl.when(j == 0)
        def _():
            m_sc[...] = jnp.full_like(m_sc, NEG_INF)
            l_sc[...] = jnp.zeros_like(l_sc)
            acc_sc[...] = jnp.zeros_like(acc_sc)

        def step(masked):
            for hh in range(hb):
                sl = slice(hh * HP, (hh + 1) * HP)
                s = lax.dot_general(q_ref[:, sl], k_ref[:, sl], _NT, preferred_element_type=F32) * (ATT_SCALE * LOG2E)
                if masked:
                    row = lax.broadcasted_iota(jnp.int32, (bq, bq), 0)
                    col = lax.broadcasted_iota(jnp.int32, (bq, bq), 1)
                    s = jnp.where(col <= row, s, NEG_INF)
                m_prev = m_sc[hh]
                m_new = jnp.maximum(m_prev, jnp.max(s, axis=1, keepdims=True))
                alpha = jnp.exp2(m_prev - m_new)
                p = jnp.exp2(s - jnp.concatenate([m_new] * (bq // HP), axis=1))
                l_sc[hh] = alpha * l_sc[hh] + jnp.sum(p, axis=1, keepdims=True)
                acc_sc[hh] = alpha * acc_sc[hh] + jnp.dot(p.astype(BF16), v_ref[:, sl], preferred_element_type=F32)
                m_sc[hh] = m_new

        @pl.when(j < i)
        def _():
            step(False)

        @pl.when(j == i)
        def _():
            step(True)
            for hh in range(hb):
                o_ref[:, hh * HP:(hh + 1) * HP] = acc_sc[hh] / l_sc[hh]
                lse_ref[hh] = jnp.transpose((m_sc[hh] + jnp.log2(l_sc[hh])) * LN2)[0:1, :]

        if carry is not None:
            _carry_wait(exchange, (pl.program_id(0) == HEADS // hb - 1) & (t == i_tab.shape[0] - 1))

    qs = pl.BlockSpec((bq, hb * HP), lambda h, t, it, jt: (it[t], h))
    ks = pl.BlockSpec((bq, hb * HP), lambda h, t, it, jt: (jt[t], h))
    c_in, c_out, c_shape, c_sems = _carry_specs(carry)
    gs = pltpu.PrefetchScalarGridSpec(
        num_scalar_prefetch=2, grid=(HEADS // hb, i_tab.shape[0]), in_specs=[qs, ks, ks] + c_in,
        out_specs=[qs, pl.BlockSpec((hb, 1, bq), lambda h, t, it, jt: (h, 0, it[t]))] + c_out,
        scratch_shapes=[pltpu.VMEM((hb, bq, HP), F32)] * 3 + c_sems)
    return pl.pallas_call(
        body, name=name, grid_spec=gs,
        out_shape=[jax.ShapeDtypeStruct((T, HEADS * HP), F32), jax.ShapeDtypeStruct((HEADS, 1, T), F32)] + c_shape,
        compiler_params=pltpu.CompilerParams(dimension_semantics=("arbitrary", "arbitrary")),
    )(i_tab, j_tab, q, k, v, *([] if carry is None else [carry[0]]))


def _attn_delta_fn(do, o):
    prod = do * o
    cols = [jnp.sum(prod[:, h * HP:(h + 1) * HP], axis=1, keepdims=True) for h in range(HEADS)]
    return jnp.concatenate(cols, axis=1), do


def _flash_bwd(q, k, v, do, lse_row, delta_row, name, bq=512, carry=None):
    T = q.shape[0]
    bq = min(bq, T)
    nq = T // bq
    hb = ATT_HB
    n_carry = 0 if carry is None else 1

    i_tab, j_tab = _tri_tables(nq, by_row=False)
    n_blk = i_tab.shape[0]

    def body(it_ref, jt_ref, q_ref, k_ref, v_ref, do_ref, lse_ref, dl_ref, *rest):
        dq_ref, dk_ref, dv_ref = rest[n_carry:n_carry + 3]
        t = pl.program_id(1)
        i, j = it_ref[t], jt_ref[t]
        if carry is not None:
            exchange = functools.partial(_carried_exchange, rest[0], rest[4], rest[-2], rest[-1], carry[1])
            _carry_start(exchange, (pl.program_id(0) == 0) & (t == 0))

        @pl.when(t == 0)
        def _():
            dq_ref[...] = jnp.zeros_like(dq_ref)

        @pl.when(i == j)
        def _():
            dk_ref[...] = jnp.zeros_like(dk_ref)
            dv_ref[...] = jnp.zeros_like(dv_ref)

        def step(masked):
            rows = pl.ds(pl.multiple_of(i * bq, bq), bq)
            for hh in range(hb):
                sl = slice(hh * HP, (hh + 1) * HP)
                qh, kh, doh = q_ref[:, sl], k_ref[:, sl], do_ref[:, sl]
                st = lax.dot_general(kh, qh, _NT, preferred_element_type=F32) * (ATT_SCALE * LOG2E)
                pt = jnp.exp2(st - lse_ref[hh] * LOG2E)
                if masked:
                    krow = lax.broadcasted_iota(jnp.int32, (bq, bq), 0)
                    qcol = lax.broadcasted_iota(jnp.int32, (bq, bq), 1)
                    pt = jnp.where(krow <= qcol, pt, 0.0)
                dv_ref[:, sl] += jnp.dot(pt.astype(BF16), doh, preferred_element_type=F32)
                dpt = lax.dot_general(v_ref[:, sl], doh, _NT, preferred_element_type=F32)
                dst = (pt * (dpt - dl_ref[hh])).astype(BF16)
                dk_ref[:, sl] += jnp.dot(dst, qh, preferred_element_type=F32)
                dq_ref[rows, sl] += lax.dot_general(dst, kh, _TN, preferred_element_type=F32)

        @pl.when(i > j)
        def _():
            step(False)

        @pl.when(i == j)
        def _():
            step(True)

        @pl.when(i == nq - 1)
        def _():
            dk_ref[...] *= ATT_SCALE

        @pl.when(t == n_blk - 1)
        def _():
            dq_ref[...] *= ATT_SCALE

        if carry is not None:
            _carry_wait(exchange, (pl.program_id(0) == HEADS // hb - 1) & (t == n_blk - 1))

    qs = pl.BlockSpec((bq, hb * HP), lambda h, t, it, jt: (it[t], h))
    ks = pl.BlockSpec((bq, hb * HP), lambda h, t, it, jt: (jt[t], h))
    rs = pl.BlockSpec((hb, 1, bq), lambda h, t, it, jt: (h, 0, it[t]))
    full = pl.BlockSpec((T, hb * HP), lambda h, t, it, jt: (0, h))
    c_in, c_out, c_shape, c_sems = _carry_specs(carry)
    gs = pltpu.PrefetchScalarGridSpec(num_scalar_prefetch=2, grid=(HEADS // hb, n_blk),
                                      in_specs=[qs, ks, ks, qs, rs, rs] + c_in, out_specs=[full, ks, ks] + c_out,
                                      scratch_shapes=c_sems)
    return pl.pallas_call(
        body, name=name, grid_spec=gs, out_shape=[jax.ShapeDtypeStruct((T, HEADS * HP), F32)] * 3 + c_shape,
        compiler_params=pltpu.CompilerParams(dimension_semantics=("arbitrary", "arbitrary")),
    )(i_tab, j_tab, q, k, v, do, lse_row, delta_row, *([] if carry is None else [carry[0]]))


def _xattn_heads(q, k, v):
    scale = XD ** -0.5
    ps = []
    for h in range(XH):
        sl = slice(h * XD, (h + 1) * XD)
        s = lax.dot_general(q[:, sl].astype(BF16), k[:, sl].astype(BF16), (((1,), (1,)), ((), ())),
                            preferred_element_type=F32) * scale
        e = jnp.exp(s - jnp.max(s, axis=1, keepdims=True))
        ps.append(e / jnp.sum(e, axis=1, keepdims=True))
    return ps


def _xattn_fwd_fn(q, k, v):
    ps = _xattn_heads(q, k, v)
    o = [jnp.dot(p.astype(BF16), v[:, h * XD:(h + 1) * XD].astype(BF16), preferred_element_type=F32)
         for h, p in enumerate(ps)]
    return (jnp.concatenate(o, axis=1),)


def _xattn_bwd_fn(q, do, k, v):
    scale = XD ** -0.5
    ps = _xattn_heads(q, k, v)
    dqs, dks, dvs = [], [], []
    tdims = (((0,), (0,)), ((), ()))
    for h, p in enumerate(ps):
        sl = slice(h * XD, (h + 1) * XD)
        doh = do[:, sl].astype(BF16)
        dvs.append(lax.dot_general(p.astype(BF16), doh, tdims, preferred_element_type=F32))
        dp = lax.dot_general(doh, v[:, sl].astype(BF16), (((1,), (1,)), ((), ())), preferred_element_type=F32)
        ds = (p * (dp - jnp.sum(dp * p, axis=1, keepdims=True)) * scale).astype(BF16)
        dqs.append(jnp.dot(ds, k[:, sl].astype(BF16), preferred_element_type=F32))
        dks.append(lax.dot_general(ds, q[:, sl].astype(BF16), tdims, preferred_element_type=F32))
    return jnp.concatenate(dqs, axis=1), jnp.concatenate(dks, axis=1), jnp.concatenate(dvs, axis=1)


S5_SUB = 8


def _cmul_add(xr, xi, ar, ai, sr, si):
    return xr + ar * sr - ai * si, xi + ar * si + ai * sr


def _s5_tables(ar, ai, reverse):
    pows = [(ar, ai)]
    for _ in range(S5_SUB - 1):
        pr, pi = pows[-1]
        pows.append((pr * ar - pi * ai, pr * ai + pi * ar))
    cr = jnp.concatenate([p[0] for p in pows], axis=0)
    ci = jnp.concatenate([p[1] for p in pows], axis=0)
    if reverse:
        cr, ci = cr[::-1], ci[::-1]
    t = jnp.arange(S5_SUB)[:, None]
    blocks = [jnp.concatenate([cr, ci], axis=1)]
    for s in (1, 2, 4):
        keep = (t < S5_SUB - s) if reverse else (t >= s)
        sr, si = pows[s - 1]
        blocks.append(jnp.concatenate([jnp.where(keep, sr, 0.0), jnp.where(keep, si, 0.0)], axis=1))
    return jnp.concatenate(blocks, axis=0)


def _sub_scan(xr, xi, tab_ref, reverse):
    for k, s in enumerate((1, 2, 4)):
        blk = slice(S5_SUB * (k + 1), S5_SUB * (k + 2))
        sh = S5_SUB - s if reverse else s
        xr, xi = _cmul_add(xr, xi, tab_ref[blk, :S5_N], tab_ref[blk, S5_N:], pltpu.roll(xr, sh, 0), pltpu.roll(xi, sh, 0))
    return xr, xi


def _s5_fwd(z, b_st, c_st, d_skip, tab, name, tc=512):
    T = z.shape[0]
    tc = min(tc, T)
    nsub = tc // S5_SUB

    def body(u_ref, bst_ref, cst_ref, d_ref, tab_ref, h_ref, yc_ref, yg_ref, carry, x_sc):
        @pl.when(pl.program_id(0) == 0)
        def _():
            carry[...] = jnp.zeros_like(carry)

        u = u_ref[...].astype(F32)
        x_sc[...] = jnp.dot(u.astype(BF16), bst_ref[...], preferred_element_type=F32)

        def sub(b, c):
            r = pl.ds(pl.multiple_of(b * S5_SUB, S5_SUB), S5_SUB)
            xr, xi = _sub_scan(x_sc[r, :S5_N], x_sc[r, S5_N:], tab_ref, False)
            hr, hi = _cmul_add(xr, xi, tab_ref[0:S5_SUB, :S5_N], tab_ref[0:S5_SUB, S5_N:], c[0], c[1])
            x_sc[r, :S5_N] = hr
            x_sc[r, S5_N:] = hi
            return hr[S5_SUB - 1:S5_SUB], hi[S5_SUB - 1:S5_SUB]

        cr, ci = lax.fori_loop(0, nsub, sub, (carry[0:1, :S5_N], carry[0:1, S5_N:]))
        carry[0:1, :S5_N] = cr
        carry[0:1, S5_N:] = ci
        h = x_sc[...]
        h_ref[...] = h
        yc = jnp.dot(h.astype(BF16), cst_ref[...], preferred_element_type=F32)
        yc_ref[...] = yc
        yg_ref[...] = _gelu_skip(yc, u, d_ref[...]).astype(yg_ref.dtype)

    whole = lambda a: pl.BlockSpec(a.shape, lambda i: (0, 0))
    row = lambda w: pl.BlockSpec((tc, w), lambda i: (i, 0))
    return pl.pallas_call(
        body, name=name, grid=(T // tc,),
        in_specs=[pl.BlockSpec((tc, S5_W), lambda i: (i, Z_U // S5_W)), whole(b_st), whole(c_st), whole(d_skip), whole(tab)],
        out_specs=[row(2 * S5_N), row(S5_W), row(S5_W)],
        out_shape=[jax.ShapeDtypeStruct((T, 2 * S5_N), F32), jax.ShapeDtypeStruct((T, S5_W), F32),
                   jax.ShapeDtypeStruct((T, S5_W), BF16)],
        scratch_shapes=[pltpu.VMEM((8, 2 * S5_N), F32), pltpu.VMEM((tc, 2 * S5_N), F32)],
        compiler_params=pltpu.CompilerParams(dimension_semantics=("arbitrary",)),
    )(z, b_st, c_st, d_skip, tab)


def _s5_bwd(dyg, yc, z, h, b_st, c_st, d_skip, tab, name, tc=512):
    T = z.shape[0]
    tc = min(tc, T)
    nc, nsub = T // tc, tc // S5_SUB

    def body(dyg_ref, yc_ref, u_ref, h_ref, hp_ref, bst_ref, cst_ref, d_ref, tab_ref,
             du_ref, da_ref, db_ref, dct_ref, dd_ref, carry, x_sc):
        g = pl.program_id(0)

        @pl.when(g == 0)
        def _():
            carry[...] = jnp.zeros_like(carry)
            for r in (da_ref, db_ref, dct_ref, dd_ref):
                r[...] = jnp.zeros_like(r)

        u = u_ref[...].astype(F32)
        _, vjp = jax.vjp(_gelu_skip, yc_ref[...], u, d_ref[...])
        dyc, du_skip, dd = vjp(dyg_ref[...])
        dd_ref[...] += dd
        dyc16 = dyc.astype(BF16)
        dct_ref[...] += lax.dot_general(dyc16, h_ref[...].astype(BF16), _TN, preferred_element_type=F32)
        x_sc[...] = lax.dot_general(dyc16, cst_ref[...], _NT, preferred_element_type=F32)
        first = jnp.where(g == nc - 1, 0.0, 1.0)
        row0 = lax.broadcasted_iota(jnp.int32, (S5_SUB, S5_N), 0) == 0

        def sub(k, c):
            cr, ci, acc_r, acc_i = c
            b = nsub - 1 - k
            r = pl.ds(pl.multiple_of(b * S5_SUB, S5_SUB), S5_SUB)
            xr, xi = _sub_scan(x_sc[r, :S5_N], x_sc[r, S5_N:], tab_ref, True)
            lr, li = _cmul_add(xr, xi, tab_ref[0:S5_SUB, :S5_N], tab_ref[0:S5_SUB, S5_N:], cr, ci)
            x_sc[r, :S5_N] = lr
            x_sc[r, S5_N:] = li
            rp = pl.ds(pl.multiple_of(jnp.maximum(b - 1, 0) * S5_SUB, S5_SUB), S5_SUB)
            last_r = jnp.where(b == 0, hp_ref[S5_SUB - 1:S5_SUB, :S5_N] * first, h_ref[rp, :S5_N][S5_SUB - 1:S5_SUB])
            last_i = jnp.where(b == 0, hp_ref[S5_SUB - 1:S5_SUB, S5_N:] * first, h_ref[rp, S5_N:][S5_SUB - 1:S5_SUB])
            pr = jnp.where(row0, last_r, pltpu.roll(h_ref[r, :S5_N], 1, 0))
            pi = jnp.where(row0, last_i, pltpu.roll(h_ref[r, S5_N:], 1, 0))
            return lr[0:1], li[0:1], acc_r + (lr * pr + li * pi), acc_i + (li * pr - lr * pi)

        zero = jnp.zeros((S5_SUB, S5_N), F32)
        cr, ci, acc_r, acc_i = lax.fori_loop(0, nsub, sub, (carry[0:1, :S5_N], carry[0:1, S5_N:], zero, zero))
        carry[0:1, :S5_N] = cr
        carry[0:1, S5_N:] = ci
        da_ref[0:1, :S5_N] += jnp.sum(acc_r, axis=0, keepdims=True)
        da_ref[0:1, S5_N:] += jnp.sum(acc_i, axis=0, keepdims=True)
        lam16 = x_sc[...].astype(BF16)
        db_ref[...] += lax.dot_general(u.astype(BF16), lam16, _TN, preferred_element_type=F32)
        du = lax.dot_general(lam16, bst_ref[...], _NT, preferred_element_type=F32) + du_skip
        du_ref[...] = du.astype(du_ref.dtype)

    whole = lambda a: pl.BlockSpec(a.shape, lambda g: (0, 0))
    row = lambda w, off=0: pl.BlockSpec((tc, w), lambda g: (nc - 1 - g, off))
    prev = pl.BlockSpec((S5_SUB, 2 * S5_N), lambda g: (jnp.maximum((nc - 1 - g) * nsub - 1, 0), 0))
    acc = lambda s: pl.BlockSpec(s, lambda g: (0, 0))
    return pl.pallas_call(
        body, name=name, grid=(nc,),
        in_specs=[row(S5_W), row(S5_W), row(S5_W, Z_U // S5_W), row(2 * S5_N), prev, whole(b_st), whole(c_st),
                  whole(d_skip), whole(tab)],
        out_specs=[row(S5_W), acc((1, 2 * S5_N)), acc((S5_W, 2 * S5_N)), acc((S5_W, 2 * S5_N)), acc((1, S5_W))],
        out_shape=[jax.ShapeDtypeStruct((T, S5_W), BF16), jax.ShapeDtypeStruct((1, 2 * S5_N), F32),
                   jax.ShapeDtypeStruct((S5_W, 2 * S5_N), F32), jax.ShapeDtypeStruct((S5_W, 2 * S5_N), F32),
                   jax.ShapeDtypeStruct((1, S5_W), F32)],
        scratch_shapes=[pltpu.VMEM((8, 2 * S5_N), F32), pltpu.VMEM((tc, 2 * S5_N), F32)],
        compiler_params=pltpu.CompilerParams(dimension_semantics=("arbitrary",)),
    )(dyg, yc, z, h, h, b_st, c_st, d_skip, tab)


def _s5_discretize(lam_re, lam_im, log_dt, b_re, b_im):
    lr = jnp.minimum(lam_re, S5_MAX_RE)
    li = lam_im
    dt = jnp.exp(log_dt)[:, None]
    mag = jnp.exp(lr * dt)
    ang = li * dt
    ab_re = mag * jnp.cos(ang)
    ab_im = mag * jnp.sin(ang)
    den = lr * lr + li * li
    nr = ab_re - 1.0
    f_re = ((nr * lr + ab_im * li) / den)[..., None]
    f_im = ((ab_im * lr - nr * li) / den)[..., None]
    return ab_re, ab_im, f_re * b_re - f_im * b_im, f_re * b_im + f_im * b_re


def _adamw_fn(w, g, m, v):
    m = ADAM_B1 * m + (1.0 - ADAM_B1) * g
    v = ADAM_B2 * v + (1.0 - ADAM_B2) * (g * g)
    m_hat = m / (1.0 - ADAM_B1 ** ADAM_STEP)
    v_hat = v / (1.0 - ADAM_B2 ** ADAM_STEP)
    delta = -ADAM_LR * (m_hat / (jnp.sqrt(v_hat) + ADAM_EPS) + ADAM_WD * w)
    return delta, m, v


def _adamw(w, g, m, v, name):
    wd = w.shape[1]
    return _rowwise(_adamw_fn, name, [_whole(w), _whole(g), _whole(m), _whole(v)], [], [(wd, F32)] * 3,
                    tm=_row_tile(w.shape[0]))


def _me():
    return lax.axis_index("x"), lax.axis_index("y"), lax.axis_index("c")


def _chip_of(j):
    return j // 2, j % 2


_HBM = pl.BlockSpec(memory_space=pl.ANY)


def _gather_weights(shard):
    _, R, W = shard.shape

    def body(s_ref, out_ref, send_sems, recv_sems):
        x, y, c = _me()
        mine = 2 * x + y
        sib = (x, y, 1 - c)
        chips = [(1 - x, y), (x, 1 - y), (1 - x, 1 - y)]

        def rcopy(kk, src, chip_idx, half, to):
            return pltpu.make_async_remote_copy(src_ref=src, dst_ref=out_ref.at[chip_idx, half],
                                                send_sem=send_sems.at[kk], recv_sem=recv_sems.at[kk],
                                                device_id=to, device_id_type=MESH)

        first = [rcopy(j, s_ref.at[c], mine, c, (*chip, c)) for j, chip in enumerate(chips)]
        for cp in first:
            cp.start()
        passed = []
        for j, (px, py) in enumerate(chips):
            src_chip = 2 * px + py
            rcopy(j, s_ref.at[c], src_chip, c, (x, y, c)).wait_recv()
            fwd = rcopy(3 + j, out_ref.at[src_chip, c], src_chip, c, sib)
            fwd.start()
            passed.append(fwd)
        for j, (px, py) in enumerate(chips):
            rcopy(3 + j, s_ref.at[c], 2 * px + py, 1 - c, (x, y, c)).wait_recv()
        for cp in first + passed:
            cp.wait_send()

    return pl.pallas_call(
        body, name="gather_weights", in_specs=[_HBM], out_specs=_HBM,
        out_shape=jax.ShapeDtypeStruct((4, 2, R, W), shard.dtype),
        scratch_shapes=[pltpu.SemaphoreType.DMA((6,)), pltpu.SemaphoreType.DMA((6,))],
    )(shard)


def _pair_exchange(g, name):
    _, _, R, W = g.shape

    def body(g_ref, out_ref, send_sem, recv_sem):
        x, y, c = _me()
        cp = pltpu.make_async_remote_copy(src_ref=g_ref.at[:, 1 - c], dst_ref=out_ref, send_sem=send_sem,
                                          recv_sem=recv_sem, device_id=(x, y, 1 - c), device_id_type=MESH)
        cp.start()
        cp.wait()

    return pl.pallas_call(
        body, name=name, in_specs=[_HBM], out_specs=_HBM,
        out_shape=jax.ShapeDtypeStruct((4, R, W), g.dtype),
        scratch_shapes=[pltpu.SemaphoreType.DMA(()), pltpu.SemaphoreType.DMA(())],
    )(g)


def _pair_add(g, recv, cidx, name):
    _, _, R, W = g.shape
    tr = _row_tile(R)

    def body(c_ref, a_ref, b_ref, o_ref):
        o_ref[...] = (a_ref[...] + b_ref[...]).astype(o_ref.dtype)

    gs = pltpu.PrefetchScalarGridSpec(
        num_scalar_prefetch=1, grid=(4, R // tr),
        in_specs=[pl.BlockSpec((None, None, tr, W), lambda j, r, c: (j, c[0], r, 0)),
                  pl.BlockSpec((None, tr, W), lambda j, r, c: (j, r, 0))],
        out_specs=pl.BlockSpec((None, tr, W), lambda j, r, c: (j, r, 0)))
    return pl.pallas_call(body, name=name, grid_spec=gs,
                          out_shape=jax.ShapeDtypeStruct((4, R, W), BF16))(cidx, g, recv)


def _carried_exchange(src_ref, dst_ref, send_sems, recv_sems, spread, incoming):
    x, y, c = _me()
    mine = 2 * x + y
    copies = []
    for j, (px, py) in enumerate([(1 - x, y), (x, 1 - y), (1 - x, 1 - y)]):
        src = src_ref.at[2 * px + py] if spread else src_ref.at[c]
        slot = 2 * px + py if incoming else mine
        copies.append(pltpu.make_async_remote_copy(src_ref=src, dst_ref=dst_ref.at[slot], send_sem=send_sems.at[j],
                                                   recv_sem=recv_sems.at[j], device_id=(px, py, c), device_id_type=MESH))
    return copies


def _chip_exchange(p, name):
    _, R, W = p.shape

    def body(p_ref, out_ref, send_sems, recv_sems):
        outs = _carried_exchange(p_ref, out_ref, send_sems, recv_sems, True, False)
        for cp in outs:
            cp.start()
        for cp in _carried_exchange(p_ref, out_ref, send_sems, recv_sems, True, True):
            cp.wait_recv()
        for cp in outs:
            cp.wait_send()

    return pl.pallas_call(
        body, name=name, in_specs=[_HBM], out_specs=_HBM,
        out_shape=jax.ShapeDtypeStruct((4, R, W), p.dtype),
        scratch_shapes=[pltpu.SemaphoreType.DMA((3,)), pltpu.SemaphoreType.DMA((3,))],
    )(p)


def _chip_sum(q, name):
    _, R, W = q.shape
    tr = _row_tile(R)

    def body(q_ref, o_ref):
        f = lambda t: q_ref[t].astype(F32)
        o_ref[...] = ((f(0) + f(1)) + f(2)) + f(3)

    return pl.pallas_call(body, name=name, grid=(R // tr,),
                          in_specs=[pl.BlockSpec((4, tr, W), lambda r: (0, r, 0))],
                          out_specs=pl.BlockSpec((tr, W), lambda r: (r, 0)),
                          out_shape=jax.ShapeDtypeStruct((R, W), F32))(q)


def _pair_share(r, name):
    R, W = r.shape

    def body(r_ref, out_ref, send_sem, recv_sem):
        x, y, c = _me()
        cp = pltpu.make_async_remote_copy(src_ref=r_ref, dst_ref=out_ref, send_sem=send_sem, recv_sem=recv_sem,
                                          device_id=(x, y, 1 - c), device_id_type=MESH)
        cp.start()
        cp.wait()

    return pl.pallas_call(
        body, name=name, in_specs=[_HBM], out_specs=_HBM,
        out_shape=jax.ShapeDtypeStruct((R, W), r.dtype),
        scratch_shapes=[pltpu.SemaphoreType.DMA(()), pltpu.SemaphoreType.DMA(())],
    )(r)


def _allreduce_small(vec):
    R, W = vec.shape

    def body(v_ref, out_ref, buf, send_sems, recv_sems):
        x, y, c = _me()
        me = 4 * x + 2 * y + c
        buf[me] = v_ref[...]
        cps = []
        for kk in range(1, 8):
            peer = (x ^ (kk >> 2), y ^ ((kk >> 1) & 1), c ^ (kk & 1))
            cp = pltpu.make_async_remote_copy(src_ref=v_ref, dst_ref=buf.at[me], send_sem=send_sems.at[kk - 1],
                                              recv_sem=recv_sems.at[kk - 1], device_id=peer, device_id_type=MESH)
            cp.start()
            cps.append(cp)
        for kk in range(1, 8):
            peer = (x ^ (kk >> 2), y ^ ((kk >> 1) & 1), c ^ (kk & 1))
            pltpu.make_async_remote_copy(src_ref=v_ref, dst_ref=buf.at[me ^ kk], send_sem=send_sems.at[kk - 1],
                                         recv_sem=recv_sems.at[kk - 1], device_id=peer, device_id_type=MESH).wait_recv()
        for cp in cps:
            cp.wait_send()
        acc = buf[0]
        for d in range(1, 8):
            acc = acc + buf[d]
        out_ref[...] = acc

    vm = pl.BlockSpec(memory_space=pltpu.VMEM)
    return pl.pallas_call(
        body, name="allreduce_small", in_specs=[vm], out_specs=vm, out_shape=jax.ShapeDtypeStruct((R, W), F32),
        scratch_shapes=[pltpu.VMEM((8, R, W), F32), pltpu.SemaphoreType.DMA((7,)), pltpu.SemaphoreType.DMA((7,))],
    )(vec)


def _pack_flat(parts, align, dtype):
    flat = jnp.concatenate([p.reshape(-1).astype(dtype) for p in parts])
    n = flat.shape[0]
    pad = (-n) % align
    return jnp.pad(flat, (0, pad)) if pad else flat


def _unpack_flat(flat, shapes):
    out, off = [], 0
    for s in shapes:
        n = math.prod(s)
        out.append(flat[off:off + n].reshape(s))
        off += n
    return out


def _full_from_shards(name, sh):
    if name in COL_SHARDED:
        return jnp.transpose(sh, (1, 0, 2)).reshape(sh.shape[1], 4 * sh.shape[2])
    return sh.reshape(4 * sh.shape[1], sh.shape[2])


def _shards_from_full(name, full):
    if name in COL_SHARDED:
        r, cc = full.shape[0], full.shape[1] // 4
        return jnp.transpose(full.reshape(r, 4, cc), (1, 0, 2)).reshape(4, r * cc)
    return full.reshape(4, -1)


def _pad_heads(w, width):
    k = w.shape[0]
    return jnp.pad(w.reshape(k, HEADS, width), ((0, 0), (0, 0), (0, HP - width))).reshape(k, HEADS * HP)


def _unpad_heads(w, width):
    k = w.shape[0]
    return w.reshape(k, HEADS, HP)[:, :, :width].reshape(k, HEADS * width)


def _blockdiag(t):
    g, a, b = t.shape
    return jnp.einsum("gab,gk->gakb", t, jnp.eye(g, dtype=t.dtype)).reshape(g * a, g * b)


def _blockdiag_t(m, a, b):
    g = m.shape[0] // a
    return jnp.einsum("gagb->gab", m.reshape(g, a, g, b))


def _local_step(x, mem, positions, target, W, sp, overlap=None):
    T = x.shape[0]
    row = lambda v: v.reshape(1, -1)

    ab_re, ab_im, bb_re, bb_im = _s5_discretize(sp["s5_lam_re"], sp["s5_lam_im"], sp["s5_log_dt"], sp["s5_b_re"], sp["s5_b_im"])
    ar, ai = ab_re.reshape(1, S5_N), ab_im.reshape(1, S5_N)
    tab_fwd, tab_rev = _s5_tables(ar, ai, False), _s5_tables(ar, -ai, True)
    b_st = jnp.concatenate([_blockdiag(jnp.swapaxes(bb_re, 1, 2)), _blockdiag(jnp.swapaxes(bb_im, 1, 2))],
                           axis=1).astype(BF16)
    c_st = jnp.concatenate([_blockdiag(jnp.swapaxes(sp["s5_c_re"], 1, 2)),
                            -_blockdiag(jnp.swapaxes(sp["s5_c_im"], 1, 2))], axis=0).astype(BF16)
    inv = ROPE_THETA ** (-jnp.arange(0, ROPE, 2, dtype=F32) / ROPE)
    ang = positions.astype(F32)[:, None] * inv
    cs, sn = jnp.cos(ang), jnp.sin(ang)
    one, zero = jnp.ones((T, NOPE), F32), jnp.zeros((T, NOPE), F32)
    z16, tail1, tail0 = jnp.zeros((T, ROPE // 2), F32), jnp.ones((T, HP - QK), F32), jnp.zeros((T, HP - QK), F32)
    t_cos = jnp.concatenate([one, cs, cs, tail1], axis=1)
    t_s1 = jnp.concatenate([zero, z16, sn, tail0], axis=1)
    t_s2 = jnp.concatenate([zero, -sn, z16, tail0], axis=1)

    w_in = W["w_in"]
    w_in_re = jnp.concatenate([w_in[:, 800:], w_in[:, :768], w_in[:, 768:800],
                               jnp.zeros((D_MODEL, HP - ROPE), w_in.dtype)], axis=1)
    w_uq_p = _pad_heads(W["w_uq"], QK)
    wkv = W["w_ukv"].reshape(S5_W, HEADS, NOPE + VD)
    w_uk_p = _pad_heads(wkv[:, :, :NOPE].reshape(S5_W, HEADS * NOPE), NOPE)
    w_uv_p = _pad_heads(wkv[:, :, NOPE:].reshape(S5_W, HEADS * VD), VD)
    w_oa_p = jnp.pad(W["w_oa"].reshape(HEADS, VD, D_MODEL), ((0, 0), (0, HP - VD), (0, 0))).reshape(HEADS * HP, D_MODEL)

    g_in, b_in = row(sp["ln_in_g"]), row(sp["ln_in_b"])
    g1, b1, g2, b2, g3, b3 = (row(sp[k]) for k in ("ln1_g", "ln1_b", "ln2_g", "ln2_b", "ln3_g", "ln3_b"))
    gq, gkv, d_skip = row(sp["q_norm_g"]), row(sp["kv_norm_g"]), row(sp["s5_d"])

    (h0,) = _rowwise(lambda a, g, b: (_ln(a, g, b),), "ln_in_fwd", [_whole(x)], [g_in, b_in], [(D_MODEL, F32)])
    z = _mm(h0, w_in_re, "in_proj", tm=512, tn=Z_W, out_dtype=BF16)
    hs, yc, yg = _s5_fwd(z, b_st, c_st, d_skip, tab_fwd, "s5_fwd")
    y2 = _mm(yg, W["w_glu"], "glu_proj", tn=2048, out_dtype=BF16)
    cqn, ckvn = _rowwise(lambda a, b, ga, gb: (_rms(a, ga), _rms(b, gb)), "mla_norm_fwd",
                         [(z, Z_CQ, S5_W), (z, Z_CKV, S5_W)], [gq, gkv], [(S5_W, BF16)] * 2)
    q_raw = _mm(cqn, w_uq_p, "mla_uq")
    k_raw = _mm(ckvn, w_uk_p, "mla_uk")
    v = _mm(ckvn, w_uv_p, "mla_uv", out_dtype=BF16)

    def rope_fwd(qr, kr_raw, krz, tc_, t1, t2):
        q = _rope(qr, _tile8(tc_), _tile8(t1), _tile8(t2))
        kr = _rope(pltpu.roll(krz, NOPE, 1), tc_, t1, t2)
        return q, kr_raw + _tile8(kr)

    q, k = _rowwise(rope_fwd, "mla_rope_fwd", [_whole(q_raw), _whole(k_raw), (z, Z_KR, HP), _whole(t_cos), _whole(t_s1), _whole(t_s2)],
                    [], [(HEADS * HP, BF16)] * 2)
    if overlap is None:
        o, lse = _flash_fwd(q, k, v, "mla_attn_fwd")
    else:
        o, lse, landed = _flash_fwd(q, k, v, "mla_attn_fwd", carry=(overlap[0], False))
        W = {**W, **overlap[1](landed)}
    a_out = _mm(o, w_oa_p, "mla_oa", out_dtype=BF16)
    (mixin,) = _rowwise(lambda *a: (_gate_mix(*a),), "gate_mix_fwd",
                        [(z, Z_GS, D_MODEL), (z, Z_GA, D_MODEL), (y2, 0, D_MODEL), (y2, D_MODEL, D_MODEL), _whole(a_out)],
                        [], [(D_MODEL, BF16)])
    mix = _mm(mixin, W["w_o"], "mix_o")
    (h1,) = _rowwise(lambda h, r, g, b: (_ln_res(h, r, g, b),), "ln1_fwd", [_whole(h0), _whole(mix)], [g1, b1], [(D_MODEL, F32)])
    xq = _mm(h1, W["w_xq"], "x_q", out_dtype=BF16)
    xk = _mm(mem, W["w_xk"], "x_k", out_dtype=BF16)
    xv = _mm(mem, W["w_xv"], "x_v", out_dtype=BF16)
    (xo,) = _rowwise(_xattn_fwd_fn, "xattn_fwd", [_whole(xq)], [xk, xv], [(D_MODEL, BF16)], tm=1024, upcast=False)
    xa = _mm(xo, W["w_xo"], "x_o")
    (h2,) = _rowwise(lambda h, r, g, b: (_ln_res(h, r, g, b),), "ln2_fwd", [_whole(h1), _whole(xa)], [g2, b2], [(D_MODEL, F32)])
    a_up = _mm(h2, W["w_up"], "mlp_up", out_dtype=BF16)
    ff = _mm(a_up, W["w_down"], "mlp_down", act=_relu2)

    def loss_fn(h, r, tgt, g, b):
        def f(r_, g_, b_):
            e = _ln_res(h, r_, g_, b_) - tgt
            return 0.5 * jnp.sum(jnp.mean(e * e, axis=-1))
        lv, (dr_, dg_, db_) = jax.value_and_grad(f, argnums=(0, 1, 2))(r, g, b)
        return dr_, dg_, db_, jnp.broadcast_to(lv, (1, 128))

    dr3, dg3, db3, lossv = _rowwise(loss_fn, "ln3_loss_bwd", [_whole(h2), _whole(ff), _whole(target)], [g3, b3],
                                    [(D_MODEL, F32)], [(1, D_MODEL), (1, D_MODEL), (1, 128)])
    gW, gs = {}, {"ln3_g": dg3, "ln3_b": db3}

    da = _mm(dr3, W["w_down"], "mlp_down_bwd_a", "nt", epi=lambda acc, a: acc * (2.0 * jnp.maximum(a, 0.0)), extras=(a_up,),
             out_dtype=BF16)
    gW["w_down"] = _mm(a_up, dr3, "mlp_down_bwd_w", "tn", m=4 * D_MODEL, n=D_MODEL, act=_relu2)
    gW["w_up"] = _mm(h2, da, "mlp_up_bwd_w", "tn", m=D_MODEL, n=4 * D_MODEL)
    dh2 = _mm(da, W["w_up"], "mlp_up_bwd_a", "nt", epi=lambda acc, e: acc + DN_ALPHA * e, extras=(dr3,))

    def ln_res_bwd(h, r, dy, g, b):
        _, vjp = jax.vjp(_ln_res, h, r, g, b)
        _, dr_, dg_, db_ = vjp(dy)
        return dr_, dg_, db_

    dr2, gs["ln2_g"], gs["ln2_b"] = _rowwise(ln_res_bwd, "ln2_bwd", [_whole(h1), _whole(xa), _whole(dh2)], [g2, b2],
                                             [(D_MODEL, F32)], [(1, D_MODEL)] * 2)
    dxo = _mm(dr2, W["w_xo"], "x_o_bwd_a", "nt", out_dtype=BF16)
    gW["w_xo"] = _mm(xo, dr2, "x_o_bwd_w", "tn", m=D_MODEL, n=D_MODEL)
    dxq, dxk, dxv = _rowwise(_xattn_bwd_fn, "xattn_bwd", [_whole(xq), _whole(dxo)], [xk, xv], [(D_MODEL, BF16)],
                             [(xk.shape[0], D_MODEL)] * 2, tm=1024, upcast=False)
    gW["w_xq"] = _mm(h1, dxq, "x_q_bwd_w", "tn", m=D_MODEL, n=D_MODEL)
    gW["w_xk"] = _mm(mem, dxk, "x_k_bwd_w", "tn", m=D_MODEL, n=D_MODEL)
    gW["w_xv"] = _mm(mem, dxv, "x_v_bwd_w", "tn", m=D_MODEL, n=D_MODEL)
    dh1 = _mm(dxq, W["w_xq"], "x_q_bwd_a", "nt", epi=lambda acc, e: acc + DN_ALPHA * e, extras=(dr2,))

    dr1, gs["ln1_g"], gs["ln1_b"] = _rowwise(ln_res_bwd, "ln1_bwd", [_whole(h0), _whole(mix), _whole(dh1)], [g1, b1],
                                             [(D_MODEL, F32)], [(1, D_MODEL)] * 2)
    dmixin = _mm(dr1, W["w_o"], "mix_o_bwd_a", "nt")
    gW["w_o"] = _mm(mixin, dr1, "mix_o_bwd_w", "tn", m=D_MODEL, n=D_MODEL)

    def gate_mix_bwd(gs_, ga_, ya, yb, ao, dy):
        _, vjp = jax.vjp(_gate_mix, gs_, ga_, ya, yb, ao)
        return vjp(dy)

    dgs, dga, dy2a, dy2b, da_out = _rowwise(
        gate_mix_bwd, "gate_mix_bwd",
        [(z, Z_GS, D_MODEL), (z, Z_GA, D_MODEL), (y2, 0, D_MODEL), (y2, D_MODEL, D_MODEL), _whole(a_out), _whole(dmixin)],
        [], [(D_MODEL, BF16)] * 5)

    do = _mm(da_out, w_oa_p, "mla_oa_bwd_a", "nt")
    g_oa_p = _mm(o, da_out, "mla_oa_bwd_w", "tn", m=HEADS * HP, n=D_MODEL)
    gW["w_oa"] = g_oa_p.reshape(HEADS, HP, D_MODEL)[:, :VD].reshape(HEADS * VD, D_MODEL)
    delta, do16 = _rowwise(_attn_delta_fn, "mla_attn_delta", [_whole(do), _whole(o)], [], [(HEADS, F32), (HEADS * HP, BF16)])
    delta_row = delta.T.reshape(HEADS, 1, T)
    if overlap is None:
        dq, dk, dv = _flash_bwd(q, k, v, do16, lse, delta_row, "mla_attn_bwd")
        riding = None
    else:
        sent = overlap[2]({n: gW[n] for n in BIG_LATE})
        dq, dk, dv, landed = _flash_bwd(q, k, v, do16, lse, delta_row, "mla_attn_bwd", carry=(sent, True))
        riding = (sent, landed)

    def rope_bwd(dq_, dk_, tc_, t1, t2):
        dqr = _rope_t(dq_, _tile8(tc_), _tile8(t1), _tile8(t2))
        dkr = dk_[:, 0:HP]
        for hh in range(1, HEADS):
            dkr = dkr + dk_[:, hh * HP:(hh + 1) * HP]
        dkr = _rope_t(jnp.where(_rope_lanes(dkr.shape), dkr, 0.0), tc_, t1, t2)
        dkr = pltpu.roll(dkr, NOPE, 1)
        lane = lax.broadcasted_iota(jnp.int32, dkr.shape, 1)
        return dqr, jnp.where(lane < ROPE, dkr, 0.0)

    dq_raw, dkr = _rowwise(rope_bwd, "mla_rope_bwd", [_whole(dq), _whole(dk), _whole(t_cos), _whole(t_s1), _whole(t_s2)], [],
                           [(HEADS * HP, BF16), (HP, BF16)])
    dcqn = _mm(dq_raw, w_uq_p, "mla_uq_bwd_a", "nt")
    gW["w_uq"] = _unpad_heads(_mm(cqn, dq_raw, "mla_uq_bwd_w", "tn", m=S5_W, n=HEADS * HP), QK)
    dckvn_k = _mm(dk, w_uk_p, "mla_uk_bwd_a", "nt")
    dckvn = _mm(dv, w_uv_p, "mla_uv_bwd_a", "nt", epi=lambda acc, e: acc + e, extras=(dckvn_k,))
    g_uk = _unpad_heads(_mm(ckvn, dk, "mla_uk_bwd_w", "tn", m=S5_W, n=HEADS * HP), NOPE).reshape(S5_W, HEADS, NOPE)
    g_uv = _unpad_heads(_mm(ckvn, dv, "mla_uv_bwd_w", "tn", m=S5_W, n=HEADS * HP), VD).reshape(S5_W, HEADS, VD)
    gW["w_ukv"] = jnp.concatenate([g_uk, g_uv], axis=2).reshape(S5_W, HEADS * (NOPE + VD))

    def norm_bwd(a, b, da_, db_, ga, gb):
        _, vja = jax.vjp(_rms, a, ga)
        _, vjb = jax.vjp(_rms, b, gb)
        dxa, dga_ = vja(da_)
        dxb, dgb_ = vjb(db_)
        return dxa, dxb, dga_, dgb_

    dcq, dckv, gs["q_norm_g"], gs["kv_norm_g"] = _rowwise(
        norm_bwd, "mla_norm_bwd", [(z, Z_CQ, S5_W), (z, Z_CKV, S5_W), _whole(dcqn), _whole(dckvn)], [gq, gkv],
        [(S5_W, BF16)] * 2, [(1, S5_W)] * 2)

    dy2 = jnp.concatenate([dy2a, dy2b], axis=1)
    dyg = _mm(dy2, W["w_glu"], "glu_bwd_a", "nt")
    gW["w_glu"] = _mm(yg, dy2, "glu_bwd_w", "tn", m=S5_W, n=2 * D_MODEL)

    du, gs["s5_ab"], g_bst, g_cst_t, gs["s5_d"] = _s5_bwd(dyg, yc, z, hs, b_st, c_st, d_skip, tab_rev, "s5_bwd")
    gs["s5_bb_re"] = jnp.swapaxes(_blockdiag_t(g_bst[:, :S5_N], S5_H, S5_P), 1, 2)
    gs["s5_bb_im"] = jnp.swapaxes(_blockdiag_t(g_bst[:, S5_N:], S5_H, S5_P), 1, 2)
    gs["s5_c_re"] = _blockdiag_t(g_cst_t[:, :S5_N], S5_H, S5_P)
    gs["s5_c_im"] = -_blockdiag_t(g_cst_t[:, S5_N:], S5_H, S5_P)

    dz = jnp.concatenate([dgs, dga, du, dcq, dckv, dkr], axis=1)
    g_in_re = _mm(h0, dz, "in_proj_bwd_w", "tn", m=D_MODEL, n=Z_W, tm=512, tn=Z_W)
    gW["w_in"] = jnp.concatenate([g_in_re[:, Z_U:Z_KR], g_in_re[:, Z_KR:Z_KR + ROPE], g_in_re[:, :Z_U]], axis=1)
    dh0 = _mm(dz, w_in_re, "in_proj_bwd_a", "nt", tm=512, tk=Z_W, epi=lambda acc, e: acc + DN_ALPHA * e, extras=(dr1,))

    def ln_bwd(a, dy, g, b):
        _, vjp = jax.vjp(_ln, a, g, b)
        return vjp(dy)

    dx, gs["ln_in_g"], gs["ln_in_b"] = _rowwise(ln_bwd, "ln_in_bwd", [_whole(x), _whole(dh0)], [g_in, b_in],
                                                [(D_MODEL, F32)], [(1, D_MODEL)] * 2)
    return lossv[:, :1], dx, gW, gs, riding


_RAW_SMALL = (("loss", (1, 1)), ("ln_in_g", (1, D_MODEL)), ("ln_in_b", (1, D_MODEL)), ("ln1_g", (1, D_MODEL)),
              ("ln1_b", (1, D_MODEL)), ("ln2_g", (1, D_MODEL)), ("ln2_b", (1, D_MODEL)), ("ln3_g", (1, D_MODEL)),
              ("ln3_b", (1, D_MODEL)), ("q_norm_g", (1, S5_W)), ("kv_norm_g", (1, S5_W)), ("s5_d", (1, S5_W)),
              ("s5_ab", (1, 2 * S5_N)), ("s5_bb_re", (S5_G, S5_P, S5_H)), ("s5_bb_im", (S5_G, S5_P, S5_H)),
              ("s5_c_re", (S5_G, S5_H, S5_P)), ("s5_c_im", (S5_G, S5_H, S5_P)))


def kernel(x, mem, positions, ln_in_g, ln_in_b, w_in, s5_lam_re, s5_lam_im, s5_log_dt, s5_b_re, s5_b_im, s5_c_re, s5_c_im, s5_d, w_glu, q_norm_g, w_uq, kv_norm_g, w_ukv, w_oa, w_o, ln1_g, ln1_b, w_xq, w_xk, w_xv, w_xo, ln2_g, ln2_b, w_up, w_down, ln3_g, ln3_b, loss_target, m_ln_in_g, m_ln_in_b, m_w_in, m_s5_lam_re, m_s5_lam_im, m_s5_log_dt, m_s5_b_re, m_s5_b_im, m_s5_c_re, m_s5_c_im, m_s5_d, m_w_glu, m_q_norm_g, m_w_uq, m_kv_norm_g, m_w_ukv, m_w_oa, m_w_o, m_ln1_g, m_ln1_b, m_w_xq, m_w_xk, m_w_xv, m_w_xo, m_ln2_g, m_ln2_b, m_w_up, m_w_down, m_ln3_g, m_ln3_b, v_ln_in_g, v_ln_in_b, v_w_in, v_s5_lam_re, v_s5_lam_im, v_s5_log_dt, v_s5_b_re, v_s5_b_im, v_s5_c_re, v_s5_c_im, v_s5_d, v_w_glu, v_q_norm_g, v_w_uq, v_kv_norm_g, v_w_ukv, v_w_oa, v_w_o, v_ln1_g, v_ln1_b, v_w_xq, v_w_xk, v_w_xv, v_w_xo, v_ln2_g, v_ln2_b, v_w_up, v_w_down, v_ln3_g, v_ln3_b):
    a = dict(locals())
    wts = {n: a[n] for n in WEIGHTS}
    ms = {n: a["m_" + n] for n in WEIGHTS}
    vs = {n: a["v_" + n] for n in WEIGHTS}
    shard2d = {n: wts[n].reshape(wts[n].shape[-2], wts[n].shape[-1]) for n in BIG}
    nrow = {n: shard2d[n].size // 1024 for n in BIG}
    mine = 2 * lax.axis_index("x") + lax.axis_index("y")
    my_c = lax.axis_index("c")
    cidx = my_c.astype(jnp.int32).reshape(1)

    def group_rows(group):
        rows = sum(nrow[n] for n in group)
        return rows, (-rows) % PACK_ROWS

    def pack_shards(group):
        rows, pad = group_rows(group)
        parts = [shard2d[n].astype(BF16).reshape(nrow[n], 1024) for n in group] + [jnp.zeros((pad, 1024), BF16)]
        return jnp.concatenate(parts, axis=0).reshape(2, (rows + pad) // 2, 1024)

    def unpack_full(group, gathered):
        out, off = {}, 0
        for n in group:
            r, cc = shard2d[n].shape
            piece = lax.optimization_barrier(gathered[:, off:off + nrow[n]])
            out[n] = _full_from_shards(n, piece.reshape(4, r, cc))
            off += nrow[n]
        return out

    def pack_grads(group, gw):
        rows, pad = group_rows(group)
        parts = [_shards_from_full(n, gw[n]).reshape(4, nrow[n], 1024) for n in group] + [jnp.zeros((4, pad, 1024), F32)]
        return jnp.concatenate(parts, axis=1).reshape(4, 2, (rows + pad) // 2, 1024)

    def by_core(own, other):
        is_mine = (jnp.arange(2) == my_c).reshape((2,) + (1,) * own.ndim)
        return jnp.where(is_mine, own[None], other[None])

    packed_e, packed_l = pack_shards(BIG_EARLY), pack_shards(BIG_LATE)
    gathered = lax.dynamic_update_slice(_gather_weights(packed_e), packed_e[None], (mine, 0, 0, 0))
    W = unpack_full(BIG_EARLY, gathered.reshape(4, -1, 1024))

    def finish_gather(landed):
        half = packed_l.shape[1]
        own_half = lax.dynamic_index_in_dim(packed_l, my_c, axis=0, keepdims=True)
        landed = lax.dynamic_update_slice(landed, own_half, (mine, 0, 0))
        other = _pair_share(landed.reshape(4 * half, 1024), "gather_late_share").reshape(4, half, 1024)
        both = jnp.moveaxis(by_core(landed, other), 0, 1)
        return unpack_full(BIG_LATE, both.reshape(4, 2 * half, 1024))

    def pair_reduce(group, gw, tag):
        gflat = pack_grads(group, gw)
        return _pair_add(gflat, _pair_exchange(gflat, "grad_pair_exchange_" + tag), cidx, "grad_pair_add_" + tag)

    def finish_reduce(group, sent, landed, tag):
        own = lax.dynamic_slice(sent, (mine, 0, 0), (1,) + sent.shape[1:])
        red = _chip_sum(lax.dynamic_update_slice(landed, own, (mine, 0, 0)), "grad_chip_sum_" + tag)
        gsh = by_core(red, _pair_share(red, "grad_pair_share_" + tag)).reshape(-1, 1024)
        out, off = {}, 0
        for n in group:
            out[n] = lax.optimization_barrier(gsh[off:off + nrow[n]]).reshape(shard2d[n].shape)
            off += nrow[n]
        return out

    sp = {n: wts[n] for n in SMALL}
    sp_local = {n: (sp[n][0] if sp[n].ndim > 1 else sp[n]) for n in SMALL}
    overlap = (packed_l, finish_gather, lambda gw: pair_reduce(BIG_LATE, gw, "late"))
    lossv, dx, gW, gs, riding = _local_step(x[0], mem[0], positions[0], loss_target[0], W, sp_local, overlap)

    g_out = finish_reduce(BIG_LATE, riding[0], riding[1], "late")
    sent_e = pair_reduce(BIG_EARLY, gW, "early")
    g_out.update(finish_reduce(BIG_EARLY, sent_e, _chip_exchange(sent_e, "grad_chip_exchange_early"), "early"))

    gs["loss"] = lossv
    raw = _pack_flat([gs[n].reshape(s) for n, s in _RAW_SMALL], 8 * 1024, F32)
    raw = _allreduce_small(raw.reshape(-1, 1024)).reshape(-1)
    rs = dict(zip([n for n, _ in _RAW_SMALL], _unpack_flat(raw, [s for _, s in _RAW_SMALL])))
    loss = rs["loss"].reshape(())
    _, disc_vjp = jax.vjp(_s5_discretize, sp_local["s5_lam_re"], sp_local["s5_lam_im"], sp_local["s5_log_dt"],
                          sp_local["s5_b_re"], sp_local["s5_b_im"])
    d_ab = rs["s5_ab"].reshape(2, S5_G, S5_P)
    g_lre, g_lim, g_ldt, g_bre, g_bim = disc_vjp((d_ab[0], d_ab[1], rs["s5_bb_re"], rs["s5_bb_im"]))
    small_g = {"s5_lam_re": g_lre, "s5_lam_im": g_lim, "s5_log_dt": g_ldt, "s5_b_re": g_bre, "s5_b_im": g_bim,
               "s5_c_re": rs["s5_c_re"], "s5_c_im": rs["s5_c_im"]}
    for n in ("ln_in_g", "ln_in_b", "ln1_g", "ln1_b", "ln2_g", "ln2_b", "ln3_g", "ln3_b", "q_norm_g", "kv_norm_g", "s5_d"):
        small_g[n] = rs[n]

    grads, deltas, new_m, new_v = {}, {}, {}, {}
    for n in BIG:
        d_, m_, v_ = _adamw(shard2d[n], g_out[n], ms[n].reshape(shard2d[n].shape), vs[n].reshape(shard2d[n].shape), "adamw_" + n)
        grads[n] = g_out[n].reshape(wts[n].shape)
        deltas[n], new_m[n], new_v[n] = (t.reshape(wts[n].shape) for t in (d_, m_, v_))
    packs = [_pack_flat([t[n] for n in SMALL], 8 * 1024, F32).reshape(-1, 1024)
             for t in (wts, {n: small_g[n].reshape(wts[n].shape) for n in SMALL}, ms, vs)]
    sd, sm, sv = _adamw(*packs, "adamw_small")
    shapes = [wts[n].shape for n in SMALL]
    for n, d_, m_, v_ in zip(SMALL, _unpack_flat(sd.reshape(-1), shapes), _unpack_flat(sm.reshape(-1), shapes),
                             _unpack_flat(sv.reshape(-1), shapes)):
        grads[n] = small_g[n].reshape(wts[n].shape)
        deltas[n], new_m[n], new_v[n] = d_, m_, v_

    return (loss, dx[None], *[grads[n] for n in WEIGHTS], *[deltas[n] for n in WEIGHTS],
            *[new_m[n] for n in WEIGHTS], *[new_v[n] for n in WEIGHTS])
```

```python
import functools
import math

import jax
import jax.numpy as jnp
from jax import lax
from jax.experimental import pallas as pl
from jax.experimental.pallas import tpu as pltpu

F32 = jnp.float32
BF16 = jnp.bfloat16
MESH = pl.DeviceIdType.MESH

D_MODEL = 1024
S5_W = 256
S5_G = 16
S5_H = 16
S5_P = 64
S5_N = S5_G * S5_P
S5_MAX_RE = -1e-4
HEADS = 8
NOPE = 64
ROPE = 32
QK = NOPE + ROPE
VD = 64
HP = 128
XH = 4
XD = 256
LN_EPS = 1e-5
RMS_EPS = 1e-6
NEG_INF = -1e30
DN_ALPHA = 2.0 ** 0.25
ROPE_THETA = 10000.0
ADAM_LR, ADAM_B1, ADAM_B2, ADAM_EPS, ADAM_WD, ADAM_STEP = 0.001, 0.9, 0.999, 1e-08, 0.01, 10

Z_GS, Z_GA, Z_U, Z_CQ, Z_CKV, Z_KR, Z_W = 0, 1024, 2048, 2304, 2560, 2816, 2944
SCAN_CHUNK = 256

BIG_EARLY = ("w_in", "w_glu", "w_uq", "w_ukv", "w_oa", "w_o")
BIG_LATE = ("w_up", "w_xq", "w_xk", "w_xv", "w_xo", "w_down")
BIG = BIG_EARLY + BIG_LATE
COL_SHARDED = ("w_in", "w_glu", "w_uq", "w_ukv", "w_oa", "w_up")
SMALL = ("ln_in_g", "ln_in_b", "s5_lam_re", "s5_lam_im", "s5_log_dt", "s5_b_re", "s5_b_im", "s5_c_re", "s5_c_im",
         "s5_d", "q_norm_g", "kv_norm_g", "ln1_g", "ln1_b", "ln2_g", "ln2_b", "ln3_g", "ln3_b")
WEIGHTS = ("ln_in_g", "ln_in_b", "w_in", "s5_lam_re", "s5_lam_im", "s5_log_dt", "s5_b_re", "s5_b_im", "s5_c_re",
           "s5_c_im", "s5_d", "w_glu", "q_norm_g", "w_uq", "kv_norm_g", "w_ukv", "w_oa", "w_o", "ln1_g", "ln1_b",
           "w_xq", "w_xk", "w_xv", "w_xo", "ln2_g", "ln2_b", "w_up", "w_down", "ln3_g", "ln3_b")
PACK_ROWS = 2 * 16


def _row_tile(n, cap=512):
    best = n
    for t in range(8, min(n, cap) + 1, 8):
        if n % t == 0:
            best = t
    return best


def _pick(n, cap):
    best = None
    for t in range(128, min(n, cap) + 1, 128):
        if n % t == 0:
            best = t
    return best if best is not None and (best >= 512 or best == n) else n


def _mm(a, b, name, mode="nn", *, a_off=0, b_off=0, m=None, n=None, act=None, epi=None, extras=(),
        out_dtype=F32, tm=1024, tn=1024, tk=None):
    if mode == "nn":
        M, (K, N) = a.shape[0], b.shape
    elif mode == "nt":
        M, (N, K) = a.shape[0], b.shape
    else:
        K, M, N = a.shape[0], m, n
    if mode == "tn":
        tm, tn, tk = _pick(M, tm), _pick(N, tn), min(tk or 512, K)
    else:
        tm, tn, tk = min(tm, M), _pick(N, tn), _pick(K, tk or 1024)
    assert M % tm == 0 and N % tn == 0 and K % tk == 0, (name, M, N, K, tm, tn, tk)
    nk = K // tk
    if mode == "tn":
        assert a_off % tm == 0 and b_off % tn == 0
        ao, bo = a_off // tm, b_off // tn
        a_spec = pl.BlockSpec((tk, tm), lambda i, j, k: (k, i + ao))
        b_spec = pl.BlockSpec((tk, tn), lambda i, j, k: (k, j + bo))
        dims = (((0,), (0,)), ((), ()))
    else:
        assert a_off % tk == 0
        ao = a_off // tk
        a_spec = pl.BlockSpec((tm, tk), lambda i, j, k: (i, k + ao))
        if mode == "nn":
            b_spec = pl.BlockSpec((tk, tn), lambda i, j, k: (k, j))
            dims = (((1,), (0,)), ((), ()))
        else:
            b_spec = pl.BlockSpec((tn, tk), lambda i, j, k: (j, k))
            dims = (((1,), (1,)), ((), ()))
    e_spec = pl.BlockSpec((tm, tn), lambda i, j, k: (i, j))
    n_extra = len(extras)

    def body(*refs):
        a_ref, b_ref = refs[0], refs[1]
        e_refs = refs[2:2 + n_extra]
        o_ref = refs[2 + n_extra]
        av = a_ref[...]
        if act is not None:
            av = act(av.astype(F32))
        p = lax.dot_general(av.astype(BF16), b_ref[...].astype(BF16), dims, preferred_element_type=F32)

        def finish(r):
            if epi is not None:
                r = epi(r, *[e[...] for e in e_refs])
            o_ref[...] = r.astype(o_ref.dtype)

        if nk == 1:
            finish(p)
        else:
            acc = refs[3 + n_extra]
            k = pl.program_id(2)

            @pl.when(k == 0)
            def _():
                acc[...] = p

            @pl.when(k > 0)
            def _():
                acc[...] += p

            @pl.when(k == nk - 1)
            def _():
                finish(acc[...])

    return pl.pallas_call(
        body, name=name, grid=(M // tm, N // tn, nk),
        in_specs=[a_spec, b_spec] + [e_spec] * n_extra, out_specs=e_spec,
        out_shape=jax.ShapeDtypeStruct((M, N), out_dtype),
        scratch_shapes=[pltpu.VMEM((tm, tn), F32)] if nk > 1 else [],
        compiler_params=pltpu.CompilerParams(dimension_semantics=("parallel", "parallel", "arbitrary")),
    )(a, b, *extras)


def _rowwise(fn, name, rows, pars, outs, accs=(), tm=256, upcast=True):
    T = rows[0][0].shape[0]
    tm = min(tm, T)
    assert T % tm == 0
    n_in, n_out = len(rows) + len(pars), len(outs)
    in_specs = []
    for arr, off, w in rows:
        assert off % w == 0 and arr.shape[0] == T, name
        in_specs.append(pl.BlockSpec((tm, w), functools.partial(lambda i, o: (i, o), o=off // w)))
    for p in pars:
        in_specs.append(pl.BlockSpec(p.shape, lambda i: (0, 0)))
    out_specs = [pl.BlockSpec((tm, w), lambda i: (i, 0)) for w, _ in outs]
    out_specs += [pl.BlockSpec(s, lambda i: (0, 0)) for s in accs]
    out_shape = [jax.ShapeDtypeStruct((T, w), dt) for w, dt in outs] + [jax.ShapeDtypeStruct(s, F32) for s in accs]

    def body(*refs):
        res = fn(*[r[...].astype(F32) if upcast else r[...] for r in refs[:n_in]])
        o_refs = refs[n_in:]
        for r, v in zip(o_refs[:n_out], res[:n_out]):
            r[...] = v.astype(r.dtype)
        if accs:
            i = pl.program_id(0)

            @pl.when(i == 0)
            def _():
                for r in o_refs[n_out:]:
                    r[...] = jnp.zeros_like(r)

            for r, v in zip(o_refs[n_out:], res[n_out:]):
                r[...] += v

    res = pl.pallas_call(
        body, name=name, grid=(T // tm,), in_specs=in_specs, out_specs=out_specs, out_shape=out_shape,
        compiler_params=pltpu.CompilerParams(dimension_semantics=("arbitrary",)),
    )(*[r[0] for r in rows], *pars)
    return res


def _whole(arr):
    return (arr, 0, arr.shape[1])


def _ln(x, g, b):
    mu = jnp.mean(x, axis=-1, keepdims=True)
    xc = x - mu
    var = jnp.mean(xc * xc, axis=-1, keepdims=True)
    return xc * lax.rsqrt(var + LN_EPS) * g + b


def _ln_res(h, r, g, b):
    return _ln(DN_ALPHA * h + r, g, b)


def _rms(x, g):
    return x * lax.rsqrt(jnp.mean(x * x, axis=-1, keepdims=True) + RMS_EPS) * g


def _gelu_skip(yc, u, d):
    y = yc + d * u
    return 0.5 * y * (1.0 + lax.erf(y * (1.0 / math.sqrt(2.0))))


def _gate_mix(gs, ga, y2a, y2b, aout):
    return jax.nn.sigmoid(gs) * (y2a * jax.nn.sigmoid(y2b)) + jax.nn.sigmoid(ga) * aout


def _relu2(a):
    r = jnp.maximum(a, 0.0)
    return r * r


def _tile8(t):
    return jnp.concatenate([t] * HEADS, axis=1)


def _rope(x, cos, s1, s2):
    w = x.shape[1]
    return x * cos + pltpu.roll(x, ROPE // 2, 1) * s1 + pltpu.roll(x, w - ROPE // 2, 1) * s2


def _rope_t(dy, cos, s1, s2):
    w = dy.shape[1]
    return dy * cos + pltpu.roll(dy * s1, w - ROPE // 2, 1) + pltpu.roll(dy * s2, ROPE // 2, 1)


def _rope_lanes(shape):
    lane = lax.broadcasted_iota(jnp.int32, shape, 1) % HP
    return (lane >= NOPE) & (lane < QK)


ATT_HB = 4
ATT_SCALE = QK ** -0.5
LOG2E = 1.4426950408889634
LN2 = 0.6931471805599453
_NT = (((1,), (1,)), ((), ()))
_TN = (((0,), (0,)), ((), ()))


def _carry_specs(carry):
    if carry is None:
        return [], [], [], []
    arr = carry[0]
    return ([_HBM], [_HBM], [jax.ShapeDtypeStruct((4,) + arr.shape[1:], arr.dtype)],
            [pltpu.SemaphoreType.DMA((3,)), pltpu.SemaphoreType.DMA((3,))])


def _carry_start(exchange, when):
    @pl.when(when)
    def _():
        for cp in exchange(False):
            cp.start()


def _carry_wait(exchange, when):
    @pl.when(when)
    def _():
        for cp in exchange(True):
            cp.wait_recv()
        for cp in exchange(False):
            cp.wait_send()


def _tri_tables(n, by_row):
    if by_row:
        pairs = [(i, j) for i in range(n) for j in range(i + 1)]
    else:
        pairs = [(i, j) for j in range(n) for i in range(j, n)]
    return jnp.array([p[0] for p in pairs], jnp.int32), jnp.array([p[1] for p in pairs], jnp.int32)


def _flash_fwd(q, k, v, name, bq=512, carry=None):
    T = q.shape[0]
    bq = min(bq, T)
    nq = T // bq
    hb = ATT_HB
    n_carry = 0 if carry is None else 1

    i_tab, j_tab = _tri_tables(nq, by_row=True)

    def body(it_ref, jt_ref, q_ref, k_ref, v_ref, *rest):
        o_ref, lse_ref = rest[n_carry:n_carry + 2]
        m_sc, l_sc, acc_sc = rest[2 * n_carry + 2:2 * n_carry + 5]
        t = pl.program_id(1)
        i, j = it_ref[t], jt_ref[t]
        if carry is not None:
            exchange = functools.partial(_carried_exchange, rest[0], rest[3], rest[-2], rest[-1], carry[1])
            _carry_start(exchange, (pl.program_id(0) == 0) & (t == 0))

        @pl.when(j == 0)
        def _():
            m_sc[...] = jnp.full_like(m_sc, NEG_INF)
            l_sc[...] = jnp.zeros_like(l_sc)
            acc_sc[...] = jnp.zeros_like(acc_sc)

        def step(masked):
            for hh in range(hb):
                sl = slice(hh * HP, (hh + 1) * HP)
                s = lax.dot_general(q_ref[:, sl], k_ref[:, sl], _NT, preferred_element_type=F32)
                if masked:
                    row = lax.broadcasted_iota(jnp.int32, (bq, bq), 0)
                    col = lax.broadcasted_iota(jnp.int32, (bq, bq), 1)
                    s = jnp.where(col <= row, s, NEG_INF)
                m_prev = m_sc[hh]
                m_new = jnp.maximum(m_prev, jnp.max(s, axis=1, keepdims=True))
                alpha = jnp.exp2(m_prev - m_new)
                p = jnp.exp2(s - jnp.concatenate([m_new] * (bq // HP), axis=1))
                l_sc[hh] = alpha * l_sc[hh] + jnp.sum(p, axis=1, keepdims=True)
                acc_sc[hh] = alpha * acc_sc[hh] + jnp.dot(p.astype(BF16), v_ref[:, sl], preferred_element_type=F32)
                m_sc[hh] = m_new

        @pl.when(j < i)
        def _():
            step(False)

        @pl.when(j == i)
        def _():
            step(True)
            for hh in range(hb):
                o_ref[:, hh * HP:(hh + 1) * HP] = acc_sc[hh] / l_sc[hh]
                lse_ref[hh] = jnp.transpose((m_sc[hh] + jnp.log2(l_sc[hh])) * LN2)[0:1, :]

        if carry is not None:
            _carry_wait(exchange, (pl.program_id(0) == HEADS // hb - 1) & (t == i_tab.shape[0] - 1))

    qs = pl.BlockSpec((bq, hb * HP), lambda h, t, it, jt: (it[t], h))
    ks = pl.BlockSpec((bq, hb * HP), lambda h, t, it, jt: (jt[t], h))
    c_in, c_out, c_shape, c_sems = _carry_specs(carry)
    gs = pltpu.PrefetchScalarGridSpec(
        num_scalar_prefetch=2, grid=(HEADS // hb, i_tab.shape[0]), in_specs=[qs, ks, ks] + c_in,
        out_specs=[qs, pl.BlockSpec((hb, 1, bq), lambda h, t, it, jt: (h, 0, it[t]))] + c_out,
        scratch_shapes=[pltpu.VMEM((hb, bq, HP), F32)] * 3 + c_sems)
    return pl.pallas_call(
        body, name=name, grid_spec=gs,
        out_shape=[jax.ShapeDtypeStruct((T, HEADS * HP), F32), jax.ShapeDtypeStruct((HEADS, 1, T), F32)] + c_shape,
        compiler_params=pltpu.CompilerParams(dimension_semantics=("arbitrary", "arbitrary")),
    )(i_tab, j_tab, q, k, v, *([] if carry is None else [carry[0]]))


def _attn_delta_fn(do, o):
    prod = do * o
    cols = [jnp.sum(prod[:, h * HP:(h + 1) * HP], axis=1, keepdims=True) for h in range(HEADS)]
    return jnp.concatenate(cols, axis=1), do


def _flash_bwd(q, k, v, do, lse_row, delta_row, name, bq=512, carry=None):
    T = q.shape[0]
    bq = min(bq, T)
    nq = T // bq
    hb = ATT_HB
    n_carry = 0 if carry is None else 1

    i_tab, j_tab = _tri_tables(nq, by_row=False)
    n_blk = i_tab.shape[0]

    def body(it_ref, jt_ref, q_ref, k_ref, v_ref, do_ref, lse_ref, dl_ref, *rest):
        dq_ref, dk_ref, dv_ref = rest[n_carry:n_carry + 3]
        t = pl.program_id(1)
        i, j = it_ref[t], jt_ref[t]
        if carry is not None:
            exchange = functools.partial(_carried_exchange, rest[0], rest[4], rest[-2], rest[-1], carry[1])
            _carry_start(exchange, (pl.program_id(0) == 0) & (t == 0))

        @pl.when(t == 0)
        def _():
            dq_ref[...] = jnp.zeros_like(dq_ref)

        @pl.when(i == j)
        def _():
            dk_ref[...] = jnp.zeros_like(dk_ref)
            dv_ref[...] = jnp.zeros_like(dv_ref)

        def step(masked):
            rows = pl.ds(pl.multiple_of(i * bq, bq), bq)
            for hh in range(hb):
                sl = slice(hh * HP, (hh + 1) * HP)
                qh, kh, doh = q_ref[:, sl], k_ref[:, sl], do_ref[:, sl]
                st = lax.dot_general(kh, qh, _NT, preferred_element_type=F32)
                pt = jnp.exp2(st - lse_ref[hh] * LOG2E)
                if masked:
                    krow = lax.broadcasted_iota(jnp.int32, (bq, bq), 0)
                    qcol = lax.broadcasted_iota(jnp.int32, (bq, bq), 1)
                    pt = jnp.where(krow <= qcol, pt, 0.0)
                dv_ref[:, sl] += jnp.dot(pt.astype(BF16), doh, preferred_element_type=F32)
                dpt = lax.dot_general(v_ref[:, sl], doh, _NT, preferred_element_type=F32)
                dst = (pt * (dpt - dl_ref[hh])).astype(BF16)
                dk_ref[:, sl] += jnp.dot(dst, qh, preferred_element_type=F32)
                dq_ref[rows, sl] += lax.dot_general(dst, kh, _TN, preferred_element_type=F32)

        @pl.when(i > j)
        def _():
            step(False)

        @pl.when(i == j)
        def _():
            step(True)

        @pl.when(i == nq - 1)
        def _():
            dk_ref[...] *= LN2

        @pl.when(t == n_blk - 1)
        def _():
            dq_ref[...] *= ATT_SCALE

        if carry is not None:
            _carry_wait(exchange, (pl.program_id(0) == HEADS // hb - 1) & (t == n_blk - 1))

    qs = pl.BlockSpec((bq, hb * HP), lambda h, t, it, jt: (it[t], h))
    ks = pl.BlockSpec((bq, hb * HP), lambda h, t, it, jt: (jt[t], h))
    rs = pl.BlockSpec((hb, 1, bq), lambda h, t, it, jt: (h, 0, it[t]))
    full = pl.BlockSpec((T, hb * HP), lambda h, t, it, jt: (0, h))
    c_in, c_out, c_shape, c_sems = _carry_specs(carry)
    gs = pltpu.PrefetchScalarGridSpec(num_scalar_prefetch=2, grid=(HEADS // hb, n_blk),
                                      in_specs=[qs, ks, ks, qs, rs, rs] + c_in, out_specs=[full, ks, ks] + c_out,
                                      scratch_shapes=c_sems)
    return pl.pallas_call(
        body, name=name, grid_spec=gs, out_shape=[jax.ShapeDtypeStruct((T, HEADS * HP), F32)] * 3 + c_shape,
        compiler_params=pltpu.CompilerParams(dimension_semantics=("arbitrary", "arbitrary")),
    )(i_tab, j_tab, q, k, v, do, lse_row, delta_row, *([] if carry is None else [carry[0]]))


def _xattn_heads(q, k, v):
    scale = XD ** -0.5
    ps = []
    for h in range(XH):
        sl = slice(h * XD, (h + 1) * XD)
        s = lax.dot_general(q[:, sl].astype(BF16), k[:, sl].astype(BF16), (((1,), (1,)), ((), ())),
                            preferred_element_type=F32) * scale
        e = jnp.exp(s - jnp.max(s, axis=1, keepdims=True))
        ps.append(e / jnp.sum(e, axis=1, keepdims=True))
    return ps


def _xattn_fwd_fn(q, k, v):
    ps = _xattn_heads(q, k, v)
    o = [jnp.dot(p.astype(BF16), v[:, h * XD:(h + 1) * XD].astype(BF16), preferred_element_type=F32)
         for h, p in enumerate(ps)]
    return (jnp.concatenate(o, axis=1),)


def _xattn_bwd_fn(q, do, k, v):
    scale = XD ** -0.5
    ps = _xattn_heads(q, k, v)
    dqs, dks, dvs = [], [], []
    tdims = (((0,), (0,)), ((), ()))
    for h, p in enumerate(ps):
        sl = slice(h * XD, (h + 1) * XD)
        doh = do[:, sl].astype(BF16)
        dvs.append(lax.dot_general(p.astype(BF16), doh, tdims, preferred_element_type=F32))
        dp = lax.dot_general(doh, v[:, sl].astype(BF16), (((1,), (1,)), ((), ())), preferred_element_type=F32)
        ds = (p * (dp - jnp.sum(dp * p, axis=1, keepdims=True)) * scale).astype(BF16)
        dqs.append(jnp.dot(ds, k[:, sl].astype(BF16), preferred_element_type=F32))
        dks.append(lax.dot_general(ds, q[:, sl].astype(BF16), tdims, preferred_element_type=F32))
    return jnp.concatenate(dqs, axis=1), jnp.concatenate(dks, axis=1), jnp.concatenate(dvs, axis=1)


S5_SUB = 8


def _cmul_add(xr, xi, ar, ai, sr, si):
    return xr + ar * sr - ai * si, xi + ar * si + ai * sr


def _s5_tables(ar, ai, reverse):
    pows = [(ar, ai)]
    for _ in range(S5_SUB - 1):
        pr, pi = pows[-1]
        pows.append((pr * ar - pi * ai, pr * ai + pi * ar))
    cr = jnp.concatenate([p[0] for p in pows], axis=0)
    ci = jnp.concatenate([p[1] for p in pows], axis=0)
    if reverse:
        cr, ci = cr[::-1], ci[::-1]
    t = jnp.arange(S5_SUB)[:, None]
    blocks = [jnp.concatenate([cr, ci], axis=1)]
    for s in (1, 2, 4):
        keep = (t < S5_SUB - s) if reverse else (t >= s)
        sr, si = pows[s - 1]
        blocks.append(jnp.concatenate([jnp.where(keep, sr, 0.0), jnp.where(keep, si, 0.0)], axis=1))
    return jnp.concatenate(blocks, axis=0)


def _sub_scan(xr, xi, tab_ref, reverse):
    for k, s in enumerate((1, 2, 4)):
        blk = slice(S5_SUB * (k + 1), S5_SUB * (k + 2))
        sh = S5_SUB - s if reverse else s
        xr, xi = _cmul_add(xr, xi, tab_ref[blk, :S5_N], tab_ref[blk, S5_N:], pltpu.roll(xr, sh, 0), pltpu.roll(xi, sh, 0))
    return xr, xi


def _s5_fwd(z, b_st, c_st, d_skip, tab, name, tc=512):
    T = z.shape[0]
    tc = min(tc, T)
    nsub = tc // S5_SUB

    def body(u_ref, bst_ref, cst_ref, d_ref, tab_ref, h_ref, yc_ref, yg_ref, carry, x_sc):
        @pl.when(pl.program_id(0) == 0)
        def _():
            carry[...] = jnp.zeros_like(carry)

        u = u_ref[...].astype(F32)
        x_sc[...] = jnp.dot(u.astype(BF16), bst_ref[...], preferred_element_type=F32)

        def sub(b, c):
            r = pl.ds(pl.multiple_of(b * S5_SUB, S5_SUB), S5_SUB)
            xr, xi = _sub_scan(x_sc[r, :S5_N], x_sc[r, S5_N:], tab_ref, False)
            hr, hi = _cmul_add(xr, xi, tab_ref[0:S5_SUB, :S5_N], tab_ref[0:S5_SUB, S5_N:], c[0], c[1])
            x_sc[r, :S5_N] = hr
            x_sc[r, S5_N:] = hi
            return hr[S5_SUB - 1:S5_SUB], hi[S5_SUB - 1:S5_SUB]

        cr, ci = lax.fori_loop(0, nsub, sub, (carry[0:1, :S5_N], carry[0:1, S5_N:]))
        carry[0:1, :S5_N] = cr
        carry[0:1, S5_N:] = ci
        h = x_sc[...]
        h_ref[...] = h
        yc = jnp.dot(h.astype(BF16), cst_ref[...], preferred_element_type=F32)
        yc_ref[...] = yc
        yg_ref[...] = _gelu_skip(yc, u, d_ref[...]).astype(yg_ref.dtype)

    whole = lambda a: pl.BlockSpec(a.shape, lambda i: (0, 0))
    row = lambda w: pl.BlockSpec((tc, w), lambda i: (i, 0))
    return pl.pallas_call(
        body, name=name, grid=(T // tc,),
        in_specs=[pl.BlockSpec((tc, S5_W), lambda i: (i, Z_U // S5_W)), whole(b_st), whole(c_st), whole(d_skip), whole(tab)],
        out_specs=[row(2 * S5_N), row(S5_W), row(S5_W)],
        out_shape=[jax.ShapeDtypeStruct((T, 2 * S5_N), F32), jax.ShapeDtypeStruct((T, S5_W), F32),
                   jax.ShapeDtypeStruct((T, S5_W), BF16)],
        scratch_shapes=[pltpu.VMEM((8, 2 * S5_N), F32), pltpu.VMEM((tc, 2 * S5_N), F32)],
        compiler_params=pltpu.CompilerParams(dimension_semantics=("arbitrary",)),
    )(z, b_st, c_st, d_skip, tab)


def _s5_bwd(dyg, yc, z, h, b_st, c_st, d_skip, tab, name, tc=512):
    T = z.shape[0]
    tc = min(tc, T)
    nc, nsub = T // tc, tc // S5_SUB

    def body(dyg_ref, yc_ref, u_ref, h_ref, hp_ref, bst_ref, cst_ref, d_ref, tab_ref,
             du_ref, da_ref, db_ref, dct_ref, dd_ref, carry, x_sc):
        g = pl.program_id(0)

        @pl.when(g == 0)
        def _():
            carry[...] = jnp.zeros_like(carry)
            for r in (da_ref, db_ref, dct_ref, dd_ref):
                r[...] = jnp.zeros_like(r)

        u = u_ref[...].astype(F32)
        _, vjp = jax.vjp(_gelu_skip, yc_ref[...], u, d_ref[...])
        dyc, du_skip, dd = vjp(dyg_ref[...])
        dd_ref[...] += dd
        dyc16 = dyc.astype(BF16)
        dct_ref[...] += lax.dot_general(dyc16, h_ref[...].astype(BF16), _TN, preferred_element_type=F32)
        x_sc[...] = lax.dot_general(dyc16, cst_ref[...], _NT, preferred_element_type=F32)
        first = jnp.where(g == nc - 1, 0.0, 1.0)
        row0 = lax.broadcasted_iota(jnp.int32, (S5_SUB, S5_N), 0) == 0

        def sub(k, c):
            cr, ci, acc_r, acc_i = c
            b = nsub - 1 - k
            r = pl.ds(pl.multiple_of(b * S5_SUB, S5_SUB), S5_SUB)
            xr, xi = _sub_scan(x_sc[r, :S5_N], x_sc[r, S5_N:], tab_ref, True)
            lr, li = _cmul_add(xr, xi, tab_ref[0:S5_SUB, :S5_N], tab_ref[0:S5_SUB, S5_N:], cr, ci)
            x_sc[r, :S5_N] = lr
            x_sc[r, S5_N:] = li
            rp = pl.ds(pl.multiple_of(jnp.maximum(b - 1, 0) * S5_SUB, S5_SUB), S5_SUB)
            last_r = jnp.where(b == 0, hp_ref[S5_SUB - 1:S5_SUB, :S5_N] * first, h_ref[rp, :S5_N][S5_SUB - 1:S5_SUB])
            last_i = jnp.where(b == 0, hp_ref[S5_SUB - 1:S5_SUB, S5_N:] * first, h_ref[rp, S5_N:][S5_SUB - 1:S5_SUB])
            pr = jnp.where(row0, last_r, pltpu.roll(h_ref[r, :S5_N], 1, 0))
            pi = jnp.where(row0, last_i, pltpu.roll(h_ref[r, S5_N:], 1, 0))
            return lr[0:1], li[0:1], acc_r + (lr * pr + li * pi), acc_i + (li * pr - lr * pi)

        zero = jnp.zeros((S5_SUB, S5_N), F32)
        cr, ci, acc_r, acc_i = lax.fori_loop(0, nsub, sub, (carry[0:1, :S5_N], carry[0:1, S5_N:], zero, zero))
        carry[0:1, :S5_N] = cr
        carry[0:1, S5_N:] = ci
        da_ref[0:1, :S5_N] += jnp.sum(acc_r, axis=0, keepdims=True)
        da_ref[0:1, S5_N:] += jnp.sum(acc_i, axis=0, keepdims=True)
        lam16 = x_sc[...].astype(BF16)
        db_ref[...] += lax.dot_general(u.astype(BF16), lam16, _TN, preferred_element_type=F32)
        du = lax.dot_general(lam16, bst_ref[...], _NT, preferred_element_type=F32) + du_skip
        du_ref[...] = du.astype(du_ref.dtype)

    whole = lambda a: pl.BlockSpec(a.shape, lambda g: (0, 0))
    row = lambda w, off=0: pl.BlockSpec((tc, w), lambda g: (nc - 1 - g, off))
    prev = pl.BlockSpec((S5_SUB, 2 * S5_N), lambda g: (jnp.maximum((nc - 1 - g) * nsub - 1, 0), 0))
    acc = lambda s: pl.BlockSpec(s, lambda g: (0, 0))
    return pl.pallas_call(
        body, name=name, grid=(nc,),
        in_specs=[row(S5_W), row(S5_W), row(S5_W, Z_U // S5_W), row(2 * S5_N), prev, whole(b_st), whole(c_st),
                  whole(d_skip), whole(tab)],
        out_specs=[row(S5_W), acc((1, 2 * S5_N)), acc((S5_W, 2 * S5_N)), acc((S5_W, 2 * S5_N)), acc((1, S5_W))],
        out_shape=[jax.ShapeDtypeStruct((T, S5_W), BF16), jax.ShapeDtypeStruct((1, 2 * S5_N), F32),
                   jax.ShapeDtypeStruct((S5_W, 2 * S5_N), F32), jax.ShapeDtypeStruct((S5_W, 2 * S5_N), F32),
                   jax.ShapeDtypeStruct((1, S5_W), F32)],
        scratch_shapes=[pltpu.VMEM((8, 2 * S5_N), F32), pltpu.VMEM((tc, 2 * S5_N), F32)],
        compiler_params=pltpu.CompilerParams(dimension_semantics=("arbitrary",)),
    )(dyg, yc, z, h, h, b_st, c_st, d_skip, tab)


def _s5_discretize(lam_re, lam_im, log_dt, b_re, b_im):
    lr = jnp.minimum(lam_re, S5_MAX_RE)
    li = lam_im
    dt = jnp.exp(log_dt)[:, None]
    mag = jnp.exp(lr * dt)
    ang = li * dt
    ab_re = mag * jnp.cos(ang)
    ab_im = mag * jnp.sin(ang)
    den = lr * lr + li * li
    nr = ab_re - 1.0
    f_re = ((nr * lr + ab_im * li) / den)[..., None]
    f_im = ((ab_im * lr - nr * li) / den)[..., None]
    return ab_re, ab_im, f_re * b_re - f_im * b_im, f_re * b_im + f_im * b_re


def _adamw_fn(w, g, m, v):
    m = ADAM_B1 * m + (1.0 - ADAM_B1) * g
    v = ADAM_B2 * v + (1.0 - ADAM_B2) * (g * g)
    m_hat = m / (1.0 - ADAM_B1 ** ADAM_STEP)
    v_hat = v / (1.0 - ADAM_B2 ** ADAM_STEP)
    delta = -ADAM_LR * (m_hat / (jnp.sqrt(v_hat) + ADAM_EPS) + ADAM_WD * w)
    return delta, m, v


def _adamw(w, g, m, v, name):
    wd = w.shape[1]
    return _rowwise(_adamw_fn, name, [_whole(w), _whole(g), _whole(m), _whole(v)], [], [(wd, F32)] * 3,
                    tm=_row_tile(w.shape[0]))


def _adamw_many(ws, gs, ms, vs, name):
    n = len(ws)

    def body(*refs):
        for i in range(n):
            d_, m_, v_ = _adamw_fn(*[refs[kk * n + i][...] for kk in range(4)])
            refs[4 * n + i][...] = d_
            refs[5 * n + i][...] = m_
            refs[6 * n + i][...] = v_

    out = pl.pallas_call(body, name=name, out_shape=[jax.ShapeDtypeStruct(w.shape, F32) for w in ws] * 3)(*ws, *gs, *ms, *vs)
    return out[:n], out[n:2 * n], out[2 * n:]


def _me():
    return lax.axis_index("x"), lax.axis_index("y"), lax.axis_index("c")


def _chip_of(j):
    return j // 2, j % 2


_HBM = pl.BlockSpec(memory_space=pl.ANY)


def _gather_weights(shard):
    _, R, W = shard.shape

    def body(s_ref, out_ref, send_sems, recv_sems):
        x, y, c = _me()
        mine = 2 * x + y
        sib = (x, y, 1 - c)
        chips = [(1 - x, y), (x, 1 - y), (1 - x, 1 - y)]

        def rcopy(kk, src, chip_idx, half, to):
            return pltpu.make_async_remote_copy(src_ref=src, dst_ref=out_ref.at[chip_idx, half],
                                                send_sem=send_sems.at[kk], recv_sem=recv_sems.at[kk],
                                                device_id=to, device_id_type=MESH)

        first = [rcopy(j, s_ref.at[c], mine, c, (*chip, c)) for j, chip in enumerate(chips)]
        for cp in first:
            cp.start()
        passed = []
        for j, (px, py) in enumerate(chips):
            src_chip = 2 * px + py
            rcopy(j, s_ref.at[c], src_chip, c, (x, y, c)).wait_recv()
            fwd = rcopy(3 + j, out_ref.at[src_chip, c], src_chip, c, sib)
            fwd.start()
            passed.append(fwd)
        for j, (px, py) in enumerate(chips):
            rcopy(3 + j, s_ref.at[c], 2 * px + py, 1 - c, (x, y, c)).wait_recv()
        for cp in first + passed:
            cp.wait_send()

    return pl.pallas_call(
        body, name="gather_weights", in_specs=[_HBM], out_specs=_HBM,
        out_shape=jax.ShapeDtypeStruct((4, 2, R, W), shard.dtype),
        scratch_shapes=[pltpu.SemaphoreType.DMA((6,)), pltpu.SemaphoreType.DMA((6,))],
    )(shard)


def _pair_exchange(g, name):
    _, _, R, W = g.shape

    def body(g_ref, out_ref, send_sem, recv_sem):
        x, y, c = _me()
        cp = pltpu.make_async_remote_copy(src_ref=g_ref.at[:, 1 - c], dst_ref=out_ref, send_sem=send_sem,
                                          recv_sem=recv_sem, device_id=(x, y, 1 - c), device_id_type=MESH)
        cp.start()
        cp.wait()

    return pl.pallas_call(
        body, name=name, in_specs=[_HBM], out_specs=_HBM,
        out_shape=jax.ShapeDtypeStruct((4, R, W), g.dtype),
        scratch_shapes=[pltpu.SemaphoreType.DMA(()), pltpu.SemaphoreType.DMA(())],
    )(g)


def _pair_add(g, recv, cidx, name):
    _, _, R, W = g.shape
    tr = _row_tile(R)

    def body(c_ref, a_ref, b_ref, o_ref):
        o_ref[...] = (a_ref[...] + b_ref[...]).astype(o_ref.dtype)

    gs = pltpu.PrefetchScalarGridSpec(
        num_scalar_prefetch=1, grid=(4, R // tr),
        in_specs=[pl.BlockSpec((None, None, tr, W), lambda j, r, c: (j, c[0], r, 0)),
                  pl.BlockSpec((None, tr, W), lambda j, r, c: (j, r, 0))],
        out_specs=pl.BlockSpec((None, tr, W), lambda j, r, c: (j, r, 0)))
    return pl.pallas_call(body, name=name, grid_spec=gs,
                          out_shape=jax.ShapeDtypeStruct((4, R, W), BF16))(cidx, g, recv)


def _carried_exchange(src_ref, dst_ref, send_sems, recv_sems, spread, incoming):
    x, y, c = _me()
    mine = 2 * x + y
    copies = []
    for j, (px, py) in enumerate([(1 - x, y), (x, 1 - y), (1 - x, 1 - y)]):
        src = src_ref.at[2 * px + py] if spread else src_ref.at[c]
        slot = 2 * px + py if incoming else mine
        copies.append(pltpu.make_async_remote_copy(src_ref=src, dst_ref=dst_ref.at[slot], send_sem=send_sems.at[j],
                                                   recv_sem=recv_sems.at[j], device_id=(px, py, c), device_id_type=MESH))
    return copies


def _chip_exchange(p, name):
    _, R, W = p.shape

    def body(p_ref, out_ref, send_sems, recv_sems):
        outs = _carried_exchange(p_ref, out_ref, send_sems, recv_sems, True, False)
        for cp in outs:
            cp.start()
        for cp in _carried_exchange(p_ref, out_ref, send_sems, recv_sems, True, True):
            cp.wait_recv()
        for cp in outs:
            cp.wait_send()

    return pl.pallas_call(
        body, name=name, in_specs=[_HBM], out_specs=_HBM,
        out_shape=jax.ShapeDtypeStruct((4, R, W), p.dtype),
        scratch_shapes=[pltpu.SemaphoreType.DMA((3,)), pltpu.SemaphoreType.DMA((3,))],
    )(p)


def _chip_sum(q, name):
    _, R, W = q.shape
    tr = _row_tile(R)

    def body(q_ref, o_ref):
        f = lambda t: q_ref[t].astype(F32)
        o_ref[...] = ((f(0) + f(1)) + f(2)) + f(3)

    return pl.pallas_call(body, name=name, grid=(R // tr,),
                          in_specs=[pl.BlockSpec((4, tr, W), lambda r: (0, r, 0))],
                          out_specs=pl.BlockSpec((tr, W), lambda r: (r, 0)),
                          out_shape=jax.ShapeDtypeStruct((R, W), F32))(q)


def _pair_share(r, name):
    R, W = r.shape

    def body(r_ref, out_ref, send_sem, recv_sem):
        x, y, c = _me()
        cp = pltpu.make_async_remote_copy(src_ref=r_ref, dst_ref=out_ref, send_sem=send_sem, recv_sem=recv_sem,
                                          device_id=(x, y, 1 - c), device_id_type=MESH)
        cp.start()
        cp.wait()

    return pl.pallas_call(
        body, name=name, in_specs=[_HBM], out_specs=_HBM,
        out_shape=jax.ShapeDtypeStruct((R, W), r.dtype),
        scratch_shapes=[pltpu.SemaphoreType.DMA(()), pltpu.SemaphoreType.DMA(())],
    )(r)


def _allreduce_small(vec):
    R, W = vec.shape

    def body(v_ref, out_ref, buf, send_sems, recv_sems):
        x, y, c = _me()
        me = 4 * x + 2 * y + c
        buf[me] = v_ref[...]
        cps = []
        for kk in range(1, 8):
            peer = (x ^ (kk >> 2), y ^ ((kk >> 1) & 1), c ^ (kk & 1))
            cp = pltpu.make_async_remote_copy(src_ref=v_ref, dst_ref=buf.at[me], send_sem=send_sems.at[kk - 1],
                                              recv_sem=recv_sems.at[kk - 1], device_id=peer, device_id_type=MESH)
            cp.start()
            cps.append(cp)
        for kk in range(1, 8):
            peer = (x ^ (kk >> 2), y ^ ((kk >> 1) & 1), c ^ (kk & 1))
            pltpu.make_async_remote_copy(src_ref=v_ref, dst_ref=buf.at[me ^ kk], send_sem=send_sems.at[kk - 1],
                                         recv_sem=recv_sems.at[kk - 1], device_id=peer, device_id_type=MESH).wait_recv()
        for cp in cps:
            cp.wait_send()
        acc = buf[0]
        for d in range(1, 8):
            acc = acc + buf[d]
        out_ref[...] = acc

    vm = pl.BlockSpec(memory_space=pltpu.VMEM)
    return pl.pallas_call(
        body, name="allreduce_small", in_specs=[vm], out_specs=vm, out_shape=jax.ShapeDtypeStruct((R, W), F32),
        scratch_shapes=[pltpu.VMEM((8, R, W), F32), pltpu.SemaphoreType.DMA((7,)), pltpu.SemaphoreType.DMA((7,))],
    )(vec)


def _pack_flat(parts, align, dtype):
    flat = jnp.concatenate([p.reshape(-1).astype(dtype) for p in parts])
    n = flat.shape[0]
    pad = (-n) % align
    return jnp.pad(flat, (0, pad)) if pad else flat


def _unpack_flat(flat, shapes):
    out, off = [], 0
    for s in shapes:
        n = math.prod(s)
        out.append(flat[off:off + n].reshape(s))
        off += n
    return out


def _full_from_shards(name, sh):
    if name in COL_SHARDED:
        return jnp.transpose(sh, (1, 0, 2)).reshape(sh.shape[1], 4 * sh.shape[2])
    return sh.reshape(4 * sh.shape[1], sh.shape[2])


def _shards_from_full(name, full):
    if name in COL_SHARDED:
        r, cc = full.shape[0], full.shape[1] // 4
        return jnp.transpose(full.reshape(r, 4, cc), (1, 0, 2)).reshape(4, r * cc)
    return full.reshape(4, -1)


def _pad_heads(w, width):
    k = w.shape[0]
    return jnp.pad(w.reshape(k, HEADS, width), ((0, 0), (0, 0), (0, HP - width))).reshape(k, HEADS * HP)


def _unpad_heads(w, width):
    k = w.shape[0]
    return w.reshape(k, HEADS, HP)[:, :, :width].reshape(k, HEADS * width)


def _blockdiag(t):
    g, a, b = t.shape
    return jnp.einsum("gab,gk->gakb", t, jnp.eye(g, dtype=t.dtype)).reshape(g * a, g * b)


def _blockdiag_t(m, a, b):
    g = m.shape[0] // a
    return jnp.einsum("gagb->gab", m.reshape(g, a, g, b))


def _local_step(x, mem, positions, target, W, sp, overlap=None):
    T = x.shape[0]
    row = lambda v: v.reshape(1, -1)

    ab_re, ab_im, bb_re, bb_im = _s5_discretize(sp["s5_lam_re"], sp["s5_lam_im"], sp["s5_log_dt"], sp["s5_b_re"], sp["s5_b_im"])
    ar, ai = ab_re.reshape(1, S5_N), ab_im.reshape(1, S5_N)
    tab_fwd, tab_rev = _s5_tables(ar, ai, False), _s5_tables(ar, -ai, True)
    b_st = jnp.concatenate([_blockdiag(jnp.swapaxes(bb_re, 1, 2)), _blockdiag(jnp.swapaxes(bb_im, 1, 2))],
                           axis=1).astype(BF16)
    c_st = jnp.concatenate([_blockdiag(jnp.swapaxes(sp["s5_c_re"], 1, 2)),
                            -_blockdiag(jnp.swapaxes(sp["s5_c_im"], 1, 2))], axis=0).astype(BF16)
    inv = ROPE_THETA ** (-jnp.arange(0, ROPE, 2, dtype=F32) / ROPE)
    ang = positions.astype(F32)[:, None] * inv
    cs, sn = jnp.cos(ang), jnp.sin(ang)
    one, zero = jnp.ones((T, NOPE), F32), jnp.zeros((T, NOPE), F32)
    z16, tail1, tail0 = jnp.zeros((T, ROPE // 2), F32), jnp.ones((T, HP - QK), F32), jnp.zeros((T, HP - QK), F32)
    t_cos = jnp.concatenate([one, cs, cs, tail1], axis=1)
    t_s1 = jnp.concatenate([zero, z16, sn, tail0], axis=1)
    t_s2 = jnp.concatenate([zero, -sn, z16, tail0], axis=1)

    w_in = W["w_in"]
    w_in_re = jnp.concatenate([w_in[:, 800:], w_in[:, :768], w_in[:, 768:800],
                               jnp.zeros((D_MODEL, HP - ROPE), w_in.dtype)], axis=1)
    w_uq_p = _pad_heads(W["w_uq"], QK)
    wkv = W["w_ukv"].reshape(S5_W, HEADS, NOPE + VD)
    w_uk_p = _pad_heads(wkv[:, :, :NOPE].reshape(S5_W, HEADS * NOPE), NOPE)
    w_uv_p = _pad_heads(wkv[:, :, NOPE:].reshape(S5_W, HEADS * VD), VD)
    w_oa_p = jnp.pad(W["w_oa"].reshape(HEADS, VD, D_MODEL), ((0, 0), (0, HP - VD), (0, 0))).reshape(HEADS * HP, D_MODEL)

    g_in, b_in = row(sp["ln_in_g"]), row(sp["ln_in_b"])
    g1, b1, g2, b2, g3, b3 = (row(sp[k]) for k in ("ln1_g", "ln1_b", "ln2_g", "ln2_b", "ln3_g", "ln3_b"))
    gq, gkv, d_skip = row(sp["q_norm_g"]), row(sp["kv_norm_g"]), row(sp["s5_d"])

    (h0,) = _rowwise(lambda a, g, b: (_ln(a, g, b),), "ln_in_fwd", [_whole(x)], [g_in, b_in], [(D_MODEL, F32)], tm=512)
    z = _mm(h0, w_in_re, "in_proj", tm=512, tn=Z_W, out_dtype=BF16)
    hs, yc, yg = _s5_fwd(z, b_st, c_st, d_skip, tab_fwd, "s5_fwd")
    y2 = _mm(yg, W["w_glu"], "glu_proj", tn=2048, out_dtype=BF16)
    cqn, ckvn = _rowwise(lambda a, b, ga, gb: (_rms(a, ga), _rms(b, gb)), "mla_norm_fwd",
                         [(z, Z_CQ, S5_W), (z, Z_CKV, S5_W)], [gq, gkv], [(S5_W, BF16)] * 2)
    q_raw = _mm(cqn, w_uq_p, "mla_uq")
    k_raw = _mm(ckvn, w_uk_p, "mla_uk")
    v = _mm(ckvn, w_uv_p, "mla_uv", out_dtype=BF16)

    def rope_fwd(qr, kr_raw, krz, tc_, t1, t2):
        q = _rope(qr, _tile8(tc_), _tile8(t1), _tile8(t2)) * (ATT_SCALE * LOG2E)
        kr = _rope(pltpu.roll(krz, NOPE, 1), tc_, t1, t2)
        return q, kr_raw + _tile8(kr)

    q, k = _rowwise(rope_fwd, "mla_rope_fwd", [_whole(q_raw), _whole(k_raw), (z, Z_KR, HP), _whole(t_cos), _whole(t_s1), _whole(t_s2)],
                    [], [(HEADS * HP, BF16)] * 2)
    if overlap is None:
        o, lse = _flash_fwd(q, k, v, "mla_attn_fwd")
    else:
        o, lse, landed = _flash_fwd(q, k, v, "mla_attn_fwd", carry=(overlap[0], False))
        W = {**W, **overlap[1](landed)}
    a_out = _mm(o, w_oa_p, "mla_oa", out_dtype=BF16)
    (mixin,) = _rowwise(lambda *a: (_gate_mix(*a),), "gate_mix_fwd",
                        [(z, Z_GS, D_MODEL), (z, Z_GA, D_MODEL), (y2, 0, D_MODEL), (y2, D_MODEL, D_MODEL), _whole(a_out)],
                        [], [(D_MODEL, BF16)])
    mix = _mm(mixin, W["w_o"], "mix_o")
    (h1,) = _rowwise(lambda h, r, g, b: (_ln_res(h, r, g, b),), "ln1_fwd", [_whole(h0), _whole(mix)], [g1, b1], [(D_MODEL, F32)], tm=512)
    xq = _mm(h1, W["w_xq"], "x_q", out_dtype=BF16)
    xk = _mm(mem, W["w_xk"], "x_k", out_dtype=BF16)
    xv = _mm(mem, W["w_xv"], "x_v", out_dtype=BF16)
    (xo,) = _rowwise(_xattn_fwd_fn, "xattn_fwd", [_whole(xq)], [xk, xv], [(D_MODEL, BF16)], tm=1024, upcast=False)
    xa = _mm(xo, W["w_xo"], "x_o")
    (h2,) = _rowwise(lambda h, r, g, b: (_ln_res(h, r, g, b),), "ln2_fwd", [_whole(h1), _whole(xa)], [g2, b2], [(D_MODEL, F32)], tm=512)
    a_up = _mm(h2, W["w_up"], "mlp_up", out_dtype=BF16)
    ff = _mm(a_up, W["w_down"], "mlp_down", act=_relu2)

    def loss_fn(h, r, tgt, g, b):
        def f(r_, g_, b_):
            e = _ln_res(h, r_, g_, b_) - tgt
            return 0.5 * jnp.sum(jnp.mean(e * e, axis=-1))
        lv, (dr_, dg_, db_) = jax.value_and_grad(f, argnums=(0, 1, 2))(r, g, b)
        return dr_, dg_, db_, jnp.broadcast_to(lv, (1, 128))

    dr3, dg3, db3, lossv = _rowwise(loss_fn, "ln3_loss_bwd", [_whole(h2), _whole(ff), _whole(target)], [g3, b3],
                                    [(D_MODEL, F32)], [(1, D_MODEL), (1, D_MODEL), (1, 128)], tm=512)
    gW, gs = {}, {"ln3_g": dg3, "ln3_b": db3}

    da = _mm(dr3, W["w_down"], "mlp_down_bwd_a", "nt", epi=lambda acc, a: acc * (2.0 * jnp.maximum(a, 0.0)), extras=(a_up,),
             out_dtype=BF16)
    gW["w_down"] = _mm(a_up, dr3, "mlp_down_bwd_w", "tn", m=4 * D_MODEL, n=D_MODEL, act=_relu2)
    gW["w_up"] = _mm(h2, da, "mlp_up_bwd_w", "tn", m=D_MODEL, n=4 * D_MODEL)
    dh2 = _mm(da, W["w_up"], "mlp_up_bwd_a", "nt", epi=lambda acc, e: acc + DN_ALPHA * e, extras=(dr3,))

    def ln_res_bwd(h, r, dy, g, b):
        _, vjp = jax.vjp(_ln_res, h, r, g, b)
        _, dr_, dg_, db_ = vjp(dy)
        return dr_, dg_, db_

    dr2, gs["ln2_g"], gs["ln2_b"] = _rowwise(ln_res_bwd, "ln2_bwd", [_whole(h1), _whole(xa), _whole(dh2)], [g2, b2],
                                             [(D_MODEL, F32)], [(1, D_MODEL)] * 2, tm=512)
    dxo = _mm(dr2, W["w_xo"], "x_o_bwd_a", "nt", out_dtype=BF16)
    gW["w_xo"] = _mm(xo, dr2, "x_o_bwd_w", "tn", m=D_MODEL, n=D_MODEL)
    dxq, dxk, dxv = _rowwise(_xattn_bwd_fn, "xattn_bwd", [_whole(xq), _whole(dxo)], [xk, xv], [(D_MODEL, BF16)],
                             [(xk.shape[0], D_MODEL)] * 2, tm=1024, upcast=False)
    gW["w_xq"] = _mm(h1, dxq, "x_q_bwd_w", "tn", m=D_MODEL, n=D_MODEL)
    gW["w_xk"] = _mm(mem, dxk, "x_k_bwd_w", "tn", m=D_MODEL, n=D_MODEL)
    gW["w_xv"] = _mm(mem, dxv, "x_v_bwd_w", "tn", m=D_MODEL, n=D_MODEL)
    dh1 = _mm(dxq, W["w_xq"], "x_q_bwd_a", "nt", epi=lambda acc, e: acc + DN_ALPHA * e, extras=(dr2,))

    dr1, gs["ln1_g"], gs["ln1_b"] = _rowwise(ln_res_bwd, "ln1_bwd", [_whole(h0), _whole(mix), _whole(dh1)], [g1, b1],
                                             [(D_MODEL, F32)], [(1, D_MODEL)] * 2, tm=512)
    dmixin = _mm(dr1, W["w_o"], "mix_o_bwd_a", "nt")
    gW["w_o"] = _mm(mixin, dr1, "mix_o_bwd_w", "tn", m=D_MODEL, n=D_MODEL)

    def gate_mix_bwd(gs_, ga_, ya, yb, ao, dy):
        _, vjp = jax.vjp(_gate_mix, gs_, ga_, ya, yb, ao)
        return vjp(dy)

    dgs, dga, dy2a, dy2b, da_out = _rowwise(
        gate_mix_bwd, "gate_mix_bwd",
        [(z, Z_GS, D_MODEL), (z, Z_GA, D_MODEL), (y2, 0, D_MODEL), (y2, D_MODEL, D_MODEL), _whole(a_out), _whole(dmixin)],
        [], [(D_MODEL, BF16)] * 5)

    do = _mm(da_out, w_oa_p, "mla_oa_bwd_a", "nt")
    g_oa_p = _mm(o, da_out, "mla_oa_bwd_w", "tn", m=HEADS * HP, n=D_MODEL)
    gW["w_oa"] = g_oa_p.reshape(HEADS, HP, D_MODEL)[:, :VD].reshape(HEADS * VD, D_MODEL)
    delta, do16 = _rowwise(_attn_delta_fn, "mla_attn_delta", [_whole(do), _whole(o)], [], [(HEADS, F32), (HEADS * HP, BF16)])
    delta_row = delta.T.reshape(HEADS, 1, T)
    if overlap is None:
        dq, dk, dv = _flash_bwd(q, k, v, do16, lse, delta_row, "mla_attn_bwd")
        riding = None
    else:
        sent = overlap[2]({n: gW[n] for n in BIG_LATE})
        dq, dk, dv, landed = _flash_bwd(q, k, v, do16, lse, delta_row, "mla_attn_bwd", carry=(sent, True))
        riding = (sent, landed)

    def rope_bwd(dq_, dk_, tc_, t1, t2):
        dqr = _rope_t(dq_, _tile8(tc_), _tile8(t1), _tile8(t2))
        dkr = dk_[:, 0:HP]
        for hh in range(1, HEADS):
            dkr = dkr + dk_[:, hh * HP:(hh + 1) * HP]
        dkr = _rope_t(jnp.where(_rope_lanes(dkr.shape), dkr, 0.0), tc_, t1, t2)
        dkr = pltpu.roll(dkr, NOPE, 1)
        lane = lax.broadcasted_iota(jnp.int32, dkr.shape, 1)
        return dqr, jnp.where(lane < ROPE, dkr, 0.0)

    dq_raw, dkr = _rowwise(rope_bwd, "mla_rope_bwd", [_whole(dq), _whole(dk), _whole(t_cos), _whole(t_s1), _whole(t_s2)], [],
                           [(HEADS * HP, BF16), (HP, BF16)])
    dcqn = _mm(dq_raw, w_uq_p, "mla_uq_bwd_a", "nt")
    gW["w_uq"] = _unpad_heads(_mm(cqn, dq_raw, "mla_uq_bwd_w", "tn", m=S5_W, n=HEADS * HP), QK)
    dckvn_k = _mm(dk, w_uk_p, "mla_uk_bwd_a", "nt")
    dckvn = _mm(dv, w_uv_p, "mla_uv_bwd_a", "nt", epi=lambda acc, e: acc + e, extras=(dckvn_k,))
    g_uk = _unpad_heads(_mm(ckvn, dk, "mla_uk_bwd_w", "tn", m=S5_W, n=HEADS * HP), NOPE).reshape(S5_W, HEADS, NOPE)
    g_uv = _unpad_heads(_mm(ckvn, dv, "mla_uv_bwd_w", "tn", m=S5_W, n=HEADS * HP), VD).reshape(S5_W, HEADS, VD)
    gW["w_ukv"] = jnp.concatenate([g_uk, g_uv], axis=2).reshape(S5_W, HEADS * (NOPE + VD))

    def norm_bwd(a, b, da_, db_, ga, gb):
        _, vja = jax.vjp(_rms, a, ga)
        _, vjb = jax.vjp(_rms, b, gb)
        dxa, dga_ = vja(da_)
        dxb, dgb_ = vjb(db_)
        return dxa, dxb, dga_, dgb_

    dcq, dckv, gs["q_norm_g"], gs["kv_norm_g"] = _rowwise(
        norm_bwd, "mla_norm_bwd", [(z, Z_CQ, S5_W), (z, Z_CKV, S5_W), _whole(dcqn), _whole(dckvn)], [gq, gkv],
        [(S5_W, BF16)] * 2, [(1, S5_W)] * 2)

    dy2 = jnp.concatenate([dy2a, dy2b], axis=1)
    dyg = _mm(dy2, W["w_glu"], "glu_bwd_a", "nt")
    gW["w_glu"] = _mm(yg, dy2, "glu_bwd_w", "tn", m=S5_W, n=2 * D_MODEL)

    du, gs["s5_ab"], g_bst, g_cst_t, gs["s5_d"] = _s5_bwd(dyg, yc, z, hs, b_st, c_st, d_skip, tab_rev, "s5_bwd")
    gs["s5_bb_re"] = jnp.swapaxes(_blockdiag_t(g_bst[:, :S5_N], S5_H, S5_P), 1, 2)
    gs["s5_bb_im"] = jnp.swapaxes(_blockdiag_t(g_bst[:, S5_N:], S5_H, S5_P), 1, 2)
    gs["s5_c_re"] = _blockdiag_t(g_cst_t[:, :S5_N], S5_H, S5_P)
    gs["s5_c_im"] = -_blockdiag_t(g_cst_t[:, S5_N:], S5_H, S5_P)

    dz = jnp.concatenate([dgs, dga, du, dcq, dckv, dkr], axis=1)
    g_in_re = _mm(h0, dz, "in_proj_bwd_w", "tn", m=D_MODEL, n=Z_W, tm=512, tn=Z_W)
    gW["w_in"] = jnp.concatenate([g_in_re[:, Z_U:Z_KR], g_in_re[:, Z_KR:Z_KR + ROPE], g_in_re[:, :Z_U]], axis=1)
    dh0 = _mm(dz, w_in_re, "in_proj_bwd_a", "nt", tm=512, tk=Z_W, epi=lambda acc, e: acc + DN_ALPHA * e, extras=(dr1,))

    def ln_bwd(a, dy, g, b):
        _, vjp = jax.vjp(_ln, a, g, b)
        return vjp(dy)

    dx, gs["ln_in_g"], gs["ln_in_b"] = _rowwise(ln_bwd, "ln_in_bwd", [_whole(x), _whole(dh0)], [g_in, b_in],
                                                [(D_MODEL, F32)], [(1, D_MODEL)] * 2, tm=512)
    return lossv[:, :1], dx, gW, gs, riding


_RAW_SMALL = (("loss", (1, 1)), ("ln_in_g", (1, D_MODEL)), ("ln_in_b", (1, D_MODEL)), ("ln1_g", (1, D_MODEL)),
              ("ln1_b", (1, D_MODEL)), ("ln2_g", (1, D_MODEL)), ("ln2_b", (1, D_MODEL)), ("ln3_g", (1, D_MODEL)),
              ("ln3_b", (1, D_MODEL)), ("q_norm_g", (1, S5_W)), ("kv_norm_g", (1, S5_W)), ("s5_d", (1, S5_W)),
              ("s5_ab", (1, 2 * S5_N)), ("s5_bb_re", (S5_G, S5_P, S5_H)), ("s5_bb_im", (S5_G, S5_P, S5_H)),
              ("s5_c_re", (S5_G, S5_H, S5_P)), ("s5_c_im", (S5_G, S5_H, S5_P)))


def kernel(x, mem, positions, ln_in_g, ln_in_b, w_in, s5_lam_re, s5_lam_im, s5_log_dt, s5_b_re, s5_b_im, s5_c_re, s5_c_im, s5_d, w_glu, q_norm_g, w_uq, kv_norm_g, w_ukv, w_oa, w_o, ln1_g, ln1_b, w_xq, w_xk, w_xv, w_xo, ln2_g, ln2_b, w_up, w_down, ln3_g, ln3_b, loss_target, m_ln_in_g, m_ln_in_b, m_w_in, m_s5_lam_re, m_s5_lam_im, m_s5_log_dt, m_s5_b_re, m_s5_b_im, m_s5_c_re, m_s5_c_im, m_s5_d, m_w_glu, m_q_norm_g, m_w_uq, m_kv_norm_g, m_w_ukv, m_w_oa, m_w_o, m_ln1_g, m_ln1_b, m_w_xq, m_w_xk, m_w_xv, m_w_xo, m_ln2_g, m_ln2_b, m_w_up, m_w_down, m_ln3_g, m_ln3_b, v_ln_in_g, v_ln_in_b, v_w_in, v_s5_lam_re, v_s5_lam_im, v_s5_log_dt, v_s5_b_re, v_s5_b_im, v_s5_c_re, v_s5_c_im, v_s5_d, v_w_glu, v_q_norm_g, v_w_uq, v_kv_norm_g, v_w_ukv, v_w_oa, v_w_o, v_ln1_g, v_ln1_b, v_w_xq, v_w_xk, v_w_xv, v_w_xo, v_ln2_g, v_ln2_b, v_w_up, v_w_down, v_ln3_g, v_ln3_b):
    a = dict(locals())
    wts = {n: a[n] for n in WEIGHTS}
    ms = {n: a["m_" + n] for n in WEIGHTS}
    vs = {n: a["v_" + n] for n in WEIGHTS}
    shard2d = {n: wts[n].reshape(wts[n].shape[-2], wts[n].shape[-1]) for n in BIG}
    nrow = {n: shard2d[n].size // 1024 for n in BIG}
    mine = 2 * lax.axis_index("x") + lax.axis_index("y")
    my_c = lax.axis_index("c")
    cidx = my_c.astype(jnp.int32).reshape(1)

    def group_rows(group):
        rows = sum(nrow[n] for n in group)
        return rows, (-rows) % PACK_ROWS

    def pack_shards(group):
        rows, pad = group_rows(group)
        parts = [shard2d[n].astype(BF16).reshape(nrow[n], 1024) for n in group] + [jnp.zeros((pad, 1024), BF16)]
        return jnp.concatenate(parts, axis=0).reshape(2, (rows + pad) // 2, 1024)

    def unpack_full(group, gathered):
        out, off = {}, 0
        for n in group:
            r, cc = shard2d[n].shape
            piece = lax.optimization_barrier(gathered[:, off:off + nrow[n]])
            out[n] = _full_from_shards(n, piece.reshape(4, r, cc))
            off += nrow[n]
        return out

    def pack_grads(group, gw):
        rows, pad = group_rows(group)
        parts = [_shards_from_full(n, gw[n]).reshape(4, nrow[n], 1024) for n in group] + [jnp.zeros((4, pad, 1024), F32)]
        return jnp.concatenate(parts, axis=1).reshape(4, 2, (rows + pad) // 2, 1024)

    def by_core(own, other):
        is_mine = (jnp.arange(2) == my_c).reshape((2,) + (1,) * own.ndim)
        return jnp.where(is_mine, own[None], other[None])

    packed_e, packed_l = pack_shards(BIG_EARLY), pack_shards(BIG_LATE)
    gathered = lax.dynamic_update_slice(_gather_weights(packed_e), packed_e[None], (mine, 0, 0, 0))
    W = unpack_full(BIG_EARLY, gathered.reshape(4, -1, 1024))

    def finish_gather(landed):
        half = packed_l.shape[1]
        own_half = lax.dynamic_index_in_dim(packed_l, my_c, axis=0, keepdims=True)
        landed = lax.dynamic_update_slice(landed, own_half, (mine, 0, 0))
        other = _pair_share(landed.reshape(4 * half, 1024), "gather_late_share").reshape(4, half, 1024)
        both = jnp.moveaxis(by_core(landed, other), 0, 1)
        return unpack_full(BIG_LATE, both.reshape(4, 2 * half, 1024))

    def pair_reduce(group, gw, tag):
        gflat = pack_grads(group, gw)
        return _pair_add(gflat, _pair_exchange(gflat, "grad_pair_exchange_" + tag), cidx, "grad_pair_add_" + tag)

    def finish_reduce(group, sent, landed, tag):
        own = lax.dynamic_slice(sent, (mine, 0, 0), (1,) + sent.shape[1:])
        red = _chip_sum(lax.dynamic_update_slice(landed, own, (mine, 0, 0)), "grad_chip_sum_" + tag)
        gsh = by_core(red, _pair_share(red, "grad_pair_share_" + tag)).reshape(-1, 1024)
        out, off = {}, 0
        for n in group:
            out[n] = lax.optimization_barrier(gsh[off:off + nrow[n]]).reshape(shard2d[n].shape)
            off += nrow[n]
        return out

    sp = {n: wts[n] for n in SMALL}
    sp_local = {n: (sp[n][0] if sp[n].ndim > 1 else sp[n]) for n in SMALL}
    overlap = (packed_l, finish_gather, lambda gw: pair_reduce(BIG_LATE, gw, "late"))
    lossv, dx, gW, gs, riding = _local_step(x[0], mem[0], positions[0], loss_target[0], W, sp_local, overlap)

    g_out = finish_reduce(BIG_LATE, riding[0], riding[1], "late")
    sent_e = pair_reduce(BIG_EARLY, gW, "early")
    g_out.update(finish_reduce(BIG_EARLY, sent_e, _chip_exchange(sent_e, "grad_chip_exchange_early"), "early"))

    gs["loss"] = lossv
    raw = _pack_flat([gs[n].reshape(s) for n, s in _RAW_SMALL], 8 * 1024, F32)
    raw = _allreduce_small(raw.reshape(-1, 1024)).reshape(-1)
    rs = dict(zip([n for n, _ in _RAW_SMALL], _unpack_flat(raw, [s for _, s in _RAW_SMALL])))
    loss = rs["loss"].reshape(())
    _, disc_vjp = jax.vjp(_s5_discretize, sp_local["s5_lam_re"], sp_local["s5_lam_im"], sp_local["s5_log_dt"],
                          sp_local["s5_b_re"], sp_local["s5_b_im"])
    d_ab = rs["s5_ab"].reshape(2, S5_G, S5_P)
    g_lre, g_lim, g_ldt, g_bre, g_bim = disc_vjp((d_ab[0], d_ab[1], rs["s5_bb_re"], rs["s5_bb_im"]))
    small_g = {"s5_lam_re": g_lre, "s5_lam_im": g_lim, "s5_log_dt": g_ldt, "s5_b_re": g_bre, "s5_b_im": g_bim,
               "s5_c_re": rs["s5_c_re"], "s5_c_im": rs["s5_c_im"]}
    for n in ("ln_in_g", "ln_in_b", "ln1_g", "ln1_b", "ln2_g", "ln2_b", "ln3_g", "ln3_b", "q_norm_g", "kv_norm_g", "s5_d"):
        small_g[n] = rs[n]

    grads, deltas, new_m, new_v = {}, {}, {}, {}
    for n in BIG:
        d_, m_, v_ = _adamw(shard2d[n], g_out[n], ms[n].reshape(shard2d[n].shape), vs[n].reshape(shard2d[n].shape), "adamw_" + n)
        grads[n] = g_out[n].reshape(wts[n].shape)
        deltas[n], new_m[n], new_v[n] = (t.reshape(wts[n].shape) for t in (d_, m_, v_))
    as2d = lambda t: t.reshape(-1, t.shape[-1])
    sd, sm, sv = _adamw_many([as2d(wts[n]) for n in SMALL], [as2d(small_g[n].reshape(wts[n].shape)) for n in SMALL],
                             [as2d(ms[n]) for n in SMALL], [as2d(vs[n]) for n in SMALL], "adamw_small")
    for n, d_, m_, v_ in zip(SMALL, sd, sm, sv):
        grads[n] = small_g[n].reshape(wts[n].shape)
        deltas[n], new_m[n], new_v[n] = (t.reshape(wts[n].shape) for t in (d_, m_, v_))

    return (loss, dx[None], *[grads[n] for n in WEIGHTS], *[deltas[n] for n in WEIGHTS],
            *[new_m[n] for n in WEIGHTS], *[new_v[n] for n in WEIGHTS])
```

```python
import functools
import math

import jax
import jax.numpy as jnp
from jax import lax
from jax.experimental import pallas as pl
from jax.experimental.pallas import tpu as pltpu

F32 = jnp.float32
BF16 = jnp.bfloat16
MESH = pl.DeviceIdType.MESH

D_MODEL = 1024
S5_W = 256
S5_G = 16
S5_H = 16
S5_P = 64
S5_N = S5_G * S5_P
S5_MAX_RE = -1e-4
HEADS = 8
NOPE = 64
ROPE = 32
QK = NOPE + ROPE
VD = 64
HP = 128
XH = 4
XD = 256
LN_EPS = 1e-5
RMS_EPS = 1e-6
NEG_INF = -1e30
DN_ALPHA = 2.0 ** 0.25
ROPE_THETA = 10000.0
ADAM_LR, ADAM_B1, ADAM_B2, ADAM_EPS, ADAM_WD, ADAM_STEP = 0.001, 0.9, 0.999, 1e-08, 0.01, 10

Z_GS, Z_GA, Z_U, Z_CQ, Z_CKV, Z_KR, Z_W = 0, 1024, 2048, 2304, 2560, 2816, 2944
SCAN_CHUNK = 256

BIG_EARLY = ("w_in", "w_glu", "w_uq", "w_ukv", "w_oa", "w_o")
BIG_LATE = ("w_up", "w_xq", "w_xk", "w_xv", "w_xo", "w_down")
BIG = BIG_EARLY + BIG_LATE
COL_SHARDED = ("w_in", "w_glu", "w_uq", "w_ukv", "w_oa", "w_up")
SMALL = ("ln_in_g", "ln_in_b", "s5_lam_re", "s5_lam_im", "s5_log_dt", "s5_b_re", "s5_b_im", "s5_c_re", "s5_c_im",
         "s5_d", "q_norm_g", "kv_norm_g", "ln1_g", "ln1_b", "ln2_g", "ln2_b", "ln3_g", "ln3_b")
WEIGHTS = ("ln_in_g", "ln_in_b", "w_in", "s5_lam_re", "s5_lam_im", "s5_log_dt", "s5_b_re", "s5_b_im", "s5_c_re",
           "s5_c_im", "s5_d", "w_glu", "q_norm_g", "w_uq", "kv_norm_g", "w_ukv", "w_oa", "w_o", "ln1_g", "ln1_b",
           "w_xq", "w_xk", "w_xv", "w_xo", "ln2_g", "ln2_b", "w_up", "w_down", "ln3_g", "ln3_b")
PACK_ROWS = 2 * 16


def _row_tile(n, cap=512):
    best = n
    for t in range(8, min(n, cap) + 1, 8):
        if n % t == 0:
            best = t
    return best


def _pick(n, cap):
    best = None
    for t in range(128, min(n, cap) + 1, 128):
        if n % t == 0:
            best = t
    return best if best is not None and (best >= 512 or best == n) else n


def _mm(a, b, name, mode="nn", *, a_off=0, b_off=0, m=None, n=None, act=None, epi=None, extras=(), pextras=(),
        outs=None, out_dtype=F32, tm=1024, tn=1024, tk=None):
    if mode == "nn":
        M, (K, N) = a.shape[0], b.shape
    elif mode == "nt":
        M, (N, K) = a.shape[0], b.shape
    else:
        K, M, N = a.shape[0], m, n
    if mode == "tn":
        tm, tn, tk = _pick(M, tm), _pick(N, tn), min(tk or 512, K)
    else:
        tm, tn, tk = min(tm, M), _pick(N, tn), _pick(K, tk or 1024)
    assert M % tm == 0 and N % tn == 0 and K % tk == 0, (name, M, N, K, tm, tn, tk)
    nk = K // tk
    if mode == "tn":
        assert a_off % tm == 0 and b_off % tn == 0
        ao, bo = a_off // tm, b_off // tn
        a_spec = pl.BlockSpec((tk, tm), lambda i, j, k: (k, i + ao))
        b_spec = pl.BlockSpec((tk, tn), lambda i, j, k: (k, j + bo))
        dims = (((0,), (0,)), ((), ()))
    else:
        assert a_off % tk == 0
        ao = a_off // tk
        a_spec = pl.BlockSpec((tm, tk), lambda i, j, k: (i, k + ao))
        if mode == "nn":
            b_spec = pl.BlockSpec((tk, tn), lambda i, j, k: (k, j))
            dims = (((1,), (0,)), ((), ()))
        else:
            b_spec = pl.BlockSpec((tn, tk), lambda i, j, k: (j, k))
            dims = (((1,), (1,)), ((), ()))
    e_spec = pl.BlockSpec((tm, tn), lambda i, j, k: (i, j))
    p_spec = pl.BlockSpec((1, tn), lambda i, j, k: (0, j))
    n_extra = len(extras) + len(pextras)
    if outs is None:
        o_specs, o_shapes = [e_spec], [jax.ShapeDtypeStruct((M, N), out_dtype)]
    else:
        assert all(w == tn or N == tn for w, _ in outs), name
        o_specs = [e_spec if w == tn else pl.BlockSpec((tm, w), lambda i, j, k: (i, 0)) for w, _ in outs]
        o_shapes = [jax.ShapeDtypeStruct((M, N if w == tn else w), dt) for w, dt in outs]
    n_out = len(o_specs)

    def body(*refs):
        a_ref, b_ref = refs[0], refs[1]
        e_refs = refs[2:2 + n_extra]
        o_refs = refs[2 + n_extra:2 + n_extra + n_out]
        av = a_ref[...]
        if act is not None:
            av = act(av.astype(F32))
        p = lax.dot_general(av.astype(BF16), b_ref[...].astype(BF16), dims, preferred_element_type=F32)

        def finish(r):
            if epi is not None:
                r = epi(r, *[e[...] for e in e_refs])
            for o_ref, v in zip(o_refs, r if isinstance(r, tuple) else (r,)):
                o_ref[...] = v.astype(o_ref.dtype)

        if nk == 1:
            finish(p)
        else:
            acc = refs[2 + n_extra + n_out]
            k = pl.program_id(2)

            @pl.when(k == 0)
            def _():
                acc[...] = p

            @pl.when(k > 0)
            def _():
                acc[...] += p

            @pl.when(k == nk - 1)
            def _():
                finish(acc[...])

    res = pl.pallas_call(
        body, name=name, grid=(M // tm, N // tn, nk),
        in_specs=[a_spec, b_spec] + [e_spec] * len(extras) + [p_spec] * len(pextras), out_specs=o_specs,
        out_shape=o_shapes,
        scratch_shapes=[pltpu.VMEM((tm, tn), F32)] if nk > 1 else [],
        compiler_params=pltpu.CompilerParams(dimension_semantics=("parallel", "parallel", "arbitrary")),
    )(a, b, *extras, *pextras)
    return res[0] if outs is None else res


def _rowwise(fn, name, rows, pars, outs, accs=(), tm=256, upcast=True):
    T = rows[0][0].shape[0]
    tm = min(tm, T)
    assert T % tm == 0
    n_in, n_out = len(rows) + len(pars), len(outs)
    in_specs = []
    for arr, off, w in rows:
        assert off % w == 0 and arr.shape[0] == T, name
        in_specs.append(pl.BlockSpec((tm, w), functools.partial(lambda i, o: (i, o), o=off // w)))
    for p in pars:
        in_specs.append(pl.BlockSpec(p.shape, lambda i: (0, 0)))
    out_specs = [pl.BlockSpec((tm, w), lambda i: (i, 0)) for w, _ in outs]
    out_specs += [pl.BlockSpec(s, lambda i: (0, 0)) for s in accs]
    out_shape = [jax.ShapeDtypeStruct((T, w), dt) for w, dt in outs] + [jax.ShapeDtypeStruct(s, F32) for s in accs]

    def body(*refs):
        res = fn(*[r[...].astype(F32) if upcast else r[...] for r in refs[:n_in]])
        o_refs = refs[n_in:]
        for r, v in zip(o_refs[:n_out], res[:n_out]):
            r[...] = v.astype(r.dtype)
        if accs:
            i = pl.program_id(0)

            @pl.when(i == 0)
            def _():
                for r in o_refs[n_out:]:
                    r[...] = jnp.zeros_like(r)

            for r, v in zip(o_refs[n_out:], res[n_out:]):
                r[...] += v

    res = pl.pallas_call(
        body, name=name, grid=(T // tm,), in_specs=in_specs, out_specs=out_specs, out_shape=out_shape,
        compiler_params=pltpu.CompilerParams(dimension_semantics=("arbitrary",)),
    )(*[r[0] for r in rows], *pars)
    return res


def _whole(arr):
    return (arr, 0, arr.shape[1])


def _ln(x, g, b):
    mu = jnp.mean(x, axis=-1, keepdims=True)
    xc = x - mu
    var = jnp.mean(xc * xc, axis=-1, keepdims=True)
    return xc * lax.rsqrt(var + LN_EPS) * g + b


def _ln_res(h, r, g, b):
    return _ln(DN_ALPHA * h + r, g, b)


def _rms(x, g):
    return x * lax.rsqrt(jnp.mean(x * x, axis=-1, keepdims=True) + RMS_EPS) * g


def _gelu_skip(yc, u, d):
    y = yc + d * u
    return 0.5 * y * (1.0 + lax.erf(y * (1.0 / math.sqrt(2.0))))


def _gate_mix(gs, ga, y2a, y2b, aout):
    return jax.nn.sigmoid(gs) * (y2a * jax.nn.sigmoid(y2b)) + jax.nn.sigmoid(ga) * aout


def _relu2(a):
    r = jnp.maximum(a, 0.0)
    return r * r


def _tile8(t):
    return jnp.concatenate([t] * HEADS, axis=1)


def _rope(x, cos, s1, s2):
    w = x.shape[1]
    return x * cos + pltpu.roll(x, ROPE // 2, 1) * s1 + pltpu.roll(x, w - ROPE // 2, 1) * s2


def _rope_t(dy, cos, s1, s2):
    w = dy.shape[1]
    return dy * cos + pltpu.roll(dy * s1, w - ROPE // 2, 1) + pltpu.roll(dy * s2, ROPE // 2, 1)


def _rope_lanes(shape):
    lane = lax.broadcasted_iota(jnp.int32, shape, 1) % HP
    return (lane >= NOPE) & (lane < QK)


ATT_HB_FWD, ATT_HB_BWD = 8, 4
ATT_SCALE = QK ** -0.5
LOG2E = 1.4426950408889634
LN2 = 0.6931471805599453
_NT = (((1,), (1,)), ((), ()))
_TN = (((0,), (0,)), ((), ()))


def _carry_specs(carry):
    if carry is None:
        return [], [], [], []
    arr = carry[0]
    return ([_HBM], [_HBM], [jax.ShapeDtypeStruct((4,) + arr.shape[1:], arr.dtype)],
            [pltpu.SemaphoreType.DMA((3,)), pltpu.SemaphoreType.DMA((3,))])


def _carry_start(exchange, when):
    @pl.when(when)
    def _():
        for cp in exchange(False):
            cp.start()


def _carry_wait(exchange, when):
    @pl.when(when)
    def _():
        for cp in exchange(True):
            cp.wait_recv()
        for cp in exchange(False):
            cp.wait_send()


def _tri_tables(n, by_row):
    if by_row:
        pairs = [(i, j) for i in range(n) for j in range(i + 1)]
    else:
        pairs = [(i, j) for j in range(n) for i in range(j, n)]
    return jnp.array([p[0] for p in pairs], jnp.int32), jnp.array([p[1] for p in pairs], jnp.int32)


def _flash_fwd(q, k, v, name, bq=512, carry=None):
    T = q.shape[0]
    bq = min(bq, T)
    nq = T // bq
    hb = ATT_HB_FWD
    n_carry = 0 if carry is None else 1

    i_tab, j_tab = _tri_tables(nq, by_row=True)

    def body(it_ref, jt_ref, q_ref, k_ref, v_ref, *rest):
        o_ref, lse_ref = rest[n_carry:n_carry + 2]
        m_sc, l_sc, acc_sc = rest[2 * n_carry + 2:2 * n_carry + 5]
        t = pl.program_id(1)
        i, j = it_ref[t], jt_ref[t]
        if carry is not None:
            exchange = functools.partial(_carried_exchange, rest[0], rest[3], rest[-2], rest[-1], carry[1])
            _carry_start(exchange, (pl.program_id(0) == 0) & (t == 0))

        @pl.when(j == 0)
        def _():
            m_sc[...] = jnp.full_like(m_sc, NEG_INF)
            l_sc[...] = jnp.zeros_like(l_sc)
            acc_sc[...] = jnp.zeros_like(acc_sc)

        def step(masked):
            for hh in range(hb):
                sl = slice(hh * HP, (hh + 1) * HP)
                s = lax.dot_general(q_ref[:, sl], k_ref[:, sl], _NT, preferred_element_type=F32)
                if masked:
                    row = lax.broadcasted_iota(jnp.int32, (bq, bq), 0)
                    col = lax.broadcasted_iota(jnp.int32, (bq, bq), 1)
                    s = jnp.where(col <= row, s, NEG_INF)
                m_prev = m_sc[hh]
                m_new = jnp.maximum(m_prev, jnp.max(s, axis=1, keepdims=True))
                alpha = jnp.exp2(m_prev - m_new)
                p = jnp.exp2(s - jnp.concatenate([m_new] * (bq // HP), axis=1))
                l_sc[hh] = alpha * l_sc[hh] + jnp.sum(p, axis=1, keepdims=True)
                acc_sc[hh] = alpha * acc_sc[hh] + jnp.dot(p.astype(BF16), v_ref[:, sl], preferred_element_type=F32)
                m_sc[hh] = m_new

        @pl.when(j < i)
        def _():
            step(False)

        @pl.when(j == i)
        def _():
            step(True)
            for hh in range(hb):
                o_ref[:, hh * HP:(hh + 1) * HP] = acc_sc[hh] / l_sc[hh]
                lse_ref[hh] = jnp.transpose((m_sc[hh] + jnp.log2(l_sc[hh])) * LN2)[0:1, :]

        if carry is not None:
            _carry_wait(exchange, (pl.program_id(0) == HEADS // hb - 1) & (t == i_tab.shape[0] - 1))

    qs = pl.BlockSpec((bq, hb * HP), lambda h, t, it, jt: (it[t], h))
    ks = pl.BlockSpec((bq, hb * HP), lambda h, t, it, jt: (jt[t], h))
    c_in, c_out, c_shape, c_sems = _carry_specs(carry)
    gs = pltpu.PrefetchScalarGridSpec(
        num_scalar_prefetch=2, grid=(HEADS // hb, i_tab.shape[0]), in_specs=[qs, ks, ks] + c_in,
        out_specs=[qs, pl.BlockSpec((hb, 1, bq), lambda h, t, it, jt: (h, 0, it[t]))] + c_out,
        scratch_shapes=[pltpu.VMEM((hb, bq, HP), F32)] * 3 + c_sems)
    return pl.pallas_call(
        body, name=name, grid_spec=gs,
        out_shape=[jax.ShapeDtypeStruct((T, HEADS * HP), F32), jax.ShapeDtypeStruct((HEADS, 1, T), F32)] + c_shape,
        compiler_params=pltpu.CompilerParams(dimension_semantics=("arbitrary", "arbitrary")),
    )(i_tab, j_tab, q, k, v, *([] if carry is None else [carry[0]]))


def _flash_bwd(q, k, v, do, lse_row, delta_row, name, bq=512, carry=None):
    T = q.shape[0]
    bq = min(bq, T)
    nq = T // bq
    hb = ATT_HB_BWD
    n_carry = 0 if carry is None else 1

    i_tab, j_tab = _tri_tables(nq, by_row=False)
    n_blk = i_tab.shape[0]

    def body(it_ref, jt_ref, q_ref, k_ref, v_ref, do_ref, lse_ref, dl_ref, *rest):
        dq_ref, dk_ref, dv_ref = rest[n_carry:n_carry + 3]
        t = pl.program_id(1)
        i, j = it_ref[t], jt_ref[t]
        if carry is not None:
            exchange = functools.partial(_carried_exchange, rest[0], rest[4], rest[-2], rest[-1], carry[1])
            _carry_start(exchange, (pl.program_id(0) == 0) & (t == 0))

        @pl.when(t == 0)
        def _():
            dq_ref[...] = jnp.zeros_like(dq_ref)

        @pl.when(i == j)
        def _():
            dk_ref[...] = jnp.zeros_like(dk_ref)
            dv_ref[...] = jnp.zeros_like(dv_ref)

        def step(masked):
            rows = pl.ds(pl.multiple_of(i * bq, bq), bq)
            for hh in range(hb):
                sl = slice(hh * HP, (hh + 1) * HP)
                qh, kh, doh = q_ref[:, sl], k_ref[:, sl], do_ref[:, sl]
                st = lax.dot_general(kh, qh, _NT, preferred_element_type=F32)
                pt = jnp.exp2(st - lse_ref[hh] * LOG2E)
                if masked:
                    krow = lax.broadcasted_iota(jnp.int32, (bq, bq), 0)
                    qcol = lax.broadcasted_iota(jnp.int32, (bq, bq), 1)
                    pt = jnp.where(krow <= qcol, pt, 0.0)
                dv_ref[:, sl] += jnp.dot(pt.astype(BF16), doh, preferred_element_type=F32)
                dpt = lax.dot_general(v_ref[:, sl], doh, _NT, preferred_element_type=F32)
                dst = (pt * (dpt - dl_ref[hh])).astype(BF16)
                dk_ref[:, sl] += jnp.dot(dst, qh, preferred_element_type=F32)
                dq_ref[rows, sl] += lax.dot_general(dst, kh, _TN, preferred_element_type=F32)

        @pl.when(i > j)
        def _():
            step(False)

        @pl.when(i == j)
        def _():
            step(True)

        @pl.when(i == nq - 1)
        def _():
            dk_ref[...] *= LN2

        @pl.when(t == n_blk - 1)
        def _():
            dq_ref[...] *= ATT_SCALE

        if carry is not None:
            _carry_wait(exchange, (pl.program_id(0) == HEADS // hb - 1) & (t == n_blk - 1))

    qs = pl.BlockSpec((bq, hb * HP), lambda h, t, it, jt: (it[t], h))
    ks = pl.BlockSpec((bq, hb * HP), lambda h, t, it, jt: (jt[t], h))
    rs = pl.BlockSpec((hb, 1, bq), lambda h, t, it, jt: (h, 0, it[t]))
    full = pl.BlockSpec((T, hb * HP), lambda h, t, it, jt: (0, h))
    c_in, c_out, c_shape, c_sems = _carry_specs(carry)
    gs = pltpu.PrefetchScalarGridSpec(num_scalar_prefetch=2, grid=(HEADS // hb, n_blk),
                                      in_specs=[qs, ks, ks, qs, rs, rs] + c_in, out_specs=[full, ks, ks] + c_out,
                                      scratch_shapes=c_sems)
    return pl.pallas_call(
        body, name=name, grid_spec=gs, out_shape=[jax.ShapeDtypeStruct((T, HEADS * HP), F32)] * 3 + c_shape,
        compiler_params=pltpu.CompilerParams(dimension_semantics=("arbitrary", "arbitrary")),
    )(i_tab, j_tab, q, k, v, do, lse_row, delta_row, *([] if carry is None else [carry[0]]))


def _xattn_heads(q, k, v):
    scale = XD ** -0.5
    ps = []
    for h in range(XH):
        sl = slice(h * XD, (h + 1) * XD)
        s = lax.dot_general(q[:, sl].astype(BF16), k[:, sl].astype(BF16), (((1,), (1,)), ((), ())),
                            preferred_element_type=F32) * scale
        e = jnp.exp(s - jnp.max(s, axis=1, keepdims=True))
        ps.append(e / jnp.sum(e, axis=1, keepdims=True))
    return ps


def _xattn_fwd_fn(q, k, v):
    ps = _xattn_heads(q, k, v)
    o = [jnp.dot(p.astype(BF16), v[:, h * XD:(h + 1) * XD].astype(BF16), preferred_element_type=F32)
         for h, p in enumerate(ps)]
    return (jnp.concatenate(o, axis=1),)


def _xattn_bwd_fn(q, do, k, v):
    scale = XD ** -0.5
    ps = _xattn_heads(q, k, v)
    dqs, dks, dvs = [], [], []
    tdims = (((0,), (0,)), ((), ()))
    for h, p in enumerate(ps):
        sl = slice(h * XD, (h + 1) * XD)
        doh = do[:, sl].astype(BF16)
        dvs.append(lax.dot_general(p.astype(BF16), doh, tdims, preferred_element_type=F32))
        dp = lax.dot_general(doh, v[:, sl].astype(BF16), (((1,), (1,)), ((), ())), preferred_element_type=F32)
        ds = (p * (dp - jnp.sum(dp * p, axis=1, keepdims=True)) * scale).astype(BF16)
        dqs.append(jnp.dot(ds, k[:, sl].astype(BF16), preferred_element_type=F32))
        dks.append(lax.dot_general(ds, q[:, sl].astype(BF16), tdims, preferred_element_type=F32))
    return jnp.concatenate(dqs, axis=1), jnp.concatenate(dks, axis=1), jnp.concatenate(dvs, axis=1)


S5_SUB = 8


def _cmul_add(xr, xi, ar, ai, sr, si):
    return xr + ar * sr - ai * si, xi + ar * si + ai * sr


def _s5_tables(ar, ai, reverse):
    pows = [(ar, ai)]
    for _ in range(S5_SUB - 1):
        pr, pi = pows[-1]
        pows.append((pr * ar - pi * ai, pr * ai + pi * ar))
    cr = jnp.concatenate([p[0] for p in pows], axis=0)
    ci = jnp.concatenate([p[1] for p in pows], axis=0)
    if reverse:
        cr, ci = cr[::-1], ci[::-1]
    t = jnp.arange(S5_SUB)[:, None]
    blocks = [jnp.concatenate([cr, ci], axis=1)]
    for s in (1, 2, 4):
        keep = (t < S5_SUB - s) if reverse else (t >= s)
        sr, si = pows[s - 1]
        blocks.append(jnp.concatenate([jnp.where(keep, sr, 0.0), jnp.where(keep, si, 0.0)], axis=1))
    return jnp.concatenate(blocks, axis=0)


def _sub_scan(xr, xi, tab_ref, reverse):
    for k, s in enumerate((1, 2, 4)):
        blk = slice(S5_SUB * (k + 1), S5_SUB * (k + 2))
        sh = S5_SUB - s if reverse else s
        xr, xi = _cmul_add(xr, xi, tab_ref[blk, :S5_N], tab_ref[blk, S5_N:], pltpu.roll(xr, sh, 0), pltpu.roll(xi, sh, 0))
    return xr, xi


def _s5_fwd(z, b_st, c_st, d_skip, tab, name, tc=512):
    T = z.shape[0]
    tc = min(tc, T)
    nsub = tc // S5_SUB

    def body(u_ref, bst_ref, cst_ref, d_ref, tab_ref, h_ref, yc_ref, yg_ref, carry, x_sc):
        @pl.when(pl.program_id(0) == 0)
        def _():
            carry[...] = jnp.zeros_like(carry)

        u = u_ref[...].astype(F32)
        x_sc[...] = jnp.dot(u.astype(BF16), bst_ref[...], preferred_element_type=F32)

        def sub(b, c):
            r = pl.ds(pl.multiple_of(b * S5_SUB, S5_SUB), S5_SUB)
            xr, xi = _sub_scan(x_sc[r, :S5_N], x_sc[r, S5_N:], tab_ref, False)
            hr, hi = _cmul_add(xr, xi, tab_ref[0:S5_SUB, :S5_N], tab_ref[0:S5_SUB, S5_N:], c[0], c[1])
            x_sc[r, :S5_N] = hr
            x_sc[r, S5_N:] = hi
            return hr[S5_SUB - 1:S5_SUB], hi[S5_SUB - 1:S5_SUB]

        cr, ci = lax.fori_loop(0, nsub, sub, (carry[0:1, :S5_N], carry[0:1, S5_N:]))
        carry[0:1, :S5_N] = cr
        carry[0:1, S5_N:] = ci
        h = x_sc[...]
        h_ref[...] = h
        yc = jnp.dot(h.astype(BF16), cst_ref[...], preferred_element_type=F32)
        yc_ref[...] = yc
        yg_ref[...] = _gelu_skip(yc, u, d_ref[...]).astype(yg_ref.dtype)

    whole = lambda a: pl.BlockSpec(a.shape, lambda i: (0, 0))
    row = lambda w: pl.BlockSpec((tc, w), lambda i: (i, 0))
    return pl.pallas_call(
        body, name=name, grid=(T // tc,),
        in_specs=[pl.BlockSpec((tc, S5_W), lambda i: (i, Z_U // S5_W)), whole(b_st), whole(c_st), whole(d_skip), whole(tab)],
        out_specs=[row(2 * S5_N), row(S5_W), row(S5_W)],
        out_shape=[jax.ShapeDtypeStruct((T, 2 * S5_N), F32), jax.ShapeDtypeStruct((T, S5_W), F32),
                   jax.ShapeDtypeStruct((T, S5_W), BF16)],
        scratch_shapes=[pltpu.VMEM((8, 2 * S5_N), F32), pltpu.VMEM((tc, 2 * S5_N), F32)],
        compiler_params=pltpu.CompilerParams(dimension_semantics=("arbitrary",)),
    )(z, b_st, c_st, d_skip, tab)


def _s5_bwd(dyg, yc, z, h, b_st, c_st, d_skip, tab, name, tc=512):
    T = z.shape[0]
    tc = min(tc, T)
    nc, nsub = T // tc, tc // S5_SUB

    def body(dyg_ref, yc_ref, u_ref, h_ref, hp_ref, bst_ref, cst_ref, d_ref, tab_ref,
             du_ref, da_ref, db_ref, dct_ref, dd_ref, carry, x_sc):
        g = pl.program_id(0)

        @pl.when(g == 0)
        def _():
            carry[...] = jnp.zeros_like(carry)
            for r in (da_ref, db_ref, dct_ref, dd_ref):
                r[...] = jnp.zeros_like(r)

        u = u_ref[...].astype(F32)
        _, vjp = jax.vjp(_gelu_skip, yc_ref[...], u, d_ref[...])
        dyc, du_skip, dd = vjp(dyg_ref[...])
        dd_ref[...] += dd
        dyc16 = dyc.astype(BF16)
        dct_ref[...] += lax.dot_general(dyc16, h_ref[...].astype(BF16), _TN, preferred_element_type=F32)
        x_sc[...] = lax.dot_general(dyc16, cst_ref[...], _NT, preferred_element_type=F32)
        first = jnp.where(g == nc - 1, 0.0, 1.0)
        row0 = lax.broadcasted_iota(jnp.int32, (S5_SUB, S5_N), 0) == 0

        def sub(k, c):
            cr, ci, acc_r, acc_i = c
            b = nsub - 1 - k
            r = pl.ds(pl.multiple_of(b * S5_SUB, S5_SUB), S5_SUB)
            xr, xi = _sub_scan(x_sc[r, :S5_N], x_sc[r, S5_N:], tab_ref, True)
            lr, li = _cmul_add(xr, xi, tab_ref[0:S5_SUB, :S5_N], tab_ref[0:S5_SUB, S5_N:], cr, ci)
            x_sc[r, :S5_N] = lr
            x_sc[r, S5_N:] = li
            rp = pl.ds(pl.multiple_of(jnp.maximum(b - 1, 0) * S5_SUB, S5_SUB), S5_SUB)
            last_r = jnp.where(b == 0, hp_ref[S5_SUB - 1:S5_SUB, :S5_N] * first, h_ref[rp, :S5_N][S5_SUB - 1:S5_SUB])
            last_i = jnp.where(b == 0, hp_ref[S5_SUB - 1:S5_SUB, S5_N:] * first, h_ref[rp, S5_N:][S5_SUB - 1:S5_SUB])
            pr = jnp.where(row0, last_r, pltpu.roll(h_ref[r, :S5_N], 1, 0))
            pi = jnp.where(row0, last_i, pltpu.roll(h_ref[r, S5_N:], 1, 0))
            return lr[0:1], li[0:1], acc_r + (lr * pr + li * pi), acc_i + (li * pr - lr * pi)

        zero = jnp.zeros((S5_SUB, S5_N), F32)
        cr, ci, acc_r, acc_i = lax.fori_loop(0, nsub, sub, (carry[0:1, :S5_N], carry[0:1, S5_N:], zero, zero))
        carry[0:1, :S5_N] = cr
        carry[0:1, S5_N:] = ci
        da_ref[0:1, :S5_N] += jnp.sum(acc_r, axis=0, keepdims=True)
        da_ref[0:1, S5_N:] += jnp.sum(acc_i, axis=0, keepdims=True)
        lam16 = x_sc[...].astype(BF16)
        db_ref[...] += lax.dot_general(u.astype(BF16), lam16, _TN, preferred_element_type=F32)
        du = lax.dot_general(lam16, bst_ref[...], _NT, preferred_element_type=F32) + du_skip
        du_ref[...] = du.astype(du_ref.dtype)

    whole = lambda a: pl.BlockSpec(a.shape, lambda g: (0, 0))
    row = lambda w, off=0: pl.BlockSpec((tc, w), lambda g: (nc - 1 - g, off))
    prev = pl.BlockSpec((S5_SUB, 2 * S5_N), lambda g: (jnp.maximum((nc - 1 - g) * nsub - 1, 0), 0))
    acc = lambda s: pl.BlockSpec(s, lambda g: (0, 0))
    return pl.pallas_call(
        body, name=name, grid=(nc,),
        in_specs=[row(S5_W), row(S5_W), row(S5_W, Z_U // S5_W), row(2 * S5_N), prev, whole(b_st), whole(c_st),
                  whole(d_skip), whole(tab)],
        out_specs=[row(S5_W), acc((1, 2 * S5_N)), acc((S5_W, 2 * S5_N)), acc((S5_W, 2 * S5_N)), acc((1, S5_W))],
        out_shape=[jax.ShapeDtypeStruct((T, S5_W), BF16), jax.ShapeDtypeStruct((1, 2 * S5_N), F32),
                   jax.ShapeDtypeStruct((S5_W, 2 * S5_N), F32), jax.ShapeDtypeStruct((S5_W, 2 * S5_N), F32),
                   jax.ShapeDtypeStruct((1, S5_W), F32)],
        scratch_shapes=[pltpu.VMEM((8, 2 * S5_N), F32), pltpu.VMEM((tc, 2 * S5_N), F32)],
        compiler_params=pltpu.CompilerParams(dimension_semantics=("arbitrary",)),
    )(dyg, yc, z, h, h, b_st, c_st, d_skip, tab)


def _s5_discretize(lam_re, lam_im, log_dt, b_re, b_im):
    lr = jnp.minimum(lam_re, S5_MAX_RE)
    li = lam_im
    dt = jnp.exp(log_dt)[:, None]
    mag = jnp.exp(lr * dt)
    ang = li * dt
    ab_re = mag * jnp.cos(ang)
    ab_im = mag * jnp.sin(ang)
    den = lr * lr + li * li
    nr = ab_re - 1.0
    f_re = ((nr * lr + ab_im * li) / den)[..., None]
    f_im = ((ab_im * lr - nr * li) / den)[..., None]
    return ab_re, ab_im, f_re * b_re - f_im * b_im, f_re * b_im + f_im * b_re


def _adamw_fn(w, g, m, v):
    m = ADAM_B1 * m + (1.0 - ADAM_B1) * g
    v = ADAM_B2 * v + (1.0 - ADAM_B2) * (g * g)
    m_hat = m / (1.0 - ADAM_B1 ** ADAM_STEP)
    v_hat = v / (1.0 - ADAM_B2 ** ADAM_STEP)
    delta = -ADAM_LR * (m_hat / (jnp.sqrt(v_hat) + ADAM_EPS) + ADAM_WD * w)
    return delta, m, v


def _adamw(w, g, m, v, name):
    wd = w.shape[1]
    return _rowwise(_adamw_fn, name, [_whole(w), _whole(g), _whole(m), _whole(v)], [], [(wd, F32)] * 3,
                    tm=_row_tile(w.shape[0]))


def _adamw_many(ws, gs, ms, vs, name):
    n = len(ws)

    def body(*refs):
        for i in range(n):
            d_, m_, v_ = _adamw_fn(*[refs[kk * n + i][...] for kk in range(4)])
            refs[4 * n + i][...] = d_
            refs[5 * n + i][...] = m_
            refs[6 * n + i][...] = v_

    out = pl.pallas_call(body, name=name, out_shape=[jax.ShapeDtypeStruct(w.shape, F32) for w in ws] * 3)(*ws, *gs, *ms, *vs)
    return out[:n], out[n:2 * n], out[2 * n:]


def _me():
    return lax.axis_index("x"), lax.axis_index("y"), lax.axis_index("c")


def _chip_of(j):
    return j // 2, j % 2


_HBM = pl.BlockSpec(memory_space=pl.ANY)


def _gather_weights(shard):
    _, R, W = shard.shape

    def body(s_ref, out_ref, send_sems, recv_sems):
        x, y, c = _me()
        mine = 2 * x + y
        sib = (x, y, 1 - c)
        chips = [(1 - x, y), (x, 1 - y), (1 - x, 1 - y)]

        def rcopy(kk, src, chip_idx, half, to):
            return pltpu.make_async_remote_copy(src_ref=src, dst_ref=out_ref.at[chip_idx, half],
                                                send_sem=send_sems.at[kk], recv_sem=recv_sems.at[kk],
                                                device_id=to, device_id_type=MESH)

        first = [rcopy(j, s_ref.at[c], mine, c, (*chip, c)) for j, chip in enumerate(chips)]
        for cp in first:
            cp.start()
        passed = []
        for j, (px, py) in enumerate(chips):
            src_chip = 2 * px + py
            rcopy(j, s_ref.at[c], src_chip, c, (x, y, c)).wait_recv()
            fwd = rcopy(3 + j, out_ref.at[src_chip, c], src_chip, c, sib)
            fwd.start()
            passed.append(fwd)
        for j, (px, py) in enumerate(chips):
            rcopy(3 + j, s_ref.at[c], 2 * px + py, 1 - c, (x, y, c)).wait_recv()
        for cp in first + passed:
            cp.wait_send()

    return pl.pallas_call(
        body, name="gather_weights", in_specs=[_HBM], out_specs=_HBM,
        out_shape=jax.ShapeDtypeStruct((4, 2, R, W), shard.dtype),
        scratch_shapes=[pltpu.SemaphoreType.DMA((6,)), pltpu.SemaphoreType.DMA((6,))],
    )(shard)


def _pair_exchange(g, name):
    _, _, R, W = g.shape

    def body(g_ref, out_ref, send_sem, recv_sem):
        x, y, c = _me()
        cp = pltpu.make_async_remote_copy(src_ref=g_ref.at[:, 1 - c], dst_ref=out_ref, send_sem=send_sem,
                                          recv_sem=recv_sem, device_id=(x, y, 1 - c), device_id_type=MESH)
        cp.start()
        cp.wait()

    return pl.pallas_call(
        body, name=name, in_specs=[_HBM], out_specs=_HBM,
        out_shape=jax.ShapeDtypeStruct((4, R, W), g.dtype),
        scratch_shapes=[pltpu.SemaphoreType.DMA(()), pltpu.SemaphoreType.DMA(())],
    )(g)


def _pair_add(g, recv, cidx, name):
    _, _, R, W = g.shape
    tr = _row_tile(R)

    def body(c_ref, a_ref, b_ref, o_ref):
        o_ref[...] = (a_ref[...] + b_ref[...]).astype(o_ref.dtype)

    gs = pltpu.PrefetchScalarGridSpec(
        num_scalar_prefetch=1, grid=(4, R // tr),
        in_specs=[pl.BlockSpec((None, None, tr, W), lambda j, r, c: (j, c[0], r, 0)),
                  pl.BlockSpec((None, tr, W), lambda j, r, c: (j, r, 0))],
        out_specs=pl.BlockSpec((None, tr, W), lambda j, r, c: (j, r, 0)))
    return pl.pallas_call(body, name=name, grid_spec=gs,
                          out_shape=jax.ShapeDtypeStruct((4, R, W), BF16))(cidx, g, recv)


def _carried_exchange(src_ref, dst_ref, send_sems, recv_sems, spread, incoming):
    x, y, c = _me()
    mine = 2 * x + y
    copies = []
    for j, (px, py) in enumerate([(1 - x, y), (x, 1 - y), (1 - x, 1 - y)]):
        src = src_ref.at[2 * px + py] if spread else src_ref.at[c]
        slot = 2 * px + py if incoming else mine
        copies.append(pltpu.make_async_remote_copy(src_ref=src, dst_ref=dst_ref.at[slot], send_sem=send_sems.at[j],
                                                   recv_sem=recv_sems.at[j], device_id=(px, py, c), device_id_type=MESH))
    return copies


def _chip_exchange(p, name):
    _, R, W = p.shape

    def body(p_ref, out_ref, send_sems, recv_sems):
        outs = _carried_exchange(p_ref, out_ref, send_sems, recv_sems, True, False)
        for cp in outs:
            cp.start()
        for cp in _carried_exchange(p_ref, out_ref, send_sems, recv_sems, True, True):
            cp.wait_recv()
        for cp in outs:
            cp.wait_send()

    return pl.pallas_call(
        body, name=name, in_specs=[_HBM], out_specs=_HBM,
        out_shape=jax.ShapeDtypeStruct((4, R, W), p.dtype),
        scratch_shapes=[pltpu.SemaphoreType.DMA((3,)), pltpu.SemaphoreType.DMA((3,))],
    )(p)


def _chip_sum(q, name):
    _, R, W = q.shape
    tr = _row_tile(R)

    def body(q_ref, o_ref):
        f = lambda t: q_ref[t].astype(F32)
        o_ref[...] = ((f(0) + f(1)) + f(2)) + f(3)

    return pl.pallas_call(body, name=name, grid=(R // tr,),
                          in_specs=[pl.BlockSpec((4, tr, W), lambda r: (0, r, 0))],
                          out_specs=pl.BlockSpec((tr, W), lambda r: (r, 0)),
                          out_shape=jax.ShapeDtypeStruct((R, W), F32))(q)


def _pair_share(r, name):
    R, W = r.shape

    def body(r_ref, out_ref, send_sem, recv_sem):
        x, y, c = _me()
        cp = pltpu.make_async_remote_copy(src_ref=r_ref, dst_ref=out_ref, send_sem=send_sem, recv_sem=recv_sem,
                                          device_id=(x, y, 1 - c), device_id_type=MESH)
        cp.start()
        cp.wait()

    return pl.pallas_call(
        body, name=name, in_specs=[_HBM], out_specs=_HBM,
        out_shape=jax.ShapeDtypeStruct((R, W), r.dtype),
        scratch_shapes=[pltpu.SemaphoreType.DMA(()), pltpu.SemaphoreType.DMA(())],
    )(r)


def _allreduce_small(vec):
    R, W = vec.shape

    def body(v_ref, out_ref, buf, send_sems, recv_sems):
        x, y, c = _me()
        me = 4 * x + 2 * y + c
        buf[me] = v_ref[...]
        cps = []
        for kk in range(1, 8):
            peer = (x ^ (kk >> 2), y ^ ((kk >> 1) & 1), c ^ (kk & 1))
            cp = pltpu.make_async_remote_copy(src_ref=v_ref, dst_ref=buf.at[me], send_sem=send_sems.at[kk - 1],
                                              recv_sem=recv_sems.at[kk - 1], device_id=peer, device_id_type=MESH)
            cp.start()
            cps.append(cp)
        for kk in range(1, 8):
            peer = (x ^ (kk >> 2), y ^ ((kk >> 1) & 1), c ^ (kk & 1))
            pltpu.make_async_remote_copy(src_ref=v_ref, dst_ref=buf.at[me ^ kk], send_sem=send_sems.at[kk - 1],
                                         recv_sem=recv_sems.at[kk - 1], device_id=peer, device_id_type=MESH).wait_recv()
        for cp in cps:
            cp.wait_send()
        acc = buf[0]
        for d in range(1, 8):
            acc = acc + buf[d]
        out_ref[...] = acc

    vm = pl.BlockSpec(memory_space=pltpu.VMEM)
    return pl.pallas_call(
        body, name="allreduce_small", in_specs=[vm], out_specs=vm, out_shape=jax.ShapeDtypeStruct((R, W), F32),
        scratch_shapes=[pltpu.VMEM((8, R, W), F32), pltpu.SemaphoreType.DMA((7,)), pltpu.SemaphoreType.DMA((7,))],
    )(vec)


def _pack_flat(parts, align, dtype):
    flat = jnp.concatenate([p.reshape(-1).astype(dtype) for p in parts])
    n = flat.shape[0]
    pad = (-n) % align
    return jnp.pad(flat, (0, pad)) if pad else flat


def _unpack_flat(flat, shapes):
    out, off = [], 0
    for s in shapes:
        n = math.prod(s)
        out.append(flat[off:off + n].reshape(s))
        off += n
    return out


def _full_from_shards(name, sh):
    if name in COL_SHARDED:
        return jnp.transpose(sh, (1, 0, 2)).reshape(sh.shape[1], 4 * sh.shape[2])
    return sh.reshape(4 * sh.shape[1], sh.shape[2])


def _shards_from_full(name, full):
    if name in COL_SHARDED:
        r, cc = full.shape[0], full.shape[1] // 4
        return jnp.transpose(full.reshape(r, 4, cc), (1, 0, 2)).reshape(4, r * cc)
    return full.reshape(4, -1)


def _pad_heads(w, width):
    k = w.shape[0]
    return jnp.pad(w.reshape(k, HEADS, width), ((0, 0), (0, 0), (0, HP - width))).reshape(k, HEADS * HP)


def _unpad_heads(w, width):
    k = w.shape[0]
    return w.reshape(k, HEADS, HP)[:, :, :width].reshape(k, HEADS * width)


def _blockdiag(t):
    g, a, b = t.shape
    return jnp.einsum("gab,gk->gakb", t, jnp.eye(g, dtype=t.dtype)).reshape(g * a, g * b)


def _blockdiag_t(m, a, b):
    g = m.shape[0] // a
    return jnp.einsum("gagb->gab", m.reshape(g, a, g, b))


def _local_step(x, mem, positions, target, W, sp, overlap=None):
    T = x.shape[0]
    row = lambda v: v.reshape(1, -1)

    ab_re, ab_im, bb_re, bb_im = _s5_discretize(sp["s5_lam_re"], sp["s5_lam_im"], sp["s5_log_dt"], sp["s5_b_re"], sp["s5_b_im"])
    ar, ai = ab_re.reshape(1, S5_N), ab_im.reshape(1, S5_N)
    tab_fwd, tab_rev = _s5_tables(ar, ai, False), _s5_tables(ar, -ai, True)
    b_st = jnp.concatenate([_blockdiag(jnp.swapaxes(bb_re, 1, 2)), _blockdiag(jnp.swapaxes(bb_im, 1, 2))],
                           axis=1).astype(BF16)
    c_st = jnp.concatenate([_blockdiag(jnp.swapaxes(sp["s5_c_re"], 1, 2)),
                            -_blockdiag(jnp.swapaxes(sp["s5_c_im"], 1, 2))], axis=0).astype(BF16)
    inv = ROPE_THETA ** (-jnp.arange(0, ROPE, 2, dtype=F32) / ROPE)
    ang = positions.astype(F32)[:, None] * inv
    cs, sn = jnp.cos(ang), jnp.sin(ang)
    one, zero = jnp.ones((T, NOPE), F32), jnp.zeros((T, NOPE), F32)
    z16, tail1, tail0 = jnp.zeros((T, ROPE // 2), F32), jnp.ones((T, HP - QK), F32), jnp.zeros((T, HP - QK), F32)
    t_cos = jnp.concatenate([one, cs, cs, tail1], axis=1)
    t_s1 = jnp.concatenate([zero, z16, sn, tail0], axis=1)
    t_s2 = jnp.concatenate([zero, -sn, z16, tail0], axis=1)

    w_in = W["w_in"]
    w_in_re = jnp.concatenate([w_in[:, 800:], w_in[:, :768], w_in[:, 768:800],
                               jnp.zeros((D_MODEL, HP - ROPE), w_in.dtype)], axis=1)
    w_uq_p = _pad_heads(W["w_uq"], QK)
    wkv = W["w_ukv"].reshape(S5_W, HEADS, NOPE + VD)
    w_uk_p = _pad_heads(wkv[:, :, :NOPE].reshape(S5_W, HEADS * NOPE), NOPE)
    w_uv_p = _pad_heads(wkv[:, :, NOPE:].reshape(S5_W, HEADS * VD), VD)
    w_oa_p = jnp.pad(W["w_oa"].reshape(HEADS, VD, D_MODEL), ((0, 0), (0, HP - VD), (0, 0))).reshape(HEADS * HP, D_MODEL)

    g_in, b_in = row(sp["ln_in_g"]), row(sp["ln_in_b"])
    g1, b1, g2, b2, g3, b3 = (row(sp[k]) for k in ("ln1_g", "ln1_b", "ln2_g", "ln2_b", "ln3_g", "ln3_b"))
    gq, gkv, d_skip = row(sp["q_norm_g"]), row(sp["kv_norm_g"]), row(sp["s5_d"])

    (h0,) = _rowwise(lambda a, g, b: (_ln(a, g, b),), "ln_in_fwd", [_whole(x)], [g_in, b_in], [(D_MODEL, F32)], tm=512)
    z = _mm(h0, w_in_re, "in_proj", tm=512, tn=Z_W, out_dtype=BF16)
    hs, yc, yg = _s5_fwd(z, b_st, c_st, d_skip, tab_fwd, "s5_fwd")
    y2 = _mm(yg, W["w_glu"], "glu_proj", tn=2048, out_dtype=BF16)
    cqn, ckvn = _rowwise(lambda a, b, ga, gb: (_rms(a, ga), _rms(b, gb)), "mla_norm_fwd",
                         [(z, Z_CQ, S5_W), (z, Z_CKV, S5_W)], [gq, gkv], [(S5_W, BF16)] * 2)
    q_raw = _mm(cqn, w_uq_p, "mla_uq")
    k_raw = _mm(ckvn, w_uk_p, "mla_uk")
    v = _mm(ckvn, w_uv_p, "mla_uv", out_dtype=BF16)

    def rope_fwd(qr, kr_raw, krz, tc_, t1, t2):
        q = _rope(qr, _tile8(tc_), _tile8(t1), _tile8(t2)) * (ATT_SCALE * LOG2E)
        kr = _rope(pltpu.roll(krz, NOPE, 1), tc_, t1, t2)
        return q, kr_raw + _tile8(kr)

    q, k = _rowwise(rope_fwd, "mla_rope_fwd", [_whole(q_raw), _whole(k_raw), (z, Z_KR, HP), _whole(t_cos), _whole(t_s1), _whole(t_s2)],
                    [], [(HEADS * HP, BF16)] * 2)
    if overlap is None:
        o, lse = _flash_fwd(q, k, v, "mla_attn_fwd")
    else:
        o, lse, landed = _flash_fwd(q, k, v, "mla_attn_fwd", carry=(overlap[0], False))
        W = {**W, **overlap[1](landed)}
    a_out = _mm(o, w_oa_p, "mla_oa", out_dtype=BF16)
    (mixin,) = _rowwise(lambda *a: (_gate_mix(*a),), "gate_mix_fwd",
                        [(z, Z_GS, D_MODEL), (z, Z_GA, D_MODEL), (y2, 0, D_MODEL), (y2, D_MODEL, D_MODEL), _whole(a_out)],
                        [], [(D_MODEL, BF16)])
    def ln_epi(acc, h, g, b):
        s_ = DN_ALPHA * h + acc
        return s_, _ln(s_, g, b)

    two_rows = [(D_MODEL, F32)] * 2
    s1, h1 = _mm(mixin, W["w_o"], "mix_o_ln1", epi=ln_epi, extras=(h0,), pextras=(g1, b1), outs=two_rows, tm=512)
    xq = _mm(h1, W["w_xq"], "x_q", out_dtype=BF16)
    xk = _mm(mem, W["w_xk"], "x_k", out_dtype=BF16)
    xv = _mm(mem, W["w_xv"], "x_v", out_dtype=BF16)
    (xo,) = _rowwise(_xattn_fwd_fn, "xattn_fwd", [_whole(xq)], [xk, xv], [(D_MODEL, BF16)], tm=1024, upcast=False)
    s2, h2 = _mm(xo, W["w_xo"], "x_o_ln2", epi=ln_epi, extras=(h1,), pextras=(g2, b2), outs=two_rows, tm=512)
    a_up = _mm(h2, W["w_up"], "mlp_up", out_dtype=BF16)
    ff = _mm(a_up, W["w_down"], "mlp_down", act=_relu2)

    def loss_fn(h, r, tgt, g, b):
        def f(r_, g_, b_):
            e = _ln_res(h, r_, g_, b_) - tgt
            return 0.5 * jnp.sum(jnp.mean(e * e, axis=-1))
        lv, (dr_, dg_, db_) = jax.value_and_grad(f, argnums=(0, 1, 2))(r, g, b)
        return dr_, dg_, db_, jnp.broadcast_to(lv, (1, 128))

    dr3, dg3, db3, lossv = _rowwise(loss_fn, "ln3_loss_bwd", [_whole(h2), _whole(ff), _whole(target)], [g3, b3],
                                    [(D_MODEL, F32)], [(1, D_MODEL), (1, D_MODEL), (1, 128)], tm=512)
    gW, gs = {}, {"ln3_g": dg3, "ln3_b": db3}

    da = _mm(dr3, W["w_down"], "mlp_down_bwd_a", "nt", epi=lambda acc, a: acc * (2.0 * jnp.maximum(a, 0.0)), extras=(a_up,),
             out_dtype=BF16)
    gW["w_down"] = _mm(a_up, dr3, "mlp_down_bwd_w", "tn", m=4 * D_MODEL, n=D_MODEL, act=_relu2)
    gW["w_up"] = _mm(h2, da, "mlp_up_bwd_w", "tn", m=D_MODEL, n=4 * D_MODEL)
    dh2 = _mm(da, W["w_up"], "mlp_up_bwd_a", "nt", epi=lambda acc, e: acc + DN_ALPHA * e, extras=(dr3,))

    def ln_bwd(a, dy, g, b):
        _, vjp = jax.vjp(_ln, a, g, b)
        return vjp(dy)

    dr2, gs["ln2_g"], gs["ln2_b"] = _rowwise(ln_bwd, "ln2_bwd", [_whole(s2), _whole(dh2)], [g2, b2],
                                             [(D_MODEL, F32)], [(1, D_MODEL)] * 2, tm=512)
    dxo = _mm(dr2, W["w_xo"], "x_o_bwd_a", "nt", out_dtype=BF16)
    gW["w_xo"] = _mm(xo, dr2, "x_o_bwd_w", "tn", m=D_MODEL, n=D_MODEL)
    dxq, dxk, dxv = _rowwise(_xattn_bwd_fn, "xattn_bwd", [_whole(xq), _whole(dxo)], [xk, xv], [(D_MODEL, BF16)],
                             [(xk.shape[0], D_MODEL)] * 2, tm=1024, upcast=False)
    gW["w_xq"] = _mm(h1, dxq, "x_q_bwd_w", "tn", m=D_MODEL, n=D_MODEL)
    gW["w_xk"] = _mm(mem, dxk, "x_k_bwd_w", "tn", m=D_MODEL, n=D_MODEL)
    gW["w_xv"] = _mm(mem, dxv, "x_v_bwd_w", "tn", m=D_MODEL, n=D_MODEL)
    dh1 = _mm(dxq, W["w_xq"], "x_q_bwd_a", "nt", epi=lambda acc, e: acc + DN_ALPHA * e, extras=(dr2,))

    dr1, gs["ln1_g"], gs["ln1_b"] = _rowwise(ln_bwd, "ln1_bwd", [_whole(s1), _whole(dh1)], [g1, b1],
                                             [(D_MODEL, F32)], [(1, D_MODEL)] * 2, tm=512)
    dmixin = _mm(dr1, W["w_o"], "mix_o_bwd_a", "nt")
    gW["w_o"] = _mm(mixin, dr1, "mix_o_bwd_w", "tn", m=D_MODEL, n=D_MODEL)

    def gate_mix_bwd(gs_, ga_, ya, yb, ao, dy):
        _, vjp = jax.vjp(_gate_mix, gs_, ga_, ya, yb, ao)
        return vjp(dy)

    dgs, dga, dy2a, dy2b, da_out = _rowwise(
        gate_mix_bwd, "gate_mix_bwd",
        [(z, Z_GS, D_MODEL), (z, Z_GA, D_MODEL), (y2, 0, D_MODEL), (y2, D_MODEL, D_MODEL), _whole(a_out), _whole(dmixin)],
        [], [(D_MODEL, BF16)] * 5)

    def delta_epi(acc, o_):
        prod = acc * o_
        cols = [jnp.sum(prod[:, h * HP:(h + 1) * HP], axis=1, keepdims=True) for h in range(HEADS)]
        return acc, jnp.concatenate(cols, axis=1)

    do16, delta = _mm(da_out, w_oa_p, "mla_oa_bwd_a", "nt", epi=delta_epi, extras=(o,),
                      outs=[(HEADS * HP, BF16), (HEADS, F32)], tm=512)
    g_oa_p = _mm(o, da_out, "mla_oa_bwd_w", "tn", m=HEADS * HP, n=D_MODEL)
    gW["w_oa"] = g_oa_p.reshape(HEADS, HP, D_MODEL)[:, :VD].reshape(HEADS * VD, D_MODEL)
    delta_row = delta.T.reshape(HEADS, 1, T)
    if overlap is None:
        dq, dk, dv = _flash_bwd(q, k, v, do16, lse, delta_row, "mla_attn_bwd")
        riding = None
    else:
        sent = overlap[2]({n: gW[n] for n in BIG_LATE})
        dq, dk, dv, landed = _flash_bwd(q, k, v, do16, lse, delta_row, "mla_attn_bwd", carry=(sent, True))
        riding = (sent, landed)

    def rope_bwd(dq_, dk_, tc_, t1, t2):
        dqr = _rope_t(dq_, _tile8(tc_), _tile8(t1), _tile8(t2))
        dkr = dk_[:, 0:HP]
        for hh in range(1, HEADS):
            dkr = dkr + dk_[:, hh * HP:(hh + 1) * HP]
        dkr = _rope_t(jnp.where(_rope_lanes(dkr.shape), dkr, 0.0), tc_, t1, t2)
        dkr = pltpu.roll(dkr, NOPE, 1)
        lane = lax.broadcasted_iota(jnp.int32, dkr.shape, 1)
        return dqr, jnp.where(lane < ROPE, dkr, 0.0)

    dq_raw, dkr = _rowwise(rope_bwd, "mla_rope_bwd", [_whole(dq), _whole(dk), _whole(t_cos), _whole(t_s1), _whole(t_s2)], [],
                           [(HEADS * HP, BF16), (HP, BF16)])
    dcqn = _mm(dq_raw, w_uq_p, "mla_uq_bwd_a", "nt")
    gW["w_uq"] = _unpad_heads(_mm(cqn, dq_raw, "mla_uq_bwd_w", "tn", m=S5_W, n=HEADS * HP), QK)
    dckvn_k = _mm(dk, w_uk_p, "mla_uk_bwd_a", "nt")
    dckvn = _mm(dv, w_uv_p, "mla_uv_bwd_a", "nt", epi=lambda acc, e: acc + e, extras=(dckvn_k,))
    g_uk = _unpad_heads(_mm(ckvn, dk, "mla_uk_bwd_w", "tn", m=S5_W, n=HEADS * HP), NOPE).reshape(S5_W, HEADS, NOPE)
    g_uv = _unpad_heads(_mm(ckvn, dv, "mla_uv_bwd_w", "tn", m=S5_W, n=HEADS * HP), VD).reshape(S5_W, HEADS, VD)
    gW["w_ukv"] = jnp.concatenate([g_uk, g_uv], axis=2).reshape(S5_W, HEADS * (NOPE + VD))

    def norm_bwd(a, b, da_, db_, ga, gb):
        _, vja = jax.vjp(_rms, a, ga)
        _, vjb = jax.vjp(_rms, b, gb)
        dxa, dga_ = vja(da_)
        dxb, dgb_ = vjb(db_)
        return dxa, dxb, dga_, dgb_

    dcq, dckv, gs["q_norm_g"], gs["kv_norm_g"] = _rowwise(
        norm_bwd, "mla_norm_bwd", [(z, Z_CQ, S5_W), (z, Z_CKV, S5_W), _whole(dcqn), _whole(dckvn)], [gq, gkv],
        [(S5_W, BF16)] * 2, [(1, S5_W)] * 2)

    dy2 = jnp.concatenate([dy2a, dy2b], axis=1)
    dyg = _mm(dy2, W["w_glu"], "glu_bwd_a", "nt")
    gW["w_glu"] = _mm(yg, dy2, "glu_bwd_w", "tn", m=S5_W, n=2 * D_MODEL)

    du, gs["s5_ab"], g_bst, g_cst_t, gs["s5_d"] = _s5_bwd(dyg, yc, z, hs, b_st, c_st, d_skip, tab_rev, "s5_bwd")
    gs["s5_bb_re"] = jnp.swapaxes(_blockdiag_t(g_bst[:, :S5_N], S5_H, S5_P), 1, 2)
    gs["s5_bb_im"] = jnp.swapaxes(_blockdiag_t(g_bst[:, S5_N:], S5_H, S5_P), 1, 2)
    gs["s5_c_re"] = _blockdiag_t(g_cst_t[:, :S5_N], S5_H, S5_P)
    gs["s5_c_im"] = -_blockdiag_t(g_cst_t[:, S5_N:], S5_H, S5_P)

    dz = jnp.concatenate([dgs, dga, du, dcq, dckv, dkr], axis=1)
    g_in_re = _mm(h0, dz, "in_proj_bwd_w", "tn", m=D_MODEL, n=Z_W, tm=512, tn=Z_W)
    gW["w_in"] = jnp.concatenate([g_in_re[:, Z_U:Z_KR], g_in_re[:, Z_KR:Z_KR + ROPE], g_in_re[:, :Z_U]], axis=1)
    dh0 = _mm(dz, w_in_re, "in_proj_bwd_a", "nt", tm=512, tk=Z_W, epi=lambda acc, e: acc + DN_ALPHA * e, extras=(dr1,))

    dx, gs["ln_in_g"], gs["ln_in_b"] = _rowwise(ln_bwd, "ln_in_bwd", [_whole(x), _whole(dh0)], [g_in, b_in],
                                                [(D_MODEL, F32)], [(1, D_MODEL)] * 2, tm=512)
    return lossv[:, :1], dx, gW, gs, riding


_RAW_SMALL = (("loss", (1, 1)), ("ln_in_g", (1, D_MODEL)), ("ln_in_b", (1, D_MODEL)), ("ln1_g", (1, D_MODEL)),
              ("ln1_b", (1, D_MODEL)), ("ln2_g", (1, D_MODEL)), ("ln2_b", (1, D_MODEL)), ("ln3_g", (1, D_MODEL)),
              ("ln3_b", (1, D_MODEL)), ("q_norm_g", (1, S5_W)), ("kv_norm_g", (1, S5_W)), ("s5_d", (1, S5_W)),
              ("s5_ab", (1, 2 * S5_N)), ("s5_bb_re", (S5_G, S5_P, S5_H)), ("s5_bb_im", (S5_G, S5_P, S5_H)),
              ("s5_c_re", (S5_G, S5_H, S5_P)), ("s5_c_im", (S5_G, S5_H, S5_P)))


def kernel(x, mem, positions, ln_in_g, ln_in_b, w_in, s5_lam_re, s5_lam_im, s5_log_dt, s5_b_re, s5_b_im, s5_c_re, s5_c_im, s5_d, w_glu, q_norm_g, w_uq, kv_norm_g, w_ukv, w_oa, w_o, ln1_g, ln1_b, w_xq, w_xk, w_xv, w_xo, ln2_g, ln2_b, w_up, w_down, ln3_g, ln3_b, loss_target, m_ln_in_g, m_ln_in_b, m_w_in, m_s5_lam_re, m_s5_lam_im, m_s5_log_dt, m_s5_b_re, m_s5_b_im, m_s5_c_re, m_s5_c_im, m_s5_d, m_w_glu, m_q_norm_g, m_w_uq, m_kv_norm_g, m_w_ukv, m_w_oa, m_w_o, m_ln1_g, m_ln1_b, m_w_xq, m_w_xk, m_w_xv, m_w_xo, m_ln2_g, m_ln2_b, m_w_up, m_w_down, m_ln3_g, m_ln3_b, v_ln_in_g, v_ln_in_b, v_w_in, v_s5_lam_re, v_s5_lam_im, v_s5_log_dt, v_s5_b_re, v_s5_b_im, v_s5_c_re, v_s5_c_im, v_s5_d, v_w_glu, v_q_norm_g, v_w_uq, v_kv_norm_g, v_w_ukv, v_w_oa, v_w_o, v_ln1_g, v_ln1_b, v_w_xq, v_w_xk, v_w_xv, v_w_xo, v_ln2_g, v_ln2_b, v_w_up, v_w_down, v_ln3_g, v_ln3_b):
    a = dict(locals())
    wts = {n: a[n] for n in WEIGHTS}
    ms = {n: a["m_" + n] for n in WEIGHTS}
    vs = {n: a["v_" + n] for n in WEIGHTS}
    shard2d = {n: wts[n].reshape(wts[n].shape[-2], wts[n].shape[-1]) for n in BIG}
    nrow = {n: shard2d[n].size // 1024 for n in BIG}
    mine = 2 * lax.axis_index("x") + lax.axis_index("y")
    my_c = lax.axis_index("c")
    cidx = my_c.astype(jnp.int32).reshape(1)

    def group_rows(group):
        rows = sum(nrow[n] for n in group)
        return rows, (-rows) % PACK_ROWS

    def pack_shards(group):
        rows, pad = group_rows(group)
        parts = [shard2d[n].astype(BF16).reshape(nrow[n], 1024) for n in group] + [jnp.zeros((pad, 1024), BF16)]
        return jnp.concatenate(parts, axis=0).reshape(2, (rows + pad) // 2, 1024)

    def unpack_full(group, gathered):
        out, off = {}, 0
        for n in group:
            r, cc = shard2d[n].shape
            piece = lax.optimization_barrier(gathered[:, off:off + nrow[n]])
            out[n] = _full_from_shards(n, piece.reshape(4, r, cc))
            off += nrow[n]
        return out

    def pack_grads(group, gw):
        rows, pad = group_rows(group)
        parts = [_shards_from_full(n, gw[n]).reshape(4, nrow[n], 1024) for n in group] + [jnp.zeros((4, pad, 1024), F32)]
        return jnp.concatenate(parts, axis=1).reshape(4, 2, (rows + pad) // 2, 1024)

    def by_core(own, other):
        return jnp.where(my_c == 0, jnp.stack([own, other]), jnp.stack([other, own]))

    packed_e, packed_l = pack_shards(BIG_EARLY), pack_shards(BIG_LATE)
    gathered = lax.dynamic_update_slice(_gather_weights(packed_e), packed_e[None], (mine, 0, 0, 0))
    W = unpack_full(BIG_EARLY, gathered.reshape(4, -1, 1024))

    def finish_gather(landed):
        half = packed_l.shape[1]
        own_half = lax.dynamic_index_in_dim(packed_l, my_c, axis=0, keepdims=True)
        landed = lax.dynamic_update_slice(landed, own_half, (mine, 0, 0))
        other = _pair_share(landed.reshape(4 * half, 1024), "gather_late_share").reshape(4, half, 1024)
        both = jnp.moveaxis(by_core(landed, other), 0, 1)
        return unpack_full(BIG_LATE, both.reshape(4, 2 * half, 1024))

    def pair_reduce(group, gw, tag):
        gflat = pack_grads(group, gw)
        return _pair_add(gflat, _pair_exchange(gflat, "grad_pair_exchange_" + tag), cidx, "grad_pair_add_" + tag)

    def finish_reduce(group, sent, landed, tag):
        own = lax.dynamic_slice(sent, (mine, 0, 0), (1,) + sent.shape[1:])
        red = _chip_sum(lax.dynamic_update_slice(landed, own, (mine, 0, 0)), "grad_chip_sum_" + tag)
        gsh = by_core(red, _pair_share(red, "grad_pair_share_" + tag)).reshape(-1, 1024)
        out, off = {}, 0
        for n in group:
            out[n] = lax.optimization_barrier(gsh[off:off + nrow[n]]).reshape(shard2d[n].shape)
            off += nrow[n]
        return out

    sp = {n: wts[n] for n in SMALL}
    sp_local = {n: (sp[n][0] if sp[n].ndim > 1 else sp[n]) for n in SMALL}
    overlap = (packed_l, finish_gather, lambda gw: pair_reduce(BIG_LATE, gw, "late"))
    lossv, dx, gW, gs, riding = _local_step(x[0], mem[0], positions[0], loss_target[0], W, sp_local, overlap)

    g_out = finish_reduce(BIG_LATE, riding[0], riding[1], "late")
    sent_e = pair_reduce(BIG_EARLY, gW, "early")
    g_out.update(finish_reduce(BIG_EARLY, sent_e, _chip_exchange(sent_e, "grad_chip_exchange_early"), "early"))

    gs["loss"] = lossv
    raw = _pack_flat([gs[n].reshape(s) for n, s in _RAW_SMALL], 8 * 1024, F32)
    raw = _allreduce_small(raw.reshape(-1, 1024)).reshape(-1)
    rs = dict(zip([n for n, _ in _RAW_SMALL], _unpack_flat(raw, [s for _, s in _RAW_SMALL])))
    loss = rs["loss"].reshape(())
    _, disc_vjp = jax.vjp(_s5_discretize, sp_local["s5_lam_re"], sp_local["s5_lam_im"], sp_local["s5_log_dt"],
                          sp_local["s5_b_re"], sp_local["s5_b_im"])
    d_ab = rs["s5_ab"].reshape(2, S5_G, S5_P)
    g_lre, g_lim, g_ldt, g_bre, g_bim = disc_vjp((d_ab[0], d_ab[1], rs["s5_bb_re"], rs["s5_bb_im"]))
    small_g = {"s5_lam_re": g_lre, "s5_lam_im": g_lim, "s5_log_dt": g_ldt, "s5_b_re": g_bre, "s5_b_im": g_bim,
               "s5_c_re": rs["s5_c_re"], "s5_c_im": rs["s5_c_im"]}
    for n in ("ln_in_g", "ln_in_b", "ln1_g", "ln1_b", "ln2_g", "ln2_b", "ln3_g", "ln3_b", "q_norm_g", "kv_norm_g", "s5_d"):
        small_g[n] = rs[n]

    grads, deltas, new_m, new_v = {}, {}, {}, {}
    for n in BIG:
        d_, m_, v_ = _adamw(shard2d[n], g_out[n], ms[n].reshape(shard2d[n].shape), vs[n].reshape(shard2d[n].shape), "adamw_" + n)
        grads[n] = g_out[n].reshape(wts[n].shape)
        deltas[n], new_m[n], new_v[n] = (t.reshape(wts[n].shape) for t in (d_, m_, v_))
    as2d = lambda t: t.reshape(-1, t.shape[-1])
    sd, sm, sv = _adamw_many([as2d(wts[n]) for n in SMALL], [as2d(small_g[n].reshape(wts[n].shape)) for n in SMALL],
                             [as2d(ms[n]) for n in SMALL], [as2d(vs[n]) for n in SMALL], "adamw_small")
    for n, d_, m_, v_ in zip(SMALL, sd, sm, sv):
        grads[n] = small_g[n].reshape(wts[n].shape)
        deltas[n], new_m[n], new_v[n] = (t.reshape(wts[n].shape) for t in (d_, m_, v_))

    return (loss, dx[None], *[grads[n] for n in WEIGHTS], *[deltas[n] for n in WEIGHTS],
            *[new_m[n] for n in WEIGHTS], *[new_v[n] for n in WEIGHTS])
```

```python
import functools
import math

import jax
import jax.numpy as jnp
from jax import lax
from jax.experimental import pallas as pl
from jax.experimental.pallas import tpu as pltpu

F32 = jnp.float32
BF16 = jnp.bfloat16
MESH = pl.DeviceIdType.MESH

D_MODEL = 1024
S5_W = 256
S5_G = 16
S5_H = 16
S5_P = 64
S5_N = S5_G * S5_P
S5_MAX_RE = -1e-4
HEADS = 8
NOPE = 64
ROPE = 32
QK = NOPE + ROPE
VD = 64
HP = 128
XH = 4
XD = 256
LN_EPS = 1e-5
RMS_EPS = 1e-6
NEG_INF = -1e30
DN_ALPHA = 2.0 ** 0.25
ROPE_THETA = 10000.0
ADAM_LR, ADAM_B1, ADAM_B2, ADAM_EPS, ADAM_WD, ADAM_STEP = 0.001, 0.9, 0.999, 1e-08, 0.01, 10

Z_GS, Z_GA, Z_U, Z_CQ, Z_CKV, Z_KR, Z_W = 0, 1024, 2048, 2304, 2560, 2816, 2944
SCAN_CHUNK = 256

BIG_EARLY = ("w_in", "w_glu", "w_uq", "w_ukv", "w_oa", "w_o")
BIG_LATE = ("w_up", "w_xq", "w_xk", "w_xv", "w_xo", "w_down")
BIG = BIG_EARLY + BIG_LATE
COL_SHARDED = ("w_in", "w_glu", "w_uq", "w_ukv", "w_oa", "w_up")
SMALL = ("ln_in_g", "ln_in_b", "s5_lam_re", "s5_lam_im", "s5_log_dt", "s5_b_re", "s5_b_im", "s5_c_re", "s5_c_im",
         "s5_d", "q_norm_g", "kv_norm_g", "ln1_g", "ln1_b", "ln2_g", "ln2_b", "ln3_g", "ln3_b")
WEIGHTS = ("ln_in_g", "ln_in_b", "w_in", "s5_lam_re", "s5_lam_im", "s5_log_dt", "s5_b_re", "s5_b_im", "s5_c_re",
           "s5_c_im", "s5_d", "w_glu", "q_norm_g", "w_uq", "kv_norm_g", "w_ukv", "w_oa", "w_o", "ln1_g", "ln1_b",
           "w_xq", "w_xk", "w_xv", "w_xo", "ln2_g", "ln2_b", "w_up", "w_down", "ln3_g", "ln3_b")
PACK_ROWS = 2 * 16


def _row_tile(n, cap=512):
    best = n
    for t in range(8, min(n, cap) + 1, 8):
        if n % t == 0:
            best = t
    return best


def _pick(n, cap):
    best = None
    for t in range(128, min(n, cap) + 1, 128):
        if n % t == 0:
            best = t
    return best if best is not None and (best >= 512 or best == n) else n


def _mm(a, b, name, mode="nn", *, a_off=0, b_off=0, m=None, n=None, act=None, epi=None, extras=(), rextras=(),
        pextras=(), outs=None, out_dtype=F32, tm=1024, tn=1024, tk=None):
    if mode == "nn":
        M, (K, N) = a.shape[0], b.shape
    elif mode == "nt":
        M, (N, K) = a.shape[0], b.shape
    else:
        K, M, N = a.shape[0], m, n
    if mode == "tn":
        tm, tn, tk = _pick(M, tm), _pick(N, tn), min(tk or 1024, K)
    else:
        tm, tn, tk = min(tm, M), _pick(N, tn), _pick(K, tk or 1024)
    assert M % tm == 0 and N % tn == 0 and K % tk == 0, (name, M, N, K, tm, tn, tk)
    nk = K // tk
    if mode == "tn":
        assert a_off % tm == 0 and b_off % tn == 0
        ao, bo = a_off // tm, b_off // tn
        a_spec = pl.BlockSpec((tk, tm), lambda i, j, k: (k, i + ao))
        b_spec = pl.BlockSpec((tk, tn), lambda i, j, k: (k, j + bo))
        dims = (((0,), (0,)), ((), ()))
    else:
        assert a_off % tk == 0
        ao = a_off // tk
        a_spec = pl.BlockSpec((tm, tk), lambda i, j, k: (i, k + ao))
        if mode == "nn":
            b_spec = pl.BlockSpec((tk, tn), lambda i, j, k: (k, j))
            dims = (((1,), (0,)), ((), ()))
        else:
            b_spec = pl.BlockSpec((tn, tk), lambda i, j, k: (j, k))
            dims = (((1,), (1,)), ((), ()))
    e_spec = pl.BlockSpec((tm, tn), lambda i, j, k: (i, j))
    p_spec = pl.BlockSpec((1, tn), lambda i, j, k: (0, j))
    r_specs = [pl.BlockSpec((tm, w), functools.partial(lambda i, j, k, o: (i, o), o=off // w)) for _, off, w in rextras]
    n_extra = len(extras) + len(rextras) + len(pextras)
    if outs is None:
        o_specs, o_shapes = [e_spec], [jax.ShapeDtypeStruct((M, N), out_dtype)]
    else:
        assert all(w == tn or N == tn for w, _ in outs), name
        o_specs = [e_spec if w == tn else pl.BlockSpec((tm, w), lambda i, j, k: (i, 0)) for w, _ in outs]
        o_shapes = [jax.ShapeDtypeStruct((M, N if w == tn else w), dt) for w, dt in outs]
    n_out = len(o_specs)
    direct = outs is None and epi is None and out_dtype == F32

    def body(*refs):
        a_ref, b_ref = refs[0], refs[1]
        e_refs = refs[2:2 + n_extra]
        o_refs = refs[2 + n_extra:2 + n_extra + n_out]
        av = a_ref[...]
        if act is not None:
            av = act(av.astype(F32))
        p = lax.dot_general(av.astype(BF16), b_ref[...].astype(BF16), dims, preferred_element_type=F32)

        def finish(r):
            if epi is not None:
                r = epi(r, *[e[...] for e in e_refs])
            for o_ref, v in zip(o_refs, r if isinstance(r, tuple) else (r,)):
                o_ref[...] = v.astype(o_ref.dtype)

        if nk == 1:
            finish(p)
        else:
            acc = o_refs[0] if direct else refs[2 + n_extra + n_out]
            k = pl.program_id(2)

            @pl.when(k == 0)
            def _():
                acc[...] = p

            @pl.when(k > 0)
            def _():
                acc[...] += p

            if not direct:
                @pl.when(k == nk - 1)
                def _():
                    finish(acc[...])

    res = pl.pallas_call(
        body, name=name, grid=(M // tm, N // tn, nk),
        in_specs=[a_spec, b_spec] + [e_spec] * len(extras) + r_specs + [p_spec] * len(pextras), out_specs=o_specs,
        out_shape=o_shapes,
        scratch_shapes=[pltpu.VMEM((tm, tn), F32)] if nk > 1 and not direct else [],
        compiler_params=pltpu.CompilerParams(dimension_semantics=("parallel", "parallel", "arbitrary")),
    )(a, b, *extras, *[r[0] for r in rextras], *pextras)
    return res[0] if outs is None else res


def _rowwise(fn, name, rows, pars, outs, accs=(), tm=256, upcast=True):
    T = rows[0][0].shape[0]
    tm = min(tm, T)
    assert T % tm == 0
    n_in, n_out = len(rows) + len(pars), len(outs)
    in_specs = []
    for arr, off, w in rows:
        assert off % w == 0 and arr.shape[0] == T, name
        in_specs.append(pl.BlockSpec((tm, w), functools.partial(lambda i, o: (i, o), o=off // w)))
    for p in pars:
        in_specs.append(pl.BlockSpec(p.shape, lambda i: (0, 0)))
    out_specs = [pl.BlockSpec((tm, w), lambda i: (i, 0)) for w, _ in outs]
    out_specs += [pl.BlockSpec(s, lambda i: (0, 0)) for s in accs]
    out_shape = [jax.ShapeDtypeStruct((T, w), dt) for w, dt in outs] + [jax.ShapeDtypeStruct(s, F32) for s in accs]

    def body(*refs):
        res = fn(*[r[...].astype(F32) if upcast else r[...] for r in refs[:n_in]])
        o_refs = refs[n_in:]
        for r, v in zip(o_refs[:n_out], res[:n_out]):
            r[...] = v.astype(r.dtype)
        if accs:
            i = pl.program_id(0)

            @pl.when(i == 0)
            def _():
                for r in o_refs[n_out:]:
                    r[...] = jnp.zeros_like(r)

            for r, v in zip(o_refs[n_out:], res[n_out:]):
                r[...] += v

    res = pl.pallas_call(
        body, name=name, grid=(T // tm,), in_specs=in_specs, out_specs=out_specs, out_shape=out_shape,
        compiler_params=pltpu.CompilerParams(dimension_semantics=("arbitrary",)),
    )(*[r[0] for r in rows], *pars)
    return res


def _whole(arr):
    return (arr, 0, arr.shape[1])


def _ln(x, g, b):
    mu = jnp.mean(x, axis=-1, keepdims=True)
    xc = x - mu
    var = jnp.mean(xc * xc, axis=-1, keepdims=True)
    return xc * lax.rsqrt(var + LN_EPS) * g + b


def _ln_res(h, r, g, b):
    return _ln(DN_ALPHA * h + r, g, b)


def _rms(x, g):
    return x * lax.rsqrt(jnp.mean(x * x, axis=-1, keepdims=True) + RMS_EPS) * g


def _gelu_skip(yc, u, d):
    y = yc + d * u
    return 0.5 * y * (1.0 + lax.erf(y * (1.0 / math.sqrt(2.0))))


def _gate_mix(gs, ga, y2a, y2b, aout):
    return jax.nn.sigmoid(gs) * (y2a * jax.nn.sigmoid(y2b)) + jax.nn.sigmoid(ga) * aout


def _relu2(a):
    r = jnp.maximum(a, 0.0)
    return r * r


def _tile8(t):
    return jnp.concatenate([t] * HEADS, axis=1)


def _rope(x, cos, s1, s2):
    w = x.shape[1]
    return x * cos + pltpu.roll(x, ROPE // 2, 1) * s1 + pltpu.roll(x, w - ROPE // 2, 1) * s2


def _rope_t(dy, cos, s1, s2):
    w = dy.shape[1]
    return dy * cos + pltpu.roll(dy * s1, w - ROPE // 2, 1) + pltpu.roll(dy * s2, ROPE // 2, 1)


def _rope_lanes(shape):
    lane = lax.broadcasted_iota(jnp.int32, shape, 1) % HP
    return (lane >= NOPE) & (lane < QK)


ATT_HB_FWD, ATT_HB_BWD = 8, 4
ATT_SCALE = QK ** -0.5
LOG2E = 1.4426950408889634
LN2 = 0.6931471805599453
_NT = (((1,), (1,)), ((), ()))
_TN = (((0,), (0,)), ((), ()))


def _carry_specs(carry):
    if carry is None:
        return [], [], [], []
    arr = carry[0]
    return ([_HBM], [_HBM], [jax.ShapeDtypeStruct((4,) + arr.shape[1:], arr.dtype)],
            [pltpu.SemaphoreType.DMA((3,)), pltpu.SemaphoreType.DMA((3,))])


def _carry_start(exchange, when):
    @pl.when(when)
    def _():
        for cp in exchange(False):
            cp.start()


def _carry_wait(exchange, when):
    @pl.when(when)
    def _():
        for cp in exchange(True):
            cp.wait_recv()
        for cp in exchange(False):
            cp.wait_send()


def _tri_tables(n, by_row):
    if by_row:
        pairs = [(i, j) for i in range(n) for j in range(i + 1)]
    else:
        pairs = [(i, j) for j in range(n) for i in range(j, n)]
    return jnp.array([p[0] for p in pairs], jnp.int32), jnp.array([p[1] for p in pairs], jnp.int32)


def _flash_fwd(q, k, v, name, bq=512, carry=None):
    T = q.shape[0]
    bq = min(bq, T)
    nq = T // bq
    hb = ATT_HB_FWD
    n_carry = 0 if carry is None else 1

    i_tab, j_tab = _tri_tables(nq, by_row=True)

    def body(it_ref, jt_ref, q_ref, k_ref, v_ref, *rest):
        o_ref, lse_ref = rest[n_carry:n_carry + 2]
        m_sc, l_sc, acc_sc = rest[2 * n_carry + 2:2 * n_carry + 5]
        t = pl.program_id(1)
        i, j = it_ref[t], jt_ref[t]
        if carry is not None:
            exchange = functools.partial(_carried_exchange, rest[0], rest[3], rest[-2], rest[-1], carry[1])
            _carry_start(exchange, (pl.program_id(0) == 0) & (t == 0))

        @pl.when(j == 0)
        def _():
            m_sc[...] = jnp.full_like(m_sc, NEG_INF)
            l_sc[...] = jnp.zeros_like(l_sc)
            acc_sc[...] = jnp.zeros_like(acc_sc)

        def step(masked):
            for hh in range(hb):
                sl = slice(hh * HP, (hh + 1) * HP)
                s = lax.dot_general(q_ref[:, sl], k_ref[:, sl], _NT, preferred_element_type=F32)
                if masked:
                    row = lax.broadcasted_iota(jnp.int32, (bq, bq), 0)
                    col = lax.broadcasted_iota(jnp.int32, (bq, bq), 1)
                    s = jnp.where(col <= row, s, NEG_INF)
                m_prev = m_sc[hh]
                m_new = jnp.maximum(m_prev, jnp.max(s, axis=1, keepdims=True))
                alpha = jnp.exp2(m_prev - m_new)
                p = jnp.exp2(s - jnp.concatenate([m_new] * (bq // HP), axis=1))
                l_sc[hh] = alpha * l_sc[hh] + jnp.sum(p, axis=1, keepdims=True)
                acc_sc[hh] = alpha * acc_sc[hh] + jnp.dot(p.astype(BF16), v_ref[:, sl], preferred_element_type=F32)
                m_sc[hh] = m_new

        @pl.when(j < i)
        def _():
            step(False)

        @pl.when(j == i)
        def _():
            step(True)
            for hh in range(hb):
                o_ref[:, hh * HP:(hh + 1) * HP] = acc_sc[hh] / l_sc[hh]
                lse_ref[hh] = jnp.transpose((m_sc[hh] + jnp.log2(l_sc[hh])) * LN2)[0:1, :]

        if carry is not None:
            _carry_wait(exchange, (pl.program_id(0) == HEADS // hb - 1) & (t == i_tab.shape[0] - 1))

    qs = pl.BlockSpec((bq, hb * HP), lambda h, t, it, jt: (it[t], h))
    ks = pl.BlockSpec((bq, hb * HP), lambda h, t, it, jt: (jt[t], h))
    c_in, c_out, c_shape, c_sems = _carry_specs(carry)
    gs = pltpu.PrefetchScalarGridSpec(
        num_scalar_prefetch=2, grid=(HEADS // hb, i_tab.shape[0]), in_specs=[qs, ks, ks] + c_in,
        out_specs=[qs, pl.BlockSpec((hb, 1, bq), lambda h, t, it, jt: (h, 0, it[t]))] + c_out,
        scratch_shapes=[pltpu.VMEM((hb, bq, HP), F32)] * 3 + c_sems)
    return pl.pallas_call(
        body, name=name, grid_spec=gs,
        out_shape=[jax.ShapeDtypeStruct((T, HEADS * HP), F32), jax.ShapeDtypeStruct((HEADS, 1, T), F32)] + c_shape,
        compiler_params=pltpu.CompilerParams(dimension_semantics=("arbitrary", "arbitrary")),
    )(i_tab, j_tab, q, k, v, *([] if carry is None else [carry[0]]))


def _flash_bwd(q, k, v, do, lse_row, delta_row, name, bq=512, carry=None):
    T = q.shape[0]
    bq = min(bq, T)
    nq = T // bq
    hb = ATT_HB_BWD
    n_carry = 0 if carry is None else 1

    i_tab, j_tab = _tri_tables(nq, by_row=False)
    n_blk = i_tab.shape[0]

    def body(it_ref, jt_ref, q_ref, k_ref, v_ref, do_ref, lse_ref, dl_ref, *rest):
        dq_ref, dk_ref, dv_ref = rest[n_carry:n_carry + 3]
        t = pl.program_id(1)
        i, j = it_ref[t], jt_ref[t]
        if carry is not None:
            exchange = functools.partial(_carried_exchange, rest[0], rest[4], rest[-2], rest[-1], carry[1])
            _carry_start(exchange, (pl.program_id(0) == 0) & (t == 0))

        @pl.when(t == 0)
        def _():
            dq_ref[...] = jnp.zeros_like(dq_ref)

        @pl.when(i == j)
        def _():
            dk_ref[...] = jnp.zeros_like(dk_ref)
            dv_ref[...] = jnp.zeros_like(dv_ref)

        def step(masked):
            rows = pl.ds(pl.multiple_of(i * bq, bq), bq)
            for hh in range(hb):
                sl = slice(hh * HP, (hh + 1) * HP)
                qh, kh, doh = q_ref[:, sl], k_ref[:, sl], do_ref[:, sl]
                st = lax.dot_general(kh, qh, _NT, preferred_element_type=F32)
                pt = jnp.exp2(st - lse_ref[hh] * LOG2E)
                if masked:
                    krow = lax.broadcasted_iota(jnp.int32, (bq, bq), 0)
                    qcol = lax.broadcasted_iota(jnp.int32, (bq, bq), 1)
                    pt = jnp.where(krow <= qcol, pt, 0.0)
                dv_ref[:, sl] += jnp.dot(pt.astype(BF16), doh, preferred_element_type=F32)
                dpt = lax.dot_general(v_ref[:, sl], doh, _NT, preferred_element_type=F32)
                dst = (pt * (dpt - dl_ref[hh])).astype(BF16)
                dk_ref[:, sl] += jnp.dot(dst, qh, preferred_element_type=F32)
                dq_ref[rows, sl] += lax.dot_general(dst, kh, _TN, preferred_element_type=F32)

        @pl.when(i > j)
        def _():
            step(False)

        @pl.when(i == j)
        def _():
            step(True)

        @pl.when(i == nq - 1)
        def _():
            dk_ref[...] *= LN2

        @pl.when(t == n_blk - 1)
        def _():
            dq_ref[...] *= ATT_SCALE

        if carry is not None:
            _carry_wait(exchange, (pl.program_id(0) == HEADS // hb - 1) & (t == n_blk - 1))

    qs = pl.BlockSpec((bq, hb * HP), lambda h, t, it, jt: (it[t], h))
    ks = pl.BlockSpec((bq, hb * HP), lambda h, t, it, jt: (jt[t], h))
    rs = pl.BlockSpec((hb, 1, bq), lambda h, t, it, jt: (h, 0, it[t]))
    full = pl.BlockSpec((T, hb * HP), lambda h, t, it, jt: (0, h))
    c_in, c_out, c_shape, c_sems = _carry_specs(carry)
    gs = pltpu.PrefetchScalarGridSpec(num_scalar_prefetch=2, grid=(HEADS // hb, n_blk),
                                      in_specs=[qs, ks, ks, qs, rs, rs] + c_in, out_specs=[full, ks, ks] + c_out,
                                      scratch_shapes=c_sems)
    return pl.pallas_call(
        body, name=name, grid_spec=gs, out_shape=[jax.ShapeDtypeStruct((T, HEADS * HP), F32)] * 3 + c_shape,
        compiler_params=pltpu.CompilerParams(dimension_semantics=("arbitrary", "arbitrary")),
    )(i_tab, j_tab, q, k, v, do, lse_row, delta_row, *([] if carry is None else [carry[0]]))


def _xattn_heads(q, k, v):
    scale = XD ** -0.5
    ps = []
    for h in range(XH):
        sl = slice(h * XD, (h + 1) * XD)
        s = lax.dot_general(q[:, sl].astype(BF16), k[:, sl].astype(BF16), (((1,), (1,)), ((), ())),
                            preferred_element_type=F32) * scale
        e = jnp.exp(s - jnp.max(s, axis=1, keepdims=True))
        ps.append(e / jnp.sum(e, axis=1, keepdims=True))
    return ps


def _xattn_fwd_fn(q, k, v):
    ps = _xattn_heads(q, k, v)
    o = [jnp.dot(p.astype(BF16), v[:, h * XD:(h + 1) * XD].astype(BF16), preferred_element_type=F32)
         for h, p in enumerate(ps)]
    return (jnp.concatenate(o, axis=1),)


def _xattn_bwd_fn(q, do, k, v):
    scale = XD ** -0.5
    ps = _xattn_heads(q, k, v)
    dqs, dks, dvs = [], [], []
    tdims = (((0,), (0,)), ((), ()))
    for h, p in enumerate(ps):
        sl = slice(h * XD, (h + 1) * XD)
        doh = do[:, sl].astype(BF16)
        dvs.append(lax.dot_general(p.astype(BF16), doh, tdims, preferred_element_type=F32))
        dp = lax.dot_general(doh, v[:, sl].astype(BF16), (((1,), (1,)), ((), ())), preferred_element_type=F32)
        ds = (p * (dp - jnp.sum(dp * p, axis=1, keepdims=True)) * scale).astype(BF16)
        dqs.append(jnp.dot(ds, k[:, sl].astype(BF16), preferred_element_type=F32))
        dks.append(lax.dot_general(ds, q[:, sl].astype(BF16), tdims, preferred_element_type=F32))
    return jnp.concatenate(dqs, axis=1), jnp.concatenate(dks, axis=1), jnp.concatenate(dvs, axis=1)


S5_SUB = 8


def _cmul_add(xr, xi, ar, ai, sr, si):
    return xr + ar * sr - ai * si, xi + ar * si + ai * sr


def _s5_tables(ar, ai, reverse):
    pows = [(ar, ai)]
    for _ in range(S5_SUB - 1):
        pr, pi = pows[-1]
        pows.append((pr * ar - pi * ai, pr * ai + pi * ar))
    cr = jnp.concatenate([p[0] for p in pows], axis=0)
    ci = jnp.concatenate([p[1] for p in pows], axis=0)
    if reverse:
        cr, ci = cr[::-1], ci[::-1]
    t = jnp.arange(S5_SUB)[:, None]
    blocks = [jnp.concatenate([cr, ci], axis=1)]
    for s in (1, 2, 4):
        keep = (t < S5_SUB - s) if reverse else (t >= s)
        sr, si = pows[s - 1]
        blocks.append(jnp.concatenate([jnp.where(keep, sr, 0.0), jnp.where(keep, si, 0.0)], axis=1))
    return jnp.concatenate(blocks, axis=0)


def _sub_scan(xr, xi, tab_ref, reverse):
    for k, s in enumerate((1, 2, 4)):
        blk = slice(S5_SUB * (k + 1), S5_SUB * (k + 2))
        sh = S5_SUB - s if reverse else s
        xr, xi = _cmul_add(xr, xi, tab_ref[blk, :S5_N], tab_ref[blk, S5_N:], pltpu.roll(xr, sh, 0), pltpu.roll(xi, sh, 0))
    return xr, xi


def _s5_fwd(z, b_st, c_st, d_skip, tab, name, tc=512):
    T = z.shape[0]
    tc = min(tc, T)
    nsub = tc // S5_SUB

    def body(u_ref, bst_ref, cst_ref, d_ref, tab_ref, h_ref, yc_ref, yg_ref, carry, x_sc):
        @pl.when(pl.program_id(0) == 0)
        def _():
            carry[...] = jnp.zeros_like(carry)

        u = u_ref[...].astype(F32)
        x_sc[...] = jnp.dot(u.astype(BF16), bst_ref[...], preferred_element_type=F32)

        def sub(b, c):
            r = pl.ds(pl.multiple_of(b * S5_SUB, S5_SUB), S5_SUB)
            xr, xi = _sub_scan(x_sc[r, :S5_N], x_sc[r, S5_N:], tab_ref, False)
            hr, hi = _cmul_add(xr, xi, tab_ref[0:S5_SUB, :S5_N], tab_ref[0:S5_SUB, S5_N:], c[0], c[1])
            x_sc[r, :S5_N] = hr
            x_sc[r, S5_N:] = hi
            return hr[S5_SUB - 1:S5_SUB], hi[S5_SUB - 1:S5_SUB]

        cr, ci = lax.fori_loop(0, nsub, sub, (carry[0:1, :S5_N], carry[0:1, S5_N:]))
        carry[0:1, :S5_N] = cr
        carry[0:1, S5_N:] = ci
        h = x_sc[...]
        h_ref[...] = h
        yc = jnp.dot(h.astype(BF16), cst_ref[...], preferred_element_type=F32)
        yc_ref[...] = yc
        yg_ref[...] = _gelu_skip(yc, u, d_ref[...]).astype(yg_ref.dtype)

    whole = lambda a: pl.BlockSpec(a.shape, lambda i: (0, 0))
    row = lambda w: pl.BlockSpec((tc, w), lambda i: (i, 0))
    return pl.pallas_call(
        body, name=name, grid=(T // tc,),
        in_specs=[pl.BlockSpec((tc, S5_W), lambda i: (i, Z_U // S5_W)), whole(b_st), whole(c_st), whole(d_skip), whole(tab)],
        out_specs=[row(2 * S5_N), row(S5_W), row(S5_W)],
        out_shape=[jax.ShapeDtypeStruct((T, 2 * S5_N), F32), jax.ShapeDtypeStruct((T, S5_W), F32),
                   jax.ShapeDtypeStruct((T, S5_W), BF16)],
        scratch_shapes=[pltpu.VMEM((8, 2 * S5_N), F32), pltpu.VMEM((tc, 2 * S5_N), F32)],
        compiler_params=pltpu.CompilerParams(dimension_semantics=("arbitrary",)),
    )(z, b_st, c_st, d_skip, tab)


def _s5_bwd(dyg, yc, z, h, b_st, c_st, d_skip, tab, name, tc=512):
    T = z.shape[0]
    tc = min(tc, T)
    nc, nsub = T // tc, tc // S5_SUB

    def body(dyg_ref, yc_ref, u_ref, h_ref, hp_ref, bst_ref, cst_ref, d_ref, tab_ref,
             du_ref, da_ref, db_ref, dct_ref, dd_ref, carry, x_sc):
        g = pl.program_id(0)

        @pl.when(g == 0)
        def _():
            carry[...] = jnp.zeros_like(carry)
            for r in (da_ref, db_ref, dct_ref, dd_ref):
                r[...] = jnp.zeros_like(r)

        u = u_ref[...].astype(F32)
        _, vjp = jax.vjp(_gelu_skip, yc_ref[...], u, d_ref[...])
        dyc, du_skip, dd = vjp(dyg_ref[...])
        dd_ref[...] += dd
        dyc16 = dyc.astype(BF16)
        dct_ref[...] += lax.dot_general(dyc16, h_ref[...].astype(BF16), _TN, preferred_element_type=F32)
        x_sc[...] = lax.dot_general(dyc16, cst_ref[...], _NT, preferred_element_type=F32)
        first = jnp.where(g == nc - 1, 0.0, 1.0)
        row0 = lax.broadcasted_iota(jnp.int32, (S5_SUB, S5_N), 0) == 0

        def sub(k, c):
            cr, ci, acc_r, acc_i = c
            b = nsub - 1 - k
            r = pl.ds(pl.multiple_of(b * S5_SUB, S5_SUB), S5_SUB)
            xr, xi = _sub_scan(x_sc[r, :S5_N], x_sc[r, S5_N:], tab_ref, True)
            lr, li = _cmul_add(xr, xi, tab_ref[0:S5_SUB, :S5_N], tab_ref[0:S5_SUB, S5_N:], cr, ci)
            x_sc[r, :S5_N] = lr
            x_sc[r, S5_N:] = li
            rp = pl.ds(pl.multiple_of(jnp.maximum(b - 1, 0) * S5_SUB, S5_SUB), S5_SUB)
            last_r = jnp.where(b == 0, hp_ref[S5_SUB - 1:S5_SUB, :S5_N] * first, h_ref[rp, :S5_N][S5_SUB - 1:S5_SUB])
            last_i = jnp.where(b == 0, hp_ref[S5_SUB - 1:S5_SUB, S5_N:] * first, h_ref[rp, S5_N:][S5_SUB - 1:S5_SUB])
            pr = jnp.where(row0, last_r, pltpu.roll(h_ref[r, :S5_N], 1, 0))
            pi = jnp.where(row0, last_i, pltpu.roll(h_ref[r, S5_N:], 1, 0))
            return lr[0:1], li[0:1], acc_r + (lr * pr + li * pi), acc_i + (li * pr - lr * pi)

        zero = jnp.zeros((S5_SUB, S5_N), F32)
        cr, ci, acc_r, acc_i = lax.fori_loop(0, nsub, sub, (carry[0:1, :S5_N], carry[0:1, S5_N:], zero, zero))
        carry[0:1, :S5_N] = cr
        carry[0:1, S5_N:] = ci
        da_ref[0:1, :S5_N] += jnp.sum(acc_r, axis=0, keepdims=True)
        da_ref[0:1, S5_N:] += jnp.sum(acc_i, axis=0, keepdims=True)
        lam16 = x_sc[...].astype(BF16)
        db_ref[...] += lax.dot_general(u.astype(BF16), lam16, _TN, preferred_element_type=F32)
        du = lax.dot_general(lam16, bst_ref[...], _NT, preferred_element_type=F32) + du_skip
        du_ref[...] = du.astype(du_ref.dtype)

    whole = lambda a: pl.BlockSpec(a.shape, lambda g: (0, 0))
    row = lambda w, off=0: pl.BlockSpec((tc, w), lambda g: (nc - 1 - g, off))
    prev = pl.BlockSpec((S5_SUB, 2 * S5_N), lambda g: (jnp.maximum((nc - 1 - g) * nsub - 1, 0), 0))
    acc = lambda s: pl.BlockSpec(s, lambda g: (0, 0))
    return pl.pallas_call(
        body, name=name, grid=(nc,),
        in_specs=[row(S5_W), row(S5_W), row(S5_W, Z_U // S5_W), row(2 * S5_N), prev, whole(b_st), whole(c_st),
                  whole(d_skip), whole(tab)],
        out_specs=[row(S5_W), acc((1, 2 * S5_N)), acc((S5_W, 2 * S5_N)), acc((S5_W, 2 * S5_N)), acc((1, S5_W))],
        out_shape=[jax.ShapeDtypeStruct((T, S5_W), BF16), jax.ShapeDtypeStruct((1, 2 * S5_N), F32),
                   jax.ShapeDtypeStruct((S5_W, 2 * S5_N), F32), jax.ShapeDtypeStruct((S5_W, 2 * S5_N), F32),
                   jax.ShapeDtypeStruct((1, S5_W), F32)],
        scratch_shapes=[pltpu.VMEM((8, 2 * S5_N), F32), pltpu.VMEM((tc, 2 * S5_N), F32)],
        compiler_params=pltpu.CompilerParams(dimension_semantics=("arbitrary",)),
    )(dyg, yc, z, h, h, b_st, c_st, d_skip, tab)


def _s5_discretize(lam_re, lam_im, log_dt, b_re, b_im):
    lr = jnp.minimum(lam_re, S5_MAX_RE)
    li = lam_im
    dt = jnp.exp(log_dt)[:, None]
    mag = jnp.exp(lr * dt)
    ang = li * dt
    ab_re = mag * jnp.cos(ang)
    ab_im = mag * jnp.sin(ang)
    den = lr * lr + li * li
    nr = ab_re - 1.0
    f_re = ((nr * lr + ab_im * li) / den)[..., None]
    f_im = ((ab_im * lr - nr * li) / den)[..., None]
    return ab_re, ab_im, f_re * b_re - f_im * b_im, f_re * b_im + f_im * b_re


def _adamw_fn(w, g, m, v):
    m = ADAM_B1 * m + (1.0 - ADAM_B1) * g
    v = ADAM_B2 * v + (1.0 - ADAM_B2) * (g * g)
    m_hat = m / (1.0 - ADAM_B1 ** ADAM_STEP)
    v_hat = v / (1.0 - ADAM_B2 ** ADAM_STEP)
    delta = -ADAM_LR * (m_hat / (jnp.sqrt(v_hat) + ADAM_EPS) + ADAM_WD * w)
    return delta, m, v


def _adamw(w, g, m, v, name):
    wd = w.shape[1]
    return _rowwise(_adamw_fn, name, [_whole(w), _whole(g), _whole(m), _whole(v)], [], [(wd, F32)] * 3,
                    tm=_row_tile(w.shape[0]))


def _adamw_many(ws, gs, ms, vs, name):
    n = len(ws)

    def body(*refs):
        for i in range(n):
            d_, m_, v_ = _adamw_fn(*[refs[kk * n + i][...] for kk in range(4)])
            refs[4 * n + i][...] = d_
            refs[5 * n + i][...] = m_
            refs[6 * n + i][...] = v_

    out = pl.pallas_call(body, name=name, out_shape=[jax.ShapeDtypeStruct(w.shape, F32) for w in ws] * 3)(*ws, *gs, *ms, *vs)
    return out[:n], out[n:2 * n], out[2 * n:]


def _me():
    return lax.axis_index("x"), lax.axis_index("y"), lax.axis_index("c")


def _chip_of(j):
    return j // 2, j % 2


_HBM = pl.BlockSpec(memory_space=pl.ANY)


def _gather_weights(shard):
    _, R, W = shard.shape

    def body(s_ref, out_ref, send_sems, recv_sems):
        x, y, c = _me()
        mine = 2 * x + y
        sib = (x, y, 1 - c)
        chips = [(1 - x, y), (x, 1 - y), (1 - x, 1 - y)]

        def rcopy(kk, src, chip_idx, half, to):
            return pltpu.make_async_remote_copy(src_ref=src, dst_ref=out_ref.at[chip_idx, half],
                                                send_sem=send_sems.at[kk], recv_sem=recv_sems.at[kk],
                                                device_id=to, device_id_type=MESH)

        first = [rcopy(j, s_ref.at[c], mine, c, (*chip, c)) for j, chip in enumerate(chips)]
        for cp in first:
            cp.start()
        passed = []
        for j, (px, py) in enumerate(chips):
            src_chip = 2 * px + py
            rcopy(j, s_ref.at[c], src_chip, c, (x, y, c)).wait_recv()
            fwd = rcopy(3 + j, out_ref.at[src_chip, c], src_chip, c, sib)
            fwd.start()
            passed.append(fwd)
        for j, (px, py) in enumerate(chips):
            rcopy(3 + j, s_ref.at[c], 2 * px + py, 1 - c, (x, y, c)).wait_recv()
        for cp in first + passed:
            cp.wait_send()

    return pl.pallas_call(
        body, name="gather_weights", in_specs=[_HBM], out_specs=_HBM,
        out_shape=jax.ShapeDtypeStruct((4, 2, R, W), shard.dtype),
        scratch_shapes=[pltpu.SemaphoreType.DMA((6,)), pltpu.SemaphoreType.DMA((6,))],
    )(shard)


def _pair_exchange(g, name):
    _, _, R, W = g.shape

    def body(g_ref, out_ref, send_sem, recv_sem):
        x, y, c = _me()
        cp = pltpu.make_async_remote_copy(src_ref=g_ref.at[:, 1 - c], dst_ref=out_ref, send_sem=send_sem,
                                          recv_sem=recv_sem, device_id=(x, y, 1 - c), device_id_type=MESH)
        cp.start()
        cp.wait()

    return pl.pallas_call(
        body, name=name, in_specs=[_HBM], out_specs=_HBM,
        out_shape=jax.ShapeDtypeStruct((4, R, W), g.dtype),
        scratch_shapes=[pltpu.SemaphoreType.DMA(()), pltpu.SemaphoreType.DMA(())],
    )(g)


def _pair_add(g, recv, cidx, name):
    _, _, R, W = g.shape
    tr = _row_tile(R)

    def body(c_ref, a_ref, b_ref, o_ref):
        o_ref[...] = (a_ref[...] + b_ref[...]).astype(o_ref.dtype)

    gs = pltpu.PrefetchScalarGridSpec(
        num_scalar_prefetch=1, grid=(4, R // tr),
        in_specs=[pl.BlockSpec((None, None, tr, W), lambda j, r, c: (j, c[0], r, 0)),
                  pl.BlockSpec((None, tr, W), lambda j, r, c: (j, r, 0))],
        out_specs=pl.BlockSpec((None, tr, W), lambda j, r, c: (j, r, 0)))
    return pl.pallas_call(body, name=name, grid_spec=gs,
                          out_shape=jax.ShapeDtypeStruct((4, R, W), BF16))(cidx, g, recv)


def _carried_exchange(src_ref, dst_ref, send_sems, recv_sems, spread, incoming):
    x, y, c = _me()
    mine = 2 * x + y
    copies = []
    for j, (px, py) in enumerate([(1 - x, y), (x, 1 - y), (1 - x, 1 - y)]):
        src = src_ref.at[2 * px + py] if spread else src_ref.at[c]
        slot = 2 * px + py if incoming else mine
        copies.append(pltpu.make_async_remote_copy(src_ref=src, dst_ref=dst_ref.at[slot], send_sem=send_sems.at[j],
                                                   recv_sem=recv_sems.at[j], device_id=(px, py, c), device_id_type=MESH))
    return copies


def _chip_exchange(p, name):
    _, R, W = p.shape

    def body(p_ref, out_ref, send_sems, recv_sems):
        outs = _carried_exchange(p_ref, out_ref, send_sems, recv_sems, True, False)
        for cp in outs:
            cp.start()
        for cp in _carried_exchange(p_ref, out_ref, send_sems, recv_sems, True, True):
            cp.wait_recv()
        for cp in outs:
            cp.wait_send()

    return pl.pallas_call(
        body, name=name, in_specs=[_HBM], out_specs=_HBM,
        out_shape=jax.ShapeDtypeStruct((4, R, W), p.dtype),
        scratch_shapes=[pltpu.SemaphoreType.DMA((3,)), pltpu.SemaphoreType.DMA((3,))],
    )(p)


def _chip_sum(q, name):
    _, R, W = q.shape
    tr = _row_tile(R)

    def body(q_ref, o_ref):
        f = lambda t: q_ref[t].astype(F32)
        o_ref[...] = ((f(0) + f(1)) + f(2)) + f(3)

    return pl.pallas_call(body, name=name, grid=(R // tr,),
                          in_specs=[pl.BlockSpec((4, tr, W), lambda r: (0, r, 0))],
                          out_specs=pl.BlockSpec((tr, W), lambda r: (r, 0)),
                          out_shape=jax.ShapeDtypeStruct((R, W), F32))(q)


def _pair_share(r, name):
    R, W = r.shape

    def body(r_ref, out_ref, send_sem, recv_sem):
        x, y, c = _me()
        cp = pltpu.make_async_remote_copy(src_ref=r_ref, dst_ref=out_ref, send_sem=send_sem, recv_sem=recv_sem,
                                          device_id=(x, y, 1 - c), device_id_type=MESH)
        cp.start()
        cp.wait()

    return pl.pallas_call(
        body, name=name, in_specs=[_HBM], out_specs=_HBM,
        out_shape=jax.ShapeDtypeStruct((R, W), r.dtype),
        scratch_shapes=[pltpu.SemaphoreType.DMA(()), pltpu.SemaphoreType.DMA(())],
    )(r)


def _allreduce_small(vec):
    R, W = vec.shape

    def body(v_ref, out_ref, buf, send_sems, recv_sems):
        x, y, c = _me()
        me = 4 * x + 2 * y + c
        buf[me] = v_ref[...]
        cps = []
        for kk in range(1, 8):
            peer = (x ^ (kk >> 2), y ^ ((kk >> 1) & 1), c ^ (kk & 1))
            cp = pltpu.make_async_remote_copy(src_ref=v_ref, dst_ref=buf.at[me], send_sem=send_sems.at[kk - 1],
                                              recv_sem=recv_sems.at[kk - 1], device_id=peer, device_id_type=MESH)
            cp.start()
            cps.append(cp)
        for kk in range(1, 8):
            peer = (x ^ (kk >> 2), y ^ ((kk >> 1) & 1), c ^ (kk & 1))
            pltpu.make_async_remote_copy(src_ref=v_ref, dst_ref=buf.at[me ^ kk], send_sem=send_sems.at[kk - 1],
                                         recv_sem=recv_sems.at[kk - 1], device_id=peer, device_id_type=MESH).wait_recv()
        for cp in cps:
            cp.wait_send()
        acc = buf[0]
        for d in range(1, 8):
            acc = acc + buf[d]
        out_ref[...] = acc

    vm = pl.BlockSpec(memory_space=pltpu.VMEM)
    return pl.pallas_call(
        body, name="allreduce_small", in_specs=[vm], out_specs=vm, out_shape=jax.ShapeDtypeStruct((R, W), F32),
        scratch_shapes=[pltpu.VMEM((8, R, W), F32), pltpu.SemaphoreType.DMA((7,)), pltpu.SemaphoreType.DMA((7,))],
    )(vec)


def _pack_flat(parts, align, dtype):
    flat = jnp.concatenate([p.reshape(-1).astype(dtype) for p in parts])
    n = flat.shape[0]
    pad = (-n) % align
    return jnp.pad(flat, (0, pad)) if pad else flat


def _unpack_flat(flat, shapes):
    out, off = [], 0
    for s in shapes:
        n = math.prod(s)
        out.append(flat[off:off + n].reshape(s))
        off += n
    return out


def _full_from_shards(name, sh):
    if name in COL_SHARDED:
        return jnp.transpose(sh, (1, 0, 2)).reshape(sh.shape[1], 4 * sh.shape[2])
    return sh.reshape(4 * sh.shape[1], sh.shape[2])


def _shards_from_full(name, full):
    if name in COL_SHARDED:
        r, cc = full.shape[0], full.shape[1] // 4
        return jnp.transpose(full.reshape(r, 4, cc), (1, 0, 2)).reshape(4, r * cc)
    return full.reshape(4, -1)


def _pad_heads(w, width):
    k = w.shape[0]
    return jnp.pad(w.reshape(k, HEADS, width), ((0, 0), (0, 0), (0, HP - width))).reshape(k, HEADS * HP)


def _unpad_heads(w, width):
    k = w.shape[0]
    return w.reshape(k, HEADS, HP)[:, :, :width].reshape(k, HEADS * width)


def _blockdiag(t):
    g, a, b = t.shape
    return jnp.einsum("gab,gk->gakb", t, jnp.eye(g, dtype=t.dtype)).reshape(g * a, g * b)


def _blockdiag_t(m, a, b):
    g = m.shape[0] // a
    return jnp.einsum("gagb->gab", m.reshape(g, a, g, b))


def _local_step(x, mem, positions, target, W, sp, overlap=None):
    T = x.shape[0]
    row = lambda v: v.reshape(1, -1)

    ab_re, ab_im, bb_re, bb_im = _s5_discretize(sp["s5_lam_re"], sp["s5_lam_im"], sp["s5_log_dt"], sp["s5_b_re"], sp["s5_b_im"])
    ar, ai = ab_re.reshape(1, S5_N), ab_im.reshape(1, S5_N)
    tab_fwd, tab_rev = _s5_tables(ar, ai, False), _s5_tables(ar, -ai, True)
    b_st = jnp.concatenate([_blockdiag(jnp.swapaxes(bb_re, 1, 2)), _blockdiag(jnp.swapaxes(bb_im, 1, 2))],
                           axis=1).astype(BF16)
    c_st = jnp.concatenate([_blockdiag(jnp.swapaxes(sp["s5_c_re"], 1, 2)),
                            -_blockdiag(jnp.swapaxes(sp["s5_c_im"], 1, 2))], axis=0).astype(BF16)
    inv = ROPE_THETA ** (-jnp.arange(0, ROPE, 2, dtype=F32) / ROPE)
    ang = positions.astype(F32)[:, None] * inv
    cs, sn = jnp.cos(ang), jnp.sin(ang)
    one, zero = jnp.ones((T, NOPE), F32), jnp.zeros((T, NOPE), F32)
    z16, tail1, tail0 = jnp.zeros((T, ROPE // 2), F32), jnp.ones((T, HP - QK), F32), jnp.zeros((T, HP - QK), F32)
    t_cos = jnp.concatenate([one, cs, cs, tail1], axis=1)
    t_s1 = jnp.concatenate([zero, z16, sn, tail0], axis=1)
    t_s2 = jnp.concatenate([zero, -sn, z16, tail0], axis=1)

    w_in = W["w_in"]
    w_in_re = jnp.concatenate([w_in[:, 800:], w_in[:, :768], w_in[:, 768:800],
                               jnp.zeros((D_MODEL, HP - ROPE), w_in.dtype)], axis=1)
    w_uq_p = _pad_heads(W["w_uq"], QK)
    wkv = W["w_ukv"].reshape(S5_W, HEADS, NOPE + VD)
    w_uk_p = _pad_heads(wkv[:, :, :NOPE].reshape(S5_W, HEADS * NOPE), NOPE)
    w_uv_p = _pad_heads(wkv[:, :, NOPE:].reshape(S5_W, HEADS * VD), VD)
    w_oa_p = jnp.pad(W["w_oa"].reshape(HEADS, VD, D_MODEL), ((0, 0), (0, HP - VD), (0, 0))).reshape(HEADS * HP, D_MODEL)

    g_in, b_in = row(sp["ln_in_g"]), row(sp["ln_in_b"])
    g1, b1, g2, b2, g3, b3 = (row(sp[k]) for k in ("ln1_g", "ln1_b", "ln2_g", "ln2_b", "ln3_g", "ln3_b"))
    gq, gkv, d_skip = row(sp["q_norm_g"]), row(sp["kv_norm_g"]), row(sp["s5_d"])

    (h0,) = _rowwise(lambda a, g, b: (_ln(a, g, b),), "ln_in_fwd", [_whole(x)], [g_in, b_in], [(D_MODEL, F32)], tm=512)
    z = _mm(h0, w_in_re, "in_proj", tm=512, tn=Z_W, out_dtype=BF16)
    hs, yc, yg = _s5_fwd(z, b_st, c_st, d_skip, tab_fwd, "s5_fwd")
    y2 = _mm(yg, W["w_glu"], "glu_proj", tn=2048, out_dtype=BF16)
    cqn, ckvn = _rowwise(lambda a, b, ga, gb: (_rms(a, ga), _rms(b, gb)), "mla_norm_fwd",
                         [(z, Z_CQ, S5_W), (z, Z_CKV, S5_W)], [gq, gkv], [(S5_W, BF16)] * 2)
    tabs = [(t_cos, 0, HP), (t_s1, 0, HP), (t_s2, 0, HP)]

    def rope_q(acc, tc_, t1, t2):
        return _rope(acc, _tile8(tc_), _tile8(t1), _tile8(t2)) * (ATT_SCALE * LOG2E)

    def rope_k(acc, krz, tc_, t1, t2):
        return acc + _tile8(_rope(pltpu.roll(krz.astype(F32), NOPE, 1), tc_, t1, t2))

    q = _mm(cqn, w_uq_p, "mla_uq_rope", epi=rope_q, rextras=tabs, out_dtype=BF16)
    k = _mm(ckvn, w_uk_p, "mla_uk_rope", epi=rope_k, rextras=[(z, Z_KR, HP)] + tabs, out_dtype=BF16)
    v = _mm(ckvn, w_uv_p, "mla_uv", out_dtype=BF16)
    if overlap is None:
        o, lse = _flash_fwd(q, k, v, "mla_attn_fwd")
    else:
        o, lse, landed = _flash_fwd(q, k, v, "mla_attn_fwd", carry=(overlap[0], False))
        W = {**W, **overlap[1](landed)}
    def gate_epi(acc, gs_, ga_, ya, yb):
        f = lambda t: t.astype(F32)
        return acc, _gate_mix(f(gs_), f(ga_), f(ya), f(yb), acc.astype(BF16).astype(F32))

    a_out, mixin = _mm(o, w_oa_p, "mla_oa_gate", epi=gate_epi, outs=[(D_MODEL, BF16)] * 2, tm=512,
                       rextras=[(z, Z_GS, D_MODEL), (z, Z_GA, D_MODEL), (y2, 0, D_MODEL), (y2, D_MODEL, D_MODEL)])
    def ln_epi(acc, h, g, b):
        s_ = DN_ALPHA * h + acc
        return s_, _ln(s_, g, b)

    two_rows = [(D_MODEL, F32)] * 2
    s1, h1 = _mm(mixin, W["w_o"], "mix_o_ln1", epi=ln_epi, extras=(h0,), pextras=(g1, b1), outs=two_rows, tm=512)
    xq = _mm(h1, W["w_xq"], "x_q", out_dtype=BF16)
    xk = _mm(mem, W["w_xk"], "x_k", out_dtype=BF16)
    xv = _mm(mem, W["w_xv"], "x_v", out_dtype=BF16)
    (xo,) = _rowwise(_xattn_fwd_fn, "xattn_fwd", [_whole(xq)], [xk, xv], [(D_MODEL, BF16)], tm=1024, upcast=False)
    s2, h2 = _mm(xo, W["w_xo"], "x_o_ln2", epi=ln_epi, extras=(h1,), pextras=(g2, b2), outs=two_rows, tm=512)
    a_up = _mm(h2, W["w_up"], "mlp_up", out_dtype=BF16)
    ff = _mm(a_up, W["w_down"], "mlp_down", act=_relu2, tk=2048)

    def loss_fn(h, r, tgt, g, b):
        def f(r_, g_, b_):
            e = _ln_res(h, r_, g_, b_) - tgt
            return 0.5 * jnp.sum(jnp.mean(e * e, axis=-1))
        lv, (dr_, dg_, db_) = jax.value_and_grad(f, argnums=(0, 1, 2))(r, g, b)
        return dr_, dg_, db_, jnp.broadcast_to(lv, (1, 128))

    dr3, dg3, db3, lossv = _rowwise(loss_fn, "ln3_loss_bwd", [_whole(h2), _whole(ff), _whole(target)], [g3, b3],
                                    [(D_MODEL, F32)], [(1, D_MODEL), (1, D_MODEL), (1, 128)], tm=512)
    gW, gs = {}, {"ln3_g": dg3, "ln3_b": db3}

    da = _mm(dr3, W["w_down"], "mlp_down_bwd_a", "nt", epi=lambda acc, a: acc * (2.0 * jnp.maximum(a, 0.0)), extras=(a_up,),
             out_dtype=BF16)
    gW["w_down"] = _mm(a_up, dr3, "mlp_down_bwd_w", "tn", m=4 * D_MODEL, n=D_MODEL, act=_relu2, tk=2048)
    gW["w_up"] = _mm(h2, da, "mlp_up_bwd_w", "tn", m=D_MODEL, n=4 * D_MODEL, tk=2048)
    dh2 = _mm(da, W["w_up"], "mlp_up_bwd_a", "nt", epi=lambda acc, e: acc + DN_ALPHA * e, extras=(dr3,), tk=2048)

    def ln_bwd(a, dy, g, b):
        _, vjp = jax.vjp(_ln, a, g, b)
        return vjp(dy)

    dr2, gs["ln2_g"], gs["ln2_b"] = _rowwise(ln_bwd, "ln2_bwd", [_whole(s2), _whole(dh2)], [g2, b2],
                                             [(D_MODEL, F32)], [(1, D_MODEL)] * 2, tm=512)
    dxo = _mm(dr2, W["w_xo"], "x_o_bwd_a", "nt", out_dtype=BF16)
    gW["w_xo"] = _mm(xo, dr2, "x_o_bwd_w", "tn", m=D_MODEL, n=D_MODEL)
    dxq, dxk, dxv = _rowwise(_xattn_bwd_fn, "xattn_bwd", [_whole(xq), _whole(dxo)], [xk, xv], [(D_MODEL, BF16)],
                             [(xk.shape[0], D_MODEL)] * 2, tm=1024, upcast=False)
    gW["w_xq"] = _mm(h1, dxq, "x_q_bwd_w", "tn", m=D_MODEL, n=D_MODEL)
    gW["w_xk"] = _mm(mem, dxk, "x_k_bwd_w", "tn", m=D_MODEL, n=D_MODEL)
    gW["w_xv"] = _mm(mem, dxv, "x_v_bwd_w", "tn", m=D_MODEL, n=D_MODEL)
    dh1 = _mm(dxq, W["w_xq"], "x_q_bwd_a", "nt", epi=lambda acc, e: acc + DN_ALPHA * e, extras=(dr2,))

    dr1, gs["ln1_g"], gs["ln1_b"] = _rowwise(ln_bwd, "ln1_bwd", [_whole(s1), _whole(dh1)], [g1, b1],
                                             [(D_MODEL, F32)], [(1, D_MODEL)] * 2, tm=512)
    dmixin = _mm(dr1, W["w_o"], "mix_o_bwd_a", "nt")
    gW["w_o"] = _mm(mixin, dr1, "mix_o_bwd_w", "tn", m=D_MODEL, n=D_MODEL)

    def gate_mix_bwd(gs_, ga_, ya, yb, ao, dy):
        _, vjp = jax.vjp(_gate_mix, gs_, ga_, ya, yb, ao)
        return vjp(dy)

    dgs, dga, dy2a, dy2b, da_out = _rowwise(
        gate_mix_bwd, "gate_mix_bwd",
        [(z, Z_GS, D_MODEL), (z, Z_GA, D_MODEL), (y2, 0, D_MODEL), (y2, D_MODEL, D_MODEL), _whole(a_out), _whole(dmixin)],
        [], [(D_MODEL, BF16)] * 5)

    def delta_epi(acc, o_):
        prod = acc * o_
        cols = [jnp.sum(prod[:, h * HP:(h + 1) * HP], axis=1, keepdims=True) for h in range(HEADS)]
        return acc, jnp.concatenate(cols, axis=1)

    do16, delta = _mm(da_out, w_oa_p, "mla_oa_bwd_a", "nt", epi=delta_epi, extras=(o,),
                      outs=[(HEADS * HP, BF16), (HEADS, F32)], tm=512)
    g_oa_p = _mm(o, da_out, "mla_oa_bwd_w", "tn", m=HEADS * HP, n=D_MODEL)
    gW["w_oa"] = g_oa_p.reshape(HEADS, HP, D_MODEL)[:, :VD].reshape(HEADS * VD, D_MODEL)
    delta_row = delta.T.reshape(HEADS, 1, T)
    if overlap is None:
        dq, dk, dv = _flash_bwd(q, k, v, do16, lse, delta_row, "mla_attn_bwd")
        riding = None
    else:
        sent = overlap[2]({n: gW[n] for n in BIG_LATE})
        dq, dk, dv, landed = _flash_bwd(q, k, v, do16, lse, delta_row, "mla_attn_bwd", carry=(sent, True))
        riding = (sent, landed)

    def rope_bwd(dq_, dk_, tc_, t1, t2):
        dqr = _rope_t(dq_, _tile8(tc_), _tile8(t1), _tile8(t2))
        dkr = dk_[:, 0:HP]
        for hh in range(1, HEADS):
            dkr = dkr + dk_[:, hh * HP:(hh + 1) * HP]
        dkr = _rope_t(jnp.where(_rope_lanes(dkr.shape), dkr, 0.0), tc_, t1, t2)
        dkr = pltpu.roll(dkr, NOPE, 1)
        lane = lax.broadcasted_iota(jnp.int32, dkr.shape, 1)
        return dqr, jnp.where(lane < ROPE, dkr, 0.0)

    dq_raw, dkr = _rowwise(rope_bwd, "mla_rope_bwd", [_whole(dq), _whole(dk), _whole(t_cos), _whole(t_s1), _whole(t_s2)], [],
                           [(HEADS * HP, BF16), (HP, BF16)])
    dcqn = _mm(dq_raw, w_uq_p, "mla_uq_bwd_a", "nt")
    gW["w_uq"] = _unpad_heads(_mm(cqn, dq_raw, "mla_uq_bwd_w", "tn", m=S5_W, n=HEADS * HP), QK)
    dckvn_k = _mm(dk, w_uk_p, "mla_uk_bwd_a", "nt")
    dckvn = _mm(dv, w_uv_p, "mla_uv_bwd_a", "nt", epi=lambda acc, e: acc + e, extras=(dckvn_k,))
    g_uk = _unpad_heads(_mm(ckvn, dk, "mla_uk_bwd_w", "tn", m=S5_W, n=HEADS * HP), NOPE).reshape(S5_W, HEADS, NOPE)
    g_uv = _unpad_heads(_mm(ckvn, dv, "mla_uv_bwd_w", "tn", m=S5_W, n=HEADS * HP), VD).reshape(S5_W, HEADS, VD)
    gW["w_ukv"] = jnp.concatenate([g_uk, g_uv], axis=2).reshape(S5_W, HEADS * (NOPE + VD))

    def norm_bwd(a, b, da_, db_, ga, gb):
        _, vja = jax.vjp(_rms, a, ga)
        _, vjb = jax.vjp(_rms, b, gb)
        dxa, dga_ = vja(da_)
        dxb, dgb_ = vjb(db_)
        return dxa, dxb, dga_, dgb_

    dcq, dckv, gs["q_norm_g"], gs["kv_norm_g"] = _rowwise(
        norm_bwd, "mla_norm_bwd", [(z, Z_CQ, S5_W), (z, Z_CKV, S5_W), _whole(dcqn), _whole(dckvn)], [gq, gkv],
        [(S5_W, BF16)] * 2, [(1, S5_W)] * 2)

    dy2 = jnp.concatenate([dy2a, dy2b], axis=1)
    dyg = _mm(dy2, W["w_glu"], "glu_bwd_a", "nt")
    gW["w_glu"] = _mm(yg, dy2, "glu_bwd_w", "tn", m=S5_W, n=2 * D_MODEL)

    du, gs["s5_ab"], g_bst, g_cst_t, gs["s5_d"] = _s5_bwd(dyg, yc, z, hs, b_st, c_st, d_skip, tab_rev, "s5_bwd")
    gs["s5_bb_re"] = jnp.swapaxes(_blockdiag_t(g_bst[:, :S5_N], S5_H, S5_P), 1, 2)
    gs["s5_bb_im"] = jnp.swapaxes(_blockdiag_t(g_bst[:, S5_N:], S5_H, S5_P), 1, 2)
    gs["s5_c_re"] = _blockdiag_t(g_cst_t[:, :S5_N], S5_H, S5_P)
    gs["s5_c_im"] = -_blockdiag_t(g_cst_t[:, S5_N:], S5_H, S5_P)

    dz = jnp.concatenate([dgs, dga, du, dcq, dckv, dkr], axis=1)
    g_in_re = _mm(h0, dz, "in_proj_bwd_w", "tn", m=D_MODEL, n=Z_W, tm=512, tn=Z_W)
    gW["w_in"] = jnp.concatenate([g_in_re[:, Z_U:Z_KR], g_in_re[:, Z_KR:Z_KR + ROPE], g_in_re[:, :Z_U]], axis=1)
    dh0 = _mm(dz, w_in_re, "in_proj_bwd_a", "nt", tm=512, tk=Z_W, epi=lambda acc, e: acc + DN_ALPHA * e, extras=(dr1,))

    dx, gs["ln_in_g"], gs["ln_in_b"] = _rowwise(ln_bwd, "ln_in_bwd", [_whole(x), _whole(dh0)], [g_in, b_in],
                                                [(D_MODEL, F32)], [(1, D_MODEL)] * 2, tm=512)
    return lossv[:, :1], dx, gW, gs, riding


_RAW_SMALL = (("loss", (1, 1)), ("ln_in_g", (1, D_MODEL)), ("ln_in_b", (1, D_MODEL)), ("ln1_g", (1, D_MODEL)),
              ("ln1_b", (1, D_MODEL)), ("ln2_g", (1, D_MODEL)), ("ln2_b", (1, D_MODEL)), ("ln3_g", (1, D_MODEL)),
              ("ln3_b", (1, D_MODEL)), ("q_norm_g", (1, S5_W)), ("kv_norm_g", (1, S5_W)), ("s5_d", (1, S5_W)),
              ("s5_ab", (1, 2 * S5_N)), ("s5_bb_re", (S5_G, S5_P, S5_H)), ("s5_bb_im", (S5_G, S5_P, S5_H)),
              ("s5_c_re", (S5_G, S5_H, S5_P)), ("s5_c_im", (S5_G, S5_H, S5_P)))


def kernel(x, mem, positions, ln_in_g, ln_in_b, w_in, s5_lam_re, s5_lam_im, s5_log_dt, s5_b_re, s5_b_im, s5_c_re, s5_c_im, s5_d, w_glu, q_norm_g, w_uq, kv_norm_g, w_ukv, w_oa, w_o, ln1_g, ln1_b, w_xq, w_xk, w_xv, w_xo, ln2_g, ln2_b, w_up, w_down, ln3_g, ln3_b, loss_target, m_ln_in_g, m_ln_in_b, m_w_in, m_s5_lam_re, m_s5_lam_im, m_s5_log_dt, m_s5_b_re, m_s5_b_im, m_s5_c_re, m_s5_c_im, m_s5_d, m_w_glu, m_q_norm_g, m_w_uq, m_kv_norm_g, m_w_ukv, m_w_oa, m_w_o, m_ln1_g, m_ln1_b, m_w_xq, m_w_xk, m_w_xv, m_w_xo, m_ln2_g, m_ln2_b, m_w_up, m_w_down, m_ln3_g, m_ln3_b, v_ln_in_g, v_ln_in_b, v_w_in, v_s5_lam_re, v_s5_lam_im, v_s5_log_dt, v_s5_b_re, v_s5_b_im, v_s5_c_re, v_s5_c_im, v_s5_d, v_w_glu, v_q_norm_g, v_w_uq, v_kv_norm_g, v_w_ukv, v_w_oa, v_w_o, v_ln1_g, v_ln1_b, v_w_xq, v_w_xk, v_w_xv, v_w_xo, v_ln2_g, v_ln2_b, v_w_up, v_w_down, v_ln3_g, v_ln3_b):
    a = dict(locals())
    wts = {n: a[n] for n in WEIGHTS}
    ms = {n: a["m_" + n] for n in WEIGHTS}
    vs = {n: a["v_" + n] for n in WEIGHTS}
    shard2d = {n: wts[n].reshape(wts[n].shape[-2], wts[n].shape[-1]) for n in BIG}
    nrow = {n: shard2d[n].size // 1024 for n in BIG}
    mine = 2 * lax.axis_index("x") + lax.axis_index("y")
    my_c = lax.axis_index("c")
    cidx = my_c.astype(jnp.int32).reshape(1)

    def group_rows(group):
        rows = sum(nrow[n] for n in group)
        return rows, (-rows) % PACK_ROWS

    def pack_shards(group):
        rows, pad = group_rows(group)
        parts = [shard2d[n].astype(BF16).reshape(nrow[n], 1024) for n in group] + [jnp.zeros((pad, 1024), BF16)]
        return jnp.concatenate(parts, axis=0).reshape(2, (rows + pad) // 2, 1024)

    def unpack_full(group, gathered):
        out, off = {}, 0
        for n in group:
            r, cc = shard2d[n].shape
            piece = lax.optimization_barrier(gathered[:, off:off + nrow[n]])
            out[n] = _full_from_shards(n, piece.reshape(4, r, cc))
            off += nrow[n]
        return out

    def pack_grads(group, gw):
        rows, pad = group_rows(group)
        parts = [_shards_from_full(n, gw[n]).reshape(4, nrow[n], 1024) for n in group] + [jnp.zeros((4, pad, 1024), F32)]
        return jnp.concatenate(parts, axis=1).reshape(4, 2, (rows + pad) // 2, 1024)

    def by_core(own, other):
        return jnp.where(my_c == 0, jnp.stack([own, other]), jnp.stack([other, own]))

    packed_e, packed_l = pack_shards(BIG_EARLY), pack_shards(BIG_LATE)
    gathered = lax.dynamic_update_slice(_gather_weights(packed_e), packed_e[None], (mine, 0, 0, 0))
    W = unpack_full(BIG_EARLY, gathered.reshape(4, -1, 1024))

    def finish_gather(landed):
        half = packed_l.shape[1]
        own_half = lax.dynamic_index_in_dim(packed_l, my_c, axis=0, keepdims=True)
        landed = lax.dynamic_update_slice(landed, own_half, (mine, 0, 0))
        other = _pair_share(landed.reshape(4 * half, 1024), "gather_late_share").reshape(4, half, 1024)
        both = jnp.moveaxis(by_core(landed, other), 0, 1)
        return unpack_full(BIG_LATE, both.reshape(4, 2 * half, 1024))

    def pair_reduce(group, gw, tag):
        gflat = pack_grads(group, gw)
        return _pair_add(gflat, _pair_exchange(gflat, "grad_pair_exchange_" + tag), cidx, "grad_pair_add_" + tag)

    def finish_reduce(group, sent, landed, tag):
        own = lax.dynamic_slice(sent, (mine, 0, 0), (1,) + sent.shape[1:])
        red = _chip_sum(lax.dynamic_update_slice(landed, own, (mine, 0, 0)), "grad_chip_sum_" + tag)
        gsh = by_core(red, _pair_share(red, "grad_pair_share_" + tag)).reshape(-1, 1024)
        out, off = {}, 0
        for n in group:
            out[n] = lax.optimization_barrier(gsh[off:off + nrow[n]]).reshape(shard2d[n].shape)
            off += nrow[n]
        return out

    sp = {n: wts[n] for n in SMALL}
    sp_local = {n: (sp[n][0] if sp[n].ndim > 1 else sp[n]) for n in SMALL}
    overlap = (packed_l, finish_gather, lambda gw: pair_reduce(BIG_LATE, gw, "late"))
    lossv, dx, gW, gs, riding = _local_step(x[0], mem[0], positions[0], loss_target[0], W, sp_local, overlap)

    g_out = finish_reduce(BIG_LATE, riding[0], riding[1], "late")
    sent_e = pair_reduce(BIG_EARLY, gW, "early")
    g_out.update(finish_reduce(BIG_EARLY, sent_e, _chip_exchange(sent_e, "grad_chip_exchange_early"), "early"))

    gs["loss"] = lossv
    raw = _pack_flat([gs[n].reshape(s) for n, s in _RAW_SMALL], 8 * 1024, F32)
    raw = _allreduce_small(raw.reshape(-1, 1024)).reshape(-1)
    rs = dict(zip([n for n, _ in _RAW_SMALL], _unpack_flat(raw, [s for _, s in _RAW_SMALL])))
    loss = rs["loss"].reshape(())
    _, disc_vjp = jax.vjp(_s5_discretize, sp_local["s5_lam_re"], sp_local["s5_lam_im"], sp_local["s5_log_dt"],
                          sp_local["s5_b_re"], sp_local["s5_b_im"])
    d_ab = rs["s5_ab"].reshape(2, S5_G, S5_P)
    g_lre, g_lim, g_ldt, g_bre, g_bim = disc_vjp((d_ab[0], d_ab[1], rs["s5_bb_re"], rs["s5_bb_im"]))
    small_g = {"s5_lam_re": g_lre, "s5_lam_im": g_lim, "s5_log_dt": g_ldt, "s5_b_re": g_bre, "s5_b_im": g_bim,
               "s5_c_re": rs["s5_c_re"], "s5_c_im": rs["s5_c_im"]}
    for n in ("ln_in_g", "ln_in_b", "ln1_g", "ln1_b", "ln2_g", "ln2_b", "ln3_g", "ln3_b", "q_norm_g", "kv_norm_g", "s5_d"):
        small_g[n] = rs[n]

    grads, deltas, new_m, new_v = {}, {}, {}, {}
    for n in BIG:
        d_, m_, v_ = _adamw(shard2d[n], g_out[n], ms[n].reshape(shard2d[n].shape), vs[n].reshape(shard2d[n].shape), "adamw_" + n)
        grads[n] = g_out[n].reshape(wts[n].shape)
        deltas[n], new_m[n], new_v[n] = (t.reshape(wts[n].shape) for t in (d_, m_, v_))
    as2d = lambda t: t.reshape(-1, t.shape[-1])
    sd, sm, sv = _adamw_many([as2d(wts[n]) for n in SMALL], [as2d(small_g[n].reshape(wts[n].shape)) for n in SMALL],
                             [as2d(ms[n]) for n in SMALL], [as2d(vs[n]) for n in SMALL], "adamw_small")
    for n, d_, m_, v_ in zip(SMALL, sd, sm, sv):
        grads[n] = small_g[n].reshape(wts[n].shape)
        deltas[n], new_m[n], new_v[n] = (t.reshape(wts[n].shape) for t in (d_, m_, v_))

    return (loss, dx[None], *[grads[n] for n in WEIGHTS], *[deltas[n] for n in WEIGHTS],
            *[new_m[n] for n in WEIGHTS], *[new_v[n] for n in WEIGHTS])
```

```python
import functools
import math

import jax
import jax.numpy as jnp
from jax import lax
from jax.experimental import pallas as pl
from jax.experimental.pallas import tpu as pltpu

F32 = jnp.float32
BF16 = jnp.bfloat16
MESH = pl.DeviceIdType.MESH

D_MODEL = 1024
S5_W = 256
S5_G = 16
S5_H = 16
S5_P = 64
S5_N = S5_G * S5_P
S5_MAX_RE = -1e-4
HEADS = 8
NOPE = 64
ROPE = 32
QK = NOPE + ROPE
VD = 64
HP = 128
XH = 4
XD = 256
LN_EPS = 1e-5
RMS_EPS = 1e-6
NEG_INF = -1e30
DN_ALPHA = 2.0 ** 0.25
ROPE_THETA = 10000.0
ADAM_LR, ADAM_B1, ADAM_B2, ADAM_EPS, ADAM_WD, ADAM_STEP = 0.001, 0.9, 0.999, 1e-08, 0.01, 10

Z_GS, Z_GA, Z_U, Z_CQ, Z_CKV, Z_KR, Z_W = 0, 1024, 2048, 2304, 2560, 2816, 2944
SCAN_CHUNK = 256

BIG_EARLY = ("w_in", "w_glu", "w_uq", "w_ukv", "w_oa", "w_o")
BIG_LATE = ("w_up", "w_xq", "w_xk", "w_xv", "w_xo", "w_down")
BIG = BIG_EARLY + BIG_LATE
COL_SHARDED = ("w_in", "w_glu", "w_uq", "w_ukv", "w_oa", "w_up")
SMALL = ("ln_in_g", "ln_in_b", "s5_lam_re", "s5_lam_im", "s5_log_dt", "s5_b_re", "s5_b_im", "s5_c_re", "s5_c_im",
         "s5_d", "q_norm_g", "kv_norm_g", "ln1_g", "ln1_b", "ln2_g", "ln2_b", "ln3_g", "ln3_b")
WEIGHTS = ("ln_in_g", "ln_in_b", "w_in", "s5_lam_re", "s5_lam_im", "s5_log_dt", "s5_b_re", "s5_b_im", "s5_c_re",
           "s5_c_im", "s5_d", "w_glu", "q_norm_g", "w_uq", "kv_norm_g", "w_ukv", "w_oa", "w_o", "ln1_g", "ln1_b",
           "w_xq", "w_xk", "w_xv", "w_xo", "ln2_g", "ln2_b", "w_up", "w_down", "ln3_g", "ln3_b")
PACK_ROWS = 2 * 16


def _row_tile(n, cap=512):
    best = n
    for t in range(8, min(n, cap) + 1, 8):
        if n % t == 0:
            best = t
    return best


def _pick(n, cap):
    best = None
    for t in range(128, min(n, cap) + 1, 128):
        if n % t == 0:
            best = t
    return best if best is not None and (best >= 512 or best == n) else n


def _mm(a, b, name, mode="nn", *, a_off=0, b_off=0, m=None, n=None, act=None, epi=None, extras=(), rextras=(),
        pextras=(), outs=None, out_dtype=F32, tm=1024, tn=1024, tk=None):
    if mode == "nn":
        M, (K, N) = a.shape[0], b.shape
    elif mode == "nt":
        M, (N, K) = a.shape[0], b.shape
    else:
        K, M, N = a.shape[0], m, n
    if mode == "tn":
        tm, tn, tk = _pick(M, tm), _pick(N, tn), min(tk or 1024, K)
    else:
        tm, tn, tk = min(tm, M), _pick(N, tn), _pick(K, tk or 1024)
    assert M % tm == 0 and N % tn == 0 and K % tk == 0, (name, M, N, K, tm, tn, tk)
    nk = K // tk
    if mode == "tn":
        assert a_off % tm == 0 and b_off % tn == 0
        ao, bo = a_off // tm, b_off // tn
        a_spec = pl.BlockSpec((tk, tm), lambda i, j, k: (k, i + ao))
        b_spec = pl.BlockSpec((tk, tn), lambda i, j, k: (k, j + bo))
        dims = (((0,), (0,)), ((), ()))
    else:
        assert a_off % tk == 0
        ao = a_off // tk
        a_spec = pl.BlockSpec((tm, tk), lambda i, j, k: (i, k + ao))
        if mode == "nn":
            b_spec = pl.BlockSpec((tk, tn), lambda i, j, k: (k, j))
            dims = (((1,), (0,)), ((), ()))
        else:
            b_spec = pl.BlockSpec((tn, tk), lambda i, j, k: (j, k))
            dims = (((1,), (1,)), ((), ()))
    e_spec = pl.BlockSpec((tm, tn), lambda i, j, k: (i, j))
    p_spec = pl.BlockSpec((1, tn), lambda i, j, k: (0, j))
    r_specs = [pl.BlockSpec((tm, w), functools.partial(lambda i, j, k, o: (i, o), o=off // w)) for _, off, w in rextras]
    n_extra = len(extras) + len(rextras) + len(pextras)
    if outs is None:
        o_specs, o_shapes = [e_spec], [jax.ShapeDtypeStruct((M, N), out_dtype)]
    else:
        assert all((w == tn and dt != "sum") or N == tn for w, dt in outs), name
        o_specs = [e_spec if w == tn and dt != "sum" else pl.BlockSpec((8 if dt == "sum" else tm, w), lambda i, j, k: (i, 0))
                   for w, dt in outs]
        o_shapes = [jax.ShapeDtypeStruct((M // tm * 8, w), F32) if dt == "sum" else
                    jax.ShapeDtypeStruct((M, N if w == tn else w), dt) for w, dt in outs]
    n_out = len(o_specs)
    direct = outs is None and epi is None and out_dtype == F32

    def body(*refs):
        a_ref, b_ref = refs[0], refs[1]
        e_refs = refs[2:2 + n_extra]
        o_refs = refs[2 + n_extra:2 + n_extra + n_out]
        av = a_ref[...]
        if act is not None:
            av = act(av.astype(F32))
        p = lax.dot_general(av.astype(BF16), b_ref[...].astype(BF16), dims, preferred_element_type=F32)

        def finish(r):
            if epi is not None:
                r = epi(r, *[e[...] for e in e_refs])
            for o_ref, v in zip(o_refs, r if isinstance(r, tuple) else (r,)):
                o_ref[...] = jnp.broadcast_to(v, o_ref.shape).astype(o_ref.dtype)

        if nk == 1:
            finish(p)
        else:
            acc = o_refs[0] if direct else refs[2 + n_extra + n_out]
            k = pl.program_id(2)

            @pl.when(k == 0)
            def _():
                acc[...] = p

            @pl.when(k > 0)
            def _():
                acc[...] += p

            if not direct:
                @pl.when(k == nk - 1)
                def _():
                    finish(acc[...])

    res = pl.pallas_call(
        body, name=name, grid=(M // tm, N // tn, nk),
        in_specs=[a_spec, b_spec] + [e_spec] * len(extras) + r_specs + [p_spec] * len(pextras), out_specs=o_specs,
        out_shape=o_shapes,
        scratch_shapes=[pltpu.VMEM((tm, tn), F32)] if nk > 1 and not direct else [],
        compiler_params=pltpu.CompilerParams(dimension_semantics=("parallel", "parallel", "arbitrary")),
    )(a, b, *extras, *[r[0] for r in rextras], *pextras)
    if outs is None:
        return res[0]
    return [r.reshape(M // tm, 8, -1)[:, 0].sum(axis=0, keepdims=True) if dt == "sum" else r for r, (_, dt) in zip(res, outs)]


def _rowwise(fn, name, rows, pars, outs, accs=(), tm=256, upcast=True):
    T = rows[0][0].shape[0]
    tm = min(tm, T)
    assert T % tm == 0
    n_in, n_out = len(rows) + len(pars), len(outs)
    in_specs = []
    for arr, off, w in rows:
        assert off % w == 0 and arr.shape[0] == T, name
        in_specs.append(pl.BlockSpec((tm, w), functools.partial(lambda i, o: (i, o), o=off // w)))
    for p in pars:
        in_specs.append(pl.BlockSpec(p.shape, lambda i: (0, 0)))
    out_specs = [pl.BlockSpec((tm, w), lambda i: (i, 0)) for w, _ in outs]
    out_specs += [pl.BlockSpec(s, lambda i: (0, 0)) for s in accs]
    out_shape = [jax.ShapeDtypeStruct((T, w), dt) for w, dt in outs] + [jax.ShapeDtypeStruct(s, F32) for s in accs]

    def body(*refs):
        res = fn(*[r[...].astype(F32) if upcast else r[...] for r in refs[:n_in]])
        o_refs = refs[n_in:]
        for r, v in zip(o_refs[:n_out], res[:n_out]):
            r[...] = v.astype(r.dtype)
        if accs:
            i = pl.program_id(0)

            @pl.when(i == 0)
            def _():
                for r in o_refs[n_out:]:
                    r[...] = jnp.zeros_like(r)

            for r, v in zip(o_refs[n_out:], res[n_out:]):
                r[...] += v

    res = pl.pallas_call(
        body, name=name, grid=(T // tm,), in_specs=in_specs, out_specs=out_specs, out_shape=out_shape,
        compiler_params=pltpu.CompilerParams(dimension_semantics=("arbitrary",)),
    )(*[r[0] for r in rows], *pars)
    return res


def _whole(arr):
    return (arr, 0, arr.shape[1])


def _ln(x, g, b):
    mu = jnp.mean(x, axis=-1, keepdims=True)
    xc = x - mu
    var = jnp.mean(xc * xc, axis=-1, keepdims=True)
    return xc * lax.rsqrt(var + LN_EPS) * g + b


def _ln_res(h, r, g, b):
    return _ln(DN_ALPHA * h + r, g, b)


def _rms(x, g):
    return x * lax.rsqrt(jnp.mean(x * x, axis=-1, keepdims=True) + RMS_EPS) * g


def _gelu_skip(yc, u, d):
    y = yc + d * u
    return 0.5 * y * (1.0 + lax.erf(y * (1.0 / math.sqrt(2.0))))


def _gate_mix(gs, ga, y2a, y2b, aout):
    return jax.nn.sigmoid(gs) * (y2a * jax.nn.sigmoid(y2b)) + jax.nn.sigmoid(ga) * aout


def _relu2(a):
    r = jnp.maximum(a, 0.0)
    return r * r


def _tile8(t):
    return jnp.concatenate([t] * HEADS, axis=1)


def _rope(x, cos, s1, s2):
    w = x.shape[1]
    return x * cos + pltpu.roll(x, ROPE // 2, 1) * s1 + pltpu.roll(x, w - ROPE // 2, 1) * s2


def _rope_t(dy, cos, s1, s2):
    w = dy.shape[1]
    return dy * cos + pltpu.roll(dy * s1, w - ROPE // 2, 1) + pltpu.roll(dy * s2, ROPE // 2, 1)


def _rope_lanes(shape):
    lane = lax.broadcasted_iota(jnp.int32, shape, 1) % HP
    return (lane >= NOPE) & (lane < QK)


ATT_HB_FWD, ATT_HB_BWD = 8, 4
ATT_SCALE = QK ** -0.5
LOG2E = 1.4426950408889634
LN2 = 0.6931471805599453
_NT = (((1,), (1,)), ((), ()))
_TN = (((0,), (0,)), ((), ()))


def _carry_specs(carry):
    if carry is None:
        return [], [], [], []
    arr = carry[0]
    return ([_HBM], [_HBM], [jax.ShapeDtypeStruct((4,) + arr.shape[1:], arr.dtype)],
            [pltpu.SemaphoreType.DMA((3,)), pltpu.SemaphoreType.DMA((3,))])


def _carry_start(exchange, when):
    @pl.when(when)
    def _():
        for cp in exchange(False):
            cp.start()


def _carry_wait(exchange, when):
    @pl.when(when)
    def _():
        for cp in exchange(True):
            cp.wait_recv()
        for cp in exchange(False):
            cp.wait_send()


def _tri_tables(n, by_row):
    if by_row:
        pairs = [(i, j) for i in range(n) for j in range(i + 1)]
    else:
        pairs = [(i, j) for j in range(n) for i in range(j, n)]
    return jnp.array([p[0] for p in pairs], jnp.int32), jnp.array([p[1] for p in pairs], jnp.int32)


def _flash_fwd(q, k, v, name, bq=512, carry=None):
    T = q.shape[0]
    bq = min(bq, T)
    nq = T // bq
    hb = ATT_HB_FWD
    n_carry = 0 if carry is None else 1

    i_tab, j_tab = _tri_tables(nq, by_row=True)

    def body(it_ref, jt_ref, q_ref, k_ref, v_ref, *rest):
        o_ref, lse_ref = rest[n_carry:n_carry + 2]
        m_sc, l_sc, acc_sc = rest[2 * n_carry + 2:2 * n_carry + 5]
        t = pl.program_id(1)
        i, j = it_ref[t], jt_ref[t]
        if carry is not None:
            exchange = functools.partial(_carried_exchange, rest[0], rest[3], rest[-2], rest[-1], carry[1])
            _carry_start(exchange, (pl.program_id(0) == 0) & (t == 0))

        @pl.when(j == 0)
        def _():
            m_sc[...] = jnp.full_like(m_sc, NEG_INF)
            l_sc[...] = jnp.zeros_like(l_sc)
            acc_sc[...] = jnp.zeros_like(acc_sc)

        def step(masked):
            for hh in range(hb):
                sl = slice(hh * HP, (hh + 1) * HP)
                s = lax.dot_general(q_ref[:, sl], k_ref[:, sl], _NT, preferred_element_type=F32)
                if masked:
                    row = lax.broadcasted_iota(jnp.int32, (bq, bq), 0)
                    col = lax.broadcasted_iota(jnp.int32, (bq, bq), 1)
                    s = jnp.where(col <= row, s, NEG_INF)
                m_prev = m_sc[hh]
                m_new = jnp.maximum(m_prev, jnp.max(s, axis=1, keepdims=True))
                alpha = jnp.exp2(m_prev - m_new)
                p = jnp.exp2(s - jnp.concatenate([m_new] * (bq // HP), axis=1))
                l_sc[hh] = alpha * l_sc[hh] + jnp.sum(p, axis=1, keepdims=True)
                acc_sc[hh] = alpha * acc_sc[hh] + jnp.dot(p.astype(BF16), v_ref[:, sl], preferred_element_type=F32)
                m_sc[hh] = m_new

        @pl.when(j < i)
        def _():
            step(False)

        @pl.when(j == i)
        def _():
            step(True)
            for hh in range(hb):
                o_ref[:, hh * HP:(hh + 1) * HP] = acc_sc[hh] / l_sc[hh]
                lse_ref[hh] = jnp.transpose((m_sc[hh] + jnp.log2(l_sc[hh])) * LN2)[0:1, :]

        if carry is not None:
            _carry_wait(exchange, (pl.program_id(0) == HEADS // hb - 1) & (t == i_tab.shape[0] - 1))

    qs = pl.BlockSpec((bq, hb * HP), lambda h, t, it, jt: (it[t], h))
    ks = pl.BlockSpec((bq, hb * HP), lambda h, t, it, jt: (jt[t], h))
    c_in, c_out, c_shape, c_sems = _carry_specs(carry)
    gs = pltpu.PrefetchScalarGridSpec(
        num_scalar_prefetch=2, grid=(HEADS // hb, i_tab.shape[0]), in_specs=[qs, ks, ks] + c_in,
        out_specs=[qs, pl.BlockSpec((hb, 1, bq), lambda h, t, it, jt: (h, 0, it[t]))] + c_out,
        scratch_shapes=[pltpu.VMEM((hb, bq, HP), F32)] * 3 + c_sems)
    return pl.pallas_call(
        body, name=name, grid_spec=gs,
        out_shape=[jax.ShapeDtypeStruct((T, HEADS * HP), F32), jax.ShapeDtypeStruct((HEADS, 1, T), F32)] + c_shape,
        compiler_params=pltpu.CompilerParams(dimension_semantics=("arbitrary", "arbitrary")),
    )(i_tab, j_tab, q, k, v, *([] if carry is None else [carry[0]]))


def _flash_bwd(q, k, v, do, lse_row, delta_row, name, bq=512, carry=None):
    T = q.shape[0]
    bq = min(bq, T)
    nq = T // bq
    hb = ATT_HB_BWD
    n_carry = 0 if carry is None else 1

    i_tab, j_tab = _tri_tables(nq, by_row=False)
    n_blk = i_tab.shape[0]

    def body(it_ref, jt_ref, q_ref, k_ref, v_ref, do_ref, lse_ref, dl_ref, *rest):
        dq_ref, dk_ref, dv_ref = rest[n_carry:n_carry + 3]
        t = pl.program_id(1)
        i, j = it_ref[t], jt_ref[t]
        if carry is not None:
            exchange = functools.partial(_carried_exchange, rest[0], rest[4], rest[-2], rest[-1], carry[1])
            _carry_start(exchange, (pl.program_id(0) == 0) & (t == 0))

        @pl.when(t == 0)
        def _():
            dq_ref[...] = jnp.zeros_like(dq_ref)

        @pl.when(i == j)
        def _():
            dk_ref[...] = jnp.zeros_like(dk_ref)
            dv_ref[...] = jnp.zeros_like(dv_ref)

        def step(masked):
            rows = pl.ds(pl.multiple_of(i * bq, bq), bq)
            for hh in range(hb):
                sl = slice(hh * HP, (hh + 1) * HP)
                qh, kh, doh = q_ref[:, sl], k_ref[:, sl], do_ref[:, sl]
                st = lax.dot_general(kh, qh, _NT, preferred_element_type=F32)
                pt = jnp.exp2(st - lse_ref[hh] * LOG2E)
                if masked:
                    krow = lax.broadcasted_iota(jnp.int32, (bq, bq), 0)
                    qcol = lax.broadcasted_iota(jnp.int32, (bq, bq), 1)
                    pt = jnp.where(krow <= qcol, pt, 0.0)
                dv_ref[:, sl] += jnp.dot(pt.astype(BF16), doh, preferred_element_type=F32)
                dpt = lax.dot_general(v_ref[:, sl], doh, _NT, preferred_element_type=F32)
                dst = (pt * (dpt - dl_ref[hh])).astype(BF16)
                dk_ref[:, sl] += jnp.dot(dst, qh, preferred_element_type=F32)
                dq_ref[rows, sl] += lax.dot_general(dst, kh, _TN, preferred_element_type=F32)

        @pl.when(i > j)
        def _():
            step(False)

        @pl.when(i == j)
        def _():
            step(True)

        @pl.when(i == nq - 1)
        def _():
            dk_ref[...] *= LN2

        @pl.when(t == n_blk - 1)
        def _():
            dq_ref[...] *= ATT_SCALE

        if carry is not None:
            _carry_wait(exchange, (pl.program_id(0) == HEADS // hb - 1) & (t == n_blk - 1))

    qs = pl.BlockSpec((bq, hb * HP), lambda h, t, it, jt: (it[t], h))
    ks = pl.BlockSpec((bq, hb * HP), lambda h, t, it, jt: (jt[t], h))
    rs = pl.BlockSpec((hb, 1, bq), lambda h, t, it, jt: (h, 0, it[t]))
    full = pl.BlockSpec((T, hb * HP), lambda h, t, it, jt: (0, h))
    c_in, c_out, c_shape, c_sems = _carry_specs(carry)
    gs = pltpu.PrefetchScalarGridSpec(num_scalar_prefetch=2, grid=(HEADS // hb, n_blk),
                                      in_specs=[qs, ks, ks, qs, rs, rs] + c_in, out_specs=[full, ks, ks] + c_out,
                                      scratch_shapes=c_sems)
    return pl.pallas_call(
        body, name=name, grid_spec=gs, out_shape=[jax.ShapeDtypeStruct((T, HEADS * HP), F32)] * 3 + c_shape,
        compiler_params=pltpu.CompilerParams(dimension_semantics=("arbitrary", "arbitrary")),
    )(i_tab, j_tab, q, k, v, do, lse_row, delta_row, *([] if carry is None else [carry[0]]))


def _xattn_heads(q, k, v):
    scale = XD ** -0.5
    ps = []
    for h in range(XH):
        sl = slice(h * XD, (h + 1) * XD)
        s = lax.dot_general(q[:, sl].astype(BF16), k[:, sl].astype(BF16), (((1,), (1,)), ((), ())),
                            preferred_element_type=F32) * scale
        e = jnp.exp(s - jnp.max(s, axis=1, keepdims=True))
        ps.append(e / jnp.sum(e, axis=1, keepdims=True))
    return ps


def _xattn_fwd_fn(q, k, v):
    ps = _xattn_heads(q, k, v)
    o = [jnp.dot(p.astype(BF16), v[:, h * XD:(h + 1) * XD].astype(BF16), preferred_element_type=F32)
         for h, p in enumerate(ps)]
    return (jnp.concatenate(o, axis=1),)


def _xattn_bwd_fn(q, do, k, v):
    scale = XD ** -0.5
    ps = _xattn_heads(q, k, v)
    dqs, dks, dvs = [], [], []
    tdims = (((0,), (0,)), ((), ()))
    for h, p in enumerate(ps):
        sl = slice(h * XD, (h + 1) * XD)
        doh = do[:, sl].astype(BF16)
        dvs.append(lax.dot_general(p.astype(BF16), doh, tdims, preferred_element_type=F32))
        dp = lax.dot_general(doh, v[:, sl].astype(BF16), (((1,), (1,)), ((), ())), preferred_element_type=F32)
        ds = (p * (dp - jnp.sum(dp * p, axis=1, keepdims=True)) * scale).astype(BF16)
        dqs.append(jnp.dot(ds, k[:, sl].astype(BF16), preferred_element_type=F32))
        dks.append(lax.dot_general(ds, q[:, sl].astype(BF16), tdims, preferred_element_type=F32))
    return jnp.concatenate(dqs, axis=1), jnp.concatenate(dks, axis=1), jnp.concatenate(dvs, axis=1)


S5_SUB = 8


def _cmul_add(xr, xi, ar, ai, sr, si):
    return xr + ar * sr - ai * si, xi + ar * si + ai * sr


def _s5_tables(ar, ai, reverse):
    pows = [(ar, ai)]
    for _ in range(S5_SUB - 1):
        pr, pi = pows[-1]
        pows.append((pr * ar - pi * ai, pr * ai + pi * ar))
    cr = jnp.concatenate([p[0] for p in pows], axis=0)
    ci = jnp.concatenate([p[1] for p in pows], axis=0)
    if reverse:
        cr, ci = cr[::-1], ci[::-1]
    t = jnp.arange(S5_SUB)[:, None]
    blocks = [jnp.concatenate([cr, ci], axis=1)]
    for s in (1, 2, 4):
        keep = (t < S5_SUB - s) if reverse else (t >= s)
        sr, si = pows[s - 1]
        blocks.append(jnp.concatenate([jnp.where(keep, sr, 0.0), jnp.where(keep, si, 0.0)], axis=1))
    return jnp.concatenate(blocks, axis=0)


def _sub_scan(xr, xi, tab_ref, reverse):
    for k, s in enumerate((1, 2, 4)):
        blk = slice(S5_SUB * (k + 1), S5_SUB * (k + 2))
        sh = S5_SUB - s if reverse else s
        xr, xi = _cmul_add(xr, xi, tab_ref[blk, :S5_N], tab_ref[blk, S5_N:], pltpu.roll(xr, sh, 0), pltpu.roll(xi, sh, 0))
    return xr, xi


def _s5_fwd(z, b_st, c_st, d_skip, tab, name, tc=512):
    T = z.shape[0]
    tc = min(tc, T)
    nsub = tc // S5_SUB

    def body(u_ref, bst_ref, cst_ref, d_ref, tab_ref, h_ref, yc_ref, yg_ref, carry, x_sc):
        @pl.when(pl.program_id(0) == 0)
        def _():
            carry[...] = jnp.zeros_like(carry)

        u = u_ref[...].astype(F32)
        x_sc[...] = jnp.dot(u.astype(BF16), bst_ref[...], preferred_element_type=F32)

        def sub(b, c):
            r = pl.ds(pl.multiple_of(b * S5_SUB, S5_SUB), S5_SUB)
            xr, xi = _sub_scan(x_sc[r, :S5_N], x_sc[r, S5_N:], tab_ref, False)
            hr, hi = _cmul_add(xr, xi, tab_ref[0:S5_SUB, :S5_N], tab_ref[0:S5_SUB, S5_N:], c[0], c[1])
            x_sc[r, :S5_N] = hr
            x_sc[r, S5_N:] = hi
            return hr[S5_SUB - 1:S5_SUB], hi[S5_SUB - 1:S5_SUB]

        cr, ci = lax.fori_loop(0, nsub, sub, (carry[0:1, :S5_N], carry[0:1, S5_N:]))
        carry[0:1, :S5_N] = cr
        carry[0:1, S5_N:] = ci
        h = x_sc[...]
        h_ref[...] = h
        yc = jnp.dot(h.astype(BF16), cst_ref[...], preferred_element_type=F32)
        yc_ref[...] = yc
        yg_ref[...] = _gelu_skip(yc, u, d_ref[...]).astype(yg_ref.dtype)

    whole = lambda a: pl.BlockSpec(a.shape, lambda i: (0, 0))
    row = lambda w: pl.BlockSpec((tc, w), lambda i: (i, 0))
    return pl.pallas_call(
        body, name=name, grid=(T // tc,),
        in_specs=[pl.BlockSpec((tc, S5_W), lambda i: (i, Z_U // S5_W)), whole(b_st), whole(c_st), whole(d_skip), whole(tab)],
        out_specs=[row(2 * S5_N), row(S5_W), row(S5_W)],
        out_shape=[jax.ShapeDtypeStruct((T, 2 * S5_N), F32), jax.ShapeDtypeStruct((T, S5_W), F32),
                   jax.ShapeDtypeStruct((T, S5_W), BF16)],
        scratch_shapes=[pltpu.VMEM((8, 2 * S5_N), F32), pltpu.VMEM((tc, 2 * S5_N), F32)],
        compiler_params=pltpu.CompilerParams(dimension_semantics=("arbitrary",)),
    )(z, b_st, c_st, d_skip, tab)


def _s5_bwd(dyg, yc, z, h, b_st, c_st, d_skip, tab, name, tc=512):
    T = z.shape[0]
    tc = min(tc, T)
    nc, nsub = T // tc, tc // S5_SUB

    def body(dyg_ref, yc_ref, u_ref, h_ref, hp_ref, bst_ref, cst_ref, d_ref, tab_ref,
             du_ref, da_ref, db_ref, dct_ref, dd_ref, carry, x_sc):
        g = pl.program_id(0)

        @pl.when(g == 0)
        def _():
            carry[...] = jnp.zeros_like(carry)
            for r in (da_ref, db_ref, dct_ref, dd_ref):
                r[...] = jnp.zeros_like(r)

        u = u_ref[...].astype(F32)
        _, vjp = jax.vjp(_gelu_skip, yc_ref[...], u, d_ref[...])
        dyc, du_skip, dd = vjp(dyg_ref[...])
        dd_ref[...] += dd
        dyc16 = dyc.astype(BF16)
        dct_ref[...] += lax.dot_general(dyc16, h_ref[...].astype(BF16), _TN, preferred_element_type=F32)
        x_sc[...] = lax.dot_general(dyc16, cst_ref[...], _NT, preferred_element_type=F32)
        first = jnp.where(g == nc - 1, 0.0, 1.0)
        row0 = lax.broadcasted_iota(jnp.int32, (S5_SUB, S5_N), 0) == 0

        def sub(k, c):
            cr, ci, acc_r, acc_i = c
            b = nsub - 1 - k
            r = pl.ds(pl.multiple_of(b * S5_SUB, S5_SUB), S5_SUB)
            xr, xi = _sub_scan(x_sc[r, :S5_N], x_sc[r, S5_N:], tab_ref, True)
            lr, li = _cmul_add(xr, xi, tab_ref[0:S5_SUB, :S5_N], tab_ref[0:S5_SUB, S5_N:], cr, ci)
            x_sc[r, :S5_N] = lr
            x_sc[r, S5_N:] = li
            rp = pl.ds(pl.multiple_of(jnp.maximum(b - 1, 0) * S5_SUB, S5_SUB), S5_SUB)
            last_r = jnp.where(b == 0, hp_ref[S5_SUB - 1:S5_SUB, :S5_N] * first, h_ref[rp, :S5_N][S5_SUB - 1:S5_SUB])
            last_i = jnp.where(b == 0, hp_ref[S5_SUB - 1:S5_SUB, S5_N:] * first, h_ref[rp, S5_N:][S5_SUB - 1:S5_SUB])
            pr = jnp.where(row0, last_r, pltpu.roll(h_ref[r, :S5_N], 1, 0))
            pi = jnp.where(row0, last_i, pltpu.roll(h_ref[r, S5_N:], 1, 0))
            return lr[0:1], li[0:1], acc_r + (lr * pr + li * pi), acc_i + (li * pr - lr * pi)

        zero = jnp.zeros((S5_SUB, S5_N), F32)
        cr, ci, acc_r, acc_i = lax.fori_loop(0, nsub, sub, (carry[0:1, :S5_N], carry[0:1, S5_N:], zero, zero))
        carry[0:1, :S5_N] = cr
        carry[0:1, S5_N:] = ci
        da_ref[0:1, :S5_N] += jnp.sum(acc_r, axis=0, keepdims=True)
        da_ref[0:1, S5_N:] += jnp.sum(acc_i, axis=0, keepdims=True)
        lam16 = x_sc[...].astype(BF16)
        db_ref[...] += lax.dot_general(u.astype(BF16), lam16, _TN, preferred_element_type=F32)
        du = lax.dot_general(lam16, bst_ref[...], _NT, preferred_element_type=F32) + du_skip
        du_ref[...] = du.astype(du_ref.dtype)

    whole = lambda a: pl.BlockSpec(a.shape, lambda g: (0, 0))
    row = lambda w, off=0: pl.BlockSpec((tc, w), lambda g: (nc - 1 - g, off))
    prev = pl.BlockSpec((S5_SUB, 2 * S5_N), lambda g: (jnp.maximum((nc - 1 - g) * nsub - 1, 0), 0))
    acc = lambda s: pl.BlockSpec(s, lambda g: (0, 0))
    return pl.pallas_call(
        body, name=name, grid=(nc,),
        in_specs=[row(S5_W), row(S5_W), row(S5_W, Z_U // S5_W), row(2 * S5_N), prev, whole(b_st), whole(c_st),
                  whole(d_skip), whole(tab)],
        out_specs=[row(S5_W), acc((1, 2 * S5_N)), acc((S5_W, 2 * S5_N)), acc((S5_W, 2 * S5_N)), acc((1, S5_W))],
        out_shape=[jax.ShapeDtypeStruct((T, S5_W), BF16), jax.ShapeDtypeStruct((1, 2 * S5_N), F32),
                   jax.ShapeDtypeStruct((S5_W, 2 * S5_N), F32), jax.ShapeDtypeStruct((S5_W, 2 * S5_N), F32),
                   jax.ShapeDtypeStruct((1, S5_W), F32)],
        scratch_shapes=[pltpu.VMEM((8, 2 * S5_N), F32), pltpu.VMEM((tc, 2 * S5_N), F32)],
        compiler_params=pltpu.CompilerParams(dimension_semantics=("arbitrary",)),
    )(dyg, yc, z, h, h, b_st, c_st, d_skip, tab)


def _s5_discretize(lam_re, lam_im, log_dt, b_re, b_im):
    lr = jnp.minimum(lam_re, S5_MAX_RE)
    li = lam_im
    dt = jnp.exp(log_dt)[:, None]
    mag = jnp.exp(lr * dt)
    ang = li * dt
    ab_re = mag * jnp.cos(ang)
    ab_im = mag * jnp.sin(ang)
    den = lr * lr + li * li
    nr = ab_re - 1.0
    f_re = ((nr * lr + ab_im * li) / den)[..., None]
    f_im = ((ab_im * lr - nr * li) / den)[..., None]
    return ab_re, ab_im, f_re * b_re - f_im * b_im, f_re * b_im + f_im * b_re


def _adamw_fn(w, g, m, v):
    m = ADAM_B1 * m + (1.0 - ADAM_B1) * g
    v = ADAM_B2 * v + (1.0 - ADAM_B2) * (g * g)
    m_hat = m / (1.0 - ADAM_B1 ** ADAM_STEP)
    v_hat = v / (1.0 - ADAM_B2 ** ADAM_STEP)
    delta = -ADAM_LR * (m_hat / (jnp.sqrt(v_hat) + ADAM_EPS) + ADAM_WD * w)
    return delta, m, v


def _adamw(w, g, m, v, name):
    wd = w.shape[1]
    return _rowwise(_adamw_fn, name, [_whole(w), _whole(g), _whole(m), _whole(v)], [], [(wd, F32)] * 3,
                    tm=_row_tile(w.shape[0]))


def _adamw_many(ws, gs, ms, vs, name):
    n = len(ws)

    def body(*refs):
        for i in range(n):
            d_, m_, v_ = _adamw_fn(*[refs[kk * n + i][...] for kk in range(4)])
            refs[4 * n + i][...] = d_
            refs[5 * n + i][...] = m_
            refs[6 * n + i][...] = v_

    out = pl.pallas_call(body, name=name, out_shape=[jax.ShapeDtypeStruct(w.shape, F32) for w in ws] * 3)(*ws, *gs, *ms, *vs)
    return out[:n], out[n:2 * n], out[2 * n:]


def _me():
    return lax.axis_index("x"), lax.axis_index("y"), lax.axis_index("c")


def _chip_of(j):
    return j // 2, j % 2


_HBM = pl.BlockSpec(memory_space=pl.ANY)


def _gather_weights(shard):
    _, R, W = shard.shape

    def body(s_ref, out_ref, send_sems, recv_sems):
        x, y, c = _me()
        mine = 2 * x + y
        sib = (x, y, 1 - c)
        chips = [(1 - x, y), (x, 1 - y), (1 - x, 1 - y)]

        def rcopy(kk, src, chip_idx, half, to):
            return pltpu.make_async_remote_copy(src_ref=src, dst_ref=out_ref.at[chip_idx, half],
                                                send_sem=send_sems.at[kk], recv_sem=recv_sems.at[kk],
                                                device_id=to, device_id_type=MESH)

        first = [rcopy(j, s_ref.at[c], mine, c, (*chip, c)) for j, chip in enumerate(chips)]
        for cp in first:
            cp.start()
        passed = []
        for j, (px, py) in enumerate(chips):
            src_chip = 2 * px + py
            rcopy(j, s_ref.at[c], src_chip, c, (x, y, c)).wait_recv()
            fwd = rcopy(3 + j, out_ref.at[src_chip, c], src_chip, c, sib)
            fwd.start()
            passed.append(fwd)
        for j, (px, py) in enumerate(chips):
            rcopy(3 + j, s_ref.at[c], 2 * px + py, 1 - c, (x, y, c)).wait_recv()
        for cp in first + passed:
            cp.wait_send()

    return pl.pallas_call(
        body, name="gather_weights", in_specs=[_HBM], out_specs=_HBM,
        out_shape=jax.ShapeDtypeStruct((4, 2, R, W), shard.dtype),
        scratch_shapes=[pltpu.SemaphoreType.DMA((6,)), pltpu.SemaphoreType.DMA((6,))],
    )(shard)


def _pair_exchange(g, name):
    _, _, R, W = g.shape

    def body(g_ref, out_ref, send_sem, recv_sem):
        x, y, c = _me()
        cp = pltpu.make_async_remote_copy(src_ref=g_ref.at[:, 1 - c], dst_ref=out_ref, send_sem=send_sem,
                                          recv_sem=recv_sem, device_id=(x, y, 1 - c), device_id_type=MESH)
        cp.start()
        cp.wait()

    return pl.pallas_call(
        body, name=name, in_specs=[_HBM], out_specs=_HBM,
        out_shape=jax.ShapeDtypeStruct((4, R, W), g.dtype),
        scratch_shapes=[pltpu.SemaphoreType.DMA(()), pltpu.SemaphoreType.DMA(())],
    )(g)


def _pair_add(g, recv, cidx, name):
    _, _, R, W = g.shape
    tr = _row_tile(R)

    def body(c_ref, a_ref, b_ref, o_ref):
        o_ref[...] = (a_ref[...] + b_ref[...]).astype(o_ref.dtype)

    gs = pltpu.PrefetchScalarGridSpec(
        num_scalar_prefetch=1, grid=(4, R // tr),
        in_specs=[pl.BlockSpec((None, None, tr, W), lambda j, r, c: (j, c[0], r, 0)),
                  pl.BlockSpec((None, tr, W), lambda j, r, c: (j, r, 0))],
        out_specs=pl.BlockSpec((None, tr, W), lambda j, r, c: (j, r, 0)))
    return pl.pallas_call(body, name=name, grid_spec=gs,
                          out_shape=jax.ShapeDtypeStruct((4, R, W), BF16))(cidx, g, recv)


def _carried_exchange(src_ref, dst_ref, send_sems, recv_sems, spread, incoming):
    x, y, c = _me()
    mine = 2 * x + y
    copies = []
    for j, (px, py) in enumerate([(1 - x, y), (x, 1 - y), (1 - x, 1 - y)]):
        src = src_ref.at[2 * px + py] if spread else src_ref.at[c]
        slot = 2 * px + py if incoming else mine
        copies.append(pltpu.make_async_remote_copy(src_ref=src, dst_ref=dst_ref.at[slot], send_sem=send_sems.at[j],
                                                   recv_sem=recv_sems.at[j], device_id=(px, py, c), device_id_type=MESH))
    return copies


def _chip_exchange(p, name):
    _, R, W = p.shape

    def body(p_ref, out_ref, send_sems, recv_sems):
        outs = _carried_exchange(p_ref, out_ref, send_sems, recv_sems, True, False)
        for cp in outs:
            cp.start()
        for cp in _carried_exchange(p_ref, out_ref, send_sems, recv_sems, True, True):
            cp.wait_recv()
        for cp in outs:
            cp.wait_send()

    return pl.pallas_call(
        body, name=name, in_specs=[_HBM], out_specs=_HBM,
        out_shape=jax.ShapeDtypeStruct((4, R, W), p.dtype),
        scratch_shapes=[pltpu.SemaphoreType.DMA((3,)), pltpu.SemaphoreType.DMA((3,))],
    )(p)


def _chip_sum(q, name):
    _, R, W = q.shape
    tr = _row_tile(R)

    def body(q_ref, o_ref):
        f = lambda t: q_ref[t].astype(F32)
        o_ref[...] = ((f(0) + f(1)) + f(2)) + f(3)

    return pl.pallas_call(body, name=name, grid=(R // tr,),
                          in_specs=[pl.BlockSpec((4, tr, W), lambda r: (0, r, 0))],
                          out_specs=pl.BlockSpec((tr, W), lambda r: (r, 0)),
                          out_shape=jax.ShapeDtypeStruct((R, W), F32))(q)


def _pair_share(r, name):
    R, W = r.shape

    def body(r_ref, out_ref, send_sem, recv_sem):
        x, y, c = _me()
        cp = pltpu.make_async_remote_copy(src_ref=r_ref, dst_ref=out_ref, send_sem=send_sem, recv_sem=recv_sem,
                                          device_id=(x, y, 1 - c), device_id_type=MESH)
        cp.start()
        cp.wait()

    return pl.pallas_call(
        body, name=name, in_specs=[_HBM], out_specs=_HBM,
        out_shape=jax.ShapeDtypeStruct((R, W), r.dtype),
        scratch_shapes=[pltpu.SemaphoreType.DMA(()), pltpu.SemaphoreType.DMA(())],
    )(r)


def _allreduce_small(vec):
    R, W = vec.shape

    def body(v_ref, out_ref, buf, send_sems, recv_sems):
        x, y, c = _me()
        me = 4 * x + 2 * y + c
        buf[me] = v_ref[...]
        cps = []
        for kk in range(1, 8):
            peer = (x ^ (kk >> 2), y ^ ((kk >> 1) & 1), c ^ (kk & 1))
            cp = pltpu.make_async_remote_copy(src_ref=v_ref, dst_ref=buf.at[me], send_sem=send_sems.at[kk - 1],
                                              recv_sem=recv_sems.at[kk - 1], device_id=peer, device_id_type=MESH)
            cp.start()
            cps.append(cp)
        for kk in range(1, 8):
            peer = (x ^ (kk >> 2), y ^ ((kk >> 1) & 1), c ^ (kk & 1))
            pltpu.make_async_remote_copy(src_ref=v_ref, dst_ref=buf.at[me ^ kk], send_sem=send_sems.at[kk - 1],
                                         recv_sem=recv_sems.at[kk - 1], device_id=peer, device_id_type=MESH).wait_recv()
        for cp in cps:
            cp.wait_send()
        acc = buf[0]
        for d in range(1, 8):
            acc = acc + buf[d]
        out_ref[...] = acc

    vm = pl.BlockSpec(memory_space=pltpu.VMEM)
    return pl.pallas_call(
        body, name="allreduce_small", in_specs=[vm], out_specs=vm, out_shape=jax.ShapeDtypeStruct((R, W), F32),
        scratch_shapes=[pltpu.VMEM((8, R, W), F32), pltpu.SemaphoreType.DMA((7,)), pltpu.SemaphoreType.DMA((7,))],
    )(vec)


def _pack_flat(parts, align, dtype):
    flat = jnp.concatenate([p.reshape(-1).astype(dtype) for p in parts])
    n = flat.shape[0]
    pad = (-n) % align
    return jnp.pad(flat, (0, pad)) if pad else flat


def _unpack_flat(flat, shapes):
    out, off = [], 0
    for s in shapes:
        n = math.prod(s)
        out.append(flat[off:off + n].reshape(s))
        off += n
    return out


def _full_from_shards(name, sh):
    if name in COL_SHARDED:
        return jnp.transpose(sh, (1, 0, 2)).reshape(sh.shape[1], 4 * sh.shape[2])
    return sh.reshape(4 * sh.shape[1], sh.shape[2])


def _shards_from_full(name, full):
    if name in COL_SHARDED:
        r, cc = full.shape[0], full.shape[1] // 4
        return jnp.transpose(full.reshape(r, 4, cc), (1, 0, 2)).reshape(4, r * cc)
    return full.reshape(4, -1)


def _pad_heads(w, width):
    k = w.shape[0]
    return jnp.pad(w.reshape(k, HEADS, width), ((0, 0), (0, 0), (0, HP - width))).reshape(k, HEADS * HP)


def _unpad_heads(w, width):
    k = w.shape[0]
    return w.reshape(k, HEADS, HP)[:, :, :width].reshape(k, HEADS * width)


def _blockdiag(t):
    g, a, b = t.shape
    return jnp.einsum("gab,gk->gakb", t, jnp.eye(g, dtype=t.dtype)).reshape(g * a, g * b)


def _blockdiag_t(m, a, b):
    g = m.shape[0] // a
    return jnp.einsum("gagb->gab", m.reshape(g, a, g, b))


def _local_step(x, mem, positions, target, W, sp, overlap=None):
    T = x.shape[0]
    row = lambda v: v.reshape(1, -1)

    ab_re, ab_im, bb_re, bb_im = _s5_discretize(sp["s5_lam_re"], sp["s5_lam_im"], sp["s5_log_dt"], sp["s5_b_re"], sp["s5_b_im"])
    ar, ai = ab_re.reshape(1, S5_N), ab_im.reshape(1, S5_N)
    tab_fwd, tab_rev = _s5_tables(ar, ai, False), _s5_tables(ar, -ai, True)
    b_st = jnp.concatenate([_blockdiag(jnp.swapaxes(bb_re, 1, 2)), _blockdiag(jnp.swapaxes(bb_im, 1, 2))],
                           axis=1).astype(BF16)
    c_st = jnp.concatenate([_blockdiag(jnp.swapaxes(sp["s5_c_re"], 1, 2)),
                            -_blockdiag(jnp.swapaxes(sp["s5_c_im"], 1, 2))], axis=0).astype(BF16)
    inv = ROPE_THETA ** (-jnp.arange(0, ROPE, 2, dtype=F32) / ROPE)
    ang = positions.astype(F32)[:, None] * inv
    cs, sn = jnp.cos(ang), jnp.sin(ang)
    one, zero = jnp.ones((T, NOPE), F32), jnp.zeros((T, NOPE), F32)
    z16, tail1, tail0 = jnp.zeros((T, ROPE // 2), F32), jnp.ones((T, HP - QK), F32), jnp.zeros((T, HP - QK), F32)
    t_cos = jnp.concatenate([one, cs, cs, tail1], axis=1)
    t_s1 = jnp.concatenate([zero, z16, sn, tail0], axis=1)
    t_s2 = jnp.concatenate([zero, -sn, z16, tail0], axis=1)

    w_in = W["w_in"]
    w_in_re = jnp.concatenate([w_in[:, 800:], w_in[:, :768], w_in[:, 768:800],
                               jnp.zeros((D_MODEL, HP - ROPE), w_in.dtype)], axis=1)
    w_uq_p = _pad_heads(W["w_uq"], QK)
    wkv = W["w_ukv"].reshape(S5_W, HEADS, NOPE + VD)
    w_uk_p = _pad_heads(wkv[:, :, :NOPE].reshape(S5_W, HEADS * NOPE), NOPE)
    w_uv_p = _pad_heads(wkv[:, :, NOPE:].reshape(S5_W, HEADS * VD), VD)
    w_oa_p = jnp.pad(W["w_oa"].reshape(HEADS, VD, D_MODEL), ((0, 0), (0, HP - VD), (0, 0))).reshape(HEADS * HP, D_MODEL)

    g_in, b_in = row(sp["ln_in_g"]), row(sp["ln_in_b"])
    g1, b1, g2, b2, g3, b3 = (row(sp[k]) for k in ("ln1_g", "ln1_b", "ln2_g", "ln2_b", "ln3_g", "ln3_b"))
    gq, gkv, d_skip = row(sp["q_norm_g"]), row(sp["kv_norm_g"]), row(sp["s5_d"])

    (h0,) = _rowwise(lambda a, g, b: (_ln(a, g, b),), "ln_in_fwd", [_whole(x)], [g_in, b_in], [(D_MODEL, F32)], tm=512)
    z = _mm(h0, w_in_re, "in_proj", tm=512, tn=Z_W, out_dtype=BF16)
    hs, yc, yg = _s5_fwd(z, b_st, c_st, d_skip, tab_fwd, "s5_fwd")
    y2 = _mm(yg, W["w_glu"], "glu_proj", tn=2048, out_dtype=BF16)
    cqn, ckvn = _rowwise(lambda a, b, ga, gb: (_rms(a, ga), _rms(b, gb)), "mla_norm_fwd",
                         [(z, Z_CQ, S5_W), (z, Z_CKV, S5_W)], [gq, gkv], [(S5_W, BF16)] * 2)
    tabs = [(t_cos, 0, HP), (t_s1, 0, HP), (t_s2, 0, HP)]

    def rope_q(acc, tc_, t1, t2):
        return _rope(acc, _tile8(tc_), _tile8(t1), _tile8(t2)) * (ATT_SCALE * LOG2E)

    def rope_k(acc, krz, tc_, t1, t2):
        return acc + _tile8(_rope(pltpu.roll(krz.astype(F32), NOPE, 1), tc_, t1, t2))

    q = _mm(cqn, w_uq_p, "mla_uq_rope", epi=rope_q, rextras=tabs, out_dtype=BF16)
    k = _mm(ckvn, w_uk_p, "mla_uk_rope", epi=rope_k, rextras=[(z, Z_KR, HP)] + tabs, out_dtype=BF16)
    v = _mm(ckvn, w_uv_p, "mla_uv", out_dtype=BF16)
    if overlap is None:
        o, lse = _flash_fwd(q, k, v, "mla_attn_fwd")
    else:
        o, lse, landed = _flash_fwd(q, k, v, "mla_attn_fwd", carry=(overlap[0], False))
        W = {**W, **overlap[1](landed)}
    def gate_epi(acc, gs_, ga_, ya, yb):
        f = lambda t: t.astype(F32)
        return acc, _gate_mix(f(gs_), f(ga_), f(ya), f(yb), acc.astype(BF16).astype(F32))

    a_out, mixin = _mm(o, w_oa_p, "mla_oa_gate", epi=gate_epi, outs=[(D_MODEL, BF16)] * 2, tm=512,
                       rextras=[(z, Z_GS, D_MODEL), (z, Z_GA, D_MODEL), (y2, 0, D_MODEL), (y2, D_MODEL, D_MODEL)])
    def ln_epi(acc, h, g, b):
        s_ = DN_ALPHA * h + acc
        return s_, _ln(s_, g, b)

    two_rows = [(D_MODEL, F32)] * 2
    s1, h1 = _mm(mixin, W["w_o"], "mix_o_ln1", epi=ln_epi, extras=(h0,), pextras=(g1, b1), outs=two_rows, tm=512)
    xq = _mm(h1, W["w_xq"], "x_q", out_dtype=BF16)
    xk = _mm(mem, W["w_xk"], "x_k", out_dtype=BF16)
    xv = _mm(mem, W["w_xv"], "x_v", out_dtype=BF16)
    (xo,) = _rowwise(_xattn_fwd_fn, "xattn_fwd", [_whole(xq)], [xk, xv], [(D_MODEL, BF16)], tm=1024, upcast=False)
    s2, h2 = _mm(xo, W["w_xo"], "x_o_ln2", epi=ln_epi, extras=(h1,), pextras=(g2, b2), outs=two_rows, tm=512)
    a_up = _mm(h2, W["w_up"], "mlp_up", out_dtype=BF16)
    def loss_epi(acc, h, tgt, g, b):
        def f(r_, g_, b_):
            e = _ln_res(h, r_, g_, b_) - tgt
            return 0.5 * jnp.sum(jnp.mean(e * e, axis=-1))
        lv, (dr_, dg_, db_) = jax.value_and_grad(f, argnums=(0, 1, 2))(acc, g, b)
        return dr_, dg_, db_, jnp.broadcast_to(lv, (1, 128))

    row_and_sums = [(D_MODEL, F32), (D_MODEL, "sum"), (D_MODEL, "sum")]
    dr3, dg3, db3, lossv = _mm(a_up, W["w_down"], "mlp_down_loss", act=_relu2, tm=512, tk=2048, epi=loss_epi,
                               extras=(h2, target), pextras=(g3, b3), outs=row_and_sums + [(128, "sum")])
    gW, gs = {}, {"ln3_g": dg3, "ln3_b": db3}

    da = _mm(dr3, W["w_down"], "mlp_down_bwd_a", "nt", epi=lambda acc, a: acc * (2.0 * jnp.maximum(a, 0.0)), extras=(a_up,),
             out_dtype=BF16)
    gW["w_down"] = _mm(a_up, dr3, "mlp_down_bwd_w", "tn", m=4 * D_MODEL, n=D_MODEL, act=_relu2, tk=2048)
    gW["w_up"] = _mm(h2, da, "mlp_up_bwd_w", "tn", m=D_MODEL, n=4 * D_MODEL, tk=2048)

    def ln_bwd_epi(acc, e, s_, g, b):
        _, vjp = jax.vjp(_ln, s_, g, b)
        return vjp(acc + DN_ALPHA * e)

    dr2, gs["ln2_g"], gs["ln2_b"] = _mm(da, W["w_up"], "mlp_up_bwd_ln2", "nt", tm=512, tk=2048, epi=ln_bwd_epi,
                                        extras=(dr3, s2), pextras=(g2, b2), outs=row_and_sums)
    dxo = _mm(dr2, W["w_xo"], "x_o_bwd_a", "nt", out_dtype=BF16)
    gW["w_xo"] = _mm(xo, dr2, "x_o_bwd_w", "tn", m=D_MODEL, n=D_MODEL)
    dxq, dxk, dxv = _rowwise(_xattn_bwd_fn, "xattn_bwd", [_whole(xq), _whole(dxo)], [xk, xv], [(D_MODEL, BF16)],
                             [(xk.shape[0], D_MODEL)] * 2, tm=1024, upcast=False)
    gW["w_xq"] = _mm(h1, dxq, "x_q_bwd_w", "tn", m=D_MODEL, n=D_MODEL)
    gW["w_xk"] = _mm(mem, dxk, "x_k_bwd_w", "tn", m=D_MODEL, n=D_MODEL)
    gW["w_xv"] = _mm(mem, dxv, "x_v_bwd_w", "tn", m=D_MODEL, n=D_MODEL)

    dr1, gs["ln1_g"], gs["ln1_b"] = _mm(dxq, W["w_xq"], "x_q_bwd_ln1", "nt", tm=512, epi=ln_bwd_epi,
                                        extras=(dr2, s1), pextras=(g1, b1), outs=row_and_sums)
    gW["w_o"] = _mm(mixin, dr1, "mix_o_bwd_w", "tn", m=D_MODEL, n=D_MODEL)

    def gate_bwd_epi(acc, *tiles):
        _, vjp = jax.vjp(_gate_mix, *[t.astype(F32) for t in tiles])
        return vjp(acc)

    dgs, dga, dy2a, dy2b, da_out = _mm(
        dr1, W["w_o"], "mix_o_bwd_gate", "nt", tm=512, epi=gate_bwd_epi, outs=[(D_MODEL, BF16)] * 5,
        rextras=[(z, Z_GS, D_MODEL), (z, Z_GA, D_MODEL), (y2, 0, D_MODEL), (y2, D_MODEL, D_MODEL), _whole(a_out)])

    def delta_epi(acc, o_):
        prod = acc * o_
        cols = [jnp.sum(prod[:, h * HP:(h + 1) * HP], axis=1, keepdims=True) for h in range(HEADS)]
        return acc, jnp.concatenate(cols, axis=1)

    do16, delta = _mm(da_out, w_oa_p, "mla_oa_bwd_a", "nt", epi=delta_epi, extras=(o,),
                      outs=[(HEADS * HP, BF16), (HEADS, F32)], tm=512)
    g_oa_p = _mm(o, da_out, "mla_oa_bwd_w", "tn", m=HEADS * HP, n=D_MODEL)
    gW["w_oa"] = g_oa_p.reshape(HEADS, HP, D_MODEL)[:, :VD].reshape(HEADS * VD, D_MODEL)
    delta_row = delta.T.reshape(HEADS, 1, T)
    if overlap is None:
        dq, dk, dv = _flash_bwd(q, k, v, do16, lse, delta_row, "mla_attn_bwd")
        riding = None
    else:
        sent = overlap[2]({n: gW[n] for n in BIG_LATE})
        dq, dk, dv, landed = _flash_bwd(q, k, v, do16, lse, delta_row, "mla_attn_bwd", carry=(sent, True))
        riding = (sent, landed)

    def rope_bwd(dq_, dk_, tc_, t1, t2):
        dqr = _rope_t(dq_, _tile8(tc_), _tile8(t1), _tile8(t2))
        dkr = dk_[:, 0:HP]
        for hh in range(1, HEADS):
            dkr = dkr + dk_[:, hh * HP:(hh + 1) * HP]
        dkr = _rope_t(jnp.where(_rope_lanes(dkr.shape), dkr, 0.0), tc_, t1, t2)
        dkr = pltpu.roll(dkr, NOPE, 1)
        lane = lax.broadcasted_iota(jnp.int32, dkr.shape, 1)
        return dqr, jnp.where(lane < ROPE, dkr, 0.0)

    dq_raw, dkr = _rowwise(rope_bwd, "mla_rope_bwd", [_whole(dq), _whole(dk), _whole(t_cos), _whole(t_s1), _whole(t_s2)], [],
                           [(HEADS * HP, BF16), (HP, BF16)])
    dcqn = _mm(dq_raw, w_uq_p, "mla_uq_bwd_a", "nt")
    gW["w_uq"] = _unpad_heads(_mm(cqn, dq_raw, "mla_uq_bwd_w", "tn", m=S5_W, n=HEADS * HP), QK)
    dckvn_k = _mm(dk, w_uk_p, "mla_uk_bwd_a", "nt")
    dckvn = _mm(dv, w_uv_p, "mla_uv_bwd_a", "nt", epi=lambda acc, e: acc + e, extras=(dckvn_k,))
    g_uk = _unpad_heads(_mm(ckvn, dk, "mla_uk_bwd_w", "tn", m=S5_W, n=HEADS * HP), NOPE).reshape(S5_W, HEADS, NOPE)
    g_uv = _unpad_heads(_mm(ckvn, dv, "mla_uv_bwd_w", "tn", m=S5_W, n=HEADS * HP), VD).reshape(S5_W, HEADS, VD)
    gW["w_ukv"] = jnp.concatenate([g_uk, g_uv], axis=2).reshape(S5_W, HEADS * (NOPE + VD))

    def norm_bwd(a, b, da_, db_, ga, gb):
        _, vja = jax.vjp(_rms, a, ga)
        _, vjb = jax.vjp(_rms, b, gb)
        dxa, dga_ = vja(da_)
        dxb, dgb_ = vjb(db_)
        return dxa, dxb, dga_, dgb_

    dcq, dckv, gs["q_norm_g"], gs["kv_norm_g"] = _rowwise(
        norm_bwd, "mla_norm_bwd", [(z, Z_CQ, S5_W), (z, Z_CKV, S5_W), _whole(dcqn), _whole(dckvn)], [gq, gkv],
        [(S5_W, BF16)] * 2, [(1, S5_W)] * 2)

    dy2 = jnp.concatenate([dy2a, dy2b], axis=1)
    dyg = _mm(dy2, W["w_glu"], "glu_bwd_a", "nt")
    gW["w_glu"] = _mm(yg, dy2, "glu_bwd_w", "tn", m=S5_W, n=2 * D_MODEL)

    du, gs["s5_ab"], g_bst, g_cst_t, gs["s5_d"] = _s5_bwd(dyg, yc, z, hs, b_st, c_st, d_skip, tab_rev, "s5_bwd")
    gs["s5_bb_re"] = jnp.swapaxes(_blockdiag_t(g_bst[:, :S5_N], S5_H, S5_P), 1, 2)
    gs["s5_bb_im"] = jnp.swapaxes(_blockdiag_t(g_bst[:, S5_N:], S5_H, S5_P), 1, 2)
    gs["s5_c_re"] = _blockdiag_t(g_cst_t[:, :S5_N], S5_H, S5_P)
    gs["s5_c_im"] = -_blockdiag_t(g_cst_t[:, S5_N:], S5_H, S5_P)

    dz = jnp.concatenate([dgs, dga, du, dcq, dckv, dkr], axis=1)
    g_in_re = _mm(h0, dz, "in_proj_bwd_w", "tn", m=D_MODEL, n=Z_W, tm=512, tn=Z_W)
    gW["w_in"] = jnp.concatenate([g_in_re[:, Z_U:Z_KR], g_in_re[:, Z_KR:Z_KR + ROPE], g_in_re[:, :Z_U]], axis=1)
    dx, gs["ln_in_g"], gs["ln_in_b"] = _mm(dz, w_in_re, "in_proj_bwd_ln_in", "nt", tm=512, tk=Z_W, epi=ln_bwd_epi,
                                           extras=(dr1, x), pextras=(g_in, b_in), outs=row_and_sums)
    return lossv[:, :1], dx, gW, gs, riding


_RAW_SMALL = (("loss", (1, 1)), ("ln_in_g", (1, D_MODEL)), ("ln_in_b", (1, D_MODEL)), ("ln1_g", (1, D_MODEL)),
              ("ln1_b", (1, D_MODEL)), ("ln2_g", (1, D_MODEL)), ("ln2_b", (1, D_MODEL)), ("ln3_g", (1, D_MODEL)),
              ("ln3_b", (1, D_MODEL)), ("q_norm_g", (1, S5_W)), ("kv_norm_g", (1, S5_W)), ("s5_d", (1, S5_W)),
              ("s5_ab", (1, 2 * S5_N)), ("s5_bb_re", (S5_G, S5_P, S5_H)), ("s5_bb_im", (S5_G, S5_P, S5_H)),
              ("s5_c_re", (S5_G, S5_H, S5_P)), ("s5_c_im", (S5_G, S5_H, S5_P)))


def kernel(x, mem, positions, ln_in_g, ln_in_b, w_in, s5_lam_re, s5_lam_im, s5_log_dt, s5_b_re, s5_b_im, s5_c_re, s5_c_im, s5_d, w_glu, q_norm_g, w_uq, kv_norm_g, w_ukv, w_oa, w_o, ln1_g, ln1_b, w_xq, w_xk, w_xv, w_xo, ln2_g, ln2_b, w_up, w_down, ln3_g, ln3_b, loss_target, m_ln_in_g, m_ln_in_b, m_w_in, m_s5_lam_re, m_s5_lam_im, m_s5_log_dt, m_s5_b_re, m_s5_b_im, m_s5_c_re, m_s5_c_im, m_s5_d, m_w_glu, m_q_norm_g, m_w_uq, m_kv_norm_g, m_w_ukv, m_w_oa, m_w_o, m_ln1_g, m_ln1_b, m_w_xq, m_w_xk, m_w_xv, m_w_xo, m_ln2_g, m_ln2_b, m_w_up, m_w_down, m_ln3_g, m_ln3_b, v_ln_in_g, v_ln_in_b, v_w_in, v_s5_lam_re, v_s5_lam_im, v_s5_log_dt, v_s5_b_re, v_s5_b_im, v_s5_c_re, v_s5_c_im, v_s5_d, v_w_glu, v_q_norm_g, v_w_uq, v_kv_norm_g, v_w_ukv, v_w_oa, v_w_o, v_ln1_g, v_ln1_b, v_w_xq, v_w_xk, v_w_xv, v_w_xo, v_ln2_g, v_ln2_b, v_w_up, v_w_down, v_ln3_g, v_ln3_b):
    a = dict(locals())
    wts = {n: a[n] for n in WEIGHTS}
    ms = {n: a["m_" + n] for n in WEIGHTS}
    vs = {n: a["v_" + n] for n in WEIGHTS}
    shard2d = {n: wts[n].reshape(wts[n].shape[-2], wts[n].shape[-1]) for n in BIG}
    nrow = {n: shard2d[n].size // 1024 for n in BIG}
    mine = 2 * lax.axis_index("x") + lax.axis_index("y")
    my_c = lax.axis_index("c")
    cidx = my_c.astype(jnp.int32).reshape(1)

    def group_rows(group):
        rows = sum(nrow[n] for n in group)
        return rows, (-rows) % PACK_ROWS

    def pack_shards(group):
        rows, pad = group_rows(group)
        parts = [shard2d[n].astype(BF16).reshape(nrow[n], 1024) for n in group] + [jnp.zeros((pad, 1024), BF16)]
        return jnp.concatenate(parts, axis=0).reshape(2, (rows + pad) // 2, 1024)

    def unpack_full(group, gathered):
        out, off = {}, 0
        for n in group:
            r, cc = shard2d[n].shape
            piece = lax.optimization_barrier(gathered[:, off:off + nrow[n]])
            out[n] = _full_from_shards(n, piece.reshape(4, r, cc))
            off += nrow[n]
        return out

    def pack_grads(group, gw):
        rows, pad = group_rows(group)
        parts = [_shards_from_full(n, gw[n]).reshape(4, nrow[n], 1024) for n in group] + [jnp.zeros((4, pad, 1024), F32)]
        return jnp.concatenate(parts, axis=1).reshape(4, 2, (rows + pad) // 2, 1024)

    def by_core(own, other):
        return jnp.where(my_c == 0, jnp.stack([own, other]), jnp.stack([other, own]))

    packed_e, packed_l = pack_shards(BIG_EARLY), pack_shards(BIG_LATE)
    gathered = lax.dynamic_update_slice(_gather_weights(packed_e), packed_e[None], (mine, 0, 0, 0))
    W = unpack_full(BIG_EARLY, gathered.reshape(4, -1, 1024))

    def finish_gather(landed):
        half = packed_l.shape[1]
        own_half = lax.dynamic_index_in_dim(packed_l, my_c, axis=0, keepdims=True)
        landed = lax.dynamic_update_slice(landed, own_half, (mine, 0, 0))
        other = _pair_share(landed.reshape(4 * half, 1024), "gather_late_share").reshape(4, half, 1024)
        both = jnp.moveaxis(by_core(landed, other), 0, 1)
        return unpack_full(BIG_LATE, both.reshape(4, 2 * half, 1024))

    def pair_reduce(group, gw, tag):
        gflat = pack_grads(group, gw)
        return _pair_add(gflat, _pair_exchange(gflat, "grad_pair_exchange_" + tag), cidx, "grad_pair_add_" + tag)

    def finish_reduce(group, sent, landed, tag):
        own = lax.dynamic_slice(sent, (mine, 0, 0), (1,) + sent.shape[1:])
        red = _chip_sum(lax.dynamic_update_slice(landed, own, (mine, 0, 0)), "grad_chip_sum_" + tag)
        gsh = by_core(red, _pair_share(red, "grad_pair_share_" + tag)).reshape(-1, 1024)
        out, off = {}, 0
        for n in group:
            out[n] = lax.optimization_barrier(gsh[off:off + nrow[n]]).reshape(shard2d[n].shape)
            off += nrow[n]
        return out

    sp = {n: wts[n] for n in SMALL}
    sp_local = {n: (sp[n][0] if sp[n].ndim > 1 else sp[n]) for n in SMALL}
    overlap = (packed_l, finish_gather, lambda gw: pair_reduce(BIG_LATE, gw, "late"))
    lossv, dx, gW, gs, riding = _local_step(x[0], mem[0], positions[0], loss_target[0], W, sp_local, overlap)

    g_out = finish_reduce(BIG_LATE, riding[0], riding[1], "late")
    sent_e = pair_reduce(BIG_EARLY, gW, "early")
    g_out.update(finish_reduce(BIG_EARLY, sent_e, _chip_exchange(sent_e, "grad_chip_exchange_early"), "early"))

    gs["loss"] = lossv
    raw = _pack_flat([gs[n].reshape(s) for n, s in _RAW_SMALL], 8 * 1024, F32)
    raw = _allreduce_small(raw.reshape(-1, 1024)).reshape(-1)
    rs = dict(zip([n for n, _ in _RAW_SMALL], _unpack_flat(raw, [s for _, s in _RAW_SMALL])))
    loss = rs["loss"].reshape(())
    _, disc_vjp = jax.vjp(_s5_discretize, sp_local["s5_lam_re"], sp_local["s5_lam_im"], sp_local["s5_log_dt"],
                          sp_local["s5_b_re"], sp_local["s5_b_im"])
    d_ab = rs["s5_ab"].reshape(2, S5_G, S5_P)
    g_lre, g_lim, g_ldt, g_bre, g_bim = disc_vjp((d_ab[0], d_ab[1], rs["s5_bb_re"], rs["s5_bb_im"]))
    small_g = {"s5_lam_re": g_lre, "s5_lam_im": g_lim, "s5_log_dt": g_ldt, "s5_b_re": g_bre, "s5_b_im": g_bim,
               "s5_c_re": rs["s5_c_re"], "s5_c_im": rs["s5_c_im"]}
    for n in ("ln_in_g", "ln_in_b", "ln1_g", "ln1_b", "ln2_g", "ln2_b", "ln3_g", "ln3_b", "q_norm_g", "kv_norm_g", "s5_d"):
        small_g[n] = rs[n]

    grads, deltas, new_m, new_v = {}, {}, {}, {}
    for n in BIG:
        d_, m_, v_ = _adamw(shard2d[n], g_out[n], ms[n].reshape(shard2d[n].shape), vs[n].reshape(shard2d[n].shape), "adamw_" + n)
        grads[n] = g_out[n].reshape(wts[n].shape)
        deltas[n], new_m[n], new_v[n] = (t.reshape(wts[n].shape) for t in (d_, m_, v_))
    as2d = lambda t: t.reshape(-1, t.shape[-1])
    sd, sm, sv = _adamw_many([as2d(wts[n]) for n in SMALL], [as2d(small_g[n].reshape(wts[n].shape)) for n in SMALL],
                             [as2d(ms[n]) for n in SMALL], [as2d(vs[n]) for n in SMALL], "adamw_small")
    for n, d_, m_, v_ in zip(SMALL, sd, sm, sv):
        grads[n] = small_g[n].reshape(wts[n].shape)
        deltas[n], new_m[n], new_v[n] = (t.reshape(wts[n].shape) for t in (d_, m_, v_))

    return (loss, dx[None], *[grads[n] for n in WEIGHTS], *[deltas[n] for n in WEIGHTS],
            *[new_m[n] for n in WEIGHTS], *[new_v[n] for n in WEIGHTS])
```

```python
import functools
import math

import jax
import jax.numpy as jnp
from jax import lax
from jax.experimental import pallas as pl
from jax.experimental.pallas import tpu as pltpu

F32 = jnp.float32
BF16 = jnp.bfloat16
MESH = pl.DeviceIdType.MESH

D_MODEL = 1024
S5_W = 256
S5_G = 16
S5_H = 16
S5_P = 64
S5_N = S5_G * S5_P
S5_MAX_RE = -1e-4
HEADS = 8
NOPE = 64
ROPE = 32
QK = NOPE + ROPE
VD = 64
HP = 128
XH = 4
XD = 256
LN_EPS = 1e-5
RMS_EPS = 1e-6
NEG_INF = -1e30
DN_ALPHA = 2.0 ** 0.25
ROPE_THETA = 10000.0
ADAM_LR, ADAM_B1, ADAM_B2, ADAM_EPS, ADAM_WD, ADAM_STEP = 0.001, 0.9, 0.999, 1e-08, 0.01, 10

Z_GS, Z_GA, Z_U, Z_CQ, Z_CKV, Z_KR, Z_W = 0, 1024, 2048, 2304, 2560, 2816, 2944
SCAN_CHUNK = 256

BIG_EARLY = ("w_in", "w_glu", "w_uq", "w_ukv", "w_oa", "w_o")
BIG_LATE = ("w_up", "w_xq", "w_xk", "w_xv", "w_xo", "w_down")
BIG = BIG_EARLY + BIG_LATE
COL_SHARDED = ("w_in", "w_glu", "w_uq", "w_ukv", "w_oa", "w_up")
SMALL = ("ln_in_g", "ln_in_b", "s5_lam_re", "s5_lam_im", "s5_log_dt", "s5_b_re", "s5_b_im", "s5_c_re", "s5_c_im",
         "s5_d", "q_norm_g", "kv_norm_g", "ln1_g", "ln1_b", "ln2_g", "ln2_b", "ln3_g", "ln3_b")
WEIGHTS = ("ln_in_g", "ln_in_b", "w_in", "s5_lam_re", "s5_lam_im", "s5_log_dt", "s5_b_re", "s5_b_im", "s5_c_re",
           "s5_c_im", "s5_d", "w_glu", "q_norm_g", "w_uq", "kv_norm_g", "w_ukv", "w_oa", "w_o", "ln1_g", "ln1_b",
           "w_xq", "w_xk", "w_xv", "w_xo", "ln2_g", "ln2_b", "w_up", "w_down", "ln3_g", "ln3_b")
PACK_ROWS = 2 * 16


def _row_tile(n, cap=512):
    best = n
    for t in range(8, min(n, cap) + 1, 8):
        if n % t == 0:
            best = t
    return best


def _pick(n, cap):
    best = None
    for t in range(128, min(n, cap) + 1, 128):
        if n % t == 0:
            best = t
    return best if best is not None and (best >= 512 or best == n) else n


def _mm(a, b, name, mode="nn", *, a_off=0, b_off=0, m=None, n=None, act=None, epi=None, extras=(), rextras=(),
        pextras=(), outs=None, out_dtype=F32, tm=1024, tn=1024, tk=None):
    if mode == "nn":
        M, (K, N) = a.shape[0], b.shape
    elif mode == "nt":
        M, (N, K) = a.shape[0], b.shape
    else:
        K, M, N = a.shape[0], m, n
    if mode == "tn":
        tm, tn, tk = _pick(M, tm), _pick(N, tn), min(tk or 1024, K)
    else:
        tm, tn, tk = min(tm, M), _pick(N, tn), _pick(K, tk or 1024)
    assert M % tm == 0 and N % tn == 0 and K % tk == 0, (name, M, N, K, tm, tn, tk)
    nk = K // tk
    if mode == "tn":
        assert a_off % tm == 0 and b_off % tn == 0
        ao, bo = a_off // tm, b_off // tn
        a_spec = pl.BlockSpec((tk, tm), lambda i, j, k: (k, i + ao))
        b_spec = pl.BlockSpec((tk, tn), lambda i, j, k: (k, j + bo))
        dims = (((0,), (0,)), ((), ()))
    else:
        assert a_off % tk == 0
        ao = a_off // tk
        a_spec = pl.BlockSpec((tm, tk), lambda i, j, k: (i, k + ao))
        if mode == "nn":
            b_spec = pl.BlockSpec((tk, tn), lambda i, j, k: (k, j))
            dims = (((1,), (0,)), ((), ()))
        else:
            b_spec = pl.BlockSpec((tn, tk), lambda i, j, k: (j, k))
            dims = (((1,), (1,)), ((), ()))
    e_spec = pl.BlockSpec((tm, tn), lambda i, j, k: (i, j))
    p_spec = pl.BlockSpec((1, tn), lambda i, j, k: (0, j))
    r_specs = [pl.BlockSpec((tm, w), functools.partial(lambda i, j, k, o: (i, o), o=off // w)) for _, off, w in rextras]
    n_extra = len(extras) + len(rextras) + len(pextras)
    if outs is None:
        o_specs, o_shapes = [e_spec], [jax.ShapeDtypeStruct((M, N), out_dtype)]
    else:
        assert all((w == tn and dt != "sum") or N == tn for w, dt in outs), name
        o_specs = [e_spec if w == tn and dt != "sum" else pl.BlockSpec((8 if dt == "sum" else tm, w), lambda i, j, k: (i, 0))
                   for w, dt in outs]
        o_shapes = [jax.ShapeDtypeStruct((M // tm * 8, w), F32) if dt == "sum" else
                    jax.ShapeDtypeStruct((M, N if w == tn else w), dt) for w, dt in outs]
    n_out = len(o_specs)
    direct = outs is None and epi is None and out_dtype == F32

    def body(*refs):
        a_ref, b_ref = refs[0], refs[1]
        e_refs = refs[2:2 + n_extra]
        o_refs = refs[2 + n_extra:2 + n_extra + n_out]
        av = a_ref[...]
        if act is not None:
            av = act(av.astype(F32))
        p = lax.dot_general(av.astype(BF16), b_ref[...].astype(BF16), dims, preferred_element_type=F32)

        def finish(r):
            if epi is not None:
                r = epi(r, *[e[...] for e in e_refs])
            for o_ref, v in zip(o_refs, r if isinstance(r, tuple) else (r,)):
                o_ref[...] = jnp.broadcast_to(v, o_ref.shape).astype(o_ref.dtype)

        if nk == 1:
            finish(p)
        else:
            acc = o_refs[0] if direct else refs[2 + n_extra + n_out]
            k = pl.program_id(2)

            @pl.when(k == 0)
            def _():
                acc[...] = p

            @pl.when(k > 0)
            def _():
                acc[...] += p

            if not direct:
                @pl.when(k == nk - 1)
                def _():
                    finish(acc[...])

    res = pl.pallas_call(
        body, name=name, grid=(M // tm, N // tn, nk),
        in_specs=[a_spec, b_spec] + [e_spec] * len(extras) + r_specs + [p_spec] * len(pextras), out_specs=o_specs,
        out_shape=o_shapes,
        scratch_shapes=[pltpu.VMEM((tm, tn), F32)] if nk > 1 and not direct else [],
        compiler_params=pltpu.CompilerParams(dimension_semantics=("parallel", "parallel", "arbitrary")),
    )(a, b, *extras, *[r[0] for r in rextras], *pextras)
    if outs is None:
        return res[0]
    return [r.reshape(M // tm, 8, -1)[:, 0].sum(axis=0, keepdims=True) if dt == "sum" else r for r, (_, dt) in zip(res, outs)]


def _rowwise(fn, name, rows, pars, outs, accs=(), tm=256, upcast=True):
    T = rows[0][0].shape[0]
    tm = min(tm, T)
    assert T % tm == 0
    n_in, n_out = len(rows) + len(pars), len(outs)
    in_specs = []
    for arr, off, w in rows:
        assert off % w == 0 and arr.shape[0] == T, name
        in_specs.append(pl.BlockSpec((tm, w), functools.partial(lambda i, o: (i, o), o=off // w)))
    for p in pars:
        in_specs.append(pl.BlockSpec(p.shape, lambda i: (0, 0)))
    out_specs = [pl.BlockSpec((tm, w), lambda i: (i, 0)) for w, _ in outs]
    out_specs += [pl.BlockSpec(s, lambda i: (0, 0)) for s in accs]
    out_shape = [jax.ShapeDtypeStruct((T, w), dt) for w, dt in outs] + [jax.ShapeDtypeStruct(s, F32) for s in accs]

    def body(*refs):
        res = fn(*[r[...].astype(F32) if upcast else r[...] for r in refs[:n_in]])
        o_refs = refs[n_in:]
        for r, v in zip(o_refs[:n_out], res[:n_out]):
            r[...] = v.astype(r.dtype)
        if accs:
            i = pl.program_id(0)

            @pl.when(i == 0)
            def _():
                for r in o_refs[n_out:]:
                    r[...] = jnp.zeros_like(r)

            for r, v in zip(o_refs[n_out:], res[n_out:]):
                r[...] += v

    res = pl.pallas_call(
        body, name=name, grid=(T // tm,), in_specs=in_specs, out_specs=out_specs, out_shape=out_shape,
        compiler_params=pltpu.CompilerParams(dimension_semantics=("arbitrary",)),
    )(*[r[0] for r in rows], *pars)
    return res


def _whole(arr):
    return (arr, 0, arr.shape[1])


def _ln(x, g, b):
    mu = jnp.mean(x, axis=-1, keepdims=True)
    xc = x - mu
    var = jnp.mean(xc * xc, axis=-1, keepdims=True)
    return xc * lax.rsqrt(var + LN_EPS) * g + b


def _ln_res(h, r, g, b):
    return _ln(DN_ALPHA * h + r, g, b)


def _rms(x, g):
    return x * lax.rsqrt(jnp.mean(x * x, axis=-1, keepdims=True) + RMS_EPS) * g


def _gelu_skip(yc, u, d):
    y = yc + d * u
    return 0.5 * y * (1.0 + lax.erf(y * (1.0 / math.sqrt(2.0))))


def _gate_mix(gs, ga, y2a, y2b, aout):
    return jax.nn.sigmoid(gs) * (y2a * jax.nn.sigmoid(y2b)) + jax.nn.sigmoid(ga) * aout


def _relu2(a):
    r = jnp.maximum(a, 0.0)
    return r * r


def _tile8(t):
    return jnp.concatenate([t] * HEADS, axis=1)


def _rope(x, cos, s1, s2):
    w = x.shape[1]
    return x * cos + pltpu.roll(x, ROPE // 2, 1) * s1 + pltpu.roll(x, w - ROPE // 2, 1) * s2


def _rope_t(dy, cos, s1, s2):
    w = dy.shape[1]
    return dy * cos + pltpu.roll(dy * s1, w - ROPE // 2, 1) + pltpu.roll(dy * s2, ROPE // 2, 1)


def _rope_lanes(shape):
    lane = lax.broadcasted_iota(jnp.int32, shape, 1) % HP
    return (lane >= NOPE) & (lane < QK)


ATT_HB_FWD, ATT_HB_BWD = 8, 4
ATT_SCALE = QK ** -0.5
LOG2E = 1.4426950408889634
LN2 = 0.6931471805599453
_NT = (((1,), (1,)), ((), ()))
_TN = (((0,), (0,)), ((), ()))


def _carry_specs(carry):
    if carry is None:
        return [], [], [], []
    arr = carry[0]
    return ([_HBM], [_HBM], [jax.ShapeDtypeStruct((4,) + arr.shape[1:], arr.dtype)],
            [pltpu.SemaphoreType.DMA((3,)), pltpu.SemaphoreType.DMA((3,))])


def _carry_start(exchange, when):
    @pl.when(when)
    def _():
        for cp in exchange(False):
            cp.start()


def _carry_wait(exchange, when):
    @pl.when(when)
    def _():
        for cp in exchange(True):
            cp.wait_recv()
        for cp in exchange(False):
            cp.wait_send()


def _tri_tables(n, by_row):
    if by_row:
        pairs = [(i, j) for i in range(n) for j in range(i + 1)]
    else:
        pairs = [(i, j) for j in range(n) for i in range(j, n)]
    return jnp.array([p[0] for p in pairs], jnp.int32), jnp.array([p[1] for p in pairs], jnp.int32)


def _flash_fwd(q, k, v, name, bq=512, carry=None):
    T = q.shape[0]
    bq = min(bq, T)
    nq = T // bq
    hb = ATT_HB_FWD
    n_carry = 0 if carry is None else 1

    i_tab, j_tab = _tri_tables(nq, by_row=True)

    def body(it_ref, jt_ref, q_ref, k_ref, v_ref, *rest):
        o_ref, lse_ref = rest[n_carry:n_carry + 2]
        m_sc, l_sc, acc_sc = rest[2 * n_carry + 2:2 * n_carry + 5]
        t = pl.program_id(1)
        i, j = it_ref[t], jt_ref[t]
        if carry is not None:
            exchange = functools.partial(_carried_exchange, rest[0], rest[3], rest[-2], rest[-1], carry[1])
            _carry_start(exchange, (pl.program_id(0) == 0) & (t == 0))

        @pl.when(j == 0)
        def _():
            m_sc[...] = jnp.full_like(m_sc, NEG_INF)
            l_sc[...] = jnp.zeros_like(l_sc)
            acc_sc[...] = jnp.zeros_like(acc_sc)

        def step(masked):
            for hh in range(hb):
                sl = slice(hh * HP, (hh + 1) * HP)
                s = lax.dot_general(q_ref[:, sl], k_ref[:, sl], _NT, preferred_element_type=F32)
                if masked:
                    row = lax.broadcasted_iota(jnp.int32, (bq, bq), 0)
                    col = lax.broadcasted_iota(jnp.int32, (bq, bq), 1)
                    s = jnp.where(col <= row, s, NEG_INF)
                m_prev = m_sc[hh]
                m_new = jnp.maximum(m_prev, jnp.max(s, axis=1, keepdims=True))
                alpha = jnp.exp2(m_prev - m_new)
                p = jnp.exp2(s - jnp.concatenate([m_new] * (bq // HP), axis=1))
                l_sc[hh] = alpha * l_sc[hh] + jnp.sum(p, axis=1, keepdims=True)
                acc_sc[hh] = alpha * acc_sc[hh] + jnp.dot(p.astype(BF16), v_ref[:, sl], preferred_element_type=F32)
                m_sc[hh] = m_new

        @pl.when(j < i)
        def _():
            step(False)

        @pl.when(j == i)
        def _():
            step(True)
            for hh in range(hb):
                o_ref[:, hh * HP:(hh + 1) * HP] = acc_sc[hh] / l_sc[hh]
                lse_ref[hh] = jnp.transpose((m_sc[hh] + jnp.log2(l_sc[hh])) * LN2)[0:1, :]

        if carry is not None:
            _carry_wait(exchange, (pl.program_id(0) == HEADS // hb - 1) & (t == i_tab.shape[0] - 1))

    qs = pl.BlockSpec((bq, hb * HP), lambda h, t, it, jt: (it[t], h))
    ks = pl.BlockSpec((bq, hb * HP), lambda h, t, it, jt: (jt[t], h))
    c_in, c_out, c_shape, c_sems = _carry_specs(carry)
    gs = pltpu.PrefetchScalarGridSpec(
        num_scalar_prefetch=2, grid=(HEADS // hb, i_tab.shape[0]), in_specs=[qs, ks, ks] + c_in,
        out_specs=[qs, pl.BlockSpec((hb, 1, bq), lambda h, t, it, jt: (h, 0, it[t]))] + c_out,
        scratch_shapes=[pltpu.VMEM((hb, bq, HP), F32)] * 3 + c_sems)
    return pl.pallas_call(
        body, name=name, grid_spec=gs,
        out_shape=[jax.ShapeDtypeStruct((T, HEADS * HP), F32), jax.ShapeDtypeStruct((HEADS, 1, T), F32)] + c_shape,
        compiler_params=pltpu.CompilerParams(dimension_semantics=("arbitrary", "arbitrary")),
    )(i_tab, j_tab, q, k, v, *([] if carry is None else [carry[0]]))


def _flash_bwd(q, k, v, do, lse_row, delta_row, name, bq=512, carry=None):
    T = q.shape[0]
    bq = min(bq, T)
    nq = T // bq
    hb = ATT_HB_BWD
    n_carry = 0 if carry is None else 1

    i_tab, j_tab = _tri_tables(nq, by_row=False)
    n_blk = i_tab.shape[0]

    def body(it_ref, jt_ref, q_ref, k_ref, v_ref, do_ref, lse_ref, dl_ref, *rest):
        dq_ref, dk_ref, dv_ref = rest[n_carry:n_carry + 3]
        t = pl.program_id(1)
        i, j = it_ref[t], jt_ref[t]
        if carry is not None:
            exchange = functools.partial(_carried_exchange, rest[0], rest[4], rest[-2], rest[-1], carry[1])
            _carry_start(exchange, (pl.program_id(0) == 0) & (t == 0))

        @pl.when(t == 0)
        def _():
            dq_ref[...] = jnp.zeros_like(dq_ref)

        @pl.when(i == j)
        def _():
            dk_ref[...] = jnp.zeros_like(dk_ref)
            dv_ref[...] = jnp.zeros_like(dv_ref)

        def step(masked):
            rows = pl.ds(pl.multiple_of(i * bq, bq), bq)
            for hh in range(hb):
                sl = slice(hh * HP, (hh + 1) * HP)
                qh, kh, doh = q_ref[:, sl], k_ref[:, sl], do_ref[:, sl]
                st = lax.dot_general(kh, qh, _NT, preferred_element_type=F32)
                pt = jnp.exp2(st - lse_ref[hh] * LOG2E)
                if masked:
                    krow = lax.broadcasted_iota(jnp.int32, (bq, bq), 0)
                    qcol = lax.broadcasted_iota(jnp.int32, (bq, bq), 1)
                    pt = jnp.where(krow <= qcol, pt, 0.0)
                dv_ref[:, sl] += jnp.dot(pt.astype(BF16), doh, preferred_element_type=F32)
                dpt = lax.dot_general(v_ref[:, sl], doh, _NT, preferred_element_type=F32)
                dst = (pt * (dpt - dl_ref[hh])).astype(BF16)
                dk_ref[:, sl] += jnp.dot(dst, qh, preferred_element_type=F32)
                dq_ref[rows, sl] += lax.dot_general(dst, kh, _TN, preferred_element_type=F32)

        @pl.when(i > j)
        def _():
            step(False)

        @pl.when(i == j)
        def _():
            step(True)

        @pl.when(i == nq - 1)
        def _():
            dk_ref[...] *= LN2

        @pl.when(t == n_blk - 1)
        def _():
            dq_ref[...] *= ATT_SCALE

        if carry is not None:
            _carry_wait(exchange, (pl.program_id(0) == HEADS // hb - 1) & (t == n_blk - 1))

    qs = pl.BlockSpec((bq, hb * HP), lambda h, t, it, jt: (it[t], h))
    ks = pl.BlockSpec((bq, hb * HP), lambda h, t, it, jt: (jt[t], h))
    rs = pl.BlockSpec((hb, 1, bq), lambda h, t, it, jt: (h, 0, it[t]))
    full = pl.BlockSpec((T, hb * HP), lambda h, t, it, jt: (0, h))
    c_in, c_out, c_shape, c_sems = _carry_specs(carry)
    gs = pltpu.PrefetchScalarGridSpec(num_scalar_prefetch=2, grid=(HEADS // hb, n_blk),
                                      in_specs=[qs, ks, ks, qs, rs, rs] + c_in, out_specs=[full, ks, ks] + c_out,
                                      scratch_shapes=c_sems)
    return pl.pallas_call(
        body, name=name, grid_spec=gs, out_shape=[jax.ShapeDtypeStruct((T, HEADS * HP), F32)] * 3 + c_shape,
        compiler_params=pltpu.CompilerParams(dimension_semantics=("arbitrary", "arbitrary")),
    )(i_tab, j_tab, q, k, v, do, lse_row, delta_row, *([] if carry is None else [carry[0]]))


def _xattn_heads(q, k, v):
    scale = XD ** -0.5
    ps = []
    for h in range(XH):
        sl = slice(h * XD, (h + 1) * XD)
        s = lax.dot_general(q[:, sl].astype(BF16), k[:, sl].astype(BF16), (((1,), (1,)), ((), ())),
                            preferred_element_type=F32) * scale
        e = jnp.exp(s - jnp.max(s, axis=1, keepdims=True))
        ps.append(e / jnp.sum(e, axis=1, keepdims=True))
    return ps


def _xattn_fwd_fn(q, k, v):
    ps = _xattn_heads(q, k, v)
    o = [jnp.dot(p.astype(BF16), v[:, h * XD:(h + 1) * XD].astype(BF16), preferred_element_type=F32)
         for h, p in enumerate(ps)]
    return (jnp.concatenate(o, axis=1),)


def _xattn_bwd_fn(q, do, k, v):
    scale = XD ** -0.5
    ps = _xattn_heads(q, k, v)
    dqs, dks, dvs = [], [], []
    tdims = (((0,), (0,)), ((), ()))
    for h, p in enumerate(ps):
        sl = slice(h * XD, (h + 1) * XD)
        doh = do[:, sl].astype(BF16)
        dvs.append(lax.dot_general(p.astype(BF16), doh, tdims, preferred_element_type=F32))
        dp = lax.dot_general(doh, v[:, sl].astype(BF16), (((1,), (1,)), ((), ())), preferred_element_type=F32)
        ds = (p * (dp - jnp.sum(dp * p, axis=1, keepdims=True)) * scale).astype(BF16)
        dqs.append(jnp.dot(ds, k[:, sl].astype(BF16), preferred_element_type=F32))
        dks.append(lax.dot_general(ds, q[:, sl].astype(BF16), tdims, preferred_element_type=F32))
    return jnp.concatenate(dqs, axis=1), jnp.concatenate(dks, axis=1), jnp.concatenate(dvs, axis=1)


S5_SUB = 8


def _cmul_add(xr, xi, ar, ai, sr, si):
    return xr + ar * sr - ai * si, xi + ar * si + ai * sr


def _s5_tables(ar, ai, reverse):
    pows = [(ar, ai)]
    for _ in range(S5_SUB - 1):
        pr, pi = pows[-1]
        pows.append((pr * ar - pi * ai, pr * ai + pi * ar))
    cr = jnp.concatenate([p[0] for p in pows], axis=0)
    ci = jnp.concatenate([p[1] for p in pows], axis=0)
    if reverse:
        cr, ci = cr[::-1], ci[::-1]
    t = jnp.arange(S5_SUB)[:, None]
    blocks = [jnp.concatenate([cr, ci], axis=1)]
    for s in (1, 2, 4):
        keep = (t < S5_SUB - s) if reverse else (t >= s)
        sr, si = pows[s - 1]
        blocks.append(jnp.concatenate([jnp.where(keep, sr, 0.0), jnp.where(keep, si, 0.0)], axis=1))
    return jnp.concatenate(blocks, axis=0)


def _sub_scan(xr, xi, tab_ref, reverse):
    for k, s in enumerate((1, 2, 4)):
        blk = slice(S5_SUB * (k + 1), S5_SUB * (k + 2))
        sh = S5_SUB - s if reverse else s
        xr, xi = _cmul_add(xr, xi, tab_ref[blk, :S5_N], tab_ref[blk, S5_N:], pltpu.roll(xr, sh, 0), pltpu.roll(xi, sh, 0))
    return xr, xi


def _s5_fwd(z, b_st, c_st, d_skip, tab, name, tc=512):
    T = z.shape[0]
    tc = min(tc, T)
    nsub = tc // S5_SUB

    def body(u_ref, bst_ref, cst_ref, d_ref, tab_ref, h_ref, yc_ref, yg_ref, carry, x_sc):
        @pl.when(pl.program_id(0) == 0)
        def _():
            carry[...] = jnp.zeros_like(carry)

        u = u_ref[...].astype(F32)
        x_sc[...] = jnp.dot(u.astype(BF16), bst_ref[...], preferred_element_type=F32)

        def sub(b, c):
            r = pl.ds(pl.multiple_of(b * S5_SUB, S5_SUB), S5_SUB)
            xr, xi = _sub_scan(x_sc[r, :S5_N], x_sc[r, S5_N:], tab_ref, False)
            hr, hi = _cmul_add(xr, xi, tab_ref[0:S5_SUB, :S5_N], tab_ref[0:S5_SUB, S5_N:], c[0], c[1])
            x_sc[r, :S5_N] = hr
            x_sc[r, S5_N:] = hi
            return hr[S5_SUB - 1:S5_SUB], hi[S5_SUB - 1:S5_SUB]

        cr, ci = lax.fori_loop(0, nsub, sub, (carry[0:1, :S5_N], carry[0:1, S5_N:]))
        carry[0:1, :S5_N] = cr
        carry[0:1, S5_N:] = ci
        h = x_sc[...]
        h_ref[...] = h
        yc = jnp.dot(h.astype(BF16), cst_ref[...], preferred_element_type=F32)
        yc_ref[...] = yc
        yg_ref[...] = _gelu_skip(yc, u, d_ref[...]).astype(yg_ref.dtype)

    whole = lambda a: pl.BlockSpec(a.shape, lambda i: (0, 0))
    row = lambda w: pl.BlockSpec((tc, w), lambda i: (i, 0))
    return pl.pallas_call(
        body, name=name, grid=(T // tc,),
        in_specs=[pl.BlockSpec((tc, S5_W), lambda i: (i, Z_U // S5_W)), whole(b_st), whole(c_st), whole(d_skip), whole(tab)],
        out_specs=[row(2 * S5_N), row(S5_W), row(S5_W)],
        out_shape=[jax.ShapeDtypeStruct((T, 2 * S5_N), F32), jax.ShapeDtypeStruct((T, S5_W), F32),
                   jax.ShapeDtypeStruct((T, S5_W), BF16)],
        scratch_shapes=[pltpu.VMEM((8, 2 * S5_N), F32), pltpu.VMEM((tc, 2 * S5_N), F32)],
        compiler_params=pltpu.CompilerParams(dimension_semantics=("arbitrary",)),
    )(z, b_st, c_st, d_skip, tab)


def _s5_bwd(dyg, yc, z, h, b_st, c_st, d_skip, tab, name, tc=512):
    T = z.shape[0]
    tc = min(tc, T)
    nc, nsub = T // tc, tc // S5_SUB

    def body(dyg_ref, yc_ref, u_ref, h_ref, hp_ref, bst_ref, cst_ref, d_ref, tab_ref,
             du_ref, da_ref, db_ref, dct_ref, dd_ref, carry, x_sc):
        g = pl.program_id(0)

        @pl.when(g == 0)
        def _():
            carry[...] = jnp.zeros_like(carry)
            for r in (da_ref, db_ref, dct_ref, dd_ref):
                r[...] = jnp.zeros_like(r)

        u = u_ref[...].astype(F32)
        _, vjp = jax.vjp(_gelu_skip, yc_ref[...], u, d_ref[...])
        dyc, du_skip, dd = vjp(dyg_ref[...])
        dd_ref[...] += dd
        dyc16 = dyc.astype(BF16)
        dct_ref[...] += lax.dot_general(dyc16, h_ref[...].astype(BF16), _TN, preferred_element_type=F32)
        x_sc[...] = lax.dot_general(dyc16, cst_ref[...], _NT, preferred_element_type=F32)
        first = jnp.where(g == nc - 1, 0.0, 1.0)
        row0 = lax.broadcasted_iota(jnp.int32, (S5_SUB, S5_N), 0) == 0

        def sub(k, c):
            cr, ci, acc_r, acc_i = c
            b = nsub - 1 - k
            r = pl.ds(pl.multiple_of(b * S5_SUB, S5_SUB), S5_SUB)
            xr, xi = _sub_scan(x_sc[r, :S5_N], x_sc[r, S5_N:], tab_ref, True)
            lr, li = _cmul_add(xr, xi, tab_ref[0:S5_SUB, :S5_N], tab_ref[0:S5_SUB, S5_N:], cr, ci)
            x_sc[r, :S5_N] = lr
            x_sc[r, S5_N:] = li
            rp = pl.ds(pl.multiple_of(jnp.maximum(b - 1, 0) * S5_SUB, S5_SUB), S5_SUB)
            last_r = jnp.where(b == 0, hp_ref[S5_SUB - 1:S5_SUB, :S5_N] * first, h_ref[rp, :S5_N][S5_SUB - 1:S5_SUB])
            last_i = jnp.where(b == 0, hp_ref[S5_SUB - 1:S5_SUB, S5_N:] * first, h_ref[rp, S5_N:][S5_SUB - 1:S5_SUB])
            pr = jnp.where(row0, last_r, pltpu.roll(h_ref[r, :S5_N], 1, 0))
            pi = jnp.where(row0, last_i, pltpu.roll(h_ref[r, S5_N:], 1, 0))
            return lr[0:1], li[0:1], acc_r + (lr * pr + li * pi), acc_i + (li * pr - lr * pi)

        zero = jnp.zeros((S5_SUB, S5_N), F32)
        cr, ci, acc_r, acc_i = lax.fori_loop(0, nsub, sub, (carry[0:1, :S5_N], carry[0:1, S5_N:], zero, zero))
        carry[0:1, :S5_N] = cr
        carry[0:1, S5_N:] = ci
        da_ref[0:1, :S5_N] += jnp.sum(acc_r, axis=0, keepdims=True)
        da_ref[0:1, S5_N:] += jnp.sum(acc_i, axis=0, keepdims=True)
        lam16 = x_sc[...].astype(BF16)
        db_ref[...] += lax.dot_general(u.astype(BF16), lam16, _TN, preferred_element_type=F32)
        du = lax.dot_general(lam16, bst_ref[...], _NT, preferred_element_type=F32) + du_skip
        du_ref[...] = du.astype(du_ref.dtype)

    whole = lambda a: pl.BlockSpec(a.shape, lambda g: (0, 0))
    row = lambda w, off=0: pl.BlockSpec((tc, w), lambda g: (nc - 1 - g, off))
    prev = pl.BlockSpec((S5_SUB, 2 * S5_N), lambda g: (jnp.maximum((nc - 1 - g) * nsub - 1, 0), 0))
    acc = lambda s: pl.BlockSpec(s, lambda g: (0, 0))
    return pl.pallas_call(
        body, name=name, grid=(nc,),
        in_specs=[row(S5_W), row(S5_W), row(S5_W, Z_U // S5_W), row(2 * S5_N), prev, whole(b_st), whole(c_st),
                  whole(d_skip), whole(tab)],
        out_specs=[row(S5_W), acc((1, 2 * S5_N)), acc((S5_W, 2 * S5_N)), acc((S5_W, 2 * S5_N)), acc((1, S5_W))],
        out_shape=[jax.ShapeDtypeStruct((T, S5_W), BF16), jax.ShapeDtypeStruct((1, 2 * S5_N), F32),
                   jax.ShapeDtypeStruct((S5_W, 2 * S5_N), F32), jax.ShapeDtypeStruct((S5_W, 2 * S5_N), F32),
                   jax.ShapeDtypeStruct((1, S5_W), F32)],
        scratch_shapes=[pltpu.VMEM((8, 2 * S5_N), F32), pltpu.VMEM((tc, 2 * S5_N), F32)],
        compiler_params=pltpu.CompilerParams(dimension_semantics=("arbitrary",)),
    )(dyg, yc, z, h, h, b_st, c_st, d_skip, tab)


def _s5_discretize(lam_re, lam_im, log_dt, b_re, b_im):
    lr = jnp.minimum(lam_re, S5_MAX_RE)
    li = lam_im
    dt = jnp.exp(log_dt)[:, None]
    mag = jnp.exp(lr * dt)
    ang = li * dt
    ab_re = mag * jnp.cos(ang)
    ab_im = mag * jnp.sin(ang)
    den = lr * lr + li * li
    nr = ab_re - 1.0
    f_re = ((nr * lr + ab_im * li) / den)[..., None]
    f_im = ((ab_im * lr - nr * li) / den)[..., None]
    return ab_re, ab_im, f_re * b_re - f_im * b_im, f_re * b_im + f_im * b_re


def _adamw_fn(w, g, m, v):
    m = ADAM_B1 * m + (1.0 - ADAM_B1) * g
    v = ADAM_B2 * v + (1.0 - ADAM_B2) * (g * g)
    m_hat = m / (1.0 - ADAM_B1 ** ADAM_STEP)
    v_hat = v / (1.0 - ADAM_B2 ** ADAM_STEP)
    delta = -ADAM_LR * (m_hat / (jnp.sqrt(v_hat) + ADAM_EPS) + ADAM_WD * w)
    return delta, m, v


def _adamw(w, g, m, v, name):
    wd = w.shape[1]
    return _rowwise(_adamw_fn, name, [_whole(w), _whole(g), _whole(m), _whole(v)], [], [(wd, F32)] * 3,
                    tm=_row_tile(w.shape[0]))


def _adamw_many(ws, gs, ms, vs, name):
    n = len(ws)

    def body(*refs):
        for i in range(n):
            d_, m_, v_ = _adamw_fn(*[refs[kk * n + i][...] for kk in range(4)])
            refs[4 * n + i][...] = d_
            refs[5 * n + i][...] = m_
            refs[6 * n + i][...] = v_

    out = pl.pallas_call(body, name=name, out_shape=[jax.ShapeDtypeStruct(w.shape, F32) for w in ws] * 3)(*ws, *gs, *ms, *vs)
    return out[:n], out[n:2 * n], out[2 * n:]


def _me():
    return lax.axis_index("x"), lax.axis_index("y"), lax.axis_index("c")


def _chip_of(j):
    return j // 2, j % 2


_HBM = pl.BlockSpec(memory_space=pl.ANY)


def _gather_weights(shard):
    _, R, W = shard.shape

    def body(s_ref, out_ref, send_sems, recv_sems):
        x, y, c = _me()
        mine = 2 * x + y
        sib = (x, y, 1 - c)
        chips = [(1 - x, y), (x, 1 - y), (1 - x, 1 - y)]

        def rcopy(kk, src, chip_idx, half, to):
            return pltpu.make_async_remote_copy(src_ref=src, dst_ref=out_ref.at[chip_idx, half],
                                                send_sem=send_sems.at[kk], recv_sem=recv_sems.at[kk],
                                                device_id=to, device_id_type=MESH)

        first = [rcopy(j, s_ref.at[c], mine, c, (*chip, c)) for j, chip in enumerate(chips)]
        for cp in first:
            cp.start()
        passed = []
        for j, (px, py) in enumerate(chips):
            src_chip = 2 * px + py
            rcopy(j, s_ref.at[c], src_chip, c, (x, y, c)).wait_recv()
            fwd = rcopy(3 + j, out_ref.at[src_chip, c], src_chip, c, sib)
            fwd.start()
            passed.append(fwd)
        for j, (px, py) in enumerate(chips):
            rcopy(3 + j, s_ref.at[c], 2 * px + py, 1 - c, (x, y, c)).wait_recv()
        for cp in first + passed:
            cp.wait_send()

    return pl.pallas_call(
        body, name="gather_weights", in_specs=[_HBM], out_specs=_HBM,
        out_shape=jax.ShapeDtypeStruct((4, 2, R, W), shard.dtype),
        scratch_shapes=[pltpu.SemaphoreType.DMA((6,)), pltpu.SemaphoreType.DMA((6,))],
    )(shard)


def _pair_exchange(g, name):
    _, _, R, W = g.shape

    def body(g_ref, out_ref, send_sem, recv_sem):
        x, y, c = _me()
        cp = pltpu.make_async_remote_copy(src_ref=g_ref.at[:, 1 - c], dst_ref=out_ref, send_sem=send_sem,
                                          recv_sem=recv_sem, device_id=(x, y, 1 - c), device_id_type=MESH)
        cp.start()
        cp.wait()

    return pl.pallas_call(
        body, name=name, in_specs=[_HBM], out_specs=_HBM,
        out_shape=jax.ShapeDtypeStruct((4, R, W), g.dtype),
        scratch_shapes=[pltpu.SemaphoreType.DMA(()), pltpu.SemaphoreType.DMA(())],
    )(g)


def _pair_add(g, recv, cidx, name):
    _, _, R, W = g.shape
    tr = _row_tile(R)

    def body(c_ref, a_ref, b_ref, o_ref):
        o_ref[...] = (a_ref[...].astype(F32) + b_ref[...].astype(F32)).astype(o_ref.dtype)

    gs = pltpu.PrefetchScalarGridSpec(
        num_scalar_prefetch=1, grid=(4, R // tr),
        in_specs=[pl.BlockSpec((None, None, tr, W), lambda j, r, c: (j, c[0], r, 0)),
                  pl.BlockSpec((None, tr, W), lambda j, r, c: (j, r, 0))],
        out_specs=pl.BlockSpec((None, tr, W), lambda j, r, c: (j, r, 0)))
    return pl.pallas_call(body, name=name, grid_spec=gs,
                          out_shape=jax.ShapeDtypeStruct((4, R, W), BF16))(cidx, g, recv)


def _carried_exchange(src_ref, dst_ref, send_sems, recv_sems, spread, incoming):
    x, y, c = _me()
    mine = 2 * x + y
    copies = []
    for j, (px, py) in enumerate([(1 - x, y), (x, 1 - y), (1 - x, 1 - y)]):
        src = src_ref.at[2 * px + py] if spread else src_ref.at[c]
        slot = 2 * px + py if incoming else mine
        copies.append(pltpu.make_async_remote_copy(src_ref=src, dst_ref=dst_ref.at[slot], send_sem=send_sems.at[j],
                                                   recv_sem=recv_sems.at[j], device_id=(px, py, c), device_id_type=MESH))
    return copies


def _chip_exchange(p, name):
    _, R, W = p.shape

    def body(p_ref, out_ref, send_sems, recv_sems):
        outs = _carried_exchange(p_ref, out_ref, send_sems, recv_sems, True, False)
        for cp in outs:
            cp.start()
        for cp in _carried_exchange(p_ref, out_ref, send_sems, recv_sems, True, True):
            cp.wait_recv()
        for cp in outs:
            cp.wait_send()

    return pl.pallas_call(
        body, name=name, in_specs=[_HBM], out_specs=_HBM,
        out_shape=jax.ShapeDtypeStruct((4, R, W), p.dtype),
        scratch_shapes=[pltpu.SemaphoreType.DMA((3,)), pltpu.SemaphoreType.DMA((3,))],
    )(p)


def _chip_sum(q, name):
    _, R, W = q.shape
    tr = _row_tile(R)

    def body(q_ref, o_ref):
        f = lambda t: q_ref[t].astype(F32)
        o_ref[...] = ((f(0) + f(1)) + f(2)) + f(3)

    return pl.pallas_call(body, name=name, grid=(R // tr,),
                          in_specs=[pl.BlockSpec((4, tr, W), lambda r: (0, r, 0))],
                          out_specs=pl.BlockSpec((tr, W), lambda r: (r, 0)),
                          out_shape=jax.ShapeDtypeStruct((R, W), F32))(q)


def _pair_share(r, name):
    R, W = r.shape

    def body(r_ref, out_ref, send_sem, recv_sem):
        x, y, c = _me()
        cp = pltpu.make_async_remote_copy(src_ref=r_ref, dst_ref=out_ref, send_sem=send_sem, recv_sem=recv_sem,
                                          device_id=(x, y, 1 - c), device_id_type=MESH)
        cp.start()
        cp.wait()

    return pl.pallas_call(
        body, name=name, in_specs=[_HBM], out_specs=_HBM,
        out_shape=jax.ShapeDtypeStruct((R, W), r.dtype),
        scratch_shapes=[pltpu.SemaphoreType.DMA(()), pltpu.SemaphoreType.DMA(())],
    )(r)


def _allreduce_small(vec):
    R, W = vec.shape

    def body(v_ref, out_ref, buf, send_sems, recv_sems):
        x, y, c = _me()
        me = 4 * x + 2 * y + c
        buf[me] = v_ref[...]
        cps = []
        for kk in range(1, 8):
            peer = (x ^ (kk >> 2), y ^ ((kk >> 1) & 1), c ^ (kk & 1))
            cp = pltpu.make_async_remote_copy(src_ref=v_ref, dst_ref=buf.at[me], send_sem=send_sems.at[kk - 1],
                                              recv_sem=recv_sems.at[kk - 1], device_id=peer, device_id_type=MESH)
            cp.start()
            cps.append(cp)
        for kk in range(1, 8):
            peer = (x ^ (kk >> 2), y ^ ((kk >> 1) & 1), c ^ (kk & 1))
            pltpu.make_async_remote_copy(src_ref=v_ref, dst_ref=buf.at[me ^ kk], send_sem=send_sems.at[kk - 1],
                                         recv_sem=recv_sems.at[kk - 1], device_id=peer, device_id_type=MESH).wait_recv()
        for cp in cps:
            cp.wait_send()
        acc = buf[0]
        for d in range(1, 8):
            acc = acc + buf[d]
        out_ref[...] = acc

    vm = pl.BlockSpec(memory_space=pltpu.VMEM)
    return pl.pallas_call(
        body, name="allreduce_small", in_specs=[vm], out_specs=vm, out_shape=jax.ShapeDtypeStruct((R, W), F32),
        scratch_shapes=[pltpu.VMEM((8, R, W), F32), pltpu.SemaphoreType.DMA((7,)), pltpu.SemaphoreType.DMA((7,))],
    )(vec)


def _pack_flat(parts, align, dtype):
    flat = jnp.concatenate([p.reshape(-1).astype(dtype) for p in parts])
    n = flat.shape[0]
    pad = (-n) % align
    return jnp.pad(flat, (0, pad)) if pad else flat


def _unpack_flat(flat, shapes):
    out, off = [], 0
    for s in shapes:
        n = math.prod(s)
        out.append(flat[off:off + n].reshape(s))
        off += n
    return out


def _full_from_shards(name, sh):
    if name in COL_SHARDED:
        return jnp.transpose(sh, (1, 0, 2)).reshape(sh.shape[1], 4 * sh.shape[2])
    return sh.reshape(4 * sh.shape[1], sh.shape[2])


def _shards_from_full(name, full):
    if name in COL_SHARDED:
        r, cc = full.shape[0], full.shape[1] // 4
        return jnp.transpose(full.reshape(r, 4, cc), (1, 0, 2)).reshape(4, r * cc)
    return full.reshape(4, -1)


def _pad_heads(w, width):
    k = w.shape[0]
    return jnp.pad(w.reshape(k, HEADS, width), ((0, 0), (0, 0), (0, HP - width))).reshape(k, HEADS * HP)


def _unpad_heads(w, width):
    k = w.shape[0]
    return w.reshape(k, HEADS, HP)[:, :, :width].reshape(k, HEADS * width)


def _blockdiag(t):
    g, a, b = t.shape
    return jnp.einsum("gab,gk->gakb", t, jnp.eye(g, dtype=t.dtype)).reshape(g * a, g * b)


def _blockdiag_t(m, a, b):
    g = m.shape[0] // a
    return jnp.einsum("gagb->gab", m.reshape(g, a, g, b))


def _local_step(x, mem, positions, target, W, sp, overlap=None):
    T = x.shape[0]
    row = lambda v: v.reshape(1, -1)

    ab_re, ab_im, bb_re, bb_im = _s5_discretize(sp["s5_lam_re"], sp["s5_lam_im"], sp["s5_log_dt"], sp["s5_b_re"], sp["s5_b_im"])
    ar, ai = ab_re.reshape(1, S5_N), ab_im.reshape(1, S5_N)
    tab_fwd, tab_rev = _s5_tables(ar, ai, False), _s5_tables(ar, -ai, True)
    b_st = jnp.concatenate([_blockdiag(jnp.swapaxes(bb_re, 1, 2)), _blockdiag(jnp.swapaxes(bb_im, 1, 2))],
                           axis=1).astype(BF16)
    c_st = jnp.concatenate([_blockdiag(jnp.swapaxes(sp["s5_c_re"], 1, 2)),
                            -_blockdiag(jnp.swapaxes(sp["s5_c_im"], 1, 2))], axis=0).astype(BF16)
    inv = ROPE_THETA ** (-jnp.arange(0, ROPE, 2, dtype=F32) / ROPE)
    ang = positions.astype(F32)[:, None] * inv
    cs, sn = jnp.cos(ang), jnp.sin(ang)
    one, zero = jnp.ones((T, NOPE), F32), jnp.zeros((T, NOPE), F32)
    z16, tail1, tail0 = jnp.zeros((T, ROPE // 2), F32), jnp.ones((T, HP - QK), F32), jnp.zeros((T, HP - QK), F32)
    t_cos = jnp.concatenate([one, cs, cs, tail1], axis=1)
    t_s1 = jnp.concatenate([zero, z16, sn, tail0], axis=1)
    t_s2 = jnp.concatenate([zero, -sn, z16, tail0], axis=1)

    w_in = W["w_in"]
    w_in_re = jnp.concatenate([w_in[:, 800:], w_in[:, :768], w_in[:, 768:800],
                               jnp.zeros((D_MODEL, HP - ROPE), w_in.dtype)], axis=1)
    w_uq_p = _pad_heads(W["w_uq"], QK)
    wkv = W["w_ukv"].reshape(S5_W, HEADS, NOPE + VD)
    w_uk_p = _pad_heads(wkv[:, :, :NOPE].reshape(S5_W, HEADS * NOPE), NOPE)
    w_uv_p = _pad_heads(wkv[:, :, NOPE:].reshape(S5_W, HEADS * VD), VD)
    w_oa_p = jnp.pad(W["w_oa"].reshape(HEADS, VD, D_MODEL), ((0, 0), (0, HP - VD), (0, 0))).reshape(HEADS * HP, D_MODEL)

    g_in, b_in = row(sp["ln_in_g"]), row(sp["ln_in_b"])
    g1, b1, g2, b2, g3, b3 = (row(sp[k]) for k in ("ln1_g", "ln1_b", "ln2_g", "ln2_b", "ln3_g", "ln3_b"))
    gq, gkv, d_skip = row(sp["q_norm_g"]), row(sp["kv_norm_g"]), row(sp["s5_d"])

    (h0,) = _rowwise(lambda a, g, b: (_ln(a, g, b),), "ln_in_fwd", [_whole(x)], [g_in, b_in], [(D_MODEL, F32)], tm=512)
    z = _mm(h0, w_in_re, "in_proj", tm=512, tn=Z_W, out_dtype=BF16)
    hs, yc, yg = _s5_fwd(z, b_st, c_st, d_skip, tab_fwd, "s5_fwd")
    y2 = _mm(yg, W["w_glu"], "glu_proj", tn=2048, out_dtype=BF16)
    cqn, ckvn = _rowwise(lambda a, b, ga, gb: (_rms(a, ga), _rms(b, gb)), "mla_norm_fwd",
                         [(z, Z_CQ, S5_W), (z, Z_CKV, S5_W)], [gq, gkv], [(S5_W, BF16)] * 2)
    tabs = [(t_cos, 0, HP), (t_s1, 0, HP), (t_s2, 0, HP)]

    def rope_q(acc, tc_, t1, t2):
        return _rope(acc, _tile8(tc_), _tile8(t1), _tile8(t2)) * (ATT_SCALE * LOG2E)

    def rope_k(acc, krz, tc_, t1, t2):
        return acc + _tile8(_rope(pltpu.roll(krz.astype(F32), NOPE, 1), tc_, t1, t2))

    q = _mm(cqn, w_uq_p, "mla_uq_rope", epi=rope_q, rextras=tabs, out_dtype=BF16)
    k = _mm(ckvn, w_uk_p, "mla_uk_rope", epi=rope_k, rextras=[(z, Z_KR, HP)] + tabs, out_dtype=BF16)
    v = _mm(ckvn, w_uv_p, "mla_uv", out_dtype=BF16)
    if overlap is None:
        o, lse = _flash_fwd(q, k, v, "mla_attn_fwd")
    else:
        o, lse, landed = _flash_fwd(q, k, v, "mla_attn_fwd", carry=(overlap[0], False))
        W = {**W, **overlap[1](landed)}
    def gate_epi(acc, gs_, ga_, ya, yb):
        f = lambda t: t.astype(F32)
        return acc, _gate_mix(f(gs_), f(ga_), f(ya), f(yb), acc.astype(BF16).astype(F32))

    a_out, mixin = _mm(o, w_oa_p, "mla_oa_gate", epi=gate_epi, outs=[(D_MODEL, BF16)] * 2, tm=512,
                       rextras=[(z, Z_GS, D_MODEL), (z, Z_GA, D_MODEL), (y2, 0, D_MODEL), (y2, D_MODEL, D_MODEL)])
    def ln_epi(acc, h, g, b):
        s_ = DN_ALPHA * h + acc
        return s_, _ln(s_, g, b)

    two_rows = [(D_MODEL, F32)] * 2
    s1, h1 = _mm(mixin, W["w_o"], "mix_o_ln1", epi=ln_epi, extras=(h0,), pextras=(g1, b1), outs=two_rows, tm=512)
    xq = _mm(h1, W["w_xq"], "x_q", out_dtype=BF16)
    xk = _mm(mem, W["w_xk"], "x_k", out_dtype=BF16)
    xv = _mm(mem, W["w_xv"], "x_v", out_dtype=BF16)
    (xo,) = _rowwise(_xattn_fwd_fn, "xattn_fwd", [_whole(xq)], [xk, xv], [(D_MODEL, BF16)], tm=1024, upcast=False)
    s2, h2 = _mm(xo, W["w_xo"], "x_o_ln2", epi=ln_epi, extras=(h1,), pextras=(g2, b2), outs=two_rows, tm=512)
    a_up = _mm(h2, W["w_up"], "mlp_up", out_dtype=BF16)
    def loss_epi(acc, h, tgt, g, b):
        def f(r_, g_, b_):
            e = _ln_res(h, r_, g_, b_) - tgt
            return 0.5 * jnp.sum(jnp.mean(e * e, axis=-1))
        lv, (dr_, dg_, db_) = jax.value_and_grad(f, argnums=(0, 1, 2))(acc, g, b)
        return dr_, dg_, db_, jnp.broadcast_to(lv, (1, 128))

    row_and_sums = [(D_MODEL, F32), (D_MODEL, "sum"), (D_MODEL, "sum")]
    dr3, dg3, db3, lossv = _mm(a_up, W["w_down"], "mlp_down_loss", act=_relu2, tm=512, tk=4 * D_MODEL, epi=loss_epi,
                               extras=(h2, target), pextras=(g3, b3), outs=row_and_sums + [(128, "sum")])
    gW, gs = {}, {"ln3_g": dg3, "ln3_b": db3}

    da = _mm(dr3, W["w_down"], "mlp_down_bwd_a", "nt", epi=lambda acc, a: acc * (2.0 * jnp.maximum(a, 0.0)), extras=(a_up,),
             out_dtype=BF16)
    gW["w_down"] = _mm(a_up, dr3, "mlp_down_bwd_w", "tn", m=4 * D_MODEL, n=D_MODEL, act=_relu2, tk=2048)
    gW["w_up"] = _mm(h2, da, "mlp_up_bwd_w", "tn", m=D_MODEL, n=4 * D_MODEL, tk=2048)

    def ln_bwd_epi(acc, e, s_, g, b):
        _, vjp = jax.vjp(_ln, s_, g, b)
        return vjp(acc + DN_ALPHA * e)

    dr2, gs["ln2_g"], gs["ln2_b"] = _mm(da, W["w_up"], "mlp_up_bwd_ln2", "nt", tm=512, tk=4 * D_MODEL, epi=ln_bwd_epi,
                                        extras=(dr3, s2), pextras=(g2, b2), outs=row_and_sums)
    dxo = _mm(dr2, W["w_xo"], "x_o_bwd_a", "nt", out_dtype=BF16)
    gW["w_xo"] = _mm(xo, dr2, "x_o_bwd_w", "tn", m=D_MODEL, n=D_MODEL)
    dxq, dxk, dxv = _rowwise(_xattn_bwd_fn, "xattn_bwd", [_whole(xq), _whole(dxo)], [xk, xv], [(D_MODEL, BF16)],
                             [(xk.shape[0], D_MODEL)] * 2, tm=1024, upcast=False)
    gW["w_xq"] = _mm(h1, dxq, "x_q_bwd_w", "tn", m=D_MODEL, n=D_MODEL)
    gW["w_xk"] = _mm(mem, dxk, "x_k_bwd_w", "tn", m=D_MODEL, n=D_MODEL)
    gW["w_xv"] = _mm(mem, dxv, "x_v_bwd_w", "tn", m=D_MODEL, n=D_MODEL)

    dr1, gs["ln1_g"], gs["ln1_b"] = _mm(dxq, W["w_xq"], "x_q_bwd_ln1", "nt", tm=512, epi=ln_bwd_epi,
                                        extras=(dr2, s1), pextras=(g1, b1), outs=row_and_sums)
    gW["w_o"] = _mm(mixin, dr1, "mix_o_bwd_w", "tn", m=D_MODEL, n=D_MODEL)

    def gate_bwd_epi(acc, *tiles):
        _, vjp = jax.vjp(_gate_mix, *[t.astype(F32) for t in tiles])
        return vjp(acc)

    dgs, dga, dy2a, dy2b, da_out = _mm(
        dr1, W["w_o"], "mix_o_bwd_gate", "nt", tm=512, epi=gate_bwd_epi, outs=[(D_MODEL, BF16)] * 5,
        rextras=[(z, Z_GS, D_MODEL), (z, Z_GA, D_MODEL), (y2, 0, D_MODEL), (y2, D_MODEL, D_MODEL), _whole(a_out)])

    def delta_epi(acc, o_):
        prod = acc * o_
        cols = [jnp.sum(prod[:, h * HP:(h + 1) * HP], axis=1, keepdims=True) for h in range(HEADS)]
        return acc, jnp.concatenate(cols, axis=1)

    do16, delta = _mm(da_out, w_oa_p, "mla_oa_bwd_a", "nt", epi=delta_epi, extras=(o,),
                      outs=[(HEADS * HP, BF16), (HEADS, F32)], tm=512)
    g_oa_p = _mm(o, da_out, "mla_oa_bwd_w", "tn", m=HEADS * HP, n=D_MODEL)
    gW["w_oa"] = g_oa_p.reshape(HEADS, HP, D_MODEL)[:, :VD].reshape(HEADS * VD, D_MODEL)
    delta_row = delta.T.reshape(HEADS, 1, T)
    if overlap is None:
        dq, dk, dv = _flash_bwd(q, k, v, do16, lse, delta_row, "mla_attn_bwd")
        riding = None
    else:
        sent = overlap[2]({n: gW[n] for n in BIG_LATE})
        dq, dk, dv, landed = _flash_bwd(q, k, v, do16, lse, delta_row, "mla_attn_bwd", carry=(sent, True))
        riding = (sent, landed)

    def rope_bwd(dq_, dk_, tc_, t1, t2):
        dqr = _rope_t(dq_, _tile8(tc_), _tile8(t1), _tile8(t2))
        dkr = dk_[:, 0:HP]
        for hh in range(1, HEADS):
            dkr = dkr + dk_[:, hh * HP:(hh + 1) * HP]
        dkr = _rope_t(jnp.where(_rope_lanes(dkr.shape), dkr, 0.0), tc_, t1, t2)
        dkr = pltpu.roll(dkr, NOPE, 1)
        lane = lax.broadcasted_iota(jnp.int32, dkr.shape, 1)
        return dqr, jnp.where(lane < ROPE, dkr, 0.0)

    dq_raw, dkr = _rowwise(rope_bwd, "mla_rope_bwd", [_whole(dq), _whole(dk), _whole(t_cos), _whole(t_s1), _whole(t_s2)], [],
                           [(HEADS * HP, BF16), (HP, BF16)])
    dcqn = _mm(dq_raw, w_uq_p, "mla_uq_bwd_a", "nt")
    gW["w_uq"] = _unpad_heads(_mm(cqn, dq_raw, "mla_uq_bwd_w", "tn", m=S5_W, n=HEADS * HP), QK)
    dckvn_k = _mm(dk, w_uk_p, "mla_uk_bwd_a", "nt")
    dckvn = _mm(dv, w_uv_p, "mla_uv_bwd_a", "nt", epi=lambda acc, e: acc + e, extras=(dckvn_k,))
    g_uk = _unpad_heads(_mm(ckvn, dk, "mla_uk_bwd_w", "tn", m=S5_W, n=HEADS * HP), NOPE).reshape(S5_W, HEADS, NOPE)
    g_uv = _unpad_heads(_mm(ckvn, dv, "mla_uv_bwd_w", "tn", m=S5_W, n=HEADS * HP), VD).reshape(S5_W, HEADS, VD)
    gW["w_ukv"] = jnp.concatenate([g_uk, g_uv], axis=2).reshape(S5_W, HEADS * (NOPE + VD))

    def norm_bwd(a, b, da_, db_, ga, gb):
        _, vja = jax.vjp(_rms, a, ga)
        _, vjb = jax.vjp(_rms, b, gb)
        dxa, dga_ = vja(da_)
        dxb, dgb_ = vjb(db_)
        return dxa, dxb, dga_, dgb_

    dcq, dckv, gs["q_norm_g"], gs["kv_norm_g"] = _rowwise(
        norm_bwd, "mla_norm_bwd", [(z, Z_CQ, S5_W), (z, Z_CKV, S5_W), _whole(dcqn), _whole(dckvn)], [gq, gkv],
        [(S5_W, BF16)] * 2, [(1, S5_W)] * 2)

    dy2 = jnp.concatenate([dy2a, dy2b], axis=1)
    dyg = _mm(dy2, W["w_glu"], "glu_bwd_a", "nt")
    gW["w_glu"] = _mm(yg, dy2, "glu_bwd_w", "tn", m=S5_W, n=2 * D_MODEL)

    du, gs["s5_ab"], g_bst, g_cst_t, gs["s5_d"] = _s5_bwd(dyg, yc, z, hs, b_st, c_st, d_skip, tab_rev, "s5_bwd")
    gs["s5_bb_re"] = jnp.swapaxes(_blockdiag_t(g_bst[:, :S5_N], S5_H, S5_P), 1, 2)
    gs["s5_bb_im"] = jnp.swapaxes(_blockdiag_t(g_bst[:, S5_N:], S5_H, S5_P), 1, 2)
    gs["s5_c_re"] = _blockdiag_t(g_cst_t[:, :S5_N], S5_H, S5_P)
    gs["s5_c_im"] = -_blockdiag_t(g_cst_t[:, S5_N:], S5_H, S5_P)

    dz = jnp.concatenate([dgs, dga, du, dcq, dckv, dkr], axis=1)
    g_in_re = _mm(h0, dz, "in_proj_bwd_w", "tn", m=D_MODEL, n=Z_W, tm=512, tn=Z_W)
    gW["w_in"] = jnp.concatenate([g_in_re[:, Z_U:Z_KR], g_in_re[:, Z_KR:Z_KR + ROPE], g_in_re[:, :Z_U]], axis=1)
    dx, gs["ln_in_g"], gs["ln_in_b"] = _mm(dz, w_in_re, "in_proj_bwd_ln_in", "nt", tm=512, tk=Z_W, epi=ln_bwd_epi,
                                           extras=(dr1, x), pextras=(g_in, b_in), outs=row_and_sums)
    return lossv[:, :1], dx, gW, gs, riding


_RAW_SMALL = (("loss", (1, 1)), ("ln_in_g", (1, D_MODEL)), ("ln_in_b", (1, D_MODEL)), ("ln1_g", (1, D_MODEL)),
              ("ln1_b", (1, D_MODEL)), ("ln2_g", (1, D_MODEL)), ("ln2_b", (1, D_MODEL)), ("ln3_g", (1, D_MODEL)),
              ("ln3_b", (1, D_MODEL)), ("q_norm_g", (1, S5_W)), ("kv_norm_g", (1, S5_W)), ("s5_d", (1, S5_W)),
              ("s5_ab", (1, 2 * S5_N)), ("s5_bb_re", (S5_G, S5_P, S5_H)), ("s5_bb_im", (S5_G, S5_P, S5_H)),
              ("s5_c_re", (S5_G, S5_H, S5_P)), ("s5_c_im", (S5_G, S5_H, S5_P)))


def kernel(x, mem, positions, ln_in_g, ln_in_b, w_in, s5_lam_re, s5_lam_im, s5_log_dt, s5_b_re, s5_b_im, s5_c_re, s5_c_im, s5_d, w_glu, q_norm_g, w_uq, kv_norm_g, w_ukv, w_oa, w_o, ln1_g, ln1_b, w_xq, w_xk, w_xv, w_xo, ln2_g, ln2_b, w_up, w_down, ln3_g, ln3_b, loss_target, m_ln_in_g, m_ln_in_b, m_w_in, m_s5_lam_re, m_s5_lam_im, m_s5_log_dt, m_s5_b_re, m_s5_b_im, m_s5_c_re, m_s5_c_im, m_s5_d, m_w_glu, m_q_norm_g, m_w_uq, m_kv_norm_g, m_w_ukv, m_w_oa, m_w_o, m_ln1_g, m_ln1_b, m_w_xq, m_w_xk, m_w_xv, m_w_xo, m_ln2_g, m_ln2_b, m_w_up, m_w_down, m_ln3_g, m_ln3_b, v_ln_in_g, v_ln_in_b, v_w_in, v_s5_lam_re, v_s5_lam_im, v_s5_log_dt, v_s5_b_re, v_s5_b_im, v_s5_c_re, v_s5_c_im, v_s5_d, v_w_glu, v_q_norm_g, v_w_uq, v_kv_norm_g, v_w_ukv, v_w_oa, v_w_o, v_ln1_g, v_ln1_b, v_w_xq, v_w_xk, v_w_xv, v_w_xo, v_ln2_g, v_ln2_b, v_w_up, v_w_down, v_ln3_g, v_ln3_b):
    a = dict(locals())
    wts = {n: a[n] for n in WEIGHTS}
    ms = {n: a["m_" + n] for n in WEIGHTS}
    vs = {n: a["v_" + n] for n in WEIGHTS}
    shard2d = {n: wts[n].reshape(wts[n].shape[-2], wts[n].shape[-1]) for n in BIG}
    nrow = {n: shard2d[n].size // 1024 for n in BIG}
    mine = 2 * lax.axis_index("x") + lax.axis_index("y")
    my_c = lax.axis_index("c")
    cidx = my_c.astype(jnp.int32).reshape(1)

    def group_rows(group):
        rows = sum(nrow[n] for n in group)
        return rows, (-rows) % PACK_ROWS

    def pack_shards(group):
        rows, pad = group_rows(group)
        parts = [shard2d[n].astype(BF16).reshape(nrow[n], 1024) for n in group] + [jnp.zeros((pad, 1024), BF16)]
        return jnp.concatenate(parts, axis=0).reshape(2, (rows + pad) // 2, 1024)

    def unpack_full(group, gathered):
        out, off = {}, 0
        for n in group:
            r, cc = shard2d[n].shape
            piece = lax.optimization_barrier(gathered[:, off:off + nrow[n]])
            out[n] = _full_from_shards(n, piece.reshape(4, r, cc))
            off += nrow[n]
        return out

    def pack_grads(group, gw):
        rows, pad = group_rows(group)
        parts = [_shards_from_full(n, gw[n]).astype(BF16).reshape(4, nrow[n], 1024) for n in group]
        return jnp.concatenate(parts + [jnp.zeros((4, pad, 1024), BF16)], axis=1).reshape(4, 2, (rows + pad) // 2, 1024)

    def by_core(own, other):
        return jnp.where(my_c == 0, jnp.stack([own, other]), jnp.stack([other, own]))

    packed_e, packed_l = pack_shards(BIG_EARLY), pack_shards(BIG_LATE)
    gathered = lax.dynamic_update_slice(_gather_weights(packed_e), packed_e[None], (mine, 0, 0, 0))
    W = unpack_full(BIG_EARLY, gathered.reshape(4, -1, 1024))

    def finish_gather(landed):
        half = packed_l.shape[1]
        own_half = lax.dynamic_index_in_dim(packed_l, my_c, axis=0, keepdims=True)
        landed = lax.dynamic_update_slice(landed, own_half, (mine, 0, 0))
        other = _pair_share(landed.reshape(4 * half, 1024), "gather_late_share").reshape(4, half, 1024)
        both = jnp.moveaxis(by_core(landed, other), 0, 1)
        return unpack_full(BIG_LATE, both.reshape(4, 2 * half, 1024))

    def pair_reduce(group, gw, tag):
        gflat = pack_grads(group, gw)
        return _pair_add(gflat, _pair_exchange(gflat, "grad_pair_exchange_" + tag), cidx, "grad_pair_add_" + tag)

    def finish_reduce(group, sent, landed, tag):
        own = lax.dynamic_slice(sent, (mine, 0, 0), (1,) + sent.shape[1:])
        red = _chip_sum(lax.dynamic_update_slice(landed, own, (mine, 0, 0)), "grad_chip_sum_" + tag)
        gsh = by_core(red, _pair_share(red, "grad_pair_share_" + tag)).reshape(-1, 1024)
        out, off = {}, 0
        for n in group:
            out[n] = lax.optimization_barrier(gsh[off:off + nrow[n]]).reshape(shard2d[n].shape)
            off += nrow[n]
        return out

    sp = {n: wts[n] for n in SMALL}
    sp_local = {n: (sp[n][0] if sp[n].ndim > 1 else sp[n]) for n in SMALL}
    overlap = (packed_l, finish_gather, lambda gw: pair_reduce(BIG_LATE, gw, "late"))
    lossv, dx, gW, gs, riding = _local_step(x[0], mem[0], positions[0], loss_target[0], W, sp_local, overlap)

    g_out = finish_reduce(BIG_LATE, riding[0], riding[1], "late")
    sent_e = pair_reduce(BIG_EARLY, gW, "early")
    g_out.update(finish_reduce(BIG_EARLY, sent_e, _chip_exchange(sent_e, "grad_chip_exchange_early"), "early"))

    gs["loss"] = lossv
    raw = _pack_flat([gs[n].reshape(s) for n, s in _RAW_SMALL], 8 * 1024, F32)
    raw = _allreduce_small(raw.reshape(-1, 1024)).reshape(-1)
    rs = dict(zip([n for n, _ in _RAW_SMALL], _unpack_flat(raw, [s for _, s in _RAW_SMALL])))
    loss = rs["loss"].reshape(())
    _, disc_vjp = jax.vjp(_s5_discretize, sp_local["s5_lam_re"], sp_local["s5_lam_im"], sp_local["s5_log_dt"],
                          sp_local["s5_b_re"], sp_local["s5_b_im"])
    d_ab = rs["s5_ab"].reshape(2, S5_G, S5_P)
    g_lre, g_lim, g_ldt, g_bre, g_bim = disc_vjp((d_ab[0], d_ab[1], rs["s5_bb_re"], rs["s5_bb_im"]))
    small_g = {"s5_lam_re": g_lre, "s5_lam_im": g_lim, "s5_log_dt": g_ldt, "s5_b_re": g_bre, "s5_b_im": g_bim,
               "s5_c_re": rs["s5_c_re"], "s5_c_im": rs["s5_c_im"]}
    for n in ("ln_in_g", "ln_in_b", "ln1_g", "ln1_b", "ln2_g", "ln2_b", "ln3_g", "ln3_b", "q_norm_g", "kv_norm_g", "s5_d"):
        small_g[n] = rs[n]

    grads, deltas, new_m, new_v = {}, {}, {}, {}
    for n in BIG:
        d_, m_, v_ = _adamw(shard2d[n], g_out[n], ms[n].reshape(shard2d[n].shape), vs[n].reshape(shard2d[n].shape), "adamw_" + n)
        grads[n] = g_out[n].reshape(wts[n].shape)
        deltas[n], new_m[n], new_v[n] = (t.reshape(wts[n].shape) for t in (d_, m_, v_))
    as2d = lambda t: t.reshape(-1, t.shape[-1])
    sd, sm, sv = _adamw_many([as2d(wts[n]) for n in SMALL], [as2d(small_g[n].reshape(wts[n].shape)) for n in SMALL],
                             [as2d(ms[n]) for n in SMALL], [as2d(vs[n]) for n in SMALL], "adamw_small")
    for n, d_, m_, v_ in zip(SMALL, sd, sm, sv):
        grads[n] = small_g[n].reshape(wts[n].shape)
        deltas[n], new_m[n], new_v[n] = (t.reshape(wts[n].shape) for t in (d_, m_, v_))

    return (loss, dx[None], *[grads[n] for n in WEIGHTS], *[deltas[n] for n in WEIGHTS],
            *[new_m[n] for n in WEIGHTS], *[new_v[n] for n in WEIGHTS])
```

```python
import functools
import math

import jax
import jax.numpy as jnp
from jax import lax
from jax.experimental import pallas as pl
from jax.experimental.pallas import tpu as pltpu

F32 = jnp.float32
BF16 = jnp.bfloat16
MESH = pl.DeviceIdType.MESH

D_MODEL = 1024
S5_W = 256
S5_G = 16
S5_H = 16
S5_P = 64
S5_N = S5_G * S5_P
S5_MAX_RE = -1e-4
HEADS = 8
NOPE = 64
ROPE = 32
QK = NOPE + ROPE
VD = 64
HP = 128
XH = 4
XD = 256
LN_EPS = 1e-5
RMS_EPS = 1e-6
NEG_INF = -1e30
DN_ALPHA = 2.0 ** 0.25
ROPE_THETA = 10000.0
ADAM_LR, ADAM_B1, ADAM_B2, ADAM_EPS, ADAM_WD, ADAM_STEP = 0.001, 0.9, 0.999, 1e-08, 0.01, 10

Z_GS, Z_GA, Z_U, Z_CQ, Z_CKV, Z_KR, Z_W = 0, 1024, 2048, 2304, 2560, 2816, 2944
SCAN_CHUNK = 256

BIG_EARLY = ("w_in", "w_glu", "w_uq", "w_ukv", "w_oa", "w_o")
BIG_LATE = ("w_up", "w_xq", "w_xk", "w_xv", "w_xo", "w_down")
BIG = BIG_EARLY + BIG_LATE
COL_SHARDED = ("w_in", "w_glu", "w_uq", "w_ukv", "w_oa", "w_up")
SMALL = ("ln_in_g", "ln_in_b", "s5_lam_re", "s5_lam_im", "s5_log_dt", "s5_b_re", "s5_b_im", "s5_c_re", "s5_c_im",
         "s5_d", "q_norm_g", "kv_norm_g", "ln1_g", "ln1_b", "ln2_g", "ln2_b", "ln3_g", "ln3_b")
WEIGHTS = ("ln_in_g", "ln_in_b", "w_in", "s5_lam_re", "s5_lam_im", "s5_log_dt", "s5_b_re", "s5_b_im", "s5_c_re",
           "s5_c_im", "s5_d", "w_glu", "q_norm_g", "w_uq", "kv_norm_g", "w_ukv", "w_oa", "w_o", "ln1_g", "ln1_b",
           "w_xq", "w_xk", "w_xv", "w_xo", "ln2_g", "ln2_b", "w_up", "w_down", "ln3_g", "ln3_b")
PACK_ROWS = 2 * 16


def _row_tile(n, cap=512):
    best = n
    for t in range(8, min(n, cap) + 1, 8):
        if n % t == 0:
            best = t
    return best


def _pick(n, cap):
    best = None
    for t in range(128, min(n, cap) + 1, 128):
        if n % t == 0:
            best = t
    return best if best is not None and (best >= 512 or best == n) else n


def _mm(a, b, name, mode="nn", *, a_off=0, b_off=0, m=None, n=None, act=None, epi=None, extras=(), rextras=(),
        pextras=(), outs=None, out_dtype=None, tm=1024, tn=1024, tk=None):
    out_dtype = out_dtype or (BF16 if mode == "tn" else F32)
    if mode == "nn":
        M, (K, N) = a.shape[0], b.shape
    elif mode == "nt":
        M, (N, K) = a.shape[0], b.shape
    else:
        K, M, N = a.shape[0], m, n
    if mode == "tn":
        tm, tn, tk = _pick(M, tm), _pick(N, tn), min(tk or 1024, K)
    else:
        tm, tn, tk = min(tm, M), _pick(N, tn), _pick(K, tk or 1024)
    assert M % tm == 0 and N % tn == 0 and K % tk == 0, (name, M, N, K, tm, tn, tk)
    nk = K // tk
    if mode == "tn":
        assert a_off % tm == 0 and b_off % tn == 0
        ao, bo = a_off // tm, b_off // tn
        a_spec = pl.BlockSpec((tk, tm), lambda i, j, k: (k, i + ao))
        b_spec = pl.BlockSpec((tk, tn), lambda i, j, k: (k, j + bo))
        dims = (((0,), (0,)), ((), ()))
    else:
        assert a_off % tk == 0
        ao = a_off // tk
        a_spec = pl.BlockSpec((tm, tk), lambda i, j, k: (i, k + ao))
        if mode == "nn":
            b_spec = pl.BlockSpec((tk, tn), lambda i, j, k: (k, j))
            dims = (((1,), (0,)), ((), ()))
        else:
            b_spec = pl.BlockSpec((tn, tk), lambda i, j, k: (j, k))
            dims = (((1,), (1,)), ((), ()))
    e_spec = pl.BlockSpec((tm, tn), lambda i, j, k: (i, j))
    p_spec = pl.BlockSpec((1, tn), lambda i, j, k: (0, j))
    r_specs = [pl.BlockSpec((tm, w), functools.partial(lambda i, j, k, o: (i, o), o=off // w)) for _, off, w in rextras]
    n_extra = len(extras) + len(rextras) + len(pextras)
    if outs is None:
        o_specs, o_shapes = [e_spec], [jax.ShapeDtypeStruct((M, N), out_dtype)]
    else:
        assert all((w == tn and dt != "sum") or N == tn for w, dt in outs), name
        o_specs = [e_spec if w == tn and dt != "sum" else pl.BlockSpec((8 if dt == "sum" else tm, w), lambda i, j, k: (i, 0))
                   for w, dt in outs]
        o_shapes = [jax.ShapeDtypeStruct((M // tm * 8, w), F32) if dt == "sum" else
                    jax.ShapeDtypeStruct((M, N if w == tn else w), dt) for w, dt in outs]
    n_out = len(o_specs)
    direct = outs is None and epi is None and out_dtype == F32

    def body(*refs):
        a_ref, b_ref = refs[0], refs[1]
        e_refs = refs[2:2 + n_extra]
        o_refs = refs[2 + n_extra:2 + n_extra + n_out]
        av = a_ref[...]
        if act is not None:
            av = act(av.astype(F32))
        p = lax.dot_general(av.astype(BF16), b_ref[...].astype(BF16), dims, preferred_element_type=F32)

        def finish(r):
            if epi is not None:
                r = epi(r, *[e[...] for e in e_refs])
            for o_ref, v in zip(o_refs, r if isinstance(r, tuple) else (r,)):
                o_ref[...] = jnp.broadcast_to(v, o_ref.shape).astype(o_ref.dtype)

        if nk == 1:
            finish(p)
        else:
            acc = o_refs[0] if direct else refs[2 + n_extra + n_out]
            k = pl.program_id(2)

            @pl.when(k == 0)
            def _():
                acc[...] = p

            @pl.when(k > 0)
            def _():
                acc[...] += p

            if not direct:
                @pl.when(k == nk - 1)
                def _():
                    finish(acc[...])

    res = pl.pallas_call(
        body, name=name, grid=(M // tm, N // tn, nk),
        in_specs=[a_spec, b_spec] + [e_spec] * len(extras) + r_specs + [p_spec] * len(pextras), out_specs=o_specs,
        out_shape=o_shapes,
        scratch_shapes=[pltpu.VMEM((tm, tn), F32)] if nk > 1 and not direct else [],
        compiler_params=pltpu.CompilerParams(dimension_semantics=("parallel", "parallel", "arbitrary")),
    )(a, b, *extras, *[r[0] for r in rextras], *pextras)
    if outs is None:
        return res[0]
    return [r.reshape(M // tm, 8, -1)[:, 0].sum(axis=0, keepdims=True) if dt == "sum" else r for r, (_, dt) in zip(res, outs)]


def _rowwise(fn, name, rows, pars, outs, accs=(), tm=256, upcast=True):
    T = rows[0][0].shape[0]
    tm = min(tm, T)
    assert T % tm == 0
    n_in, n_out = len(rows) + len(pars), len(outs)
    in_specs = []
    for arr, off, w in rows:
        assert off % w == 0 and arr.shape[0] == T, name
        in_specs.append(pl.BlockSpec((tm, w), functools.partial(lambda i, o: (i, o), o=off // w)))
    for p in pars:
        in_specs.append(pl.BlockSpec(p.shape, lambda i: (0, 0)))
    out_specs = [pl.BlockSpec((tm, w), lambda i: (i, 0)) for w, _ in outs]
    out_specs += [pl.BlockSpec(s, lambda i: (0, 0)) for s in accs]
    out_shape = [jax.ShapeDtypeStruct((T, w), dt) for w, dt in outs] + [jax.ShapeDtypeStruct(s, F32) for s in accs]

    def body(*refs):
        res = fn(*[r[...].astype(F32) if upcast else r[...] for r in refs[:n_in]])
        o_refs = refs[n_in:]
        for r, v in zip(o_refs[:n_out], res[:n_out]):
            r[...] = v.astype(r.dtype)
        if accs:
            i = pl.program_id(0)

            @pl.when(i == 0)
            def _():
                for r in o_refs[n_out:]:
                    r[...] = jnp.zeros_like(r)

            for r, v in zip(o_refs[n_out:], res[n_out:]):
                r[...] += v

    res = pl.pallas_call(
        body, name=name, grid=(T // tm,), in_specs=in_specs, out_specs=out_specs, out_shape=out_shape,
        compiler_params=pltpu.CompilerParams(dimension_semantics=("arbitrary",)),
    )(*[r[0] for r in rows], *pars)
    return res


def _whole(arr):
    return (arr, 0, arr.shape[1])


def _ln(x, g, b):
    mu = jnp.mean(x, axis=-1, keepdims=True)
    xc = x - mu
    var = jnp.mean(xc * xc, axis=-1, keepdims=True)
    return xc * lax.rsqrt(var + LN_EPS) * g + b


def _ln_res(h, r, g, b):
    return _ln(DN_ALPHA * h + r, g, b)


def _rms(x, g):
    return x * lax.rsqrt(jnp.mean(x * x, axis=-1, keepdims=True) + RMS_EPS) * g


def _gelu_skip(yc, u, d):
    y = yc + d * u
    return 0.5 * y * (1.0 + lax.erf(y * (1.0 / math.sqrt(2.0))))


def _gate_mix(gs, ga, y2a, y2b, aout):
    return jax.nn.sigmoid(gs) * (y2a * jax.nn.sigmoid(y2b)) + jax.nn.sigmoid(ga) * aout


def _relu2(a):
    r = jnp.maximum(a, 0.0)
    return r * r


def _tile8(t):
    return jnp.concatenate([t] * HEADS, axis=1)


def _rope(x, cos, s1, s2):
    w = x.shape[1]
    return x * cos + pltpu.roll(x, ROPE // 2, 1) * s1 + pltpu.roll(x, w - ROPE // 2, 1) * s2


def _rope_t(dy, cos, s1, s2):
    w = dy.shape[1]
    return dy * cos + pltpu.roll(dy * s1, w - ROPE // 2, 1) + pltpu.roll(dy * s2, ROPE // 2, 1)


def _rope_lanes(shape):
    lane = lax.broadcasted_iota(jnp.int32, shape, 1) % HP
    return (lane >= NOPE) & (lane < QK)


ATT_HB_FWD, ATT_HB_BWD = 8, 4
ATT_SCALE = QK ** -0.5
LOG2E = 1.4426950408889634
LN2 = 0.6931471805599453
_NT = (((1,), (1,)), ((), ()))
_TN = (((0,), (0,)), ((), ()))


def _carry_specs(carry):
    if carry is None:
        return [], [], [], []
    arr = carry[0]
    return ([_HBM], [_HBM], [jax.ShapeDtypeStruct((4,) + arr.shape[1:], arr.dtype)],
            [pltpu.SemaphoreType.DMA((3,)), pltpu.SemaphoreType.DMA((3,))])


def _carry_start(exchange, when):
    @pl.when(when)
    def _():
        for cp in exchange(False):
            cp.start()


def _carry_wait(exchange, when):
    @pl.when(when)
    def _():
        for cp in exchange(True):
            cp.wait_recv()
        for cp in exchange(False):
            cp.wait_send()


def _tri_tables(n, by_row):
    if by_row:
        pairs = [(i, j) for i in range(n) for j in range(i + 1)]
    else:
        pairs = [(i, j) for j in range(n) for i in range(j, n)]
    return jnp.array([p[0] for p in pairs], jnp.int32), jnp.array([p[1] for p in pairs], jnp.int32)


def _flash_fwd(q, k, v, name, bq=512, carry=None):
    T = q.shape[0]
    bq = min(bq, T)
    nq = T // bq
    hb = ATT_HB_FWD
    n_carry = 0 if carry is None else 1

    i_tab, j_tab = _tri_tables(nq, by_row=True)

    def body(it_ref, jt_ref, q_ref, k_ref, v_ref, *rest):
        o_ref, lse_ref = rest[n_carry:n_carry + 2]
        m_sc, l_sc, acc_sc = rest[2 * n_carry + 2:2 * n_carry + 5]
        t = pl.program_id(1)
        i, j = it_ref[t], jt_ref[t]
        if carry is not None:
            exchange = functools.partial(_carried_exchange, rest[0], rest[3], rest[-2], rest[-1], carry[1])
            _carry_start(exchange, (pl.program_id(0) == 0) & (t == 0))

        @pl.when(j == 0)
        def _():
            m_sc[...] = jnp.full_like(m_sc, NEG_INF)
            l_sc[...] = jnp.zeros_like(l_sc)
            acc_sc[...] = jnp.zeros_like(acc_sc)

        def step(masked):
            for hh in range(hb):
                sl = slice(hh * HP, (hh + 1) * HP)
                s = lax.dot_general(q_ref[:, sl], k_ref[:, sl], _NT, preferred_element_type=F32)
                if masked:
                    row = lax.broadcasted_iota(jnp.int32, (bq, bq), 0)
                    col = lax.broadcasted_iota(jnp.int32, (bq, bq), 1)
                    s = jnp.where(col <= row, s, NEG_INF)
                m_prev = m_sc[hh]
                m_new = jnp.maximum(m_prev, jnp.max(s, axis=1, keepdims=True))
                alpha = jnp.exp2(m_prev - m_new)
                p = jnp.exp2(s - jnp.concatenate([m_new] * (bq // HP), axis=1))
                l_sc[hh] = alpha * l_sc[hh] + jnp.sum(p, axis=1, keepdims=True)
                acc_sc[hh] = alpha * acc_sc[hh] + jnp.dot(p.astype(BF16), v_ref[:, sl], preferred_element_type=F32)
                m_sc[hh] = m_new

        @pl.when(j < i)
        def _():
            step(False)

        @pl.when(j == i)
        def _():
            step(True)
            for hh in range(hb):
                o_ref[:, hh * HP:(hh + 1) * HP] = acc_sc[hh] / l_sc[hh]
                lse_ref[hh] = jnp.transpose((m_sc[hh] + jnp.log2(l_sc[hh])) * LN2)[0:1, :]

        if carry is not None:
            _carry_wait(exchange, (pl.program_id(0) == HEADS // hb - 1) & (t == i_tab.shape[0] - 1))

    qs = pl.BlockSpec((bq, hb * HP), lambda h, t, it, jt: (it[t], h))
    ks = pl.BlockSpec((bq, hb * HP), lambda h, t, it, jt: (jt[t], h))
    c_in, c_out, c_shape, c_sems = _carry_specs(carry)
    gs = pltpu.PrefetchScalarGridSpec(
        num_scalar_prefetch=2, grid=(HEADS // hb, i_tab.shape[0]), in_specs=[qs, ks, ks] + c_in,
        out_specs=[qs, pl.BlockSpec((hb, 1, bq), lambda h, t, it, jt: (h, 0, it[t]))] + c_out,
        scratch_shapes=[pltpu.VMEM((hb, bq, HP), F32)] * 3 + c_sems)
    return pl.pallas_call(
        body, name=name, grid_spec=gs,
        out_shape=[jax.ShapeDtypeStruct((T, HEADS * HP), F32), jax.ShapeDtypeStruct((HEADS, 1, T), F32)] + c_shape,
        compiler_params=pltpu.CompilerParams(dimension_semantics=("arbitrary", "arbitrary")),
    )(i_tab, j_tab, q, k, v, *([] if carry is None else [carry[0]]))


def _flash_bwd(q, k, v, do, lse_row, delta_row, name, bq=512, carry=None):
    T = q.shape[0]
    bq = min(bq, T)
    nq = T // bq
    hb = ATT_HB_BWD
    n_carry = 0 if carry is None else 1

    i_tab, j_tab = _tri_tables(nq, by_row=False)
    n_blk = i_tab.shape[0]

    def body(it_ref, jt_ref, q_ref, k_ref, v_ref, do_ref, lse_ref, dl_ref, *rest):
        dq_ref, dk_ref, dv_ref = rest[n_carry:n_carry + 3]
        t = pl.program_id(1)
        i, j = it_ref[t], jt_ref[t]
        if carry is not None:
            exchange = functools.partial(_carried_exchange, rest[0], rest[4], rest[-2], rest[-1], carry[1])
            _carry_start(exchange, (pl.program_id(0) == 0) & (t == 0))

        @pl.when(t == 0)
        def _():
            dq_ref[...] = jnp.zeros_like(dq_ref)

        @pl.when(i == j)
        def _():
            dk_ref[...] = jnp.zeros_like(dk_ref)
            dv_ref[...] = jnp.zeros_like(dv_ref)

        def step(masked):
            rows = pl.ds(pl.multiple_of(i * bq, bq), bq)
            for hh in range(hb):
                sl = slice(hh * HP, (hh + 1) * HP)
                qh, kh, doh = q_ref[:, sl], k_ref[:, sl], do_ref[:, sl]
                st = lax.dot_general(kh, qh, _NT, preferred_element_type=F32)
                pt = jnp.exp2(st - lse_ref[hh] * LOG2E)
                if masked:
                    krow = lax.broadcasted_iota(jnp.int32, (bq, bq), 0)
                    qcol = lax.broadcasted_iota(jnp.int32, (bq, bq), 1)
                    pt = jnp.where(krow <= qcol, pt, 0.0)
                dv_ref[:, sl] += jnp.dot(pt.astype(BF16), doh, preferred_element_type=F32)
                dpt = lax.dot_general(v_ref[:, sl], doh, _NT, preferred_element_type=F32)
                dst = (pt * (dpt - dl_ref[hh])).astype(BF16)
                dk_ref[:, sl] += jnp.dot(dst, qh, preferred_element_type=F32)
                dq_ref[rows, sl] += lax.dot_general(dst, kh, _TN, preferred_element_type=F32)

        @pl.when(i > j)
        def _():
            step(False)

        @pl.when(i == j)
        def _():
            step(True)

        @pl.when(i == nq - 1)
        def _():
            dk_ref[...] *= LN2

        @pl.when(t == n_blk - 1)
        def _():
            dq_ref[...] *= ATT_SCALE

        if carry is not None:
            _carry_wait(exchange, (pl.program_id(0) == HEADS // hb - 1) & (t == n_blk - 1))

    qs = pl.BlockSpec((bq, hb * HP), lambda h, t, it, jt: (it[t], h))
    ks = pl.BlockSpec((bq, hb * HP), lambda h, t, it, jt: (jt[t], h))
    rs = pl.BlockSpec((hb, 1, bq), lambda h, t, it, jt: (h, 0, it[t]))
    full = pl.BlockSpec((T, hb * HP), lambda h, t, it, jt: (0, h))
    c_in, c_out, c_shape, c_sems = _carry_specs(carry)
    gs = pltpu.PrefetchScalarGridSpec(num_scalar_prefetch=2, grid=(HEADS // hb, n_blk),
                                      in_specs=[qs, ks, ks, qs, rs, rs] + c_in, out_specs=[full, ks, ks] + c_out,
                                      scratch_shapes=c_sems)
    return pl.pallas_call(
        body, name=name, grid_spec=gs, out_shape=[jax.ShapeDtypeStruct((T, HEADS * HP), F32)] * 3 + c_shape,
        compiler_params=pltpu.CompilerParams(dimension_semantics=("arbitrary", "arbitrary")),
    )(i_tab, j_tab, q, k, v, do, lse_row, delta_row, *([] if carry is None else [carry[0]]))


def _xattn_heads(q, k, v):
    scale = XD ** -0.5
    ps = []
    for h in range(XH):
        sl = slice(h * XD, (h + 1) * XD)
        s = lax.dot_general(q[:, sl].astype(BF16), k[:, sl].astype(BF16), (((1,), (1,)), ((), ())),
                            preferred_element_type=F32) * scale
        e = jnp.exp(s - jnp.max(s, axis=1, keepdims=True))
        ps.append(e / jnp.sum(e, axis=1, keepdims=True))
    return ps


def _xattn_fwd_fn(q, k, v):
    ps = _xattn_heads(q, k, v)
    o = [jnp.dot(p.astype(BF16), v[:, h * XD:(h + 1) * XD].astype(BF16), preferred_element_type=F32)
         for h, p in enumerate(ps)]
    return (jnp.concatenate(o, axis=1),)


def _xattn_bwd_fn(q, do, k, v):
    scale = XD ** -0.5
    ps = _xattn_heads(q, k, v)
    dqs, dks, dvs = [], [], []
    tdims = (((0,), (0,)), ((), ()))
    for h, p in enumerate(ps):
        sl = slice(h * XD, (h + 1) * XD)
        doh = do[:, sl].astype(BF16)
        dvs.append(lax.dot_general(p.astype(BF16), doh, tdims, preferred_element_type=F32))
        dp = lax.dot_general(doh, v[:, sl].astype(BF16), (((1,), (1,)), ((), ())), preferred_element_type=F32)
        ds = (p * (dp - jnp.sum(dp * p, axis=1, keepdims=True)) * scale).astype(BF16)
        dqs.append(jnp.dot(ds, k[:, sl].astype(BF16), preferred_element_type=F32))
        dks.append(lax.dot_general(ds, q[:, sl].astype(BF16), tdims, preferred_element_type=F32))
    return jnp.concatenate(dqs, axis=1), jnp.concatenate(dks, axis=1), jnp.concatenate(dvs, axis=1)


S5_SUB = 8


def _cmul_add(xr, xi, ar, ai, sr, si):
    return xr + ar * sr - ai * si, xi + ar * si + ai * sr


def _s5_tables(ar, ai, reverse):
    pows = [(ar, ai)]
    for _ in range(S5_SUB - 1):
        pr, pi = pows[-1]
        pows.append((pr * ar - pi * ai, pr * ai + pi * ar))
    cr = jnp.concatenate([p[0] for p in pows], axis=0)
    ci = jnp.concatenate([p[1] for p in pows], axis=0)
    if reverse:
        cr, ci = cr[::-1], ci[::-1]
    t = jnp.arange(S5_SUB)[:, None]
    blocks = [jnp.concatenate([cr, ci], axis=1)]
    for s in (1, 2, 4):
        keep = (t < S5_SUB - s) if reverse else (t >= s)
        sr, si = pows[s - 1]
        blocks.append(jnp.concatenate([jnp.where(keep, sr, 0.0), jnp.where(keep, si, 0.0)], axis=1))
    return jnp.concatenate(blocks, axis=0)


def _sub_scan(xr, xi, tab_ref, reverse):
    for k, s in enumerate((1, 2, 4)):
        blk = slice(S5_SUB * (k + 1), S5_SUB * (k + 2))
        sh = S5_SUB - s if reverse else s
        xr, xi = _cmul_add(xr, xi, tab_ref[blk, :S5_N], tab_ref[blk, S5_N:], pltpu.roll(xr, sh, 0), pltpu.roll(xi, sh, 0))
    return xr, xi


def _s5_fwd(z, b_st, c_st, d_skip, tab, name, tc=512):
    T = z.shape[0]
    tc = min(tc, T)
    nsub = tc // S5_SUB

    def body(u_ref, bst_ref, cst_ref, d_ref, tab_ref, h_ref, yc_ref, yg_ref, carry, x_sc):
        @pl.when(pl.program_id(0) == 0)
        def _():
            carry[...] = jnp.zeros_like(carry)

        u = u_ref[...].astype(F32)
        x_sc[...] = jnp.dot(u.astype(BF16), bst_ref[...], preferred_element_type=F32)

        def sub(b, c):
            r = pl.ds(pl.multiple_of(b * S5_SUB, S5_SUB), S5_SUB)
            xr, xi = _sub_scan(x_sc[r, :S5_N], x_sc[r, S5_N:], tab_ref, False)
            hr, hi = _cmul_add(xr, xi, tab_ref[0:S5_SUB, :S5_N], tab_ref[0:S5_SUB, S5_N:], c[0], c[1])
            x_sc[r, :S5_N] = hr
            x_sc[r, S5_N:] = hi
            return hr[S5_SUB - 1:S5_SUB], hi[S5_SUB - 1:S5_SUB]

        cr, ci = lax.fori_loop(0, nsub, sub, (carry[0:1, :S5_N], carry[0:1, S5_N:]))
        carry[0:1, :S5_N] = cr
        carry[0:1, S5_N:] = ci
        h = x_sc[...]
        h_ref[...] = h
        yc = jnp.dot(h.astype(BF16), cst_ref[...], preferred_element_type=F32)
        yc_ref[...] = yc
        yg_ref[...] = _gelu_skip(yc, u, d_ref[...]).astype(yg_ref.dtype)

    whole = lambda a: pl.BlockSpec(a.shape, lambda i: (0, 0))
    row = lambda w: pl.BlockSpec((tc, w), lambda i: (i, 0))
    return pl.pallas_call(
        body, name=name, grid=(T // tc,),
        in_specs=[pl.BlockSpec((tc, S5_W), lambda i: (i, Z_U // S5_W)), whole(b_st), whole(c_st), whole(d_skip), whole(tab)],
        out_specs=[row(2 * S5_N), row(S5_W), row(S5_W)],
        out_shape=[jax.ShapeDtypeStruct((T, 2 * S5_N), F32), jax.ShapeDtypeStruct((T, S5_W), F32),
                   jax.ShapeDtypeStruct((T, S5_W), BF16)],
        scratch_shapes=[pltpu.VMEM((8, 2 * S5_N), F32), pltpu.VMEM((tc, 2 * S5_N), F32)],
        compiler_params=pltpu.CompilerParams(dimension_semantics=("arbitrary",)),
    )(z, b_st, c_st, d_skip, tab)


def _s5_bwd(dyg, yc, z, h, b_st, c_st, d_skip, tab, name, tc=512):
    T = z.shape[0]
    tc = min(tc, T)
    nc, nsub = T // tc, tc // S5_SUB

    def body(dyg_ref, yc_ref, u_ref, h_ref, hp_ref, bst_ref, cst_ref, d_ref, tab_ref,
             du_ref, da_ref, db_ref, dct_ref, dd_ref, carry, x_sc):
        g = pl.program_id(0)

        @pl.when(g == 0)
        def _():
            carry[...] = jnp.zeros_like(carry)
            for r in (da_ref, db_ref, dct_ref, dd_ref):
                r[...] = jnp.zeros_like(r)

        u = u_ref[...].astype(F32)
        _, vjp = jax.vjp(_gelu_skip, yc_ref[...], u, d_ref[...])
        dyc, du_skip, dd = vjp(dyg_ref[...])
        dd_ref[...] += dd
        dyc16 = dyc.astype(BF16)
        dct_ref[...] += lax.dot_general(dyc16, h_ref[...].astype(BF16), _TN, preferred_element_type=F32)
        x_sc[...] = lax.dot_general(dyc16, cst_ref[...], _NT, preferred_element_type=F32)
        first = jnp.where(g == nc - 1, 0.0, 1.0)
        row0 = lax.broadcasted_iota(jnp.int32, (S5_SUB, S5_N), 0) == 0

        def sub(k, c):
            cr, ci, acc_r, acc_i = c
            b = nsub - 1 - k
            r = pl.ds(pl.multiple_of(b * S5_SUB, S5_SUB), S5_SUB)
            xr, xi = _sub_scan(x_sc[r, :S5_N], x_sc[r, S5_N:], tab_ref, True)
            lr, li = _cmul_add(xr, xi, tab_ref[0:S5_SUB, :S5_N], tab_ref[0:S5_SUB, S5_N:], cr, ci)
            x_sc[r, :S5_N] = lr
            x_sc[r, S5_N:] = li
            rp = pl.ds(pl.multiple_of(jnp.maximum(b - 1, 0) * S5_SUB, S5_SUB), S5_SUB)
            last_r = jnp.where(b == 0, hp_ref[S5_SUB - 1:S5_SUB, :S5_N] * first, h_ref[rp, :S5_N][S5_SUB - 1:S5_SUB])
            last_i = jnp.where(b == 0, hp_ref[S5_SUB - 1:S5_SUB, S5_N:] * first, h_ref[rp, S5_N:][S5_SUB - 1:S5_SUB])
            pr = jnp.where(row0, last_r, pltpu.roll(h_ref[r, :S5_N], 1, 0))
            pi = jnp.where(row0, last_i, pltpu.roll(h_ref[r, S5_N:], 1, 0))
            return lr[0:1], li[0:1], acc_r + (lr * pr + li * pi), acc_i + (li * pr - lr * pi)

        zero = jnp.zeros((S5_SUB, S5_N), F32)
        cr, ci, acc_r, acc_i = lax.fori_loop(0, nsub, sub, (carry[0:1, :S5_N], carry[0:1, S5_N:], zero, zero))
        carry[0:1, :S5_N] = cr
        carry[0:1, S5_N:] = ci
        da_ref[0:1, :S5_N] += jnp.sum(acc_r, axis=0, keepdims=True)
        da_ref[0:1, S5_N:] += jnp.sum(acc_i, axis=0, keepdims=True)
        lam16 = x_sc[...].astype(BF16)
        db_ref[...] += lax.dot_general(u.astype(BF16), lam16, _TN, preferred_element_type=F32)
        du = lax.dot_general(lam16, bst_ref[...], _NT, preferred_element_type=F32) + du_skip
        du_ref[...] = du.astype(du_ref.dtype)

    whole = lambda a: pl.BlockSpec(a.shape, lambda g: (0, 0))
    row = lambda w, off=0: pl.BlockSpec((tc, w), lambda g: (nc - 1 - g, off))
    prev = pl.BlockSpec((S5_SUB, 2 * S5_N), lambda g: (jnp.maximum((nc - 1 - g) * nsub - 1, 0), 0))
    acc = lambda s: pl.BlockSpec(s, lambda g: (0, 0))
    return pl.pallas_call(
        body, name=name, grid=(nc,),
        in_specs=[row(S5_W), row(S5_W), row(S5_W, Z_U // S5_W), row(2 * S5_N), prev, whole(b_st), whole(c_st),
                  whole(d_skip), whole(tab)],
        out_specs=[row(S5_W), acc((1, 2 * S5_N)), acc((S5_W, 2 * S5_N)), acc((S5_W, 2 * S5_N)), acc((1, S5_W))],
        out_shape=[jax.ShapeDtypeStruct((T, S5_W), BF16), jax.ShapeDtypeStruct((1, 2 * S5_N), F32),
                   jax.ShapeDtypeStruct((S5_W, 2 * S5_N), F32), jax.ShapeDtypeStruct((S5_W, 2 * S5_N), F32),
                   jax.ShapeDtypeStruct((1, S5_W), F32)],
        scratch_shapes=[pltpu.VMEM((8, 2 * S5_N), F32), pltpu.VMEM((tc, 2 * S5_N), F32)],
        compiler_params=pltpu.CompilerParams(dimension_semantics=("arbitrary",)),
    )(dyg, yc, z, h, h, b_st, c_st, d_skip, tab)


def _s5_discretize(lam_re, lam_im, log_dt, b_re, b_im):
    lr = jnp.minimum(lam_re, S5_MAX_RE)
    li = lam_im
    dt = jnp.exp(log_dt)[:, None]
    mag = jnp.exp(lr * dt)
    ang = li * dt
    ab_re = mag * jnp.cos(ang)
    ab_im = mag * jnp.sin(ang)
    den = lr * lr + li * li
    nr = ab_re - 1.0
    f_re = ((nr * lr + ab_im * li) / den)[..., None]
    f_im = ((ab_im * lr - nr * li) / den)[..., None]
    return ab_re, ab_im, f_re * b_re - f_im * b_im, f_re * b_im + f_im * b_re


def _adamw_fn(w, g, m, v):
    m = ADAM_B1 * m + (1.0 - ADAM_B1) * g
    v = ADAM_B2 * v + (1.0 - ADAM_B2) * (g * g)
    m_hat = m / (1.0 - ADAM_B1 ** ADAM_STEP)
    v_hat = v / (1.0 - ADAM_B2 ** ADAM_STEP)
    delta = -ADAM_LR * (m_hat / (jnp.sqrt(v_hat) + ADAM_EPS) + ADAM_WD * w)
    return delta, m, v


def _adamw(w, g, m, v, name):
    wd = w.shape[1]
    return _rowwise(_adamw_fn, name, [_whole(w), _whole(g), _whole(m), _whole(v)], [], [(wd, F32)] * 3,
                    tm=_row_tile(w.shape[0]))


def _adamw_many(ws, gs, ms, vs, name):
    n = len(ws)

    def body(*refs):
        for i in range(n):
            d_, m_, v_ = _adamw_fn(*[refs[kk * n + i][...] for kk in range(4)])
            refs[4 * n + i][...] = d_
            refs[5 * n + i][...] = m_
            refs[6 * n + i][...] = v_

    out = pl.pallas_call(body, name=name, out_shape=[jax.ShapeDtypeStruct(w.shape, F32) for w in ws] * 3)(*ws, *gs, *ms, *vs)
    return out[:n], out[n:2 * n], out[2 * n:]


def _me():
    return lax.axis_index("x"), lax.axis_index("y"), lax.axis_index("c")


def _chip_of(j):
    return j // 2, j % 2


_HBM = pl.BlockSpec(memory_space=pl.ANY)


def _gather_weights(shard):
    _, R, W = shard.shape

    def body(s_ref, out_ref, send_sems, recv_sems):
        x, y, c = _me()
        mine = 2 * x + y
        sib = (x, y, 1 - c)
        chips = [(1 - x, y), (x, 1 - y), (1 - x, 1 - y)]

        def rcopy(kk, src, chip_idx, half, to):
            return pltpu.make_async_remote_copy(src_ref=src, dst_ref=out_ref.at[chip_idx, half],
                                                send_sem=send_sems.at[kk], recv_sem=recv_sems.at[kk],
                                                device_id=to, device_id_type=MESH)

        first = [rcopy(j, s_ref.at[c], mine, c, (*chip, c)) for j, chip in enumerate(chips)]
        for cp in first:
            cp.start()
        passed = []
        for j, (px, py) in enumerate(chips):
            src_chip = 2 * px + py
            rcopy(j, s_ref.at[c], src_chip, c, (x, y, c)).wait_recv()
            fwd = rcopy(3 + j, out_ref.at[src_chip, c], src_chip, c, sib)
            fwd.start()
            passed.append(fwd)
        for j, (px, py) in enumerate(chips):
            rcopy(3 + j, s_ref.at[c], 2 * px + py, 1 - c, (x, y, c)).wait_recv()
        for cp in first + passed:
            cp.wait_send()

    return pl.pallas_call(
        body, name="gather_weights", in_specs=[_HBM], out_specs=_HBM,
        out_shape=jax.ShapeDtypeStruct((4, 2, R, W), shard.dtype),
        scratch_shapes=[pltpu.SemaphoreType.DMA((6,)), pltpu.SemaphoreType.DMA((6,))],
    )(shard)


def _pair_exchange(g, name):
    _, _, R, W = g.shape

    def body(g_ref, out_ref, send_sem, recv_sem):
        x, y, c = _me()
        cp = pltpu.make_async_remote_copy(src_ref=g_ref.at[:, 1 - c], dst_ref=out_ref, send_sem=send_sem,
                                          recv_sem=recv_sem, device_id=(x, y, 1 - c), device_id_type=MESH)
        cp.start()
        cp.wait()

    return pl.pallas_call(
        body, name=name, in_specs=[_HBM], out_specs=_HBM,
        out_shape=jax.ShapeDtypeStruct((4, R, W), g.dtype),
        scratch_shapes=[pltpu.SemaphoreType.DMA(()), pltpu.SemaphoreType.DMA(())],
    )(g)


def _pair_add(g, recv, cidx, name):
    _, _, R, W = g.shape
    tr = _row_tile(R)

    def body(c_ref, a_ref, b_ref, o_ref):
        o_ref[...] = (a_ref[...].astype(F32) + b_ref[...].astype(F32)).astype(o_ref.dtype)

    gs = pltpu.PrefetchScalarGridSpec(
        num_scalar_prefetch=1, grid=(4, R // tr),
        in_specs=[pl.BlockSpec((None, None, tr, W), lambda j, r, c: (j, c[0], r, 0)),
                  pl.BlockSpec((None, tr, W), lambda j, r, c: (j, r, 0))],
        out_specs=pl.BlockSpec((None, tr, W), lambda j, r, c: (j, r, 0)))
    return pl.pallas_call(body, name=name, grid_spec=gs,
                          out_shape=jax.ShapeDtypeStruct((4, R, W), BF16))(cidx, g, recv)


def _carried_exchange(src_ref, dst_ref, send_sems, recv_sems, spread, incoming):
    x, y, c = _me()
    mine = 2 * x + y
    copies = []
    for j, (px, py) in enumerate([(1 - x, y), (x, 1 - y), (1 - x, 1 - y)]):
        src = src_ref.at[2 * px + py] if spread else src_ref.at[c]
        slot = 2 * px + py if incoming else mine
        copies.append(pltpu.make_async_remote_copy(src_ref=src, dst_ref=dst_ref.at[slot], send_sem=send_sems.at[j],
                                                   recv_sem=recv_sems.at[j], device_id=(px, py, c), device_id_type=MESH))
    return copies


def _chip_exchange(p, name):
    _, R, W = p.shape

    def body(p_ref, out_ref, send_sems, recv_sems):
        outs = _carried_exchange(p_ref, out_ref, send_sems, recv_sems, True, False)
        for cp in outs:
            cp.start()
        for cp in _carried_exchange(p_ref, out_ref, send_sems, recv_sems, True, True):
            cp.wait_recv()
        for cp in outs:
            cp.wait_send()

    return pl.pallas_call(
        body, name=name, in_specs=[_HBM], out_specs=_HBM,
        out_shape=jax.ShapeDtypeStruct((4, R, W), p.dtype),
        scratch_shapes=[pltpu.SemaphoreType.DMA((3,)), pltpu.SemaphoreType.DMA((3,))],
    )(p)


def _chip_sum(q, name):
    _, R, W = q.shape
    tr = _row_tile(R)

    def body(q_ref, o_ref):
        f = lambda t: q_ref[t].astype(F32)
        o_ref[...] = ((f(0) + f(1)) + f(2)) + f(3)

    return pl.pallas_call(body, name=name, grid=(R // tr,),
                          in_specs=[pl.BlockSpec((4, tr, W), lambda r: (0, r, 0))],
                          out_specs=pl.BlockSpec((tr, W), lambda r: (r, 0)),
                          out_shape=jax.ShapeDtypeStruct((R, W), F32))(q)


def _pair_share(r, name):
    R, W = r.shape

    def body(r_ref, out_ref, send_sem, recv_sem):
        x, y, c = _me()
        cp = pltpu.make_async_remote_copy(src_ref=r_ref, dst_ref=out_ref, send_sem=send_sem, recv_sem=recv_sem,
                                          device_id=(x, y, 1 - c), device_id_type=MESH)
        cp.start()
        cp.wait()

    return pl.pallas_call(
        body, name=name, in_specs=[_HBM], out_specs=_HBM,
        out_shape=jax.ShapeDtypeStruct((R, W), r.dtype),
        scratch_shapes=[pltpu.SemaphoreType.DMA(()), pltpu.SemaphoreType.DMA(())],
    )(r)


def _allreduce_small(vec):
    R, W = vec.shape

    def body(v_ref, out_ref, buf, send_sems, recv_sems):
        x, y, c = _me()
        me = 4 * x + 2 * y + c
        buf[me] = v_ref[...]
        cps = []
        for kk in range(1, 8):
            peer = (x ^ (kk >> 2), y ^ ((kk >> 1) & 1), c ^ (kk & 1))
            cp = pltpu.make_async_remote_copy(src_ref=v_ref, dst_ref=buf.at[me], send_sem=send_sems.at[kk - 1],
                                              recv_sem=recv_sems.at[kk - 1], device_id=peer, device_id_type=MESH)
            cp.start()
            cps.append(cp)
        for kk in range(1, 8):
            peer = (x ^ (kk >> 2), y ^ ((kk >> 1) & 1), c ^ (kk & 1))
            pltpu.make_async_remote_copy(src_ref=v_ref, dst_ref=buf.at[me ^ kk], send_sem=send_sems.at[kk - 1],
                                         recv_sem=recv_sems.at[kk - 1], device_id=peer, device_id_type=MESH).wait_recv()
        for cp in cps:
            cp.wait_send()
        acc = buf[0]
        for d in range(1, 8):
            acc = acc + buf[d]
        out_ref[...] = acc

    vm = pl.BlockSpec(memory_space=pltpu.VMEM)
    return pl.pallas_call(
        body, name="allreduce_small", in_specs=[vm], out_specs=vm, out_shape=jax.ShapeDtypeStruct((R, W), F32),
        scratch_shapes=[pltpu.VMEM((8, R, W), F32), pltpu.SemaphoreType.DMA((7,)), pltpu.SemaphoreType.DMA((7,))],
    )(vec)


def _pack_flat(parts, align, dtype):
    flat = jnp.concatenate([p.reshape(-1).astype(dtype) for p in parts])
    n = flat.shape[0]
    pad = (-n) % align
    return jnp.pad(flat, (0, pad)) if pad else flat


def _unpack_flat(flat, shapes):
    out, off = [], 0
    for s in shapes:
        n = math.prod(s)
        out.append(flat[off:off + n].reshape(s))
        off += n
    return out


def _full_from_shards(name, sh):
    if name in COL_SHARDED:
        return jnp.transpose(sh, (1, 0, 2)).reshape(sh.shape[1], 4 * sh.shape[2])
    return sh.reshape(4 * sh.shape[1], sh.shape[2])


def _shards_from_full(name, full):
    if name in COL_SHARDED:
        r, cc = full.shape[0], full.shape[1] // 4
        return jnp.transpose(full.reshape(r, 4, cc), (1, 0, 2)).reshape(4, r * cc)
    return full.reshape(4, -1)


def _pad_heads(w, width):
    k = w.shape[0]
    return jnp.pad(w.reshape(k, HEADS, width), ((0, 0), (0, 0), (0, HP - width))).reshape(k, HEADS * HP)


def _unpad_heads(w, width):
    k = w.shape[0]
    return w.reshape(k, HEADS, HP)[:, :, :width].reshape(k, HEADS * width)


def _blockdiag(t):
    g, a, b = t.shape
    return jnp.einsum("gab,gk->gakb", t, jnp.eye(g, dtype=t.dtype)).reshape(g * a, g * b)


def _blockdiag_t(m, a, b):
    g = m.shape[0] // a
    return jnp.einsum("gagb->gab", m.reshape(g, a, g, b))


def _local_step(x, mem, positions, target, W, sp, overlap=None):
    T = x.shape[0]
    row = lambda v: v.reshape(1, -1)

    ab_re, ab_im, bb_re, bb_im = _s5_discretize(sp["s5_lam_re"], sp["s5_lam_im"], sp["s5_log_dt"], sp["s5_b_re"], sp["s5_b_im"])
    ar, ai = ab_re.reshape(1, S5_N), ab_im.reshape(1, S5_N)
    tab_fwd, tab_rev = _s5_tables(ar, ai, False), _s5_tables(ar, -ai, True)
    b_st = jnp.concatenate([_blockdiag(jnp.swapaxes(bb_re, 1, 2)), _blockdiag(jnp.swapaxes(bb_im, 1, 2))],
                           axis=1).astype(BF16)
    c_st = jnp.concatenate([_blockdiag(jnp.swapaxes(sp["s5_c_re"], 1, 2)),
                            -_blockdiag(jnp.swapaxes(sp["s5_c_im"], 1, 2))], axis=0).astype(BF16)
    inv = ROPE_THETA ** (-jnp.arange(0, ROPE, 2, dtype=F32) / ROPE)
    ang = positions.astype(F32)[:, None] * inv
    cs, sn = jnp.cos(ang), jnp.sin(ang)
    one, zero = jnp.ones((T, NOPE), F32), jnp.zeros((T, NOPE), F32)
    z16, tail1, tail0 = jnp.zeros((T, ROPE // 2), F32), jnp.ones((T, HP - QK), F32), jnp.zeros((T, HP - QK), F32)
    t_cos = jnp.concatenate([one, cs, cs, tail1], axis=1)
    t_s1 = jnp.concatenate([zero, z16, sn, tail0], axis=1)
    t_s2 = jnp.concatenate([zero, -sn, z16, tail0], axis=1)

    w_in = W["w_in"]
    w_in_re = jnp.concatenate([w_in[:, 800:], w_in[:, :768], w_in[:, 768:800],
                               jnp.zeros((D_MODEL, HP - ROPE), w_in.dtype)], axis=1)
    w_uq_p = _pad_heads(W["w_uq"], QK)
    wkv = W["w_ukv"].reshape(S5_W, HEADS, NOPE + VD)
    w_uk_p = _pad_heads(wkv[:, :, :NOPE].reshape(S5_W, HEADS * NOPE), NOPE)
    w_uv_p = _pad_heads(wkv[:, :, NOPE:].reshape(S5_W, HEADS * VD), VD)
    w_oa_p = jnp.pad(W["w_oa"].reshape(HEADS, VD, D_MODEL), ((0, 0), (0, HP - VD), (0, 0))).reshape(HEADS * HP, D_MODEL)

    g_in, b_in = row(sp["ln_in_g"]), row(sp["ln_in_b"])
    g1, b1, g2, b2, g3, b3 = (row(sp[k]) for k in ("ln1_g", "ln1_b", "ln2_g", "ln2_b", "ln3_g", "ln3_b"))
    gq, gkv, d_skip = row(sp["q_norm_g"]), row(sp["kv_norm_g"]), row(sp["s5_d"])

    (h0,) = _rowwise(lambda a, g, b: (_ln(a, g, b),), "ln_in_fwd", [_whole(x)], [g_in, b_in], [(D_MODEL, F32)], tm=512)
    z = _mm(h0, w_in_re, "in_proj", tm=512, tn=Z_W, out_dtype=BF16)
    hs, yc, yg = _s5_fwd(z, b_st, c_st, d_skip, tab_fwd, "s5_fwd")
    y2 = _mm(yg, W["w_glu"], "glu_proj", tn=2048, out_dtype=BF16)
    cqn, ckvn = _rowwise(lambda a, b, ga, gb: (_rms(a, ga), _rms(b, gb)), "mla_norm_fwd",
                         [(z, Z_CQ, S5_W), (z, Z_CKV, S5_W)], [gq, gkv], [(S5_W, BF16)] * 2)
    tabs = [(t_cos, 0, HP), (t_s1, 0, HP), (t_s2, 0, HP)]

    def rope_q(acc, tc_, t1, t2):
        return _rope(acc, _tile8(tc_), _tile8(t1), _tile8(t2)) * (ATT_SCALE * LOG2E)

    def rope_k(acc, krz, tc_, t1, t2):
        return acc + _tile8(_rope(pltpu.roll(krz.astype(F32), NOPE, 1), tc_, t1, t2))

    q = _mm(cqn, w_uq_p, "mla_uq_rope", epi=rope_q, rextras=tabs, out_dtype=BF16)
    k = _mm(ckvn, w_uk_p, "mla_uk_rope", epi=rope_k, rextras=[(z, Z_KR, HP)] + tabs, out_dtype=BF16)
    v = _mm(ckvn, w_uv_p, "mla_uv", out_dtype=BF16)
    if overlap is None:
        o, lse = _flash_fwd(q, k, v, "mla_attn_fwd")
    else:
        o, lse, landed = _flash_fwd(q, k, v, "mla_attn_fwd", carry=(overlap[0], False))
        W = {**W, **overlap[1](landed)}
    def gate_epi(acc, gs_, ga_, ya, yb):
        f = lambda t: t.astype(F32)
        return acc, _gate_mix(f(gs_), f(ga_), f(ya), f(yb), acc.astype(BF16).astype(F32))

    a_out, mixin = _mm(o, w_oa_p, "mla_oa_gate", epi=gate_epi, outs=[(D_MODEL, BF16)] * 2, tm=512,
                       rextras=[(z, Z_GS, D_MODEL), (z, Z_GA, D_MODEL), (y2, 0, D_MODEL), (y2, D_MODEL, D_MODEL)])
    def ln_epi(acc, h, g, b):
        s_ = DN_ALPHA * h + acc
        return s_, _ln(s_, g, b)

    two_rows = [(D_MODEL, F32)] * 2
    s1, h1 = _mm(mixin, W["w_o"], "mix_o_ln1", epi=ln_epi, extras=(h0,), pextras=(g1, b1), outs=two_rows, tm=512)
    xq = _mm(h1, W["w_xq"], "x_q", out_dtype=BF16)
    xk = _mm(mem, W["w_xk"], "x_k", out_dtype=BF16)
    xv = _mm(mem, W["w_xv"], "x_v", out_dtype=BF16)
    (xo,) = _rowwise(_xattn_fwd_fn, "xattn_fwd", [_whole(xq)], [xk, xv], [(D_MODEL, BF16)], tm=1024, upcast=False)
    s2, h2 = _mm(xo, W["w_xo"], "x_o_ln2", epi=ln_epi, extras=(h1,), pextras=(g2, b2), outs=two_rows, tm=512)
    a_up = _mm(h2, W["w_up"], "mlp_up", out_dtype=BF16)
    def loss_epi(acc, h, tgt, g, b):
        def f(r_, g_, b_):
            e = _ln_res(h, r_, g_, b_) - tgt
            return 0.5 * jnp.sum(jnp.mean(e * e, axis=-1))
        lv, (dr_, dg_, db_) = jax.value_and_grad(f, argnums=(0, 1, 2))(acc, g, b)
        return dr_, dg_, db_, jnp.broadcast_to(lv, (1, 128))

    row_and_sums = [(D_MODEL, F32), (D_MODEL, "sum"), (D_MODEL, "sum")]
    dr3, dg3, db3, lossv = _mm(a_up, W["w_down"], "mlp_down_loss", act=_relu2, tm=512, tk=4 * D_MODEL, epi=loss_epi,
                               extras=(h2, target), pextras=(g3, b3), outs=row_and_sums + [(128, "sum")])
    gW, gs = {}, {"ln3_g": dg3, "ln3_b": db3}

    da = _mm(dr3, W["w_down"], "mlp_down_bwd_a", "nt", epi=lambda acc, a: acc * (2.0 * jnp.maximum(a, 0.0)), extras=(a_up,),
             out_dtype=BF16)
    gW["w_down"] = _mm(a_up, dr3, "mlp_down_bwd_w", "tn", m=4 * D_MODEL, n=D_MODEL, act=_relu2, tk=2048)
    gW["w_up"] = _mm(h2, da, "mlp_up_bwd_w", "tn", m=D_MODEL, n=4 * D_MODEL, tk=2048)

    def ln_bwd_epi(acc, e, s_, g, b):
        _, vjp = jax.vjp(_ln, s_, g, b)
        return vjp(acc + DN_ALPHA * e)

    dr2, gs["ln2_g"], gs["ln2_b"] = _mm(da, W["w_up"], "mlp_up_bwd_ln2", "nt", tm=512, tk=4 * D_MODEL, epi=ln_bwd_epi,
                                        extras=(dr3, s2), pextras=(g2, b2), outs=row_and_sums)
    dxo = _mm(dr2, W["w_xo"], "x_o_bwd_a", "nt", out_dtype=BF16)
    gW["w_xo"] = _mm(xo, dr2, "x_o_bwd_w", "tn", m=D_MODEL, n=D_MODEL)
    dxq, dxk, dxv = _rowwise(_xattn_bwd_fn, "xattn_bwd", [_whole(xq), _whole(dxo)], [xk, xv], [(D_MODEL, BF16)],
                             [(xk.shape[0], D_MODEL)] * 2, tm=1024, upcast=False)
    gW["w_xq"] = _mm(h1, dxq, "x_q_bwd_w", "tn", m=D_MODEL, n=D_MODEL)
    gW["w_xk"] = _mm(mem, dxk, "x_k_bwd_w", "tn", m=D_MODEL, n=D_MODEL)
    gW["w_xv"] = _mm(mem, dxv, "x_v_bwd_w", "tn", m=D_MODEL, n=D_MODEL)

    dr1, gs["ln1_g"], gs["ln1_b"] = _mm(dxq, W["w_xq"], "x_q_bwd_ln1", "nt", tm=512, epi=ln_bwd_epi,
                                        extras=(dr2, s1), pextras=(g1, b1), outs=row_and_sums)
    gW["w_o"] = _mm(mixin, dr1, "mix_o_bwd_w", "tn", m=D_MODEL, n=D_MODEL)

    def gate_bwd_epi(acc, *tiles):
        _, vjp = jax.vjp(_gate_mix, *[t.astype(F32) for t in tiles])
        return vjp(acc)

    dgs, dga, dy2a, dy2b, da_out = _mm(
        dr1, W["w_o"], "mix_o_bwd_gate", "nt", tm=512, epi=gate_bwd_epi, outs=[(D_MODEL, BF16)] * 5,
        rextras=[(z, Z_GS, D_MODEL), (z, Z_GA, D_MODEL), (y2, 0, D_MODEL), (y2, D_MODEL, D_MODEL), _whole(a_out)])

    def delta_epi(acc, o_):
        prod = acc * o_
        cols = [jnp.sum(prod[:, h * HP:(h + 1) * HP], axis=1, keepdims=True) for h in range(HEADS)]
        return acc, jnp.concatenate(cols, axis=1)

    do16, delta = _mm(da_out, w_oa_p, "mla_oa_bwd_a", "nt", epi=delta_epi, extras=(o,),
                      outs=[(HEADS * HP, BF16), (HEADS, F32)], tm=512)
    g_oa_p = _mm(o, da_out, "mla_oa_bwd_w", "tn", m=HEADS * HP, n=D_MODEL)
    gW["w_oa"] = g_oa_p.reshape(HEADS, HP, D_MODEL)[:, :VD].reshape(HEADS * VD, D_MODEL)
    delta_row = delta.T.reshape(HEADS, 1, T)
    if overlap is None:
        dq, dk, dv = _flash_bwd(q, k, v, do16, lse, delta_row, "mla_attn_bwd")
        riding = None
    else:
        sent = overlap[2]({n: gW[n] for n in BIG_LATE})
        dq, dk, dv, landed = _flash_bwd(q, k, v, do16, lse, delta_row, "mla_attn_bwd", carry=(sent, True))
        riding = (sent, landed)

    def rope_bwd(dq_, dk_, tc_, t1, t2):
        dqr = _rope_t(dq_, _tile8(tc_), _tile8(t1), _tile8(t2))
        dkr = dk_[:, 0:HP]
        for hh in range(1, HEADS):
            dkr = dkr + dk_[:, hh * HP:(hh + 1) * HP]
        dkr = _rope_t(jnp.where(_rope_lanes(dkr.shape), dkr, 0.0), tc_, t1, t2)
        dkr = pltpu.roll(dkr, NOPE, 1)
        lane = lax.broadcasted_iota(jnp.int32, dkr.shape, 1)
        return dqr, jnp.where(lane < ROPE, dkr, 0.0)

    dq_raw, dkr = _rowwise(rope_bwd, "mla_rope_bwd", [_whole(dq), _whole(dk), _whole(t_cos), _whole(t_s1), _whole(t_s2)], [],
                           [(HEADS * HP, BF16), (HP, BF16)])
    dcqn = _mm(dq_raw, w_uq_p, "mla_uq_bwd_a", "nt")
    gW["w_uq"] = _unpad_heads(_mm(cqn, dq_raw, "mla_uq_bwd_w", "tn", m=S5_W, n=HEADS * HP), QK)
    dckvn_k = _mm(dk, w_uk_p, "mla_uk_bwd_a", "nt")
    dckvn = _mm(dv, w_uv_p, "mla_uv_bwd_a", "nt", epi=lambda acc, e: acc + e, extras=(dckvn_k,))
    g_uk = _unpad_heads(_mm(ckvn, dk, "mla_uk_bwd_w", "tn", m=S5_W, n=HEADS * HP), NOPE).reshape(S5_W, HEADS, NOPE)
    g_uv = _unpad_heads(_mm(ckvn, dv, "mla_uv_bwd_w", "tn", m=S5_W, n=HEADS * HP), VD).reshape(S5_W, HEADS, VD)
    gW["w_ukv"] = jnp.concatenate([g_uk, g_uv], axis=2).reshape(S5_W, HEADS * (NOPE + VD))

    def norm_bwd(a, b, da_, db_, ga, gb):
        _, vja = jax.vjp(_rms, a, ga)
        _, vjb = jax.vjp(_rms, b, gb)
        dxa, dga_ = vja(da_)
        dxb, dgb_ = vjb(db_)
        return dxa, dxb, dga_, dgb_

    dcq, dckv, gs["q_norm_g"], gs["kv_norm_g"] = _rowwise(
        norm_bwd, "mla_norm_bwd", [(z, Z_CQ, S5_W), (z, Z_CKV, S5_W), _whole(dcqn), _whole(dckvn)], [gq, gkv],
        [(S5_W, BF16)] * 2, [(1, S5_W)] * 2)

    dy2 = jnp.concatenate([dy2a, dy2b], axis=1)
    dyg = _mm(dy2, W["w_glu"], "glu_bwd_a", "nt")
    gW["w_glu"] = _mm(yg, dy2, "glu_bwd_w", "tn", m=S5_W, n=2 * D_MODEL)

    du, gs["s5_ab"], g_bst, g_cst_t, gs["s5_d"] = _s5_bwd(dyg, yc, z, hs, b_st, c_st, d_skip, tab_rev, "s5_bwd")
    gs["s5_bb_re"] = jnp.swapaxes(_blockdiag_t(g_bst[:, :S5_N], S5_H, S5_P), 1, 2)
    gs["s5_bb_im"] = jnp.swapaxes(_blockdiag_t(g_bst[:, S5_N:], S5_H, S5_P), 1, 2)
    gs["s5_c_re"] = _blockdiag_t(g_cst_t[:, :S5_N], S5_H, S5_P)
    gs["s5_c_im"] = -_blockdiag_t(g_cst_t[:, S5_N:], S5_H, S5_P)

    dz = jnp.concatenate([dgs, dga, du, dcq, dckv, dkr], axis=1)
    g_in_re = _mm(h0, dz, "in_proj_bwd_w", "tn", m=D_MODEL, n=Z_W, tm=512, tn=Z_W)
    gW["w_in"] = jnp.concatenate([g_in_re[:, Z_U:Z_KR], g_in_re[:, Z_KR:Z_KR + ROPE], g_in_re[:, :Z_U]], axis=1)
    dx, gs["ln_in_g"], gs["ln_in_b"] = _mm(dz, w_in_re, "in_proj_bwd_ln_in", "nt", tm=512, tk=Z_W, epi=ln_bwd_epi,
                                           extras=(dr1, x), pextras=(g_in, b_in), outs=row_and_sums)
    return lossv[:, :1], dx, gW, gs, riding


_RAW_SMALL = (("loss", (1, 1)), ("ln_in_g", (1, D_MODEL)), ("ln_in_b", (1, D_MODEL)), ("ln1_g", (1, D_MODEL)),
              ("ln1_b", (1, D_MODEL)), ("ln2_g", (1, D_MODEL)), ("ln2_b", (1, D_MODEL)), ("ln3_g", (1, D_MODEL)),
              ("ln3_b", (1, D_MODEL)), ("q_norm_g", (1, S5_W)), ("kv_norm_g", (1, S5_W)), ("s5_d", (1, S5_W)),
              ("s5_ab", (1, 2 * S5_N)), ("s5_bb_re", (S5_G, S5_P, S5_H)), ("s5_bb_im", (S5_G, S5_P, S5_H)),
              ("s5_c_re", (S5_G, S5_H, S5_P)), ("s5_c_im", (S5_G, S5_H, S5_P)))


def kernel(x, mem, positions, ln_in_g, ln_in_b, w_in, s5_lam_re, s5_lam_im, s5_log_dt, s5_b_re, s5_b_im, s5_c_re, s5_c_im, s5_d, w_glu, q_norm_g, w_uq, kv_norm_g, w_ukv, w_oa, w_o, ln1_g, ln1_b, w_xq, w_xk, w_xv, w_xo, ln2_g, ln2_b, w_up, w_down, ln3_g, ln3_b, loss_target, m_ln_in_g, m_ln_in_b, m_w_in, m_s5_lam_re, m_s5_lam_im, m_s5_log_dt, m_s5_b_re, m_s5_b_im, m_s5_c_re, m_s5_c_im, m_s5_d, m_w_glu, m_q_norm_g, m_w_uq, m_kv_norm_g, m_w_ukv, m_w_oa, m_w_o, m_ln1_g, m_ln1_b, m_w_xq, m_w_xk, m_w_xv, m_w_xo, m_ln2_g, m_ln2_b, m_w_up, m_w_down, m_ln3_g, m_ln3_b, v_ln_in_g, v_ln_in_b, v_w_in, v_s5_lam_re, v_s5_lam_im, v_s5_log_dt, v_s5_b_re, v_s5_b_im, v_s5_c_re, v_s5_c_im, v_s5_d, v_w_glu, v_q_norm_g, v_w_uq, v_kv_norm_g, v_w_ukv, v_w_oa, v_w_o, v_ln1_g, v_ln1_b, v_w_xq, v_w_xk, v_w_xv, v_w_xo, v_ln2_g, v_ln2_b, v_w_up, v_w_down, v_ln3_g, v_ln3_b):
    a = dict(locals())
    wts = {n: a[n] for n in WEIGHTS}
    ms = {n: a["m_" + n] for n in WEIGHTS}
    vs = {n: a["v_" + n] for n in WEIGHTS}
    shard2d = {n: wts[n].reshape(wts[n].shape[-2], wts[n].shape[-1]) for n in BIG}
    nrow = {n: shard2d[n].size // 1024 for n in BIG}
    mine = 2 * lax.axis_index("x") + lax.axis_index("y")
    my_c = lax.axis_index("c")
    cidx = my_c.astype(jnp.int32).reshape(1)

    def group_rows(group):
        rows = sum(nrow[n] for n in group)
        return rows, (-rows) % PACK_ROWS

    def pack_shards(group):
        rows, pad = group_rows(group)
        parts = [shard2d[n].astype(BF16).reshape(nrow[n], 1024) for n in group] + [jnp.zeros((pad, 1024), BF16)]
        return jnp.concatenate(parts, axis=0).reshape(2, (rows + pad) // 2, 1024)

    def unpack_full(group, gathered):
        out, off = {}, 0
        for n in group:
            r, cc = shard2d[n].shape
            piece = lax.optimization_barrier(gathered[:, off:off + nrow[n]])
            out[n] = _full_from_shards(n, piece.reshape(4, r, cc))
            off += nrow[n]
        return out

    def pack_grads(group, gw):
        rows, pad = group_rows(group)
        parts = [_shards_from_full(n, gw[n].astype(BF16)).reshape(4, nrow[n], 1024) for n in group]
        return jnp.concatenate(parts + [jnp.zeros((4, pad, 1024), BF16)], axis=1).reshape(4, 2, (rows + pad) // 2, 1024)

    def by_core(own, other):
        return jnp.where(my_c == 0, jnp.stack([own, other]), jnp.stack([other, own]))

    packed_e, packed_l = pack_shards(BIG_EARLY), pack_shards(BIG_LATE)
    gathered = lax.dynamic_update_slice(_gather_weights(packed_e), packed_e[None], (mine, 0, 0, 0))
    W = unpack_full(BIG_EARLY, gathered.reshape(4, -1, 1024))

    def finish_gather(landed):
        half = packed_l.shape[1]
        own_half = lax.dynamic_index_in_dim(packed_l, my_c, axis=0, keepdims=True)
        landed = lax.dynamic_update_slice(landed, own_half, (mine, 0, 0))
        other = _pair_share(landed.reshape(4 * half, 1024), "gather_late_share").reshape(4, half, 1024)
        out, off = {}, 0
        for n in BIG_LATE:
            r, cc = shard2d[n].shape
            h, lo = divmod(off, half)
            assert lo + nrow[n] <= half, n
            piece = jnp.where(my_c == h, landed[:, lo:lo + nrow[n]], other[:, lo:lo + nrow[n]])
            out[n] = _full_from_shards(n, lax.optimization_barrier(piece).reshape(4, r, cc))
            off += nrow[n]
        return out

    def pair_reduce(group, gw, tag):
        gflat = pack_grads(group, gw)
        return _pair_add(gflat, _pair_exchange(gflat, "grad_pair_exchange_" + tag), cidx, "grad_pair_add_" + tag)

    def finish_reduce(group, sent, landed, tag):
        own = lax.dynamic_slice(sent, (mine, 0, 0), (1,) + sent.shape[1:])
        red = _chip_sum(lax.dynamic_update_slice(landed, own, (mine, 0, 0)), "grad_chip_sum_" + tag)
        gsh = by_core(red, _pair_share(red, "grad_pair_share_" + tag)).reshape(-1, 1024)
        out, off = {}, 0
        for n in group:
            out[n] = lax.optimization_barrier(gsh[off:off + nrow[n]]).reshape(shard2d[n].shape)
            off += nrow[n]
        return out

    sp = {n: wts[n] for n in SMALL}
    sp_local = {n: (sp[n][0] if sp[n].ndim > 1 else sp[n]) for n in SMALL}
    overlap = (packed_l, finish_gather, lambda gw: pair_reduce(BIG_LATE, gw, "late"))
    lossv, dx, gW, gs, riding = _local_step(x[0], mem[0], positions[0], loss_target[0], W, sp_local, overlap)

    g_out = finish_reduce(BIG_LATE, riding[0], riding[1], "late")
    sent_e = pair_reduce(BIG_EARLY, gW, "early")
    g_out.update(finish_reduce(BIG_EARLY, sent_e, _chip_exchange(sent_e, "grad_chip_exchange_early"), "early"))

    gs["loss"] = lossv
    raw = _pack_flat([gs[n].reshape(s) for n, s in _RAW_SMALL], 8 * 1024, F32)
    raw = _allreduce_small(raw.reshape(-1, 1024)).reshape(-1)
    rs = dict(zip([n for n, _ in _RAW_SMALL], _unpack_flat(raw, [s for _, s in _RAW_SMALL])))
    loss = rs["loss"].reshape(())
    _, disc_vjp = jax.vjp(_s5_discretize, sp_local["s5_lam_re"], sp_local["s5_lam_im"], sp_local["s5_log_dt"],
                          sp_local["s5_b_re"], sp_local["s5_b_im"])
    d_ab = rs["s5_ab"].reshape(2, S5_G, S5_P)
    g_lre, g_lim, g_ldt, g_bre, g_bim = disc_vjp((d_ab[0], d_ab[1], rs["s5_bb_re"], rs["s5_bb_im"]))
    small_g = {"s5_lam_re": g_lre, "s5_lam_im": g_lim, "s5_log_dt": g_ldt, "s5_b_re": g_bre, "s5_b_im": g_bim,
               "s5_c_re": rs["s5_c_re"], "s5_c_im": rs["s5_c_im"]}
    for n in ("ln_in_g", "ln_in_b", "ln1_g", "ln1_b", "ln2_g", "ln2_b", "ln3_g", "ln3_b", "q_norm_g", "kv_norm_g", "s5_d"):
        small_g[n] = rs[n]

    grads, deltas, new_m, new_v = {}, {}, {}, {}
    for n in BIG:
        d_, m_, v_ = _adamw(shard2d[n], g_out[n], ms[n].reshape(shard2d[n].shape), vs[n].reshape(shard2d[n].shape), "adamw_" + n)
        grads[n] = g_out[n].reshape(wts[n].shape)
        deltas[n], new_m[n], new_v[n] = (t.reshape(wts[n].shape) for t in (d_, m_, v_))
    as2d = lambda t: t.reshape(-1, t.shape[-1])
    sd, sm, sv = _adamw_many([as2d(wts[n]) for n in SMALL], [as2d(small_g[n].reshape(wts[n].shape)) for n in SMALL],
                             [as2d(ms[n]) for n in SMALL], [as2d(vs[n]) for n in SMALL], "adamw_small")
    for n, d_, m_, v_ in zip(SMALL, sd, sm, sv):
        grads[n] = small_g[n].reshape(wts[n].shape)
        deltas[n], new_m[n], new_v[n] = (t.reshape(wts[n].shape) for t in (d_, m_, v_))

    return (loss, dx[None], *[grads[n] for n in WEIGHTS], *[deltas[n] for n in WEIGHTS],
            *[new_m[n] for n in WEIGHTS], *[new_v[n] for n in WEIGHTS])
```

```python
import functools
import math

import jax
import jax.numpy as jnp
from jax import lax
from jax.experimental import pallas as pl
from jax.experimental.pallas import tpu as pltpu

F32 = jnp.float32
BF16 = jnp.bfloat16
MESH = pl.DeviceIdType.MESH

D_MODEL = 1024
S5_W = 256
S5_G = 16
S5_H = 16
S5_P = 64
S5_N = S5_G * S5_P
S5_MAX_RE = -1e-4
HEADS = 8
NOPE = 64
ROPE = 32
QK = NOPE + ROPE
VD = 64
HP = 128
XH = 4
XD = 256
LN_EPS = 1e-5
RMS_EPS = 1e-6
NEG_INF = -1e30
DN_ALPHA = 2.0 ** 0.25
ROPE_THETA = 10000.0
ADAM_LR, ADAM_B1, ADAM_B2, ADAM_EPS, ADAM_WD, ADAM_STEP = 0.001, 0.9, 0.999, 1e-08, 0.01, 10

Z_GS, Z_GA, Z_U, Z_CQ, Z_CKV, Z_KR, Z_W = 0, 1024, 2048, 2304, 2560, 2816, 2944
SCAN_CHUNK = 256

BIG_EARLY = ("w_in", "w_glu", "w_uq", "w_ukv", "w_oa", "w_o")
BIG_LATE = ("w_up", "w_xq", "w_xk", "w_xv", "w_xo", "w_down")
BIG = BIG_EARLY + BIG_LATE
COL_SHARDED = ("w_in", "w_glu", "w_uq", "w_ukv", "w_oa", "w_up")
SMALL = ("ln_in_g", "ln_in_b", "s5_lam_re", "s5_lam_im", "s5_log_dt", "s5_b_re", "s5_b_im", "s5_c_re", "s5_c_im",
         "s5_d", "q_norm_g", "kv_norm_g", "ln1_g", "ln1_b", "ln2_g", "ln2_b", "ln3_g", "ln3_b")
WEIGHTS = ("ln_in_g", "ln_in_b", "w_in", "s5_lam_re", "s5_lam_im", "s5_log_dt", "s5_b_re", "s5_b_im", "s5_c_re",
           "s5_c_im", "s5_d", "w_glu", "q_norm_g", "w_uq", "kv_norm_g", "w_ukv", "w_oa", "w_o", "ln1_g", "ln1_b",
           "w_xq", "w_xk", "w_xv", "w_xo", "ln2_g", "ln2_b", "w_up", "w_down", "ln3_g", "ln3_b")
PACK_ROWS = 2 * 16


def _row_tile(n, cap=512):
    best = n
    for t in range(8, min(n, cap) + 1, 8):
        if n % t == 0:
            best = t
    return best


def _pick(n, cap):
    best = None
    for t in range(128, min(n, cap) + 1, 128):
        if n % t == 0:
            best = t
    return best if best is not None and (best >= 512 or best == n) else n


def _mm(a, b, name, mode="nn", *, a_off=0, b_off=0, m=None, n=None, act=None, epi=None, extras=(), rextras=(),
        pextras=(), outs=None, out_dtype=None, tm=1024, tn=1024, tk=None):
    out_dtype = out_dtype or (BF16 if mode == "tn" else F32)
    if mode == "nn":
        M, (K, N) = a.shape[0], b.shape
    elif mode == "nt":
        M, (N, K) = a.shape[0], b.shape
    else:
        K, M, N = a.shape[0], m, n
    if mode == "tn":
        tm, tn, tk = _pick(M, tm), _pick(N, tn), min(tk or 1024, K)
    else:
        tm, tn, tk = min(tm, M), _pick(N, tn), _pick(K, tk or 1024)
    assert M % tm == 0 and N % tn == 0 and K % tk == 0, (name, M, N, K, tm, tn, tk)
    nk = K // tk
    if mode == "tn":
        assert a_off % tm == 0 and b_off % tn == 0
        ao, bo = a_off // tm, b_off // tn
        a_spec = pl.BlockSpec((tk, tm), lambda i, j, k: (k, i + ao))
        b_spec = pl.BlockSpec((tk, tn), lambda i, j, k: (k, j + bo))
        dims = (((0,), (0,)), ((), ()))
    else:
        assert a_off % tk == 0
        ao = a_off // tk
        a_spec = pl.BlockSpec((tm, tk), lambda i, j, k: (i, k + ao))
        if mode == "nn":
            b_spec = pl.BlockSpec((tk, tn), lambda i, j, k: (k, j))
            dims = (((1,), (0,)), ((), ()))
        else:
            b_spec = pl.BlockSpec((tn, tk), lambda i, j, k: (j, k))
            dims = (((1,), (1,)), ((), ()))
    e_spec = pl.BlockSpec((tm, tn), lambda i, j, k: (i, j))
    r_specs = [pl.BlockSpec((tm, w), functools.partial(lambda i, j, k, o: (i, o), o=off // w)) for _, off, w in rextras]
    p_specs = [pl.BlockSpec((1, tn), lambda i, j, k: (0, j)) if p.shape[1] == N else pl.BlockSpec(p.shape, lambda i, j, k: (0, 0))
               for p in pextras]
    n_extra = len(extras) + len(rextras) + len(pextras)
    if outs is None:
        o_specs, o_shapes = [e_spec], [jax.ShapeDtypeStruct((M, N), out_dtype)]
    else:
        assert all((w == tn and dt != "sum") or N == tn for w, dt in outs), name
        o_specs = [e_spec if w == tn and dt != "sum" else pl.BlockSpec((8 if dt == "sum" else tm, w), lambda i, j, k: (i, 0))
                   for w, dt in outs]
        o_shapes = [jax.ShapeDtypeStruct((M // tm * 8, w), F32) if dt == "sum" else
                    jax.ShapeDtypeStruct((M, N if w == tn else w), dt) for w, dt in outs]
    n_out = len(o_specs)
    direct = outs is None and epi is None and out_dtype == F32

    def body(*refs):
        a_ref, b_ref = refs[0], refs[1]
        e_refs = refs[2:2 + n_extra]
        o_refs = refs[2 + n_extra:2 + n_extra + n_out]
        av = a_ref[...]
        if act is not None:
            av = act(av.astype(F32))
        p = lax.dot_general(av.astype(BF16), b_ref[...].astype(BF16), dims, preferred_element_type=F32)

        def finish(r):
            if epi is not None:
                r = epi(r, *[e[...] for e in e_refs])
            for o_ref, v in zip(o_refs, r if isinstance(r, tuple) else (r,)):
                o_ref[...] = jnp.broadcast_to(v, o_ref.shape).astype(o_ref.dtype)

        if nk == 1:
            finish(p)
        else:
            acc = o_refs[0] if direct else refs[2 + n_extra + n_out]
            k = pl.program_id(2)

            @pl.when(k == 0)
            def _():
                acc[...] = p

            @pl.when(k > 0)
            def _():
                acc[...] += p

            if not direct:
                @pl.when(k == nk - 1)
                def _():
                    finish(acc[...])

    res = pl.pallas_call(
        body, name=name, grid=(M // tm, N // tn, nk),
        in_specs=[a_spec, b_spec] + [e_spec] * len(extras) + r_specs + p_specs, out_specs=o_specs,
        out_shape=o_shapes,
        scratch_shapes=[pltpu.VMEM((tm, tn), F32)] if nk > 1 and not direct else [],
        compiler_params=pltpu.CompilerParams(dimension_semantics=("parallel", "parallel", "arbitrary")),
    )(a, b, *extras, *[r[0] for r in rextras], *pextras)
    if outs is None:
        return res[0]
    return [r.reshape(M // tm, 8, -1)[:, 0].sum(axis=0, keepdims=True) if dt == "sum" else r for r, (_, dt) in zip(res, outs)]


def _rowwise(fn, name, rows, pars, outs, accs=(), tm=256, upcast=True):
    T = rows[0][0].shape[0]
    tm = min(tm, T)
    assert T % tm == 0
    n_in, n_out = len(rows) + len(pars), len(outs)
    in_specs = []
    for arr, off, w in rows:
        assert off % w == 0 and arr.shape[0] == T, name
        in_specs.append(pl.BlockSpec((tm, w), functools.partial(lambda i, o: (i, o), o=off // w)))
    for p in pars:
        in_specs.append(pl.BlockSpec(p.shape, lambda i: (0, 0)))
    out_specs = [pl.BlockSpec((tm, w), lambda i: (i, 0)) for w, _ in outs]
    out_specs += [pl.BlockSpec(s, lambda i: (0, 0)) for s in accs]
    out_shape = [jax.ShapeDtypeStruct((T, w), dt) for w, dt in outs] + [jax.ShapeDtypeStruct(s, F32) for s in accs]

    def body(*refs):
        res = fn(*[r[...].astype(F32) if upcast else r[...] for r in refs[:n_in]])
        o_refs = refs[n_in:]
        for r, v in zip(o_refs[:n_out], res[:n_out]):
            r[...] = v.astype(r.dtype)
        if accs:
            i = pl.program_id(0)

            @pl.when(i == 0)
            def _():
                for r in o_refs[n_out:]:
                    r[...] = jnp.zeros_like(r)

            for r, v in zip(o_refs[n_out:], res[n_out:]):
                r[...] += v

    res = pl.pallas_call(
        body, name=name, grid=(T // tm,), in_specs=in_specs, out_specs=out_specs, out_shape=out_shape,
        compiler_params=pltpu.CompilerParams(dimension_semantics=("arbitrary",)),
    )(*[r[0] for r in rows], *pars)
    return res


def _whole(arr):
    return (arr, 0, arr.shape[1])


def _ln(x, g, b):
    mu = jnp.mean(x, axis=-1, keepdims=True)
    xc = x - mu
    var = jnp.mean(xc * xc, axis=-1, keepdims=True)
    return xc * lax.rsqrt(var + LN_EPS) * g + b


def _ln_res(h, r, g, b):
    return _ln(DN_ALPHA * h + r, g, b)


def _rms(x, g):
    return x * lax.rsqrt(jnp.mean(x * x, axis=-1, keepdims=True) + RMS_EPS) * g


def _gelu_skip(yc, u, d):
    y = yc + d * u
    return 0.5 * y * (1.0 + lax.erf(y * (1.0 / math.sqrt(2.0))))


def _gate_mix(gs, ga, y2a, y2b, aout):
    return jax.nn.sigmoid(gs) * (y2a * jax.nn.sigmoid(y2b)) + jax.nn.sigmoid(ga) * aout


def _relu2(a):
    r = jnp.maximum(a, 0.0)
    return r * r


def _tile8(t):
    return jnp.concatenate([t] * HEADS, axis=1)


def _rope(x, cos, s1, s2):
    w = x.shape[1]
    return x * cos + pltpu.roll(x, ROPE // 2, 1) * s1 + pltpu.roll(x, w - ROPE // 2, 1) * s2


def _rope_t(dy, cos, s1, s2):
    w = dy.shape[1]
    return dy * cos + pltpu.roll(dy * s1, w - ROPE // 2, 1) + pltpu.roll(dy * s2, ROPE // 2, 1)


def _rope_lanes(shape):
    lane = lax.broadcasted_iota(jnp.int32, shape, 1) % HP
    return (lane >= NOPE) & (lane < QK)


ATT_HB_FWD, ATT_HB_BWD = 8, 8
ATT_SCALE = QK ** -0.5
LOG2E = 1.4426950408889634
LN2 = 0.6931471805599453
_NT = (((1,), (1,)), ((), ()))
_TN = (((0,), (0,)), ((), ()))


def _carry_specs(carry):
    if carry is None:
        return [], [], [], []
    arr = carry[0]
    return ([_HBM], [_HBM], [jax.ShapeDtypeStruct((4,) + arr.shape[1:], arr.dtype)],
            [pltpu.SemaphoreType.DMA((3,)), pltpu.SemaphoreType.DMA((3,))])


def _carry_start(exchange, when):
    @pl.when(when)
    def _():
        for cp in exchange(False):
            cp.start()


def _carry_wait(exchange, when):
    @pl.when(when)
    def _():
        for cp in exchange(True):
            cp.wait_recv()
        for cp in exchange(False):
            cp.wait_send()


def _tri_tables(n, by_row):
    if by_row:
        pairs = [(i, j) for i in range(n) for j in range(i + 1)]
    else:
        pairs = [(i, j) for j in range(n) for i in range(j, n)]
    return jnp.array([p[0] for p in pairs], jnp.int32), jnp.array([p[1] for p in pairs], jnp.int32)


def _flash_fwd(q, k, v, name, bq=512, carry=None):
    T = q.shape[0]
    bq = min(bq, T)
    nq = T // bq
    hb = ATT_HB_FWD
    n_carry = 0 if carry is None else 1

    i_tab, j_tab = _tri_tables(nq, by_row=True)

    def body(it_ref, jt_ref, q_ref, k_ref, v_ref, *rest):
        o_ref, lse_ref = rest[n_carry:n_carry + 2]
        m_sc, l_sc, acc_sc = rest[2 * n_carry + 2:2 * n_carry + 5]
        t = pl.program_id(1)
        i, j = it_ref[t], jt_ref[t]
        if carry is not None:
            exchange = functools.partial(_carried_exchange, rest[0], rest[3], rest[-2], rest[-1], carry[1])
            _carry_start(exchange, (pl.program_id(0) == 0) & (t == 0))

        @pl.when(j == 0)
        def _():
            m_sc[...] = jnp.full_like(m_sc, NEG_INF)
            l_sc[...] = jnp.zeros_like(l_sc)
            acc_sc[...] = jnp.zeros_like(acc_sc)

        def step(masked):
            for hh in range(hb):
                sl = slice(hh * HP, (hh + 1) * HP)
                s = lax.dot_general(q_ref[:, sl], k_ref[:, sl], _NT, preferred_element_type=F32)
                if masked:
                    row = lax.broadcasted_iota(jnp.int32, (bq, bq), 0)
                    col = lax.broadcasted_iota(jnp.int32, (bq, bq), 1)
                    s = jnp.where(col <= row, s, NEG_INF)
                m_prev = m_sc[hh]
                m_new = jnp.maximum(m_prev, jnp.max(s, axis=1, keepdims=True))
                alpha = jnp.exp2(m_prev - m_new)
                p = jnp.exp2(s - jnp.concatenate([m_new] * (bq // HP), axis=1))
                l_sc[hh] = alpha * l_sc[hh] + jnp.sum(p, axis=1, keepdims=True)
                acc_sc[hh] = alpha * acc_sc[hh] + jnp.dot(p.astype(BF16), v_ref[:, sl], preferred_element_type=F32)
                m_sc[hh] = m_new

        @pl.when(j < i)
        def _():
            step(False)

        @pl.when(j == i)
        def _():
            step(True)
            for hh in range(hb):
                o_ref[:, hh * HP:(hh + 1) * HP] = acc_sc[hh] / l_sc[hh]
                lse_ref[hh] = jnp.transpose((m_sc[hh] + jnp.log2(l_sc[hh])) * LN2)[0:1, :]

        if carry is not None:
            _carry_wait(exchange, (pl.program_id(0) == HEADS // hb - 1) & (t == i_tab.shape[0] - 1))

    qs = pl.BlockSpec((bq, hb * HP), lambda h, t, it, jt: (it[t], h))
    ks = pl.BlockSpec((bq, hb * HP), lambda h, t, it, jt: (jt[t], h))
    c_in, c_out, c_shape, c_sems = _carry_specs(carry)
    gs = pltpu.PrefetchScalarGridSpec(
        num_scalar_prefetch=2, grid=(HEADS // hb, i_tab.shape[0]), in_specs=[qs, ks, ks] + c_in,
        out_specs=[qs, pl.BlockSpec((hb, 1, bq), lambda h, t, it, jt: (h, 0, it[t]))] + c_out,
        scratch_shapes=[pltpu.VMEM((hb, bq, HP), F32)] * 3 + c_sems)
    return pl.pallas_call(
        body, name=name, grid_spec=gs,
        out_shape=[jax.ShapeDtypeStruct((T, HEADS * HP), F32), jax.ShapeDtypeStruct((HEADS, 1, T), F32)] + c_shape,
        compiler_params=pltpu.CompilerParams(dimension_semantics=("arbitrary", "arbitrary")),
    )(i_tab, j_tab, q, k, v, *([] if carry is None else [carry[0]]))


def _flash_bwd(q, k, v, do, lse_row, delta_row, name, bq=512, carry=None):
    T = q.shape[0]
    bq = min(bq, T)
    nq = T // bq
    hb = ATT_HB_BWD
    n_carry = 0 if carry is None else 1

    i_tab, j_tab = _tri_tables(nq, by_row=False)
    n_blk = i_tab.shape[0]

    def body(it_ref, jt_ref, q_ref, k_ref, v_ref, do_ref, lse_ref, dl_ref, *rest):
        dq_ref, dk_ref, dv_ref = rest[n_carry:n_carry + 3]
        t = pl.program_id(1)
        i, j = it_ref[t], jt_ref[t]
        if carry is not None:
            exchange = functools.partial(_carried_exchange, rest[0], rest[4], rest[-2], rest[-1], carry[1])
            _carry_start(exchange, (pl.program_id(0) == 0) & (t == 0))

        @pl.when(t == 0)
        def _():
            dq_ref[...] = jnp.zeros_like(dq_ref)

        @pl.when(i == j)
        def _():
            dk_ref[...] = jnp.zeros_like(dk_ref)
            dv_ref[...] = jnp.zeros_like(dv_ref)

        def step(masked):
            rows = pl.ds(pl.multiple_of(i * bq, bq), bq)
            for hh in range(hb):
                sl = slice(hh * HP, (hh + 1) * HP)
                qh, kh, doh = q_ref[:, sl], k_ref[:, sl], do_ref[:, sl]
                st = lax.dot_general(kh, qh, _NT, preferred_element_type=F32)
                pt = jnp.exp2(st - lse_ref[hh] * LOG2E)
                if masked:
                    krow = lax.broadcasted_iota(jnp.int32, (bq, bq), 0)
                    qcol = lax.broadcasted_iota(jnp.int32, (bq, bq), 1)
                    pt = jnp.where(krow <= qcol, pt, 0.0)
                dv_ref[:, sl] += jnp.dot(pt.astype(BF16), doh, preferred_element_type=F32)
                dpt = lax.dot_general(v_ref[:, sl], doh, _NT, preferred_element_type=F32)
                dst = (pt * (dpt - dl_ref[hh])).astype(BF16)
                dk_ref[:, sl] += jnp.dot(dst, qh, preferred_element_type=F32)
                dq_ref[rows, sl] += lax.dot_general(dst, kh, _TN, preferred_element_type=F32)

        @pl.when(i > j)
        def _():
            step(False)

        @pl.when(i == j)
        def _():
            step(True)

        @pl.when(i == nq - 1)
        def _():
            dk_ref[...] *= LN2

        @pl.when(t == n_blk - 1)
        def _():
            dq_ref[...] *= ATT_SCALE

        if carry is not None:
            _carry_wait(exchange, (pl.program_id(0) == HEADS // hb - 1) & (t == n_blk - 1))

    qs = pl.BlockSpec((bq, hb * HP), lambda h, t, it, jt: (it[t], h))
    ks = pl.BlockSpec((bq, hb * HP), lambda h, t, it, jt: (jt[t], h))
    rs = pl.BlockSpec((hb, 1, bq), lambda h, t, it, jt: (h, 0, it[t]))
    full = pl.BlockSpec((T, hb * HP), lambda h, t, it, jt: (0, h), pipeline_mode=pl.Buffered(1))
    c_in, c_out, c_shape, c_sems = _carry_specs(carry)
    gs = pltpu.PrefetchScalarGridSpec(num_scalar_prefetch=2, grid=(HEADS // hb, n_blk),
                                      in_specs=[qs, ks, ks, qs, rs, rs] + c_in, out_specs=[full, ks, ks] + c_out,
                                      scratch_shapes=c_sems)
    return pl.pallas_call(
        body, name=name, grid_spec=gs, out_shape=[jax.ShapeDtypeStruct((T, HEADS * HP), F32)] * 3 + c_shape,
        compiler_params=pltpu.CompilerParams(dimension_semantics=("arbitrary", "arbitrary")),
    )(i_tab, j_tab, q, k, v, do, lse_row, delta_row, *([] if carry is None else [carry[0]]))


def _xattn_heads(q, k, v):
    scale = XD ** -0.5
    ps = []
    for h in range(XH):
        sl = slice(h * XD, (h + 1) * XD)
        s = lax.dot_general(q[:, sl].astype(BF16), k[:, sl].astype(BF16), (((1,), (1,)), ((), ())),
                            preferred_element_type=F32) * scale
        e = jnp.exp(s - jnp.max(s, axis=1, keepdims=True))
        ps.append(e / jnp.sum(e, axis=1, keepdims=True))
    return ps


def _xattn_fwd_fn(q, k, v):
    ps = _xattn_heads(q, k, v)
    o = [jnp.dot(p.astype(BF16), v[:, h * XD:(h + 1) * XD].astype(BF16), preferred_element_type=F32)
         for h, p in enumerate(ps)]
    return (jnp.concatenate(o, axis=1),)


def _xattn_bwd_fn(q, do, k, v):
    scale = XD ** -0.5
    ps = _xattn_heads(q, k, v)
    dqs, dks, dvs = [], [], []
    tdims = (((0,), (0,)), ((), ()))
    for h, p in enumerate(ps):
        sl = slice(h * XD, (h + 1) * XD)
        doh = do[:, sl].astype(BF16)
        dvs.append(lax.dot_general(p.astype(BF16), doh, tdims, preferred_element_type=F32))
        dp = lax.dot_general(doh, v[:, sl].astype(BF16), (((1,), (1,)), ((), ())), preferred_element_type=F32)
        ds = (p * (dp - jnp.sum(dp * p, axis=1, keepdims=True)) * scale).astype(BF16)
        dqs.append(jnp.dot(ds, k[:, sl].astype(BF16), preferred_element_type=F32))
        dks.append(lax.dot_general(ds, q[:, sl].astype(BF16), tdims, preferred_element_type=F32))
    return jnp.concatenate(dqs, axis=1), jnp.concatenate(dks, axis=1), jnp.concatenate(dvs, axis=1)


S5_SUB = 8


def _cmul_add(xr, xi, ar, ai, sr, si):
    return xr + ar * sr - ai * si, xi + ar * si + ai * sr


def _s5_tables(ar, ai, reverse):
    pows = [(ar, ai)]
    for _ in range(S5_SUB - 1):
        pr, pi = pows[-1]
        pows.append((pr * ar - pi * ai, pr * ai + pi * ar))
    cr = jnp.concatenate([p[0] for p in pows], axis=0)
    ci = jnp.concatenate([p[1] for p in pows], axis=0)
    if reverse:
        cr, ci = cr[::-1], ci[::-1]
    t = jnp.arange(S5_SUB)[:, None]
    blocks = [jnp.concatenate([cr, ci], axis=1)]
    for s in (1, 2, 4):
        keep = (t < S5_SUB - s) if reverse else (t >= s)
        sr, si = pows[s - 1]
        blocks.append(jnp.concatenate([jnp.where(keep, sr, 0.0), jnp.where(keep, si, 0.0)], axis=1))
    return jnp.concatenate(blocks, axis=0)


def _sub_scan(xr, xi, tab_ref, reverse):
    for k, s in enumerate((1, 2, 4)):
        blk = slice(S5_SUB * (k + 1), S5_SUB * (k + 2))
        sh = S5_SUB - s if reverse else s
        xr, xi = _cmul_add(xr, xi, tab_ref[blk, :S5_N], tab_ref[blk, S5_N:], pltpu.roll(xr, sh, 0), pltpu.roll(xi, sh, 0))
    return xr, xi


def _s5_fwd(z, b_st, c_st, d_skip, tab, name, tc=512):
    T = z.shape[0]
    tc = min(tc, T)
    nsub = tc // S5_SUB

    def body(u_ref, bst_ref, cst_ref, d_ref, tab_ref, h_ref, yc_ref, yg_ref, carry, x_sc):
        @pl.when(pl.program_id(0) == 0)
        def _():
            carry[...] = jnp.zeros_like(carry)

        u = u_ref[...].astype(F32)
        x_sc[...] = jnp.dot(u.astype(BF16), bst_ref[...], preferred_element_type=F32)

        def sub(b, c):
            r = pl.ds(pl.multiple_of(b * S5_SUB, S5_SUB), S5_SUB)
            xr, xi = _sub_scan(x_sc[r, :S5_N], x_sc[r, S5_N:], tab_ref, False)
            hr, hi = _cmul_add(xr, xi, tab_ref[0:S5_SUB, :S5_N], tab_ref[0:S5_SUB, S5_N:], c[0], c[1])
            x_sc[r, :S5_N] = hr
            x_sc[r, S5_N:] = hi
            return hr[S5_SUB - 1:S5_SUB], hi[S5_SUB - 1:S5_SUB]

        cr, ci = lax.fori_loop(0, nsub, sub, (carry[0:1, :S5_N], carry[0:1, S5_N:]))
        carry[0:1, :S5_N] = cr
        carry[0:1, S5_N:] = ci
        h = x_sc[...]
        h_ref[...] = h
        yc = jnp.dot(h.astype(BF16), cst_ref[...], preferred_element_type=F32)
        yc_ref[...] = yc
        yg_ref[...] = _gelu_skip(yc, u, d_ref[...]).astype(yg_ref.dtype)

    whole = lambda a: pl.BlockSpec(a.shape, lambda i: (0, 0))
    row = lambda w: pl.BlockSpec((tc, w), lambda i: (i, 0))
    return pl.pallas_call(
        body, name=name, grid=(T // tc,),
        in_specs=[pl.BlockSpec((tc, S5_W), lambda i: (i, Z_U // S5_W)), whole(b_st), whole(c_st), whole(d_skip), whole(tab)],
        out_specs=[row(2 * S5_N), row(S5_W), row(S5_W)],
        out_shape=[jax.ShapeDtypeStruct((T, 2 * S5_N), F32), jax.ShapeDtypeStruct((T, S5_W), F32),
                   jax.ShapeDtypeStruct((T, S5_W), BF16)],
        scratch_shapes=[pltpu.VMEM((8, 2 * S5_N), F32), pltpu.VMEM((tc, 2 * S5_N), F32)],
        compiler_params=pltpu.CompilerParams(dimension_semantics=("arbitrary",)),
    )(z, b_st, c_st, d_skip, tab)


def _s5_bwd(dyg, yc, z, h, b_st, c_st, d_skip, tab, name, tc=512):
    T = z.shape[0]
    tc = min(tc, T)
    nc, nsub = T // tc, tc // S5_SUB

    def body(dyg_ref, yc_ref, u_ref, h_ref, hp_ref, bst_ref, cst_ref, d_ref, tab_ref,
             du_ref, da_ref, db_ref, dct_ref, dd_ref, carry, x_sc):
        g = pl.program_id(0)

        @pl.when(g == 0)
        def _():
            carry[...] = jnp.zeros_like(carry)
            for r in (da_ref, db_ref, dct_ref, dd_ref):
                r[...] = jnp.zeros_like(r)

        u = u_ref[...].astype(F32)
        _, vjp = jax.vjp(_gelu_skip, yc_ref[...], u, d_ref[...])
        dyc, du_skip, dd = vjp(dyg_ref[...])
        dd_ref[...] += dd
        dyc16 = dyc.astype(BF16)
        dct_ref[...] += lax.dot_general(dyc16, h_ref[...].astype(BF16), _TN, preferred_element_type=F32)
        x_sc[...] = lax.dot_general(dyc16, cst_ref[...], _NT, preferred_element_type=F32)
        first = jnp.where(g == nc - 1, 0.0, 1.0)
        row0 = lax.broadcasted_iota(jnp.int32, (S5_SUB, S5_N), 0) == 0

        def sub(k, c):
            cr, ci, acc_r, acc_i = c
            b = nsub - 1 - k
            r = pl.ds(pl.multiple_of(b * S5_SUB, S5_SUB), S5_SUB)
            xr, xi = _sub_scan(x_sc[r, :S5_N], x_sc[r, S5_N:], tab_ref, True)
            lr, li = _cmul_add(xr, xi, tab_ref[0:S5_SUB, :S5_N], tab_ref[0:S5_SUB, S5_N:], cr, ci)
            x_sc[r, :S5_N] = lr
            x_sc[r, S5_N:] = li
            rp = pl.ds(pl.multiple_of(jnp.maximum(b - 1, 0) * S5_SUB, S5_SUB), S5_SUB)
            last_r = jnp.where(b == 0, hp_ref[S5_SUB - 1:S5_SUB, :S5_N] * first, h_ref[rp, :S5_N][S5_SUB - 1:S5_SUB])
            last_i = jnp.where(b == 0, hp_ref[S5_SUB - 1:S5_SUB, S5_N:] * first, h_ref[rp, S5_N:][S5_SUB - 1:S5_SUB])
            pr = jnp.where(row0, last_r, pltpu.roll(h_ref[r, :S5_N], 1, 0))
            pi = jnp.where(row0, last_i, pltpu.roll(h_ref[r, S5_N:], 1, 0))
            return lr[0:1], li[0:1], acc_r + (lr * pr + li * pi), acc_i + (li * pr - lr * pi)

        zero = jnp.zeros((S5_SUB, S5_N), F32)
        cr, ci, acc_r, acc_i = lax.fori_loop(0, nsub, sub, (carry[0:1, :S5_N], carry[0:1, S5_N:], zero, zero))
        carry[0:1, :S5_N] = cr
        carry[0:1, S5_N:] = ci
        da_ref[0:1, :S5_N] += jnp.sum(acc_r, axis=0, keepdims=True)
        da_ref[0:1, S5_N:] += jnp.sum(acc_i, axis=0, keepdims=True)
        lam16 = x_sc[...].astype(BF16)
        db_ref[...] += lax.dot_general(u.astype(BF16), lam16, _TN, preferred_element_type=F32)
        du = lax.dot_general(lam16, bst_ref[...], _NT, preferred_element_type=F32) + du_skip
        du_ref[...] = du.astype(du_ref.dtype)

    whole = lambda a: pl.BlockSpec(a.shape, lambda g: (0, 0))
    row = lambda w, off=0: pl.BlockSpec((tc, w), lambda g: (nc - 1 - g, off))
    prev = pl.BlockSpec((S5_SUB, 2 * S5_N), lambda g: (jnp.maximum((nc - 1 - g) * nsub - 1, 0), 0))
    acc = lambda s: pl.BlockSpec(s, lambda g: (0, 0))
    return pl.pallas_call(
        body, name=name, grid=(nc,),
        in_specs=[row(S5_W), row(S5_W), row(S5_W, Z_U // S5_W), row(2 * S5_N), prev, whole(b_st), whole(c_st),
                  whole(d_skip), whole(tab)],
        out_specs=[row(S5_W), acc((1, 2 * S5_N)), acc((S5_W, 2 * S5_N)), acc((S5_W, 2 * S5_N)), acc((1, S5_W))],
        out_shape=[jax.ShapeDtypeStruct((T, S5_W), BF16), jax.ShapeDtypeStruct((1, 2 * S5_N), F32),
                   jax.ShapeDtypeStruct((S5_W, 2 * S5_N), F32), jax.ShapeDtypeStruct((S5_W, 2 * S5_N), F32),
                   jax.ShapeDtypeStruct((1, S5_W), F32)],
        scratch_shapes=[pltpu.VMEM((8, 2 * S5_N), F32), pltpu.VMEM((tc, 2 * S5_N), F32)],
        compiler_params=pltpu.CompilerParams(dimension_semantics=("arbitrary",)),
    )(dyg, yc, z, h, h, b_st, c_st, d_skip, tab)


def _s5_discretize(lam_re, lam_im, log_dt, b_re, b_im):
    lr = jnp.minimum(lam_re, S5_MAX_RE)
    li = lam_im
    dt = jnp.exp(log_dt)[:, None]
    mag = jnp.exp(lr * dt)
    ang = li * dt
    ab_re = mag * jnp.cos(ang)
    ab_im = mag * jnp.sin(ang)
    den = lr * lr + li * li
    nr = ab_re - 1.0
    f_re = ((nr * lr + ab_im * li) / den)[..., None]
    f_im = ((ab_im * lr - nr * li) / den)[..., None]
    return ab_re, ab_im, f_re * b_re - f_im * b_im, f_re * b_im + f_im * b_re


def _adamw_fn(w, g, m, v):
    m = ADAM_B1 * m + (1.0 - ADAM_B1) * g
    v = ADAM_B2 * v + (1.0 - ADAM_B2) * (g * g)
    m_hat = m / (1.0 - ADAM_B1 ** ADAM_STEP)
    v_hat = v / (1.0 - ADAM_B2 ** ADAM_STEP)
    delta = -ADAM_LR * (m_hat / (jnp.sqrt(v_hat) + ADAM_EPS) + ADAM_WD * w)
    return delta, m, v


def _adamw(w, g, m, v, name):
    wd = w.shape[1]
    return _rowwise(_adamw_fn, name, [_whole(w), _whole(g), _whole(m), _whole(v)], [], [(wd, F32)] * 3,
                    tm=_row_tile(w.shape[0]))


def _adamw_many(ws, gs, ms, vs, name):
    n = len(ws)

    def body(*refs):
        for i in range(n):
            d_, m_, v_ = _adamw_fn(*[refs[kk * n + i][...] for kk in range(4)])
            refs[4 * n + i][...] = d_
            refs[5 * n + i][...] = m_
            refs[6 * n + i][...] = v_

    out = pl.pallas_call(body, name=name, out_shape=[jax.ShapeDtypeStruct(w.shape, F32) for w in ws] * 3)(*ws, *gs, *ms, *vs)
    return out[:n], out[n:2 * n], out[2 * n:]


def _me():
    return lax.axis_index("x"), lax.axis_index("y"), lax.axis_index("c")


def _chip_of(j):
    return j // 2, j % 2


_HBM = pl.BlockSpec(memory_space=pl.ANY)


def _gather_weights(shard):
    _, R, W = shard.shape

    def body(s_ref, out_ref, send_sems, recv_sems):
        x, y, c = _me()
        mine = 2 * x + y
        sib = (x, y, 1 - c)
        chips = [(1 - x, y), (x, 1 - y), (1 - x, 1 - y)]

        def rcopy(kk, src, chip_idx, half, to):
            return pltpu.make_async_remote_copy(src_ref=src, dst_ref=out_ref.at[chip_idx, half],
                                                send_sem=send_sems.at[kk], recv_sem=recv_sems.at[kk],
                                                device_id=to, device_id_type=MESH)

        first = [rcopy(j, s_ref.at[c], mine, c, (*chip, c)) for j, chip in enumerate(chips)]
        for cp in first:
            cp.start()
        passed = []
        for j, (px, py) in enumerate(chips):
            src_chip = 2 * px + py
            rcopy(j, s_ref.at[c], src_chip, c, (x, y, c)).wait_recv()
            fwd = rcopy(3 + j, out_ref.at[src_chip, c], src_chip, c, sib)
            fwd.start()
            passed.append(fwd)
        for j, (px, py) in enumerate(chips):
            rcopy(3 + j, s_ref.at[c], 2 * px + py, 1 - c, (x, y, c)).wait_recv()
        for cp in first + passed:
            cp.wait_send()

    return pl.pallas_call(
        body, name="gather_weights", in_specs=[_HBM], out_specs=_HBM,
        out_shape=jax.ShapeDtypeStruct((4, 2, R, W), shard.dtype),
        scratch_shapes=[pltpu.SemaphoreType.DMA((6,)), pltpu.SemaphoreType.DMA((6,))],
    )(shard)


def _pair_exchange(g, name):
    _, _, R, W = g.shape

    def body(g_ref, out_ref, send_sem, recv_sem):
        x, y, c = _me()
        cp = pltpu.make_async_remote_copy(src_ref=g_ref.at[:, 1 - c], dst_ref=out_ref, send_sem=send_sem,
                                          recv_sem=recv_sem, device_id=(x, y, 1 - c), device_id_type=MESH)
        cp.start()
        cp.wait()

    return pl.pallas_call(
        body, name=name, in_specs=[_HBM], out_specs=_HBM,
        out_shape=jax.ShapeDtypeStruct((4, R, W), g.dtype),
        scratch_shapes=[pltpu.SemaphoreType.DMA(()), pltpu.SemaphoreType.DMA(())],
    )(g)


def _pair_add(g, recv, cidx, name):
    _, _, R, W = g.shape
    tr = _row_tile(R)

    def body(c_ref, a_ref, b_ref, o_ref):
        o_ref[...] = (a_ref[...].astype(F32) + b_ref[...].astype(F32)).astype(o_ref.dtype)

    gs = pltpu.PrefetchScalarGridSpec(
        num_scalar_prefetch=1, grid=(4, R // tr),
        in_specs=[pl.BlockSpec((None, None, tr, W), lambda j, r, c: (j, c[0], r, 0)),
                  pl.BlockSpec((None, tr, W), lambda j, r, c: (j, r, 0))],
        out_specs=pl.BlockSpec((None, tr, W), lambda j, r, c: (j, r, 0)))
    return pl.pallas_call(body, name=name, grid_spec=gs,
                          out_shape=jax.ShapeDtypeStruct((4, R, W), BF16))(cidx, g, recv)


def _carried_exchange(src_ref, dst_ref, send_sems, recv_sems, spread, incoming):
    x, y, c = _me()
    mine = 2 * x + y
    copies = []
    for j, (px, py) in enumerate([(1 - x, y), (x, 1 - y), (1 - x, 1 - y)]):
        src = src_ref.at[2 * px + py] if spread else src_ref.at[c]
        slot = 2 * px + py if incoming else mine
        copies.append(pltpu.make_async_remote_copy(src_ref=src, dst_ref=dst_ref.at[slot], send_sem=send_sems.at[j],
                                                   recv_sem=recv_sems.at[j], device_id=(px, py, c), device_id_type=MESH))
    return copies


def _chip_exchange(p, name):
    _, R, W = p.shape

    def body(p_ref, out_ref, send_sems, recv_sems):
        outs = _carried_exchange(p_ref, out_ref, send_sems, recv_sems, True, False)
        for cp in outs:
            cp.start()
        for cp in _carried_exchange(p_ref, out_ref, send_sems, recv_sems, True, True):
            cp.wait_recv()
        for cp in outs:
            cp.wait_send()

    return pl.pallas_call(
        body, name=name, in_specs=[_HBM], out_specs=_HBM,
        out_shape=jax.ShapeDtypeStruct((4, R, W), p.dtype),
        scratch_shapes=[pltpu.SemaphoreType.DMA((3,)), pltpu.SemaphoreType.DMA((3,))],
    )(p)


def _chip_sum(q, name):
    _, R, W = q.shape
    tr = _row_tile(R)

    def body(q_ref, o_ref):
        f = lambda t: q_ref[t].astype(F32)
        o_ref[...] = ((f(0) + f(1)) + f(2)) + f(3)

    return pl.pallas_call(body, name=name, grid=(R // tr,),
                          in_specs=[pl.BlockSpec((4, tr, W), lambda r: (0, r, 0))],
                          out_specs=pl.BlockSpec((tr, W), lambda r: (r, 0)),
                          out_shape=jax.ShapeDtypeStruct((R, W), F32))(q)


def _pair_share(r, name):
    R, W = r.shape

    def body(r_ref, out_ref, send_sem, recv_sem):
        x, y, c = _me()
        cp = pltpu.make_async_remote_copy(src_ref=r_ref, dst_ref=out_ref, send_sem=send_sem, recv_sem=recv_sem,
                                          device_id=(x, y, 1 - c), device_id_type=MESH)
        cp.start()
        cp.wait()

    return pl.pallas_call(
        body, name=name, in_specs=[_HBM], out_specs=_HBM,
        out_shape=jax.ShapeDtypeStruct((R, W), r.dtype),
        scratch_shapes=[pltpu.SemaphoreType.DMA(()), pltpu.SemaphoreType.DMA(())],
    )(r)


def _allreduce_small(vec):
    R, W = vec.shape

    def body(v_ref, out_ref, buf, send_sems, recv_sems):
        x, y, c = _me()
        me = 4 * x + 2 * y + c
        buf[me] = v_ref[...]
        cps = []
        for kk in range(1, 8):
            peer = (x ^ (kk >> 2), y ^ ((kk >> 1) & 1), c ^ (kk & 1))
            cp = pltpu.make_async_remote_copy(src_ref=v_ref, dst_ref=buf.at[me], send_sem=send_sems.at[kk - 1],
                                              recv_sem=recv_sems.at[kk - 1], device_id=peer, device_id_type=MESH)
            cp.start()
            cps.append(cp)
        for kk in range(1, 8):
            peer = (x ^ (kk >> 2), y ^ ((kk >> 1) & 1), c ^ (kk & 1))
            pltpu.make_async_remote_copy(src_ref=v_ref, dst_ref=buf.at[me ^ kk], send_sem=send_sems.at[kk - 1],
                                         recv_sem=recv_sems.at[kk - 1], device_id=peer, device_id_type=MESH).wait_recv()
        for cp in cps:
            cp.wait_send()
        acc = buf[0]
        for d in range(1, 8):
            acc = acc + buf[d]
        out_ref[...] = acc

    vm = pl.BlockSpec(memory_space=pltpu.VMEM)
    return pl.pallas_call(
        body, name="allreduce_small", in_specs=[vm], out_specs=vm, out_shape=jax.ShapeDtypeStruct((R, W), F32),
        scratch_shapes=[pltpu.VMEM((8, R, W), F32), pltpu.SemaphoreType.DMA((7,)), pltpu.SemaphoreType.DMA((7,))],
    )(vec)


def _pack_flat(parts, align, dtype):
    flat = jnp.concatenate([p.reshape(-1).astype(dtype) for p in parts])
    n = flat.shape[0]
    pad = (-n) % align
    return jnp.pad(flat, (0, pad)) if pad else flat


def _unpack_flat(flat, shapes):
    out, off = [], 0
    for s in shapes:
        n = math.prod(s)
        out.append(flat[off:off + n].reshape(s))
        off += n
    return out


def _full_from_shards(name, sh):
    if name in COL_SHARDED:
        return jnp.transpose(sh, (1, 0, 2)).reshape(sh.shape[1], 4 * sh.shape[2])
    return sh.reshape(4 * sh.shape[1], sh.shape[2])


def _shards_from_full(name, full):
    if name in COL_SHARDED:
        r, cc = full.shape[0], full.shape[1] // 4
        return jnp.transpose(full.reshape(r, 4, cc), (1, 0, 2)).reshape(4, r * cc)
    return full.reshape(4, -1)


def _pad_heads(w, width):
    k = w.shape[0]
    return jnp.pad(w.reshape(k, HEADS, width), ((0, 0), (0, 0), (0, HP - width))).reshape(k, HEADS * HP)


def _unpad_heads(w, width):
    k = w.shape[0]
    return w.reshape(k, HEADS, HP)[:, :, :width].reshape(k, HEADS * width)


def _blockdiag(t):
    g, a, b = t.shape
    return jnp.einsum("gab,gk->gakb", t, jnp.eye(g, dtype=t.dtype)).reshape(g * a, g * b)


def _blockdiag_t(m, a, b):
    g = m.shape[0] // a
    return jnp.einsum("gagb->gab", m.reshape(g, a, g, b))


def _local_step(x, mem, positions, target, W, sp, overlap=None):
    T = x.shape[0]
    row = lambda v: v.reshape(1, -1)

    ab_re, ab_im, bb_re, bb_im = _s5_discretize(sp["s5_lam_re"], sp["s5_lam_im"], sp["s5_log_dt"], sp["s5_b_re"], sp["s5_b_im"])
    ar, ai = ab_re.reshape(1, S5_N), ab_im.reshape(1, S5_N)
    tab_fwd, tab_rev = _s5_tables(ar, ai, False), _s5_tables(ar, -ai, True)
    b_st = jnp.concatenate([_blockdiag(jnp.swapaxes(bb_re, 1, 2)), _blockdiag(jnp.swapaxes(bb_im, 1, 2))],
                           axis=1).astype(BF16)
    c_st = jnp.concatenate([_blockdiag(jnp.swapaxes(sp["s5_c_re"], 1, 2)),
                            -_blockdiag(jnp.swapaxes(sp["s5_c_im"], 1, 2))], axis=0).astype(BF16)
    inv = ROPE_THETA ** (-jnp.arange(0, ROPE, 2, dtype=F32) / ROPE)
    ang = positions.astype(F32)[:, None] * inv
    cs, sn = jnp.cos(ang), jnp.sin(ang)
    one, zero = jnp.ones((T, NOPE), F32), jnp.zeros((T, NOPE), F32)
    z16, tail1, tail0 = jnp.zeros((T, ROPE // 2), F32), jnp.ones((T, HP - QK), F32), jnp.zeros((T, HP - QK), F32)
    t_cos = jnp.concatenate([one, cs, cs, tail1], axis=1)
    t_s1 = jnp.concatenate([zero, z16, sn, tail0], axis=1)
    t_s2 = jnp.concatenate([zero, -sn, z16, tail0], axis=1)

    w_in = W["w_in"]
    w_in_re = jnp.concatenate([w_in[:, 800:], w_in[:, :768], w_in[:, 768:800],
                               jnp.zeros((D_MODEL, HP - ROPE), w_in.dtype)], axis=1)
    w_uq_p = _pad_heads(W["w_uq"], QK)
    wkv = W["w_ukv"].reshape(S5_W, HEADS, NOPE + VD)
    w_uk_p = _pad_heads(wkv[:, :, :NOPE].reshape(S5_W, HEADS * NOPE), NOPE)
    w_uv_p = _pad_heads(wkv[:, :, NOPE:].reshape(S5_W, HEADS * VD), VD)
    w_oa_p = jnp.pad(W["w_oa"].reshape(HEADS, VD, D_MODEL), ((0, 0), (0, HP - VD), (0, 0))).reshape(HEADS * HP, D_MODEL)

    g_in, b_in = row(sp["ln_in_g"]), row(sp["ln_in_b"])
    g1, b1, g2, b2, g3, b3 = (row(sp[k]) for k in ("ln1_g", "ln1_b", "ln2_g", "ln2_b", "ln3_g", "ln3_b"))
    gq, gkv, d_skip = row(sp["q_norm_g"]), row(sp["kv_norm_g"]), row(sp["s5_d"])

    (h0,) = _rowwise(lambda a, g, b: (_ln(a, g, b),), "ln_in_fwd", [_whole(x)], [g_in, b_in], [(D_MODEL, F32)], tm=512)
    def in_epi(acc, ga, gb):
        zb = acc.astype(BF16).astype(F32)
        return acc, _rms(zb[:, Z_CQ:Z_CQ + S5_W], ga), _rms(zb[:, Z_CKV:Z_CKV + S5_W], gb)

    z, cqn, ckvn = _mm(h0, w_in_re, "in_proj_norm", tm=512, tn=Z_W, epi=in_epi, pextras=(gq, gkv),
                       outs=[(Z_W, BF16), (S5_W, BF16), (S5_W, BF16)])
    hs, yc, yg = _s5_fwd(z, b_st, c_st, d_skip, tab_fwd, "s5_fwd")
    y2 = _mm(yg, W["w_glu"], "glu_proj", tn=2048, out_dtype=BF16)
    tabs = [(t_cos, 0, HP), (t_s1, 0, HP), (t_s2, 0, HP)]

    def rope_q(acc, tc_, t1, t2):
        return _rope(acc, _tile8(tc_), _tile8(t1), _tile8(t2)) * (ATT_SCALE * LOG2E)

    def rope_k(acc, krz, tc_, t1, t2):
        return acc + _tile8(_rope(pltpu.roll(krz.astype(F32), NOPE, 1), tc_, t1, t2))

    q = _mm(cqn, w_uq_p, "mla_uq_rope", epi=rope_q, rextras=tabs, out_dtype=BF16)
    k = _mm(ckvn, w_uk_p, "mla_uk_rope", epi=rope_k, rextras=[(z, Z_KR, HP)] + tabs, out_dtype=BF16)
    v = _mm(ckvn, w_uv_p, "mla_uv", out_dtype=BF16)
    if overlap is None:
        o, lse = _flash_fwd(q, k, v, "mla_attn_fwd")
    else:
        o, lse, landed = _flash_fwd(q, k, v, "mla_attn_fwd", carry=(overlap[0], False))
        W = {**W, **overlap[1](landed)}
    def gate_epi(acc, gs_, ga_, ya, yb):
        f = lambda t: t.astype(F32)
        return acc, _gate_mix(f(gs_), f(ga_), f(ya), f(yb), acc.astype(BF16).astype(F32))

    a_out, mixin = _mm(o, w_oa_p, "mla_oa_gate", epi=gate_epi, outs=[(D_MODEL, BF16)] * 2, tm=512,
                       rextras=[(z, Z_GS, D_MODEL), (z, Z_GA, D_MODEL), (y2, 0, D_MODEL), (y2, D_MODEL, D_MODEL)])
    def ln_epi(acc, h, g, b):
        s_ = DN_ALPHA * h + acc
        return s_, _ln(s_, g, b)

    two_rows = [(D_MODEL, F32)] * 2
    s1, h1 = _mm(mixin, W["w_o"], "mix_o_ln1", epi=ln_epi, extras=(h0,), pextras=(g1, b1), outs=two_rows, tm=512)
    xq = _mm(h1, W["w_xq"], "x_q", out_dtype=BF16)
    xk = _mm(mem, W["w_xk"], "x_k", out_dtype=BF16)
    xv = _mm(mem, W["w_xv"], "x_v", out_dtype=BF16)
    (xo,) = _rowwise(_xattn_fwd_fn, "xattn_fwd", [_whole(xq)], [xk, xv], [(D_MODEL, BF16)], tm=1024, upcast=False)
    s2, h2 = _mm(xo, W["w_xo"], "x_o_ln2", epi=ln_epi, extras=(h1,), pextras=(g2, b2), outs=two_rows, tm=512)
    a_up = _mm(h2, W["w_up"], "mlp_up", out_dtype=BF16)
    def loss_epi(acc, h, tgt, g, b):
        def f(r_, g_, b_):
            e = _ln_res(h, r_, g_, b_) - tgt
            return 0.5 * jnp.sum(jnp.mean(e * e, axis=-1))
        lv, (dr_, dg_, db_) = jax.value_and_grad(f, argnums=(0, 1, 2))(acc, g, b)
        return dr_, dg_, db_, jnp.broadcast_to(lv, (1, 128))

    row_and_sums = [(D_MODEL, F32), (D_MODEL, "sum"), (D_MODEL, "sum")]
    dr3, dg3, db3, lossv = _mm(a_up, W["w_down"], "mlp_down_loss", act=_relu2, tm=512, tk=4 * D_MODEL, epi=loss_epi,
                               extras=(h2, target), pextras=(g3, b3), outs=row_and_sums + [(128, "sum")])
    gW, gs = {}, {"ln3_g": dg3, "ln3_b": db3}

    da = _mm(dr3, W["w_down"], "mlp_down_bwd_a", "nt", epi=lambda acc, a: acc * (2.0 * jnp.maximum(a, 0.0)), extras=(a_up,),
             out_dtype=BF16)
    gW["w_down"] = _mm(a_up, dr3, "mlp_down_bwd_w", "tn", m=4 * D_MODEL, n=D_MODEL, act=_relu2, tk=2048)
    gW["w_up"] = _mm(h2, da, "mlp_up_bwd_w", "tn", m=D_MODEL, n=4 * D_MODEL, tk=2048)

    def ln_bwd_epi(acc, e, s_, g, b):
        _, vjp = jax.vjp(_ln, s_, g, b)
        return vjp(acc + DN_ALPHA * e)

    dr2, gs["ln2_g"], gs["ln2_b"] = _mm(da, W["w_up"], "mlp_up_bwd_ln2", "nt", tm=512, tk=4 * D_MODEL, epi=ln_bwd_epi,
                                        extras=(dr3, s2), pextras=(g2, b2), outs=row_and_sums)
    dxo = _mm(dr2, W["w_xo"], "x_o_bwd_a", "nt", out_dtype=BF16)
    gW["w_xo"] = _mm(xo, dr2, "x_o_bwd_w", "tn", m=D_MODEL, n=D_MODEL)
    dxq, dxk, dxv = _rowwise(_xattn_bwd_fn, "xattn_bwd", [_whole(xq), _whole(dxo)], [xk, xv], [(D_MODEL, BF16)],
                             [(xk.shape[0], D_MODEL)] * 2, tm=1024, upcast=False)
    gW["w_xq"] = _mm(h1, dxq, "x_q_bwd_w", "tn", m=D_MODEL, n=D_MODEL)
    gW["w_xk"] = _mm(mem, dxk, "x_k_bwd_w", "tn", m=D_MODEL, n=D_MODEL)
    gW["w_xv"] = _mm(mem, dxv, "x_v_bwd_w", "tn", m=D_MODEL, n=D_MODEL)

    dr1, gs["ln1_g"], gs["ln1_b"] = _mm(dxq, W["w_xq"], "x_q_bwd_ln1", "nt", tm=512, epi=ln_bwd_epi,
                                        extras=(dr2, s1), pextras=(g1, b1), outs=row_and_sums)
    gW["w_o"] = _mm(mixin, dr1, "mix_o_bwd_w", "tn", m=D_MODEL, n=D_MODEL)

    def gate_bwd_epi(acc, *tiles):
        _, vjp = jax.vjp(_gate_mix, *[t.astype(F32) for t in tiles])
        return vjp(acc)

    dgs, dga, dy2a, dy2b, da_out = _mm(
        dr1, W["w_o"], "mix_o_bwd_gate", "nt", tm=512, epi=gate_bwd_epi, outs=[(D_MODEL, BF16)] * 5,
        rextras=[(z, Z_GS, D_MODEL), (z, Z_GA, D_MODEL), (y2, 0, D_MODEL), (y2, D_MODEL, D_MODEL), _whole(a_out)])

    def delta_epi(acc, o_):
        prod = acc * o_
        cols = [jnp.sum(prod[:, h * HP:(h + 1) * HP], axis=1, keepdims=True) for h in range(HEADS)]
        return acc, jnp.concatenate(cols, axis=1)

    do16, delta = _mm(da_out, w_oa_p, "mla_oa_bwd_a", "nt", epi=delta_epi, extras=(o,),
                      outs=[(HEADS * HP, BF16), (HEADS, F32)], tm=512)
    g_oa_p = _mm(o, da_out, "mla_oa_bwd_w", "tn", m=HEADS * HP, n=D_MODEL)
    gW["w_oa"] = g_oa_p.reshape(HEADS, HP, D_MODEL)[:, :VD].reshape(HEADS * VD, D_MODEL)
    delta_row = delta.T.reshape(HEADS, 1, T)
    if overlap is None:
        dq, dk, dv = _flash_bwd(q, k, v, do16, lse, delta_row, "mla_attn_bwd")
        riding = None
    else:
        sent = overlap[2]({n: gW[n] for n in BIG_LATE})
        dq, dk, dv, landed = _flash_bwd(q, k, v, do16, lse, delta_row, "mla_attn_bwd", carry=(sent, True))
        riding = (sent, landed)

    def rope_bwd(dq_, dk_, tc_, t1, t2):
        dqr = _rope_t(dq_, _tile8(tc_), _tile8(t1), _tile8(t2))
        dkr = dk_[:, 0:HP]
        for hh in range(1, HEADS):
            dkr = dkr + dk_[:, hh * HP:(hh + 1) * HP]
        dkr = _rope_t(jnp.where(_rope_lanes(dkr.shape), dkr, 0.0), tc_, t1, t2)
        dkr = pltpu.roll(dkr, NOPE, 1)
        lane = lax.broadcasted_iota(jnp.int32, dkr.shape, 1)
        return dqr, jnp.where(lane < ROPE, dkr, 0.0)

    dq_raw, dkr = _rowwise(rope_bwd, "mla_rope_bwd", [_whole(dq), _whole(dk), _whole(t_cos), _whole(t_s1), _whole(t_s2)], [],
                           [(HEADS * HP, BF16), (HP, BF16)])
    def rms_bwd_epi(acc, *rest):
        _, vjp = jax.vjp(_rms, rest[-2].astype(F32), rest[-1])
        return vjp(acc + rest[0] if len(rest) == 3 else acc)

    row_and_sum = [(S5_W, BF16), (S5_W, "sum")]
    dcq, gs["q_norm_g"] = _mm(dq_raw, w_uq_p, "mla_uq_bwd_norm", "nt", epi=rms_bwd_epi, rextras=[(z, Z_CQ, S5_W)],
                              pextras=(gq,), outs=row_and_sum)
    gW["w_uq"] = _unpad_heads(_mm(cqn, dq_raw, "mla_uq_bwd_w", "tn", m=S5_W, n=HEADS * HP), QK)
    dckvn_k = _mm(dk, w_uk_p, "mla_uk_bwd_a", "nt")
    dckv, gs["kv_norm_g"] = _mm(dv, w_uv_p, "mla_uv_bwd_norm", "nt", epi=rms_bwd_epi, extras=(dckvn_k,),
                                rextras=[(z, Z_CKV, S5_W)], pextras=(gkv,), outs=row_and_sum)
    g_uk = _unpad_heads(_mm(ckvn, dk, "mla_uk_bwd_w", "tn", m=S5_W, n=HEADS * HP), NOPE).reshape(S5_W, HEADS, NOPE)
    g_uv = _unpad_heads(_mm(ckvn, dv, "mla_uv_bwd_w", "tn", m=S5_W, n=HEADS * HP), VD).reshape(S5_W, HEADS, VD)
    gW["w_ukv"] = jnp.concatenate([g_uk, g_uv], axis=2).reshape(S5_W, HEADS * (NOPE + VD))

    dy2 = jnp.concatenate([dy2a, dy2b], axis=1)
    dyg = _mm(dy2, W["w_glu"], "glu_bwd_a", "nt")
    gW["w_glu"] = _mm(yg, dy2, "glu_bwd_w", "tn", m=S5_W, n=2 * D_MODEL)

    du, gs["s5_ab"], g_bst, g_cst_t, gs["s5_d"] = _s5_bwd(dyg, yc, z, hs, b_st, c_st, d_skip, tab_rev, "s5_bwd")
    gs["s5_bb_re"] = jnp.swapaxes(_blockdiag_t(g_bst[:, :S5_N], S5_H, S5_P), 1, 2)
    gs["s5_bb_im"] = jnp.swapaxes(_blockdiag_t(g_bst[:, S5_N:], S5_H, S5_P), 1, 2)
    gs["s5_c_re"] = _blockdiag_t(g_cst_t[:, :S5_N], S5_H, S5_P)
    gs["s5_c_im"] = -_blockdiag_t(g_cst_t[:, S5_N:], S5_H, S5_P)

    dz = jnp.concatenate([dgs, dga, du, dcq, dckv, dkr], axis=1)
    g_in_re = _mm(h0, dz, "in_proj_bwd_w", "tn", m=D_MODEL, n=Z_W, tm=512, tn=Z_W)
    gW["w_in"] = jnp.concatenate([g_in_re[:, Z_U:Z_KR], g_in_re[:, Z_KR:Z_KR + ROPE], g_in_re[:, :Z_U]], axis=1)
    dx, gs["ln_in_g"], gs["ln_in_b"] = _mm(dz, w_in_re, "in_proj_bwd_ln_in", "nt", tm=512, tk=Z_W, epi=ln_bwd_epi,
                                           extras=(dr1, x), pextras=(g_in, b_in), outs=row_and_sums)
    return lossv[:, :1], dx, gW, gs, riding


_RAW_SMALL = (("loss", (1, 1)), ("ln_in_g", (1, D_MODEL)), ("ln_in_b", (1, D_MODEL)), ("ln1_g", (1, D_MODEL)),
              ("ln1_b", (1, D_MODEL)), ("ln2_g", (1, D_MODEL)), ("ln2_b", (1, D_MODEL)), ("ln3_g", (1, D_MODEL)),
              ("ln3_b", (1, D_MODEL)), ("q_norm_g", (1, S5_W)), ("kv_norm_g", (1, S5_W)), ("s5_d", (1, S5_W)),
              ("s5_ab", (1, 2 * S5_N)), ("s5_bb_re", (S5_G, S5_P, S5_H)), ("s5_bb_im", (S5_G, S5_P, S5_H)),
              ("s5_c_re", (S5_G, S5_H, S5_P)), ("s5_c_im", (S5_G, S5_H, S5_P)))


def kernel(x, mem, positions, ln_in_g, ln_in_b, w_in, s5_lam_re, s5_lam_im, s5_log_dt, s5_b_re, s5_b_im, s5_c_re, s5_c_im, s5_d, w_glu, q_norm_g, w_uq, kv_norm_g, w_ukv, w_oa, w_o, ln1_g, ln1_b, w_xq, w_xk, w_xv, w_xo, ln2_g, ln2_b, w_up, w_down, ln3_g, ln3_b, loss_target, m_ln_in_g, m_ln_in_b, m_w_in, m_s5_lam_re, m_s5_lam_im, m_s5_log_dt, m_s5_b_re, m_s5_b_im, m_s5_c_re, m_s5_c_im, m_s5_d, m_w_glu, m_q_norm_g, m_w_uq, m_kv_norm_g, m_w_ukv, m_w_oa, m_w_o, m_ln1_g, m_ln1_b, m_w_xq, m_w_xk, m_w_xv, m_w_xo, m_ln2_g, m_ln2_b, m_w_up, m_w_down, m_ln3_g, m_ln3_b, v_ln_in_g, v_ln_in_b, v_w_in, v_s5_lam_re, v_s5_lam_im, v_s5_log_dt, v_s5_b_re, v_s5_b_im, v_s5_c_re, v_s5_c_im, v_s5_d, v_w_glu, v_q_norm_g, v_w_uq, v_kv_norm_g, v_w_ukv, v_w_oa, v_w_o, v_ln1_g, v_ln1_b, v_w_xq, v_w_xk, v_w_xv, v_w_xo, v_ln2_g, v_ln2_b, v_w_up, v_w_down, v_ln3_g, v_ln3_b):
    a = dict(locals())
    wts = {n: a[n] for n in WEIGHTS}
    ms = {n: a["m_" + n] for n in WEIGHTS}
    vs = {n: a["v_" + n] for n in WEIGHTS}
    shard2d = {n: wts[n].reshape(wts[n].shape[-2], wts[n].shape[-1]) for n in BIG}
    nrow = {n: shard2d[n].size // 1024 for n in BIG}
    mine = 2 * lax.axis_index("x") + lax.axis_index("y")
    my_c = lax.axis_index("c")
    cidx = my_c.astype(jnp.int32).reshape(1)

    def group_rows(group):
        rows = sum(nrow[n] for n in group)
        return rows, (-rows) % PACK_ROWS

    def pack_shards(group):
        rows, pad = group_rows(group)
        parts = [shard2d[n].astype(BF16).reshape(nrow[n], 1024) for n in group] + [jnp.zeros((pad, 1024), BF16)]
        return jnp.concatenate(parts, axis=0).reshape(2, (rows + pad) // 2, 1024)

    def unpack_full(group, gathered):
        out, off = {}, 0
        for n in group:
            r, cc = shard2d[n].shape
            piece = lax.optimization_barrier(gathered[:, off:off + nrow[n]])
            out[n] = _full_from_shards(n, piece.reshape(4, r, cc))
            off += nrow[n]
        return out

    def pack_grads(group, gw):
        rows, pad = group_rows(group)
        parts = [_shards_from_full(n, gw[n].astype(BF16)).reshape(4, nrow[n], 1024) for n in group]
        return jnp.concatenate(parts + [jnp.zeros((4, pad, 1024), BF16)], axis=1).reshape(4, 2, (rows + pad) // 2, 1024)

    def by_core(own, other):
        return jnp.where(my_c == 0, jnp.stack([own, other]), jnp.stack([other, own]))

    packed_e, packed_l = pack_shards(BIG_EARLY), pack_shards(BIG_LATE)
    gathered = lax.dynamic_update_slice(_gather_weights(packed_e), packed_e[None], (mine, 0, 0, 0))
    W = unpack_full(BIG_EARLY, gathered.reshape(4, -1, 1024))

    def finish_gather(landed):
        half = packed_l.shape[1]
        own_half = lax.dynamic_index_in_dim(packed_l, my_c, axis=0, keepdims=True)
        landed = lax.dynamic_update_slice(landed, own_half, (mine, 0, 0))
        other = _pair_share(landed.reshape(4 * half, 1024), "gather_late_share").reshape(4, half, 1024)
        out, off = {}, 0
        for n in BIG_LATE:
            r, cc = shard2d[n].shape
            h, lo = divmod(off, half)
            assert lo + nrow[n] <= half, n
            piece = jnp.where(my_c == h, landed[:, lo:lo + nrow[n]], other[:, lo:lo + nrow[n]])
            out[n] = _full_from_shards(n, lax.optimization_barrier(piece).reshape(4, r, cc))
            off += nrow[n]
        return out

    def pair_reduce(group, gw, tag):
        gflat = pack_grads(group, gw)
        return _pair_add(gflat, _pair_exchange(gflat, "grad_pair_exchange_" + tag), cidx, "grad_pair_add_" + tag)

    def finish_reduce(group, sent, landed, tag):
        own = lax.dynamic_slice(sent, (mine, 0, 0), (1,) + sent.shape[1:])
        red = _chip_sum(lax.dynamic_update_slice(landed, own, (mine, 0, 0)), "grad_chip_sum_" + tag)
        gsh = by_core(red, _pair_share(red, "grad_pair_share_" + tag)).reshape(-1, 1024)
        out, off = {}, 0
        for n in group:
            out[n] = lax.optimization_barrier(gsh[off:off + nrow[n]]).reshape(shard2d[n].shape)
            off += nrow[n]
        return out

    sp = {n: wts[n] for n in SMALL}
    sp_local = {n: (sp[n][0] if sp[n].ndim > 1 else sp[n]) for n in SMALL}
    overlap = (packed_l, finish_gather, lambda gw: pair_reduce(BIG_LATE, gw, "late"))
    lossv, dx, gW, gs, riding = _local_step(x[0], mem[0], positions[0], loss_target[0], W, sp_local, overlap)

    g_out = finish_reduce(BIG_LATE, riding[0], riding[1], "late")
    sent_e = pair_reduce(BIG_EARLY, gW, "early")
    g_out.update(finish_reduce(BIG_EARLY, sent_e, _chip_exchange(sent_e, "grad_chip_exchange_early"), "early"))

    gs["loss"] = lossv
    raw = _pack_flat([gs[n].reshape(s) for n, s in _RAW_SMALL], 8 * 1024, F32)
    raw = _allreduce_small(raw.reshape(-1, 1024)).reshape(-1)
    rs = dict(zip([n for n, _ in _RAW_SMALL], _unpack_flat(raw, [s for _, s in _RAW_SMALL])))
    loss = rs["loss"].reshape(())
    _, disc_vjp = jax.vjp(_s5_discretize, sp_local["s5_lam_re"], sp_local["s5_lam_im"], sp_local["s5_log_dt"],
                          sp_local["s5_b_re"], sp_local["s5_b_im"])
    d_ab = rs["s5_ab"].reshape(2, S5_G, S5_P)
    g_lre, g_lim, g_ldt, g_bre, g_bim = disc_vjp((d_ab[0], d_ab[1], rs["s5_bb_re"], rs["s5_bb_im"]))
    small_g = {"s5_lam_re": g_lre, "s5_lam_im": g_lim, "s5_log_dt": g_ldt, "s5_b_re": g_bre, "s5_b_im": g_bim,
               "s5_c_re": rs["s5_c_re"], "s5_c_im": rs["s5_c_im"]}
    for n in ("ln_in_g", "ln_in_b", "ln1_g", "ln1_b", "ln2_g", "ln2_b", "ln3_g", "ln3_b", "q_norm_g", "kv_norm_g", "s5_d"):
        small_g[n] = rs[n]

    grads, deltas, new_m, new_v = {}, {}, {}, {}
    for n in BIG:
        d_, m_, v_ = _adamw(shard2d[n], g_out[n], ms[n].reshape(shard2d[n].shape), vs[n].reshape(shard2d[n].shape), "adamw_" + n)
        grads[n] = g_out[n].reshape(wts[n].shape)
        deltas[n], new_m[n], new_v[n] = (t.reshape(wts[n].shape) for t in (d_, m_, v_))
    as2d = lambda t: t.reshape(-1, t.shape[-1])
    sd, sm, sv = _adamw_many([as2d(wts[n]) for n in SMALL], [as2d(small_g[n].reshape(wts[n].shape)) for n in SMALL],
                             [as2d(ms[n]) for n in SMALL], [as2d(vs[n]) for n in SMALL], "adamw_small")
    for n, d_, m_, v_ in zip(SMALL, sd, sm, sv):
        grads[n] = small_g[n].reshape(wts[n].shape)
        deltas[n], new_m[n], new_v[n] = (t.reshape(wts[n].shape) for t in (d_, m_, v_))

    return (loss, dx[None], *[grads[n] for n in WEIGHTS], *[deltas[n] for n in WEIGHTS],
            *[new_m[n] for n in WEIGHTS], *[new_v[n] for n in WEIGHTS])
```

```python
import functools
import math

import jax
import jax.numpy as jnp
from jax import lax
from jax.experimental import pallas as pl
from jax.experimental.pallas import tpu as pltpu

F32 = jnp.float32
BF16 = jnp.bfloat16
MESH = pl.DeviceIdType.MESH

D_MODEL = 1024
S5_W = 256
S5_G = 16
S5_H = 16
S5_P = 64
S5_N = S5_G * S5_P
S5_MAX_RE = -1e-4
HEADS = 8
NOPE = 64
ROPE = 32
QK = NOPE + ROPE
VD = 64
HP = 128
XH = 4
XD = 256
LN_EPS = 1e-5
RMS_EPS = 1e-6
NEG_INF = -1e30
DN_ALPHA = 2.0 ** 0.25
ROPE_THETA = 10000.0
ADAM_LR, ADAM_B1, ADAM_B2, ADAM_EPS, ADAM_WD, ADAM_STEP = 0.001, 0.9, 0.999, 1e-08, 0.01, 10

Z_GS, Z_GA, Z_U, Z_CQ, Z_CKV, Z_KR, Z_W = 0, 1024, 2048, 2304, 2560, 2816, 2944
SCAN_CHUNK = 256

BIG_EARLY = ("w_in", "w_glu", "w_uq", "w_ukv", "w_oa", "w_o")
BIG_LATE = ("w_up", "w_xq", "w_xk", "w_xv", "w_xo", "w_down")
BIG = BIG_EARLY + BIG_LATE
COL_SHARDED = ("w_in", "w_glu", "w_uq", "w_ukv", "w_oa", "w_up")
SMALL = ("ln_in_g", "ln_in_b", "s5_lam_re", "s5_lam_im", "s5_log_dt", "s5_b_re", "s5_b_im", "s5_c_re", "s5_c_im",
         "s5_d", "q_norm_g", "kv_norm_g", "ln1_g", "ln1_b", "ln2_g", "ln2_b", "ln3_g", "ln3_b")
WEIGHTS = ("ln_in_g", "ln_in_b", "w_in", "s5_lam_re", "s5_lam_im", "s5_log_dt", "s5_b_re", "s5_b_im", "s5_c_re",
           "s5_c_im", "s5_d", "w_glu", "q_norm_g", "w_uq", "kv_norm_g", "w_ukv", "w_oa", "w_o", "ln1_g", "ln1_b",
           "w_xq", "w_xk", "w_xv", "w_xo", "ln2_g", "ln2_b", "w_up", "w_down", "ln3_g", "ln3_b")
PACK_ROWS = 2 * 16


def _row_tile(n, cap=512):
    best = n
    for t in range(8, min(n, cap) + 1, 8):
        if n % t == 0:
            best = t
    return best


def _pick(n, cap):
    best = None
    for t in range(128, min(n, cap) + 1, 128):
        if n % t == 0:
            best = t
    return best if best is not None and (best >= 512 or best == n) else n


def _mm(a, b, name, mode="nn", *, a_off=0, b_off=0, m=None, n=None, act=None, epi=None, extras=(), rextras=(),
        pextras=(), outs=None, out_dtype=None, tm=1024, tn=1024, tk=None):
    out_dtype = out_dtype or (BF16 if mode == "tn" else F32)
    if mode == "nn":
        M, (K, N) = a.shape[0], b.shape
    elif mode == "nt":
        M, (N, K) = a.shape[0], b.shape
    else:
        K, M, N = a.shape[0], m, n
    if mode == "tn":
        tm, tn, tk = _pick(M, tm), _pick(N, tn), min(tk or 1024, K)
    else:
        tm, tn, tk = min(tm, M), _pick(N, tn), _pick(K, tk or 1024)
    assert M % tm == 0 and N % tn == 0 and K % tk == 0, (name, M, N, K, tm, tn, tk)
    nk = K // tk
    if mode == "tn":
        assert a_off % tm == 0 and b_off % tn == 0
        ao, bo = a_off // tm, b_off // tn
        a_spec = pl.BlockSpec((tk, tm), lambda i, j, k: (k, i + ao))
        b_spec = pl.BlockSpec((tk, tn), lambda i, j, k: (k, j + bo))
        dims = (((0,), (0,)), ((), ()))
    else:
        assert a_off % tk == 0
        ao = a_off // tk
        a_spec = pl.BlockSpec((tm, tk), lambda i, j, k: (i, k + ao))
        if mode == "nn":
            b_spec = pl.BlockSpec((tk, tn), lambda i, j, k: (k, j))
            dims = (((1,), (0,)), ((), ()))
        else:
            b_spec = pl.BlockSpec((tn, tk), lambda i, j, k: (j, k))
            dims = (((1,), (1,)), ((), ()))
    e_spec = pl.BlockSpec((tm, tn), lambda i, j, k: (i, j))
    r_specs = [pl.BlockSpec((tm, w), functools.partial(lambda i, j, k, o: (i, o), o=off // w)) for _, off, w in rextras]
    p_specs = [pl.BlockSpec((1, tn), lambda i, j, k: (0, j)) if p.shape[1] == N else pl.BlockSpec(p.shape, lambda i, j, k: (0, 0))
               for p in pextras]
    n_extra = len(extras) + len(rextras) + len(pextras)
    if outs is None:
        o_specs, o_shapes = [e_spec], [jax.ShapeDtypeStruct((M, N), out_dtype)]
    else:
        assert all((w == tn and dt != "sum") or N == tn for w, dt in outs), name
        o_specs = [e_spec if w == tn and dt != "sum" else pl.BlockSpec((8 if dt == "sum" else tm, w), lambda i, j, k: (i, 0))
                   for w, dt in outs]
        o_shapes = [jax.ShapeDtypeStruct((M // tm * 8, w), F32) if dt == "sum" else
                    jax.ShapeDtypeStruct((M, N if w == tn else w), dt) for w, dt in outs]
    n_out = len(o_specs)
    direct = outs is None and epi is None and out_dtype == F32

    def body(*refs):
        a_ref, b_ref = refs[0], refs[1]
        e_refs = refs[2:2 + n_extra]
        o_refs = refs[2 + n_extra:2 + n_extra + n_out]
        av = a_ref[...]
        if act is not None:
            av = act(av.astype(F32))
        p = lax.dot_general(av.astype(BF16), b_ref[...].astype(BF16), dims, preferred_element_type=F32)

        def finish(r):
            if epi is not None:
                r = epi(r, *[e[...] for e in e_refs])
            for o_ref, v in zip(o_refs, r if isinstance(r, tuple) else (r,)):
                o_ref[...] = jnp.broadcast_to(v, o_ref.shape).astype(o_ref.dtype)

        if nk == 1:
            finish(p)
        else:
            acc = o_refs[0] if direct else refs[2 + n_extra + n_out]
            k = pl.program_id(2)

            @pl.when(k == 0)
            def _():
                acc[...] = p

            @pl.when(k > 0)
            def _():
                acc[...] += p

            if not direct:
                @pl.when(k == nk - 1)
                def _():
                    finish(acc[...])

    res = pl.pallas_call(
        body, name=name, grid=(M // tm, N // tn, nk),
        in_specs=[a_spec, b_spec] + [e_spec] * len(extras) + r_specs + p_specs, out_specs=o_specs,
        out_shape=o_shapes,
        scratch_shapes=[pltpu.VMEM((tm, tn), F32)] if nk > 1 and not direct else [],
        compiler_params=pltpu.CompilerParams(dimension_semantics=("parallel", "parallel", "arbitrary")),
    )(a, b, *extras, *[r[0] for r in rextras], *pextras)
    if outs is None:
        return res[0]
    return [r.reshape(M // tm, 8, -1)[:, 0].sum(axis=0, keepdims=True) if dt == "sum" else r for r, (_, dt) in zip(res, outs)]


def _rowwise(fn, name, rows, pars, outs, accs=(), tm=256, upcast=True):
    T = rows[0][0].shape[0]
    tm = min(tm, T)
    assert T % tm == 0
    n_in, n_out = len(rows) + len(pars), len(outs)
    in_specs = []
    for arr, off, w in rows:
        assert off % w == 0 and arr.shape[0] == T, name
        in_specs.append(pl.BlockSpec((tm, w), functools.partial(lambda i, o: (i, o), o=off // w)))
    for p in pars:
        in_specs.append(pl.BlockSpec(p.shape, lambda i: (0, 0)))
    out_specs = [pl.BlockSpec((tm, w), lambda i: (i, 0)) for w, _ in outs]
    out_specs += [pl.BlockSpec(s, lambda i: (0, 0)) for s in accs]
    out_shape = [jax.ShapeDtypeStruct((T, w), dt) for w, dt in outs] + [jax.ShapeDtypeStruct(s, F32) for s in accs]

    def body(*refs):
        res = fn(*[r[...].astype(F32) if upcast else r[...] for r in refs[:n_in]])
        o_refs = refs[n_in:]
        for r, v in zip(o_refs[:n_out], res[:n_out]):
            r[...] = v.astype(r.dtype)
        if accs:
            i = pl.program_id(0)

            @pl.when(i == 0)
            def _():
                for r in o_refs[n_out:]:
                    r[...] = jnp.zeros_like(r)

            for r, v in zip(o_refs[n_out:], res[n_out:]):
                r[...] += v

    res = pl.pallas_call(
        body, name=name, grid=(T // tm,), in_specs=in_specs, out_specs=out_specs, out_shape=out_shape,
        compiler_params=pltpu.CompilerParams(dimension_semantics=("arbitrary",)),
    )(*[r[0] for r in rows], *pars)
    return res


def _whole(arr):
    return (arr, 0, arr.shape[1])


def _ln(x, g, b):
    mu = jnp.mean(x, axis=-1, keepdims=True)
    xc = x - mu
    var = jnp.mean(xc * xc, axis=-1, keepdims=True)
    return xc * lax.rsqrt(var + LN_EPS) * g + b


def _ln_res(h, r, g, b):
    return _ln(DN_ALPHA * h + r, g, b)


def _rms(x, g):
    return x * lax.rsqrt(jnp.mean(x * x, axis=-1, keepdims=True) + RMS_EPS) * g


def _gelu_skip(yc, u, d):
    y = yc + d * u
    return 0.5 * y * (1.0 + lax.erf(y * (1.0 / math.sqrt(2.0))))


def _gate_mix(gs, ga, y2a, y2b, aout):
    return jax.nn.sigmoid(gs) * (y2a * jax.nn.sigmoid(y2b)) + jax.nn.sigmoid(ga) * aout


def _relu2(a):
    r = jnp.maximum(a, 0.0)
    return r * r


def _tile8(t):
    return jnp.concatenate([t] * HEADS, axis=1)


def _rope(x, cos, s1, s2):
    w = x.shape[1]
    return x * cos + pltpu.roll(x, ROPE // 2, 1) * s1 + pltpu.roll(x, w - ROPE // 2, 1) * s2


def _rope_t(dy, cos, s1, s2):
    w = dy.shape[1]
    return dy * cos + pltpu.roll(dy * s1, w - ROPE // 2, 1) + pltpu.roll(dy * s2, ROPE // 2, 1)


def _rope_lanes(shape):
    lane = lax.broadcasted_iota(jnp.int32, shape, 1) % HP
    return (lane >= NOPE) & (lane < QK)


ATT_HB_FWD, ATT_HB_BWD = 8, 8
ATT_BQ_FWD = 1024
ATT_SCALE = QK ** -0.5
LOG2E = 1.4426950408889634
LN2 = 0.6931471805599453
_NT = (((1,), (1,)), ((), ()))
_TN = (((0,), (0,)), ((), ()))


def _carry_specs(carry):
    if carry is None:
        return [], [], [], []
    arr = carry[0]
    return ([_HBM], [_HBM], [jax.ShapeDtypeStruct((4,) + arr.shape[1:], arr.dtype)],
            [pltpu.SemaphoreType.DMA((3,)), pltpu.SemaphoreType.DMA((3,))])


def _carry_start(exchange, when):
    @pl.when(when)
    def _():
        for cp in exchange(False):
            cp.start()


def _carry_wait(exchange, when):
    @pl.when(when)
    def _():
        for cp in exchange(True):
            cp.wait_recv()
        for cp in exchange(False):
            cp.wait_send()


def _tri_tables(n, by_row):
    if by_row:
        pairs = [(i, j) for i in range(n) for j in range(i + 1)]
    else:
        pairs = [(i, j) for j in range(n) for i in range(j, n)]
    return jnp.array([p[0] for p in pairs], jnp.int32), jnp.array([p[1] for p in pairs], jnp.int32)


def _flash_fwd(q, k, v, name, bq=512, carry=None):
    T = q.shape[0]
    bq = min(bq, T)
    nq = T // bq
    hb = ATT_HB_FWD
    n_carry = 0 if carry is None else 1

    i_tab, j_tab = _tri_tables(nq, by_row=True)

    def body(it_ref, jt_ref, q_ref, k_ref, v_ref, *rest):
        o_ref, lse_ref = rest[n_carry:n_carry + 2]
        m_sc, l_sc, acc_sc = rest[2 * n_carry + 2:2 * n_carry + 5]
        t = pl.program_id(1)
        i, j = it_ref[t], jt_ref[t]
        if carry is not None:
            exchange = functools.partial(_carried_exchange, rest[0], rest[3], rest[-2], rest[-1], carry[1])
            _carry_start(exchange, (pl.program_id(0) == 0) & (t == 0))

        @pl.when(j == 0)
        def _():
            m_sc[...] = jnp.full_like(m_sc, NEG_INF)
            l_sc[...] = jnp.zeros_like(l_sc)
            acc_sc[...] = jnp.zeros_like(acc_sc)

        def step(masked):
            for hh in range(hb):
                sl = slice(hh * HP, (hh + 1) * HP)
                s = lax.dot_general(q_ref[:, sl], k_ref[:, sl], _NT, preferred_element_type=F32)
                if masked:
                    row = lax.broadcasted_iota(jnp.int32, (bq, bq), 0)
                    col = lax.broadcasted_iota(jnp.int32, (bq, bq), 1)
                    s = jnp.where(col <= row, s, NEG_INF)
                m_prev = m_sc[hh]
                m_new = jnp.maximum(m_prev, jnp.max(s, axis=1, keepdims=True))
                alpha = jnp.exp2(m_prev - m_new)
                p = jnp.exp2(s - jnp.concatenate([m_new] * (bq // HP), axis=1))
                l_sc[hh] = alpha * l_sc[hh] + jnp.sum(p, axis=1, keepdims=True)
                acc_sc[hh] = alpha * acc_sc[hh] + jnp.dot(p.astype(BF16), v_ref[:, sl], preferred_element_type=F32)
                m_sc[hh] = m_new

        @pl.when(j < i)
        def _():
            step(False)

        @pl.when(j == i)
        def _():
            step(True)
            for hh in range(hb):
                o_ref[:, hh * HP:(hh + 1) * HP] = acc_sc[hh] / l_sc[hh]
                lse_ref[hh] = jnp.transpose((m_sc[hh] + jnp.log2(l_sc[hh])) * LN2)[0:1, :]

        if carry is not None:
            _carry_wait(exchange, (pl.program_id(0) == HEADS // hb - 1) & (t == i_tab.shape[0] - 1))

    qs = pl.BlockSpec((bq, hb * HP), lambda h, t, it, jt: (it[t], h))
    ks = pl.BlockSpec((bq, hb * HP), lambda h, t, it, jt: (jt[t], h))
    c_in, c_out, c_shape, c_sems = _carry_specs(carry)
    gs = pltpu.PrefetchScalarGridSpec(
        num_scalar_prefetch=2, grid=(HEADS // hb, i_tab.shape[0]), in_specs=[qs, ks, ks] + c_in,
        out_specs=[qs, pl.BlockSpec((hb, 1, bq), lambda h, t, it, jt: (h, 0, it[t]))] + c_out,
        scratch_shapes=[pltpu.VMEM((hb, bq, HP), F32)] * 3 + c_sems)
    return pl.pallas_call(
        body, name=name, grid_spec=gs,
        out_shape=[jax.ShapeDtypeStruct((T, HEADS * HP), F32), jax.ShapeDtypeStruct((HEADS, 1, T), F32)] + c_shape,
        compiler_params=pltpu.CompilerParams(dimension_semantics=("arbitrary", "arbitrary")),
    )(i_tab, j_tab, q, k, v, *([] if carry is None else [carry[0]]))


def _flash_bwd(q, k, v, do, lse_row, delta_row, name, bq=512, carry=None):
    T = q.shape[0]
    bq = min(bq, T)
    nq = T // bq
    hb = ATT_HB_BWD
    n_carry = 0 if carry is None else 1

    i_tab, j_tab = _tri_tables(nq, by_row=False)
    n_blk = i_tab.shape[0]

    def body(it_ref, jt_ref, q_ref, k_ref, v_ref, do_ref, lse_ref, dl_ref, *rest):
        dq_ref, dk_ref, dv_ref = rest[n_carry:n_carry + 3]
        t = pl.program_id(1)
        i, j = it_ref[t], jt_ref[t]
        if carry is not None:
            exchange = functools.partial(_carried_exchange, rest[0], rest[4], rest[-2], rest[-1], carry[1])
            _carry_start(exchange, (pl.program_id(0) == 0) & (t == 0))

        @pl.when(t == 0)
        def _():
            dq_ref[...] = jnp.zeros_like(dq_ref)

        @pl.when(i == j)
        def _():
            dk_ref[...] = jnp.zeros_like(dk_ref)
            dv_ref[...] = jnp.zeros_like(dv_ref)

        def step(masked):
            rows = pl.ds(pl.multiple_of(i * bq, bq), bq)
            for hh in range(hb):
                sl = slice(hh * HP, (hh + 1) * HP)
                qh, kh, doh = q_ref[:, sl], k_ref[:, sl], do_ref[:, sl]
                st = lax.dot_general(kh, qh, _NT, preferred_element_type=F32)
                pt = jnp.exp2(st - lse_ref[hh] * LOG2E)
                if masked:
                    krow = lax.broadcasted_iota(jnp.int32, (bq, bq), 0)
                    qcol = lax.broadcasted_iota(jnp.int32, (bq, bq), 1)
                    pt = jnp.where(krow <= qcol, pt, 0.0)
                dv_ref[:, sl] += jnp.dot(pt.astype(BF16), doh, preferred_element_type=F32)
                dpt = lax.dot_general(v_ref[:, sl], doh, _NT, preferred_element_type=F32)
                dst = (pt * (dpt - dl_ref[hh])).astype(BF16)
                dk_ref[:, sl] += jnp.dot(dst, qh, preferred_element_type=F32)
                dq_ref[rows, sl] += lax.dot_general(dst, kh, _TN, preferred_element_type=F32)

        @pl.when(i > j)
        def _():
            step(False)

        @pl.when(i == j)
        def _():
            step(True)

        @pl.when(i == nq - 1)
        def _():
            dk_ref[...] *= LN2

        @pl.when(t == n_blk - 1)
        def _():
            dq_ref[...] *= ATT_SCALE

        if carry is not None:
            _carry_wait(exchange, (pl.program_id(0) == HEADS // hb - 1) & (t == n_blk - 1))

    qs = pl.BlockSpec((bq, hb * HP), lambda h, t, it, jt: (it[t], h))
    ks = pl.BlockSpec((bq, hb * HP), lambda h, t, it, jt: (jt[t], h))
    rs = pl.BlockSpec((hb, 1, bq), lambda h, t, it, jt: (h, 0, it[t]))
    full = pl.BlockSpec((T, hb * HP), lambda h, t, it, jt: (0, h), pipeline_mode=pl.Buffered(1))
    c_in, c_out, c_shape, c_sems = _carry_specs(carry)
    gs = pltpu.PrefetchScalarGridSpec(num_scalar_prefetch=2, grid=(HEADS // hb, n_blk),
                                      in_specs=[qs, ks, ks, qs, rs, rs] + c_in, out_specs=[full, ks, ks] + c_out,
                                      scratch_shapes=c_sems)
    return pl.pallas_call(
        body, name=name, grid_spec=gs, out_shape=[jax.ShapeDtypeStruct((T, HEADS * HP), F32)] * 3 + c_shape,
        compiler_params=pltpu.CompilerParams(dimension_semantics=("arbitrary", "arbitrary")),
    )(i_tab, j_tab, q, k, v, do, lse_row, delta_row, *([] if carry is None else [carry[0]]))


def _xattn_heads(q, k, v):
    scale = XD ** -0.5
    ps = []
    for h in range(XH):
        sl = slice(h * XD, (h + 1) * XD)
        s = lax.dot_general(q[:, sl].astype(BF16), k[:, sl].astype(BF16), (((1,), (1,)), ((), ())),
                            preferred_element_type=F32) * scale
        e = jnp.exp(s - jnp.max(s, axis=1, keepdims=True))
        ps.append(e / jnp.sum(e, axis=1, keepdims=True))
    return ps


def _xattn_fwd_fn(q, k, v):
    ps = _xattn_heads(q, k, v)
    o = [jnp.dot(p.astype(BF16), v[:, h * XD:(h + 1) * XD].astype(BF16), preferred_element_type=F32)
         for h, p in enumerate(ps)]
    return (jnp.concatenate(o, axis=1),)


def _xattn_bwd_fn(q, do, k, v):
    scale = XD ** -0.5
    ps = _xattn_heads(q, k, v)
    dqs, dks, dvs = [], [], []
    tdims = (((0,), (0,)), ((), ()))
    for h, p in enumerate(ps):
        sl = slice(h * XD, (h + 1) * XD)
        doh = do[:, sl].astype(BF16)
        dvs.append(lax.dot_general(p.astype(BF16), doh, tdims, preferred_element_type=F32))
        dp = lax.dot_general(doh, v[:, sl].astype(BF16), (((1,), (1,)), ((), ())), preferred_element_type=F32)
        ds = (p * (dp - jnp.sum(dp * p, axis=1, keepdims=True)) * scale).astype(BF16)
        dqs.append(jnp.dot(ds, k[:, sl].astype(BF16), preferred_element_type=F32))
        dks.append(lax.dot_general(ds, q[:, sl].astype(BF16), tdims, preferred_element_type=F32))
    return jnp.concatenate(dqs, axis=1), jnp.concatenate(dks, axis=1), jnp.concatenate(dvs, axis=1)


S5_SUB = 8


def _cmul_add(xr, xi, ar, ai, sr, si):
    return xr + ar * sr - ai * si, xi + ar * si + ai * sr


def _s5_tables(ar, ai, reverse):
    pows = [(ar, ai)]
    for _ in range(S5_SUB - 1):
        pr, pi = pows[-1]
        pows.append((pr * ar - pi * ai, pr * ai + pi * ar))
    cr = jnp.concatenate([p[0] for p in pows], axis=0)
    ci = jnp.concatenate([p[1] for p in pows], axis=0)
    if reverse:
        cr, ci = cr[::-1], ci[::-1]
    t = jnp.arange(S5_SUB)[:, None]
    blocks = [jnp.concatenate([cr, ci], axis=1)]
    for s in (1, 2, 4):
        keep = (t < S5_SUB - s) if reverse else (t >= s)
        sr, si = pows[s - 1]
        blocks.append(jnp.concatenate([jnp.where(keep, sr, 0.0), jnp.where(keep, si, 0.0)], axis=1))
    return jnp.concatenate(blocks, axis=0)


def _sub_scan(xr, xi, tab_ref, reverse):
    for k, s in enumerate((1, 2, 4)):
        blk = slice(S5_SUB * (k + 1), S5_SUB * (k + 2))
        sh = S5_SUB - s if reverse else s
        xr, xi = _cmul_add(xr, xi, tab_ref[blk, :S5_N], tab_ref[blk, S5_N:], pltpu.roll(xr, sh, 0), pltpu.roll(xi, sh, 0))
    return xr, xi


def _s5_fwd(z, b_st, c_st, d_skip, tab, name, tc=512):
    T = z.shape[0]
    tc = min(tc, T)
    nsub = tc // S5_SUB

    def body(u_ref, bst_ref, cst_ref, d_ref, tab_ref, h_ref, yc_ref, yg_ref, carry, x_sc):
        @pl.when(pl.program_id(0) == 0)
        def _():
            carry[...] = jnp.zeros_like(carry)

        u = u_ref[...].astype(F32)
        x_sc[...] = jnp.dot(u.astype(BF16), bst_ref[...], preferred_element_type=F32)

        def sub(b, c):
            r = pl.ds(pl.multiple_of(b * S5_SUB, S5_SUB), S5_SUB)
            xr, xi = _sub_scan(x_sc[r, :S5_N], x_sc[r, S5_N:], tab_ref, False)
            hr, hi = _cmul_add(xr, xi, tab_ref[0:S5_SUB, :S5_N], tab_ref[0:S5_SUB, S5_N:], c[0], c[1])
            x_sc[r, :S5_N] = hr
            x_sc[r, S5_N:] = hi
            return hr[S5_SUB - 1:S5_SUB], hi[S5_SUB - 1:S5_SUB]

        cr, ci = lax.fori_loop(0, nsub, sub, (carry[0:1, :S5_N], carry[0:1, S5_N:]))
        carry[0:1, :S5_N] = cr
        carry[0:1, S5_N:] = ci
        h = x_sc[...]
        h_ref[...] = h
        yc = jnp.dot(h.astype(BF16), cst_ref[...], preferred_element_type=F32)
        yc_ref[...] = yc
        yg_ref[...] = _gelu_skip(yc, u, d_ref[...]).astype(yg_ref.dtype)

    whole = lambda a: pl.BlockSpec(a.shape, lambda i: (0, 0))
    row = lambda w: pl.BlockSpec((tc, w), lambda i: (i, 0))
    return pl.pallas_call(
        body, name=name, grid=(T // tc,),
        in_specs=[pl.BlockSpec((tc, S5_W), lambda i: (i, Z_U // S5_W)), whole(b_st), whole(c_st), whole(d_skip), whole(tab)],
        out_specs=[row(2 * S5_N), row(S5_W), row(S5_W)],
        out_shape=[jax.ShapeDtypeStruct((T, 2 * S5_N), F32), jax.ShapeDtypeStruct((T, S5_W), F32),
                   jax.ShapeDtypeStruct((T, S5_W), BF16)],
        scratch_shapes=[pltpu.VMEM((8, 2 * S5_N), F32), pltpu.VMEM((tc, 2 * S5_N), F32)],
        compiler_params=pltpu.CompilerParams(dimension_semantics=("arbitrary",)),
    )(z, b_st, c_st, d_skip, tab)


def _s5_bwd(dyg, yc, z, h, b_st, c_st, d_skip, tab, name, tc=512):
    T = z.shape[0]
    tc = min(tc, T)
    nc, nsub = T // tc, tc // S5_SUB

    def body(dyg_ref, yc_ref, u_ref, h_ref, hp_ref, bst_ref, cst_ref, d_ref, tab_ref,
             du_ref, da_ref, db_ref, dct_ref, dd_ref, carry, x_sc):
        g = pl.program_id(0)

        @pl.when(g == 0)
        def _():
            carry[...] = jnp.zeros_like(carry)
            for r in (da_ref, db_ref, dct_ref, dd_ref):
                r[...] = jnp.zeros_like(r)

        u = u_ref[...].astype(F32)
        _, vjp = jax.vjp(_gelu_skip, yc_ref[...], u, d_ref[...])
        dyc, du_skip, dd = vjp(dyg_ref[...])
        dd_ref[...] += dd
        dyc16 = dyc.astype(BF16)
        dct_ref[...] += lax.dot_general(dyc16, h_ref[...].astype(BF16), _TN, preferred_element_type=F32)
        x_sc[...] = lax.dot_general(dyc16, cst_ref[...], _NT, preferred_element_type=F32)
        first = jnp.where(g == nc - 1, 0.0, 1.0)
        row0 = lax.broadcasted_iota(jnp.int32, (S5_SUB, S5_N), 0) == 0

        def sub(k, c):
            cr, ci, acc_r, acc_i = c
            b = nsub - 1 - k
            r = pl.ds(pl.multiple_of(b * S5_SUB, S5_SUB), S5_SUB)
            xr, xi = _sub_scan(x_sc[r, :S5_N], x_sc[r, S5_N:], tab_ref, True)
            lr, li = _cmul_add(xr, xi, tab_ref[0:S5_SUB, :S5_N], tab_ref[0:S5_SUB, S5_N:], cr, ci)
            x_sc[r, :S5_N] = lr
            x_sc[r, S5_N:] = li
            rp = pl.ds(pl.multiple_of(jnp.maximum(b - 1, 0) * S5_SUB, S5_SUB), S5_SUB)
            last_r = jnp.where(b == 0, hp_ref[S5_SUB - 1:S5_SUB, :S5_N] * first, h_ref[rp, :S5_N][S5_SUB - 1:S5_SUB])
            last_i = jnp.where(b == 0, hp_ref[S5_SUB - 1:S5_SUB, S5_N:] * first, h_ref[rp, S5_N:][S5_SUB - 1:S5_SUB])
            pr = jnp.where(row0, last_r, pltpu.roll(h_ref[r, :S5_N], 1, 0))
            pi = jnp.where(row0, last_i, pltpu.roll(h_ref[r, S5_N:], 1, 0))
            return lr[0:1], li[0:1], acc_r + (lr * pr + li * pi), acc_i + (li * pr - lr * pi)

        zero = jnp.zeros((S5_SUB, S5_N), F32)
        cr, ci, acc_r, acc_i = lax.fori_loop(0, nsub, sub, (carry[0:1, :S5_N], carry[0:1, S5_N:], zero, zero))
        carry[0:1, :S5_N] = cr
        carry[0:1, S5_N:] = ci
        da_ref[0:1, :S5_N] += jnp.sum(acc_r, axis=0, keepdims=True)
        da_ref[0:1, S5_N:] += jnp.sum(acc_i, axis=0, keepdims=True)
        lam16 = x_sc[...].astype(BF16)
        db_ref[...] += lax.dot_general(u.astype(BF16), lam16, _TN, preferred_element_type=F32)
        du = lax.dot_general(lam16, bst_ref[...], _NT, preferred_element_type=F32) + du_skip
        du_ref[...] = du.astype(du_ref.dtype)

    whole = lambda a: pl.BlockSpec(a.shape, lambda g: (0, 0))
    row = lambda w, off=0: pl.BlockSpec((tc, w), lambda g: (nc - 1 - g, off))
    prev = pl.BlockSpec((S5_SUB, 2 * S5_N), lambda g: (jnp.maximum((nc - 1 - g) * nsub - 1, 0), 0))
    acc = lambda s: pl.BlockSpec(s, lambda g: (0, 0))
    return pl.pallas_call(
        body, name=name, grid=(nc,),
        in_specs=[row(S5_W), row(S5_W), row(S5_W, Z_U // S5_W), row(2 * S5_N), prev, whole(b_st), whole(c_st),
                  whole(d_skip), whole(tab)],
        out_specs=[row(S5_W), acc((1, 2 * S5_N)), acc((S5_W, 2 * S5_N)), acc((S5_W, 2 * S5_N)), acc((1, S5_W))],
        out_shape=[jax.ShapeDtypeStruct((T, S5_W), BF16), jax.ShapeDtypeStruct((1, 2 * S5_N), F32),
                   jax.ShapeDtypeStruct((S5_W, 2 * S5_N), F32), jax.ShapeDtypeStruct((S5_W, 2 * S5_N), F32),
                   jax.ShapeDtypeStruct((1, S5_W), F32)],
        scratch_shapes=[pltpu.VMEM((8, 2 * S5_N), F32), pltpu.VMEM((tc, 2 * S5_N), F32)],
        compiler_params=pltpu.CompilerParams(dimension_semantics=("arbitrary",)),
    )(dyg, yc, z, h, h, b_st, c_st, d_skip, tab)


def _s5_discretize(lam_re, lam_im, log_dt, b_re, b_im):
    lr = jnp.minimum(lam_re, S5_MAX_RE)
    li = lam_im
    dt = jnp.exp(log_dt)[:, None]
    mag = jnp.exp(lr * dt)
    ang = li * dt
    ab_re = mag * jnp.cos(ang)
    ab_im = mag * jnp.sin(ang)
    den = lr * lr + li * li
    nr = ab_re - 1.0
    f_re = ((nr * lr + ab_im * li) / den)[..., None]
    f_im = ((ab_im * lr - nr * li) / den)[..., None]
    return ab_re, ab_im, f_re * b_re - f_im * b_im, f_re * b_im + f_im * b_re


def _adamw_fn(w, g, m, v):
    m = ADAM_B1 * m + (1.0 - ADAM_B1) * g
    v = ADAM_B2 * v + (1.0 - ADAM_B2) * (g * g)
    m_hat = m / (1.0 - ADAM_B1 ** ADAM_STEP)
    v_hat = v / (1.0 - ADAM_B2 ** ADAM_STEP)
    delta = -ADAM_LR * (m_hat / (jnp.sqrt(v_hat) + ADAM_EPS) + ADAM_WD * w)
    return delta, m, v


def _adamw(w, g, m, v, name):
    wd = w.shape[1]
    return _rowwise(_adamw_fn, name, [_whole(w), _whole(g), _whole(m), _whole(v)], [], [(wd, F32)] * 3,
                    tm=_row_tile(w.shape[0]))


def _adamw_many(ws, gs, ms, vs, name):
    n = len(ws)

    def body(*refs):
        for i in range(n):
            d_, m_, v_ = _adamw_fn(*[refs[kk * n + i][...] for kk in range(4)])
            refs[4 * n + i][...] = d_
            refs[5 * n + i][...] = m_
            refs[6 * n + i][...] = v_

    out = pl.pallas_call(body, name=name, out_shape=[jax.ShapeDtypeStruct(w.shape, F32) for w in ws] * 3)(*ws, *gs, *ms, *vs)
    return out[:n], out[n:2 * n], out[2 * n:]


def _me():
    return lax.axis_index("x"), lax.axis_index("y"), lax.axis_index("c")


def _chip_of(j):
    return j // 2, j % 2


_HBM = pl.BlockSpec(memory_space=pl.ANY)


def _gather_weights(shard):
    _, R, W = shard.shape

    def body(s_ref, out_ref, send_sems, recv_sems):
        x, y, c = _me()
        mine = 2 * x + y
        sib = (x, y, 1 - c)
        chips = [(1 - x, y), (x, 1 - y), (1 - x, 1 - y)]

        def rcopy(kk, src, chip_idx, half, to):
            return pltpu.make_async_remote_copy(src_ref=src, dst_ref=out_ref.at[chip_idx, half],
                                                send_sem=send_sems.at[kk], recv_sem=recv_sems.at[kk],
                                                device_id=to, device_id_type=MESH)

        first = [rcopy(j, s_ref.at[c], mine, c, (*chip, c)) for j, chip in enumerate(chips)]
        for cp in first:
            cp.start()
        passed = []
        for j, (px, py) in enumerate(chips):
            src_chip = 2 * px + py
            rcopy(j, s_ref.at[c], src_chip, c, (x, y, c)).wait_recv()
            fwd = rcopy(3 + j, out_ref.at[src_chip, c], src_chip, c, sib)
            fwd.start()
            passed.append(fwd)
        for j, (px, py) in enumerate(chips):
            rcopy(3 + j, s_ref.at[c], 2 * px + py, 1 - c, (x, y, c)).wait_recv()
        for cp in first + passed:
            cp.wait_send()

    return pl.pallas_call(
        body, name="gather_weights", in_specs=[_HBM], out_specs=_HBM,
        out_shape=jax.ShapeDtypeStruct((4, 2, R, W), shard.dtype),
        scratch_shapes=[pltpu.SemaphoreType.DMA((6,)), pltpu.SemaphoreType.DMA((6,))],
    )(shard)


def _pair_exchange(g, name):
    _, _, R, W = g.shape

    def body(g_ref, out_ref, send_sem, recv_sem):
        x, y, c = _me()
        cp = pltpu.make_async_remote_copy(src_ref=g_ref.at[:, 1 - c], dst_ref=out_ref, send_sem=send_sem,
                                          recv_sem=recv_sem, device_id=(x, y, 1 - c), device_id_type=MESH)
        cp.start()
        cp.wait()

    return pl.pallas_call(
        body, name=name, in_specs=[_HBM], out_specs=_HBM,
        out_shape=jax.ShapeDtypeStruct((4, R, W), g.dtype),
        scratch_shapes=[pltpu.SemaphoreType.DMA(()), pltpu.SemaphoreType.DMA(())],
    )(g)


def _pair_add(g, recv, cidx, name):
    _, _, R, W = g.shape
    tr = _row_tile(R)

    def body(c_ref, a_ref, b_ref, o_ref):
        o_ref[...] = (a_ref[...].astype(F32) + b_ref[...].astype(F32)).astype(o_ref.dtype)

    gs = pltpu.PrefetchScalarGridSpec(
        num_scalar_prefetch=1, grid=(4, R // tr),
        in_specs=[pl.BlockSpec((None, None, tr, W), lambda j, r, c: (j, c[0], r, 0)),
                  pl.BlockSpec((None, tr, W), lambda j, r, c: (j, r, 0))],
        out_specs=pl.BlockSpec((None, tr, W), lambda j, r, c: (j, r, 0)))
    return pl.pallas_call(body, name=name, grid_spec=gs,
                          out_shape=jax.ShapeDtypeStruct((4, R, W), BF16))(cidx, g, recv)


def _carried_exchange(src_ref, dst_ref, send_sems, recv_sems, spread, incoming):
    x, y, c = _me()
    mine = 2 * x + y
    copies = []
    for j, (px, py) in enumerate([(1 - x, y), (x, 1 - y), (1 - x, 1 - y)]):
        src = src_ref.at[2 * px + py] if spread else src_ref.at[c]
        slot = 2 * px + py if incoming else mine
        copies.append(pltpu.make_async_remote_copy(src_ref=src, dst_ref=dst_ref.at[slot], send_sem=send_sems.at[j],
                                                   recv_sem=recv_sems.at[j], device_id=(px, py, c), device_id_type=MESH))
    return copies


def _chip_exchange(p, name):
    _, R, W = p.shape

    def body(p_ref, out_ref, send_sems, recv_sems):
        outs = _carried_exchange(p_ref, out_ref, send_sems, recv_sems, True, False)
        for cp in outs:
            cp.start()
        for cp in _carried_exchange(p_ref, out_ref, send_sems, recv_sems, True, True):
            cp.wait_recv()
        for cp in outs:
            cp.wait_send()

    return pl.pallas_call(
        body, name=name, in_specs=[_HBM], out_specs=_HBM,
        out_shape=jax.ShapeDtypeStruct((4, R, W), p.dtype),
        scratch_shapes=[pltpu.SemaphoreType.DMA((3,)), pltpu.SemaphoreType.DMA((3,))],
    )(p)


def _chip_sum(q, name):
    _, R, W = q.shape
    tr = _row_tile(R)

    def body(q_ref, o_ref):
        f = lambda t: q_ref[t].astype(F32)
        o_ref[...] = ((f(0) + f(1)) + f(2)) + f(3)

    return pl.pallas_call(body, name=name, grid=(R // tr,),
                          in_specs=[pl.BlockSpec((4, tr, W), lambda r: (0, r, 0))],
                          out_specs=pl.BlockSpec((tr, W), lambda r: (r, 0)),
                          out_shape=jax.ShapeDtypeStruct((R, W), F32))(q)


def _pair_share(r, name):
    R, W = r.shape

    def body(r_ref, out_ref, send_sem, recv_sem):
        x, y, c = _me()
        cp = pltpu.make_async_remote_copy(src_ref=r_ref, dst_ref=out_ref, send_sem=send_sem, recv_sem=recv_sem,
                                          device_id=(x, y, 1 - c), device_id_type=MESH)
        cp.start()
        cp.wait()

    return pl.pallas_call(
        body, name=name, in_specs=[_HBM], out_specs=_HBM,
        out_shape=jax.ShapeDtypeStruct((R, W), r.dtype),
        scratch_shapes=[pltpu.SemaphoreType.DMA(()), pltpu.SemaphoreType.DMA(())],
    )(r)


def _allreduce_small(vec):
    R, W = vec.shape

    def body(v_ref, out_ref, buf, send_sems, recv_sems):
        x, y, c = _me()
        me = 4 * x + 2 * y + c
        buf[me] = v_ref[...]
        cps = []
        for kk in range(1, 8):
            peer = (x ^ (kk >> 2), y ^ ((kk >> 1) & 1), c ^ (kk & 1))
            cp = pltpu.make_async_remote_copy(src_ref=v_ref, dst_ref=buf.at[me], send_sem=send_sems.at[kk - 1],
                                              recv_sem=recv_sems.at[kk - 1], device_id=peer, device_id_type=MESH)
            cp.start()
            cps.append(cp)
        for kk in range(1, 8):
            peer = (x ^ (kk >> 2), y ^ ((kk >> 1) & 1), c ^ (kk & 1))
            pltpu.make_async_remote_copy(src_ref=v_ref, dst_ref=buf.at[me ^ kk], send_sem=send_sems.at[kk - 1],
                                         recv_sem=recv_sems.at[kk - 1], device_id=peer, device_id_type=MESH).wait_recv()
        for cp in cps:
            cp.wait_send()
        acc = buf[0]
        for d in range(1, 8):
            acc = acc + buf[d]
        out_ref[...] = acc

    vm = pl.BlockSpec(memory_space=pltpu.VMEM)
    return pl.pallas_call(
        body, name="allreduce_small", in_specs=[vm], out_specs=vm, out_shape=jax.ShapeDtypeStruct((R, W), F32),
        scratch_shapes=[pltpu.VMEM((8, R, W), F32), pltpu.SemaphoreType.DMA((7,)), pltpu.SemaphoreType.DMA((7,))],
    )(vec)


def _pack_flat(parts, align, dtype):
    flat = jnp.concatenate([p.reshape(-1).astype(dtype) for p in parts])
    n = flat.shape[0]
    pad = (-n) % align
    return jnp.pad(flat, (0, pad)) if pad else flat


def _unpack_flat(flat, shapes):
    out, off = [], 0
    for s in shapes:
        n = math.prod(s)
        out.append(flat[off:off + n].reshape(s))
        off += n
    return out


def _full_from_shards(name, sh):
    if name in COL_SHARDED:
        return jnp.transpose(sh, (1, 0, 2)).reshape(sh.shape[1], 4 * sh.shape[2])
    return sh.reshape(4 * sh.shape[1], sh.shape[2])


def _shards_from_full(name, full):
    if name in COL_SHARDED:
        r, cc = full.shape[0], full.shape[1] // 4
        return jnp.transpose(full.reshape(r, 4, cc), (1, 0, 2)).reshape(4, r * cc)
    return full.reshape(4, -1)


def _pad_heads(w, width):
    k = w.shape[0]
    return jnp.pad(w.reshape(k, HEADS, width), ((0, 0), (0, 0), (0, HP - width))).reshape(k, HEADS * HP)


def _unpad_heads(w, width):
    k = w.shape[0]
    return w.reshape(k, HEADS, HP)[:, :, :width].reshape(k, HEADS * width)


def _blockdiag(t):
    g, a, b = t.shape
    return jnp.einsum("gab,gk->gakb", t, jnp.eye(g, dtype=t.dtype)).reshape(g * a, g * b)


def _blockdiag_t(m, a, b):
    g = m.shape[0] // a
    return jnp.einsum("gagb->gab", m.reshape(g, a, g, b))


def _local_step(x, mem, positions, target, W, sp, overlap=None):
    T = x.shape[0]
    row = lambda v: v.reshape(1, -1)

    ab_re, ab_im, bb_re, bb_im = _s5_discretize(sp["s5_lam_re"], sp["s5_lam_im"], sp["s5_log_dt"], sp["s5_b_re"], sp["s5_b_im"])
    ar, ai = ab_re.reshape(1, S5_N), ab_im.reshape(1, S5_N)
    tab_fwd, tab_rev = _s5_tables(ar, ai, False), _s5_tables(ar, -ai, True)
    b_st = jnp.concatenate([_blockdiag(jnp.swapaxes(bb_re, 1, 2)), _blockdiag(jnp.swapaxes(bb_im, 1, 2))],
                           axis=1).astype(BF16)
    c_st = jnp.concatenate([_blockdiag(jnp.swapaxes(sp["s5_c_re"], 1, 2)),
                            -_blockdiag(jnp.swapaxes(sp["s5_c_im"], 1, 2))], axis=0).astype(BF16)
    inv = ROPE_THETA ** (-jnp.arange(0, ROPE, 2, dtype=F32) / ROPE)
    ang = positions.astype(F32)[:, None] * inv
    cs, sn = jnp.cos(ang), jnp.sin(ang)
    one, zero = jnp.ones((T, NOPE), F32), jnp.zeros((T, NOPE), F32)
    z16, tail1, tail0 = jnp.zeros((T, ROPE // 2), F32), jnp.ones((T, HP - QK), F32), jnp.zeros((T, HP - QK), F32)
    t_cos = jnp.concatenate([one, cs, cs, tail1], axis=1)
    t_s1 = jnp.concatenate([zero, z16, sn, tail0], axis=1)
    t_s2 = jnp.concatenate([zero, -sn, z16, tail0], axis=1)

    w_in = W["w_in"]
    w_in_re = jnp.concatenate([w_in[:, 800:], w_in[:, :768], w_in[:, 768:800],
                               jnp.zeros((D_MODEL, HP - ROPE), w_in.dtype)], axis=1)
    w_uq_p = _pad_heads(W["w_uq"], QK)
    wkv = W["w_ukv"].reshape(S5_W, HEADS, NOPE + VD)
    w_uk_p = _pad_heads(wkv[:, :, :NOPE].reshape(S5_W, HEADS * NOPE), NOPE)
    w_uv_p = _pad_heads(wkv[:, :, NOPE:].reshape(S5_W, HEADS * VD), VD)
    w_oa_p = jnp.pad(W["w_oa"].reshape(HEADS, VD, D_MODEL), ((0, 0), (0, HP - VD), (0, 0))).reshape(HEADS * HP, D_MODEL)

    g_in, b_in = row(sp["ln_in_g"]), row(sp["ln_in_b"])
    g1, b1, g2, b2, g3, b3 = (row(sp[k]) for k in ("ln1_g", "ln1_b", "ln2_g", "ln2_b", "ln3_g", "ln3_b"))
    gq, gkv, d_skip = row(sp["q_norm_g"]), row(sp["kv_norm_g"]), row(sp["s5_d"])

    (h0,) = _rowwise(lambda a, g, b: (_ln(a, g, b),), "ln_in_fwd", [_whole(x)], [g_in, b_in], [(D_MODEL, F32)], tm=512)
    def in_epi(acc, ga, gb):
        zb = acc.astype(BF16).astype(F32)
        return acc, _rms(zb[:, Z_CQ:Z_CQ + S5_W], ga), _rms(zb[:, Z_CKV:Z_CKV + S5_W], gb)

    z, cqn, ckvn = _mm(h0, w_in_re, "in_proj_norm", tm=512, tn=Z_W, epi=in_epi, pextras=(gq, gkv),
                       outs=[(Z_W, BF16), (S5_W, BF16), (S5_W, BF16)])
    hs, yc, yg = _s5_fwd(z, b_st, c_st, d_skip, tab_fwd, "s5_fwd")
    y2 = _mm(yg, W["w_glu"], "glu_proj", tn=2048, out_dtype=BF16)
    tabs = [(t_cos, 0, HP), (t_s1, 0, HP), (t_s2, 0, HP)]

    def rope_q(acc, tc_, t1, t2):
        return _rope(acc, _tile8(tc_), _tile8(t1), _tile8(t2)) * (ATT_SCALE * LOG2E)

    def rope_k(acc, krz, tc_, t1, t2):
        return acc + _tile8(_rope(pltpu.roll(krz.astype(F32), NOPE, 1), tc_, t1, t2))

    q = _mm(cqn, w_uq_p, "mla_uq_rope", epi=rope_q, rextras=tabs, out_dtype=BF16)
    k = _mm(ckvn, w_uk_p, "mla_uk_rope", epi=rope_k, rextras=[(z, Z_KR, HP)] + tabs, out_dtype=BF16)
    v = _mm(ckvn, w_uv_p, "mla_uv", out_dtype=BF16)
    if overlap is None:
        o, lse = _flash_fwd(q, k, v, "mla_attn_fwd", bq=ATT_BQ_FWD)
    else:
        o, lse, landed = _flash_fwd(q, k, v, "mla_attn_fwd", bq=ATT_BQ_FWD, carry=(overlap[0], False))
        W = {**W, **overlap[1](landed)}
    def gate_epi(acc, gs_, ga_, ya, yb):
        f = lambda t: t.astype(F32)
        return acc, _gate_mix(f(gs_), f(ga_), f(ya), f(yb), acc.astype(BF16).astype(F32))

    a_out, mixin = _mm(o, w_oa_p, "mla_oa_gate", epi=gate_epi, outs=[(D_MODEL, BF16)] * 2, tm=512,
                       rextras=[(z, Z_GS, D_MODEL), (z, Z_GA, D_MODEL), (y2, 0, D_MODEL), (y2, D_MODEL, D_MODEL)])
    def ln_epi(acc, h, g, b):
        s_ = DN_ALPHA * h + acc
        return s_, _ln(s_, g, b)

    two_rows = [(D_MODEL, F32)] * 2
    s1, h1 = _mm(mixin, W["w_o"], "mix_o_ln1", epi=ln_epi, extras=(h0,), pextras=(g1, b1), outs=two_rows, tm=512)
    xq = _mm(h1, W["w_xq"], "x_q", out_dtype=BF16)
    xk = _mm(mem, W["w_xk"], "x_k", out_dtype=BF16)
    xv = _mm(mem, W["w_xv"], "x_v", out_dtype=BF16)
    (xo,) = _rowwise(_xattn_fwd_fn, "xattn_fwd", [_whole(xq)], [xk, xv], [(D_MODEL, BF16)], tm=1024, upcast=False)
    s2, h2 = _mm(xo, W["w_xo"], "x_o_ln2", epi=ln_epi, extras=(h1,), pextras=(g2, b2), outs=two_rows, tm=512)
    a_up = _mm(h2, W["w_up"], "mlp_up", out_dtype=BF16)
    def loss_epi(acc, h, tgt, g, b):
        def f(r_, g_, b_):
            e = _ln_res(h, r_, g_, b_) - tgt
            return 0.5 * jnp.sum(jnp.mean(e * e, axis=-1))
        lv, (dr_, dg_, db_) = jax.value_and_grad(f, argnums=(0, 1, 2))(acc, g, b)
        return dr_, dg_, db_, jnp.broadcast_to(lv, (1, 128))

    row_and_sums = [(D_MODEL, F32), (D_MODEL, "sum"), (D_MODEL, "sum")]
    dr3, dg3, db3, lossv = _mm(a_up, W["w_down"], "mlp_down_loss", act=_relu2, tm=512, tk=4 * D_MODEL, epi=loss_epi,
                               extras=(h2, target), pextras=(g3, b3), outs=row_and_sums + [(128, "sum")])
    gW, gs = {}, {"ln3_g": dg3, "ln3_b": db3}

    da = _mm(dr3, W["w_down"], "mlp_down_bwd_a", "nt", epi=lambda acc, a: acc * (2.0 * jnp.maximum(a, 0.0)), extras=(a_up,),
             out_dtype=BF16)
    gW["w_down"] = _mm(a_up, dr3, "mlp_down_bwd_w", "tn", m=4 * D_MODEL, n=D_MODEL, act=_relu2, tk=2048)
    gW["w_up"] = _mm(h2, da, "mlp_up_bwd_w", "tn", m=D_MODEL, n=4 * D_MODEL, tk=2048)

    def ln_bwd_epi(acc, e, s_, g, b):
        _, vjp = jax.vjp(_ln, s_, g, b)
        return vjp(acc + DN_ALPHA * e)

    dr2, gs["ln2_g"], gs["ln2_b"] = _mm(da, W["w_up"], "mlp_up_bwd_ln2", "nt", tm=512, tk=4 * D_MODEL, epi=ln_bwd_epi,
                                        extras=(dr3, s2), pextras=(g2, b2), outs=row_and_sums)
    dxo = _mm(dr2, W["w_xo"], "x_o_bwd_a", "nt", out_dtype=BF16)
    gW["w_xo"] = _mm(xo, dr2, "x_o_bwd_w", "tn", m=D_MODEL, n=D_MODEL)
    dxq, dxk, dxv = _rowwise(_xattn_bwd_fn, "xattn_bwd", [_whole(xq), _whole(dxo)], [xk, xv], [(D_MODEL, BF16)],
                             [(xk.shape[0], D_MODEL)] * 2, tm=1024, upcast=False)
    gW["w_xq"] = _mm(h1, dxq, "x_q_bwd_w", "tn", m=D_MODEL, n=D_MODEL)
    gW["w_xk"] = _mm(mem, dxk, "x_k_bwd_w", "tn", m=D_MODEL, n=D_MODEL)
    gW["w_xv"] = _mm(mem, dxv, "x_v_bwd_w", "tn", m=D_MODEL, n=D_MODEL)

    dr1, gs["ln1_g"], gs["ln1_b"] = _mm(dxq, W["w_xq"], "x_q_bwd_ln1", "nt", tm=512, epi=ln_bwd_epi,
                                        extras=(dr2, s1), pextras=(g1, b1), outs=row_and_sums)
    gW["w_o"] = _mm(mixin, dr1, "mix_o_bwd_w", "tn", m=D_MODEL, n=D_MODEL)

    def gate_bwd_epi(acc, *tiles):
        _, vjp = jax.vjp(_gate_mix, *[t.astype(F32) for t in tiles])
        return vjp(acc)

    dgs, dga, dy2a, dy2b, da_out = _mm(
        dr1, W["w_o"], "mix_o_bwd_gate", "nt", tm=512, epi=gate_bwd_epi, outs=[(D_MODEL, BF16)] * 5,
        rextras=[(z, Z_GS, D_MODEL), (z, Z_GA, D_MODEL), (y2, 0, D_MODEL), (y2, D_MODEL, D_MODEL), _whole(a_out)])

    def delta_epi(acc, o_):
        prod = acc * o_
        cols = [jnp.sum(prod[:, h * HP:(h + 1) * HP], axis=1, keepdims=True) for h in range(HEADS)]
        return acc, jnp.concatenate(cols, axis=1)

    do16, delta = _mm(da_out, w_oa_p, "mla_oa_bwd_a", "nt", epi=delta_epi, extras=(o,),
                      outs=[(HEADS * HP, BF16), (HEADS, F32)], tm=512)
    g_oa_p = _mm(o, da_out, "mla_oa_bwd_w", "tn", m=HEADS * HP, n=D_MODEL)
    gW["w_oa"] = g_oa_p.reshape(HEADS, HP, D_MODEL)[:, :VD].reshape(HEADS * VD, D_MODEL)
    delta_row = delta.T.reshape(HEADS, 1, T)
    if overlap is None:
        dq, dk, dv = _flash_bwd(q, k, v, do16, lse, delta_row, "mla_attn_bwd")
        riding = None
    else:
        sent = overlap[2]({n: gW[n] for n in BIG_LATE})
        dq, dk, dv, landed = _flash_bwd(q, k, v, do16, lse, delta_row, "mla_attn_bwd", carry=(sent, True))
        riding = (sent, landed)

    def rope_bwd(dq_, dk_, tc_, t1, t2):
        dqr = _rope_t(dq_, _tile8(tc_), _tile8(t1), _tile8(t2))
        dkr = dk_[:, 0:HP]
        for hh in range(1, HEADS):
            dkr = dkr + dk_[:, hh * HP:(hh + 1) * HP]
        dkr = _rope_t(jnp.where(_rope_lanes(dkr.shape), dkr, 0.0), tc_, t1, t2)
        dkr = pltpu.roll(dkr, NOPE, 1)
        lane = lax.broadcasted_iota(jnp.int32, dkr.shape, 1)
        return dqr, jnp.where(lane < ROPE, dkr, 0.0)

    dq_raw, dkr = _rowwise(rope_bwd, "mla_rope_bwd", [_whole(dq), _whole(dk), _whole(t_cos), _whole(t_s1), _whole(t_s2)], [],
                           [(HEADS * HP, BF16), (HP, BF16)])
    def rms_bwd_epi(acc, *rest):
        _, vjp = jax.vjp(_rms, rest[-2].astype(F32), rest[-1])
        return vjp(acc + rest[0] if len(rest) == 3 else acc)

    row_and_sum = [(S5_W, BF16), (S5_W, "sum")]
    dcq, gs["q_norm_g"] = _mm(dq_raw, w_uq_p, "mla_uq_bwd_norm", "nt", epi=rms_bwd_epi, rextras=[(z, Z_CQ, S5_W)],
                              pextras=(gq,), outs=row_and_sum)
    gW["w_uq"] = _unpad_heads(_mm(cqn, dq_raw, "mla_uq_bwd_w", "tn", m=S5_W, n=HEADS * HP), QK)
    dckvn_k = _mm(dk, w_uk_p, "mla_uk_bwd_a", "nt")
    dckv, gs["kv_norm_g"] = _mm(dv, w_uv_p, "mla_uv_bwd_norm", "nt", epi=rms_bwd_epi, extras=(dckvn_k,),
                                rextras=[(z, Z_CKV, S5_W)], pextras=(gkv,), outs=row_and_sum)
    g_uk = _unpad_heads(_mm(ckvn, dk, "mla_uk_bwd_w", "tn", m=S5_W, n=HEADS * HP), NOPE).reshape(S5_W, HEADS, NOPE)
    g_uv = _unpad_heads(_mm(ckvn, dv, "mla_uv_bwd_w", "tn", m=S5_W, n=HEADS * HP), VD).reshape(S5_W, HEADS, VD)
    gW["w_ukv"] = jnp.concatenate([g_uk, g_uv], axis=2).reshape(S5_W, HEADS * (NOPE + VD))

    dy2 = jnp.concatenate([dy2a, dy2b], axis=1)
    dyg = _mm(dy2, W["w_glu"], "glu_bwd_a", "nt")
    gW["w_glu"] = _mm(yg, dy2, "glu_bwd_w", "tn", m=S5_W, n=2 * D_MODEL)

    du, gs["s5_ab"], g_bst, g_cst_t, gs["s5_d"] = _s5_bwd(dyg, yc, z, hs, b_st, c_st, d_skip, tab_rev, "s5_bwd")
    gs["s5_bb_re"] = jnp.swapaxes(_blockdiag_t(g_bst[:, :S5_N], S5_H, S5_P), 1, 2)
    gs["s5_bb_im"] = jnp.swapaxes(_blockdiag_t(g_bst[:, S5_N:], S5_H, S5_P), 1, 2)
    gs["s5_c_re"] = _blockdiag_t(g_cst_t[:, :S5_N], S5_H, S5_P)
    gs["s5_c_im"] = -_blockdiag_t(g_cst_t[:, S5_N:], S5_H, S5_P)

    dz = jnp.concatenate([dgs, dga, du, dcq, dckv, dkr], axis=1)
    g_in_re = _mm(h0, dz, "in_proj_bwd_w", "tn", m=D_MODEL, n=Z_W, tm=512, tn=Z_W)
    gW["w_in"] = jnp.concatenate([g_in_re[:, Z_U:Z_KR], g_in_re[:, Z_KR:Z_KR + ROPE], g_in_re[:, :Z_U]], axis=1)
    dx, gs["ln_in_g"], gs["ln_in_b"] = _mm(dz, w_in_re, "in_proj_bwd_ln_in", "nt", tm=512, tk=Z_W, epi=ln_bwd_epi,
                                           extras=(dr1, x), pextras=(g_in, b_in), outs=row_and_sums)
    return lossv[:, :1], dx, gW, gs, riding


_RAW_SMALL = (("loss", (1, 1)), ("ln_in_g", (1, D_MODEL)), ("ln_in_b", (1, D_MODEL)), ("ln1_g", (1, D_MODEL)),
              ("ln1_b", (1, D_MODEL)), ("ln2_g", (1, D_MODEL)), ("ln2_b", (1, D_MODEL)), ("ln3_g", (1, D_MODEL)),
              ("ln3_b", (1, D_MODEL)), ("q_norm_g", (1, S5_W)), ("kv_norm_g", (1, S5_W)), ("s5_d", (1, S5_W)),
              ("s5_ab", (1, 2 * S5_N)), ("s5_bb_re", (S5_G, S5_P, S5_H)), ("s5_bb_im", (S5_G, S5_P, S5_H)),
              ("s5_c_re", (S5_G, S5_H, S5_P)), ("s5_c_im", (S5_G, S5_H, S5_P)))


def kernel(x, mem, positions, ln_in_g, ln_in_b, w_in, s5_lam_re, s5_lam_im, s5_log_dt, s5_b_re, s5_b_im, s5_c_re, s5_c_im, s5_d, w_glu, q_norm_g, w_uq, kv_norm_g, w_ukv, w_oa, w_o, ln1_g, ln1_b, w_xq, w_xk, w_xv, w_xo, ln2_g, ln2_b, w_up, w_down, ln3_g, ln3_b, loss_target, m_ln_in_g, m_ln_in_b, m_w_in, m_s5_lam_re, m_s5_lam_im, m_s5_log_dt, m_s5_b_re, m_s5_b_im, m_s5_c_re, m_s5_c_im, m_s5_d, m_w_glu, m_q_norm_g, m_w_uq, m_kv_norm_g, m_w_ukv, m_w_oa, m_w_o, m_ln1_g, m_ln1_b, m_w_xq, m_w_xk, m_w_xv, m_w_xo, m_ln2_g, m_ln2_b, m_w_up, m_w_down, m_ln3_g, m_ln3_b, v_ln_in_g, v_ln_in_b, v_w_in, v_s5_lam_re, v_s5_lam_im, v_s5_log_dt, v_s5_b_re, v_s5_b_im, v_s5_c_re, v_s5_c_im, v_s5_d, v_w_glu, v_q_norm_g, v_w_uq, v_kv_norm_g, v_w_ukv, v_w_oa, v_w_o, v_ln1_g, v_ln1_b, v_w_xq, v_w_xk, v_w_xv, v_w_xo, v_ln2_g, v_ln2_b, v_w_up, v_w_down, v_ln3_g, v_ln3_b):
    a = dict(locals())
    wts = {n: a[n] for n in WEIGHTS}
    ms = {n: a["m_" + n] for n in WEIGHTS}
    vs = {n: a["v_" + n] for n in WEIGHTS}
    shard2d = {n: wts[n].reshape(wts[n].shape[-2], wts[n].shape[-1]) for n in BIG}
    nrow = {n: shard2d[n].size // 1024 for n in BIG}
    mine = 2 * lax.axis_index("x") + lax.axis_index("y")
    my_c = lax.axis_index("c")
    cidx = my_c.astype(jnp.int32).reshape(1)

    def group_rows(group):
        rows = sum(nrow[n] for n in group)
        return rows, (-rows) % PACK_ROWS

    def pack_shards(group):
        rows, pad = group_rows(group)
        parts = [shard2d[n].astype(BF16).reshape(nrow[n], 1024) for n in group] + [jnp.zeros((pad, 1024), BF16)]
        return jnp.concatenate(parts, axis=0).reshape(2, (rows + pad) // 2, 1024)

    def unpack_full(group, gathered):
        out, off = {}, 0
        for n in group:
            r, cc = shard2d[n].shape
            piece = lax.optimization_barrier(gathered[:, off:off + nrow[n]])
            out[n] = _full_from_shards(n, piece.reshape(4, r, cc))
            off += nrow[n]
        return out

    def pack_grads(group, gw):
        rows, pad = group_rows(group)
        parts = [_shards_from_full(n, gw[n].astype(BF16)).reshape(4, nrow[n], 1024) for n in group]
        return jnp.concatenate(parts + [jnp.zeros((4, pad, 1024), BF16)], axis=1).reshape(4, 2, (rows + pad) // 2, 1024)

    def by_core(own, other):
        return jnp.where(my_c == 0, jnp.stack([own, other]), jnp.stack([other, own]))

    packed_e, packed_l = pack_shards(BIG_EARLY), pack_shards(BIG_LATE)
    gathered = lax.dynamic_update_slice(_gather_weights(packed_e), packed_e[None], (mine, 0, 0, 0))
    W = unpack_full(BIG_EARLY, gathered.reshape(4, -1, 1024))

    def finish_gather(landed):
        half = packed_l.shape[1]
        own_half = lax.dynamic_index_in_dim(packed_l, my_c, axis=0, keepdims=True)
        landed = lax.dynamic_update_slice(landed, own_half, (mine, 0, 0))
        other = _pair_share(landed.reshape(4 * half, 1024), "gather_late_share").reshape(4, half, 1024)
        out, off = {}, 0
        for n in BIG_LATE:
            r, cc = shard2d[n].shape
            h, lo = divmod(off, half)
            assert lo + nrow[n] <= half, n
            piece = jnp.where(my_c == h, landed[:, lo:lo + nrow[n]], other[:, lo:lo + nrow[n]])
            out[n] = _full_from_shards(n, lax.optimization_barrier(piece).reshape(4, r, cc))
            off += nrow[n]
        return out

    def pair_reduce(group, gw, tag):
        gflat = pack_grads(group, gw)
        return _pair_add(gflat, _pair_exchange(gflat, "grad_pair_exchange_" + tag), cidx, "grad_pair_add_" + tag)

    def finish_reduce(group, sent, landed, tag):
        own = lax.dynamic_slice(sent, (mine, 0, 0), (1,) + sent.shape[1:])
        red = _chip_sum(lax.dynamic_update_slice(landed, own, (mine, 0, 0)), "grad_chip_sum_" + tag)
        gsh = by_core(red, _pair_share(red, "grad_pair_share_" + tag)).reshape(-1, 1024)
        out, off = {}, 0
        for n in group:
            out[n] = lax.optimization_barrier(gsh[off:off + nrow[n]]).reshape(shard2d[n].shape)
            off += nrow[n]
        return out

    sp = {n: wts[n] for n in SMALL}
    sp_local = {n: (sp[n][0] if sp[n].ndim > 1 else sp[n]) for n in SMALL}
    overlap = (packed_l, finish_gather, lambda gw: pair_reduce(BIG_LATE, gw, "late"))
    lossv, dx, gW, gs, riding = _local_step(x[0], mem[0], positions[0], loss_target[0], W, sp_local, overlap)

    g_out = finish_reduce(BIG_LATE, riding[0], riding[1], "late")
    sent_e = pair_reduce(BIG_EARLY, gW, "early")
    g_out.update(finish_reduce(BIG_EARLY, sent_e, _chip_exchange(sent_e, "grad_chip_exchange_early"), "early"))

    gs["loss"] = lossv
    raw = _pack_flat([gs[n].reshape(s) for n, s in _RAW_SMALL], 8 * 1024, F32)
    raw = _allreduce_small(raw.reshape(-1, 1024)).reshape(-1)
    rs = dict(zip([n for n, _ in _RAW_SMALL], _unpack_flat(raw, [s for _, s in _RAW_SMALL])))
    loss = rs["loss"].reshape(())
    _, disc_vjp = jax.vjp(_s5_discretize, sp_local["s5_lam_re"], sp_local["s5_lam_im"], sp_local["s5_log_dt"],
                          sp_local["s5_b_re"], sp_local["s5_b_im"])
    d_ab = rs["s5_ab"].reshape(2, S5_G, S5_P)
    g_lre, g_lim, g_ldt, g_bre, g_bim = disc_vjp((d_ab[0], d_ab[1], rs["s5_bb_re"], rs["s5_bb_im"]))
    small_g = {"s5_lam_re": g_lre, "s5_lam_im": g_lim, "s5_log_dt": g_ldt, "s5_b_re": g_bre, "s5_b_im": g_bim,
               "s5_c_re": rs["s5_c_re"], "s5_c_im": rs["s5_c_im"]}
    for n in ("ln_in_g", "ln_in_b", "ln1_g", "ln1_b", "ln2_g", "ln2_b", "ln3_g", "ln3_b", "q_norm_g", "kv_norm_g", "s5_d"):
        small_g[n] = rs[n]

    grads, deltas, new_m, new_v = {}, {}, {}, {}
    for n in BIG:
        d_, m_, v_ = _adamw(shard2d[n], g_out[n], ms[n].reshape(shard2d[n].shape), vs[n].reshape(shard2d[n].shape), "adamw_" + n)
        grads[n] = g_out[n].reshape(wts[n].shape)
        deltas[n], new_m[n], new_v[n] = (t.reshape(wts[n].shape) for t in (d_, m_, v_))
    as2d = lambda t: t.reshape(-1, t.shape[-1])
    sd, sm, sv = _adamw_many([as2d(wts[n]) for n in SMALL], [as2d(small_g[n].reshape(wts[n].shape)) for n in SMALL],
                             [as2d(ms[n]) for n in SMALL], [as2d(vs[n]) for n in SMALL], "adamw_small")
    for n, d_, m_, v_ in zip(SMALL, sd, sm, sv):
        grads[n] = small_g[n].reshape(wts[n].shape)
        deltas[n], new_m[n], new_v[n] = (t.reshape(wts[n].shape) for t in (d_, m_, v_))

    return (loss, dx[None], *[grads[n] for n in WEIGHTS], *[deltas[n] for n in WEIGHTS],
            *[new_m[n] for n in WEIGHTS], *[new_v[n] for n in WEIGHTS])
```

```python
import functools
import math

import jax
import jax.numpy as jnp
from jax import lax
from jax.experimental import pallas as pl
from jax.experimental.pallas import tpu as pltpu

F32 = jnp.float32
BF16 = jnp.bfloat16
MESH = pl.DeviceIdType.MESH

D_MODEL = 1024
S5_W = 256
S5_G = 16
S5_H = 16
S5_P = 64
S5_N = S5_G * S5_P
S5_MAX_RE = -1e-4
HEADS = 8
NOPE = 64
ROPE = 32
QK = NOPE + ROPE
VD = 64
HP = 128
XH = 4
XD = 256
LN_EPS = 1e-5
RMS_EPS = 1e-6
NEG_INF = -1e30
DN_ALPHA = 2.0 ** 0.25
ROPE_THETA = 10000.0
ADAM_LR, ADAM_B1, ADAM_B2, ADAM_EPS, ADAM_WD, ADAM_STEP = 0.001, 0.9, 0.999, 1e-08, 0.01, 10

Z_GS, Z_GA, Z_U, Z_CQ, Z_CKV, Z_KR, Z_W = 0, 1024, 2048, 2304, 2560, 2816, 2944
SCAN_CHUNK = 256

BIG_EARLY = ("w_in", "w_glu", "w_uq", "w_ukv", "w_oa", "w_o")
BIG_LATE = ("w_up", "w_xq", "w_xk", "w_xv", "w_xo", "w_down")
BIG = BIG_EARLY + BIG_LATE
COL_SHARDED = ("w_in", "w_glu", "w_uq", "w_ukv", "w_oa", "w_up")
SMALL = ("ln_in_g", "ln_in_b", "s5_lam_re", "s5_lam_im", "s5_log_dt", "s5_b_re", "s5_b_im", "s5_c_re", "s5_c_im",
         "s5_d", "q_norm_g", "kv_norm_g", "ln1_g", "ln1_b", "ln2_g", "ln2_b", "ln3_g", "ln3_b")
WEIGHTS = ("ln_in_g", "ln_in_b", "w_in", "s5_lam_re", "s5_lam_im", "s5_log_dt", "s5_b_re", "s5_b_im", "s5_c_re",
           "s5_c_im", "s5_d", "w_glu", "q_norm_g", "w_uq", "kv_norm_g", "w_ukv", "w_oa", "w_o", "ln1_g", "ln1_b",
           "w_xq", "w_xk", "w_xv", "w_xo", "ln2_g", "ln2_b", "w_up", "w_down", "ln3_g", "ln3_b")
PACK_ROWS = 2 * 16


def _row_tile(n, cap=512):
    best = n
    for t in range(8, min(n, cap) + 1, 8):
        if n % t == 0:
            best = t
    return best


def _pick(n, cap):
    best = None
    for t in range(128, min(n, cap) + 1, 128):
        if n % t == 0:
            best = t
    return best if best is not None and (best >= 512 or best == n) else n


def _mm(a, b, name, mode="nn", *, a_off=0, b_off=0, m=None, n=None, act=None, epi=None, extras=(), rextras=(),
        pextras=(), outs=None, out_dtype=None, tm=1024, tn=1024, tk=None):
    out_dtype = out_dtype or (BF16 if mode == "tn" else F32)
    if mode == "nn":
        M, (K, N) = a.shape[0], b.shape
    elif mode == "nt":
        M, (N, K) = a.shape[0], b.shape
    else:
        K, M, N = a.shape[0], m, n
    if mode == "tn":
        tm, tn, tk = _pick(M, tm), _pick(N, tn), min(tk or 1024, K)
    else:
        tm, tn, tk = min(tm, M), _pick(N, tn), _pick(K, tk or 1024)
    assert M % tm == 0 and N % tn == 0 and K % tk == 0, (name, M, N, K, tm, tn, tk)
    nk = K // tk
    if mode == "tn":
        assert a_off % tm == 0 and b_off % tn == 0
        ao, bo = a_off // tm, b_off // tn
        a_spec = pl.BlockSpec((tk, tm), lambda i, j, k: (k, i + ao))
        b_spec = pl.BlockSpec((tk, tn), lambda i, j, k: (k, j + bo))
        dims = (((0,), (0,)), ((), ()))
    else:
        assert a_off % tk == 0
        ao = a_off // tk
        a_spec = pl.BlockSpec((tm, tk), lambda i, j, k: (i, k + ao))
        if mode == "nn":
            b_spec = pl.BlockSpec((tk, tn), lambda i, j, k: (k, j))
            dims = (((1,), (0,)), ((), ()))
        else:
            b_spec = pl.BlockSpec((tn, tk), lambda i, j, k: (j, k))
            dims = (((1,), (1,)), ((), ()))
    e_spec = pl.BlockSpec((tm, tn), lambda i, j, k: (i, j))
    r_specs = [pl.BlockSpec((tm, w), functools.partial(lambda i, j, k, o: (i, o), o=off // w)) for _, off, w in rextras]
    p_specs = [pl.BlockSpec((1, tn), lambda i, j, k: (0, j)) if p.shape[1] == N else pl.BlockSpec(p.shape, lambda i, j, k: (0, 0))
               for p in pextras]
    n_extra = len(extras) + len(rextras) + len(pextras)
    if outs is None:
        o_specs, o_shapes = [e_spec], [jax.ShapeDtypeStruct((M, N), out_dtype)]
    else:
        assert all((w == tn and dt != "sum") or N == tn for w, dt in outs), name
        o_specs = [e_spec if w == tn and dt != "sum" else pl.BlockSpec((8 if dt == "sum" else tm, w), lambda i, j, k: (i, 0))
                   for w, dt in outs]
        o_shapes = [jax.ShapeDtypeStruct((M // tm * 8, w), F32) if dt == "sum" else
                    jax.ShapeDtypeStruct((M, N if w == tn else w), dt) for w, dt in outs]
    n_out = len(o_specs)
    direct = outs is None and epi is None and out_dtype == F32

    def body(*refs):
        a_ref, b_ref = refs[0], refs[1]
        e_refs = refs[2:2 + n_extra]
        o_refs = refs[2 + n_extra:2 + n_extra + n_out]
        av = a_ref[...]
        if act is not None:
            av = act(av.astype(F32))
        p = lax.dot_general(av.astype(BF16), b_ref[...].astype(BF16), dims, preferred_element_type=F32)

        def finish(r):
            if epi is not None:
                r = epi(r, *[e[...] for e in e_refs])
            for o_ref, v in zip(o_refs, r if isinstance(r, tuple) else (r,)):
                o_ref[...] = jnp.broadcast_to(v, o_ref.shape).astype(o_ref.dtype)

        if nk == 1:
            finish(p)
        else:
            acc = o_refs[0] if direct else refs[2 + n_extra + n_out]
            k = pl.program_id(2)

            @pl.when(k == 0)
            def _():
                acc[...] = p

            @pl.when(k > 0)
            def _():
                acc[...] += p

            if not direct:
                @pl.when(k == nk - 1)
                def _():
                    finish(acc[...])

    res = pl.pallas_call(
        body, name=name, grid=(M // tm, N // tn, nk),
        in_specs=[a_spec, b_spec] + [e_spec] * len(extras) + r_specs + p_specs, out_specs=o_specs,
        out_shape=o_shapes,
        scratch_shapes=[pltpu.VMEM((tm, tn), F32)] if nk > 1 and not direct else [],
        compiler_params=pltpu.CompilerParams(dimension_semantics=("parallel", "parallel", "arbitrary")),
    )(a, b, *extras, *[r[0] for r in rextras], *pextras)
    if outs is None:
        return res[0]
    return [r.reshape(M // tm, 8, -1)[:, 0].sum(axis=0, keepdims=True) if dt == "sum" else r for r, (_, dt) in zip(res, outs)]


def _rowwise(fn, name, rows, pars, outs, accs=(), tm=256, upcast=True):
    T = rows[0][0].shape[0]
    tm = min(tm, T)
    assert T % tm == 0
    n_in, n_out = len(rows) + len(pars), len(outs)
    in_specs = []
    for arr, off, w in rows:
        assert off % w == 0 and arr.shape[0] == T, name
        in_specs.append(pl.BlockSpec((tm, w), functools.partial(lambda i, o: (i, o), o=off // w)))
    for p in pars:
        in_specs.append(pl.BlockSpec(p.shape, lambda i: (0, 0)))
    out_specs = [pl.BlockSpec((tm, w), lambda i: (i, 0)) for w, _ in outs]
    out_specs += [pl.BlockSpec(s, lambda i: (0, 0)) for s in accs]
    out_shape = [jax.ShapeDtypeStruct((T, w), dt) for w, dt in outs] + [jax.ShapeDtypeStruct(s, F32) for s in accs]

    def body(*refs):
        res = fn(*[r[...].astype(F32) if upcast else r[...] for r in refs[:n_in]])
        o_refs = refs[n_in:]
        for r, v in zip(o_refs[:n_out], res[:n_out]):
            r[...] = v.astype(r.dtype)
        if accs:
            i = pl.program_id(0)

            @pl.when(i == 0)
            def _():
                for r in o_refs[n_out:]:
                    r[...] = jnp.zeros_like(r)

            for r, v in zip(o_refs[n_out:], res[n_out:]):
                r[...] += v

    res = pl.pallas_call(
        body, name=name, grid=(T // tm,), in_specs=in_specs, out_specs=out_specs, out_shape=out_shape,
        compiler_params=pltpu.CompilerParams(dimension_semantics=("arbitrary",)),
    )(*[r[0] for r in rows], *pars)
    return res


def _whole(arr):
    return (arr, 0, arr.shape[1])


def _ln(x, g, b):
    mu = jnp.mean(x, axis=-1, keepdims=True)
    xc = x - mu
    var = jnp.mean(xc * xc, axis=-1, keepdims=True)
    return xc * lax.rsqrt(var + LN_EPS) * g + b


def _ln_res(h, r, g, b):
    return _ln(DN_ALPHA * h + r, g, b)


def _rms(x, g):
    return x * lax.rsqrt(jnp.mean(x * x, axis=-1, keepdims=True) + RMS_EPS) * g


def _gelu_skip(yc, u, d):
    y = yc + d * u
    return 0.5 * y * (1.0 + lax.erf(y * (1.0 / math.sqrt(2.0))))


def _gate_mix(gs, ga, y2a, y2b, aout):
    return jax.nn.sigmoid(gs) * (y2a * jax.nn.sigmoid(y2b)) + jax.nn.sigmoid(ga) * aout


def _relu2(a):
    r = jnp.maximum(a, 0.0)
    return r * r


def _tile8(t):
    return jnp.concatenate([t] * HEADS, axis=1)


def _rope(x, cos, s1, s2):
    w = x.shape[1]
    return x * cos + pltpu.roll(x, ROPE // 2, 1) * s1 + pltpu.roll(x, w - ROPE // 2, 1) * s2


def _rope_t(dy, cos, s1, s2):
    w = dy.shape[1]
    return dy * cos + pltpu.roll(dy * s1, w - ROPE // 2, 1) + pltpu.roll(dy * s2, ROPE // 2, 1)


def _rope_lanes(shape):
    lane = lax.broadcasted_iota(jnp.int32, shape, 1) % HP
    return (lane >= NOPE) & (lane < QK)


ATT_HB_FWD, ATT_HB_BWD = 8, 8
ATT_BQ_FWD = 1024
ATT_SCALE = QK ** -0.5
LOG2E = 1.4426950408889634
LN2 = 0.6931471805599453
_NT = (((1,), (1,)), ((), ()))
_TN = (((0,), (0,)), ((), ()))


def _carry_specs(carry):
    if carry is None:
        return [], [], [], []
    arr = carry[0]
    return ([_HBM], [_HBM], [jax.ShapeDtypeStruct((4,) + arr.shape[1:], arr.dtype)],
            [pltpu.SemaphoreType.DMA((3,)), pltpu.SemaphoreType.DMA((3,))])


def _carry_start(exchange, when):
    @pl.when(when)
    def _():
        for cp in exchange(False):
            cp.start()


def _carry_wait(exchange, when):
    @pl.when(when)
    def _():
        for cp in exchange(True):
            cp.wait_recv()
        for cp in exchange(False):
            cp.wait_send()


def _tri_tables(n, by_row):
    if by_row:
        pairs = [(i, j) for i in range(n) for j in range(i + 1)]
    else:
        pairs = [(i, j) for j in range(n) for i in range(j, n)]
    return jnp.array([p[0] for p in pairs], jnp.int32), jnp.array([p[1] for p in pairs], jnp.int32)


def _flash_fwd(q, k, v, name, bq=512, carry=None):
    T = q.shape[0]
    bq = min(bq, T)
    nq = T // bq
    hb = ATT_HB_FWD
    n_carry = 0 if carry is None else 1

    i_tab, j_tab = _tri_tables(nq, by_row=True)

    def body(it_ref, jt_ref, q_ref, k_ref, v_ref, *rest):
        o_ref, lse_ref = rest[n_carry:n_carry + 2]
        m_sc, l_sc, acc_sc = rest[2 * n_carry + 2:2 * n_carry + 5]
        t = pl.program_id(1)
        i, j = it_ref[t], jt_ref[t]
        if carry is not None:
            exchange = functools.partial(_carried_exchange, rest[0], rest[3], rest[-2], rest[-1], carry[1])
            _carry_start(exchange, (pl.program_id(0) == 0) & (t == 0))

        @pl.when(j == 0)
        def _():
            m_sc[...] = jnp.full_like(m_sc, NEG_INF)
            l_sc[...] = jnp.zeros_like(l_sc)
            acc_sc[...] = jnp.zeros_like(acc_sc)

        def step(masked):
            for hh in range(hb):
                sl = slice(hh * HP, (hh + 1) * HP)
                s = lax.dot_general(q_ref[:, sl], k_ref[:, sl], _NT, preferred_element_type=F32)
                if masked:
                    row = lax.broadcasted_iota(jnp.int32, (bq, bq), 0)
                    col = lax.broadcasted_iota(jnp.int32, (bq, bq), 1)
                    s = jnp.where(col <= row, s, NEG_INF)
                m_prev = m_sc[hh]
                m_new = jnp.maximum(m_prev, jnp.max(s, axis=1, keepdims=True))
                alpha = jnp.exp2(m_prev - m_new)
                p = jnp.exp2(s - jnp.concatenate([m_new] * (bq // HP), axis=1))
                l_sc[hh] = alpha * l_sc[hh] + jnp.sum(p, axis=1, keepdims=True)
                acc_sc[hh] = alpha * acc_sc[hh] + jnp.dot(p.astype(BF16), v_ref[:, sl], preferred_element_type=F32)
                m_sc[hh] = m_new

        @pl.when(j < i)
        def _():
            step(False)

        @pl.when(j == i)
        def _():
            step(True)
            for hh in range(hb):
                o_ref[:, hh * HP:(hh + 1) * HP] = acc_sc[hh] / l_sc[hh]
                lse_ref[hh] = jnp.transpose((m_sc[hh] + jnp.log2(l_sc[hh])) * LN2)[0:1, :]

        if carry is not None:
            _carry_wait(exchange, (pl.program_id(0) == HEADS // hb - 1) & (t == i_tab.shape[0] - 1))

    qs = pl.BlockSpec((bq, hb * HP), lambda h, t, it, jt: (it[t], h))
    ks = pl.BlockSpec((bq, hb * HP), lambda h, t, it, jt: (jt[t], h))
    c_in, c_out, c_shape, c_sems = _carry_specs(carry)
    gs = pltpu.PrefetchScalarGridSpec(
        num_scalar_prefetch=2, grid=(HEADS // hb, i_tab.shape[0]), in_specs=[qs, ks, ks] + c_in,
        out_specs=[qs, pl.BlockSpec((hb, 1, bq), lambda h, t, it, jt: (h, 0, it[t]))] + c_out,
        scratch_shapes=[pltpu.VMEM((hb, bq, HP), F32)] * 3 + c_sems)
    return pl.pallas_call(
        body, name=name, grid_spec=gs,
        out_shape=[jax.ShapeDtypeStruct((T, HEADS * HP), F32), jax.ShapeDtypeStruct((HEADS, 1, T), F32)] + c_shape,
        compiler_params=pltpu.CompilerParams(dimension_semantics=("arbitrary", "arbitrary")),
    )(i_tab, j_tab, q, k, v, *([] if carry is None else [carry[0]]))


def _flash_bwd(q, k, v, do, lse_row, delta_row, name, bq=512, carry=None):
    T = q.shape[0]
    bq = min(bq, T)
    nq = T // bq
    hb = ATT_HB_BWD
    n_carry = 0 if carry is None else 1

    i_tab, j_tab = _tri_tables(nq, by_row=False)
    n_blk = i_tab.shape[0]

    def body(it_ref, jt_ref, q_ref, k_ref, v_ref, do_ref, lse_ref, dl_ref, *rest):
        dq_ref, dk_ref, dv_ref = rest[n_carry:n_carry + 3]
        t = pl.program_id(1)
        i, j = it_ref[t], jt_ref[t]
        if carry is not None:
            exchange = functools.partial(_carried_exchange, rest[0], rest[4], rest[-2], rest[-1], carry[1])
            _carry_start(exchange, (pl.program_id(0) == 0) & (t == 0))

        @pl.when(t == 0)
        def _():
            dq_ref[...] = jnp.zeros_like(dq_ref)

        @pl.when(i == j)
        def _():
            dk_ref[...] = jnp.zeros_like(dk_ref)
            dv_ref[...] = jnp.zeros_like(dv_ref)

        def step(masked):
            rows = pl.ds(pl.multiple_of(i * bq, bq), bq)
            for hh in range(hb):
                sl = slice(hh * HP, (hh + 1) * HP)
                qh, kh, doh = q_ref[:, sl], k_ref[:, sl], do_ref[:, sl]
                st = lax.dot_general(kh, qh, _NT, preferred_element_type=F32)
                pt = jnp.exp2(st - lse_ref[hh] * LOG2E)
                if masked:
                    krow = lax.broadcasted_iota(jnp.int32, (bq, bq), 0)
                    qcol = lax.broadcasted_iota(jnp.int32, (bq, bq), 1)
                    pt = jnp.where(krow <= qcol, pt, 0.0)
                dv_ref[:, sl] += jnp.dot(pt.astype(BF16), doh, preferred_element_type=F32)
                dpt = lax.dot_general(v_ref[:, sl], doh, _NT, preferred_element_type=F32)
                dst = (pt * (dpt - dl_ref[hh])).astype(BF16)
                dk_ref[:, sl] += jnp.dot(dst, qh, preferred_element_type=F32)
                dq_ref[rows, sl] += lax.dot_general(dst, kh, _TN, preferred_element_type=F32)

        @pl.when(i > j)
        def _():
            step(False)

        @pl.when(i == j)
        def _():
            step(True)

        @pl.when(i == nq - 1)
        def _():
            dk_ref[...] *= LN2

        @pl.when(t == n_blk - 1)
        def _():
            dq_ref[...] *= ATT_SCALE

        if carry is not None:
            _carry_wait(exchange, (pl.program_id(0) == HEADS // hb - 1) & (t == n_blk - 1))

    qs = pl.BlockSpec((bq, hb * HP), lambda h, t, it, jt: (it[t], h))
    ks = pl.BlockSpec((bq, hb * HP), lambda h, t, it, jt: (jt[t], h))
    rs = pl.BlockSpec((hb, 1, bq), lambda h, t, it, jt: (h, 0, it[t]))
    full = pl.BlockSpec((T, hb * HP), lambda h, t, it, jt: (0, h), pipeline_mode=pl.Buffered(1))
    c_in, c_out, c_shape, c_sems = _carry_specs(carry)
    gs = pltpu.PrefetchScalarGridSpec(num_scalar_prefetch=2, grid=(HEADS // hb, n_blk),
                                      in_specs=[qs, ks, ks, qs, rs, rs] + c_in, out_specs=[full, ks, ks] + c_out,
                                      scratch_shapes=c_sems)
    return pl.pallas_call(
        body, name=name, grid_spec=gs, out_shape=[jax.ShapeDtypeStruct((T, HEADS * HP), F32)] * 3 + c_shape,
        compiler_params=pltpu.CompilerParams(dimension_semantics=("arbitrary", "arbitrary")),
    )(i_tab, j_tab, q, k, v, do, lse_row, delta_row, *([] if carry is None else [carry[0]]))


def _xattn_heads(q, k, v):
    scale = XD ** -0.5
    ps = []
    for h in range(XH):
        sl = slice(h * XD, (h + 1) * XD)
        s = lax.dot_general(q[:, sl].astype(BF16), k[:, sl].astype(BF16), (((1,), (1,)), ((), ())),
                            preferred_element_type=F32) * scale
        e = jnp.exp(s - jnp.max(s, axis=1, keepdims=True))
        ps.append(e / jnp.sum(e, axis=1, keepdims=True))
    return ps


def _xattn_fwd_fn(q, k, v):
    ps = _xattn_heads(q, k, v)
    o = [jnp.dot(p.astype(BF16), v[:, h * XD:(h + 1) * XD].astype(BF16), preferred_element_type=F32)
         for h, p in enumerate(ps)]
    return (jnp.concatenate(o, axis=1),)


def _xattn_bwd_fn(q, do, k, v):
    scale = XD ** -0.5
    ps = _xattn_heads(q, k, v)
    dqs, dks, dvs = [], [], []
    tdims = (((0,), (0,)), ((), ()))
    for h, p in enumerate(ps):
        sl = slice(h * XD, (h + 1) * XD)
        doh = do[:, sl].astype(BF16)
        dvs.append(lax.dot_general(p.astype(BF16), doh, tdims, preferred_element_type=F32))
        dp = lax.dot_general(doh, v[:, sl].astype(BF16), (((1,), (1,)), ((), ())), preferred_element_type=F32)
        ds = (p * (dp - jnp.sum(dp * p, axis=1, keepdims=True)) * scale).astype(BF16)
        dqs.append(jnp.dot(ds, k[:, sl].astype(BF16), preferred_element_type=F32))
        dks.append(lax.dot_general(ds, q[:, sl].astype(BF16), tdims, preferred_element_type=F32))
    return jnp.concatenate(dqs, axis=1), jnp.concatenate(dks, axis=1), jnp.concatenate(dvs, axis=1)


S5_SUB = 8


def _cmul_add(xr, xi, ar, ai, sr, si):
    return xr + ar * sr - ai * si, xi + ar * si + ai * sr


def _s5_tables(ar, ai, reverse):
    pows = [(ar, ai)]
    for _ in range(S5_SUB - 1):
        pr, pi = pows[-1]
        pows.append((pr * ar - pi * ai, pr * ai + pi * ar))
    cr = jnp.concatenate([p[0] for p in pows], axis=0)
    ci = jnp.concatenate([p[1] for p in pows], axis=0)
    if reverse:
        cr, ci = cr[::-1], ci[::-1]
    t = jnp.arange(S5_SUB)[:, None]
    blocks = [jnp.concatenate([cr, ci], axis=1)]
    for s in (1, 2, 4):
        keep = (t < S5_SUB - s) if reverse else (t >= s)
        sr, si = pows[s - 1]
        blocks.append(jnp.concatenate([jnp.where(keep, sr, 0.0), jnp.where(keep, si, 0.0)], axis=1))
    return jnp.concatenate(blocks, axis=0)


def _sub_scan(xr, xi, tab_ref, reverse):
    for k, s in enumerate((1, 2, 4)):
        blk = slice(S5_SUB * (k + 1), S5_SUB * (k + 2))
        sh = S5_SUB - s if reverse else s
        xr, xi = _cmul_add(xr, xi, tab_ref[blk, :S5_N], tab_ref[blk, S5_N:], pltpu.roll(xr, sh, 0), pltpu.roll(xi, sh, 0))
    return xr, xi


def _s5_fwd(z, b_st, c_st, d_skip, tab, name, tc=1024):
    T = z.shape[0]
    tc = min(tc, T)
    nsub = tc // S5_SUB

    def body(u_ref, bst_ref, cst_ref, d_ref, tab_ref, h_ref, yc_ref, yg_ref, carry, x_sc):
        @pl.when(pl.program_id(0) == 0)
        def _():
            carry[...] = jnp.zeros_like(carry)

        u = u_ref[...].astype(F32)
        x_sc[...] = jnp.dot(u.astype(BF16), bst_ref[...], preferred_element_type=F32)

        def sub(b, c):
            r = pl.ds(pl.multiple_of(b * S5_SUB, S5_SUB), S5_SUB)
            xr, xi = _sub_scan(x_sc[r, :S5_N], x_sc[r, S5_N:], tab_ref, False)
            hr, hi = _cmul_add(xr, xi, tab_ref[0:S5_SUB, :S5_N], tab_ref[0:S5_SUB, S5_N:], c[0], c[1])
            x_sc[r, :S5_N] = hr
            x_sc[r, S5_N:] = hi
            return hr[S5_SUB - 1:S5_SUB], hi[S5_SUB - 1:S5_SUB]

        cr, ci = lax.fori_loop(0, nsub, sub, (carry[0:1, :S5_N], carry[0:1, S5_N:]))
        carry[0:1, :S5_N] = cr
        carry[0:1, S5_N:] = ci
        h = x_sc[...]
        h_ref[...] = h
        yc = jnp.dot(h.astype(BF16), cst_ref[...], preferred_element_type=F32)
        yc_ref[...] = yc
        yg_ref[...] = _gelu_skip(yc, u, d_ref[...]).astype(yg_ref.dtype)

    whole = lambda a: pl.BlockSpec(a.shape, lambda i: (0, 0))
    row = lambda w: pl.BlockSpec((tc, w), lambda i: (i, 0))
    return pl.pallas_call(
        body, name=name, grid=(T // tc,),
        in_specs=[pl.BlockSpec((tc, S5_W), lambda i: (i, Z_U // S5_W)), whole(b_st), whole(c_st), whole(d_skip), whole(tab)],
        out_specs=[row(2 * S5_N), row(S5_W), row(S5_W)],
        out_shape=[jax.ShapeDtypeStruct((T, 2 * S5_N), F32), jax.ShapeDtypeStruct((T, S5_W), F32),
                   jax.ShapeDtypeStruct((T, S5_W), BF16)],
        scratch_shapes=[pltpu.VMEM((8, 2 * S5_N), F32), pltpu.VMEM((tc, 2 * S5_N), F32)],
        compiler_params=pltpu.CompilerParams(dimension_semantics=("arbitrary",)),
    )(z, b_st, c_st, d_skip, tab)


def _s5_bwd(dyg, yc, z, h, b_st, c_st, d_skip, tab, name, tc=1024):
    T = z.shape[0]
    tc = min(tc, T)
    nc, nsub = T // tc, tc // S5_SUB

    def body(dyg_ref, yc_ref, u_ref, h_ref, hp_ref, bst_ref, cst_ref, d_ref, tab_ref,
             du_ref, da_ref, db_ref, dct_ref, dd_ref, carry, x_sc):
        g = pl.program_id(0)

        @pl.when(g == 0)
        def _():
            carry[...] = jnp.zeros_like(carry)
            for r in (da_ref, db_ref, dct_ref, dd_ref):
                r[...] = jnp.zeros_like(r)

        u = u_ref[...].astype(F32)
        _, vjp = jax.vjp(_gelu_skip, yc_ref[...], u, d_ref[...])
        dyc, du_skip, dd = vjp(dyg_ref[...])
        dd_ref[...] += dd
        dyc16 = dyc.astype(BF16)
        dct_ref[...] += lax.dot_general(dyc16, h_ref[...].astype(BF16), _TN, preferred_element_type=F32)
        x_sc[...] = lax.dot_general(dyc16, cst_ref[...], _NT, preferred_element_type=F32)
        first = jnp.where(g == nc - 1, 0.0, 1.0)
        row0 = lax.broadcasted_iota(jnp.int32, (S5_SUB, S5_N), 0) == 0

        def sub(k, c):
            cr, ci, acc_r, acc_i = c
            b = nsub - 1 - k
            r = pl.ds(pl.multiple_of(b * S5_SUB, S5_SUB), S5_SUB)
            xr, xi = _sub_scan(x_sc[r, :S5_N], x_sc[r, S5_N:], tab_ref, True)
            lr, li = _cmul_add(xr, xi, tab_ref[0:S5_SUB, :S5_N], tab_ref[0:S5_SUB, S5_N:], cr, ci)
            x_sc[r, :S5_N] = lr
            x_sc[r, S5_N:] = li
            rp = pl.ds(pl.multiple_of(jnp.maximum(b - 1, 0) * S5_SUB, S5_SUB), S5_SUB)
            last_r = jnp.where(b == 0, hp_ref[S5_SUB - 1:S5_SUB, :S5_N] * first, h_ref[rp, :S5_N][S5_SUB - 1:S5_SUB])
            last_i = jnp.where(b == 0, hp_ref[S5_SUB - 1:S5_SUB, S5_N:] * first, h_ref[rp, S5_N:][S5_SUB - 1:S5_SUB])
            pr = jnp.where(row0, last_r, pltpu.roll(h_ref[r, :S5_N], 1, 0))
            pi = jnp.where(row0, last_i, pltpu.roll(h_ref[r, S5_N:], 1, 0))
            return lr[0:1], li[0:1], acc_r + (lr * pr + li * pi), acc_i + (li * pr - lr * pi)

        zero = jnp.zeros((S5_SUB, S5_N), F32)
        cr, ci, acc_r, acc_i = lax.fori_loop(0, nsub, sub, (carry[0:1, :S5_N], carry[0:1, S5_N:], zero, zero))
        carry[0:1, :S5_N] = cr
        carry[0:1, S5_N:] = ci
        da_ref[0:1, :S5_N] += jnp.sum(acc_r, axis=0, keepdims=True)
        da_ref[0:1, S5_N:] += jnp.sum(acc_i, axis=0, keepdims=True)
        lam16 = x_sc[...].astype(BF16)
        db_ref[...] += lax.dot_general(u.astype(BF16), lam16, _TN, preferred_element_type=F32)
        du = lax.dot_general(lam16, bst_ref[...], _NT, preferred_element_type=F32) + du_skip
        du_ref[...] = du.astype(du_ref.dtype)

    whole = lambda a: pl.BlockSpec(a.shape, lambda g: (0, 0))
    row = lambda w, off=0: pl.BlockSpec((tc, w), lambda g: (nc - 1 - g, off))
    prev = pl.BlockSpec((S5_SUB, 2 * S5_N), lambda g: (jnp.maximum((nc - 1 - g) * nsub - 1, 0), 0))
    acc = lambda s: pl.BlockSpec(s, lambda g: (0, 0))
    return pl.pallas_call(
        body, name=name, grid=(nc,),
        in_specs=[row(S5_W), row(S5_W), row(S5_W, Z_U // S5_W), row(2 * S5_N), prev, whole(b_st), whole(c_st),
                  whole(d_skip), whole(tab)],
        out_specs=[row(S5_W), acc((1, 2 * S5_N)), acc((S5_W, 2 * S5_N)), acc((S5_W, 2 * S5_N)), acc((1, S5_W))],
        out_shape=[jax.ShapeDtypeStruct((T, S5_W), BF16), jax.ShapeDtypeStruct((1, 2 * S5_N), F32),
                   jax.ShapeDtypeStruct((S5_W, 2 * S5_N), F32), jax.ShapeDtypeStruct((S5_W, 2 * S5_N), F32),
                   jax.ShapeDtypeStruct((1, S5_W), F32)],
        scratch_shapes=[pltpu.VMEM((8, 2 * S5_N), F32), pltpu.VMEM((tc, 2 * S5_N), F32)],
        compiler_params=pltpu.CompilerParams(dimension_semantics=("arbitrary",)),
    )(dyg, yc, z, h, h, b_st, c_st, d_skip, tab)


def _s5_discretize(lam_re, lam_im, log_dt, b_re, b_im):
    lr = jnp.minimum(lam_re, S5_MAX_RE)
    li = lam_im
    dt = jnp.exp(log_dt)[:, None]
    mag = jnp.exp(lr * dt)
    ang = li * dt
    ab_re = mag * jnp.cos(ang)
    ab_im = mag * jnp.sin(ang)
    den = lr * lr + li * li
    nr = ab_re - 1.0
    f_re = ((nr * lr + ab_im * li) / den)[..., None]
    f_im = ((ab_im * lr - nr * li) / den)[..., None]
    return ab_re, ab_im, f_re * b_re - f_im * b_im, f_re * b_im + f_im * b_re


def _adamw_fn(w, g, m, v):
    m = ADAM_B1 * m + (1.0 - ADAM_B1) * g
    v = ADAM_B2 * v + (1.0 - ADAM_B2) * (g * g)
    m_hat = m / (1.0 - ADAM_B1 ** ADAM_STEP)
    v_hat = v / (1.0 - ADAM_B2 ** ADAM_STEP)
    delta = -ADAM_LR * (m_hat / (jnp.sqrt(v_hat) + ADAM_EPS) + ADAM_WD * w)
    return delta, m, v


def _adamw(w, g, m, v, name):
    wd = w.shape[1]
    return _rowwise(_adamw_fn, name, [_whole(w), _whole(g), _whole(m), _whole(v)], [], [(wd, F32)] * 3,
                    tm=_row_tile(w.shape[0]))


def _adamw_many(ws, gs, ms, vs, name):
    n = len(ws)

    def body(*refs):
        for i in range(n):
            d_, m_, v_ = _adamw_fn(*[refs[kk * n + i][...] for kk in range(4)])
            refs[4 * n + i][...] = d_
            refs[5 * n + i][...] = m_
            refs[6 * n + i][...] = v_

    out = pl.pallas_call(body, name=name, out_shape=[jax.ShapeDtypeStruct(w.shape, F32) for w in ws] * 3)(*ws, *gs, *ms, *vs)
    return out[:n], out[n:2 * n], out[2 * n:]


def _me():
    return lax.axis_index("x"), lax.axis_index("y"), lax.axis_index("c")


def _chip_of(j):
    return j // 2, j % 2


_HBM = pl.BlockSpec(memory_space=pl.ANY)


def _gather_weights(shard):
    _, R, W = shard.shape

    def body(s_ref, out_ref, send_sems, recv_sems):
        x, y, c = _me()
        mine = 2 * x + y
        sib = (x, y, 1 - c)
        chips = [(1 - x, y), (x, 1 - y), (1 - x, 1 - y)]

        def rcopy(kk, src, chip_idx, half, to):
            return pltpu.make_async_remote_copy(src_ref=src, dst_ref=out_ref.at[chip_idx, half],
                                                send_sem=send_sems.at[kk], recv_sem=recv_sems.at[kk],
                                                device_id=to, device_id_type=MESH)

        first = [rcopy(j, s_ref.at[c], mine, c, (*chip, c)) for j, chip in enumerate(chips)]
        for cp in first:
            cp.start()
        passed = []
        for j, (px, py) in enumerate(chips):
            src_chip = 2 * px + py
            rcopy(j, s_ref.at[c], src_chip, c, (x, y, c)).wait_recv()
            fwd = rcopy(3 + j, out_ref.at[src_chip, c], src_chip, c, sib)
            fwd.start()
            passed.append(fwd)
        for j, (px, py) in enumerate(chips):
            rcopy(3 + j, s_ref.at[c], 2 * px + py, 1 - c, (x, y, c)).wait_recv()
        for cp in first + passed:
            cp.wait_send()

    return pl.pallas_call(
        body, name="gather_weights", in_specs=[_HBM], out_specs=_HBM,
        out_shape=jax.ShapeDtypeStruct((4, 2, R, W), shard.dtype),
        scratch_shapes=[pltpu.SemaphoreType.DMA((6,)), pltpu.SemaphoreType.DMA((6,))],
    )(shard)


def _pair_exchange(g, name):
    _, _, R, W = g.shape

    def body(g_ref, out_ref, send_sem, recv_sem):
        x, y, c = _me()
        cp = pltpu.make_async_remote_copy(src_ref=g_ref.at[:, 1 - c], dst_ref=out_ref, send_sem=send_sem,
                                          recv_sem=recv_sem, device_id=(x, y, 1 - c), device_id_type=MESH)
        cp.start()
        cp.wait()

    return pl.pallas_call(
        body, name=name, in_specs=[_HBM], out_specs=_HBM,
        out_shape=jax.ShapeDtypeStruct((4, R, W), g.dtype),
        scratch_shapes=[pltpu.SemaphoreType.DMA(()), pltpu.SemaphoreType.DMA(())],
    )(g)


def _pair_add(g, recv, cidx, name):
    _, _, R, W = g.shape
    tr = _row_tile(R)

    def body(c_ref, a_ref, b_ref, o_ref):
        o_ref[...] = (a_ref[...].astype(F32) + b_ref[...].astype(F32)).astype(o_ref.dtype)

    gs = pltpu.PrefetchScalarGridSpec(
        num_scalar_prefetch=1, grid=(4, R // tr),
        in_specs=[pl.BlockSpec((None, None, tr, W), lambda j, r, c: (j, c[0], r, 0)),
                  pl.BlockSpec((None, tr, W), lambda j, r, c: (j, r, 0))],
        out_specs=pl.BlockSpec((None, tr, W), lambda j, r, c: (j, r, 0)))
    return pl.pallas_call(body, name=name, grid_spec=gs,
                          out_shape=jax.ShapeDtypeStruct((4, R, W), BF16))(cidx, g, recv)


def _carried_exchange(src_ref, dst_ref, send_sems, recv_sems, spread, incoming):
    x, y, c = _me()
    mine = 2 * x + y
    copies = []
    for j, (px, py) in enumerate([(1 - x, y), (x, 1 - y), (1 - x, 1 - y)]):
        src = src_ref.at[2 * px + py] if spread else src_ref.at[c]
        slot = 2 * px + py if incoming else mine
        copies.append(pltpu.make_async_remote_copy(src_ref=src, dst_ref=dst_ref.at[slot], send_sem=send_sems.at[j],
                                                   recv_sem=recv_sems.at[j], device_id=(px, py, c), device_id_type=MESH))
    return copies


def _chip_exchange(p, name):
    _, R, W = p.shape

    def body(p_ref, out_ref, send_sems, recv_sems):
        outs = _carried_exchange(p_ref, out_ref, send_sems, recv_sems, True, False)
        for cp in outs:
            cp.start()
        for cp in _carried_exchange(p_ref, out_ref, send_sems, recv_sems, True, True):
            cp.wait_recv()
        for cp in outs:
            cp.wait_send()

    return pl.pallas_call(
        body, name=name, in_specs=[_HBM], out_specs=_HBM,
        out_shape=jax.ShapeDtypeStruct((4, R, W), p.dtype),
        scratch_shapes=[pltpu.SemaphoreType.DMA((3,)), pltpu.SemaphoreType.DMA((3,))],
    )(p)


def _chip_sum(q, name):
    _, R, W = q.shape
    tr = _row_tile(R)

    def body(q_ref, o_ref):
        f = lambda t: q_ref[t].astype(F32)
        o_ref[...] = ((f(0) + f(1)) + f(2)) + f(3)

    return pl.pallas_call(body, name=name, grid=(R // tr,),
                          in_specs=[pl.BlockSpec((4, tr, W), lambda r: (0, r, 0))],
                          out_specs=pl.BlockSpec((tr, W), lambda r: (r, 0)),
                          out_shape=jax.ShapeDtypeStruct((R, W), F32))(q)


def _pair_share(r, name):
    R, W = r.shape

    def body(r_ref, out_ref, send_sem, recv_sem):
        x, y, c = _me()
        cp = pltpu.make_async_remote_copy(src_ref=r_ref, dst_ref=out_ref, send_sem=send_sem, recv_sem=recv_sem,
                                          device_id=(x, y, 1 - c), device_id_type=MESH)
        cp.start()
        cp.wait()

    return pl.pallas_call(
        body, name=name, in_specs=[_HBM], out_specs=_HBM,
        out_shape=jax.ShapeDtypeStruct((R, W), r.dtype),
        scratch_shapes=[pltpu.SemaphoreType.DMA(()), pltpu.SemaphoreType.DMA(())],
    )(r)


def _allreduce_small(vec):
    R, W = vec.shape

    def body(v_ref, out_ref, buf, send_sems, recv_sems):
        x, y, c = _me()
        me = 4 * x + 2 * y + c
        buf[me] = v_ref[...]
        cps = []
        for kk in range(1, 8):
            peer = (x ^ (kk >> 2), y ^ ((kk >> 1) & 1), c ^ (kk & 1))
            cp = pltpu.make_async_remote_copy(src_ref=v_ref, dst_ref=buf.at[me], send_sem=send_sems.at[kk - 1],
                                              recv_sem=recv_sems.at[kk - 1], device_id=peer, device_id_type=MESH)
            cp.start()
            cps.append(cp)
        for kk in range(1, 8):
            peer = (x ^ (kk >> 2), y ^ ((kk >> 1) & 1), c ^ (kk & 1))
            pltpu.make_async_remote_copy(src_ref=v_ref, dst_ref=buf.at[me ^ kk], send_sem=send_sems.at[kk - 1],
                                         recv_sem=recv_sems.at[kk - 1], device_id=peer, device_id_type=MESH).wait_recv()
        for cp in cps:
            cp.wait_send()
        acc = buf[0]
        for d in range(1, 8):
            acc = acc + buf[d]
        out_ref[...] = acc

    vm = pl.BlockSpec(memory_space=pltpu.VMEM)
    return pl.pallas_call(
        body, name="allreduce_small", in_specs=[vm], out_specs=vm, out_shape=jax.ShapeDtypeStruct((R, W), F32),
        scratch_shapes=[pltpu.VMEM((8, R, W), F32), pltpu.SemaphoreType.DMA((7,)), pltpu.SemaphoreType.DMA((7,))],
    )(vec)


def _pack_flat(parts, align, dtype):
    flat = jnp.concatenate([p.reshape(-1).astype(dtype) for p in parts])
    n = flat.shape[0]
    pad = (-n) % align
    return jnp.pad(flat, (0, pad)) if pad else flat


def _unpack_flat(flat, shapes):
    out, off = [], 0
    for s in shapes:
        n = math.prod(s)
        out.append(flat[off:off + n].reshape(s))
        off += n
    return out


def _full_from_shards(name, sh):
    if name in COL_SHARDED:
        return jnp.transpose(sh, (1, 0, 2)).reshape(sh.shape[1], 4 * sh.shape[2])
    return sh.reshape(4 * sh.shape[1], sh.shape[2])


def _shards_from_full(name, full):
    if name in COL_SHARDED:
        r, cc = full.shape[0], full.shape[1] // 4
        return jnp.transpose(full.reshape(r, 4, cc), (1, 0, 2)).reshape(4, r * cc)
    return full.reshape(4, -1)


def _pad_heads(w, width):
    k = w.shape[0]
    return jnp.pad(w.reshape(k, HEADS, width), ((0, 0), (0, 0), (0, HP - width))).reshape(k, HEADS * HP)


def _unpad_heads(w, width):
    k = w.shape[0]
    return w.reshape(k, HEADS, HP)[:, :, :width].reshape(k, HEADS * width)


def _blockdiag(t):
    g, a, b = t.shape
    return jnp.einsum("gab,gk->gakb", t, jnp.eye(g, dtype=t.dtype)).reshape(g * a, g * b)


def _blockdiag_t(m, a, b):
    g = m.shape[0] // a
    return jnp.einsum("gagb->gab", m.reshape(g, a, g, b))


def _local_step(x, mem, positions, target, W, sp, overlap=None):
    T = x.shape[0]
    row = lambda v: v.reshape(1, -1)

    ab_re, ab_im, bb_re, bb_im = _s5_discretize(sp["s5_lam_re"], sp["s5_lam_im"], sp["s5_log_dt"], sp["s5_b_re"], sp["s5_b_im"])
    ar, ai = ab_re.reshape(1, S5_N), ab_im.reshape(1, S5_N)
    tab_fwd, tab_rev = _s5_tables(ar, ai, False), _s5_tables(ar, -ai, True)
    b_st = jnp.concatenate([_blockdiag(jnp.swapaxes(bb_re, 1, 2)), _blockdiag(jnp.swapaxes(bb_im, 1, 2))],
                           axis=1).astype(BF16)
    c_st = jnp.concatenate([_blockdiag(jnp.swapaxes(sp["s5_c_re"], 1, 2)),
                            -_blockdiag(jnp.swapaxes(sp["s5_c_im"], 1, 2))], axis=0).astype(BF16)
    inv = ROPE_THETA ** (-jnp.arange(0, ROPE, 2, dtype=F32) / ROPE)
    ang = positions.astype(F32)[:, None] * inv
    cs, sn = jnp.cos(ang), jnp.sin(ang)
    one, zero = jnp.ones((T, NOPE), F32), jnp.zeros((T, NOPE), F32)
    z16, tail1, tail0 = jnp.zeros((T, ROPE // 2), F32), jnp.ones((T, HP - QK), F32), jnp.zeros((T, HP - QK), F32)
    t_cos = jnp.concatenate([one, cs, cs, tail1], axis=1)
    t_s1 = jnp.concatenate([zero, z16, sn, tail0], axis=1)
    t_s2 = jnp.concatenate([zero, -sn, z16, tail0], axis=1)

    w_in = W["w_in"]
    w_in_re = jnp.concatenate([w_in[:, 800:], w_in[:, :768], w_in[:, 768:800],
                               jnp.zeros((D_MODEL, HP - ROPE), w_in.dtype)], axis=1)
    w_uq_p = _pad_heads(W["w_uq"], QK)
    wkv = W["w_ukv"].reshape(S5_W, HEADS, NOPE + VD)
    w_uk_p = _pad_heads(wkv[:, :, :NOPE].reshape(S5_W, HEADS * NOPE), NOPE)
    w_uv_p = _pad_heads(wkv[:, :, NOPE:].reshape(S5_W, HEADS * VD), VD)
    w_oa_p = jnp.pad(W["w_oa"].reshape(HEADS, VD, D_MODEL), ((0, 0), (0, HP - VD), (0, 0))).reshape(HEADS * HP, D_MODEL)

    g_in, b_in = row(sp["ln_in_g"]), row(sp["ln_in_b"])
    g1, b1, g2, b2, g3, b3 = (row(sp[k]) for k in ("ln1_g", "ln1_b", "ln2_g", "ln2_b", "ln3_g", "ln3_b"))
    gq, gkv, d_skip = row(sp["q_norm_g"]), row(sp["kv_norm_g"]), row(sp["s5_d"])

    (h0,) = _rowwise(lambda a, g, b: (_ln(a, g, b),), "ln_in_fwd", [_whole(x)], [g_in, b_in], [(D_MODEL, F32)], tm=512)
    def in_epi(acc, ga, gb):
        zb = acc.astype(BF16).astype(F32)
        return acc, _rms(zb[:, Z_CQ:Z_CQ + S5_W], ga), _rms(zb[:, Z_CKV:Z_CKV + S5_W], gb)

    z, cqn, ckvn = _mm(h0, w_in_re, "in_proj_norm", tm=512, tn=Z_W, epi=in_epi, pextras=(gq, gkv),
                       outs=[(Z_W, BF16), (S5_W, BF16), (S5_W, BF16)])
    hs, yc, yg = _s5_fwd(z, b_st, c_st, d_skip, tab_fwd, "s5_fwd")
    y2 = _mm(yg, W["w_glu"], "glu_proj", tn=2048, out_dtype=BF16)
    tabs = [(t_cos, 0, HP), (t_s1, 0, HP), (t_s2, 0, HP)]

    def rope_q(acc, tc_, t1, t2):
        return _rope(acc, _tile8(tc_), _tile8(t1), _tile8(t2)) * (ATT_SCALE * LOG2E)

    def rope_k(acc, krz, tc_, t1, t2):
        return acc + _tile8(_rope(pltpu.roll(krz.astype(F32), NOPE, 1), tc_, t1, t2))

    q = _mm(cqn, w_uq_p, "mla_uq_rope", epi=rope_q, rextras=tabs, out_dtype=BF16)
    k = _mm(ckvn, w_uk_p, "mla_uk_rope", epi=rope_k, rextras=[(z, Z_KR, HP)] + tabs, out_dtype=BF16)
    v = _mm(ckvn, w_uv_p, "mla_uv", out_dtype=BF16)
    if overlap is None:
        o, lse = _flash_fwd(q, k, v, "mla_attn_fwd", bq=ATT_BQ_FWD)
    else:
        o, lse, landed = _flash_fwd(q, k, v, "mla_attn_fwd", bq=ATT_BQ_FWD, carry=(overlap[0], False))
        W = {**W, **overlap[1](landed)}
    def gate_epi(acc, gs_, ga_, ya, yb):
        f = lambda t: t.astype(F32)
        return acc, _gate_mix(f(gs_), f(ga_), f(ya), f(yb), acc.astype(BF16).astype(F32))

    a_out, mixin = _mm(o, w_oa_p, "mla_oa_gate", epi=gate_epi, outs=[(D_MODEL, BF16)] * 2, tm=512,
                       rextras=[(z, Z_GS, D_MODEL), (z, Z_GA, D_MODEL), (y2, 0, D_MODEL), (y2, D_MODEL, D_MODEL)])
    def ln_epi(acc, h, g, b):
        s_ = DN_ALPHA * h + acc
        return s_, _ln(s_, g, b)

    two_rows = [(D_MODEL, F32)] * 2
    s1, h1 = _mm(mixin, W["w_o"], "mix_o_ln1", epi=ln_epi, extras=(h0,), pextras=(g1, b1), outs=two_rows, tm=512)
    xq = _mm(h1, W["w_xq"], "x_q", out_dtype=BF16)
    xk = _mm(mem, W["w_xk"], "x_k", out_dtype=BF16)
    xv = _mm(mem, W["w_xv"], "x_v", out_dtype=BF16)
    (xo,) = _rowwise(_xattn_fwd_fn, "xattn_fwd", [_whole(xq)], [xk, xv], [(D_MODEL, BF16)], tm=2048, upcast=False)
    s2, h2 = _mm(xo, W["w_xo"], "x_o_ln2", epi=ln_epi, extras=(h1,), pextras=(g2, b2), outs=two_rows, tm=512)
    a_up = _mm(h2, W["w_up"], "mlp_up", out_dtype=BF16)
    def loss_epi(acc, h, tgt, g, b):
        def f(r_, g_, b_):
            e = _ln_res(h, r_, g_, b_) - tgt
            return 0.5 * jnp.sum(jnp.mean(e * e, axis=-1))
        lv, (dr_, dg_, db_) = jax.value_and_grad(f, argnums=(0, 1, 2))(acc, g, b)
        return dr_, dg_, db_, jnp.broadcast_to(lv, (1, 128))

    row_and_sums = [(D_MODEL, F32), (D_MODEL, "sum"), (D_MODEL, "sum")]
    dr3, dg3, db3, lossv = _mm(a_up, W["w_down"], "mlp_down_loss", act=_relu2, tm=512, tk=4 * D_MODEL, epi=loss_epi,
                               extras=(h2, target), pextras=(g3, b3), outs=row_and_sums + [(128, "sum")])
    gW, gs = {}, {"ln3_g": dg3, "ln3_b": db3}

    da = _mm(dr3, W["w_down"], "mlp_down_bwd_a", "nt", epi=lambda acc, a: acc * (2.0 * jnp.maximum(a, 0.0)), extras=(a_up,),
             out_dtype=BF16, tn=2048)
    gW["w_down"] = _mm(a_up, dr3, "mlp_down_bwd_w", "tn", m=4 * D_MODEL, n=D_MODEL, act=_relu2, tk=2048)
    gW["w_up"] = _mm(h2, da, "mlp_up_bwd_w", "tn", m=D_MODEL, n=4 * D_MODEL, tk=2048)

    def ln_bwd_epi(acc, e, s_, g, b):
        _, vjp = jax.vjp(_ln, s_, g, b)
        return vjp(acc + DN_ALPHA * e)

    dr2, gs["ln2_g"], gs["ln2_b"] = _mm(da, W["w_up"], "mlp_up_bwd_ln2", "nt", tm=512, tk=4 * D_MODEL, epi=ln_bwd_epi,
                                        extras=(dr3, s2), pextras=(g2, b2), outs=row_and_sums)
    dxo = _mm(dr2, W["w_xo"], "x_o_bwd_a", "nt", out_dtype=BF16)
    gW["w_xo"] = _mm(xo, dr2, "x_o_bwd_w", "tn", m=D_MODEL, n=D_MODEL)
    dxq, dxk, dxv = _rowwise(_xattn_bwd_fn, "xattn_bwd", [_whole(xq), _whole(dxo)], [xk, xv], [(D_MODEL, BF16)],
                             [(xk.shape[0], D_MODEL)] * 2, tm=2048, upcast=False)
    gW["w_xq"] = _mm(h1, dxq, "x_q_bwd_w", "tn", m=D_MODEL, n=D_MODEL)
    gW["w_xk"] = _mm(mem, dxk, "x_k_bwd_w", "tn", m=D_MODEL, n=D_MODEL)
    gW["w_xv"] = _mm(mem, dxv, "x_v_bwd_w", "tn", m=D_MODEL, n=D_MODEL)

    dr1, gs["ln1_g"], gs["ln1_b"] = _mm(dxq, W["w_xq"], "x_q_bwd_ln1", "nt", tm=512, epi=ln_bwd_epi,
                                        extras=(dr2, s1), pextras=(g1, b1), outs=row_and_sums)
    gW["w_o"] = _mm(mixin, dr1, "mix_o_bwd_w", "tn", m=D_MODEL, n=D_MODEL)

    def gate_bwd_epi(acc, *tiles):
        _, vjp = jax.vjp(_gate_mix, *[t.astype(F32) for t in tiles])
        return vjp(acc)

    dgs, dga, dy2a, dy2b, da_out = _mm(
        dr1, W["w_o"], "mix_o_bwd_gate", "nt", tm=512, epi=gate_bwd_epi, outs=[(D_MODEL, BF16)] * 5,
        rextras=[(z, Z_GS, D_MODEL), (z, Z_GA, D_MODEL), (y2, 0, D_MODEL), (y2, D_MODEL, D_MODEL), _whole(a_out)])

    def delta_epi(acc, o_):
        prod = acc * o_
        cols = [jnp.sum(prod[:, h * HP:(h + 1) * HP], axis=1, keepdims=True) for h in range(HEADS)]
        return acc, jnp.concatenate(cols, axis=1)

    do16, delta = _mm(da_out, w_oa_p, "mla_oa_bwd_a", "nt", epi=delta_epi, extras=(o,),
                      outs=[(HEADS * HP, BF16), (HEADS, F32)], tm=512)
    g_oa_p = _mm(o, da_out, "mla_oa_bwd_w", "tn", m=HEADS * HP, n=D_MODEL)
    gW["w_oa"] = g_oa_p.reshape(HEADS, HP, D_MODEL)[:, :VD].reshape(HEADS * VD, D_MODEL)
    delta_row = delta.T.reshape(HEADS, 1, T)
    if overlap is None:
        dq, dk, dv = _flash_bwd(q, k, v, do16, lse, delta_row, "mla_attn_bwd")
        riding = None
    else:
        sent = overlap[2]({n: gW[n] for n in BIG_LATE})
        dq, dk, dv, landed = _flash_bwd(q, k, v, do16, lse, delta_row, "mla_attn_bwd", carry=(sent, True))
        riding = (sent, landed)

    def rope_bwd(dq_, dk_, tc_, t1, t2):
        dqr = _rope_t(dq_, _tile8(tc_), _tile8(t1), _tile8(t2))
        dkr = dk_[:, 0:HP]
        for hh in range(1, HEADS):
            dkr = dkr + dk_[:, hh * HP:(hh + 1) * HP]
        dkr = _rope_t(jnp.where(_rope_lanes(dkr.shape), dkr, 0.0), tc_, t1, t2)
        dkr = pltpu.roll(dkr, NOPE, 1)
        lane = lax.broadcasted_iota(jnp.int32, dkr.shape, 1)
        return dqr, jnp.where(lane < ROPE, dkr, 0.0)

    dq_raw, dkr = _rowwise(rope_bwd, "mla_rope_bwd", [_whole(dq), _whole(dk), _whole(t_cos), _whole(t_s1), _whole(t_s2)], [],
                           [(HEADS * HP, BF16), (HP, BF16)], tm=512)

    def rms_bwd_epi(acc, *rest):
        _, vjp = jax.vjp(_rms, rest[-2].astype(F32), rest[-1])
        return vjp(acc + rest[0] if len(rest) == 3 else acc)

    row_and_sum = [(S5_W, BF16), (S5_W, "sum")]
    dcq, gs["q_norm_g"] = _mm(dq_raw, w_uq_p, "mla_uq_bwd_norm", "nt", epi=rms_bwd_epi, rextras=[(z, Z_CQ, S5_W)],
                              pextras=(gq,), outs=row_and_sum)
    gW["w_uq"] = _unpad_heads(_mm(cqn, dq_raw, "mla_uq_bwd_w", "tn", m=S5_W, n=HEADS * HP), QK)
    dckvn_k = _mm(dk, w_uk_p, "mla_uk_bwd_a", "nt")
    dckv, gs["kv_norm_g"] = _mm(dv, w_uv_p, "mla_uv_bwd_norm", "nt", epi=rms_bwd_epi, extras=(dckvn_k,),
                                rextras=[(z, Z_CKV, S5_W)], pextras=(gkv,), outs=row_and_sum)
    g_uk = _unpad_heads(_mm(ckvn, dk, "mla_uk_bwd_w", "tn", m=S5_W, n=HEADS * HP), NOPE).reshape(S5_W, HEADS, NOPE)
    g_uv = _unpad_heads(_mm(ckvn, dv, "mla_uv_bwd_w", "tn", m=S5_W, n=HEADS * HP), VD).reshape(S5_W, HEADS, VD)
    gW["w_ukv"] = jnp.concatenate([g_uk, g_uv], axis=2).reshape(S5_W, HEADS * (NOPE + VD))

    dy2 = jnp.concatenate([dy2a, dy2b], axis=1)
    dyg = _mm(dy2, W["w_glu"], "glu_bwd_a", "nt")
    gW["w_glu"] = _mm(yg, dy2, "glu_bwd_w", "tn", m=S5_W, n=2 * D_MODEL)

    du, gs["s5_ab"], g_bst, g_cst_t, gs["s5_d"] = _s5_bwd(dyg, yc, z, hs, b_st, c_st, d_skip, tab_rev, "s5_bwd")
    gs["s5_bb_re"] = jnp.swapaxes(_blockdiag_t(g_bst[:, :S5_N], S5_H, S5_P), 1, 2)
    gs["s5_bb_im"] = jnp.swapaxes(_blockdiag_t(g_bst[:, S5_N:], S5_H, S5_P), 1, 2)
    gs["s5_c_re"] = _blockdiag_t(g_cst_t[:, :S5_N], S5_H, S5_P)
    gs["s5_c_im"] = -_blockdiag_t(g_cst_t[:, S5_N:], S5_H, S5_P)

    dz = jnp.concatenate([dgs, dga, du, dcq, dckv, dkr], axis=1)
    g_in_re = _mm(h0, dz, "in_proj_bwd_w", "tn", m=D_MODEL, n=Z_W, tm=512, tn=Z_W)
    gW["w_in"] = jnp.concatenate([g_in_re[:, Z_U:Z_KR], g_in_re[:, Z_KR:Z_KR + ROPE], g_in_re[:, :Z_U]], axis=1)
    dx, gs["ln_in_g"], gs["ln_in_b"] = _mm(dz, w_in_re, "in_proj_bwd_ln_in", "nt", tm=512, tk=Z_W, epi=ln_bwd_epi,
                                           extras=(dr1, x), pextras=(g_in, b_in), outs=row_and_sums)
    return lossv[:, :1], dx, gW, gs, riding


_RAW_SMALL = (("loss", (1, 1)), ("ln_in_g", (1, D_MODEL)), ("ln_in_b", (1, D_MODEL)), ("ln1_g", (1, D_MODEL)),
              ("ln1_b", (1, D_MODEL)), ("ln2_g", (1, D_MODEL)), ("ln2_b", (1, D_MODEL)), ("ln3_g", (1, D_MODEL)),
              ("ln3_b", (1, D_MODEL)), ("q_norm_g", (1, S5_W)), ("kv_norm_g", (1, S5_W)), ("s5_d", (1, S5_W)),
              ("s5_ab", (1, 2 * S5_N)), ("s5_bb_re", (S5_G, S5_P, S5_H)), ("s5_bb_im", (S5_G, S5_P, S5_H)),
              ("s5_c_re", (S5_G, S5_H, S5_P)), ("s5_c_im", (S5_G, S5_H, S5_P)))


def kernel(x, mem, positions, ln_in_g, ln_in_b, w_in, s5_lam_re, s5_lam_im, s5_log_dt, s5_b_re, s5_b_im, s5_c_re, s5_c_im, s5_d, w_glu, q_norm_g, w_uq, kv_norm_g, w_ukv, w_oa, w_o, ln1_g, ln1_b, w_xq, w_xk, w_xv, w_xo, ln2_g, ln2_b, w_up, w_down, ln3_g, ln3_b, loss_target, m_ln_in_g, m_ln_in_b, m_w_in, m_s5_lam_re, m_s5_lam_im, m_s5_log_dt, m_s5_b_re, m_s5_b_im, m_s5_c_re, m_s5_c_im, m_s5_d, m_w_glu, m_q_norm_g, m_w_uq, m_kv_norm_g, m_w_ukv, m_w_oa, m_w_o, m_ln1_g, m_ln1_b, m_w_xq, m_w_xk, m_w_xv, m_w_xo, m_ln2_g, m_ln2_b, m_w_up, m_w_down, m_ln3_g, m_ln3_b, v_ln_in_g, v_ln_in_b, v_w_in, v_s5_lam_re, v_s5_lam_im, v_s5_log_dt, v_s5_b_re, v_s5_b_im, v_s5_c_re, v_s5_c_im, v_s5_d, v_w_glu, v_q_norm_g, v_w_uq, v_kv_norm_g, v_w_ukv, v_w_oa, v_w_o, v_ln1_g, v_ln1_b, v_w_xq, v_w_xk, v_w_xv, v_w_xo, v_ln2_g, v_ln2_b, v_w_up, v_w_down, v_ln3_g, v_ln3_b):
    a = dict(locals())
    wts = {n: a[n] for n in WEIGHTS}
    ms = {n: a["m_" + n] for n in WEIGHTS}
    vs = {n: a["v_" + n] for n in WEIGHTS}
    shard2d = {n: wts[n].reshape(wts[n].shape[-2], wts[n].shape[-1]) for n in BIG}
    nrow = {n: shard2d[n].size // 1024 for n in BIG}
    mine = 2 * lax.axis_index("x") + lax.axis_index("y")
    my_c = lax.axis_index("c")
    cidx = my_c.astype(jnp.int32).reshape(1)

    def group_rows(group):
        rows = sum(nrow[n] for n in group)
        return rows, (-rows) % PACK_ROWS

    def pack_shards(group):
        rows, pad = group_rows(group)
        parts = [shard2d[n].astype(BF16).reshape(nrow[n], 1024) for n in group] + [jnp.zeros((pad, 1024), BF16)]
        return jnp.concatenate(parts, axis=0).reshape(2, (rows + pad) // 2, 1024)

    def unpack_full(group, gathered):
        out, off = {}, 0
        for n in group:
            r, cc = shard2d[n].shape
            piece = lax.optimization_barrier(gathered[:, off:off + nrow[n]])
            out[n] = _full_from_shards(n, piece.reshape(4, r, cc))
            off += nrow[n]
        return out

    def pack_grads(group, gw):
        rows, pad = group_rows(group)
        parts = [_shards_from_full(n, gw[n].astype(BF16)).reshape(4, nrow[n], 1024) for n in group]
        return jnp.concatenate(parts + [jnp.zeros((4, pad, 1024), BF16)], axis=1).reshape(4, 2, (rows + pad) // 2, 1024)

    def by_core(own, other):
        return jnp.where(my_c == 0, jnp.stack([own, other]), jnp.stack([other, own]))

    packed_e, packed_l = pack_shards(BIG_EARLY), pack_shards(BIG_LATE)
    gathered = lax.dynamic_update_slice(_gather_weights(packed_e), packed_e[None], (mine, 0, 0, 0))
    W = unpack_full(BIG_EARLY, gathered.reshape(4, -1, 1024))

    def finish_gather(landed):
        half = packed_l.shape[1]
        own_half = lax.dynamic_index_in_dim(packed_l, my_c, axis=0, keepdims=True)
        landed = lax.dynamic_update_slice(landed, own_half, (mine, 0, 0))
        other = _pair_share(landed.reshape(4 * half, 1024), "gather_late_share").reshape(4, half, 1024)
        out, off = {}, 0
        for n in BIG_LATE:
            r, cc = shard2d[n].shape
            h, lo = divmod(off, half)
            assert lo + nrow[n] <= half, n
            piece = jnp.where(my_c == h, landed[:, lo:lo + nrow[n]], other[:, lo:lo + nrow[n]])
            out[n] = _full_from_shards(n, lax.optimization_barrier(piece).reshape(4, r, cc))
            off += nrow[n]
        return out

    def pair_reduce(group, gw, tag):
        gflat = pack_grads(group, gw)
        return _pair_add(gflat, _pair_exchange(gflat, "grad_pair_exchange_" + tag), cidx, "grad_pair_add_" + tag)

    def finish_reduce(group, sent, landed, tag):
        own = lax.dynamic_slice(sent, (mine, 0, 0), (1,) + sent.shape[1:])
        red = _chip_sum(lax.dynamic_update_slice(landed, own, (mine, 0, 0)), "grad_chip_sum_" + tag)
        gsh = by_core(red, _pair_share(red, "grad_pair_share_" + tag)).reshape(-1, 1024)
        out, off = {}, 0
        for n in group:
            out[n] = lax.optimization_barrier(gsh[off:off + nrow[n]]).reshape(shard2d[n].shape)
            off += nrow[n]
        return out

    sp = {n: wts[n] for n in SMALL}
    sp_local = {n: (sp[n][0] if sp[n].ndim > 1 else sp[n]) for n in SMALL}
    overlap = (packed_l, finish_gather, lambda gw: pair_reduce(BIG_LATE, gw, "late"))
    lossv, dx, gW, gs, riding = _local_step(x[0], mem[0], positions[0], loss_target[0], W, sp_local, overlap)

    g_out = finish_reduce(BIG_LATE, riding[0], riding[1], "late")
    sent_e = pair_reduce(BIG_EARLY, gW, "early")
    g_out.update(finish_reduce(BIG_EARLY, sent_e, _chip_exchange(sent_e, "grad_chip_exchange_early"), "early"))

    gs["loss"] = lossv
    raw = _pack_flat([gs[n].reshape(s) for n, s in _RAW_SMALL], 8 * 1024, F32)
    raw = _allreduce_small(raw.reshape(-1, 1024)).reshape(-1)
    rs = dict(zip([n for n, _ in _RAW_SMALL], _unpack_flat(raw, [s for _, s in _RAW_SMALL])))
    loss = rs["loss"].reshape(())
    _, disc_vjp = jax.vjp(_s5_discretize, sp_local["s5_lam_re"], sp_local["s5_lam_im"], sp_local["s5_log_dt"],
                          sp_local["s5_b_re"], sp_local["s5_b_im"])
    d_ab = rs["s5_ab"].reshape(2, S5_G, S5_P)
    g_lre, g_lim, g_ldt, g_bre, g_bim = disc_vjp((d_ab[0], d_ab[1], rs["s5_bb_re"], rs["s5_bb_im"]))
    small_g = {"s5_lam_re": g_lre, "s5_lam_im": g_lim, "s5_log_dt": g_ldt, "s5_b_re": g_bre, "s5_b_im": g_bim,
               "s5_c_re": rs["s5_c_re"], "s5_c_im": rs["s5_c_im"]}
    for n in ("ln_in_g", "ln_in_b", "ln1_g", "ln1_b", "ln2_g", "ln2_b", "ln3_g", "ln3_b", "q_norm_g", "kv_norm_g", "s5_d"):
        small_g[n] = rs[n]

    grads, deltas, new_m, new_v = {}, {}, {}, {}
    for n in BIG:
        d_, m_, v_ = _adamw(shard2d[n], g_out[n], ms[n].reshape(shard2d[n].shape), vs[n].reshape(shard2d[n].shape), "adamw_" + n)
        grads[n] = g_out[n].reshape(wts[n].shape)
        deltas[n], new_m[n], new_v[n] = (t.reshape(wts[n].shape) for t in (d_, m_, v_))
    as2d = lambda t: t.reshape(-1, t.shape[-1])
    sd, sm, sv = _adamw_many([as2d(wts[n]) for n in SMALL], [as2d(small_g[n].reshape(wts[n].shape)) for n in SMALL],
                             [as2d(ms[n]) for n in SMALL], [as2d(vs[n]) for n in SMALL], "adamw_small")
    for n, d_, m_, v_ in zip(SMALL, sd, sm, sv):
        grads[n] = small_g[n].reshape(wts[n].shape)
        deltas[n], new_m[n], new_v[n] = (t.reshape(wts[n].shape) for t in (d_, m_, v_))

    return (loss, dx[None], *[grads[n] for n in WEIGHTS], *[deltas[n] for n in WEIGHTS],
            *[new_m[n] for n in WEIGHTS], *[new_v[n] for n in WEIGHTS])
```

```python
import functools
import math

import jax
import jax.numpy as jnp
from jax import lax
from jax.experimental import pallas as pl
from jax.experimental.pallas import tpu as pltpu

F32 = jnp.float32
BF16 = jnp.bfloat16
MESH = pl.DeviceIdType.MESH

D_MODEL = 1024
S5_W = 256
S5_G = 16
S5_H = 16
S5_P = 64
S5_N = S5_G * S5_P
S5_MAX_RE = -1e-4
HEADS = 8
NOPE = 64
ROPE = 32
QK = NOPE + ROPE
VD = 64
HP = 128
XH = 4
XD = 256
LN_EPS = 1e-5
RMS_EPS = 1e-6
NEG_INF = -1e30
DN_ALPHA = 2.0 ** 0.25
ROPE_THETA = 10000.0
ADAM_LR, ADAM_B1, ADAM_B2, ADAM_EPS, ADAM_WD, ADAM_STEP = 0.001, 0.9, 0.999, 1e-08, 0.01, 10

Z_GS, Z_GA, Z_U, Z_CQ, Z_CKV, Z_KR, Z_W = 0, 1024, 2048, 2304, 2560, 2816, 2944

BIG_EARLY = ("w_in", "w_glu", "w_uq", "w_ukv", "w_oa", "w_o")
BIG_LATE = ("w_up", "w_xq", "w_xk", "w_xv", "w_xo", "w_down")
BIG = BIG_EARLY + BIG_LATE
COL_SHARDED = ("w_in", "w_glu", "w_uq", "w_ukv", "w_oa", "w_up")
SMALL = ("ln_in_g", "ln_in_b", "s5_lam_re", "s5_lam_im", "s5_log_dt", "s5_b_re", "s5_b_im", "s5_c_re", "s5_c_im",
         "s5_d", "q_norm_g", "kv_norm_g", "ln1_g", "ln1_b", "ln2_g", "ln2_b", "ln3_g", "ln3_b")
WEIGHTS = ("ln_in_g", "ln_in_b", "w_in", "s5_lam_re", "s5_lam_im", "s5_log_dt", "s5_b_re", "s5_b_im", "s5_c_re",
           "s5_c_im", "s5_d", "w_glu", "q_norm_g", "w_uq", "kv_norm_g", "w_ukv", "w_oa", "w_o", "ln1_g", "ln1_b",
           "w_xq", "w_xk", "w_xv", "w_xo", "ln2_g", "ln2_b", "w_up", "w_down", "ln3_g", "ln3_b")
PACK_ROWS = 2 * 16


def _row_tile(n, cap=512):
    best = n
    for t in range(8, min(n, cap) + 1, 8):
        if n % t == 0:
            best = t
    return best


def _pick(n, cap):
    best = None
    for t in range(128, min(n, cap) + 1, 128):
        if n % t == 0:
            best = t
    return best if best is not None and (best >= 512 or best == n) else n


def _mm(a, b, name, mode="nn", *, a_off=0, b_off=0, m=None, n=None, act=None, epi=None, extras=(), rextras=(),
        pextras=(), outs=None, out_dtype=None, tm=1024, tn=1024, tk=None):
    out_dtype = out_dtype or (BF16 if mode == "tn" else F32)
    if mode == "nn":
        M, (K, N) = a.shape[0], b.shape
    elif mode == "nt":
        M, (N, K) = a.shape[0], b.shape
    else:
        K, M, N = a.shape[0], m, n
    if mode == "tn":
        tm, tn, tk = _pick(M, tm), _pick(N, tn), min(tk or 1024, K)
    else:
        tm, tn, tk = min(tm, M), _pick(N, tn), _pick(K, tk or 1024)
    assert M % tm == 0 and N % tn == 0 and K % tk == 0, (name, M, N, K, tm, tn, tk)
    nk = K // tk
    if mode == "tn":
        assert a_off % tm == 0 and b_off % tn == 0
        ao, bo = a_off // tm, b_off // tn
        a_spec = pl.BlockSpec((tk, tm), lambda i, j, k: (k, i + ao))
        b_spec = pl.BlockSpec((tk, tn), lambda i, j, k: (k, j + bo))
        dims = (((0,), (0,)), ((), ()))
    else:
        assert a_off % tk == 0
        ao = a_off // tk
        a_spec = pl.BlockSpec((tm, tk), lambda i, j, k: (i, k + ao))
        if mode == "nn":
            b_spec = pl.BlockSpec((tk, tn), lambda i, j, k: (k, j))
            dims = (((1,), (0,)), ((), ()))
        else:
            b_spec = pl.BlockSpec((tn, tk), lambda i, j, k: (j, k))
            dims = (((1,), (1,)), ((), ()))
    e_spec = pl.BlockSpec((tm, tn), lambda i, j, k: (i, j))
    r_specs = [pl.BlockSpec((tm, w), functools.partial(lambda i, j, k, o: (i, o), o=off // w)) for _, off, w in rextras]
    p_specs = [pl.BlockSpec((1, tn), lambda i, j, k: (0, j)) if p.shape[1] == N else pl.BlockSpec(p.shape, lambda i, j, k: (0, 0))
               for p in pextras]
    n_extra = len(extras) + len(rextras) + len(pextras)
    if outs is None:
        o_specs, o_shapes = [e_spec], [jax.ShapeDtypeStruct((M, N), out_dtype)]
    else:
        assert all((w == tn and dt != "sum") or N == tn for w, dt in outs), name
        o_specs = [e_spec if w == tn and dt != "sum" else pl.BlockSpec((8 if dt == "sum" else tm, w), lambda i, j, k: (i, 0))
                   for w, dt in outs]
        o_shapes = [jax.ShapeDtypeStruct((M // tm * 8, w), F32) if dt == "sum" else
                    jax.ShapeDtypeStruct((M, N if w == tn else w), dt) for w, dt in outs]
    n_out = len(o_specs)
    direct = outs is None and epi is None and out_dtype == F32

    def body(*refs):
        a_ref, b_ref = refs[0], refs[1]
        e_refs = refs[2:2 + n_extra]
        o_refs = refs[2 + n_extra:2 + n_extra + n_out]
        av = a_ref[...]
        if act is not None:
            av = act(av.astype(F32))
        p = lax.dot_general(av.astype(BF16), b_ref[...].astype(BF16), dims, preferred_element_type=F32)

        def finish(r):
            if epi is not None:
                r = epi(r, *[e[...] for e in e_refs])
            for o_ref, v in zip(o_refs, r if isinstance(r, tuple) else (r,)):
                o_ref[...] = jnp.broadcast_to(v, o_ref.shape).astype(o_ref.dtype)

        if nk == 1:
            finish(p)
        else:
            acc = o_refs[0] if direct else refs[2 + n_extra + n_out]
            k = pl.program_id(2)

            @pl.when(k == 0)
            def _():
                acc[...] = p

            @pl.when(k > 0)
            def _():
                acc[...] += p

            if not direct:
                @pl.when(k == nk - 1)
                def _():
                    finish(acc[...])

    res = pl.pallas_call(
        body, name=name, grid=(M // tm, N // tn, nk),
        in_specs=[a_spec, b_spec] + [e_spec] * len(extras) + r_specs + p_specs, out_specs=o_specs,
        out_shape=o_shapes,
        scratch_shapes=[pltpu.VMEM((tm, tn), F32)] if nk > 1 and not direct else [],
        compiler_params=pltpu.CompilerParams(dimension_semantics=("parallel", "parallel", "arbitrary")),
    )(a, b, *extras, *[r[0] for r in rextras], *pextras)
    if outs is None:
        return res[0]
    return [r.reshape(M // tm, 8, -1)[:, 0].sum(axis=0, keepdims=True) if dt == "sum" else r for r, (_, dt) in zip(res, outs)]


def _rowwise(fn, name, rows, pars, outs, accs=(), tm=256, upcast=True):
    T = rows[0][0].shape[0]
    tm = min(tm, T)
    assert T % tm == 0
    n_in, n_out = len(rows) + len(pars), len(outs)
    in_specs = []
    for arr, off, w in rows:
        assert off % w == 0 and arr.shape[0] == T, name
        in_specs.append(pl.BlockSpec((tm, w), functools.partial(lambda i, o: (i, o), o=off // w)))
    for p in pars:
        in_specs.append(pl.BlockSpec(p.shape, lambda i: (0, 0)))
    out_specs = [pl.BlockSpec((tm, w), lambda i: (i, 0)) for w, _ in outs]
    out_specs += [pl.BlockSpec(s, lambda i: (0, 0)) for s in accs]
    out_shape = [jax.ShapeDtypeStruct((T, w), dt) for w, dt in outs] + [jax.ShapeDtypeStruct(s, F32) for s in accs]

    def body(*refs):
        res = fn(*[r[...].astype(F32) if upcast else r[...] for r in refs[:n_in]])
        o_refs = refs[n_in:]
        for r, v in zip(o_refs[:n_out], res[:n_out]):
            r[...] = v.astype(r.dtype)
        if accs:
            i = pl.program_id(0)

            @pl.when(i == 0)
            def _():
                for r in o_refs[n_out:]:
                    r[...] = jnp.zeros_like(r)

            for r, v in zip(o_refs[n_out:], res[n_out:]):
                r[...] += v

    res = pl.pallas_call(
        body, name=name, grid=(T // tm,), in_specs=in_specs, out_specs=out_specs, out_shape=out_shape,
        compiler_params=pltpu.CompilerParams(dimension_semantics=("arbitrary",)),
    )(*[r[0] for r in rows], *pars)
    return res


def _whole(arr):
    return (arr, 0, arr.shape[1])


def _ln(x, g, b):
    mu = jnp.mean(x, axis=-1, keepdims=True)
    xc = x - mu
    var = jnp.mean(xc * xc, axis=-1, keepdims=True)
    return xc * lax.rsqrt(var + LN_EPS) * g + b


def _ln_res(h, r, g, b):
    return _ln(DN_ALPHA * h + r, g, b)


def _rms(x, g):
    return x * lax.rsqrt(jnp.mean(x * x, axis=-1, keepdims=True) + RMS_EPS) * g


def _gelu_skip(yc, u, d):
    y = yc + d * u
    return 0.5 * y * (1.0 + lax.erf(y * (1.0 / math.sqrt(2.0))))


def _gate_mix(gs, ga, y2a, y2b, aout):
    return jax.nn.sigmoid(gs) * (y2a * jax.nn.sigmoid(y2b)) + jax.nn.sigmoid(ga) * aout


def _relu2(a):
    r = jnp.maximum(a, 0.0)
    return r * r


def _tile8(t):
    return jnp.concatenate([t] * HEADS, axis=1)


def _rope(x, cos, s1, s2):
    w = x.shape[1]
    return x * cos + pltpu.roll(x, ROPE // 2, 1) * s1 + pltpu.roll(x, w - ROPE // 2, 1) * s2


def _rope_t(dy, cos, s1, s2):
    w = dy.shape[1]
    return dy * cos + pltpu.roll(dy * s1, w - ROPE // 2, 1) + pltpu.roll(dy * s2, ROPE // 2, 1)


def _rope_lanes(shape):
    lane = lax.broadcasted_iota(jnp.int32, shape, 1) % HP
    return (lane >= NOPE) & (lane < QK)


ATT_HB_FWD, ATT_HB_BWD = 8, 8
ATT_BQ_FWD = 1024
ATT_SCALE = QK ** -0.5
LOG2E = 1.4426950408889634
LN2 = 0.6931471805599453
_NT = (((1,), (1,)), ((), ()))
_TN = (((0,), (0,)), ((), ()))


def _carry_specs(carry):
    if carry is None:
        return [], [], [], []
    arr = carry[0]
    return ([_HBM], [_HBM], [jax.ShapeDtypeStruct((4,) + arr.shape[1:], arr.dtype)],
            [pltpu.SemaphoreType.DMA((3,)), pltpu.SemaphoreType.DMA((3,))])


def _carry_start(exchange, when):
    @pl.when(when)
    def _():
        for cp in exchange(False):
            cp.start()


def _carry_wait(exchange, when):
    @pl.when(when)
    def _():
        for cp in exchange(True):
            cp.wait_recv()
        for cp in exchange(False):
            cp.wait_send()


def _tri_tables(n, by_row):
    if by_row:
        pairs = [(i, j) for i in range(n) for j in range(i + 1)]
    else:
        pairs = [(i, j) for j in range(n) for i in range(j, n)]
    return jnp.array([p[0] for p in pairs], jnp.int32), jnp.array([p[1] for p in pairs], jnp.int32)


def _flash_fwd(q, k, v, name, bq=512, carry=None):
    T = q.shape[0]
    bq = min(bq, T)
    nq = T // bq
    hb = ATT_HB_FWD
    n_carry = 0 if carry is None else 1

    i_tab, j_tab = _tri_tables(nq, by_row=True)

    def body(it_ref, jt_ref, q_ref, k_ref, v_ref, *rest):
        o_ref, lse_ref = rest[n_carry:n_carry + 2]
        m_sc, l_sc, acc_sc = rest[2 * n_carry + 2:2 * n_carry + 5]
        t = pl.program_id(1)
        i, j = it_ref[t], jt_ref[t]
        if carry is not None:
            exchange = functools.partial(_carried_exchange, rest[0], rest[3], rest[-2], rest[-1], carry[1])
            _carry_start(exchange, (pl.program_id(0) == 0) & (t == 0))

        @pl.when(j == 0)
        def _():
            m_sc[...] = jnp.full_like(m_sc, NEG_INF)
            l_sc[...] = jnp.zeros_like(l_sc)
            acc_sc[...] = jnp.zeros_like(acc_sc)

        def step(masked):
            for hh in range(hb):
                sl = slice(hh * HP, (hh + 1) * HP)
                s = lax.dot_general(q_ref[:, sl], k_ref[:, sl], _NT, preferred_element_type=F32)
                if masked:
                    row = lax.broadcasted_iota(jnp.int32, (bq, bq), 0)
                    col = lax.broadcasted_iota(jnp.int32, (bq, bq), 1)
                    s = jnp.where(col <= row, s, NEG_INF)
                m_prev = m_sc[hh]
                m_new = jnp.maximum(m_prev, jnp.max(s, axis=1, keepdims=True))
                alpha = jnp.exp2(m_prev - m_new)
                p = jnp.exp2(s - jnp.concatenate([m_new] * (bq // HP), axis=1))
                l_sc[hh] = alpha * l_sc[hh] + jnp.sum(p, axis=1, keepdims=True)
                acc_sc[hh] = alpha * acc_sc[hh] + jnp.dot(p.astype(BF16), v_ref[:, sl], preferred_element_type=F32)
                m_sc[hh] = m_new

        @pl.when(j < i)
        def _():
            step(False)

        @pl.when(j == i)
        def _():
            step(True)
            for hh in range(hb):
                o_ref[:, hh * HP:(hh + 1) * HP] = acc_sc[hh] / l_sc[hh]
                lse_ref[hh] = jnp.transpose((m_sc[hh] + jnp.log2(l_sc[hh])) * LN2)[0:1, :]

        if carry is not None:
            _carry_wait(exchange, (pl.program_id(0) == HEADS // hb - 1) & (t == i_tab.shape[0] - 1))

    qs = pl.BlockSpec((bq, hb * HP), lambda h, t, it, jt: (it[t], h))
    ks = pl.BlockSpec((bq, hb * HP), lambda h, t, it, jt: (jt[t], h))
    c_in, c_out, c_shape, c_sems = _carry_specs(carry)
    gs = pltpu.PrefetchScalarGridSpec(
        num_scalar_prefetch=2, grid=(HEADS // hb, i_tab.shape[0]), in_specs=[qs, ks, ks] + c_in,
        out_specs=[qs, pl.BlockSpec((hb, 1, bq), lambda h, t, it, jt: (h, 0, it[t]))] + c_out,
        scratch_shapes=[pltpu.VMEM((hb, bq, HP), F32)] * 3 + c_sems)
    return pl.pallas_call(
        body, name=name, grid_spec=gs,
        out_shape=[jax.ShapeDtypeStruct((T, HEADS * HP), F32), jax.ShapeDtypeStruct((HEADS, 1, T), F32)] + c_shape,
        compiler_params=pltpu.CompilerParams(dimension_semantics=("arbitrary", "arbitrary")),
    )(i_tab, j_tab, q, k, v, *([] if carry is None else [carry[0]]))


def _flash_bwd(q, k, v, do, lse_row, delta_row, name, bq=512, carry=None):
    T = q.shape[0]
    bq = min(bq, T)
    nq = T // bq
    hb = ATT_HB_BWD
    n_carry = 0 if carry is None else 1

    i_tab, j_tab = _tri_tables(nq, by_row=False)
    n_blk = i_tab.shape[0]

    def body(it_ref, jt_ref, q_ref, k_ref, v_ref, do_ref, lse_ref, dl_ref, *rest):
        dq_ref, dk_ref, dv_ref = rest[n_carry:n_carry + 3]
        t = pl.program_id(1)
        i, j = it_ref[t], jt_ref[t]
        if carry is not None:
            exchange = functools.partial(_carried_exchange, rest[0], rest[4], rest[-2], rest[-1], carry[1])
            _carry_start(exchange, (pl.program_id(0) == 0) & (t == 0))

        @pl.when(t == 0)
        def _():
            dq_ref[...] = jnp.zeros_like(dq_ref)

        @pl.when(i == j)
        def _():
            dk_ref[...] = jnp.zeros_like(dk_ref)
            dv_ref[...] = jnp.zeros_like(dv_ref)

        def step(masked):
            rows = pl.ds(pl.multiple_of(i * bq, bq), bq)
            for hh in range(hb):
                sl = slice(hh * HP, (hh + 1) * HP)
                qh, kh, doh = q_ref[:, sl], k_ref[:, sl], do_ref[:, sl]
                st = lax.dot_general(kh, qh, _NT, preferred_element_type=F32)
                pt = jnp.exp2(st - lse_ref[hh] * LOG2E)
                if masked:
                    krow = lax.broadcasted_iota(jnp.int32, (bq, bq), 0)
                    qcol = lax.broadcasted_iota(jnp.int32, (bq, bq), 1)
                    pt = jnp.where(krow <= qcol, pt, 0.0)
                dv_ref[:, sl] += jnp.dot(pt.astype(BF16), doh, preferred_element_type=F32)
                dpt = lax.dot_general(v_ref[:, sl], doh, _NT, preferred_element_type=F32)
                dst = (pt * (dpt - dl_ref[hh])).astype(BF16)
                dk_ref[:, sl] += jnp.dot(dst, qh, preferred_element_type=F32)
                dq_ref[rows, sl] += lax.dot_general(dst, kh, _TN, preferred_element_type=F32)

        @pl.when(i > j)
        def _():
            step(False)

        @pl.when(i == j)
        def _():
            step(True)

        @pl.when(i == nq - 1)
        def _():
            dk_ref[...] *= LN2

        @pl.when(t == n_blk - 1)
        def _():
            dq_ref[...] *= ATT_SCALE

        if carry is not None:
            _carry_wait(exchange, (pl.program_id(0) == HEADS // hb - 1) & (t == n_blk - 1))

    qs = pl.BlockSpec((bq, hb * HP), lambda h, t, it, jt: (it[t], h))
    ks = pl.BlockSpec((bq, hb * HP), lambda h, t, it, jt: (jt[t], h))
    rs = pl.BlockSpec((hb, 1, bq), lambda h, t, it, jt: (h, 0, it[t]))
    full = pl.BlockSpec((T, hb * HP), lambda h, t, it, jt: (0, h), pipeline_mode=pl.Buffered(1))
    c_in, c_out, c_shape, c_sems = _carry_specs(carry)
    gs = pltpu.PrefetchScalarGridSpec(num_scalar_prefetch=2, grid=(HEADS // hb, n_blk),
                                      in_specs=[qs, ks, ks, qs, rs, rs] + c_in, out_specs=[full, ks, ks] + c_out,
                                      scratch_shapes=c_sems)
    return pl.pallas_call(
        body, name=name, grid_spec=gs, out_shape=[jax.ShapeDtypeStruct((T, HEADS * HP), F32)] * 3 + c_shape,
        compiler_params=pltpu.CompilerParams(dimension_semantics=("arbitrary", "arbitrary")),
    )(i_tab, j_tab, q, k, v, do, lse_row, delta_row, *([] if carry is None else [carry[0]]))


def _xattn_heads(q, k, v):
    scale = XD ** -0.5
    ps = []
    for h in range(XH):
        sl = slice(h * XD, (h + 1) * XD)
        s = lax.dot_general(q[:, sl].astype(BF16), k[:, sl].astype(BF16), (((1,), (1,)), ((), ())),
                            preferred_element_type=F32) * scale
        e = jnp.exp(s - jnp.max(s, axis=1, keepdims=True))
        ps.append(e / jnp.sum(e, axis=1, keepdims=True))
    return ps


def _xattn_fwd_fn(q, k, v):
    ps = _xattn_heads(q, k, v)
    o = [jnp.dot(p.astype(BF16), v[:, h * XD:(h + 1) * XD].astype(BF16), preferred_element_type=F32)
         for h, p in enumerate(ps)]
    return (jnp.concatenate(o, axis=1),)


def _xattn_bwd_fn(q, do, k, v):
    scale = XD ** -0.5
    ps = _xattn_heads(q, k, v)
    dqs, dks, dvs = [], [], []
    tdims = (((0,), (0,)), ((), ()))
    for h, p in enumerate(ps):
        sl = slice(h * XD, (h + 1) * XD)
        doh = do[:, sl].astype(BF16)
        dvs.append(lax.dot_general(p.astype(BF16), doh, tdims, preferred_element_type=F32))
        dp = lax.dot_general(doh, v[:, sl].astype(BF16), (((1,), (1,)), ((), ())), preferred_element_type=F32)
        ds = (p * (dp - jnp.sum(dp * p, axis=1, keepdims=True)) * scale).astype(BF16)
        dqs.append(jnp.dot(ds, k[:, sl].astype(BF16), preferred_element_type=F32))
        dks.append(lax.dot_general(ds, q[:, sl].astype(BF16), tdims, preferred_element_type=F32))
    return jnp.concatenate(dqs, axis=1), jnp.concatenate(dks, axis=1), jnp.concatenate(dvs, axis=1)


S5_SUB = 8


def _cmul_add(xr, xi, ar, ai, sr, si):
    return xr + ar * sr - ai * si, xi + ar * si + ai * sr


def _s5_tables(ar, ai, reverse):
    pows = [(ar, ai)]
    for _ in range(S5_SUB - 1):
        pr, pi = pows[-1]
        pows.append((pr * ar - pi * ai, pr * ai + pi * ar))
    cr = jnp.concatenate([p[0] for p in pows], axis=0)
    ci = jnp.concatenate([p[1] for p in pows], axis=0)
    if reverse:
        cr, ci = cr[::-1], ci[::-1]
    t = jnp.arange(S5_SUB)[:, None]
    blocks = [jnp.concatenate([cr, ci], axis=1)]
    for s in (1, 2, 4):
        keep = (t < S5_SUB - s) if reverse else (t >= s)
        sr, si = pows[s - 1]
        blocks.append(jnp.concatenate([jnp.where(keep, sr, 0.0), jnp.where(keep, si, 0.0)], axis=1))
    return jnp.concatenate(blocks, axis=0)


def _sub_scan(xr, xi, tab_ref, reverse):
    for k, s in enumerate((1, 2, 4)):
        blk = slice(S5_SUB * (k + 1), S5_SUB * (k + 2))
        sh = S5_SUB - s if reverse else s
        xr, xi = _cmul_add(xr, xi, tab_ref[blk, :S5_N], tab_ref[blk, S5_N:], pltpu.roll(xr, sh, 0), pltpu.roll(xi, sh, 0))
    return xr, xi


def _s5_fwd(z, b_st, c_st, d_skip, tab, name, tc=1024):
    T = z.shape[0]
    tc = min(tc, T)
    nsub = tc // S5_SUB

    def body(u_ref, bst_ref, cst_ref, d_ref, tab_ref, h_ref, yc_ref, yg_ref, carry, x_sc):
        @pl.when(pl.program_id(0) == 0)
        def _():
            carry[...] = jnp.zeros_like(carry)

        u = u_ref[...].astype(F32)
        x_sc[...] = jnp.dot(u.astype(BF16), bst_ref[...], preferred_element_type=F32)

        def sub(b, c):
            r = pl.ds(pl.multiple_of(b * S5_SUB, S5_SUB), S5_SUB)
            xr, xi = _sub_scan(x_sc[r, :S5_N], x_sc[r, S5_N:], tab_ref, False)
            hr, hi = _cmul_add(xr, xi, tab_ref[0:S5_SUB, :S5_N], tab_ref[0:S5_SUB, S5_N:], c[0], c[1])
            x_sc[r, :S5_N] = hr
            x_sc[r, S5_N:] = hi
            return hr[S5_SUB - 1:S5_SUB], hi[S5_SUB - 1:S5_SUB]

        cr, ci = lax.fori_loop(0, nsub, sub, (carry[0:1, :S5_N], carry[0:1, S5_N:]))
        carry[0:1, :S5_N] = cr
        carry[0:1, S5_N:] = ci
        h = x_sc[...]
        h_ref[...] = h
        yc = jnp.dot(h.astype(BF16), cst_ref[...], preferred_element_type=F32)
        yc_ref[...] = yc
        yg_ref[...] = _gelu_skip(yc, u, d_ref[...]).astype(yg_ref.dtype)

    whole = lambda a: pl.BlockSpec(a.shape, lambda i: (0, 0))
    row = lambda w: pl.BlockSpec((tc, w), lambda i: (i, 0))
    return pl.pallas_call(
        body, name=name, grid=(T // tc,),
        in_specs=[pl.BlockSpec((tc, S5_W), lambda i: (i, Z_U // S5_W)), whole(b_st), whole(c_st), whole(d_skip), whole(tab)],
        out_specs=[row(2 * S5_N), row(S5_W), row(S5_W)],
        out_shape=[jax.ShapeDtypeStruct((T, 2 * S5_N), F32), jax.ShapeDtypeStruct((T, S5_W), F32),
                   jax.ShapeDtypeStruct((T, S5_W), BF16)],
        scratch_shapes=[pltpu.VMEM((8, 2 * S5_N), F32), pltpu.VMEM((tc, 2 * S5_N), F32)],
        compiler_params=pltpu.CompilerParams(dimension_semantics=("arbitrary",)),
    )(z, b_st, c_st, d_skip, tab)


def _s5_bwd(dyg, yc, z, h, b_st, c_st, d_skip, tab, name, tc=1024):
    T = z.shape[0]
    tc = min(tc, T)
    nc, nsub = T // tc, tc // S5_SUB

    def body(dyg_ref, yc_ref, u_ref, h_ref, hp_ref, bst_ref, cst_ref, d_ref, tab_ref,
             du_ref, da_ref, db_ref, dct_ref, dd_ref, carry, x_sc):
        g = pl.program_id(0)

        @pl.when(g == 0)
        def _():
            carry[...] = jnp.zeros_like(carry)
            for r in (da_ref, db_ref, dct_ref, dd_ref):
                r[...] = jnp.zeros_like(r)

        u = u_ref[...].astype(F32)
        _, vjp = jax.vjp(_gelu_skip, yc_ref[...], u, d_ref[...])
        dyc, du_skip, dd = vjp(dyg_ref[...])
        dd_ref[...] += dd
        dyc16 = dyc.astype(BF16)
        dct_ref[...] += lax.dot_general(dyc16, h_ref[...].astype(BF16), _TN, preferred_element_type=F32)
        x_sc[...] = lax.dot_general(dyc16, cst_ref[...], _NT, preferred_element_type=F32)
        first = jnp.where(g == nc - 1, 0.0, 1.0)
        row0 = lax.broadcasted_iota(jnp.int32, (S5_SUB, S5_N), 0) == 0

        def sub(k, c):
            cr, ci, acc_r, acc_i = c
            b = nsub - 1 - k
            r = pl.ds(pl.multiple_of(b * S5_SUB, S5_SUB), S5_SUB)
            xr, xi = _sub_scan(x_sc[r, :S5_N], x_sc[r, S5_N:], tab_ref, True)
            lr, li = _cmul_add(xr, xi, tab_ref[0:S5_SUB, :S5_N], tab_ref[0:S5_SUB, S5_N:], cr, ci)
            x_sc[r, :S5_N] = lr
            x_sc[r, S5_N:] = li
            rp = pl.ds(pl.multiple_of(jnp.maximum(b - 1, 0) * S5_SUB, S5_SUB), S5_SUB)
            last_r = jnp.where(b == 0, hp_ref[S5_SUB - 1:S5_SUB, :S5_N] * first, h_ref[rp, :S5_N][S5_SUB - 1:S5_SUB])
            last_i = jnp.where(b == 0, hp_ref[S5_SUB - 1:S5_SUB, S5_N:] * first, h_ref[rp, S5_N:][S5_SUB - 1:S5_SUB])
            pr = jnp.where(row0, last_r, pltpu.roll(h_ref[r, :S5_N], 1, 0))
            pi = jnp.where(row0, last_i, pltpu.roll(h_ref[r, S5_N:], 1, 0))
            return lr[0:1], li[0:1], acc_r + (lr * pr + li * pi), acc_i + (li * pr - lr * pi)

        zero = jnp.zeros((S5_SUB, S5_N), F32)
        cr, ci, acc_r, acc_i = lax.fori_loop(0, nsub, sub, (carry[0:1, :S5_N], carry[0:1, S5_N:], zero, zero))
        carry[0:1, :S5_N] = cr
        carry[0:1, S5_N:] = ci
        da_ref[0:1, :S5_N] += jnp.sum(acc_r, axis=0, keepdims=True)
        da_ref[0:1, S5_N:] += jnp.sum(acc_i, axis=0, keepdims=True)
        lam16 = x_sc[...].astype(BF16)
        db_ref[...] += lax.dot_general(u.astype(BF16), lam16, _TN, preferred_element_type=F32)
        du = lax.dot_general(lam16, bst_ref[...], _NT, preferred_element_type=F32) + du_skip
        du_ref[...] = du.astype(du_ref.dtype)

    whole = lambda a: pl.BlockSpec(a.shape, lambda g: (0, 0))
    row = lambda w, off=0: pl.BlockSpec((tc, w), lambda g: (nc - 1 - g, off))
    prev = pl.BlockSpec((S5_SUB, 2 * S5_N), lambda g: (jnp.maximum((nc - 1 - g) * nsub - 1, 0), 0))
    acc = lambda s: pl.BlockSpec(s, lambda g: (0, 0))
    return pl.pallas_call(
        body, name=name, grid=(nc,),
        in_specs=[row(S5_W), row(S5_W), row(S5_W, Z_U // S5_W), row(2 * S5_N), prev, whole(b_st), whole(c_st),
                  whole(d_skip), whole(tab)],
        out_specs=[row(S5_W), acc((1, 2 * S5_N)), acc((S5_W, 2 * S5_N)), acc((S5_W, 2 * S5_N)), acc((1, S5_W))],
        out_shape=[jax.ShapeDtypeStruct((T, S5_W), BF16), jax.ShapeDtypeStruct((1, 2 * S5_N), F32),
                   jax.ShapeDtypeStruct((S5_W, 2 * S5_N), F32), jax.ShapeDtypeStruct((S5_W, 2 * S5_N), F32),
                   jax.ShapeDtypeStruct((1, S5_W), F32)],
        scratch_shapes=[pltpu.VMEM((8, 2 * S5_N), F32), pltpu.VMEM((tc, 2 * S5_N), F32)],
        compiler_params=pltpu.CompilerParams(dimension_semantics=("arbitrary",)),
    )(dyg, yc, z, h, h, b_st, c_st, d_skip, tab)


def _s5_discretize(lam_re, lam_im, log_dt, b_re, b_im):
    lr = jnp.minimum(lam_re, S5_MAX_RE)
    li = lam_im
    dt = jnp.exp(log_dt)[:, None]
    mag = jnp.exp(lr * dt)
    ang = li * dt
    ab_re = mag * jnp.cos(ang)
    ab_im = mag * jnp.sin(ang)
    den = lr * lr + li * li
    nr = ab_re - 1.0
    f_re = ((nr * lr + ab_im * li) / den)[..., None]
    f_im = ((ab_im * lr - nr * li) / den)[..., None]
    return ab_re, ab_im, f_re * b_re - f_im * b_im, f_re * b_im + f_im * b_re


def _adamw_fn(w, g, m, v):
    m = ADAM_B1 * m + (1.0 - ADAM_B1) * g
    v = ADAM_B2 * v + (1.0 - ADAM_B2) * (g * g)
    m_hat = m / (1.0 - ADAM_B1 ** ADAM_STEP)
    v_hat = v / (1.0 - ADAM_B2 ** ADAM_STEP)
    delta = -ADAM_LR * (m_hat / (jnp.sqrt(v_hat) + ADAM_EPS) + ADAM_WD * w)
    return delta, m, v


def _adamw(w, g, m, v, name):
    wd = w.shape[1]
    return _rowwise(_adamw_fn, name, [_whole(w), _whole(g), _whole(m), _whole(v)], [], [(wd, F32)] * 3,
                    tm=_row_tile(w.shape[0]))


def _adamw_many(ws, gs, ms, vs, name):
    n = len(ws)

    def body(*refs):
        for i in range(n):
            d_, m_, v_ = _adamw_fn(*[refs[kk * n + i][...] for kk in range(4)])
            refs[4 * n + i][...] = d_
            refs[5 * n + i][...] = m_
            refs[6 * n + i][...] = v_

    out = pl.pallas_call(body, name=name, out_shape=[jax.ShapeDtypeStruct(w.shape, F32) for w in ws] * 3)(*ws, *gs, *ms, *vs)
    return out[:n], out[n:2 * n], out[2 * n:]


def _me():
    return lax.axis_index("x"), lax.axis_index("y"), lax.axis_index("c")


_HBM = pl.BlockSpec(memory_space=pl.ANY)


def _gather_weights(shard):
    _, R, W = shard.shape

    def body(s_ref, out_ref, send_sems, recv_sems):
        x, y, c = _me()
        mine = 2 * x + y
        sib = (x, y, 1 - c)
        chips = [(1 - x, y), (x, 1 - y), (1 - x, 1 - y)]

        def rcopy(kk, src, chip_idx, half, to):
            return pltpu.make_async_remote_copy(src_ref=src, dst_ref=out_ref.at[chip_idx, half],
                                                send_sem=send_sems.at[kk], recv_sem=recv_sems.at[kk],
                                                device_id=to, device_id_type=MESH)

        first = [rcopy(j, s_ref.at[c], mine, c, (*chip, c)) for j, chip in enumerate(chips)]
        for cp in first:
            cp.start()
        passed = []
        for j, (px, py) in enumerate(chips):
            src_chip = 2 * px + py
            rcopy(j, s_ref.at[c], src_chip, c, (x, y, c)).wait_recv()
            fwd = rcopy(3 + j, out_ref.at[src_chip, c], src_chip, c, sib)
            fwd.start()
            passed.append(fwd)
        for j, (px, py) in enumerate(chips):
            rcopy(3 + j, s_ref.at[c], 2 * px + py, 1 - c, (x, y, c)).wait_recv()
        for cp in first + passed:
            cp.wait_send()

    return pl.pallas_call(
        body, name="gather_weights", in_specs=[_HBM], out_specs=_HBM,
        out_shape=jax.ShapeDtypeStruct((4, 2, R, W), shard.dtype),
        scratch_shapes=[pltpu.SemaphoreType.DMA((6,)), pltpu.SemaphoreType.DMA((6,))],
    )(shard)


def _pair_exchange(g, name):
    _, _, R, W = g.shape

    def body(g_ref, out_ref, send_sem, recv_sem):
        x, y, c = _me()
        cp = pltpu.make_async_remote_copy(src_ref=g_ref.at[:, 1 - c], dst_ref=out_ref, send_sem=send_sem,
                                          recv_sem=recv_sem, device_id=(x, y, 1 - c), device_id_type=MESH)
        cp.start()
        cp.wait()

    return pl.pallas_call(
        body, name=name, in_specs=[_HBM], out_specs=_HBM,
        out_shape=jax.ShapeDtypeStruct((4, R, W), g.dtype),
        scratch_shapes=[pltpu.SemaphoreType.DMA(()), pltpu.SemaphoreType.DMA(())],
    )(g)


def _pair_add(g, recv, cidx, name):
    _, _, R, W = g.shape
    tr = _row_tile(R)

    def body(c_ref, a_ref, b_ref, o_ref):
        o_ref[...] = (a_ref[...].astype(F32) + b_ref[...].astype(F32)).astype(o_ref.dtype)

    gs = pltpu.PrefetchScalarGridSpec(
        num_scalar_prefetch=1, grid=(4, R // tr),
        in_specs=[pl.BlockSpec((None, None, tr, W), lambda j, r, c: (j, c[0], r, 0)),
                  pl.BlockSpec((None, tr, W), lambda j, r, c: (j, r, 0))],
        out_specs=pl.BlockSpec((None, tr, W), lambda j, r, c: (j, r, 0)))
    return pl.pallas_call(body, name=name, grid_spec=gs,
                          out_shape=jax.ShapeDtypeStruct((4, R, W), BF16))(cidx, g, recv)


def _carried_exchange(src_ref, dst_ref, send_sems, recv_sems, spread, incoming):
    x, y, c = _me()
    mine = 2 * x + y
    copies = []
    for j, (px, py) in enumerate([(1 - x, y), (x, 1 - y), (1 - x, 1 - y)]):
        src = src_ref.at[2 * px + py] if spread else src_ref.at[c]
        slot = 2 * px + py if incoming else mine
        copies.append(pltpu.make_async_remote_copy(src_ref=src, dst_ref=dst_ref.at[slot], send_sem=send_sems.at[j],
                                                   recv_sem=recv_sems.at[j], device_id=(px, py, c), device_id_type=MESH))
    return copies


def _chip_exchange(p, name):
    _, R, W = p.shape

    def body(p_ref, out_ref, send_sems, recv_sems):
        outs = _carried_exchange(p_ref, out_ref, send_sems, recv_sems, True, False)
        for cp in outs:
            cp.start()
        for cp in _carried_exchange(p_ref, out_ref, send_sems, recv_sems, True, True):
            cp.wait_recv()
        for cp in outs:
            cp.wait_send()

    return pl.pallas_call(
        body, name=name, in_specs=[_HBM], out_specs=_HBM,
        out_shape=jax.ShapeDtypeStruct((4, R, W), p.dtype),
        scratch_shapes=[pltpu.SemaphoreType.DMA((3,)), pltpu.SemaphoreType.DMA((3,))],
    )(p)


def _chip_sum(q, name):
    _, R, W = q.shape
    tr = _row_tile(R)

    def body(q_ref, o_ref):
        f = lambda t: q_ref[t].astype(F32)
        o_ref[...] = ((f(0) + f(1)) + f(2)) + f(3)

    return pl.pallas_call(body, name=name, grid=(R // tr,),
                          in_specs=[pl.BlockSpec((4, tr, W), lambda r: (0, r, 0))],
                          out_specs=pl.BlockSpec((tr, W), lambda r: (r, 0)),
                          out_shape=jax.ShapeDtypeStruct((R, W), F32))(q)


def _pair_share(r, name):
    R, W = r.shape

    def body(r_ref, out_ref, send_sem, recv_sem):
        x, y, c = _me()
        cp = pltpu.make_async_remote_copy(src_ref=r_ref, dst_ref=out_ref, send_sem=send_sem, recv_sem=recv_sem,
                                          device_id=(x, y, 1 - c), device_id_type=MESH)
        cp.start()
        cp.wait()

    return pl.pallas_call(
        body, name=name, in_specs=[_HBM], out_specs=_HBM,
        out_shape=jax.ShapeDtypeStruct((R, W), r.dtype),
        scratch_shapes=[pltpu.SemaphoreType.DMA(()), pltpu.SemaphoreType.DMA(())],
    )(r)


def _allreduce_small(vec):
    R, W = vec.shape

    def body(v_ref, out_ref, buf, send_sems, recv_sems):
        x, y, c = _me()
        me = 4 * x + 2 * y + c
        buf[me] = v_ref[...]
        cps = []
        for kk in range(1, 8):
            peer = (x ^ (kk >> 2), y ^ ((kk >> 1) & 1), c ^ (kk & 1))
            cp = pltpu.make_async_remote_copy(src_ref=v_ref, dst_ref=buf.at[me], send_sem=send_sems.at[kk - 1],
                                              recv_sem=recv_sems.at[kk - 1], device_id=peer, device_id_type=MESH)
            cp.start()
            cps.append(cp)
        for kk in range(1, 8):
            peer = (x ^ (kk >> 2), y ^ ((kk >> 1) & 1), c ^ (kk & 1))
            pltpu.make_async_remote_copy(src_ref=v_ref, dst_ref=buf.at[me ^ kk], send_sem=send_sems.at[kk - 1],
                                         recv_sem=recv_sems.at[kk - 1], device_id=peer, device_id_type=MESH).wait_recv()
        for cp in cps:
            cp.wait_send()
        acc = buf[0]
        for d in range(1, 8):
            acc = acc + buf[d]
        out_ref[...] = acc

    vm = pl.BlockSpec(memory_space=pltpu.VMEM)
    return pl.pallas_call(
        body, name="allreduce_small", in_specs=[vm], out_specs=vm, out_shape=jax.ShapeDtypeStruct((R, W), F32),
        scratch_shapes=[pltpu.VMEM((8, R, W), F32), pltpu.SemaphoreType.DMA((7,)), pltpu.SemaphoreType.DMA((7,))],
    )(vec)


def _pack_flat(parts, align, dtype):
    flat = jnp.concatenate([p.reshape(-1).astype(dtype) for p in parts])
    n = flat.shape[0]
    pad = (-n) % align
    return jnp.pad(flat, (0, pad)) if pad else flat


def _unpack_flat(flat, shapes):
    out, off = [], 0
    for s in shapes:
        n = math.prod(s)
        out.append(flat[off:off + n].reshape(s))
        off += n
    return out


def _full_from_shards(name, sh):
    if name in COL_SHARDED:
        return jnp.transpose(sh, (1, 0, 2)).reshape(sh.shape[1], 4 * sh.shape[2])
    return sh.reshape(4 * sh.shape[1], sh.shape[2])


def _shards_from_full(name, full):
    if name in COL_SHARDED:
        r, cc = full.shape[0], full.shape[1] // 4
        return jnp.transpose(full.reshape(r, 4, cc), (1, 0, 2)).reshape(4, r * cc)
    return full.reshape(4, -1)


def _pad_heads(w, width):
    k = w.shape[0]
    return jnp.pad(w.reshape(k, HEADS, width), ((0, 0), (0, 0), (0, HP - width))).reshape(k, HEADS * HP)


def _unpad_heads(w, width):
    k = w.shape[0]
    return w.reshape(k, HEADS, HP)[:, :, :width].reshape(k, HEADS * width)


def _blockdiag(t):
    g, a, b = t.shape
    return jnp.einsum("gab,gk->gakb", t, jnp.eye(g, dtype=t.dtype)).reshape(g * a, g * b)


def _blockdiag_t(m, a, b):
    g = m.shape[0] // a
    return jnp.einsum("gagb->gab", m.reshape(g, a, g, b))


def _local_step(x, mem, positions, target, W, sp, overlap=None):
    T = x.shape[0]
    row = lambda v: v.reshape(1, -1)

    ab_re, ab_im, bb_re, bb_im = _s5_discretize(sp["s5_lam_re"], sp["s5_lam_im"], sp["s5_log_dt"], sp["s5_b_re"], sp["s5_b_im"])
    ar, ai = ab_re.reshape(1, S5_N), ab_im.reshape(1, S5_N)
    tab_fwd, tab_rev = _s5_tables(ar, ai, False), _s5_tables(ar, -ai, True)
    b_st = jnp.concatenate([_blockdiag(jnp.swapaxes(bb_re, 1, 2)), _blockdiag(jnp.swapaxes(bb_im, 1, 2))],
                           axis=1).astype(BF16)
    c_st = jnp.concatenate([_blockdiag(jnp.swapaxes(sp["s5_c_re"], 1, 2)),
                            -_blockdiag(jnp.swapaxes(sp["s5_c_im"], 1, 2))], axis=0).astype(BF16)
    inv = ROPE_THETA ** (-jnp.arange(0, ROPE, 2, dtype=F32) / ROPE)
    lane = jnp.arange(HP)
    rot1, rot2 = (lane >= NOPE) & (lane < NOPE + ROPE // 2), (lane >= NOPE + ROPE // 2) & (lane < QK)
    ang = positions.astype(F32)[:, None] * jnp.where(rot1 | rot2, inv[(lane - NOPE) % (ROPE // 2)], 0.0)
    sn = jnp.sin(ang)
    t_cos, t_s1, t_s2 = jnp.cos(ang), jnp.where(rot2, sn, 0.0), jnp.where(rot1, -sn, 0.0)

    w_in = W["w_in"]
    w_in_re = jnp.concatenate([w_in[:, 800:], w_in[:, :768], w_in[:, 768:800],
                               jnp.zeros((D_MODEL, HP - ROPE), w_in.dtype)], axis=1)
    w_uq_p = _pad_heads(W["w_uq"], QK)
    wkv = W["w_ukv"].reshape(S5_W, HEADS, NOPE + VD)
    w_uk_p = _pad_heads(wkv[:, :, :NOPE].reshape(S5_W, HEADS * NOPE), NOPE)
    w_uv_p = _pad_heads(wkv[:, :, NOPE:].reshape(S5_W, HEADS * VD), VD)
    w_oa_p = jnp.pad(W["w_oa"].reshape(HEADS, VD, D_MODEL), ((0, 0), (0, HP - VD), (0, 0))).reshape(HEADS * HP, D_MODEL)

    g_in, b_in = row(sp["ln_in_g"]), row(sp["ln_in_b"])
    g1, b1, g2, b2, g3, b3 = (row(sp[k]) for k in ("ln1_g", "ln1_b", "ln2_g", "ln2_b", "ln3_g", "ln3_b"))
    gq, gkv, d_skip = row(sp["q_norm_g"]), row(sp["kv_norm_g"]), row(sp["s5_d"])

    (h0,) = _rowwise(lambda a, g, b: (_ln(a, g, b),), "ln_in_fwd", [_whole(x)], [g_in, b_in], [(D_MODEL, F32)], tm=512)
    def in_epi(acc, ga, gb):
        zb = acc.astype(BF16).astype(F32)
        return acc, _rms(zb[:, Z_CQ:Z_CQ + S5_W], ga), _rms(zb[:, Z_CKV:Z_CKV + S5_W], gb)

    z, cqn, ckvn = _mm(h0, w_in_re, "in_proj_norm", tm=512, tn=Z_W, epi=in_epi, pextras=(gq, gkv),
                       outs=[(Z_W, BF16), (S5_W, BF16), (S5_W, BF16)])
    hs, yc, yg = _s5_fwd(z, b_st, c_st, d_skip, tab_fwd, "s5_fwd")
    y2 = _mm(yg, W["w_glu"], "glu_proj", tn=2048, out_dtype=BF16)
    tabs = [(t_cos, 0, HP), (t_s1, 0, HP), (t_s2, 0, HP)]

    def rope_q(acc, tc_, t1, t2):
        return _rope(acc, _tile8(tc_), _tile8(t1), _tile8(t2)) * (ATT_SCALE * LOG2E)

    def rope_k(acc, krz, tc_, t1, t2):
        return acc + _tile8(_rope(pltpu.roll(krz.astype(F32), NOPE, 1), tc_, t1, t2))

    q = _mm(cqn, w_uq_p, "mla_uq_rope", epi=rope_q, rextras=tabs, out_dtype=BF16)
    k = _mm(ckvn, w_uk_p, "mla_uk_rope", epi=rope_k, rextras=[(z, Z_KR, HP)] + tabs, out_dtype=BF16)
    v = _mm(ckvn, w_uv_p, "mla_uv", out_dtype=BF16)
    if overlap is None:
        o, lse = _flash_fwd(q, k, v, "mla_attn_fwd", bq=ATT_BQ_FWD)
    else:
        o, lse, landed = _flash_fwd(q, k, v, "mla_attn_fwd", bq=ATT_BQ_FWD, carry=(overlap[0], False))
        W = {**W, **overlap[1](landed)}
    def gate_epi(acc, gs_, ga_, ya, yb):
        f = lambda t: t.astype(F32)
        return acc, _gate_mix(f(gs_), f(ga_), f(ya), f(yb), acc.astype(BF16).astype(F32))

    a_out, mixin = _mm(o, w_oa_p, "mla_oa_gate", epi=gate_epi, outs=[(D_MODEL, BF16)] * 2, tm=512,
                       rextras=[(z, Z_GS, D_MODEL), (z, Z_GA, D_MODEL), (y2, 0, D_MODEL), (y2, D_MODEL, D_MODEL)])
    def ln_epi(acc, h, g, b):
        s_ = DN_ALPHA * h + acc
        return s_, _ln(s_, g, b)

    two_rows = [(D_MODEL, F32)] * 2
    s1, h1 = _mm(mixin, W["w_o"], "mix_o_ln1", epi=ln_epi, extras=(h0,), pextras=(g1, b1), outs=two_rows, tm=512)
    xq = _mm(h1, W["w_xq"], "x_q", out_dtype=BF16)
    xk = _mm(mem, W["w_xk"], "x_k", out_dtype=BF16)
    xv = _mm(mem, W["w_xv"], "x_v", out_dtype=BF16)
    (xo,) = _rowwise(_xattn_fwd_fn, "xattn_fwd", [_whole(xq)], [xk, xv], [(D_MODEL, BF16)], tm=2048, upcast=False)
    s2, h2 = _mm(xo, W["w_xo"], "x_o_ln2", epi=ln_epi, extras=(h1,), pextras=(g2, b2), outs=two_rows, tm=512)
    a_up = _mm(h2, W["w_up"], "mlp_up", out_dtype=BF16, tn=2048)
    def loss_epi(acc, h, tgt, g, b):
        def f(r_, g_, b_):
            e = _ln_res(h, r_, g_, b_) - tgt
            return 0.5 * jnp.sum(jnp.mean(e * e, axis=-1))
        lv, (dr_, dg_, db_) = jax.value_and_grad(f, argnums=(0, 1, 2))(acc, g, b)
        return dr_, dg_, db_, jnp.broadcast_to(lv, (1, 128))

    row_and_sums = [(D_MODEL, F32), (D_MODEL, "sum"), (D_MODEL, "sum")]
    dr3, dg3, db3, lossv = _mm(a_up, W["w_down"], "mlp_down_loss", act=_relu2, tm=512, tk=4 * D_MODEL, epi=loss_epi,
                               extras=(h2, target), pextras=(g3, b3), outs=row_and_sums + [(128, "sum")])
    gW, gs = {}, {"ln3_g": dg3, "ln3_b": db3}

    da = _mm(dr3, W["w_down"], "mlp_down_bwd_a", "nt", epi=lambda acc, a: acc * (2.0 * jnp.maximum(a, 0.0)), extras=(a_up,),
             out_dtype=BF16, tn=2048)
    gW["w_down"] = _mm(a_up, dr3, "mlp_down_bwd_w", "tn", m=4 * D_MODEL, n=D_MODEL, act=_relu2, tk=2048)
    gW["w_up"] = _mm(h2, da, "mlp_up_bwd_w", "tn", m=D_MODEL, n=4 * D_MODEL, tk=2048)

    def ln_bwd_epi(acc, e, s_, g, b):
        _, vjp = jax.vjp(_ln, s_, g, b)
        return vjp(acc + DN_ALPHA * e)

    dr2, gs["ln2_g"], gs["ln2_b"] = _mm(da, W["w_up"], "mlp_up_bwd_ln2", "nt", tm=512, tk=4 * D_MODEL, epi=ln_bwd_epi,
                                        extras=(dr3, s2), pextras=(g2, b2), outs=row_and_sums)
    dxo = _mm(dr2, W["w_xo"], "x_o_bwd_a", "nt", out_dtype=BF16)
    gW["w_xo"] = _mm(xo, dr2, "x_o_bwd_w", "tn", m=D_MODEL, n=D_MODEL)
    dxq, dxk, dxv = _rowwise(_xattn_bwd_fn, "xattn_bwd", [_whole(xq), _whole(dxo)], [xk, xv], [(D_MODEL, BF16)],
                             [(xk.shape[0], D_MODEL)] * 2, tm=2048, upcast=False)
    gW["w_xq"] = _mm(h1, dxq, "x_q_bwd_w", "tn", m=D_MODEL, n=D_MODEL)
    gW["w_xk"] = _mm(mem, dxk, "x_k_bwd_w", "tn", m=D_MODEL, n=D_MODEL)
    gW["w_xv"] = _mm(mem, dxv, "x_v_bwd_w", "tn", m=D_MODEL, n=D_MODEL)

    dr1, gs["ln1_g"], gs["ln1_b"] = _mm(dxq, W["w_xq"], "x_q_bwd_ln1", "nt", tm=512, epi=ln_bwd_epi,
                                        extras=(dr2, s1), pextras=(g1, b1), outs=row_and_sums)
    gW["w_o"] = _mm(mixin, dr1, "mix_o_bwd_w", "tn", m=D_MODEL, n=D_MODEL)

    def gate_bwd_epi(acc, *tiles):
        _, vjp = jax.vjp(_gate_mix, *[t.astype(F32) for t in tiles])
        return vjp(acc)

    dgs, dga, dy2a, dy2b, da_out = _mm(
        dr1, W["w_o"], "mix_o_bwd_gate", "nt", tm=512, epi=gate_bwd_epi, outs=[(D_MODEL, BF16)] * 5,
        rextras=[(z, Z_GS, D_MODEL), (z, Z_GA, D_MODEL), (y2, 0, D_MODEL), (y2, D_MODEL, D_MODEL), _whole(a_out)])

    def delta_epi(acc, o_):
        prod = acc * o_
        cols = [jnp.sum(prod[:, h * HP:(h + 1) * HP], axis=1, keepdims=True) for h in range(HEADS)]
        return acc, jnp.concatenate(cols, axis=1)

    do16, delta = _mm(da_out, w_oa_p, "mla_oa_bwd_a", "nt", epi=delta_epi, extras=(o,),
                      outs=[(HEADS * HP, BF16), (HEADS, F32)], tm=512)
    g_oa_p = _mm(o, da_out, "mla_oa_bwd_w", "tn", m=HEADS * HP, n=D_MODEL)
    gW["w_oa"] = g_oa_p.reshape(HEADS, HP, D_MODEL)[:, :VD].reshape(HEADS * VD, D_MODEL)
    delta_row = delta.T.reshape(HEADS, 1, T)
    if overlap is None:
        dq, dk, dv = _flash_bwd(q, k, v, do16, lse, delta_row, "mla_attn_bwd")
        riding = None
    else:
        sent = overlap[2]({n: gW[n] for n in BIG_LATE})
        dq, dk, dv, landed = _flash_bwd(q, k, v, do16, lse, delta_row, "mla_attn_bwd", carry=(sent, True))
        riding = (sent, landed)

    def rope_bwd(dq_, dk_, tc_, t1, t2):
        dqr = _rope_t(dq_, _tile8(tc_), _tile8(t1), _tile8(t2))
        dkr = dk_[:, 0:HP]
        for hh in range(1, HEADS):
            dkr = dkr + dk_[:, hh * HP:(hh + 1) * HP]
        dkr = _rope_t(jnp.where(_rope_lanes(dkr.shape), dkr, 0.0), tc_, t1, t2)
        dkr = pltpu.roll(dkr, NOPE, 1)
        lane = lax.broadcasted_iota(jnp.int32, dkr.shape, 1)
        return dqr, jnp.where(lane < ROPE, dkr, 0.0)

    dq_raw, dkr = _rowwise(rope_bwd, "mla_rope_bwd", [_whole(dq), _whole(dk), _whole(t_cos), _whole(t_s1), _whole(t_s2)], [],
                           [(HEADS * HP, BF16), (HP, BF16)], tm=512)

    def rms_bwd_epi(acc, *rest):
        _, vjp = jax.vjp(_rms, rest[-2].astype(F32), rest[-1])
        return vjp(acc + rest[0] if len(rest) == 3 else acc)

    row_and_sum = [(S5_W, BF16), (S5_W, "sum")]
    dcq, gs["q_norm_g"] = _mm(dq_raw, w_uq_p, "mla_uq_bwd_norm", "nt", epi=rms_bwd_epi, rextras=[(z, Z_CQ, S5_W)],
                              pextras=(gq,), outs=row_and_sum)
    gW["w_uq"] = _unpad_heads(_mm(cqn, dq_raw, "mla_uq_bwd_w", "tn", m=S5_W, n=HEADS * HP), QK)
    dckvn_k = _mm(dk, w_uk_p, "mla_uk_bwd_a", "nt")
    dckv, gs["kv_norm_g"] = _mm(dv, w_uv_p, "mla_uv_bwd_norm", "nt", epi=rms_bwd_epi, extras=(dckvn_k,),
                                rextras=[(z, Z_CKV, S5_W)], pextras=(gkv,), outs=row_and_sum)
    g_uk = _unpad_heads(_mm(ckvn, dk, "mla_uk_bwd_w", "tn", m=S5_W, n=HEADS * HP), NOPE).reshape(S5_W, HEADS, NOPE)
    g_uv = _unpad_heads(_mm(ckvn, dv, "mla_uv_bwd_w", "tn", m=S5_W, n=HEADS * HP), VD).reshape(S5_W, HEADS, VD)
    gW["w_ukv"] = jnp.concatenate([g_uk, g_uv], axis=2).reshape(S5_W, HEADS * (NOPE + VD))

    dy2 = jnp.concatenate([dy2a, dy2b], axis=1)
    dyg = _mm(dy2, W["w_glu"], "glu_bwd_a", "nt")
    gW["w_glu"] = _mm(yg, dy2, "glu_bwd_w", "tn", m=S5_W, n=2 * D_MODEL)

    du, gs["s5_ab"], g_bst, g_cst_t, gs["s5_d"] = _s5_bwd(dyg, yc, z, hs, b_st, c_st, d_skip, tab_rev, "s5_bwd")
    gs["s5_bb_re"] = jnp.swapaxes(_blockdiag_t(g_bst[:, :S5_N], S5_H, S5_P), 1, 2)
    gs["s5_bb_im"] = jnp.swapaxes(_blockdiag_t(g_bst[:, S5_N:], S5_H, S5_P), 1, 2)
    gs["s5_c_re"] = _blockdiag_t(g_cst_t[:, :S5_N], S5_H, S5_P)
    gs["s5_c_im"] = -_blockdiag_t(g_cst_t[:, S5_N:], S5_H, S5_P)

    dz = jnp.concatenate([dgs, dga, du, dcq, dckv, dkr], axis=1)
    g_in_re = _mm(h0, dz, "in_proj_bwd_w", "tn", m=D_MODEL, n=Z_W, tm=512, tn=Z_W)
    gW["w_in"] = jnp.concatenate([g_in_re[:, Z_U:Z_KR], g_in_re[:, Z_KR:Z_KR + ROPE], g_in_re[:, :Z_U]], axis=1)
    dx, gs["ln_in_g"], gs["ln_in_b"] = _mm(dz, w_in_re, "in_proj_bwd_ln_in", "nt", tm=512, tk=Z_W, epi=ln_bwd_epi,
                                           extras=(dr1, x), pextras=(g_in, b_in), outs=row_and_sums)
    return lossv[:, :1], dx, gW, gs, riding


_RAW_SMALL = (("loss", (1, 1)), ("ln_in_g", (1, D_MODEL)), ("ln_in_b", (1, D_MODEL)), ("ln1_g", (1, D_MODEL)),
              ("ln1_b", (1, D_MODEL)), ("ln2_g", (1, D_MODEL)), ("ln2_b", (1, D_MODEL)), ("ln3_g", (1, D_MODEL)),
              ("ln3_b", (1, D_MODEL)), ("q_norm_g", (1, S5_W)), ("kv_norm_g", (1, S5_W)), ("s5_d", (1, S5_W)),
              ("s5_ab", (1, 2 * S5_N)), ("s5_bb_re", (S5_G, S5_P, S5_H)), ("s5_bb_im", (S5_G, S5_P, S5_H)),
              ("s5_c_re", (S5_G, S5_H, S5_P)), ("s5_c_im", (S5_G, S5_H, S5_P)))


def kernel(x, mem, positions, ln_in_g, ln_in_b, w_in, s5_lam_re, s5_lam_im, s5_log_dt, s5_b_re, s5_b_im, s5_c_re, s5_c_im, s5_d, w_glu, q_norm_g, w_uq, kv_norm_g, w_ukv, w_oa, w_o, ln1_g, ln1_b, w_xq, w_xk, w_xv, w_xo, ln2_g, ln2_b, w_up, w_down, ln3_g, ln3_b, loss_target, m_ln_in_g, m_ln_in_b, m_w_in, m_s5_lam_re, m_s5_lam_im, m_s5_log_dt, m_s5_b_re, m_s5_b_im, m_s5_c_re, m_s5_c_im, m_s5_d, m_w_glu, m_q_norm_g, m_w_uq, m_kv_norm_g, m_w_ukv, m_w_oa, m_w_o, m_ln1_g, m_ln1_b, m_w_xq, m_w_xk, m_w_xv, m_w_xo, m_ln2_g, m_ln2_b, m_w_up, m_w_down, m_ln3_g, m_ln3_b, v_ln_in_g, v_ln_in_b, v_w_in, v_s5_lam_re, v_s5_lam_im, v_s5_log_dt, v_s5_b_re, v_s5_b_im, v_s5_c_re, v_s5_c_im, v_s5_d, v_w_glu, v_q_norm_g, v_w_uq, v_kv_norm_g, v_w_ukv, v_w_oa, v_w_o, v_ln1_g, v_ln1_b, v_w_xq, v_w_xk, v_w_xv, v_w_xo, v_ln2_g, v_ln2_b, v_w_up, v_w_down, v_ln3_g, v_ln3_b):
    a = dict(locals())
    wts = {n: a[n] for n in WEIGHTS}
    ms = {n: a["m_" + n] for n in WEIGHTS}
    vs = {n: a["v_" + n] for n in WEIGHTS}
    shard2d = {n: wts[n].reshape(wts[n].shape[-2], wts[n].shape[-1]) for n in BIG}
    nrow = {n: shard2d[n].size // 1024 for n in BIG}
    mine = 2 * lax.axis_index("x") + lax.axis_index("y")
    my_c = lax.axis_index("c")
    cidx = my_c.astype(jnp.int32).reshape(1)

    def group_rows(group):
        rows = sum(nrow[n] for n in group)
        return rows, (-rows) % PACK_ROWS

    def pack_shards(group):
        rows, pad = group_rows(group)
        parts = [shard2d[n].astype(BF16).reshape(nrow[n], 1024) for n in group] + [jnp.zeros((pad, 1024), BF16)]
        return jnp.concatenate(parts, axis=0).reshape(2, (rows + pad) // 2, 1024)

    def unpack_full(group, gathered):
        out, off = {}, 0
        for n in group:
            r, cc = shard2d[n].shape
            piece = lax.optimization_barrier(gathered[:, off:off + nrow[n]])
            out[n] = _full_from_shards(n, piece.reshape(4, r, cc))
            off += nrow[n]
        return out

    def pack_grads(group, gw):
        rows, pad = group_rows(group)
        parts = [_shards_from_full(n, gw[n].astype(BF16)).reshape(4, nrow[n], 1024) for n in group]
        return jnp.concatenate(parts + [jnp.zeros((4, pad, 1024), BF16)], axis=1).reshape(4, 2, (rows + pad) // 2, 1024)

    def by_core(own, other):
        return jnp.where(my_c == 0, jnp.stack([own, other]), jnp.stack([other, own]))

    packed_e, packed_l = pack_shards(BIG_EARLY), pack_shards(BIG_LATE)
    gathered = lax.dynamic_update_slice(_gather_weights(packed_e), packed_e[None], (mine, 0, 0, 0))
    W = unpack_full(BIG_EARLY, gathered.reshape(4, -1, 1024))

    def finish_gather(landed):
        half = packed_l.shape[1]
        own_half = lax.dynamic_index_in_dim(packed_l, my_c, axis=0, keepdims=True)
        landed = lax.dynamic_update_slice(landed, own_half, (mine, 0, 0))
        other = _pair_share(landed.reshape(4 * half, 1024), "gather_late_share").reshape(4, half, 1024)
        out, off = {}, 0
        for n in BIG_LATE:
            r, cc = shard2d[n].shape
            h, lo = divmod(off, half)
            assert lo + nrow[n] <= half, n
            piece = jnp.where(my_c == h, landed[:, lo:lo + nrow[n]], other[:, lo:lo + nrow[n]])
            out[n] = _full_from_shards(n, lax.optimization_barrier(piece).reshape(4, r, cc))
            off += nrow[n]
        return out

    def pair_reduce(group, gw, tag):
        gflat = pack_grads(group, gw)
        return _pair_add(gflat, _pair_exchange(gflat, "grad_pair_exchange_" + tag), cidx, "grad_pair_add_" + tag)

    def finish_reduce(group, sent, landed, tag):
        own = lax.dynamic_slice(sent, (mine, 0, 0), (1,) + sent.shape[1:])
        red = _chip_sum(lax.dynamic_update_slice(landed, own, (mine, 0, 0)), "grad_chip_sum_" + tag)
        gsh = by_core(red, _pair_share(red, "grad_pair_share_" + tag)).reshape(-1, 1024)
        out, off = {}, 0
        for n in group:
            out[n] = lax.optimization_barrier(gsh[off:off + nrow[n]]).reshape(shard2d[n].shape)
            off += nrow[n]
        return out

    sp = {n: wts[n] for n in SMALL}
    sp_local = {n: (sp[n][0] if sp[n].ndim > 1 else sp[n]) for n in SMALL}
    overlap = (packed_l, finish_gather, lambda gw: pair_reduce(BIG_LATE, gw, "late"))
    lossv, dx, gW, gs, riding = _local_step(x[0], mem[0], positions[0], loss_target[0], W, sp_local, overlap)

    g_out = finish_reduce(BIG_LATE, riding[0], riding[1], "late")
    sent_e = pair_reduce(BIG_EARLY, gW, "early")
    g_out.update(finish_reduce(BIG_EARLY, sent_e, _chip_exchange(sent_e, "grad_chip_exchange_early"), "early"))

    gs["loss"] = lossv
    raw = _pack_flat([gs[n].reshape(s) for n, s in _RAW_SMALL], 8 * 1024, F32)
    raw = _allreduce_small(raw.reshape(-1, 1024)).reshape(-1)
    rs = dict(zip([n for n, _ in _RAW_SMALL], _unpack_flat(raw, [s for _, s in _RAW_SMALL])))
    loss = rs["loss"].reshape(())
    _, disc_vjp = jax.vjp(_s5_discretize, sp_local["s5_lam_re"], sp_local["s5_lam_im"], sp_local["s5_log_dt"],
                          sp_local["s5_b_re"], sp_local["s5_b_im"])
    d_ab = rs["s5_ab"].reshape(2, S5_G, S5_P)
    g_lre, g_lim, g_ldt, g_bre, g_bim = disc_vjp((d_ab[0], d_ab[1], rs["s5_bb_re"], rs["s5_bb_im"]))
    small_g = {"s5_lam_re": g_lre, "s5_lam_im": g_lim, "s5_log_dt": g_ldt, "s5_b_re": g_bre, "s5_b_im": g_bim,
               "s5_c_re": rs["s5_c_re"], "s5_c_im": rs["s5_c_im"]}
    for n in ("ln_in_g", "ln_in_b", "ln1_g", "ln1_b", "ln2_g", "ln2_b", "ln3_g", "ln3_b", "q_norm_g", "kv_norm_g", "s5_d"):
        small_g[n] = rs[n]

    grads, deltas, new_m, new_v = {}, {}, {}, {}
    for n in BIG:
        d_, m_, v_ = _adamw(shard2d[n], g_out[n], ms[n].reshape(shard2d[n].shape), vs[n].reshape(shard2d[n].shape), "adamw_" + n)
        grads[n] = g_out[n].reshape(wts[n].shape)
        deltas[n], new_m[n], new_v[n] = (t.reshape(wts[n].shape) for t in (d_, m_, v_))
    as2d = lambda t: t.reshape(-1, t.shape[-1])
    sd, sm, sv = _adamw_many([as2d(wts[n]) for n in SMALL], [as2d(small_g[n].reshape(wts[n].shape)) for n in SMALL],
                             [as2d(ms[n]) for n in SMALL], [as2d(vs[n]) for n in SMALL], "adamw_small")
    for n, d_, m_, v_ in zip(SMALL, sd, sm, sv):
        grads[n] = small_g[n].reshape(wts[n].shape)
        deltas[n], new_m[n], new_v[n] = (t.reshape(wts[n].shape) for t in (d_, m_, v_))

    return (loss, dx[None], *[grads[n] for n in WEIGHTS], *[deltas[n] for n in WEIGHTS],
            *[new_m[n] for n in WEIGHTS], *[new_v[n] for n in WEIGHTS])
```

```python
import functools
import math

import jax
import jax.numpy as jnp
from jax import lax
from jax.experimental import pallas as pl
from jax.experimental.pallas import tpu as pltpu

F32 = jnp.float32
BF16 = jnp.bfloat16
MESH = pl.DeviceIdType.MESH

D_MODEL = 1024
S5_W = 256
S5_G = 16
S5_H = 16
S5_P = 64
S5_N = S5_G * S5_P
S5_MAX_RE = -1e-4
HEADS = 8
NOPE = 64
ROPE = 32
QK = NOPE + ROPE
VD = 64
HP = 128
XH = 4
XD = 256
LN_EPS = 1e-5
RMS_EPS = 1e-6
NEG_INF = -1e30
DN_ALPHA = 2.0 ** 0.25
ROPE_THETA = 10000.0
ADAM_LR, ADAM_B1, ADAM_B2, ADAM_EPS, ADAM_WD, ADAM_STEP = 0.001, 0.9, 0.999, 1e-08, 0.01, 10

Z_GS, Z_GA, Z_U, Z_CQ, Z_CKV, Z_KR, Z_W = 0, 1024, 2048, 2304, 2560, 2816, 2944

BIG_EARLY = ("w_in", "w_glu", "w_uq", "w_ukv", "w_oa", "w_o")
BIG_LATE = ("w_up", "w_xq", "w_xk", "w_xv", "w_xo", "w_down")
BIG = BIG_EARLY + BIG_LATE
COL_SHARDED = ("w_in", "w_glu", "w_uq", "w_ukv", "w_oa", "w_up")
SMALL = ("ln_in_g", "ln_in_b", "s5_lam_re", "s5_lam_im", "s5_log_dt", "s5_b_re", "s5_b_im", "s5_c_re", "s5_c_im",
         "s5_d", "q_norm_g", "kv_norm_g", "ln1_g", "ln1_b", "ln2_g", "ln2_b", "ln3_g", "ln3_b")
WEIGHTS = ("ln_in_g", "ln_in_b", "w_in", "s5_lam_re", "s5_lam_im", "s5_log_dt", "s5_b_re", "s5_b_im", "s5_c_re",
           "s5_c_im", "s5_d", "w_glu", "q_norm_g", "w_uq", "kv_norm_g", "w_ukv", "w_oa", "w_o", "ln1_g", "ln1_b",
           "w_xq", "w_xk", "w_xv", "w_xo", "ln2_g", "ln2_b", "w_up", "w_down", "ln3_g", "ln3_b")
PACK_ROWS = 2 * 16


def _row_tile(n, cap=512):
    best = n
    for t in range(8, min(n, cap) + 1, 8):
        if n % t == 0:
            best = t
    return best


def _pick(n, cap):
    best = None
    for t in range(128, min(n, cap) + 1, 128):
        if n % t == 0:
            best = t
    return best if best is not None and (best >= 512 or best == n) else n


def _mm(a, b, name, mode="nn", *, a_off=0, b_off=0, m=None, n=None, act=None, epi=None, extras=(), rextras=(),
        pextras=(), outs=None, out_dtype=None, tm=1024, tn=1024, tk=None):
    out_dtype = out_dtype or (BF16 if mode == "tn" else F32)
    if mode == "nn":
        M, (K, N) = a.shape[0], b.shape
    elif mode == "nt":
        M, (N, K) = a.shape[0], b.shape
    else:
        K, M, N = a.shape[0], m, n
    if mode == "tn":
        tm, tn, tk = _pick(M, tm), _pick(N, tn), min(tk or 1024, K)
    else:
        tm, tn, tk = min(tm, M), _pick(N, tn), _pick(K, tk or 1024)
    assert M % tm == 0 and N % tn == 0 and K % tk == 0, (name, M, N, K, tm, tn, tk)
    nk = K // tk
    if mode == "tn":
        assert a_off % tm == 0 and b_off % tn == 0
        ao, bo = a_off // tm, b_off // tn
        a_spec = pl.BlockSpec((tk, tm), lambda i, j, k: (k, i + ao))
        b_spec = pl.BlockSpec((tk, tn), lambda i, j, k: (k, j + bo))
        dims = (((0,), (0,)), ((), ()))
    else:
        assert a_off % tk == 0
        ao = a_off // tk
        a_spec = pl.BlockSpec((tm, tk), lambda i, j, k: (i, k + ao))
        if mode == "nn":
            b_spec = pl.BlockSpec((tk, tn), lambda i, j, k: (k, j))
            dims = (((1,), (0,)), ((), ()))
        else:
            b_spec = pl.BlockSpec((tn, tk), lambda i, j, k: (j, k))
            dims = (((1,), (1,)), ((), ()))
    e_spec = pl.BlockSpec((tm, tn), lambda i, j, k: (i, j))
    r_specs = [pl.BlockSpec((tm, w), functools.partial(lambda i, j, k, o: (i, o), o=off // w)) for _, off, w in rextras]
    p_specs = [pl.BlockSpec((1, tn), lambda i, j, k: (0, j)) if p.shape[1] == N else pl.BlockSpec(p.shape, lambda i, j, k: (0, 0))
               for p in pextras]
    n_extra = len(extras) + len(rextras) + len(pextras)
    if outs is None:
        o_specs, o_shapes = [e_spec], [jax.ShapeDtypeStruct((M, N), out_dtype)]
    else:
        assert all((w == tn and dt != "sum") or N == tn for w, dt in outs), name
        o_specs = [e_spec if w == tn and dt != "sum" else pl.BlockSpec((8 if dt == "sum" else tm, w), lambda i, j, k: (i, 0))
                   for w, dt in outs]
        o_shapes = [jax.ShapeDtypeStruct((M // tm * 8, w), F32) if dt == "sum" else
                    jax.ShapeDtypeStruct((M, N if w == tn else w), dt) for w, dt in outs]
    n_out = len(o_specs)
    direct = outs is None and epi is None and out_dtype == F32

    def body(*refs):
        a_ref, b_ref = refs[0], refs[1]
        e_refs = refs[2:2 + n_extra]
        o_refs = refs[2 + n_extra:2 + n_extra + n_out]
        av = a_ref[...]
        if act is not None:
            av = act(av.astype(F32))
        p = lax.dot_general(av.astype(BF16), b_ref[...].astype(BF16), dims, preferred_element_type=F32)

        def finish(r):
            if epi is not None:
                r = epi(r, *[e[...] for e in e_refs])
            for o_ref, v in zip(o_refs, r if isinstance(r, tuple) else (r,)):
                o_ref[...] = jnp.broadcast_to(v, o_ref.shape).astype(o_ref.dtype)

        if nk == 1:
            finish(p)
        else:
            acc = o_refs[0] if direct else refs[2 + n_extra + n_out]
            k = pl.program_id(2)

            @pl.when(k == 0)
            def _():
                acc[...] = p

            @pl.when(k > 0)
            def _():
                acc[...] += p

            if not direct:
                @pl.when(k == nk - 1)
                def _():
                    finish(acc[...])

    res = pl.pallas_call(
        body, name=name, grid=(M // tm, N // tn, nk),
        in_specs=[a_spec, b_spec] + [e_spec] * len(extras) + r_specs + p_specs, out_specs=o_specs,
        out_shape=o_shapes,
        scratch_shapes=[pltpu.VMEM((tm, tn), F32)] if nk > 1 and not direct else [],
        compiler_params=pltpu.CompilerParams(dimension_semantics=("parallel", "parallel", "arbitrary")),
    )(a, b, *extras, *[r[0] for r in rextras], *pextras)
    if outs is None:
        return res[0]
    return [r.reshape(M // tm, 8, -1)[:, 0].sum(axis=0, keepdims=True) if dt == "sum" else r for r, (_, dt) in zip(res, outs)]


def _rowwise(fn, name, rows, pars, outs, accs=(), tm=256, upcast=True):
    T = rows[0][0].shape[0]
    tm = min(tm, T)
    assert T % tm == 0
    n_in, n_out = len(rows) + len(pars), len(outs)
    in_specs = []
    for arr, off, w in rows:
        assert off % w == 0 and arr.shape[0] == T, name
        in_specs.append(pl.BlockSpec((tm, w), functools.partial(lambda i, o: (i, o), o=off // w)))
    for p in pars:
        in_specs.append(pl.BlockSpec(p.shape, lambda i: (0, 0)))
    out_specs = [pl.BlockSpec((tm, w), lambda i: (i, 0)) for w, _ in outs]
    out_specs += [pl.BlockSpec(s, lambda i: (0, 0)) for s in accs]
    out_shape = [jax.ShapeDtypeStruct((T, w), dt) for w, dt in outs] + [jax.ShapeDtypeStruct(s, F32) for s in accs]

    def body(*refs):
        res = fn(*[r[...].astype(F32) if upcast else r[...] for r in refs[:n_in]])
        o_refs = refs[n_in:]
        for r, v in zip(o_refs[:n_out], res[:n_out]):
            r[...] = v.astype(r.dtype)
        if accs:
            i = pl.program_id(0)

            @pl.when(i == 0)
            def _():
                for r in o_refs[n_out:]:
                    r[...] = jnp.zeros_like(r)

            for r, v in zip(o_refs[n_out:], res[n_out:]):
                r[...] += v

    res = pl.pallas_call(
        body, name=name, grid=(T // tm,), in_specs=in_specs, out_specs=out_specs, out_shape=out_shape,
        compiler_params=pltpu.CompilerParams(dimension_semantics=("arbitrary",)),
    )(*[r[0] for r in rows], *pars)
    return res


def _whole(arr):
    return (arr, 0, arr.shape[1])


def _ln(x, g, b):
    mu = jnp.mean(x, axis=-1, keepdims=True)
    xc = x - mu
    var = jnp.mean(xc * xc, axis=-1, keepdims=True)
    return xc * lax.rsqrt(var + LN_EPS) * g + b


def _ln_res(h, r, g, b):
    return _ln(DN_ALPHA * h + r, g, b)


def _rms(x, g):
    return x * lax.rsqrt(jnp.mean(x * x, axis=-1, keepdims=True) + RMS_EPS) * g


def _gelu_skip(yc, u, d):
    y = yc + d * u
    return 0.5 * y * (1.0 + lax.erf(y * (1.0 / math.sqrt(2.0))))


def _gate_mix(gs, ga, y2a, y2b, aout):
    return jax.nn.sigmoid(gs) * (y2a * jax.nn.sigmoid(y2b)) + jax.nn.sigmoid(ga) * aout


def _relu2(a):
    r = jnp.maximum(a, 0.0)
    return r * r


def _tile8(t):
    return jnp.concatenate([t] * HEADS, axis=1)


def _rope(x, cos, s1, s2):
    w = x.shape[1]
    return x * cos + pltpu.roll(x, ROPE // 2, 1) * s1 + pltpu.roll(x, w - ROPE // 2, 1) * s2


def _rope_t(dy, cos, s1, s2):
    w = dy.shape[1]
    return dy * cos + pltpu.roll(dy * s1, w - ROPE // 2, 1) + pltpu.roll(dy * s2, ROPE // 2, 1)


def _rope_lanes(shape):
    lane = lax.broadcasted_iota(jnp.int32, shape, 1) % HP
    return (lane >= NOPE) & (lane < QK)


ATT_HB_FWD, ATT_HB_BWD = 8, 8
ATT_BQ_FWD = 1024
ATT_SCALE = QK ** -0.5
LOG2E = 1.4426950408889634
LN2 = 0.6931471805599453
_NT = (((1,), (1,)), ((), ()))
_TN = (((0,), (0,)), ((), ()))


def _carry_specs(carry):
    if carry is None:
        return [], [], [], []
    arr = carry[0]
    return ([_HBM], [_HBM], [jax.ShapeDtypeStruct((4,) + arr.shape[1:], arr.dtype)],
            [pltpu.SemaphoreType.DMA((3,)), pltpu.SemaphoreType.DMA((3,))])


def _carry_start(exchange, when):
    @pl.when(when)
    def _():
        for cp in exchange(False):
            cp.start()


def _carry_wait(exchange, when):
    @pl.when(when)
    def _():
        for cp in exchange(True):
            cp.wait_recv()
        for cp in exchange(False):
            cp.wait_send()


def _tri_tables(n, by_row):
    if by_row:
        pairs = [(i, j) for i in range(n) for j in range(i + 1)]
    else:
        pairs = [(i, j) for j in range(n) for i in range(j, n)]
    return jnp.array([p[0] for p in pairs], jnp.int32), jnp.array([p[1] for p in pairs], jnp.int32)


def _flash_fwd(q, k, v, name, bq=512, carry=None):
    T = q.shape[0]
    bq = min(bq, T)
    nq = T // bq
    hb = ATT_HB_FWD
    n_carry = 0 if carry is None else 1

    i_tab, j_tab = _tri_tables(nq, by_row=True)

    def body(it_ref, jt_ref, q_ref, k_ref, v_ref, *rest):
        o_ref, lse_ref = rest[n_carry:n_carry + 2]
        m_sc, l_sc, acc_sc = rest[2 * n_carry + 2:2 * n_carry + 5]
        t = pl.program_id(1)
        i, j = it_ref[t], jt_ref[t]
        if carry is not None:
            exchange = functools.partial(_carried_exchange, rest[0], rest[3], rest[-2], rest[-1], carry[1])
            _carry_start(exchange, (pl.program_id(0) == 0) & (t == 0))

        @pl.when(j == 0)
        def _():
            m_sc[...] = jnp.full_like(m_sc, NEG_INF)
            l_sc[...] = jnp.zeros_like(l_sc)
            acc_sc[...] = jnp.zeros_like(acc_sc)

        def step(masked):
            for hh in range(hb):
                sl = slice(hh * HP, (hh + 1) * HP)
                s = lax.dot_general(q_ref[:, sl], k_ref[:, sl], _NT, preferred_element_type=F32)
                if masked:
                    row = lax.broadcasted_iota(jnp.int32, (bq, bq), 0)
                    col = lax.broadcasted_iota(jnp.int32, (bq, bq), 1)
                    s = jnp.where(col <= row, s, NEG_INF)
                m_prev = m_sc[hh]
                m_new = jnp.maximum(m_prev, jnp.max(s, axis=1, keepdims=True))
                alpha = jnp.exp2(m_prev - m_new)
                p = jnp.exp2(s - jnp.concatenate([m_new] * (bq // HP), axis=1))
                l_sc[hh] = alpha * l_sc[hh] + jnp.sum(p, axis=1, keepdims=True)
                acc_sc[hh] = alpha * acc_sc[hh] + jnp.dot(p.astype(BF16), v_ref[:, sl], preferred_element_type=F32)
                m_sc[hh] = m_new

        @pl.when(j < i)
        def _():
            step(False)

        @pl.when(j == i)
        def _():
            step(True)
            for hh in range(hb):
                o_ref[:, hh * HP:(hh + 1) * HP] = acc_sc[hh] / l_sc[hh]
                lse_ref[hh] = jnp.transpose((m_sc[hh] + jnp.log2(l_sc[hh])) * LN2)[0:1, :]

        if carry is not None:
            _carry_wait(exchange, (pl.program_id(0) == HEADS // hb - 1) & (t == i_tab.shape[0] - 1))

    qs = pl.BlockSpec((bq, hb * HP), lambda h, t, it, jt: (it[t], h))
    ks = pl.BlockSpec((bq, hb * HP), lambda h, t, it, jt: (jt[t], h))
    c_in, c_out, c_shape, c_sems = _carry_specs(carry)
    gs = pltpu.PrefetchScalarGridSpec(
        num_scalar_prefetch=2, grid=(HEADS // hb, i_tab.shape[0]), in_specs=[qs, ks, ks] + c_in,
        out_specs=[qs, pl.BlockSpec((hb, 1, bq), lambda h, t, it, jt: (h, 0, it[t]))] + c_out,
        scratch_shapes=[pltpu.VMEM((hb, bq, HP), F32)] * 3 + c_sems)
    return pl.pallas_call(
        body, name=name, grid_spec=gs,
        out_shape=[jax.ShapeDtypeStruct((T, HEADS * HP), F32), jax.ShapeDtypeStruct((HEADS, 1, T), F32)] + c_shape,
        compiler_params=pltpu.CompilerParams(dimension_semantics=("arbitrary", "arbitrary")),
    )(i_tab, j_tab, q, k, v, *([] if carry is None else [carry[0]]))


def _flash_bwd(q, k, v, do, lse_row, delta_row, name, bq=512, carry=None):
    T = q.shape[0]
    bq = min(bq, T)
    nq = T // bq
    hb = ATT_HB_BWD
    n_carry = 0 if carry is None else 1

    i_tab, j_tab = _tri_tables(nq, by_row=False)
    n_blk = i_tab.shape[0]

    def body(it_ref, jt_ref, q_ref, k_ref, v_ref, do_ref, lse_ref, dl_ref, *rest):
        dq_ref, dk_ref, dv_ref = rest[n_carry:n_carry + 3]
        t = pl.program_id(1)
        i, j = it_ref[t], jt_ref[t]
        if carry is not None:
            exchange = functools.partial(_carried_exchange, rest[0], rest[4], rest[-2], rest[-1], carry[1])
            _carry_start(exchange, (pl.program_id(0) == 0) & (t == 0))

        @pl.when(t == 0)
        def _():
            dq_ref[...] = jnp.zeros_like(dq_ref)

        @pl.when(i == j)
        def _():
            dk_ref[...] = jnp.zeros_like(dk_ref)
            dv_ref[...] = jnp.zeros_like(dv_ref)

        def step(masked):
            rows = pl.ds(pl.multiple_of(i * bq, bq), bq)
            for hh in range(hb):
                sl = slice(hh * HP, (hh + 1) * HP)
                qh, kh, doh = q_ref[:, sl], k_ref[:, sl], do_ref[:, sl]
                st = lax.dot_general(kh, qh, _NT, preferred_element_type=F32)
                pt = jnp.exp2(st - lse_ref[hh] * LOG2E)
                if masked:
                    krow = lax.broadcasted_iota(jnp.int32, (bq, bq), 0)
                    qcol = lax.broadcasted_iota(jnp.int32, (bq, bq), 1)
                    pt = jnp.where(krow <= qcol, pt, 0.0)
                dv_ref[:, sl] += jnp.dot(pt.astype(BF16), doh, preferred_element_type=F32)
                dpt = lax.dot_general(v_ref[:, sl], doh, _NT, preferred_element_type=F32)
                dst = (pt * (dpt - dl_ref[hh])).astype(BF16)
                dk_ref[:, sl] += jnp.dot(dst, qh, preferred_element_type=F32)
                dq_ref[rows, sl] += lax.dot_general(dst, kh, _TN, preferred_element_type=F32)

        @pl.when(i > j)
        def _():
            step(False)

        @pl.when(i == j)
        def _():
            step(True)

        @pl.when(i == nq - 1)
        def _():
            dk_ref[...] *= LN2

        @pl.when(t == n_blk - 1)
        def _():
            dq_ref[...] *= ATT_SCALE

        if carry is not None:
            _carry_wait(exchange, (pl.program_id(0) == HEADS // hb - 1) & (t == n_blk - 1))

    qs = pl.BlockSpec((bq, hb * HP), lambda h, t, it, jt: (it[t], h))
    ks = pl.BlockSpec((bq, hb * HP), lambda h, t, it, jt: (jt[t], h))
    rs = pl.BlockSpec((hb, 1, bq), lambda h, t, it, jt: (h, 0, it[t]))
    full = pl.BlockSpec((T, hb * HP), lambda h, t, it, jt: (0, h), pipeline_mode=pl.Buffered(1))
    c_in, c_out, c_shape, c_sems = _carry_specs(carry)
    gs = pltpu.PrefetchScalarGridSpec(num_scalar_prefetch=2, grid=(HEADS // hb, n_blk),
                                      in_specs=[qs, ks, ks, qs, rs, rs] + c_in, out_specs=[full, ks, ks] + c_out,
                                      scratch_shapes=c_sems)
    return pl.pallas_call(
        body, name=name, grid_spec=gs, out_shape=[jax.ShapeDtypeStruct((T, HEADS * HP), F32)] * 3 + c_shape,
        compiler_params=pltpu.CompilerParams(dimension_semantics=("arbitrary", "arbitrary")),
    )(i_tab, j_tab, q, k, v, do, lse_row, delta_row, *([] if carry is None else [carry[0]]))


def _xattn_heads(q, k, v):
    scale = XD ** -0.5
    ps = []
    for h in range(XH):
        sl = slice(h * XD, (h + 1) * XD)
        s = lax.dot_general(q[:, sl].astype(BF16), k[:, sl].astype(BF16), (((1,), (1,)), ((), ())),
                            preferred_element_type=F32) * scale
        e = jnp.exp(s - jnp.max(s, axis=1, keepdims=True))
        ps.append(e / jnp.sum(e, axis=1, keepdims=True))
    return ps


def _xattn_fwd_fn(q, k, v):
    ps = _xattn_heads(q, k, v)
    o = [jnp.dot(p.astype(BF16), v[:, h * XD:(h + 1) * XD].astype(BF16), preferred_element_type=F32)
         for h, p in enumerate(ps)]
    return (jnp.concatenate(o, axis=1),)


def _xattn_bwd_fn(q, do, k, v):
    scale = XD ** -0.5
    ps = _xattn_heads(q, k, v)
    dqs, dks, dvs = [], [], []
    tdims = (((0,), (0,)), ((), ()))
    for h, p in enumerate(ps):
        sl = slice(h * XD, (h + 1) * XD)
        doh = do[:, sl].astype(BF16)
        dvs.append(lax.dot_general(p.astype(BF16), doh, tdims, preferred_element_type=F32))
        dp = lax.dot_general(doh, v[:, sl].astype(BF16), (((1,), (1,)), ((), ())), preferred_element_type=F32)
        ds = (p * (dp - jnp.sum(dp * p, axis=1, keepdims=True)) * scale).astype(BF16)
        dqs.append(jnp.dot(ds, k[:, sl].astype(BF16), preferred_element_type=F32))
        dks.append(lax.dot_general(ds, q[:, sl].astype(BF16), tdims, preferred_element_type=F32))
    return jnp.concatenate(dqs, axis=1), jnp.concatenate(dks, axis=1), jnp.concatenate(dvs, axis=1)


S5_SUB = 8


def _cmul_add(xr, xi, ar, ai, sr, si):
    return xr + ar * sr - ai * si, xi + ar * si + ai * sr


def _s5_tables(ar, ai, reverse):
    pows = [(ar, ai)]
    for _ in range(S5_SUB - 1):
        pr, pi = pows[-1]
        pows.append((pr * ar - pi * ai, pr * ai + pi * ar))
    cr = jnp.concatenate([p[0] for p in pows], axis=0)
    ci = jnp.concatenate([p[1] for p in pows], axis=0)
    if reverse:
        cr, ci = cr[::-1], ci[::-1]
    t = jnp.arange(S5_SUB)[:, None]
    blocks = [jnp.concatenate([cr, ci], axis=1)]
    for s in (1, 2, 4):
        keep = (t < S5_SUB - s) if reverse else (t >= s)
        sr, si = pows[s - 1]
        blocks.append(jnp.concatenate([jnp.where(keep, sr, 0.0), jnp.where(keep, si, 0.0)], axis=1))
    return jnp.concatenate(blocks, axis=0)


def _sub_scan(xr, xi, tab_ref, reverse):
    for k, s in enumerate((1, 2, 4)):
        blk = slice(S5_SUB * (k + 1), S5_SUB * (k + 2))
        sh = S5_SUB - s if reverse else s
        xr, xi = _cmul_add(xr, xi, tab_ref[blk, :S5_N], tab_ref[blk, S5_N:], pltpu.roll(xr, sh, 0), pltpu.roll(xi, sh, 0))
    return xr, xi


def _s5_fwd(z, b_st, c_st, d_skip, tab, name, tc=1024):
    T = z.shape[0]
    tc = min(tc, T)
    nsub = tc // S5_SUB

    def body(u_ref, bst_ref, cst_ref, d_ref, tab_ref, h_ref, yc_ref, yg_ref, carry, x_sc):
        @pl.when(pl.program_id(0) == 0)
        def _():
            carry[...] = jnp.zeros_like(carry)

        u = u_ref[...].astype(F32)
        x_sc[...] = jnp.dot(u.astype(BF16), bst_ref[...], preferred_element_type=F32)

        def sub(b, c):
            r = pl.ds(pl.multiple_of(b * S5_SUB, S5_SUB), S5_SUB)
            xr, xi = _sub_scan(x_sc[r, :S5_N], x_sc[r, S5_N:], tab_ref, False)
            hr, hi = _cmul_add(xr, xi, tab_ref[0:S5_SUB, :S5_N], tab_ref[0:S5_SUB, S5_N:], c[0], c[1])
            x_sc[r, :S5_N] = hr
            x_sc[r, S5_N:] = hi
            return hr[S5_SUB - 1:S5_SUB], hi[S5_SUB - 1:S5_SUB]

        cr, ci = lax.fori_loop(0, nsub, sub, (carry[0:1, :S5_N], carry[0:1, S5_N:]))
        carry[0:1, :S5_N] = cr
        carry[0:1, S5_N:] = ci
        h = x_sc[...]
        h_ref[...] = h
        yc = jnp.dot(h.astype(BF16), cst_ref[...], preferred_element_type=F32)
        yc_ref[...] = yc
        yg_ref[...] = _gelu_skip(yc, u, d_ref[...]).astype(yg_ref.dtype)

    whole = lambda a: pl.BlockSpec(a.shape, lambda i: (0, 0))
    row = lambda w: pl.BlockSpec((tc, w), lambda i: (i, 0))
    return pl.pallas_call(
        body, name=name, grid=(T // tc,),
        in_specs=[pl.BlockSpec((tc, S5_W), lambda i: (i, Z_U // S5_W)), whole(b_st), whole(c_st), whole(d_skip), whole(tab)],
        out_specs=[row(2 * S5_N), row(S5_W), row(S5_W)],
        out_shape=[jax.ShapeDtypeStruct((T, 2 * S5_N), F32), jax.ShapeDtypeStruct((T, S5_W), F32),
                   jax.ShapeDtypeStruct((T, S5_W), BF16)],
        scratch_shapes=[pltpu.VMEM((8, 2 * S5_N), F32), pltpu.VMEM((tc, 2 * S5_N), F32)],
        compiler_params=pltpu.CompilerParams(dimension_semantics=("arbitrary",)),
    )(z, b_st, c_st, d_skip, tab)


def _s5_bwd(dyg, yc, z, h, b_st, c_st, d_skip, tab, name, tc=1024):
    T = z.shape[0]
    tc = min(tc, T)
    nc, nsub = T // tc, tc // S5_SUB

    def body(dyg_ref, yc_ref, u_ref, h_ref, hp_ref, bst_ref, cst_ref, d_ref, tab_ref,
             du_ref, da_ref, db_ref, dct_ref, dd_ref, carry, x_sc):
        g = pl.program_id(0)

        @pl.when(g == 0)
        def _():
            carry[...] = jnp.zeros_like(carry)
            for r in (da_ref, db_ref, dct_ref, dd_ref):
                r[...] = jnp.zeros_like(r)

        u = u_ref[...].astype(F32)
        _, vjp = jax.vjp(_gelu_skip, yc_ref[...], u, d_ref[...])
        dyc, du_skip, dd = vjp(dyg_ref[...])
        dd_ref[...] += dd
        dyc16 = dyc.astype(BF16)
        dct_ref[...] += lax.dot_general(dyc16, h_ref[...].astype(BF16), _TN, preferred_element_type=F32)
        x_sc[...] = lax.dot_general(dyc16, cst_ref[...], _NT, preferred_element_type=F32)
        first = jnp.where(g == nc - 1, 0.0, 1.0)
        row0 = lax.broadcasted_iota(jnp.int32, (S5_SUB, S5_N), 0) == 0

        def sub(k, c):
            cr, ci, acc_r, acc_i = c
            b = nsub - 1 - k
            r = pl.ds(pl.multiple_of(b * S5_SUB, S5_SUB), S5_SUB)
            xr, xi = _sub_scan(x_sc[r, :S5_N], x_sc[r, S5_N:], tab_ref, True)
            lr, li = _cmul_add(xr, xi, tab_ref[0:S5_SUB, :S5_N], tab_ref[0:S5_SUB, S5_N:], cr, ci)
            x_sc[r, :S5_N] = lr
            x_sc[r, S5_N:] = li
            rp = pl.ds(pl.multiple_of(jnp.maximum(b - 1, 0) * S5_SUB, S5_SUB), S5_SUB)
            last_r = jnp.where(b == 0, hp_ref[S5_SUB - 1:S5_SUB, :S5_N] * first, h_ref[rp, :S5_N][S5_SUB - 1:S5_SUB])
            last_i = jnp.where(b == 0, hp_ref[S5_SUB - 1:S5_SUB, S5_N:] * first, h_ref[rp, S5_N:][S5_SUB - 1:S5_SUB])
            pr = jnp.where(row0, last_r, pltpu.roll(h_ref[r, :S5_N], 1, 0))
            pi = jnp.where(row0, last_i, pltpu.roll(h_ref[r, S5_N:], 1, 0))
            return lr[0:1], li[0:1], acc_r + (lr * pr + li * pi), acc_i + (li * pr - lr * pi)

        zero = jnp.zeros((S5_SUB, S5_N), F32)
        cr, ci, acc_r, acc_i = lax.fori_loop(0, nsub, sub, (carry[0:1, :S5_N], carry[0:1, S5_N:], zero, zero))
        carry[0:1, :S5_N] = cr
        carry[0:1, S5_N:] = ci
        da_ref[0:1, :S5_N] += jnp.sum(acc_r, axis=0, keepdims=True)
        da_ref[0:1, S5_N:] += jnp.sum(acc_i, axis=0, keepdims=True)
        lam16 = x_sc[...].astype(BF16)
        db_ref[...] += lax.dot_general(u.astype(BF16), lam16, _TN, preferred_element_type=F32)
        du = lax.dot_general(lam16, bst_ref[...], _NT, preferred_element_type=F32) + du_skip
        du_ref[...] = du.astype(du_ref.dtype)

    whole = lambda a: pl.BlockSpec(a.shape, lambda g: (0, 0))
    row = lambda w, off=0: pl.BlockSpec((tc, w), lambda g: (nc - 1 - g, off))
    prev = pl.BlockSpec((S5_SUB, 2 * S5_N), lambda g: (jnp.maximum((nc - 1 - g) * nsub - 1, 0), 0))
    acc = lambda s: pl.BlockSpec(s, lambda g: (0, 0))
    return pl.pallas_call(
        body, name=name, grid=(nc,),
        in_specs=[row(S5_W), row(S5_W), row(S5_W, Z_U // S5_W), row(2 * S5_N), prev, whole(b_st), whole(c_st),
                  whole(d_skip), whole(tab)],
        out_specs=[row(S5_W), acc((1, 2 * S5_N)), acc((S5_W, 2 * S5_N)), acc((S5_W, 2 * S5_N)), acc((1, S5_W))],
        out_shape=[jax.ShapeDtypeStruct((T, S5_W), BF16), jax.ShapeDtypeStruct((1, 2 * S5_N), F32),
                   jax.ShapeDtypeStruct((S5_W, 2 * S5_N), F32), jax.ShapeDtypeStruct((S5_W, 2 * S5_N), F32),
                   jax.ShapeDtypeStruct((1, S5_W), F32)],
        scratch_shapes=[pltpu.VMEM((8, 2 * S5_N), F32), pltpu.VMEM((tc, 2 * S5_N), F32)],
        compiler_params=pltpu.CompilerParams(dimension_semantics=("arbitrary",)),
    )(dyg, yc, z, h, h, b_st, c_st, d_skip, tab)


def _s5_discretize(lam_re, lam_im, log_dt, b_re, b_im):
    lr = jnp.minimum(lam_re, S5_MAX_RE)
    li = lam_im
    dt = jnp.exp(log_dt)[:, None]
    mag = jnp.exp(lr * dt)
    ang = li * dt
    ab_re = mag * jnp.cos(ang)
    ab_im = mag * jnp.sin(ang)
    den = lr * lr + li * li
    nr = ab_re - 1.0
    f_re = ((nr * lr + ab_im * li) / den)[..., None]
    f_im = ((ab_im * lr - nr * li) / den)[..., None]
    return ab_re, ab_im, f_re * b_re - f_im * b_im, f_re * b_im + f_im * b_re


def _adamw_fn(w, g, m, v):
    m = ADAM_B1 * m + (1.0 - ADAM_B1) * g
    v = ADAM_B2 * v + (1.0 - ADAM_B2) * (g * g)
    m_hat = m / (1.0 - ADAM_B1 ** ADAM_STEP)
    v_hat = v / (1.0 - ADAM_B2 ** ADAM_STEP)
    delta = -ADAM_LR * (m_hat / (jnp.sqrt(v_hat) + ADAM_EPS) + ADAM_WD * w)
    return delta, m, v


def _adamw(w, g, m, v, name):
    wd = w.shape[1]
    return _rowwise(_adamw_fn, name, [_whole(w), _whole(g), _whole(m), _whole(v)], [], [(wd, F32)] * 3,
                    tm=_row_tile(w.shape[0]))


def _adamw_many(ws, gs, ms, vs, name):
    n = len(ws)

    def body(*refs):
        for i in range(n):
            d_, m_, v_ = _adamw_fn(*[refs[kk * n + i][...] for kk in range(4)])
            refs[4 * n + i][...] = d_
            refs[5 * n + i][...] = m_
            refs[6 * n + i][...] = v_

    out = pl.pallas_call(body, name=name, out_shape=[jax.ShapeDtypeStruct(w.shape, F32) for w in ws] * 3)(*ws, *gs, *ms, *vs)
    return out[:n], out[n:2 * n], out[2 * n:]


def _me():
    return lax.axis_index("x"), lax.axis_index("y"), lax.axis_index("c")


_HBM = pl.BlockSpec(memory_space=pl.ANY)


def _gather_weights(shard):
    _, R, W = shard.shape

    def body(s_ref, out_ref, send_sems, recv_sems):
        x, y, c = _me()
        mine = 2 * x + y
        sib = (x, y, 1 - c)
        chips = [(1 - x, y), (x, 1 - y), (1 - x, 1 - y)]

        def rcopy(kk, src, chip_idx, half, to):
            return pltpu.make_async_remote_copy(src_ref=src, dst_ref=out_ref.at[chip_idx, half],
                                                send_sem=send_sems.at[kk], recv_sem=recv_sems.at[kk],
                                                device_id=to, device_id_type=MESH)

        first = [rcopy(j, s_ref.at[c], mine, c, (*chip, c)) for j, chip in enumerate(chips)]
        for cp in first:
            cp.start()
        passed = []
        for j, (px, py) in enumerate(chips):
            src_chip = 2 * px + py
            rcopy(j, s_ref.at[c], src_chip, c, (x, y, c)).wait_recv()
            fwd = rcopy(3 + j, out_ref.at[src_chip, c], src_chip, c, sib)
            fwd.start()
            passed.append(fwd)
        for j, (px, py) in enumerate(chips):
            rcopy(3 + j, s_ref.at[c], 2 * px + py, 1 - c, (x, y, c)).wait_recv()
        for cp in first + passed:
            cp.wait_send()

    return pl.pallas_call(
        body, name="gather_weights", in_specs=[_HBM], out_specs=_HBM,
        out_shape=jax.ShapeDtypeStruct((4, 2, R, W), shard.dtype),
        scratch_shapes=[pltpu.SemaphoreType.DMA((6,)), pltpu.SemaphoreType.DMA((6,))],
    )(shard)


def _pair_exchange(g, name):
    _, _, R, W = g.shape

    def body(g_ref, out_ref, send_sem, recv_sem):
        x, y, c = _me()
        cp = pltpu.make_async_remote_copy(src_ref=g_ref.at[:, 1 - c], dst_ref=out_ref, send_sem=send_sem,
                                          recv_sem=recv_sem, device_id=(x, y, 1 - c), device_id_type=MESH)
        cp.start()
        cp.wait()

    return pl.pallas_call(
        body, name=name, in_specs=[_HBM], out_specs=_HBM,
        out_shape=jax.ShapeDtypeStruct((4, R, W), g.dtype),
        scratch_shapes=[pltpu.SemaphoreType.DMA(()), pltpu.SemaphoreType.DMA(())],
    )(g)


def _pair_add(g, recv, cidx, name):
    _, _, R, W = g.shape
    tr = _row_tile(R)

    def body(c_ref, a_ref, b_ref, o_ref):
        o_ref[...] = (a_ref[...].astype(F32) + b_ref[...].astype(F32)).astype(o_ref.dtype)

    gs = pltpu.PrefetchScalarGridSpec(
        num_scalar_prefetch=1, grid=(4, R // tr),
        in_specs=[pl.BlockSpec((None, None, tr, W), lambda j, r, c: (j, c[0], r, 0)),
                  pl.BlockSpec((None, tr, W), lambda j, r, c: (j, r, 0))],
        out_specs=pl.BlockSpec((None, tr, W), lambda j, r, c: (j, r, 0)))
    return pl.pallas_call(body, name=name, grid_spec=gs,
                          out_shape=jax.ShapeDtypeStruct((4, R, W), BF16))(cidx, g, recv)


def _carried_exchange(src_ref, dst_ref, send_sems, recv_sems, spread, incoming):
    x, y, c = _me()
    mine = 2 * x + y
    copies = []
    for j, (px, py) in enumerate([(1 - x, y), (x, 1 - y), (1 - x, 1 - y)]):
        src = src_ref.at[2 * px + py] if spread else src_ref.at[c]
        slot = 2 * px + py if incoming else mine
        copies.append(pltpu.make_async_remote_copy(src_ref=src, dst_ref=dst_ref.at[slot], send_sem=send_sems.at[j],
                                                   recv_sem=recv_sems.at[j], device_id=(px, py, c), device_id_type=MESH))
    return copies


def _chip_exchange(p, name):
    _, R, W = p.shape

    def body(p_ref, out_ref, send_sems, recv_sems):
        outs = _carried_exchange(p_ref, out_ref, send_sems, recv_sems, True, False)
        for cp in outs:
            cp.start()
        for cp in _carried_exchange(p_ref, out_ref, send_sems, recv_sems, True, True):
            cp.wait_recv()
        for cp in outs:
            cp.wait_send()

    return pl.pallas_call(
        body, name=name, in_specs=[_HBM], out_specs=_HBM,
        out_shape=jax.ShapeDtypeStruct((4, R, W), p.dtype),
        scratch_shapes=[pltpu.SemaphoreType.DMA((3,)), pltpu.SemaphoreType.DMA((3,))],
    )(p)


def _chip_sum(q, name):
    _, R, W = q.shape
    tr = _row_tile(R)

    def body(q_ref, o_ref):
        f = lambda t: q_ref[t].astype(F32)
        o_ref[...] = ((f(0) + f(1)) + f(2)) + f(3)

    return pl.pallas_call(body, name=name, grid=(R // tr,),
                          in_specs=[pl.BlockSpec((4, tr, W), lambda r: (0, r, 0))],
                          out_specs=pl.BlockSpec((tr, W), lambda r: (r, 0)),
                          out_shape=jax.ShapeDtypeStruct((R, W), F32))(q)


def _pair_share(r, name):
    R, W = r.shape

    def body(r_ref, out_ref, send_sem, recv_sem):
        x, y, c = _me()
        cp = pltpu.make_async_remote_copy(src_ref=r_ref, dst_ref=out_ref, send_sem=send_sem, recv_sem=recv_sem,
                                          device_id=(x, y, 1 - c), device_id_type=MESH)
        cp.start()
        cp.wait()

    return pl.pallas_call(
        body, name=name, in_specs=[_HBM], out_specs=_HBM,
        out_shape=jax.ShapeDtypeStruct((R, W), r.dtype),
        scratch_shapes=[pltpu.SemaphoreType.DMA(()), pltpu.SemaphoreType.DMA(())],
    )(r)


def _allreduce_small(vec):
    R, W = vec.shape

    def body(v_ref, out_ref, buf, send_sems, recv_sems):
        x, y, c = _me()
        me = 4 * x + 2 * y + c
        buf[me] = v_ref[...]
        cps = []
        for kk in range(1, 8):
            peer = (x ^ (kk >> 2), y ^ ((kk >> 1) & 1), c ^ (kk & 1))
            cp = pltpu.make_async_remote_copy(src_ref=v_ref, dst_ref=buf.at[me], send_sem=send_sems.at[kk - 1],
                                              recv_sem=recv_sems.at[kk - 1], device_id=peer, device_id_type=MESH)
            cp.start()
            cps.append(cp)
        for kk in range(1, 8):
            peer = (x ^ (kk >> 2), y ^ ((kk >> 1) & 1), c ^ (kk & 1))
            pltpu.make_async_remote_copy(src_ref=v_ref, dst_ref=buf.at[me ^ kk], send_sem=send_sems.at[kk - 1],
                                         recv_sem=recv_sems.at[kk - 1], device_id=peer, device_id_type=MESH).wait_recv()
        for cp in cps:
            cp.wait_send()
        acc = buf[0]
        for d in range(1, 8):
            acc = acc + buf[d]
        out_ref[...] = acc

    vm = pl.BlockSpec(memory_space=pltpu.VMEM)
    return pl.pallas_call(
        body, name="allreduce_small", in_specs=[vm], out_specs=vm, out_shape=jax.ShapeDtypeStruct((R, W), F32),
        scratch_shapes=[pltpu.VMEM((8, R, W), F32), pltpu.SemaphoreType.DMA((7,)), pltpu.SemaphoreType.DMA((7,))],
    )(vec)


def _pack_flat(parts, align, dtype):
    flat = jnp.concatenate([p.reshape(-1).astype(dtype) for p in parts])
    n = flat.shape[0]
    pad = (-n) % align
    return jnp.pad(flat, (0, pad)) if pad else flat


def _unpack_flat(flat, shapes):
    out, off = [], 0
    for s in shapes:
        n = math.prod(s)
        out.append(flat[off:off + n].reshape(s))
        off += n
    return out


def _full_from_shards(name, sh):
    if name in COL_SHARDED:
        return jnp.transpose(sh, (1, 0, 2)).reshape(sh.shape[1], 4 * sh.shape[2])
    return sh.reshape(4 * sh.shape[1], sh.shape[2])


def _shards_from_full(name, full):
    if name in COL_SHARDED:
        r, cc = full.shape[0], full.shape[1] // 4
        return jnp.transpose(full.reshape(r, 4, cc), (1, 0, 2)).reshape(4, r * cc)
    return full.reshape(4, -1)


def _pad_heads(w, width):
    k = w.shape[0]
    return jnp.pad(w.reshape(k, HEADS, width), ((0, 0), (0, 0), (0, HP - width))).reshape(k, HEADS * HP)


def _unpad_heads(w, width):
    k = w.shape[0]
    return w.reshape(k, HEADS, HP)[:, :, :width].reshape(k, HEADS * width)


def _blockdiag(t):
    g, a, b = t.shape
    return jnp.einsum("gab,gk->gakb", t, jnp.eye(g, dtype=t.dtype)).reshape(g * a, g * b)


def _blockdiag_t(m, a, b):
    g = m.shape[0] // a
    return jnp.einsum("gagb->gab", m.reshape(g, a, g, b))


def _local_step(x, mem, positions, target, W, sp, overlap=None):
    T = x.shape[0]
    row = lambda v: v.reshape(1, -1)

    ab_re, ab_im, bb_re, bb_im = _s5_discretize(sp["s5_lam_re"], sp["s5_lam_im"], sp["s5_log_dt"], sp["s5_b_re"], sp["s5_b_im"])
    ar, ai = ab_re.reshape(1, S5_N), ab_im.reshape(1, S5_N)
    tab_fwd, tab_rev = _s5_tables(ar, ai, False), _s5_tables(ar, -ai, True)
    b_st = jnp.concatenate([_blockdiag(jnp.swapaxes(bb_re, 1, 2)), _blockdiag(jnp.swapaxes(bb_im, 1, 2))],
                           axis=1).astype(BF16)
    c_st = jnp.concatenate([_blockdiag(jnp.swapaxes(sp["s5_c_re"], 1, 2)),
                            -_blockdiag(jnp.swapaxes(sp["s5_c_im"], 1, 2))], axis=0).astype(BF16)
    inv = ROPE_THETA ** (-jnp.arange(0, ROPE, 2, dtype=F32) / ROPE)
    lane = jnp.arange(HP)
    rot1, rot2 = (lane >= NOPE) & (lane < NOPE + ROPE // 2), (lane >= NOPE + ROPE // 2) & (lane < QK)
    ang = positions.astype(F32)[:, None] * jnp.where(rot1 | rot2, inv[(lane - NOPE) % (ROPE // 2)], 0.0)
    sn = jnp.sin(ang)
    t_cos, t_s1, t_s2 = jnp.cos(ang), jnp.where(rot2, sn, 0.0), jnp.where(rot1, -sn, 0.0)

    w_in = W["w_in"]
    w_in_re = jnp.concatenate([w_in[:, 800:], w_in[:, :768], w_in[:, 768:800],
                               jnp.zeros((D_MODEL, HP - ROPE), w_in.dtype)], axis=1)
    w_uq_p = _pad_heads(W["w_uq"], QK)
    wkv = W["w_ukv"].reshape(S5_W, HEADS, NOPE + VD)
    w_uk_p = _pad_heads(wkv[:, :, :NOPE].reshape(S5_W, HEADS * NOPE), NOPE)
    w_uv_p = _pad_heads(wkv[:, :, NOPE:].reshape(S5_W, HEADS * VD), VD)
    w_oa_p = jnp.pad(W["w_oa"].reshape(HEADS, VD, D_MODEL), ((0, 0), (0, HP - VD), (0, 0))).reshape(HEADS * HP, D_MODEL)

    g_in, b_in = row(sp["ln_in_g"]), row(sp["ln_in_b"])
    g1, b1, g2, b2, g3, b3 = (row(sp[k]) for k in ("ln1_g", "ln1_b", "ln2_g", "ln2_b", "ln3_g", "ln3_b"))
    gq, gkv, d_skip = row(sp["q_norm_g"]), row(sp["kv_norm_g"]), row(sp["s5_d"])

    (h0,) = _rowwise(lambda a, g, b: (_ln(a, g, b),), "ln_in_fwd", [_whole(x)], [g_in, b_in], [(D_MODEL, F32)], tm=512)
    def in_epi(acc, ga, gb):
        zb = acc.astype(BF16).astype(F32)
        return acc, _rms(zb[:, Z_CQ:Z_CQ + S5_W], ga), _rms(zb[:, Z_CKV:Z_CKV + S5_W], gb)

    z, cqn, ckvn = _mm(h0, w_in_re, "in_proj_norm", tm=512, tn=Z_W, epi=in_epi, pextras=(gq, gkv),
                       outs=[(Z_W, BF16), (S5_W, BF16), (S5_W, BF16)])
    hs, yc, yg = _s5_fwd(z, b_st, c_st, d_skip, tab_fwd, "s5_fwd")
    y2 = _mm(yg, W["w_glu"], "glu_proj", tn=2048, out_dtype=BF16)
    tabs = [(t_cos, 0, HP), (t_s1, 0, HP), (t_s2, 0, HP)]

    def rope_q(acc, tc_, t1, t2):
        return _rope(acc, _tile8(tc_), _tile8(t1), _tile8(t2)) * (ATT_SCALE * LOG2E)

    def rope_k(acc, krz, tc_, t1, t2):
        return acc + _tile8(_rope(pltpu.roll(krz.astype(F32), NOPE, 1), tc_, t1, t2))

    q = _mm(cqn, w_uq_p, "mla_uq_rope", epi=rope_q, rextras=tabs, out_dtype=BF16)
    k = _mm(ckvn, w_uk_p, "mla_uk_rope", epi=rope_k, rextras=[(z, Z_KR, HP)] + tabs, out_dtype=BF16)
    v = _mm(ckvn, w_uv_p, "mla_uv", out_dtype=BF16)
    if overlap is None:
        o, lse = _flash_fwd(q, k, v, "mla_attn_fwd", bq=ATT_BQ_FWD)
    else:
        o, lse, landed = _flash_fwd(q, k, v, "mla_attn_fwd", bq=ATT_BQ_FWD, carry=(overlap[0], False))
        W = {**W, **overlap[1](landed)}
    def gate_epi(acc, gs_, ga_, ya, yb):
        f = lambda t: t.astype(F32)
        return acc, _gate_mix(f(gs_), f(ga_), f(ya), f(yb), acc.astype(BF16).astype(F32))

    a_out, mixin = _mm(o, w_oa_p, "mla_oa_gate", epi=gate_epi, outs=[(D_MODEL, BF16)] * 2, tm=512,
                       rextras=[(z, Z_GS, D_MODEL), (z, Z_GA, D_MODEL), (y2, 0, D_MODEL), (y2, D_MODEL, D_MODEL)])
    def ln_epi(acc, h, g, b):
        s_ = DN_ALPHA * h + acc
        return s_, _ln(s_, g, b)

    two_rows = [(D_MODEL, F32)] * 2
    s1, h1 = _mm(mixin, W["w_o"], "mix_o_ln1", epi=ln_epi, extras=(h0,), pextras=(g1, b1), outs=two_rows, tm=512)
    xq = _mm(h1, W["w_xq"], "x_q", out_dtype=BF16)
    xk = _mm(mem, W["w_xk"], "x_k", out_dtype=BF16)
    xv = _mm(mem, W["w_xv"], "x_v", out_dtype=BF16)
    (xo,) = _rowwise(_xattn_fwd_fn, "xattn_fwd", [_whole(xq)], [xk, xv], [(D_MODEL, BF16)], tm=2048, upcast=False)
    s2, h2 = _mm(xo, W["w_xo"], "x_o_ln2", epi=ln_epi, extras=(h1,), pextras=(g2, b2), outs=two_rows, tm=512)
    a_up = _mm(h2, W["w_up"], "mlp_up", out_dtype=BF16, tn=2048)
    def loss_epi(acc, h, tgt, g, b):
        def f(r_, g_, b_):
            e = _ln_res(h, r_, g_, b_) - tgt
            return 0.5 * jnp.sum(jnp.mean(e * e, axis=-1))
        lv, (dr_, dg_, db_) = jax.value_and_grad(f, argnums=(0, 1, 2))(acc, g, b)
        return dr_, dg_, db_, jnp.broadcast_to(lv, (1, 128))

    row_and_sums = [(D_MODEL, F32), (D_MODEL, "sum"), (D_MODEL, "sum")]
    dr3, dg3, db3, lossv = _mm(a_up, W["w_down"], "mlp_down_loss", act=_relu2, tm=512, tk=4 * D_MODEL, epi=loss_epi,
                               extras=(h2, target), pextras=(g3, b3), outs=row_and_sums + [(128, "sum")])
    gW, gs = {}, {"ln3_g": dg3, "ln3_b": db3}

    da = _mm(dr3, W["w_down"], "mlp_down_bwd_a", "nt", epi=lambda acc, a: acc * (2.0 * jnp.maximum(a, 0.0)), extras=(a_up,),
             out_dtype=BF16, tn=2048)
    gW["w_down"] = _mm(a_up, dr3, "mlp_down_bwd_w", "tn", m=4 * D_MODEL, n=D_MODEL, act=_relu2, tk=2048)
    gW["w_up"] = _mm(h2, da, "mlp_up_bwd_w", "tn", m=D_MODEL, n=4 * D_MODEL, tk=2048)

    def ln_bwd_epi(acc, e, s_, g, b):
        _, vjp = jax.vjp(_ln, s_, g, b)
        return vjp(acc + DN_ALPHA * e)

    dr2, gs["ln2_g"], gs["ln2_b"] = _mm(da, W["w_up"], "mlp_up_bwd_ln2", "nt", tm=512, tk=4 * D_MODEL, epi=ln_bwd_epi,
                                        extras=(dr3, s2), pextras=(g2, b2), outs=row_and_sums)
    dxo = _mm(dr2, W["w_xo"], "x_o_bwd_a", "nt", out_dtype=BF16)
    gW["w_xo"] = _mm(xo, dr2, "x_o_bwd_w", "tn", m=D_MODEL, n=D_MODEL, tk=2048)
    dxq, dxk, dxv = _rowwise(_xattn_bwd_fn, "xattn_bwd", [_whole(xq), _whole(dxo)], [xk, xv], [(D_MODEL, BF16)],
                             [(xk.shape[0], D_MODEL)] * 2, tm=2048, upcast=False)
    gW["w_xq"] = _mm(h1, dxq, "x_q_bwd_w", "tn", m=D_MODEL, n=D_MODEL, tk=2048)
    gW["w_xk"] = _mm(mem, dxk, "x_k_bwd_w", "tn", m=D_MODEL, n=D_MODEL)
    gW["w_xv"] = _mm(mem, dxv, "x_v_bwd_w", "tn", m=D_MODEL, n=D_MODEL)

    dr1, gs["ln1_g"], gs["ln1_b"] = _mm(dxq, W["w_xq"], "x_q_bwd_ln1", "nt", tm=512, epi=ln_bwd_epi,
                                        extras=(dr2, s1), pextras=(g1, b1), outs=row_and_sums)
    gW["w_o"] = _mm(mixin, dr1, "mix_o_bwd_w", "tn", m=D_MODEL, n=D_MODEL, tk=2048)

    def gate_bwd_epi(acc, *tiles):
        _, vjp = jax.vjp(_gate_mix, *[t.astype(F32) for t in tiles])
        return vjp(acc)

    dgs, dga, dy2a, dy2b, da_out = _mm(
        dr1, W["w_o"], "mix_o_bwd_gate", "nt", tm=512, epi=gate_bwd_epi, outs=[(D_MODEL, BF16)] * 5,
        rextras=[(z, Z_GS, D_MODEL), (z, Z_GA, D_MODEL), (y2, 0, D_MODEL), (y2, D_MODEL, D_MODEL), _whole(a_out)])

    def delta_epi(acc, o_):
        prod = acc * o_
        cols = [jnp.sum(prod[:, h * HP:(h + 1) * HP], axis=1, keepdims=True) for h in range(HEADS)]
        return acc, jnp.concatenate(cols, axis=1)

    do16, delta = _mm(da_out, w_oa_p, "mla_oa_bwd_a", "nt", epi=delta_epi, extras=(o,),
                      outs=[(HEADS * HP, BF16), (HEADS, F32)], tm=512)
    g_oa_p = _mm(o, da_out, "mla_oa_bwd_w", "tn", m=HEADS * HP, n=D_MODEL, tk=2048)
    gW["w_oa"] = g_oa_p.reshape(HEADS, HP, D_MODEL)[:, :VD].reshape(HEADS * VD, D_MODEL)
    delta_row = delta.T.reshape(HEADS, 1, T)
    if overlap is None:
        dq, dk, dv = _flash_bwd(q, k, v, do16, lse, delta_row, "mla_attn_bwd")
        riding = None
    else:
        sent = overlap[2]({n: gW[n] for n in BIG_LATE})
        dq, dk, dv, landed = _flash_bwd(q, k, v, do16, lse, delta_row, "mla_attn_bwd", carry=(sent, True))
        riding = (sent, landed)

    def rope_bwd(dq_, dk_, tc_, t1, t2):
        dqr = _rope_t(dq_, _tile8(tc_), _tile8(t1), _tile8(t2))
        dkr = dk_[:, 0:HP]
        for hh in range(1, HEADS):
            dkr = dkr + dk_[:, hh * HP:(hh + 1) * HP]
        dkr = _rope_t(jnp.where(_rope_lanes(dkr.shape), dkr, 0.0), tc_, t1, t2)
        dkr = pltpu.roll(dkr, NOPE, 1)
        lane = lax.broadcasted_iota(jnp.int32, dkr.shape, 1)
        return dqr, jnp.where(lane < ROPE, dkr, 0.0)

    dq_raw, dkr = _rowwise(rope_bwd, "mla_rope_bwd", [_whole(dq), _whole(dk), _whole(t_cos), _whole(t_s1), _whole(t_s2)], [],
                           [(HEADS * HP, BF16), (HP, BF16)], tm=512)

    def rms_bwd_epi(acc, *rest):
        _, vjp = jax.vjp(_rms, rest[-2].astype(F32), rest[-1])
        return vjp(acc + rest[0] if len(rest) == 3 else acc)

    row_and_sum = [(S5_W, BF16), (S5_W, "sum")]
    dcq, gs["q_norm_g"] = _mm(dq_raw, w_uq_p, "mla_uq_bwd_norm", "nt", epi=rms_bwd_epi, rextras=[(z, Z_CQ, S5_W)],
                              pextras=(gq,), outs=row_and_sum)
    gW["w_uq"] = _unpad_heads(_mm(cqn, dq_raw, "mla_uq_bwd_w", "tn", m=S5_W, n=HEADS * HP), QK)
    dckvn_k = _mm(dk, w_uk_p, "mla_uk_bwd_a", "nt")
    dckv, gs["kv_norm_g"] = _mm(dv, w_uv_p, "mla_uv_bwd_norm", "nt", epi=rms_bwd_epi, extras=(dckvn_k,),
                                rextras=[(z, Z_CKV, S5_W)], pextras=(gkv,), outs=row_and_sum)
    g_uk = _unpad_heads(_mm(ckvn, dk, "mla_uk_bwd_w", "tn", m=S5_W, n=HEADS * HP), NOPE).reshape(S5_W, HEADS, NOPE)
    g_uv = _unpad_heads(_mm(ckvn, dv, "mla_uv_bwd_w", "tn", m=S5_W, n=HEADS * HP), VD).reshape(S5_W, HEADS, VD)
    gW["w_ukv"] = jnp.concatenate([g_uk, g_uv], axis=2).reshape(S5_W, HEADS * (NOPE + VD))

    dy2 = jnp.concatenate([dy2a, dy2b], axis=1)
    dyg = _mm(dy2, W["w_glu"], "glu_bwd_a", "nt")
    gW["w_glu"] = _mm(yg, dy2, "glu_bwd_w", "tn", m=S5_W, n=2 * D_MODEL, tk=2048)

    du, gs["s5_ab"], g_bst, g_cst_t, gs["s5_d"] = _s5_bwd(dyg, yc, z, hs, b_st, c_st, d_skip, tab_rev, "s5_bwd")
    gs["s5_bb_re"] = jnp.swapaxes(_blockdiag_t(g_bst[:, :S5_N], S5_H, S5_P), 1, 2)
    gs["s5_bb_im"] = jnp.swapaxes(_blockdiag_t(g_bst[:, S5_N:], S5_H, S5_P), 1, 2)
    gs["s5_c_re"] = _blockdiag_t(g_cst_t[:, :S5_N], S5_H, S5_P)
    gs["s5_c_im"] = -_blockdiag_t(g_cst_t[:, S5_N:], S5_H, S5_P)

    dz = jnp.concatenate([dgs, dga, du, dcq, dckv, dkr], axis=1)
    g_in_re = _mm(h0, dz, "in_proj_bwd_w", "tn", m=D_MODEL, n=Z_W, tm=512, tn=Z_W)
    gW["w_in"] = jnp.concatenate([g_in_re[:, Z_U:Z_KR], g_in_re[:, Z_KR:Z_KR + ROPE], g_in_re[:, :Z_U]], axis=1)
    dx, gs["ln_in_g"], gs["ln_in_b"] = _mm(dz, w_in_re, "in_proj_bwd_ln_in", "nt", tm=512, tk=Z_W, epi=ln_bwd_epi,
                                           extras=(dr1, x), pextras=(g_in, b_in), outs=row_and_sums)
    return lossv[:, :1], dx, gW, gs, riding


_RAW_SMALL = (("loss", (1, 1)), ("ln_in_g", (1, D_MODEL)), ("ln_in_b", (1, D_MODEL)), ("ln1_g", (1, D_MODEL)),
              ("ln1_b", (1, D_MODEL)), ("ln2_g", (1, D_MODEL)), ("ln2_b", (1, D_MODEL)), ("ln3_g", (1, D_MODEL)),
              ("ln3_b", (1, D_MODEL)), ("q_norm_g", (1, S5_W)), ("kv_norm_g", (1, S5_W)), ("s5_d", (1, S5_W)),
              ("s5_ab", (1, 2 * S5_N)), ("s5_bb_re", (S5_G, S5_P, S5_H)), ("s5_bb_im", (S5_G, S5_P, S5_H)),
              ("s5_c_re", (S5_G, S5_H, S5_P)), ("s5_c_im", (S5_G, S5_H, S5_P)))


def kernel(x, mem, positions, ln_in_g, ln_in_b, w_in, s5_lam_re, s5_lam_im, s5_log_dt, s5_b_re, s5_b_im, s5_c_re, s5_c_im, s5_d, w_glu, q_norm_g, w_uq, kv_norm_g, w_ukv, w_oa, w_o, ln1_g, ln1_b, w_xq, w_xk, w_xv, w_xo, ln2_g, ln2_b, w_up, w_down, ln3_g, ln3_b, loss_target, m_ln_in_g, m_ln_in_b, m_w_in, m_s5_lam_re, m_s5_lam_im, m_s5_log_dt, m_s5_b_re, m_s5_b_im, m_s5_c_re, m_s5_c_im, m_s5_d, m_w_glu, m_q_norm_g, m_w_uq, m_kv_norm_g, m_w_ukv, m_w_oa, m_w_o, m_ln1_g, m_ln1_b, m_w_xq, m_w_xk, m_w_xv, m_w_xo, m_ln2_g, m_ln2_b, m_w_up, m_w_down, m_ln3_g, m_ln3_b, v_ln_in_g, v_ln_in_b, v_w_in, v_s5_lam_re, v_s5_lam_im, v_s5_log_dt, v_s5_b_re, v_s5_b_im, v_s5_c_re, v_s5_c_im, v_s5_d, v_w_glu, v_q_norm_g, v_w_uq, v_kv_norm_g, v_w_ukv, v_w_oa, v_w_o, v_ln1_g, v_ln1_b, v_w_xq, v_w_xk, v_w_xv, v_w_xo, v_ln2_g, v_ln2_b, v_w_up, v_w_down, v_ln3_g, v_ln3_b):
    a = dict(locals())
    wts = {n: a[n] for n in WEIGHTS}
    ms = {n: a["m_" + n] for n in WEIGHTS}
    vs = {n: a["v_" + n] for n in WEIGHTS}
    shard2d = {n: wts[n].reshape(wts[n].shape[-2], wts[n].shape[-1]) for n in BIG}
    nrow = {n: shard2d[n].size // 1024 for n in BIG}
    mine = 2 * lax.axis_index("x") + lax.axis_index("y")
    my_c = lax.axis_index("c")
    cidx = my_c.astype(jnp.int32).reshape(1)

    def group_rows(group):
        rows = sum(nrow[n] for n in group)
        return rows, (-rows) % PACK_ROWS

    def pack_shards(group):
        rows, pad = group_rows(group)
        parts = [shard2d[n].astype(BF16).reshape(nrow[n], 1024) for n in group] + [jnp.zeros((pad, 1024), BF16)]
        return jnp.concatenate(parts, axis=0).reshape(2, (rows + pad) // 2, 1024)

    def unpack_full(group, gathered):
        out, off = {}, 0
        for n in group:
            r, cc = shard2d[n].shape
            piece = lax.optimization_barrier(gathered[:, off:off + nrow[n]])
            out[n] = _full_from_shards(n, piece.reshape(4, r, cc))
            off += nrow[n]
        return out

    def pack_grads(group, gw):
        rows, pad = group_rows(group)
        parts = [_shards_from_full(n, gw[n].astype(BF16)).reshape(4, nrow[n], 1024) for n in group]
        return jnp.concatenate(parts + [jnp.zeros((4, pad, 1024), BF16)], axis=1).reshape(4, 2, (rows + pad) // 2, 1024)

    def by_core(own, other):
        return jnp.where(my_c == 0, jnp.stack([own, other]), jnp.stack([other, own]))

    packed_e, packed_l = pack_shards(BIG_EARLY), pack_shards(BIG_LATE)
    gathered = lax.dynamic_update_slice(_gather_weights(packed_e), packed_e[None], (mine, 0, 0, 0))
    W = unpack_full(BIG_EARLY, gathered.reshape(4, -1, 1024))

    def finish_gather(landed):
        half = packed_l.shape[1]
        own_half = lax.dynamic_index_in_dim(packed_l, my_c, axis=0, keepdims=True)
        landed = lax.dynamic_update_slice(landed, own_half, (mine, 0, 0))
        other = _pair_share(landed.reshape(4 * half, 1024), "gather_late_share").reshape(4, half, 1024)
        out, off = {}, 0
        for n in BIG_LATE:
            r, cc = shard2d[n].shape
            h, lo = divmod(off, half)
            assert lo + nrow[n] <= half, n
            piece = jnp.where(my_c == h, landed[:, lo:lo + nrow[n]], other[:, lo:lo + nrow[n]])
            out[n] = _full_from_shards(n, lax.optimization_barrier(piece).reshape(4, r, cc))
            off += nrow[n]
        return out

    def pair_reduce(group, gw, tag):
        gflat = pack_grads(group, gw)
        return _pair_add(gflat, _pair_exchange(gflat, "grad_pair_exchange_" + tag), cidx, "grad_pair_add_" + tag)

    def finish_reduce(group, sent, landed, tag):
        own = lax.dynamic_slice(sent, (mine, 0, 0), (1,) + sent.shape[1:])
        red = _chip_sum(lax.dynamic_update_slice(landed, own, (mine, 0, 0)), "grad_chip_sum_" + tag)
        gsh = by_core(red, _pair_share(red, "grad_pair_share_" + tag)).reshape(-1, 1024)
        out, off = {}, 0
        for n in group:
            out[n] = lax.optimization_barrier(gsh[off:off + nrow[n]]).reshape(shard2d[n].shape)
            off += nrow[n]
        return out

    sp = {n: wts[n] for n in SMALL}
    sp_local = {n: (sp[n][0] if sp[n].ndim > 1 else sp[n]) for n in SMALL}
    overlap = (packed_l, finish_gather, lambda gw: pair_reduce(BIG_LATE, gw, "late"))
    lossv, dx, gW, gs, riding = _local_step(x[0], mem[0], positions[0], loss_target[0], W, sp_local, overlap)

    g_out = finish_reduce(BIG_LATE, riding[0], riding[1], "late")
    sent_e = pair_reduce(BIG_EARLY, gW, "early")
    g_out.update(finish_reduce(BIG_EARLY, sent_e, _chip_exchange(sent_e, "grad_chip_exchange_early"), "early"))

    gs["loss"] = lossv
    raw = _pack_flat([gs[n].reshape(s) for n, s in _RAW_SMALL], 8 * 1024, F32)
    raw = _allreduce_small(raw.reshape(-1, 1024)).reshape(-1)
    rs = dict(zip([n for n, _ in _RAW_SMALL], _unpack_flat(raw, [s for _, s in _RAW_SMALL])))
    loss = rs["loss"].reshape(())
    _, disc_vjp = jax.vjp(_s5_discretize, sp_local["s5_lam_re"], sp_local["s5_lam_im"], sp_local["s5_log_dt"],
                          sp_local["s5_b_re"], sp_local["s5_b_im"])
    d_ab = rs["s5_ab"].reshape(2, S5_G, S5_P)
    g_lre, g_lim, g_ldt, g_bre, g_bim = disc_vjp((d_ab[0], d_ab[1], rs["s5_bb_re"], rs["s5_bb_im"]))
    small_g = {"s5_lam_re": g_lre, "s5_lam_im": g_lim, "s5_log_dt": g_ldt, "s5_b_re": g_bre, "s5_b_im": g_bim,
               "s5_c_re": rs["s5_c_re"], "s5_c_im": rs["s5_c_im"]}
    for n in ("ln_in_g", "ln_in_b", "ln1_g", "ln1_b", "ln2_g", "ln2_b", "ln3_g", "ln3_b", "q_norm_g", "kv_norm_g", "s5_d"):
        small_g[n] = rs[n]

    grads, deltas, new_m, new_v = {}, {}, {}, {}
    for n in BIG:
        d_, m_, v_ = _adamw(shard2d[n], g_out[n], ms[n].reshape(shard2d[n].shape), vs[n].reshape(shard2d[n].shape), "adamw_" + n)
        grads[n] = g_out[n].reshape(wts[n].shape)
        deltas[n], new_m[n], new_v[n] = (t.reshape(wts[n].shape) for t in (d_, m_, v_))
    as2d = lambda t: t.reshape(-1, t.shape[-1])
    sd, sm, sv = _adamw_many([as2d(wts[n]) for n in SMALL], [as2d(small_g[n].reshape(wts[n].shape)) for n in SMALL],
                             [as2d(ms[n]) for n in SMALL], [as2d(vs[n]) for n in SMALL], "adamw_small")
    for n, d_, m_, v_ in zip(SMALL, sd, sm, sv):
        grads[n] = small_g[n].reshape(wts[n].shape)
        deltas[n], new_m[n], new_v[n] = (t.reshape(wts[n].shape) for t in (d_, m_, v_))

    return (loss, dx[None], *[grads[n] for n in WEIGHTS], *[deltas[n] for n in WEIGHTS],
            *[new_m[n] for n in WEIGHTS], *[new_v[n] for n in WEIGHTS])
```

```python
import functools
import math

import jax
import jax.numpy as jnp
from jax import lax
from jax.experimental import pallas as pl
from jax.experimental.pallas import tpu as pltpu

F32 = jnp.float32
BF16 = jnp.bfloat16
MESH = pl.DeviceIdType.MESH

D_MODEL = 1024
S5_W = 256
S5_G = 16
S5_H = 16
S5_P = 64
S5_N = S5_G * S5_P
S5_MAX_RE = -1e-4
HEADS = 8
NOPE = 64
ROPE = 32
QK = NOPE + ROPE
VD = 64
HP = 128
XH = 4
XD = 256
LN_EPS = 1e-5
RMS_EPS = 1e-6
NEG_INF = -1e30
DN_ALPHA = 2.0 ** 0.25
ROPE_THETA = 10000.0
ADAM_LR, ADAM_B1, ADAM_B2, ADAM_EPS, ADAM_WD, ADAM_STEP = 0.001, 0.9, 0.999, 1e-08, 0.01, 10

Z_GS, Z_GA, Z_U, Z_CQ, Z_CKV, Z_KR, Z_W = 0, 1024, 2048, 2304, 2560, 2816, 2944

BIG_EARLY = ("w_in", "w_glu", "w_uq", "w_ukv", "w_oa", "w_o")
BIG_LATE = ("w_up", "w_xq", "w_xk", "w_xv", "w_xo", "w_down")
BIG = BIG_EARLY + BIG_LATE
COL_SHARDED = ("w_in", "w_glu", "w_uq", "w_ukv", "w_oa", "w_up")
SMALL = ("ln_in_g", "ln_in_b", "s5_lam_re", "s5_lam_im", "s5_log_dt", "s5_b_re", "s5_b_im", "s5_c_re", "s5_c_im",
         "s5_d", "q_norm_g", "kv_norm_g", "ln1_g", "ln1_b", "ln2_g", "ln2_b", "ln3_g", "ln3_b")
WEIGHTS = ("ln_in_g", "ln_in_b", "w_in", "s5_lam_re", "s5_lam_im", "s5_log_dt", "s5_b_re", "s5_b_im", "s5_c_re",
           "s5_c_im", "s5_d", "w_glu", "q_norm_g", "w_uq", "kv_norm_g", "w_ukv", "w_oa", "w_o", "ln1_g", "ln1_b",
           "w_xq", "w_xk", "w_xv", "w_xo", "ln2_g", "ln2_b", "w_up", "w_down", "ln3_g", "ln3_b")
PACK_ROWS = 2 * 16


def _row_tile(n, cap=512):
    best = n
    for t in range(8, min(n, cap) + 1, 8):
        if n % t == 0:
            best = t
    return best


def _pick(n, cap):
    best = None
    for t in range(128, min(n, cap) + 1, 128):
        if n % t == 0:
            best = t
    return best if best is not None and (best >= 512 or best == n) else n


def _mm(a, b, name, mode="nn", *, a_off=0, b_off=0, m=None, n=None, act=None, epi=None, extras=(), rextras=(),
        pextras=(), outs=None, out_dtype=None, tm=1024, tn=1024, tk=None):
    out_dtype = out_dtype or (BF16 if mode == "tn" else F32)
    if mode == "nn":
        M, (K, N) = a.shape[0], b.shape
    elif mode == "nt":
        M, (N, K) = a.shape[0], b.shape
    else:
        K, M, N = a.shape[0], m, n
    if mode == "tn":
        tm, tn, tk = _pick(M, tm), _pick(N, tn), min(tk or 1024, K)
    else:
        tm, tn, tk = min(tm, M), _pick(N, tn), _pick(K, tk or 1024)
    assert M % tm == 0 and N % tn == 0 and K % tk == 0, (name, M, N, K, tm, tn, tk)
    nk = K // tk
    if mode == "tn":
        assert a_off % tm == 0 and b_off % tn == 0
        ao, bo = a_off // tm, b_off // tn
        a_spec = pl.BlockSpec((tk, tm), lambda i, j, k: (k, i + ao))
        b_spec = pl.BlockSpec((tk, tn), lambda i, j, k: (k, j + bo))
        dims = (((0,), (0,)), ((), ()))
    else:
        assert a_off % tk == 0
        ao = a_off // tk
        a_spec = pl.BlockSpec((tm, tk), lambda i, j, k: (i, k + ao))
        if mode == "nn":
            b_spec = pl.BlockSpec((tk, tn), lambda i, j, k: (k, j))
            dims = (((1,), (0,)), ((), ()))
        else:
            b_spec = pl.BlockSpec((tn, tk), lambda i, j, k: (j, k))
            dims = (((1,), (1,)), ((), ()))
    e_spec = pl.BlockSpec((tm, tn), lambda i, j, k: (i, j))
    r_specs = [pl.BlockSpec((tm, w), functools.partial(lambda i, j, k, o: (i, o), o=off // w)) for _, off, w in rextras]
    p_specs = [pl.BlockSpec((1, tn), lambda i, j, k: (0, j)) if p.shape[1] == N else pl.BlockSpec(p.shape, lambda i, j, k: (0, 0))
               for p in pextras]
    n_extra = len(extras) + len(rextras) + len(pextras)
    if outs is None:
        o_specs, o_shapes = [e_spec], [jax.ShapeDtypeStruct((M, N), out_dtype)]
    else:
        assert all((w == tn and dt != "sum") or N == tn for w, dt in outs), name
        o_specs = [e_spec if w == tn and dt != "sum" else pl.BlockSpec((8 if dt == "sum" else tm, w), lambda i, j, k: (i, 0))
                   for w, dt in outs]
        o_shapes = [jax.ShapeDtypeStruct((M // tm * 8, w), F32) if dt == "sum" else
                    jax.ShapeDtypeStruct((M, N if w == tn else w), dt) for w, dt in outs]
    n_out = len(o_specs)
    direct = outs is None and epi is None and out_dtype == F32

    def body(*refs):
        a_ref, b_ref = refs[0], refs[1]
        e_refs = refs[2:2 + n_extra]
        o_refs = refs[2 + n_extra:2 + n_extra + n_out]
        av = a_ref[...]
        if act is not None:
            av = act(av.astype(F32))
        p = lax.dot_general(av.astype(BF16), b_ref[...].astype(BF16), dims, preferred_element_type=F32)

        def finish(r):
            if epi is not None:
                r = epi(r, *[e[...] for e in e_refs])
            for o_ref, v in zip(o_refs, r if isinstance(r, tuple) else (r,)):
                o_ref[...] = jnp.broadcast_to(v, o_ref.shape).astype(o_ref.dtype)

        if nk == 1:
            finish(p)
        else:
            acc = o_refs[0] if direct else refs[2 + n_extra + n_out]
            k = pl.program_id(2)

            @pl.when(k == 0)
            def _():
                acc[...] = p

            @pl.when(k > 0)
            def _():
                acc[...] += p

            if not direct:
                @pl.when(k == nk - 1)
                def _():
                    finish(acc[...])

    res = pl.pallas_call(
        body, name=name, grid=(M // tm, N // tn, nk),
        in_specs=[a_spec, b_spec] + [e_spec] * len(extras) + r_specs + p_specs, out_specs=o_specs,
        out_shape=o_shapes,
        scratch_shapes=[pltpu.VMEM((tm, tn), F32)] if nk > 1 and not direct else [],
        compiler_params=pltpu.CompilerParams(dimension_semantics=("parallel", "parallel", "arbitrary")),
    )(a, b, *extras, *[r[0] for r in rextras], *pextras)
    if outs is None:
        return res[0]
    return [r.reshape(M // tm, 8, -1)[:, 0].sum(axis=0, keepdims=True) if dt == "sum" else r for r, (_, dt) in zip(res, outs)]


def _rowwise(fn, name, rows, pars, outs, accs=(), tm=256, upcast=True):
    T = rows[0][0].shape[0]
    tm = min(tm, T)
    assert T % tm == 0
    n_in, n_out = len(rows) + len(pars), len(outs)
    in_specs = []
    for arr, off, w in rows:
        assert off % w == 0 and arr.shape[0] == T, name
        in_specs.append(pl.BlockSpec((tm, w), functools.partial(lambda i, o: (i, o), o=off // w)))
    for p in pars:
        in_specs.append(pl.BlockSpec(p.shape, lambda i: (0, 0)))
    out_specs = [pl.BlockSpec((tm, w), lambda i: (i, 0)) for w, _ in outs]
    out_specs += [pl.BlockSpec(s, lambda i: (0, 0)) for s in accs]
    out_shape = [jax.ShapeDtypeStruct((T, w), dt) for w, dt in outs] + [jax.ShapeDtypeStruct(s, F32) for s in accs]

    def body(*refs):
        res = fn(*[r[...].astype(F32) if upcast else r[...] for r in refs[:n_in]])
        o_refs = refs[n_in:]
        for r, v in zip(o_refs[:n_out], res[:n_out]):
            r[...] = v.astype(r.dtype)
        if accs:
            i = pl.program_id(0)

            @pl.when(i == 0)
            def _():
                for r in o_refs[n_out:]:
                    r[...] = jnp.zeros_like(r)

            for r, v in zip(o_refs[n_out:], res[n_out:]):
                r[...] += v

    res = pl.pallas_call(
        body, name=name, grid=(T // tm,), in_specs=in_specs, out_specs=out_specs, out_shape=out_shape,
        compiler_params=pltpu.CompilerParams(dimension_semantics=("arbitrary",)),
    )(*[r[0] for r in rows], *pars)
    return res


def _whole(arr):
    return (arr, 0, arr.shape[1])


def _ln(x, g, b):
    mu = jnp.mean(x, axis=-1, keepdims=True)
    xc = x - mu
    var = jnp.mean(xc * xc, axis=-1, keepdims=True)
    return xc * lax.rsqrt(var + LN_EPS) * g + b


def _ln_res(h, r, g, b):
    return _ln(DN_ALPHA * h + r, g, b)


def _rms(x, g):
    return x * lax.rsqrt(jnp.mean(x * x, axis=-1, keepdims=True) + RMS_EPS) * g


def _gelu_skip(yc, u, d):
    y = yc + d * u
    return 0.5 * y * (1.0 + lax.erf(y * (1.0 / math.sqrt(2.0))))


def _gate_mix(gs, ga, y2a, y2b, aout):
    return jax.nn.sigmoid(gs) * (y2a * jax.nn.sigmoid(y2b)) + jax.nn.sigmoid(ga) * aout


def _relu2(a):
    r = jnp.maximum(a, 0.0)
    return r * r


def _tile8(t):
    return jnp.concatenate([t] * HEADS, axis=1)


def _rope(x, cos, s1, s2):
    w = x.shape[1]
    return x * cos + pltpu.roll(x, ROPE // 2, 1) * s1 + pltpu.roll(x, w - ROPE // 2, 1) * s2


def _rope_t(dy, cos, s1, s2):
    w = dy.shape[1]
    return dy * cos + pltpu.roll(dy * s1, w - ROPE // 2, 1) + pltpu.roll(dy * s2, ROPE // 2, 1)


def _rope_lanes(shape):
    lane = lax.broadcasted_iota(jnp.int32, shape, 1) % HP
    return (lane >= NOPE) & (lane < QK)


ATT_HB_FWD, ATT_HB_BWD = 8, 8
ATT_KR_BWD = 2
ATT_BQ_FWD = 1024
ATT_SCALE = QK ** -0.5
LOG2E = 1.4426950408889634
LN2 = 0.6931471805599453
_NT = (((1,), (1,)), ((), ()))
_TN = (((0,), (0,)), ((), ()))


def _carry_specs(carry):
    if carry is None:
        return [], [], [], []
    arr = carry[0]
    return ([_HBM], [_HBM], [jax.ShapeDtypeStruct((4,) + arr.shape[1:], arr.dtype)],
            [pltpu.SemaphoreType.DMA((3,)), pltpu.SemaphoreType.DMA((3,))])


def _carry_start(exchange, when):
    @pl.when(when)
    def _():
        for cp in exchange(False):
            cp.start()


def _carry_wait(exchange, when):
    @pl.when(when)
    def _():
        for cp in exchange(True):
            cp.wait_recv()
        for cp in exchange(False):
            cp.wait_send()


def _tri_tables(n, by_row):
    if by_row:
        pairs = [(i, j) for i in range(n) for j in range(i + 1)]
    else:
        pairs = [(i, j) for j in range(n) for i in range(j, n)]
    return jnp.array([p[0] for p in pairs], jnp.int32), jnp.array([p[1] for p in pairs], jnp.int32)


def _flash_fwd(q, k, v, name, bq=512, carry=None):
    T = q.shape[0]
    bq = min(bq, T)
    nq = T // bq
    hb = ATT_HB_FWD
    n_carry = 0 if carry is None else 1

    i_tab, j_tab = _tri_tables(nq, by_row=True)

    def body(it_ref, jt_ref, q_ref, k_ref, v_ref, *rest):
        o_ref, lse_ref = rest[n_carry:n_carry + 2]
        m_sc, l_sc, acc_sc = rest[2 * n_carry + 2:2 * n_carry + 5]
        t = pl.program_id(1)
        i, j = it_ref[t], jt_ref[t]
        if carry is not None:
            exchange = functools.partial(_carried_exchange, rest[0], rest[3], rest[-2], rest[-1], carry[1])
            _carry_start(exchange, (pl.program_id(0) == 0) & (t == 0))

        @pl.when(j == 0)
        def _():
            m_sc[...] = jnp.full_like(m_sc, NEG_INF)
            l_sc[...] = jnp.zeros_like(l_sc)
            acc_sc[...] = jnp.zeros_like(acc_sc)

        def step(masked):
            for hh in range(hb):
                sl = slice(hh * HP, (hh + 1) * HP)
                s = lax.dot_general(q_ref[:, sl], k_ref[:, sl], _NT, preferred_element_type=F32)
                if masked:
                    row = lax.broadcasted_iota(jnp.int32, (bq, bq), 0)
                    col = lax.broadcasted_iota(jnp.int32, (bq, bq), 1)
                    s = jnp.where(col <= row, s, NEG_INF)
                m_prev = m_sc[hh]
                m_new = jnp.maximum(m_prev, jnp.max(s, axis=1, keepdims=True))
                alpha = jnp.exp2(m_prev - m_new)
                p = jnp.exp2(s - jnp.concatenate([m_new] * (bq // HP), axis=1))
                l_sc[hh] = alpha * l_sc[hh] + jnp.sum(p, axis=1, keepdims=True)
                acc_sc[hh] = alpha * acc_sc[hh] + jnp.dot(p.astype(BF16), v_ref[:, sl], preferred_element_type=F32)
                m_sc[hh] = m_new

        @pl.when(j < i)
        def _():
            step(False)

        @pl.when(j == i)
        def _():
            step(True)
            for hh in range(hb):
                o_ref[:, hh * HP:(hh + 1) * HP] = acc_sc[hh] / l_sc[hh]
                lse_ref[hh] = jnp.transpose((m_sc[hh] + jnp.log2(l_sc[hh])) * LN2)[0:1, :]

        if carry is not None:
            _carry_wait(exchange, (pl.program_id(0) == HEADS // hb - 1) & (t == i_tab.shape[0] - 1))

    qs = pl.BlockSpec((bq, hb * HP), lambda h, t, it, jt: (it[t], h))
    ks = pl.BlockSpec((bq, hb * HP), lambda h, t, it, jt: (jt[t], h))
    c_in, c_out, c_shape, c_sems = _carry_specs(carry)
    gs = pltpu.PrefetchScalarGridSpec(
        num_scalar_prefetch=2, grid=(HEADS // hb, i_tab.shape[0]), in_specs=[qs, ks, ks] + c_in,
        out_specs=[qs, pl.BlockSpec((hb, 1, bq), lambda h, t, it, jt: (h, 0, it[t]))] + c_out,
        scratch_shapes=[pltpu.VMEM((hb, bq, HP), F32)] * 3 + c_sems)
    return pl.pallas_call(
        body, name=name, grid_spec=gs,
        out_shape=[jax.ShapeDtypeStruct((T, HEADS * HP), F32), jax.ShapeDtypeStruct((HEADS, 1, T), F32)] + c_shape,
        compiler_params=pltpu.CompilerParams(dimension_semantics=("arbitrary", "arbitrary")),
    )(i_tab, j_tab, q, k, v, *([] if carry is None else [carry[0]]))


def _flash_bwd(q, k, v, do, lse_row, delta_row, name, bq=512, kr=ATT_KR_BWD, carry=None):
    T = q.shape[0]
    bq = min(bq, T)
    nq = T // bq
    kr = kr if nq % kr == 0 else 1
    bk, nk = bq * kr, nq // kr
    hb = ATT_HB_BWD
    n_carry = 0 if carry is None else 1

    pairs = [(i, j) for j in range(nk) for i in range(kr * j, nq)]
    i_tab = jnp.array([p[0] for p in pairs], jnp.int32)
    j_tab = jnp.array([p[1] for p in pairs], jnp.int32)
    n_blk = len(pairs)

    def body(it_ref, jt_ref, q_ref, k_ref, v_ref, do_ref, lse_ref, dl_ref, *rest):
        dq_ref, dk_ref, dv_ref = rest[n_carry:n_carry + 3]
        t = pl.program_id(1)
        i, j = it_ref[t], jt_ref[t]
        if carry is not None:
            exchange = functools.partial(_carried_exchange, rest[0], rest[4], rest[-2], rest[-1], carry[1])
            _carry_start(exchange, (pl.program_id(0) == 0) & (t == 0))

        @pl.when(t == 0)
        def _():
            dq_ref[...] = jnp.zeros_like(dq_ref)

        @pl.when(i == kr * j)
        def _():
            dk_ref[...] = jnp.zeros_like(dk_ref)
            dv_ref[...] = jnp.zeros_like(dv_ref)

        def step(masked):
            rows = pl.ds(pl.multiple_of(i * bq, bq), bq)
            for hh in range(hb):
                sl = slice(hh * HP, (hh + 1) * HP)
                qh, kh, doh = q_ref[:, sl], k_ref[:, sl], do_ref[:, sl]
                st = lax.dot_general(kh, qh, _NT, preferred_element_type=F32)
                pt = jnp.exp2(st - lse_ref[hh] * LOG2E)
                if masked:
                    krow = lax.broadcasted_iota(jnp.int32, (bk, bq), 0)
                    qcol = lax.broadcasted_iota(jnp.int32, (bk, bq), 1) + (i * bq - j * bk)
                    pt = jnp.where(krow <= qcol, pt, 0.0)
                dv_ref[:, sl] += jnp.dot(pt.astype(BF16), doh, preferred_element_type=F32)
                dpt = lax.dot_general(v_ref[:, sl], doh, _NT, preferred_element_type=F32)
                dst = (pt * (dpt - dl_ref[hh])).astype(BF16)
                dk_ref[:, sl] += jnp.dot(dst, qh, preferred_element_type=F32)
                dq_ref[rows, sl] += lax.dot_general(dst, kh, _TN, preferred_element_type=F32)

        @pl.when(i >= kr * (j + 1))
        def _():
            step(False)

        @pl.when(i < kr * (j + 1))
        def _():
            step(True)

        @pl.when(i == nq - 1)
        def _():
            dk_ref[...] *= LN2

        @pl.when(t == n_blk - 1)
        def _():
            dq_ref[...] *= ATT_SCALE

        if carry is not None:
            _carry_wait(exchange, (pl.program_id(0) == HEADS // hb - 1) & (t == n_blk - 1))

    qs = pl.BlockSpec((bq, hb * HP), lambda h, t, it, jt: (it[t], h))
    ks = pl.BlockSpec((bk, hb * HP), lambda h, t, it, jt: (jt[t], h))
    ko = pl.BlockSpec((bk, hb * HP), lambda h, t, it, jt: (jt[t], h), pipeline_mode=pl.Buffered(1))
    rs = pl.BlockSpec((hb, 1, bq), lambda h, t, it, jt: (h, 0, it[t]))
    full = pl.BlockSpec((T, hb * HP), lambda h, t, it, jt: (0, h), pipeline_mode=pl.Buffered(1))
    c_in, c_out, c_shape, c_sems = _carry_specs(carry)
    gs = pltpu.PrefetchScalarGridSpec(num_scalar_prefetch=2, grid=(HEADS // hb, n_blk),
                                      in_specs=[qs, ks, ks, qs, rs, rs] + c_in, out_specs=[full, ko, ko] + c_out,
                                      scratch_shapes=c_sems)
    return pl.pallas_call(
        body, name=name, grid_spec=gs, out_shape=[jax.ShapeDtypeStruct((T, HEADS * HP), F32)] * 3 + c_shape,
        compiler_params=pltpu.CompilerParams(dimension_semantics=("arbitrary", "arbitrary")),
    )(i_tab, j_tab, q, k, v, do, lse_row, delta_row, *([] if carry is None else [carry[0]]))


def _xattn_heads(q, k, v):
    scale = XD ** -0.5
    ps = []
    for h in range(XH):
        sl = slice(h * XD, (h + 1) * XD)
        s = lax.dot_general(q[:, sl].astype(BF16), k[:, sl].astype(BF16), (((1,), (1,)), ((), ())),
                            preferred_element_type=F32) * scale
        e = jnp.exp(s - jnp.max(s, axis=1, keepdims=True))
        ps.append(e / jnp.sum(e, axis=1, keepdims=True))
    return ps


def _xattn_fwd_fn(q, k, v):
    ps = _xattn_heads(q, k, v)
    o = [jnp.dot(p.astype(BF16), v[:, h * XD:(h + 1) * XD].astype(BF16), preferred_element_type=F32)
         for h, p in enumerate(ps)]
    return (jnp.concatenate(o, axis=1),)


def _xattn_bwd_fn(q, do, k, v):
    scale = XD ** -0.5
    ps = _xattn_heads(q, k, v)
    dqs, dks, dvs = [], [], []
    tdims = (((0,), (0,)), ((), ()))
    for h, p in enumerate(ps):
        sl = slice(h * XD, (h + 1) * XD)
        doh = do[:, sl].astype(BF16)
        dvs.append(lax.dot_general(p.astype(BF16), doh, tdims, preferred_element_type=F32))
        dp = lax.dot_general(doh, v[:, sl].astype(BF16), (((1,), (1,)), ((), ())), preferred_element_type=F32)
        ds = (p * (dp - jnp.sum(dp * p, axis=1, keepdims=True)) * scale).astype(BF16)
        dqs.append(jnp.dot(ds, k[:, sl].astype(BF16), preferred_element_type=F32))
        dks.append(lax.dot_general(ds, q[:, sl].astype(BF16), tdims, preferred_element_type=F32))
    return jnp.concatenate(dqs, axis=1), jnp.concatenate(dks, axis=1), jnp.concatenate(dvs, axis=1)


S5_SUB = 8


def _cmul_add(xr, xi, ar, ai, sr, si):
    return xr + ar * sr - ai * si, xi + ar * si + ai * sr


def _s5_tables(ar, ai, reverse):
    pows = [(ar, ai)]
    for _ in range(S5_SUB - 1):
        pr, pi = pows[-1]
        pows.append((pr * ar - pi * ai, pr * ai + pi * ar))
    cr = jnp.concatenate([p[0] for p in pows], axis=0)
    ci = jnp.concatenate([p[1] for p in pows], axis=0)
    if reverse:
        cr, ci = cr[::-1], ci[::-1]
    t = jnp.arange(S5_SUB)[:, None]
    blocks = [jnp.concatenate([cr, ci], axis=1)]
    for s in (1, 2, 4):
        keep = (t < S5_SUB - s) if reverse else (t >= s)
        sr, si = pows[s - 1]
        blocks.append(jnp.concatenate([jnp.where(keep, sr, 0.0), jnp.where(keep, si, 0.0)], axis=1))
    return jnp.concatenate(blocks, axis=0)


def _sub_scan(xr, xi, tab_ref, reverse):
    for k, s in enumerate((1, 2, 4)):
        blk = slice(S5_SUB * (k + 1), S5_SUB * (k + 2))
        sh = S5_SUB - s if reverse else s
        xr, xi = _cmul_add(xr, xi, tab_ref[blk, :S5_N], tab_ref[blk, S5_N:], pltpu.roll(xr, sh, 0), pltpu.roll(xi, sh, 0))
    return xr, xi


def _s5_fwd(z, b_st, c_st, d_skip, tab, name, tc=1024):
    T = z.shape[0]
    tc = min(tc, T)
    nsub = tc // S5_SUB

    def body(u_ref, bst_ref, cst_ref, d_ref, tab_ref, h_ref, yc_ref, yg_ref, carry, x_sc):
        @pl.when(pl.program_id(0) == 0)
        def _():
            carry[...] = jnp.zeros_like(carry)

        u = u_ref[...].astype(F32)
        x_sc[...] = jnp.dot(u.astype(BF16), bst_ref[...], preferred_element_type=F32)

        def sub(b, c):
            r = pl.ds(pl.multiple_of(b * S5_SUB, S5_SUB), S5_SUB)
            xr, xi = _sub_scan(x_sc[r, :S5_N], x_sc[r, S5_N:], tab_ref, False)
            hr, hi = _cmul_add(xr, xi, tab_ref[0:S5_SUB, :S5_N], tab_ref[0:S5_SUB, S5_N:], c[0], c[1])
            x_sc[r, :S5_N] = hr
            x_sc[r, S5_N:] = hi
            return hr[S5_SUB - 1:S5_SUB], hi[S5_SUB - 1:S5_SUB]

        cr, ci = lax.fori_loop(0, nsub, sub, (carry[0:1, :S5_N], carry[0:1, S5_N:]))
        carry[0:1, :S5_N] = cr
        carry[0:1, S5_N:] = ci
        h = x_sc[...]
        h_ref[...] = h
        yc = jnp.dot(h.astype(BF16), cst_ref[...], preferred_element_type=F32)
        yc_ref[...] = yc
        yg_ref[...] = _gelu_skip(yc, u, d_ref[...]).astype(yg_ref.dtype)

    whole = lambda a: pl.BlockSpec(a.shape, lambda i: (0, 0))
    row = lambda w: pl.BlockSpec((tc, w), lambda i: (i, 0))
    return pl.pallas_call(
        body, name=name, grid=(T // tc,),
        in_specs=[pl.BlockSpec((tc, S5_W), lambda i: (i, Z_U // S5_W)), whole(b_st), whole(c_st), whole(d_skip), whole(tab)],
        out_specs=[row(2 * S5_N), row(S5_W), row(S5_W)],
        out_shape=[jax.ShapeDtypeStruct((T, 2 * S5_N), F32), jax.ShapeDtypeStruct((T, S5_W), F32),
                   jax.ShapeDtypeStruct((T, S5_W), BF16)],
        scratch_shapes=[pltpu.VMEM((8, 2 * S5_N), F32), pltpu.VMEM((tc, 2 * S5_N), F32)],
        compiler_params=pltpu.CompilerParams(dimension_semantics=("arbitrary",)),
    )(z, b_st, c_st, d_skip, tab)


def _s5_bwd(dyg, yc, z, h, b_st, c_st, d_skip, tab, name, tc=1024):
    T = z.shape[0]
    tc = min(tc, T)
    nc, nsub = T // tc, tc // S5_SUB

    def body(dyg_ref, yc_ref, u_ref, h_ref, hp_ref, bst_ref, cst_ref, d_ref, tab_ref,
             du_ref, da_ref, db_ref, dct_ref, dd_ref, carry, x_sc):
        g = pl.program_id(0)

        @pl.when(g == 0)
        def _():
            carry[...] = jnp.zeros_like(carry)
            for r in (da_ref, db_ref, dct_ref, dd_ref):
                r[...] = jnp.zeros_like(r)

        u = u_ref[...].astype(F32)
        _, vjp = jax.vjp(_gelu_skip, yc_ref[...], u, d_ref[...])
        dyc, du_skip, dd = vjp(dyg_ref[...])
        dd_ref[...] += dd
        dyc16 = dyc.astype(BF16)
        dct_ref[...] += lax.dot_general(dyc16, h_ref[...].astype(BF16), _TN, preferred_element_type=F32)
        x_sc[...] = lax.dot_general(dyc16, cst_ref[...], _NT, preferred_element_type=F32)
        first = jnp.where(g == nc - 1, 0.0, 1.0)
        row0 = lax.broadcasted_iota(jnp.int32, (S5_SUB, S5_N), 0) == 0

        def sub(k, c):
            cr, ci, acc_r, acc_i = c
            b = nsub - 1 - k
            r = pl.ds(pl.multiple_of(b * S5_SUB, S5_SUB), S5_SUB)
            xr, xi = _sub_scan(x_sc[r, :S5_N], x_sc[r, S5_N:], tab_ref, True)
            lr, li = _cmul_add(xr, xi, tab_ref[0:S5_SUB, :S5_N], tab_ref[0:S5_SUB, S5_N:], cr, ci)
            x_sc[r, :S5_N] = lr
            x_sc[r, S5_N:] = li
            rp = pl.ds(pl.multiple_of(jnp.maximum(b - 1, 0) * S5_SUB, S5_SUB), S5_SUB)
            last_r = jnp.where(b == 0, hp_ref[S5_SUB - 1:S5_SUB, :S5_N] * first, h_ref[rp, :S5_N][S5_SUB - 1:S5_SUB])
            last_i = jnp.where(b == 0, hp_ref[S5_SUB - 1:S5_SUB, S5_N:] * first, h_ref[rp, S5_N:][S5_SUB - 1:S5_SUB])
            pr = jnp.where(row0, last_r, pltpu.roll(h_ref[r, :S5_N], 1, 0))
            pi = jnp.where(row0, last_i, pltpu.roll(h_ref[r, S5_N:], 1, 0))
            return lr[0:1], li[0:1], acc_r + (lr * pr + li * pi), acc_i + (li * pr - lr * pi)

        zero = jnp.zeros((S5_SUB, S5_N), F32)
        cr, ci, acc_r, acc_i = lax.fori_loop(0, nsub, sub, (carry[0:1, :S5_N], carry[0:1, S5_N:], zero, zero))
        carry[0:1, :S5_N] = cr
        carry[0:1, S5_N:] = ci
        da_ref[0:1, :S5_N] += jnp.sum(acc_r, axis=0, keepdims=True)
        da_ref[0:1, S5_N:] += jnp.sum(acc_i, axis=0, keepdims=True)
        lam16 = x_sc[...].astype(BF16)
        db_ref[...] += lax.dot_general(u.astype(BF16), lam16, _TN, preferred_element_type=F32)
        du = lax.dot_general(lam16, bst_ref[...], _NT, preferred_element_type=F32) + du_skip
        du_ref[...] = du.astype(du_ref.dtype)

    whole = lambda a: pl.BlockSpec(a.shape, lambda g: (0, 0))
    row = lambda w, off=0: pl.BlockSpec((tc, w), lambda g: (nc - 1 - g, off))
    prev = pl.BlockSpec((S5_SUB, 2 * S5_N), lambda g: (jnp.maximum((nc - 1 - g) * nsub - 1, 0), 0))
    acc = lambda s: pl.BlockSpec(s, lambda g: (0, 0))
    return pl.pallas_call(
        body, name=name, grid=(nc,),
        in_specs=[row(S5_W), row(S5_W), row(S5_W, Z_U // S5_W), row(2 * S5_N), prev, whole(b_st), whole(c_st),
                  whole(d_skip), whole(tab)],
        out_specs=[row(S5_W), acc((1, 2 * S5_N)), acc((S5_W, 2 * S5_N)), acc((S5_W, 2 * S5_N)), acc((1, S5_W))],
        out_shape=[jax.ShapeDtypeStruct((T, S5_W), BF16), jax.ShapeDtypeStruct((1, 2 * S5_N), F32),
                   jax.ShapeDtypeStruct((S5_W, 2 * S5_N), F32), jax.ShapeDtypeStruct((S5_W, 2 * S5_N), F32),
                   jax.ShapeDtypeStruct((1, S5_W), F32)],
        scratch_shapes=[pltpu.VMEM((8, 2 * S5_N), F32), pltpu.VMEM((tc, 2 * S5_N), F32)],
        compiler_params=pltpu.CompilerParams(dimension_semantics=("arbitrary",)),
    )(dyg, yc, z, h, h, b_st, c_st, d_skip, tab)


def _s5_discretize(lam_re, lam_im, log_dt, b_re, b_im):
    lr = jnp.minimum(lam_re, S5_MAX_RE)
    li = lam_im
    dt = jnp.exp(log_dt)[:, None]
    mag = jnp.exp(lr * dt)
    ang = li * dt
    ab_re = mag * jnp.cos(ang)
    ab_im = mag * jnp.sin(ang)
    den = lr * lr + li * li
    nr = ab_re - 1.0
    f_re = ((nr * lr + ab_im * li) / den)[..., None]
    f_im = ((ab_im * lr - nr * li) / den)[..., None]
    return ab_re, ab_im, f_re * b_re - f_im * b_im, f_re * b_im + f_im * b_re


def _adamw_fn(w, g, m, v):
    m = ADAM_B1 * m + (1.0 - ADAM_B1) * g
    v = ADAM_B2 * v + (1.0 - ADAM_B2) * (g * g)
    m_hat = m / (1.0 - ADAM_B1 ** ADAM_STEP)
    v_hat = v / (1.0 - ADAM_B2 ** ADAM_STEP)
    delta = -ADAM_LR * (m_hat / (jnp.sqrt(v_hat) + ADAM_EPS) + ADAM_WD * w)
    return delta, m, v


def _adamw(w, g, m, v, name):
    wd = w.shape[1]
    return _rowwise(_adamw_fn, name, [_whole(w), _whole(g), _whole(m), _whole(v)], [], [(wd, F32)] * 3,
                    tm=_row_tile(w.shape[0]))


def _adamw_many(ws, gs, ms, vs, name):
    n = len(ws)

    def body(*refs):
        for i in range(n):
            d_, m_, v_ = _adamw_fn(*[refs[kk * n + i][...] for kk in range(4)])
            refs[4 * n + i][...] = d_
            refs[5 * n + i][...] = m_
            refs[6 * n + i][...] = v_

    out = pl.pallas_call(body, name=name, out_shape=[jax.ShapeDtypeStruct(w.shape, F32) for w in ws] * 3)(*ws, *gs, *ms, *vs)
    return out[:n], out[n:2 * n], out[2 * n:]


def _me():
    return lax.axis_index("x"), lax.axis_index("y"), lax.axis_index("c")


_HBM = pl.BlockSpec(memory_space=pl.ANY)


def _gather_weights(shard):
    _, R, W = shard.shape

    def body(s_ref, out_ref, send_sems, recv_sems):
        x, y, c = _me()
        mine = 2 * x + y
        sib = (x, y, 1 - c)
        chips = [(1 - x, y), (x, 1 - y), (1 - x, 1 - y)]

        def rcopy(kk, src, chip_idx, half, to):
            return pltpu.make_async_remote_copy(src_ref=src, dst_ref=out_ref.at[chip_idx, half],
                                                send_sem=send_sems.at[kk], recv_sem=recv_sems.at[kk],
                                                device_id=to, device_id_type=MESH)

        first = [rcopy(j, s_ref.at[c], mine, c, (*chip, c)) for j, chip in enumerate(chips)]
        for cp in first:
            cp.start()
        passed = []
        for j, (px, py) in enumerate(chips):
            src_chip = 2 * px + py
            rcopy(j, s_ref.at[c], src_chip, c, (x, y, c)).wait_recv()
            fwd = rcopy(3 + j, out_ref.at[src_chip, c], src_chip, c, sib)
            fwd.start()
            passed.append(fwd)
        for j, (px, py) in enumerate(chips):
            rcopy(3 + j, s_ref.at[c], 2 * px + py, 1 - c, (x, y, c)).wait_recv()
        for cp in first + passed:
            cp.wait_send()

    return pl.pallas_call(
        body, name="gather_weights", in_specs=[_HBM], out_specs=_HBM,
        out_shape=jax.ShapeDtypeStruct((4, 2, R, W), shard.dtype),
        scratch_shapes=[pltpu.SemaphoreType.DMA((6,)), pltpu.SemaphoreType.DMA((6,))],
    )(shard)


def _pair_exchange(g, name):
    _, _, R, W = g.shape

    def body(g_ref, out_ref, send_sem, recv_sem):
        x, y, c = _me()
        cp = pltpu.make_async_remote_copy(src_ref=g_ref.at[:, 1 - c], dst_ref=out_ref, send_sem=send_sem,
                                          recv_sem=recv_sem, device_id=(x, y, 1 - c), device_id_type=MESH)
        cp.start()
        cp.wait()

    return pl.pallas_call(
        body, name=name, in_specs=[_HBM], out_specs=_HBM,
        out_shape=jax.ShapeDtypeStruct((4, R, W), g.dtype),
        scratch_shapes=[pltpu.SemaphoreType.DMA(()), pltpu.SemaphoreType.DMA(())],
    )(g)


def _pair_add(g, recv, cidx, name):
    _, _, R, W = g.shape
    tr = _row_tile(R)

    def body(c_ref, a_ref, b_ref, o_ref):
        o_ref[...] = (a_ref[...].astype(F32) + b_ref[...].astype(F32)).astype(o_ref.dtype)

    gs = pltpu.PrefetchScalarGridSpec(
        num_scalar_prefetch=1, grid=(4, R // tr),
        in_specs=[pl.BlockSpec((None, None, tr, W), lambda j, r, c: (j, c[0], r, 0)),
                  pl.BlockSpec((None, tr, W), lambda j, r, c: (j, r, 0))],
        out_specs=pl.BlockSpec((None, tr, W), lambda j, r, c: (j, r, 0)))
    return pl.pallas_call(body, name=name, grid_spec=gs,
                          out_shape=jax.ShapeDtypeStruct((4, R, W), BF16))(cidx, g, recv)


def _carried_exchange(src_ref, dst_ref, send_sems, recv_sems, spread, incoming):
    x, y, c = _me()
    mine = 2 * x + y
    copies = []
    for j, (px, py) in enumerate([(1 - x, y), (x, 1 - y), (1 - x, 1 - y)]):
        src = src_ref.at[2 * px + py] if spread else src_ref.at[c]
        slot = 2 * px + py if incoming else mine
        copies.append(pltpu.make_async_remote_copy(src_ref=src, dst_ref=dst_ref.at[slot], send_sem=send_sems.at[j],
                                                   recv_sem=recv_sems.at[j], device_id=(px, py, c), device_id_type=MESH))
    return copies


def _chip_exchange(p, name):
    _, R, W = p.shape

    def body(p_ref, out_ref, send_sems, recv_sems):
        outs = _carried_exchange(p_ref, out_ref, send_sems, recv_sems, True, False)
        for cp in outs:
            cp.start()
        for cp in _carried_exchange(p_ref, out_ref, send_sems, recv_sems, True, True):
            cp.wait_recv()
        for cp in outs:
            cp.wait_send()

    return pl.pallas_call(
        body, name=name, in_specs=[_HBM], out_specs=_HBM,
        out_shape=jax.ShapeDtypeStruct((4, R, W), p.dtype),
        scratch_shapes=[pltpu.SemaphoreType.DMA((3,)), pltpu.SemaphoreType.DMA((3,))],
    )(p)


def _chip_sum(q, name):
    _, R, W = q.shape
    tr = _row_tile(R)

    def body(q_ref, o_ref):
        f = lambda t: q_ref[t].astype(F32)
        o_ref[...] = ((f(0) + f(1)) + f(2)) + f(3)

    return pl.pallas_call(body, name=name, grid=(R // tr,),
                          in_specs=[pl.BlockSpec((4, tr, W), lambda r: (0, r, 0))],
                          out_specs=pl.BlockSpec((tr, W), lambda r: (r, 0)),
                          out_shape=jax.ShapeDtypeStruct((R, W), F32))(q)


def _pair_share(r, name):
    R, W = r.shape

    def body(r_ref, out_ref, send_sem, recv_sem):
        x, y, c = _me()
        cp = pltpu.make_async_remote_copy(src_ref=r_ref, dst_ref=out_ref, send_sem=send_sem, recv_sem=recv_sem,
                                          device_id=(x, y, 1 - c), device_id_type=MESH)
        cp.start()
        cp.wait()

    return pl.pallas_call(
        body, name=name, in_specs=[_HBM], out_specs=_HBM,
        out_shape=jax.ShapeDtypeStruct((R, W), r.dtype),
        scratch_shapes=[pltpu.SemaphoreType.DMA(()), pltpu.SemaphoreType.DMA(())],
    )(r)


def _allreduce_small(vec):
    R, W = vec.shape

    def body(v_ref, out_ref, buf, send_sems, recv_sems):
        x, y, c = _me()
        me = 4 * x + 2 * y + c
        buf[me] = v_ref[...]
        cps = []
        for kk in range(1, 8):
            peer = (x ^ (kk >> 2), y ^ ((kk >> 1) & 1), c ^ (kk & 1))
            cp = pltpu.make_async_remote_copy(src_ref=v_ref, dst_ref=buf.at[me], send_sem=send_sems.at[kk - 1],
                                              recv_sem=recv_sems.at[kk - 1], device_id=peer, device_id_type=MESH)
            cp.start()
            cps.append(cp)
        for kk in range(1, 8):
            peer = (x ^ (kk >> 2), y ^ ((kk >> 1) & 1), c ^ (kk & 1))
            pltpu.make_async_remote_copy(src_ref=v_ref, dst_ref=buf.at[me ^ kk], send_sem=send_sems.at[kk - 1],
                                         recv_sem=recv_sems.at[kk - 1], device_id=peer, device_id_type=MESH).wait_recv()
        for cp in cps:
            cp.wait_send()
        acc = buf[0]
        for d in range(1, 8):
            acc = acc + buf[d]
        out_ref[...] = acc

    vm = pl.BlockSpec(memory_space=pltpu.VMEM)
    return pl.pallas_call(
        body, name="allreduce_small", in_specs=[vm], out_specs=vm, out_shape=jax.ShapeDtypeStruct((R, W), F32),
        scratch_shapes=[pltpu.VMEM((8, R, W), F32), pltpu.SemaphoreType.DMA((7,)), pltpu.SemaphoreType.DMA((7,))],
    )(vec)


def _pack_flat(parts, align, dtype):
    flat = jnp.concatenate([p.reshape(-1).astype(dtype) for p in parts])
    n = flat.shape[0]
    pad = (-n) % align
    return jnp.pad(flat, (0, pad)) if pad else flat


def _unpack_flat(flat, shapes):
    out, off = [], 0
    for s in shapes:
        n = math.prod(s)
        out.append(flat[off:off + n].reshape(s))
        off += n
    return out


def _full_from_shards(name, sh):
    if name in COL_SHARDED:
        return jnp.transpose(sh, (1, 0, 2)).reshape(sh.shape[1], 4 * sh.shape[2])
    return sh.reshape(4 * sh.shape[1], sh.shape[2])


def _shards_from_full(name, full):
    if name in COL_SHARDED:
        r, cc = full.shape[0], full.shape[1] // 4
        return jnp.transpose(full.reshape(r, 4, cc), (1, 0, 2)).reshape(4, r * cc)
    return full.reshape(4, -1)


def _pad_heads(w, width):
    k = w.shape[0]
    return jnp.pad(w.reshape(k, HEADS, width), ((0, 0), (0, 0), (0, HP - width))).reshape(k, HEADS * HP)


def _unpad_heads(w, width):
    k = w.shape[0]
    return w.reshape(k, HEADS, HP)[:, :, :width].reshape(k, HEADS * width)


def _blockdiag(t):
    g, a, b = t.shape
    return jnp.einsum("gab,gk->gakb", t, jnp.eye(g, dtype=t.dtype)).reshape(g * a, g * b)


def _blockdiag_t(m, a, b):
    g = m.shape[0] // a
    return jnp.einsum("gagb->gab", m.reshape(g, a, g, b))


def _local_step(x, mem, positions, target, W, sp, overlap=None):
    T = x.shape[0]
    row = lambda v: v.reshape(1, -1)

    ab_re, ab_im, bb_re, bb_im = _s5_discretize(sp["s5_lam_re"], sp["s5_lam_im"], sp["s5_log_dt"], sp["s5_b_re"], sp["s5_b_im"])
    ar, ai = ab_re.reshape(1, S5_N), ab_im.reshape(1, S5_N)
    tab_fwd, tab_rev = _s5_tables(ar, ai, False), _s5_tables(ar, -ai, True)
    b_st = jnp.concatenate([_blockdiag(jnp.swapaxes(bb_re, 1, 2)), _blockdiag(jnp.swapaxes(bb_im, 1, 2))],
                           axis=1).astype(BF16)
    c_st = jnp.concatenate([_blockdiag(jnp.swapaxes(sp["s5_c_re"], 1, 2)),
                            -_blockdiag(jnp.swapaxes(sp["s5_c_im"], 1, 2))], axis=0).astype(BF16)
    inv = ROPE_THETA ** (-jnp.arange(0, ROPE, 2, dtype=F32) / ROPE)
    lane = jnp.arange(HP)
    rot1, rot2 = (lane >= NOPE) & (lane < NOPE + ROPE // 2), (lane >= NOPE + ROPE // 2) & (lane < QK)
    ang = positions.astype(F32)[:, None] * jnp.where(rot1 | rot2, inv[(lane - NOPE) % (ROPE // 2)], 0.0)
    sn = jnp.sin(ang)
    t_cos, t_s1, t_s2 = jnp.cos(ang), jnp.where(rot2, sn, 0.0), jnp.where(rot1, -sn, 0.0)

    w_in = W["w_in"]
    w_in_re = jnp.concatenate([w_in[:, 800:], w_in[:, :768], w_in[:, 768:800],
                               jnp.zeros((D_MODEL, HP - ROPE), w_in.dtype)], axis=1)
    w_uq_p = _pad_heads(W["w_uq"], QK)
    wkv = W["w_ukv"].reshape(S5_W, HEADS, NOPE + VD)
    w_uk_p = _pad_heads(wkv[:, :, :NOPE].reshape(S5_W, HEADS * NOPE), NOPE)
    w_uv_p = _pad_heads(wkv[:, :, NOPE:].reshape(S5_W, HEADS * VD), VD)
    w_oa_p = jnp.pad(W["w_oa"].reshape(HEADS, VD, D_MODEL), ((0, 0), (0, HP - VD), (0, 0))).reshape(HEADS * HP, D_MODEL)

    g_in, b_in = row(sp["ln_in_g"]), row(sp["ln_in_b"])
    g1, b1, g2, b2, g3, b3 = (row(sp[k]) for k in ("ln1_g", "ln1_b", "ln2_g", "ln2_b", "ln3_g", "ln3_b"))
    gq, gkv, d_skip = row(sp["q_norm_g"]), row(sp["kv_norm_g"]), row(sp["s5_d"])

    (h0,) = _rowwise(lambda a, g, b: (_ln(a, g, b),), "ln_in_fwd", [_whole(x)], [g_in, b_in], [(D_MODEL, F32)], tm=512)
    def in_epi(acc, ga, gb):
        zb = acc.astype(BF16).astype(F32)
        return acc, _rms(zb[:, Z_CQ:Z_CQ + S5_W], ga), _rms(zb[:, Z_CKV:Z_CKV + S5_W], gb)

    z, cqn, ckvn = _mm(h0, w_in_re, "in_proj_norm", tm=512, tn=Z_W, epi=in_epi, pextras=(gq, gkv),
                       outs=[(Z_W, BF16), (S5_W, BF16), (S5_W, BF16)])
    hs, yc, yg = _s5_fwd(z, b_st, c_st, d_skip, tab_fwd, "s5_fwd")
    y2 = _mm(yg, W["w_glu"], "glu_proj", tn=2048, out_dtype=BF16)
    tabs = [(t_cos, 0, HP), (t_s1, 0, HP), (t_s2, 0, HP)]

    def rope_q(acc, tc_, t1, t2):
        return _rope(acc, _tile8(tc_), _tile8(t1), _tile8(t2)) * (ATT_SCALE * LOG2E)

    def rope_k(acc, krz, tc_, t1, t2):
        return acc + _tile8(_rope(pltpu.roll(krz.astype(F32), NOPE, 1), tc_, t1, t2))

    q = _mm(cqn, w_uq_p, "mla_uq_rope", epi=rope_q, rextras=tabs, out_dtype=BF16)
    k = _mm(ckvn, w_uk_p, "mla_uk_rope", epi=rope_k, rextras=[(z, Z_KR, HP)] + tabs, out_dtype=BF16)
    v = _mm(ckvn, w_uv_p, "mla_uv", out_dtype=BF16)
    if overlap is None:
        o, lse = _flash_fwd(q, k, v, "mla_attn_fwd", bq=ATT_BQ_FWD)
    else:
        o, lse, landed = _flash_fwd(q, k, v, "mla_attn_fwd", bq=ATT_BQ_FWD, carry=(overlap[0], False))
        W = {**W, **overlap[1](landed)}
    def gate_epi(acc, gs_, ga_, ya, yb):
        f = lambda t: t.astype(F32)
        return acc, _gate_mix(f(gs_), f(ga_), f(ya), f(yb), acc.astype(BF16).astype(F32))

    a_out, mixin = _mm(o, w_oa_p, "mla_oa_gate", epi=gate_epi, outs=[(D_MODEL, BF16)] * 2, tm=512,
                       rextras=[(z, Z_GS, D_MODEL), (z, Z_GA, D_MODEL), (y2, 0, D_MODEL), (y2, D_MODEL, D_MODEL)])
    def ln_epi(acc, h, g, b):
        s_ = DN_ALPHA * h + acc
        return s_, _ln(s_, g, b)

    two_rows = [(D_MODEL, F32)] * 2
    s1, h1 = _mm(mixin, W["w_o"], "mix_o_ln1", epi=ln_epi, extras=(h0,), pextras=(g1, b1), outs=two_rows, tm=512)
    xq = _mm(h1, W["w_xq"], "x_q", out_dtype=BF16)
    xk = _mm(mem, W["w_xk"], "x_k", out_dtype=BF16)
    xv = _mm(mem, W["w_xv"], "x_v", out_dtype=BF16)
    (xo,) = _rowwise(_xattn_fwd_fn, "xattn_fwd", [_whole(xq)], [xk, xv], [(D_MODEL, BF16)], tm=2048, upcast=False)
    s2, h2 = _mm(xo, W["w_xo"], "x_o_ln2", epi=ln_epi, extras=(h1,), pextras=(g2, b2), outs=two_rows, tm=512)
    a_up = _mm(h2, W["w_up"], "mlp_up", out_dtype=BF16, tn=2048)
    def loss_epi(acc, h, tgt, g, b):
        def f(r_, g_, b_):
            e = _ln_res(h, r_, g_, b_) - tgt
            return 0.5 * jnp.sum(jnp.mean(e * e, axis=-1))
        lv, (dr_, dg_, db_) = jax.value_and_grad(f, argnums=(0, 1, 2))(acc, g, b)
        return dr_, dg_, db_, jnp.broadcast_to(lv, (1, 128))

    row_and_sums = [(D_MODEL, F32), (D_MODEL, "sum"), (D_MODEL, "sum")]
    dr3, dg3, db3, lossv = _mm(a_up, W["w_down"], "mlp_down_loss", act=_relu2, tm=512, tk=4 * D_MODEL, epi=loss_epi,
                               extras=(h2, target), pextras=(g3, b3), outs=row_and_sums + [(128, "sum")])
    gW, gs = {}, {"ln3_g": dg3, "ln3_b": db3}

    da = _mm(dr3, W["w_down"], "mlp_down_bwd_a", "nt", epi=lambda acc, a: acc * (2.0 * jnp.maximum(a, 0.0)), extras=(a_up,),
             out_dtype=BF16, tn=2048)
    gW["w_down"] = _mm(a_up, dr3, "mlp_down_bwd_w", "tn", m=4 * D_MODEL, n=D_MODEL, act=_relu2, tk=2048)
    gW["w_up"] = _mm(h2, da, "mlp_up_bwd_w", "tn", m=D_MODEL, n=4 * D_MODEL, tk=2048)

    def ln_bwd_epi(acc, e, s_, g, b):
        _, vjp = jax.vjp(_ln, s_, g, b)
        return vjp(acc + DN_ALPHA * e)

    dr2, gs["ln2_g"], gs["ln2_b"] = _mm(da, W["w_up"], "mlp_up_bwd_ln2", "nt", tm=512, tk=4 * D_MODEL, epi=ln_bwd_epi,
                                        extras=(dr3, s2), pextras=(g2, b2), outs=row_and_sums)
    dxo = _mm(dr2, W["w_xo"], "x_o_bwd_a", "nt", out_dtype=BF16)
    gW["w_xo"] = _mm(xo, dr2, "x_o_bwd_w", "tn", m=D_MODEL, n=D_MODEL, tk=2048)
    dxq, dxk, dxv = _rowwise(_xattn_bwd_fn, "xattn_bwd", [_whole(xq), _whole(dxo)], [xk, xv], [(D_MODEL, BF16)],
                             [(xk.shape[0], D_MODEL)] * 2, tm=2048, upcast=False)
    gW["w_xq"] = _mm(h1, dxq, "x_q_bwd_w", "tn", m=D_MODEL, n=D_MODEL, tk=2048)
    gW["w_xk"] = _mm(mem, dxk, "x_k_bwd_w", "tn", m=D_MODEL, n=D_MODEL)
    gW["w_xv"] = _mm(mem, dxv, "x_v_bwd_w", "tn", m=D_MODEL, n=D_MODEL)

    dr1, gs["ln1_g"], gs["ln1_b"] = _mm(dxq, W["w_xq"], "x_q_bwd_ln1", "nt", tm=512, epi=ln_bwd_epi,
                                        extras=(dr2, s1), pextras=(g1, b1), outs=row_and_sums)
    gW["w_o"] = _mm(mixin, dr1, "mix_o_bwd_w", "tn", m=D_MODEL, n=D_MODEL, tk=2048)

    def gate_bwd_epi(acc, *tiles):
        _, vjp = jax.vjp(_gate_mix, *[t.astype(F32) for t in tiles])
        return vjp(acc)

    dgs, dga, dy2a, dy2b, da_out = _mm(
        dr1, W["w_o"], "mix_o_bwd_gate", "nt", tm=512, epi=gate_bwd_epi, outs=[(D_MODEL, BF16)] * 5,
        rextras=[(z, Z_GS, D_MODEL), (z, Z_GA, D_MODEL), (y2, 0, D_MODEL), (y2, D_MODEL, D_MODEL), _whole(a_out)])

    def delta_epi(acc, o_):
        prod = acc * o_
        cols = [jnp.sum(prod[:, h * HP:(h + 1) * HP], axis=1, keepdims=True) for h in range(HEADS)]
        return acc, jnp.concatenate(cols, axis=1)

    do16, delta = _mm(da_out, w_oa_p, "mla_oa_bwd_a", "nt", epi=delta_epi, extras=(o,),
                      outs=[(HEADS * HP, BF16), (HEADS, F32)], tm=512)
    g_oa_p = _mm(o, da_out, "mla_oa_bwd_w", "tn", m=HEADS * HP, n=D_MODEL, tk=2048)
    gW["w_oa"] = g_oa_p.reshape(HEADS, HP, D_MODEL)[:, :VD].reshape(HEADS * VD, D_MODEL)
    delta_row = delta.T.reshape(HEADS, 1, T)
    if overlap is None:
        dq, dk, dv = _flash_bwd(q, k, v, do16, lse, delta_row, "mla_attn_bwd")
        riding = None
    else:
        sent = overlap[2]({n: gW[n] for n in BIG_LATE})
        dq, dk, dv, landed = _flash_bwd(q, k, v, do16, lse, delta_row, "mla_attn_bwd", carry=(sent, True))
        riding = (sent, landed)

    def rope_bwd(dq_, dk_, tc_, t1, t2):
        dqr = _rope_t(dq_, _tile8(tc_), _tile8(t1), _tile8(t2))
        dkr = dk_[:, 0:HP]
        for hh in range(1, HEADS):
            dkr = dkr + dk_[:, hh * HP:(hh + 1) * HP]
        dkr = _rope_t(jnp.where(_rope_lanes(dkr.shape), dkr, 0.0), tc_, t1, t2)
        dkr = pltpu.roll(dkr, NOPE, 1)
        lane = lax.broadcasted_iota(jnp.int32, dkr.shape, 1)
        return dqr, jnp.where(lane < ROPE, dkr, 0.0)

    dq_raw, dkr = _rowwise(rope_bwd, "mla_rope_bwd", [_whole(dq), _whole(dk), _whole(t_cos), _whole(t_s1), _whole(t_s2)], [],
                           [(HEADS * HP, BF16), (HP, BF16)], tm=512)

    def rms_bwd_epi(acc, *rest):
        _, vjp = jax.vjp(_rms, rest[-2].astype(F32), rest[-1])
        return vjp(acc + rest[0] if len(rest) == 3 else acc)

    row_and_sum = [(S5_W, BF16), (S5_W, "sum")]
    dcq, gs["q_norm_g"] = _mm(dq_raw, w_uq_p, "mla_uq_bwd_norm", "nt", epi=rms_bwd_epi, rextras=[(z, Z_CQ, S5_W)],
                              pextras=(gq,), outs=row_and_sum)
    gW["w_uq"] = _unpad_heads(_mm(cqn, dq_raw, "mla_uq_bwd_w", "tn", m=S5_W, n=HEADS * HP), QK)
    dckvn_k = _mm(dk, w_uk_p, "mla_uk_bwd_a", "nt")
    dckv, gs["kv_norm_g"] = _mm(dv, w_uv_p, "mla_uv_bwd_norm", "nt", epi=rms_bwd_epi, extras=(dckvn_k,),
                                rextras=[(z, Z_CKV, S5_W)], pextras=(gkv,), outs=row_and_sum)
    g_uk = _unpad_heads(_mm(ckvn, dk, "mla_uk_bwd_w", "tn", m=S5_W, n=HEADS * HP), NOPE).reshape(S5_W, HEADS, NOPE)
    g_uv = _unpad_heads(_mm(ckvn, dv, "mla_uv_bwd_w", "tn", m=S5_W, n=HEADS * HP), VD).reshape(S5_W, HEADS, VD)
    gW["w_ukv"] = jnp.concatenate([g_uk, g_uv], axis=2).reshape(S5_W, HEADS * (NOPE + VD))

    dy2 = jnp.concatenate([dy2a, dy2b], axis=1)
    dyg = _mm(dy2, W["w_glu"], "glu_bwd_a", "nt")
    gW["w_glu"] = _mm(yg, dy2, "glu_bwd_w", "tn", m=S5_W, n=2 * D_MODEL, tk=2048)

    du, gs["s5_ab"], g_bst, g_cst_t, gs["s5_d"] = _s5_bwd(dyg, yc, z, hs, b_st, c_st, d_skip, tab_rev, "s5_bwd")
    gs["s5_bb_re"] = jnp.swapaxes(_blockdiag_t(g_bst[:, :S5_N], S5_H, S5_P), 1, 2)
    gs["s5_bb_im"] = jnp.swapaxes(_blockdiag_t(g_bst[:, S5_N:], S5_H, S5_P), 1, 2)
    gs["s5_c_re"] = _blockdiag_t(g_cst_t[:, :S5_N], S5_H, S5_P)
    gs["s5_c_im"] = -_blockdiag_t(g_cst_t[:, S5_N:], S5_H, S5_P)

    dz = jnp.concatenate([dgs, dga, du, dcq, dckv, dkr], axis=1)
    g_in_re = _mm(h0, dz, "in_proj_bwd_w", "tn", m=D_MODEL, n=Z_W, tm=512, tn=Z_W)
    gW["w_in"] = jnp.concatenate([g_in_re[:, Z_U:Z_KR], g_in_re[:, Z_KR:Z_KR + ROPE], g_in_re[:, :Z_U]], axis=1)
    dx, gs["ln_in_g"], gs["ln_in_b"] = _mm(dz, w_in_re, "in_proj_bwd_ln_in", "nt", tm=512, tk=Z_W, epi=ln_bwd_epi,
                                           extras=(dr1, x), pextras=(g_in, b_in), outs=row_and_sums)
    return lossv[:, :1], dx, gW, gs, riding


_RAW_SMALL = (("loss", (1, 1)), ("ln_in_g", (1, D_MODEL)), ("ln_in_b", (1, D_MODEL)), ("ln1_g", (1, D_MODEL)),
              ("ln1_b", (1, D_MODEL)), ("ln2_g", (1, D_MODEL)), ("ln2_b", (1, D_MODEL)), ("ln3_g", (1, D_MODEL)),
              ("ln3_b", (1, D_MODEL)), ("q_norm_g", (1, S5_W)), ("kv_norm_g", (1, S5_W)), ("s5_d", (1, S5_W)),
              ("s5_ab", (1, 2 * S5_N)), ("s5_bb_re", (S5_G, S5_P, S5_H)), ("s5_bb_im", (S5_G, S5_P, S5_H)),
              ("s5_c_re", (S5_G, S5_H, S5_P)), ("s5_c_im", (S5_G, S5_H, S5_P)))


def kernel(x, mem, positions, ln_in_g, ln_in_b, w_in, s5_lam_re, s5_lam_im, s5_log_dt, s5_b_re, s5_b_im, s5_c_re, s5_c_im, s5_d, w_glu, q_norm_g, w_uq, kv_norm_g, w_ukv, w_oa, w_o, ln1_g, ln1_b, w_xq, w_xk, w_xv, w_xo, ln2_g, ln2_b, w_up, w_down, ln3_g, ln3_b, loss_target, m_ln_in_g, m_ln_in_b, m_w_in, m_s5_lam_re, m_s5_lam_im, m_s5_log_dt, m_s5_b_re, m_s5_b_im, m_s5_c_re, m_s5_c_im, m_s5_d, m_w_glu, m_q_norm_g, m_w_uq, m_kv_norm_g, m_w_ukv, m_w_oa, m_w_o, m_ln1_g, m_ln1_b, m_w_xq, m_w_xk, m_w_xv, m_w_xo, m_ln2_g, m_ln2_b, m_w_up, m_w_down, m_ln3_g, m_ln3_b, v_ln_in_g, v_ln_in_b, v_w_in, v_s5_lam_re, v_s5_lam_im, v_s5_log_dt, v_s5_b_re, v_s5_b_im, v_s5_c_re, v_s5_c_im, v_s5_d, v_w_glu, v_q_norm_g, v_w_uq, v_kv_norm_g, v_w_ukv, v_w_oa, v_w_o, v_ln1_g, v_ln1_b, v_w_xq, v_w_xk, v_w_xv, v_w_xo, v_ln2_g, v_ln2_b, v_w_up, v_w_down, v_ln3_g, v_ln3_b):
    a = dict(locals())
    wts = {n: a[n] for n in WEIGHTS}
    ms = {n: a["m_" + n] for n in WEIGHTS}
    vs = {n: a["v_" + n] for n in WEIGHTS}
    shard2d = {n: wts[n].reshape(wts[n].shape[-2], wts[n].shape[-1]) for n in BIG}
    nrow = {n: shard2d[n].size // 1024 for n in BIG}
    mine = 2 * lax.axis_index("x") + lax.axis_index("y")
    my_c = lax.axis_index("c")
    cidx = my_c.astype(jnp.int32).reshape(1)

    def group_rows(group):
        rows = sum(nrow[n] for n in group)
        return rows, (-rows) % PACK_ROWS

    def pack_shards(group):
        rows, pad = group_rows(group)
        parts = [shard2d[n].astype(BF16).reshape(nrow[n], 1024) for n in group] + [jnp.zeros((pad, 1024), BF16)]
        return jnp.concatenate(parts, axis=0).reshape(2, (rows + pad) // 2, 1024)

    def unpack_full(group, gathered):
        out, off = {}, 0
        for n in group:
            r, cc = shard2d[n].shape
            piece = lax.optimization_barrier(gathered[:, off:off + nrow[n]])
            out[n] = _full_from_shards(n, piece.reshape(4, r, cc))
            off += nrow[n]
        return out

    def pack_grads(group, gw):
        rows, pad = group_rows(group)
        parts = [_shards_from_full(n, gw[n].astype(BF16)).reshape(4, nrow[n], 1024) for n in group]
        return jnp.concatenate(parts + [jnp.zeros((4, pad, 1024), BF16)], axis=1).reshape(4, 2, (rows + pad) // 2, 1024)

    def by_core(own, other):
        return jnp.where(my_c == 0, jnp.stack([own, other]), jnp.stack([other, own]))

    packed_e, packed_l = pack_shards(BIG_EARLY), pack_shards(BIG_LATE)
    gathered = lax.dynamic_update_slice(_gather_weights(packed_e), packed_e[None], (mine, 0, 0, 0))
    W = unpack_full(BIG_EARLY, gathered.reshape(4, -1, 1024))

    def finish_gather(landed):
        half = packed_l.shape[1]
        own_half = lax.dynamic_index_in_dim(packed_l, my_c, axis=0, keepdims=True)
        landed = lax.dynamic_update_slice(landed, own_half, (mine, 0, 0))
        other = _pair_share(landed.reshape(4 * half, 1024), "gather_late_share").reshape(4, half, 1024)
        out, off = {}, 0
        for n in BIG_LATE:
            r, cc = shard2d[n].shape
            h, lo = divmod(off, half)
            assert lo + nrow[n] <= half, n
            piece = jnp.where(my_c == h, landed[:, lo:lo + nrow[n]], other[:, lo:lo + nrow[n]])
            out[n] = _full_from_shards(n, lax.optimization_barrier(piece).reshape(4, r, cc))
            off += nrow[n]
        return out

    def pair_reduce(group, gw, tag):
        gflat = pack_grads(group, gw)
        return _pair_add(gflat, _pair_exchange(gflat, "grad_pair_exchange_" + tag), cidx, "grad_pair_add_" + tag)

    def finish_reduce(group, sent, landed, tag):
        own = lax.dynamic_slice(sent, (mine, 0, 0), (1,) + sent.shape[1:])
        red = _chip_sum(lax.dynamic_update_slice(landed, own, (mine, 0, 0)), "grad_chip_sum_" + tag)
        gsh = by_core(red, _pair_share(red, "grad_pair_share_" + tag)).reshape(-1, 1024)
        out, off = {}, 0
        for n in group:
            out[n] = lax.optimization_barrier(gsh[off:off + nrow[n]]).reshape(shard2d[n].shape)
            off += nrow[n]
        return out

    sp = {n: wts[n] for n in SMALL}
    sp_local = {n: (sp[n][0] if sp[n].ndim > 1 else sp[n]) for n in SMALL}
    overlap = (packed_l, finish_gather, lambda gw: pair_reduce(BIG_LATE, gw, "late"))
    lossv, dx, gW, gs, riding = _local_step(x[0], mem[0], positions[0], loss_target[0], W, sp_local, overlap)

    g_out = finish_reduce(BIG_LATE, riding[0], riding[1], "late")
    sent_e = pair_reduce(BIG_EARLY, gW, "early")
    g_out.update(finish_reduce(BIG_EARLY, sent_e, _chip_exchange(sent_e, "grad_chip_exchange_early"), "early"))

    gs["loss"] = lossv
    raw = _pack_flat([gs[n].reshape(s) for n, s in _RAW_SMALL], 8 * 1024, F32)
    raw = _allreduce_small(raw.reshape(-1, 1024)).reshape(-1)
    rs = dict(zip([n for n, _ in _RAW_SMALL], _unpack_flat(raw, [s for _, s in _RAW_SMALL])))
    loss = rs["loss"].reshape(())
    _, disc_vjp = jax.vjp(_s5_discretize, sp_local["s5_lam_re"], sp_local["s5_lam_im"], sp_local["s5_log_dt"],
                          sp_local["s5_b_re"], sp_local["s5_b_im"])
    d_ab = rs["s5_ab"].reshape(2, S5_G, S5_P)
    g_lre, g_lim, g_ldt, g_bre, g_bim = disc_vjp((d_ab[0], d_ab[1], rs["s5_bb_re"], rs["s5_bb_im"]))
    small_g = {"s5_lam_re": g_lre, "s5_lam_im": g_lim, "s5_log_dt": g_ldt, "s5_b_re": g_bre, "s5_b_im": g_bim,
               "s5_c_re": rs["s5_c_re"], "s5_c_im": rs["s5_c_im"]}
    for n in ("ln_in_g", "ln_in_b", "ln1_g", "ln1_b", "ln2_g", "ln2_b", "ln3_g", "ln3_b", "q_norm_g", "kv_norm_g", "s5_d"):
        small_g[n] = rs[n]

    grads, deltas, new_m, new_v = {}, {}, {}, {}
    for n in BIG:
        d_, m_, v_ = _adamw(shard2d[n], g_out[n], ms[n].reshape(shard2d[n].shape), vs[n].reshape(shard2d[n].shape), "adamw_" + n)
        grads[n] = g_out[n].reshape(wts[n].shape)
        deltas[n], new_m[n], new_v[n] = (t.reshape(wts[n].shape) for t in (d_, m_, v_))
    as2d = lambda t: t.reshape(-1, t.shape[-1])
    sd, sm, sv = _adamw_many([as2d(wts[n]) for n in SMALL], [as2d(small_g[n].reshape(wts[n].shape)) for n in SMALL],
                             [as2d(ms[n]) for n in SMALL], [as2d(vs[n]) for n in SMALL], "adamw_small")
    for n, d_, m_, v_ in zip(SMALL, sd, sm, sv):
        grads[n] = small_g[n].reshape(wts[n].shape)
        deltas[n], new_m[n], new_v[n] = (t.reshape(wts[n].shape) for t in (d_, m_, v_))

    return (loss, dx[None], *[grads[n] for n in WEIGHTS], *[deltas[n] for n in WEIGHTS],
            *[new_m[n] for n in WEIGHTS], *[new_v[n] for n in WEIGHTS])
```

```python
import functools
import math

import jax
import jax.numpy as jnp
from jax import lax
from jax.experimental import pallas as pl
from jax.experimental.pallas import tpu as pltpu

F32 = jnp.float32
BF16 = jnp.bfloat16
MESH = pl.DeviceIdType.MESH

D_MODEL = 1024
S5_W = 256
S5_G = 16
S5_H = 16
S5_P = 64
S5_N = S5_G * S5_P
S5_MAX_RE = -1e-4
HEADS = 8
NOPE = 64
ROPE = 32
QK = NOPE + ROPE
VD = 64
HP = 128
XH = 4
XD = 256
LN_EPS = 1e-5
RMS_EPS = 1e-6
NEG_INF = -1e30
DN_ALPHA = 2.0 ** 0.25
ROPE_THETA = 10000.0
ADAM_LR, ADAM_B1, ADAM_B2, ADAM_EPS, ADAM_WD, ADAM_STEP = 0.001, 0.9, 0.999, 1e-08, 0.01, 10

Z_GS, Z_GA, Z_U, Z_CQ, Z_CKV, Z_KR, Z_W = 0, 1024, 2048, 2304, 2560, 2816, 2944

BIG_EARLY = ("w_in", "w_glu", "w_uq", "w_ukv", "w_oa", "w_o")
BIG_LATE = ("w_up", "w_xq", "w_xk", "w_xv", "w_xo", "w_down")
BIG = BIG_EARLY + BIG_LATE
COL_SHARDED = ("w_in", "w_glu", "w_uq", "w_ukv", "w_oa", "w_up")
SMALL = ("ln_in_g", "ln_in_b", "s5_lam_re", "s5_lam_im", "s5_log_dt", "s5_b_re", "s5_b_im", "s5_c_re", "s5_c_im",
         "s5_d", "q_norm_g", "kv_norm_g", "ln1_g", "ln1_b", "ln2_g", "ln2_b", "ln3_g", "ln3_b")
WEIGHTS = ("ln_in_g", "ln_in_b", "w_in", "s5_lam_re", "s5_lam_im", "s5_log_dt", "s5_b_re", "s5_b_im", "s5_c_re",
           "s5_c_im", "s5_d", "w_glu", "q_norm_g", "w_uq", "kv_norm_g", "w_ukv", "w_oa", "w_o", "ln1_g", "ln1_b",
           "w_xq", "w_xk", "w_xv", "w_xo", "ln2_g", "ln2_b", "w_up", "w_down", "ln3_g", "ln3_b")
PACK_ROWS = 2 * 16


def _row_tile(n, cap=512):
    best = n
    for t in range(8, min(n, cap) + 1, 8):
        if n % t == 0:
            best = t
    return best


def _pick(n, cap):
    best = None
    for t in range(128, min(n, cap) + 1, 128):
        if n % t == 0:
            best = t
    return best if best is not None and (best >= 512 or best == n) else n


def _mm(a, b, name, mode="nn", *, a_off=0, b_off=0, m=None, n=None, act=None, epi=None, extras=(), rextras=(),
        pextras=(), outs=None, out_dtype=None, tm=1024, tn=1024, tk=None):
    out_dtype = out_dtype or (BF16 if mode == "tn" else F32)
    if mode == "nn":
        M, (K, N) = a.shape[0], b.shape
    elif mode == "nt":
        M, (N, K) = a.shape[0], b.shape
    else:
        K, M, N = a.shape[0], m, n
    if mode == "tn":
        tm, tn, tk = _pick(M, tm), _pick(N, tn), min(tk or 1024, K)
    else:
        tm, tn, tk = min(tm, M), _pick(N, tn), _pick(K, tk or 1024)
    assert M % tm == 0 and N % tn == 0 and K % tk == 0, (name, M, N, K, tm, tn, tk)
    nk = K // tk
    if mode == "tn":
        assert a_off % tm == 0 and b_off % tn == 0
        ao, bo = a_off // tm, b_off // tn
        a_spec = pl.BlockSpec((tk, tm), lambda i, j, k: (k, i + ao))
        b_spec = pl.BlockSpec((tk, tn), lambda i, j, k: (k, j + bo))
        dims = (((0,), (0,)), ((), ()))
    else:
        assert a_off % tk == 0
        ao = a_off // tk
        a_spec = pl.BlockSpec((tm, tk), lambda i, j, k: (i, k + ao))
        if mode == "nn":
            b_spec = pl.BlockSpec((tk, tn), lambda i, j, k: (k, j))
            dims = (((1,), (0,)), ((), ()))
        else:
            b_spec = pl.BlockSpec((tn, tk), lambda i, j, k: (j, k))
            dims = (((1,), (1,)), ((), ()))
    e_spec = pl.BlockSpec((tm, tn), lambda i, j, k: (i, j))
    r_specs = [pl.BlockSpec((tm, w), functools.partial(lambda i, j, k, o: (i, o), o=off // w)) for _, off, w in rextras]
    p_specs = [pl.BlockSpec((1, tn), lambda i, j, k: (0, j)) if p.shape[1] == N else pl.BlockSpec(p.shape, lambda i, j, k: (0, 0))
               for p in pextras]
    n_extra = len(extras) + len(rextras) + len(pextras)
    if outs is None:
        o_specs, o_shapes = [e_spec], [jax.ShapeDtypeStruct((M, N), out_dtype)]
    else:
        assert all((w == tn and dt != "sum") or N == tn for w, dt in outs), name
        o_specs = [e_spec if w == tn and dt != "sum" else pl.BlockSpec((8 if dt == "sum" else tm, w), lambda i, j, k: (i, 0))
                   for w, dt in outs]
        o_shapes = [jax.ShapeDtypeStruct((M // tm * 8, w), F32) if dt == "sum" else
                    jax.ShapeDtypeStruct((M, N if w == tn else w), dt) for w, dt in outs]
    n_out = len(o_specs)
    direct = outs is None and epi is None and out_dtype == F32

    def body(*refs):
        a_ref, b_ref = refs[0], refs[1]
        e_refs = refs[2:2 + n_extra]
        o_refs = refs[2 + n_extra:2 + n_extra + n_out]
        av = a_ref[...]
        if act is not None:
            av = act(av.astype(F32))
        p = lax.dot_general(av.astype(BF16), b_ref[...].astype(BF16), dims, preferred_element_type=F32)

        def finish(r):
            if epi is not None:
                r = epi(r, *[e[...] for e in e_refs])
            for o_ref, v in zip(o_refs, r if isinstance(r, tuple) else (r,)):
                o_ref[...] = jnp.broadcast_to(v, o_ref.shape).astype(o_ref.dtype)

        if nk == 1:
            finish(p)
        else:
            acc = o_refs[0] if direct else refs[2 + n_extra + n_out]
            k = pl.program_id(2)

            @pl.when(k == 0)
            def _():
                acc[...] = p

            @pl.when(k > 0)
            def _():
                acc[...] += p

            if not direct:
                @pl.when(k == nk - 1)
                def _():
                    finish(acc[...])

    res = pl.pallas_call(
        body, name=name, grid=(M // tm, N // tn, nk),
        in_specs=[a_spec, b_spec] + [e_spec] * len(extras) + r_specs + p_specs, out_specs=o_specs,
        out_shape=o_shapes,
        scratch_shapes=[pltpu.VMEM((tm, tn), F32)] if nk > 1 and not direct else [],
        compiler_params=pltpu.CompilerParams(dimension_semantics=("parallel", "parallel", "arbitrary")),
    )(a, b, *extras, *[r[0] for r in rextras], *pextras)
    if outs is None:
        return res[0]
    return [r.reshape(M // tm, 8, -1)[:, 0].sum(axis=0, keepdims=True) if dt == "sum" else r for r, (_, dt) in zip(res, outs)]


def _rowwise(fn, name, rows, pars, outs, accs=(), tm=256, upcast=True):
    T = rows[0][0].shape[0]
    tm = min(tm, T)
    assert T % tm == 0
    n_in, n_out = len(rows) + len(pars), len(outs)
    in_specs = []
    for arr, off, w in rows:
        assert off % w == 0 and arr.shape[0] == T, name
        in_specs.append(pl.BlockSpec((tm, w), functools.partial(lambda i, o: (i, o), o=off // w)))
    for p in pars:
        in_specs.append(pl.BlockSpec(p.shape, lambda i: (0, 0)))
    out_specs = [pl.BlockSpec((tm, w), lambda i: (i, 0)) for w, _ in outs]
    out_specs += [pl.BlockSpec(s, lambda i: (0, 0)) for s in accs]
    out_shape = [jax.ShapeDtypeStruct((T, w), dt) for w, dt in outs] + [jax.ShapeDtypeStruct(s, F32) for s in accs]

    def body(*refs):
        res = fn(*[r[...].astype(F32) if upcast else r[...] for r in refs[:n_in]])
        o_refs = refs[n_in:]
        for r, v in zip(o_refs[:n_out], res[:n_out]):
            r[...] = v.astype(r.dtype)
        if accs:
            i = pl.program_id(0)

            @pl.when(i == 0)
            def _():
                for r in o_refs[n_out:]:
                    r[...] = jnp.zeros_like(r)

            for r, v in zip(o_refs[n_out:], res[n_out:]):
                r[...] += v

    res = pl.pallas_call(
        body, name=name, grid=(T // tm,), in_specs=in_specs, out_specs=out_specs, out_shape=out_shape,
        compiler_params=pltpu.CompilerParams(dimension_semantics=("arbitrary",)),
    )(*[r[0] for r in rows], *pars)
    return res


def _whole(arr):
    return (arr, 0, arr.shape[1])


def _ln(x, g, b):
    mu = jnp.mean(x, axis=-1, keepdims=True)
    xc = x - mu
    var = jnp.mean(xc * xc, axis=-1, keepdims=True)
    return xc * lax.rsqrt(var + LN_EPS) * g + b


def _ln_res(h, r, g, b):
    return _ln(DN_ALPHA * h + r, g, b)


def _rms(x, g):
    return x * lax.rsqrt(jnp.mean(x * x, axis=-1, keepdims=True) + RMS_EPS) * g


def _gelu_skip(yc, u, d):
    y = yc + d * u
    return 0.5 * y * (1.0 + lax.erf(y * (1.0 / math.sqrt(2.0))))


def _gate_mix(gs, ga, y2a, y2b, aout):
    return jax.nn.sigmoid(gs) * (y2a * jax.nn.sigmoid(y2b)) + jax.nn.sigmoid(ga) * aout


def _relu2(a):
    r = jnp.maximum(a, 0.0)
    return r * r


def _tile8(t):
    return jnp.concatenate([t] * HEADS, axis=1)


def _rope(x, cos, s1, s2):
    w = x.shape[1]
    return x * cos + pltpu.roll(x, ROPE // 2, 1) * s1 + pltpu.roll(x, w - ROPE // 2, 1) * s2


def _rope_t(dy, cos, s1, s2):
    w = dy.shape[1]
    return dy * cos + pltpu.roll(dy * s1, w - ROPE // 2, 1) + pltpu.roll(dy * s2, ROPE // 2, 1)


def _rope_lanes(shape):
    lane = lax.broadcasted_iota(jnp.int32, shape, 1) % HP
    return (lane >= NOPE) & (lane < QK)


ATT_HB_FWD, ATT_HB_BWD = 8, 8
ATT_BQ_FWD = 1024
ATT_SCALE = QK ** -0.5
LOG2E = 1.4426950408889634
LN2 = 0.6931471805599453
_NT = (((1,), (1,)), ((), ()))
_TN = (((0,), (0,)), ((), ()))


def _carry_specs(carry):
    if carry is None:
        return [], [], [], []
    arr = carry[0]
    return ([_HBM], [_HBM], [jax.ShapeDtypeStruct((4,) + arr.shape[1:], arr.dtype)],
            [pltpu.SemaphoreType.DMA((3,)), pltpu.SemaphoreType.DMA((3,))])


def _carry_start(exchange, when):
    @pl.when(when)
    def _():
        for cp in exchange(False):
            cp.start()


def _carry_wait(exchange, when):
    @pl.when(when)
    def _():
        for cp in exchange(True):
            cp.wait_recv()
        for cp in exchange(False):
            cp.wait_send()


def _tri_tables(n, by_row):
    if by_row:
        pairs = [(i, j) for i in range(n) for j in range(i + 1)]
    else:
        pairs = [(i, j) for j in range(n) for i in range(j, n)]
    return jnp.array([p[0] for p in pairs], jnp.int32), jnp.array([p[1] for p in pairs], jnp.int32)


def _flash_fwd(q, k, v, name, bq=512, carry=None):
    T = q.shape[0]
    bq = min(bq, T)
    nq = T // bq
    hb = ATT_HB_FWD
    n_carry = 0 if carry is None else 1

    i_tab, j_tab = _tri_tables(nq, by_row=True)

    def body(it_ref, jt_ref, q_ref, k_ref, v_ref, *rest):
        o_ref, lse_ref = rest[n_carry:n_carry + 2]
        m_sc, l_sc, acc_sc = rest[2 * n_carry + 2:2 * n_carry + 5]
        t = pl.program_id(1)
        i, j = it_ref[t], jt_ref[t]
        if carry is not None:
            exchange = functools.partial(_carried_exchange, rest[0], rest[3], rest[-2], rest[-1], carry[1])
            _carry_start(exchange, (pl.program_id(0) == 0) & (t == 0))

        @pl.when(j == 0)
        def _():
            m_sc[...] = jnp.full_like(m_sc, NEG_INF)
            l_sc[...] = jnp.zeros_like(l_sc)
            acc_sc[...] = jnp.zeros_like(acc_sc)

        def step(masked):
            for hh in range(hb):
                sl = slice(hh * HP, (hh + 1) * HP)
                s = lax.dot_general(q_ref[:, sl], k_ref[:, sl], _NT, preferred_element_type=F32)
                if masked:
                    row = lax.broadcasted_iota(jnp.int32, (bq, bq), 0)
                    col = lax.broadcasted_iota(jnp.int32, (bq, bq), 1)
                    s = jnp.where(col <= row, s, NEG_INF)
                m_prev = m_sc[hh]
                m_new = jnp.maximum(m_prev, jnp.max(s, axis=1, keepdims=True))
                alpha = jnp.exp2(m_prev - m_new)
                p = jnp.exp2(s - jnp.concatenate([m_new] * (bq // HP), axis=1))
                l_sc[hh] = alpha * l_sc[hh] + jnp.sum(p, axis=1, keepdims=True)
                acc_sc[hh] = alpha * acc_sc[hh] + jnp.dot(p.astype(BF16), v_ref[:, sl], preferred_element_type=F32)
                m_sc[hh] = m_new

        @pl.when(j < i)
        def _():
            step(False)

        @pl.when(j == i)
        def _():
            step(True)
            for hh in range(hb):
                o_ref[:, hh * HP:(hh + 1) * HP] = acc_sc[hh] / l_sc[hh]
                lse_ref[hh] = jnp.transpose((m_sc[hh] + jnp.log2(l_sc[hh])) * LN2)[0:1, :]

        if carry is not None:
            _carry_wait(exchange, (pl.program_id(0) == HEADS // hb - 1) & (t == i_tab.shape[0] - 1))

    qs = pl.BlockSpec((bq, hb * HP), lambda h, t, it, jt: (it[t], h))
    ks = pl.BlockSpec((bq, hb * HP), lambda h, t, it, jt: (jt[t], h))
    c_in, c_out, c_shape, c_sems = _carry_specs(carry)
    gs = pltpu.PrefetchScalarGridSpec(
        num_scalar_prefetch=2, grid=(HEADS // hb, i_tab.shape[0]), in_specs=[qs, ks, ks] + c_in,
        out_specs=[qs, pl.BlockSpec((hb, 1, bq), lambda h, t, it, jt: (h, 0, it[t]))] + c_out,
        scratch_shapes=[pltpu.VMEM((hb, bq, HP), F32)] * 3 + c_sems)
    return pl.pallas_call(
        body, name=name, grid_spec=gs,
        out_shape=[jax.ShapeDtypeStruct((T, HEADS * HP), F32), jax.ShapeDtypeStruct((HEADS, 1, T), F32)] + c_shape,
        compiler_params=pltpu.CompilerParams(dimension_semantics=("arbitrary", "arbitrary")),
    )(i_tab, j_tab, q, k, v, *([] if carry is None else [carry[0]]))


def _flash_bwd(q, k, v, do, lse_row, delta_row, name, bq=512, carry=None):
    T = q.shape[0]
    bq = min(bq, T)
    nq = T // bq
    hb = ATT_HB_BWD
    n_carry = 0 if carry is None else 1

    i_tab, j_tab = _tri_tables(nq, by_row=False)
    n_blk = i_tab.shape[0]

    def body(it_ref, jt_ref, q_ref, k_ref, v_ref, do_ref, lse_ref, dl_ref, *rest):
        dq_ref, dk_ref, dv_ref = rest[n_carry:n_carry + 3]
        t = pl.program_id(1)
        i, j = it_ref[t], jt_ref[t]
        if carry is not None:
            exchange = functools.partial(_carried_exchange, rest[0], rest[4], rest[-2], rest[-1], carry[1])
            _carry_start(exchange, (pl.program_id(0) == 0) & (t == 0))

        @pl.when(t == 0)
        def _():
            dq_ref[...] = jnp.zeros_like(dq_ref)

        @pl.when(i == j)
        def _():
            dk_ref[...] = jnp.zeros_like(dk_ref)
            dv_ref[...] = jnp.zeros_like(dv_ref)

        def step(masked):
            rows = pl.ds(pl.multiple_of(i * bq, bq), bq)
            for hh in range(hb):
                sl = slice(hh * HP, (hh + 1) * HP)
                qh, kh, doh = q_ref[:, sl], k_ref[:, sl], do_ref[:, sl]
                st = lax.dot_general(kh, qh, _NT, preferred_element_type=F32)
                pt = jnp.exp2(st - lse_ref[hh] * LOG2E)
                if masked:
                    krow = lax.broadcasted_iota(jnp.int32, (bq, bq), 0)
                    qcol = lax.broadcasted_iota(jnp.int32, (bq, bq), 1)
                    pt = jnp.where(krow <= qcol, pt, 0.0)
                dv_ref[:, sl] += jnp.dot(pt.astype(BF16), doh, preferred_element_type=F32)
                dpt = lax.dot_general(v_ref[:, sl], doh, _NT, preferred_element_type=F32)
                dst = (pt * (dpt - dl_ref[hh])).astype(BF16)
                dk_ref[:, sl] += jnp.dot(dst, qh, preferred_element_type=F32)
                dq_ref[rows, sl] += lax.dot_general(dst, kh, _TN, preferred_element_type=F32)

        @pl.when(i > j)
        def _():
            step(False)

        @pl.when(i == j)
        def _():
            step(True)

        @pl.when(i == nq - 1)
        def _():
            dk_ref[...] *= LN2

        @pl.when(t == n_blk - 1)
        def _():
            dq_ref[...] *= ATT_SCALE

        if carry is not None:
            _carry_wait(exchange, (pl.program_id(0) == HEADS // hb - 1) & (t == n_blk - 1))

    qs = pl.BlockSpec((bq, hb * HP), lambda h, t, it, jt: (it[t], h))
    ks = pl.BlockSpec((bq, hb * HP), lambda h, t, it, jt: (jt[t], h))
    rs = pl.BlockSpec((hb, 1, bq), lambda h, t, it, jt: (h, 0, it[t]))
    full = pl.BlockSpec((T, hb * HP), lambda h, t, it, jt: (0, h), pipeline_mode=pl.Buffered(1))
    c_in, c_out, c_shape, c_sems = _carry_specs(carry)
    gs = pltpu.PrefetchScalarGridSpec(num_scalar_prefetch=2, grid=(HEADS // hb, n_blk),
                                      in_specs=[qs, ks, ks, qs, rs, rs] + c_in, out_specs=[full, ks, ks] + c_out,
                                      scratch_shapes=c_sems)
    return pl.pallas_call(
        body, name=name, grid_spec=gs, out_shape=[jax.ShapeDtypeStruct((T, HEADS * HP), F32)] * 3 + c_shape,
        compiler_params=pltpu.CompilerParams(dimension_semantics=("arbitrary", "arbitrary")),
    )(i_tab, j_tab, q, k, v, do, lse_row, delta_row, *([] if carry is None else [carry[0]]))


def _xattn_heads(q, k, v):
    scale = XD ** -0.5
    ps = []
    for h in range(XH):
        sl = slice(h * XD, (h + 1) * XD)
        s = lax.dot_general(q[:, sl].astype(BF16), k[:, sl].astype(BF16), (((1,), (1,)), ((), ())),
                            preferred_element_type=F32) * scale
        e = jnp.exp(s - jnp.max(s, axis=1, keepdims=True))
        ps.append(e / jnp.sum(e, axis=1, keepdims=True))
    return ps


def _xattn_fwd_fn(q, k, v):
    ps = _xattn_heads(q, k, v)
    o = [jnp.dot(p.astype(BF16), v[:, h * XD:(h + 1) * XD].astype(BF16), preferred_element_type=F32)
         for h, p in enumerate(ps)]
    return (jnp.concatenate(o, axis=1),)


def _xattn_bwd_fn(q, do, k, v):
    scale = XD ** -0.5
    ps = _xattn_heads(q, k, v)
    dqs, dks, dvs = [], [], []
    tdims = (((0,), (0,)), ((), ()))
    for h, p in enumerate(ps):
        sl = slice(h * XD, (h + 1) * XD)
        doh = do[:, sl].astype(BF16)
        dvs.append(lax.dot_general(p.astype(BF16), doh, tdims, preferred_element_type=F32))
        dp = lax.dot_general(doh, v[:, sl].astype(BF16), (((1,), (1,)), ((), ())), preferred_element_type=F32)
        ds = (p * (dp - jnp.sum(dp * p, axis=1, keepdims=True)) * scale).astype(BF16)
        dqs.append(jnp.dot(ds, k[:, sl].astype(BF16), preferred_element_type=F32))
        dks.append(lax.dot_general(ds, q[:, sl].astype(BF16), tdims, preferred_element_type=F32))
    return jnp.concatenate(dqs, axis=1), jnp.concatenate(dks, axis=1), jnp.concatenate(dvs, axis=1)


S5_SUB = 8


def _cmul_add(xr, xi, ar, ai, sr, si):
    return xr + ar * sr - ai * si, xi + ar * si + ai * sr


def _s5_tables(ar, ai, reverse):
    pows = [(ar, ai)]
    for _ in range(S5_SUB - 1):
        pr, pi = pows[-1]
        pows.append((pr * ar - pi * ai, pr * ai + pi * ar))
    cr = jnp.concatenate([p[0] for p in pows], axis=0)
    ci = jnp.concatenate([p[1] for p in pows], axis=0)
    if reverse:
        cr, ci = cr[::-1], ci[::-1]
    t = jnp.arange(S5_SUB)[:, None]
    blocks = [jnp.concatenate([cr, ci], axis=1)]
    for s in (1, 2, 4):
        keep = (t < S5_SUB - s) if reverse else (t >= s)
        sr, si = pows[s - 1]
        blocks.append(jnp.concatenate([jnp.where(keep, sr, 0.0), jnp.where(keep, si, 0.0)], axis=1))
    return jnp.concatenate(blocks, axis=0)


def _sub_scan(xr, xi, tab_ref, reverse):
    for k, s in enumerate((1, 2, 4)):
        blk = slice(S5_SUB * (k + 1), S5_SUB * (k + 2))
        sh = S5_SUB - s if reverse else s
        xr, xi = _cmul_add(xr, xi, tab_ref[blk, :S5_N], tab_ref[blk, S5_N:], pltpu.roll(xr, sh, 0), pltpu.roll(xi, sh, 0))
    return xr, xi


def _s5_fwd(z, b_st, c_st, d_skip, tab, name, tc=1024):
    T = z.shape[0]
    tc = min(tc, T)
    nsub = tc // S5_SUB

    def body(u_ref, bst_ref, cst_ref, d_ref, tab_ref, h_ref, yc_ref, yg_ref, carry, x_sc):
        @pl.when(pl.program_id(0) == 0)
        def _():
            carry[...] = jnp.zeros_like(carry)

        u = u_ref[...].astype(F32)
        x_sc[...] = jnp.dot(u.astype(BF16), bst_ref[...], preferred_element_type=F32)

        def sub(b, c):
            r = pl.ds(pl.multiple_of(b * S5_SUB, S5_SUB), S5_SUB)
            xr, xi = _sub_scan(x_sc[r, :S5_N], x_sc[r, S5_N:], tab_ref, False)
            hr, hi = _cmul_add(xr, xi, tab_ref[0:S5_SUB, :S5_N], tab_ref[0:S5_SUB, S5_N:], c[0], c[1])
            x_sc[r, :S5_N] = hr
            x_sc[r, S5_N:] = hi
            return hr[S5_SUB - 1:S5_SUB], hi[S5_SUB - 1:S5_SUB]

        cr, ci = lax.fori_loop(0, nsub, sub, (carry[0:1, :S5_N], carry[0:1, S5_N:]), unroll=2)
        carry[0:1, :S5_N] = cr
        carry[0:1, S5_N:] = ci
        h = x_sc[...]
        h_ref[...] = h
        yc = jnp.dot(h.astype(BF16), cst_ref[...], preferred_element_type=F32)
        yc_ref[...] = yc
        yg_ref[...] = _gelu_skip(yc, u, d_ref[...]).astype(yg_ref.dtype)

    whole = lambda a: pl.BlockSpec(a.shape, lambda i: (0, 0))
    row = lambda w: pl.BlockSpec((tc, w), lambda i: (i, 0))
    return pl.pallas_call(
        body, name=name, grid=(T // tc,),
        in_specs=[pl.BlockSpec((tc, S5_W), lambda i: (i, Z_U // S5_W)), whole(b_st), whole(c_st), whole(d_skip), whole(tab)],
        out_specs=[row(2 * S5_N), row(S5_W), row(S5_W)],
        out_shape=[jax.ShapeDtypeStruct((T, 2 * S5_N), F32), jax.ShapeDtypeStruct((T, S5_W), F32),
                   jax.ShapeDtypeStruct((T, S5_W), BF16)],
        scratch_shapes=[pltpu.VMEM((8, 2 * S5_N), F32), pltpu.VMEM((tc, 2 * S5_N), F32)],
        compiler_params=pltpu.CompilerParams(dimension_semantics=("arbitrary",)),
    )(z, b_st, c_st, d_skip, tab)


def _s5_bwd(dyg, yc, z, h, b_st, c_st, d_skip, tab, name, tc=1024):
    T = z.shape[0]
    tc = min(tc, T)
    nc, nsub = T // tc, tc // S5_SUB

    def body(dyg_ref, yc_ref, u_ref, h_ref, hp_ref, bst_ref, cst_ref, d_ref, tab_ref,
             du_ref, da_ref, db_ref, dct_ref, dd_ref, carry, x_sc):
        g = pl.program_id(0)

        @pl.when(g == 0)
        def _():
            carry[...] = jnp.zeros_like(carry)
            for r in (da_ref, db_ref, dct_ref, dd_ref):
                r[...] = jnp.zeros_like(r)

        u = u_ref[...].astype(F32)
        _, vjp = jax.vjp(_gelu_skip, yc_ref[...], u, d_ref[...])
        dyc, du_skip, dd = vjp(dyg_ref[...])
        dd_ref[...] += dd
        dyc16 = dyc.astype(BF16)
        dct_ref[...] += lax.dot_general(dyc16, h_ref[...].astype(BF16), _TN, preferred_element_type=F32)
        x_sc[...] = lax.dot_general(dyc16, cst_ref[...], _NT, preferred_element_type=F32)
        first = jnp.where(g == nc - 1, 0.0, 1.0)
        row0 = lax.broadcasted_iota(jnp.int32, (S5_SUB, S5_N), 0) == 0

        def sub(k, c):
            cr, ci, acc_r, acc_i = c
            b = nsub - 1 - k
            r = pl.ds(pl.multiple_of(b * S5_SUB, S5_SUB), S5_SUB)
            xr, xi = _sub_scan(x_sc[r, :S5_N], x_sc[r, S5_N:], tab_ref, True)
            lr, li = _cmul_add(xr, xi, tab_ref[0:S5_SUB, :S5_N], tab_ref[0:S5_SUB, S5_N:], cr, ci)
            x_sc[r, :S5_N] = lr
            x_sc[r, S5_N:] = li
            rp = pl.ds(pl.multiple_of(jnp.maximum(b - 1, 0) * S5_SUB, S5_SUB), S5_SUB)
            last_r = jnp.where(b == 0, hp_ref[S5_SUB - 1:S5_SUB, :S5_N] * first, h_ref[rp, :S5_N][S5_SUB - 1:S5_SUB])
            last_i = jnp.where(b == 0, hp_ref[S5_SUB - 1:S5_SUB, S5_N:] * first, h_ref[rp, S5_N:][S5_SUB - 1:S5_SUB])
            pr = jnp.where(row0, last_r, pltpu.roll(h_ref[r, :S5_N], 1, 0))
            pi = jnp.where(row0, last_i, pltpu.roll(h_ref[r, S5_N:], 1, 0))
            return lr[0:1], li[0:1], acc_r + (lr * pr + li * pi), acc_i + (li * pr - lr * pi)

        zero = jnp.zeros((S5_SUB, S5_N), F32)
        cr, ci, acc_r, acc_i = lax.fori_loop(0, nsub, sub, (carry[0:1, :S5_N], carry[0:1, S5_N:], zero, zero), unroll=2)
        carry[0:1, :S5_N] = cr
        carry[0:1, S5_N:] = ci
        da_ref[0:1, :S5_N] += jnp.sum(acc_r, axis=0, keepdims=True)
        da_ref[0:1, S5_N:] += jnp.sum(acc_i, axis=0, keepdims=True)
        lam16 = x_sc[...].astype(BF16)
        db_ref[...] += lax.dot_general(u.astype(BF16), lam16, _TN, preferred_element_type=F32)
        du = lax.dot_general(lam16, bst_ref[...], _NT, preferred_element_type=F32) + du_skip
        du_ref[...] = du.astype(du_ref.dtype)

    whole = lambda a: pl.BlockSpec(a.shape, lambda g: (0, 0))
    row = lambda w, off=0: pl.BlockSpec((tc, w), lambda g: (nc - 1 - g, off))
    prev = pl.BlockSpec((S5_SUB, 2 * S5_N), lambda g: (jnp.maximum((nc - 1 - g) * nsub - 1, 0), 0))
    acc = lambda s: pl.BlockSpec(s, lambda g: (0, 0))
    return pl.pallas_call(
        body, name=name, grid=(nc,),
        in_specs=[row(S5_W), row(S5_W), row(S5_W, Z_U // S5_W), row(2 * S5_N), prev, whole(b_st), whole(c_st),
                  whole(d_skip), whole(tab)],
        out_specs=[row(S5_W), acc((1, 2 * S5_N)), acc((S5_W, 2 * S5_N)), acc((S5_W, 2 * S5_N)), acc((1, S5_W))],
        out_shape=[jax.ShapeDtypeStruct((T, S5_W), BF16), jax.ShapeDtypeStruct((1, 2 * S5_N), F32),
                   jax.ShapeDtypeStruct((S5_W, 2 * S5_N), F32), jax.ShapeDtypeStruct((S5_W, 2 * S5_N), F32),
                   jax.ShapeDtypeStruct((1, S5_W), F32)],
        scratch_shapes=[pltpu.VMEM((8, 2 * S5_N), F32), pltpu.VMEM((tc, 2 * S5_N), F32)],
        compiler_params=pltpu.CompilerParams(dimension_semantics=("arbitrary",)),
    )(dyg, yc, z, h, h, b_st, c_st, d_skip, tab)


def _s5_discretize(lam_re, lam_im, log_dt, b_re, b_im):
    lr = jnp.minimum(lam_re, S5_MAX_RE)
    li = lam_im
    dt = jnp.exp(log_dt)[:, None]
    mag = jnp.exp(lr * dt)
    ang = li * dt
    ab_re = mag * jnp.cos(ang)
    ab_im = mag * jnp.sin(ang)
    den = lr * lr + li * li
    nr = ab_re - 1.0
    f_re = ((nr * lr + ab_im * li) / den)[..., None]
    f_im = ((ab_im * lr - nr * li) / den)[..., None]
    return ab_re, ab_im, f_re * b_re - f_im * b_im, f_re * b_im + f_im * b_re


def _adamw_fn(w, g, m, v):
    m = ADAM_B1 * m + (1.0 - ADAM_B1) * g
    v = ADAM_B2 * v + (1.0 - ADAM_B2) * (g * g)
    m_hat = m / (1.0 - ADAM_B1 ** ADAM_STEP)
    v_hat = v / (1.0 - ADAM_B2 ** ADAM_STEP)
    delta = -ADAM_LR * (m_hat / (jnp.sqrt(v_hat) + ADAM_EPS) + ADAM_WD * w)
    return delta, m, v


def _adamw(w, g, m, v, name):
    wd = w.shape[1]
    return _rowwise(_adamw_fn, name, [_whole(w), _whole(g), _whole(m), _whole(v)], [], [(wd, F32)] * 3,
                    tm=_row_tile(w.shape[0]))


def _adamw_many(ws, gs, ms, vs, name):
    n = len(ws)

    def body(*refs):
        for i in range(n):
            d_, m_, v_ = _adamw_fn(*[refs[kk * n + i][...] for kk in range(4)])
            refs[4 * n + i][...] = d_
            refs[5 * n + i][...] = m_
            refs[6 * n + i][...] = v_

    out = pl.pallas_call(body, name=name, out_shape=[jax.ShapeDtypeStruct(w.shape, F32) for w in ws] * 3)(*ws, *gs, *ms, *vs)
    return out[:n], out[n:2 * n], out[2 * n:]


def _me():
    return lax.axis_index("x"), lax.axis_index("y"), lax.axis_index("c")


_HBM = pl.BlockSpec(memory_space=pl.ANY)


def _gather_weights(shard):
    _, R, W = shard.shape

    def body(s_ref, out_ref, send_sems, recv_sems):
        x, y, c = _me()
        mine = 2 * x + y
        sib = (x, y, 1 - c)
        chips = [(1 - x, y), (x, 1 - y), (1 - x, 1 - y)]

        def rcopy(kk, src, chip_idx, half, to):
            return pltpu.make_async_remote_copy(src_ref=src, dst_ref=out_ref.at[chip_idx, half],
                                                send_sem=send_sems.at[kk], recv_sem=recv_sems.at[kk],
                                                device_id=to, device_id_type=MESH)

        first = [rcopy(j, s_ref.at[c], mine, c, (*chip, c)) for j, chip in enumerate(chips)]
        for cp in first:
            cp.start()
        passed = []
        for j, (px, py) in enumerate(chips):
            src_chip = 2 * px + py
            rcopy(j, s_ref.at[c], src_chip, c, (x, y, c)).wait_recv()
            fwd = rcopy(3 + j, out_ref.at[src_chip, c], src_chip, c, sib)
            fwd.start()
            passed.append(fwd)
        for j, (px, py) in enumerate(chips):
            rcopy(3 + j, s_ref.at[c], 2 * px + py, 1 - c, (x, y, c)).wait_recv()
        for cp in first + passed:
            cp.wait_send()

    return pl.pallas_call(
        body, name="gather_weights", in_specs=[_HBM], out_specs=_HBM,
        out_shape=jax.ShapeDtypeStruct((4, 2, R, W), shard.dtype),
        scratch_shapes=[pltpu.SemaphoreType.DMA((6,)), pltpu.SemaphoreType.DMA((6,))],
    )(shard)


def _pair_exchange(g, name):
    _, _, R, W = g.shape

    def body(g_ref, out_ref, send_sem, recv_sem):
        x, y, c = _me()
        cp = pltpu.make_async_remote_copy(src_ref=g_ref.at[:, 1 - c], dst_ref=out_ref, send_sem=send_sem,
                                          recv_sem=recv_sem, device_id=(x, y, 1 - c), device_id_type=MESH)
        cp.start()
        cp.wait()

    return pl.pallas_call(
        body, name=name, in_specs=[_HBM], out_specs=_HBM,
        out_shape=jax.ShapeDtypeStruct((4, R, W), g.dtype),
        scratch_shapes=[pltpu.SemaphoreType.DMA(()), pltpu.SemaphoreType.DMA(())],
    )(g)


def _pair_add(g, recv, cidx, name):
    _, _, R, W = g.shape
    tr = _row_tile(R)

    def body(c_ref, a_ref, b_ref, o_ref):
        o_ref[...] = (a_ref[...].astype(F32) + b_ref[...].astype(F32)).astype(o_ref.dtype)

    gs = pltpu.PrefetchScalarGridSpec(
        num_scalar_prefetch=1, grid=(4, R // tr),
        in_specs=[pl.BlockSpec((None, None, tr, W), lambda j, r, c: (j, c[0], r, 0)),
                  pl.BlockSpec((None, tr, W), lambda j, r, c: (j, r, 0))],
        out_specs=pl.BlockSpec((None, tr, W), lambda j, r, c: (j, r, 0)))
    return pl.pallas_call(body, name=name, grid_spec=gs,
                          out_shape=jax.ShapeDtypeStruct((4, R, W), BF16))(cidx, g, recv)


def _carried_exchange(src_ref, dst_ref, send_sems, recv_sems, spread, incoming):
    x, y, c = _me()
    mine = 2 * x + y
    copies = []
    for j, (px, py) in enumerate([(1 - x, y), (x, 1 - y), (1 - x, 1 - y)]):
        src = src_ref.at[2 * px + py] if spread else src_ref.at[c]
        slot = 2 * px + py if incoming else mine
        copies.append(pltpu.make_async_remote_copy(src_ref=src, dst_ref=dst_ref.at[slot], send_sem=send_sems.at[j],
                                                   recv_sem=recv_sems.at[j], device_id=(px, py, c), device_id_type=MESH))
    return copies


def _chip_exchange(p, name):
    _, R, W = p.shape

    def body(p_ref, out_ref, send_sems, recv_sems):
        outs = _carried_exchange(p_ref, out_ref, send_sems, recv_sems, True, False)
        for cp in outs:
            cp.start()
        for cp in _carried_exchange(p_ref, out_ref, send_sems, recv_sems, True, True):
            cp.wait_recv()
        for cp in outs:
            cp.wait_send()

    return pl.pallas_call(
        body, name=name, in_specs=[_HBM], out_specs=_HBM,
        out_shape=jax.ShapeDtypeStruct((4, R, W), p.dtype),
        scratch_shapes=[pltpu.SemaphoreType.DMA((3,)), pltpu.SemaphoreType.DMA((3,))],
    )(p)


def _chip_sum(q, name):
    _, R, W = q.shape
    tr = _row_tile(R)

    def body(q_ref, o_ref):
        f = lambda t: q_ref[t].astype(F32)
        o_ref[...] = ((f(0) + f(1)) + f(2)) + f(3)

    return pl.pallas_call(body, name=name, grid=(R // tr,),
                          in_specs=[pl.BlockSpec((4, tr, W), lambda r: (0, r, 0))],
                          out_specs=pl.BlockSpec((tr, W), lambda r: (r, 0)),
                          out_shape=jax.ShapeDtypeStruct((R, W), F32))(q)


def _pair_share(r, name):
    R, W = r.shape

    def body(r_ref, out_ref, send_sem, recv_sem):
        x, y, c = _me()
        cp = pltpu.make_async_remote_copy(src_ref=r_ref, dst_ref=out_ref, send_sem=send_sem, recv_sem=recv_sem,
                                          device_id=(x, y, 1 - c), device_id_type=MESH)
        cp.start()
        cp.wait()

    return pl.pallas_call(
        body, name=name, in_specs=[_HBM], out_specs=_HBM,
        out_shape=jax.ShapeDtypeStruct((R, W), r.dtype),
        scratch_shapes=[pltpu.SemaphoreType.DMA(()), pltpu.SemaphoreType.DMA(())],
    )(r)


def _allreduce_small(vec):
    R, W = vec.shape

    def body(v_ref, out_ref, buf, send_sems, recv_sems):
        x, y, c = _me()
        me = 4 * x + 2 * y + c
        buf[me] = v_ref[...]
        cps = []
        for kk in range(1, 8):
            peer = (x ^ (kk >> 2), y ^ ((kk >> 1) & 1), c ^ (kk & 1))
            cp = pltpu.make_async_remote_copy(src_ref=v_ref, dst_ref=buf.at[me], send_sem=send_sems.at[kk - 1],
                                              recv_sem=recv_sems.at[kk - 1], device_id=peer, device_id_type=MESH)
            cp.start()
            cps.append(cp)
        for kk in range(1, 8):
            peer = (x ^ (kk >> 2), y ^ ((kk >> 1) & 1), c ^ (kk & 1))
            pltpu.make_async_remote_copy(src_ref=v_ref, dst_ref=buf.at[me ^ kk], send_sem=send_sems.at[kk - 1],
                                         recv_sem=recv_sems.at[kk - 1], device_id=peer, device_id_type=MESH).wait_recv()
        for cp in cps:
            cp.wait_send()
        acc = buf[0]
        for d in range(1, 8):
            acc = acc + buf[d]
        out_ref[...] = acc

    vm = pl.BlockSpec(memory_space=pltpu.VMEM)
    return pl.pallas_call(
        body, name="allreduce_small", in_specs=[vm], out_specs=vm, out_shape=jax.ShapeDtypeStruct((R, W), F32),
        scratch_shapes=[pltpu.VMEM((8, R, W), F32), pltpu.SemaphoreType.DMA((7,)), pltpu.SemaphoreType.DMA((7,))],
    )(vec)


def _pack_flat(parts, align, dtype):
    flat = jnp.concatenate([p.reshape(-1).astype(dtype) for p in parts])
    n = flat.shape[0]
    pad = (-n) % align
    return jnp.pad(flat, (0, pad)) if pad else flat


def _unpack_flat(flat, shapes):
    out, off = [], 0
    for s in shapes:
        n = math.prod(s)
        out.append(flat[off:off + n].reshape(s))
        off += n
    return out


def _full_from_shards(name, sh):
    if name in COL_SHARDED:
        return jnp.transpose(sh, (1, 0, 2)).reshape(sh.shape[1], 4 * sh.shape[2])
    return sh.reshape(4 * sh.shape[1], sh.shape[2])


def _shards_from_full(name, full):
    if name in COL_SHARDED:
        r, cc = full.shape[0], full.shape[1] // 4
        return jnp.transpose(full.reshape(r, 4, cc), (1, 0, 2)).reshape(4, r * cc)
    return full.reshape(4, -1)


def _pad_heads(w, width):
    k = w.shape[0]
    return jnp.pad(w.reshape(k, HEADS, width), ((0, 0), (0, 0), (0, HP - width))).reshape(k, HEADS * HP)


def _unpad_heads(w, width):
    k = w.shape[0]
    return w.reshape(k, HEADS, HP)[:, :, :width].reshape(k, HEADS * width)


def _blockdiag(t):
    g, a, b = t.shape
    return jnp.einsum("gab,gk->gakb", t, jnp.eye(g, dtype=t.dtype)).reshape(g * a, g * b)


def _blockdiag_t(m, a, b):
    g = m.shape[0] // a
    return jnp.einsum("gagb->gab", m.reshape(g, a, g, b))


def _local_step(x, mem, positions, target, W, sp, overlap=None):
    T = x.shape[0]
    row = lambda v: v.reshape(1, -1)

    ab_re, ab_im, bb_re, bb_im = _s5_discretize(sp["s5_lam_re"], sp["s5_lam_im"], sp["s5_log_dt"], sp["s5_b_re"], sp["s5_b_im"])
    ar, ai = ab_re.reshape(1, S5_N), ab_im.reshape(1, S5_N)
    tab_fwd, tab_rev = _s5_tables(ar, ai, False), _s5_tables(ar, -ai, True)
    b_st = jnp.concatenate([_blockdiag(jnp.swapaxes(bb_re, 1, 2)), _blockdiag(jnp.swapaxes(bb_im, 1, 2))],
                           axis=1).astype(BF16)
    c_st = jnp.concatenate([_blockdiag(jnp.swapaxes(sp["s5_c_re"], 1, 2)),
                            -_blockdiag(jnp.swapaxes(sp["s5_c_im"], 1, 2))], axis=0).astype(BF16)
    inv = ROPE_THETA ** (-jnp.arange(0, ROPE, 2, dtype=F32) / ROPE)
    lane = jnp.arange(HP)
    rot1, rot2 = (lane >= NOPE) & (lane < NOPE + ROPE // 2), (lane >= NOPE + ROPE // 2) & (lane < QK)
    ang = positions.astype(F32)[:, None] * jnp.where(rot1 | rot2, inv[(lane - NOPE) % (ROPE // 2)], 0.0)
    sn = jnp.sin(ang)
    t_cos, t_s1, t_s2 = jnp.cos(ang), jnp.where(rot2, sn, 0.0), jnp.where(rot1, -sn, 0.0)

    w_in = W["w_in"]
    w_in_re = jnp.concatenate([w_in[:, 800:], w_in[:, :768], w_in[:, 768:800],
                               jnp.zeros((D_MODEL, HP - ROPE), w_in.dtype)], axis=1)
    w_uq_p = _pad_heads(W["w_uq"], QK)
    wkv = W["w_ukv"].reshape(S5_W, HEADS, NOPE + VD)
    w_uk_p = _pad_heads(wkv[:, :, :NOPE].reshape(S5_W, HEADS * NOPE), NOPE)
    w_uv_p = _pad_heads(wkv[:, :, NOPE:].reshape(S5_W, HEADS * VD), VD)
    w_oa_p = jnp.pad(W["w_oa"].reshape(HEADS, VD, D_MODEL), ((0, 0), (0, HP - VD), (0, 0))).reshape(HEADS * HP, D_MODEL)

    g_in, b_in = row(sp["ln_in_g"]), row(sp["ln_in_b"])
    g1, b1, g2, b2, g3, b3 = (row(sp[k]) for k in ("ln1_g", "ln1_b", "ln2_g", "ln2_b", "ln3_g", "ln3_b"))
    gq, gkv, d_skip = row(sp["q_norm_g"]), row(sp["kv_norm_g"]), row(sp["s5_d"])

    (h0,) = _rowwise(lambda a, g, b: (_ln(a, g, b),), "ln_in_fwd", [_whole(x)], [g_in, b_in], [(D_MODEL, F32)], tm=512)
    def in_epi(acc, ga, gb):
        zb = acc.astype(BF16).astype(F32)
        return acc, _rms(zb[:, Z_CQ:Z_CQ + S5_W], ga), _rms(zb[:, Z_CKV:Z_CKV + S5_W], gb)

    z, cqn, ckvn = _mm(h0, w_in_re, "in_proj_norm", tm=512, tn=Z_W, epi=in_epi, pextras=(gq, gkv),
                       outs=[(Z_W, BF16), (S5_W, BF16), (S5_W, BF16)])
    hs, yc, yg = _s5_fwd(z, b_st, c_st, d_skip, tab_fwd, "s5_fwd")
    y2 = _mm(yg, W["w_glu"], "glu_proj", tn=2048, out_dtype=BF16)
    tabs = [(t_cos, 0, HP), (t_s1, 0, HP), (t_s2, 0, HP)]

    def rope_q(acc, tc_, t1, t2):
        return _rope(acc, _tile8(tc_), _tile8(t1), _tile8(t2)) * (ATT_SCALE * LOG2E)

    def rope_k(acc, krz, tc_, t1, t2):
        return acc + _tile8(_rope(pltpu.roll(krz.astype(F32), NOPE, 1), tc_, t1, t2))

    q = _mm(cqn, w_uq_p, "mla_uq_rope", epi=rope_q, rextras=tabs, out_dtype=BF16)
    k = _mm(ckvn, w_uk_p, "mla_uk_rope", epi=rope_k, rextras=[(z, Z_KR, HP)] + tabs, out_dtype=BF16)
    v = _mm(ckvn, w_uv_p, "mla_uv", out_dtype=BF16)
    if overlap is None:
        o, lse = _flash_fwd(q, k, v, "mla_attn_fwd", bq=ATT_BQ_FWD)
    else:
        o, lse, landed = _flash_fwd(q, k, v, "mla_attn_fwd", bq=ATT_BQ_FWD, carry=(overlap[0], False))
        W = {**W, **overlap[1](landed)}
    def gate_epi(acc, gs_, ga_, ya, yb):
        f = lambda t: t.astype(F32)
        return acc, _gate_mix(f(gs_), f(ga_), f(ya), f(yb), acc.astype(BF16).astype(F32))

    a_out, mixin = _mm(o, w_oa_p, "mla_oa_gate", epi=gate_epi, outs=[(D_MODEL, BF16)] * 2, tm=512,
                       rextras=[(z, Z_GS, D_MODEL), (z, Z_GA, D_MODEL), (y2, 0, D_MODEL), (y2, D_MODEL, D_MODEL)])
    def ln_epi(acc, h, g, b):
        s_ = DN_ALPHA * h + acc
        return s_, _ln(s_, g, b)

    two_rows = [(D_MODEL, F32)] * 2
    s1, h1 = _mm(mixin, W["w_o"], "mix_o_ln1", epi=ln_epi, extras=(h0,), pextras=(g1, b1), outs=two_rows, tm=512)
    xq = _mm(h1, W["w_xq"], "x_q", out_dtype=BF16)
    xk = _mm(mem, W["w_xk"], "x_k", out_dtype=BF16)
    xv = _mm(mem, W["w_xv"], "x_v", out_dtype=BF16)
    (xo,) = _rowwise(_xattn_fwd_fn, "xattn_fwd", [_whole(xq)], [xk, xv], [(D_MODEL, BF16)], tm=2048, upcast=False)
    s2, h2 = _mm(xo, W["w_xo"], "x_o_ln2", epi=ln_epi, extras=(h1,), pextras=(g2, b2), outs=two_rows, tm=512)
    a_up = _mm(h2, W["w_up"], "mlp_up", out_dtype=BF16, tn=2048)
    def loss_epi(acc, h, tgt, g, b):
        def f(r_, g_, b_):
            e = _ln_res(h, r_, g_, b_) - tgt
            return 0.5 * jnp.sum(jnp.mean(e * e, axis=-1))
        lv, (dr_, dg_, db_) = jax.value_and_grad(f, argnums=(0, 1, 2))(acc, g, b)
        return dr_, dg_, db_, jnp.broadcast_to(lv, (1, 128))

    row_and_sums = [(D_MODEL, F32), (D_MODEL, "sum"), (D_MODEL, "sum")]
    dr3, dg3, db3, lossv = _mm(a_up, W["w_down"], "mlp_down_loss", act=_relu2, tm=512, tk=4 * D_MODEL, epi=loss_epi,
                               extras=(h2, target), pextras=(g3, b3), outs=row_and_sums + [(128, "sum")])
    gW, gs = {}, {"ln3_g": dg3, "ln3_b": db3}

    da = _mm(dr3, W["w_down"], "mlp_down_bwd_a", "nt", epi=lambda acc, a: acc * (2.0 * jnp.maximum(a, 0.0)), extras=(a_up,),
             out_dtype=BF16, tn=2048)
    gW["w_down"] = _mm(a_up, dr3, "mlp_down_bwd_w", "tn", m=4 * D_MODEL, n=D_MODEL, act=_relu2, tk=2048)
    gW["w_up"] = _mm(h2, da, "mlp_up_bwd_w", "tn", m=D_MODEL, n=4 * D_MODEL, tk=2048)

    def ln_bwd_epi(acc, e, s_, g, b):
        _, vjp = jax.vjp(_ln, s_, g, b)
        return vjp(acc + DN_ALPHA * e)

    dr2, gs["ln2_g"], gs["ln2_b"] = _mm(da, W["w_up"], "mlp_up_bwd_ln2", "nt", tm=512, tk=4 * D_MODEL, epi=ln_bwd_epi,
                                        extras=(dr3, s2), pextras=(g2, b2), outs=row_and_sums)
    dxo = _mm(dr2, W["w_xo"], "x_o_bwd_a", "nt", out_dtype=BF16)
    gW["w_xo"] = _mm(xo, dr2, "x_o_bwd_w", "tn", m=D_MODEL, n=D_MODEL, tk=2048)
    dxq, dxk, dxv = _rowwise(_xattn_bwd_fn, "xattn_bwd", [_whole(xq), _whole(dxo)], [xk, xv], [(D_MODEL, BF16)],
                             [(xk.shape[0], D_MODEL)] * 2, tm=2048, upcast=False)
    gW["w_xq"] = _mm(h1, dxq, "x_q_bwd_w", "tn", m=D_MODEL, n=D_MODEL, tk=2048)
    gW["w_xk"] = _mm(mem, dxk, "x_k_bwd_w", "tn", m=D_MODEL, n=D_MODEL)
    gW["w_xv"] = _mm(mem, dxv, "x_v_bwd_w", "tn", m=D_MODEL, n=D_MODEL)

    dr1, gs["ln1_g"], gs["ln1_b"] = _mm(dxq, W["w_xq"], "x_q_bwd_ln1", "nt", tm=512, epi=ln_bwd_epi,
                                        extras=(dr2, s1), pextras=(g1, b1), outs=row_and_sums)
    gW["w_o"] = _mm(mixin, dr1, "mix_o_bwd_w", "tn", m=D_MODEL, n=D_MODEL, tk=2048)

    def gate_bwd_epi(acc, *tiles):
        _, vjp = jax.vjp(_gate_mix, *[t.astype(F32) for t in tiles])
        return vjp(acc)

    dgs, dga, dy2a, dy2b, da_out = _mm(
        dr1, W["w_o"], "mix_o_bwd_gate", "nt", tm=512, epi=gate_bwd_epi, outs=[(D_MODEL, BF16)] * 5,
        rextras=[(z, Z_GS, D_MODEL), (z, Z_GA, D_MODEL), (y2, 0, D_MODEL), (y2, D_MODEL, D_MODEL), _whole(a_out)])

    def delta_epi(acc, o_):
        prod = acc * o_
        cols = [jnp.sum(prod[:, h * HP:(h + 1) * HP], axis=1, keepdims=True) for h in range(HEADS)]
        return acc, jnp.concatenate(cols, axis=1)

    do16, delta = _mm(da_out, w_oa_p, "mla_oa_bwd_a", "nt", epi=delta_epi, extras=(o,),
                      outs=[(HEADS * HP, BF16), (HEADS, F32)], tm=512)
    g_oa_p = _mm(o, da_out, "mla_oa_bwd_w", "tn", m=HEADS * HP, n=D_MODEL, tk=2048)
    gW["w_oa"] = g_oa_p.reshape(HEADS, HP, D_MODEL)[:, :VD].reshape(HEADS * VD, D_MODEL)
    delta_row = delta.T.reshape(HEADS, 1, T)
    if overlap is None:
        dq, dk, dv = _flash_bwd(q, k, v, do16, lse, delta_row, "mla_attn_bwd")
        riding = None
    else:
        sent = overlap[2]({n: gW[n] for n in BIG_LATE})
        dq, dk, dv, landed = _flash_bwd(q, k, v, do16, lse, delta_row, "mla_attn_bwd", carry=(sent, True))
        riding = (sent, landed)

    def rope_bwd(dq_, dk_, tc_, t1, t2):
        dqr = _rope_t(dq_, _tile8(tc_), _tile8(t1), _tile8(t2))
        dkr = dk_[:, 0:HP]
        for hh in range(1, HEADS):
            dkr = dkr + dk_[:, hh * HP:(hh + 1) * HP]
        dkr = _rope_t(jnp.where(_rope_lanes(dkr.shape), dkr, 0.0), tc_, t1, t2)
        dkr = pltpu.roll(dkr, NOPE, 1)
        lane = lax.broadcasted_iota(jnp.int32, dkr.shape, 1)
        return dqr, jnp.where(lane < ROPE, dkr, 0.0)

    dq_raw, dkr = _rowwise(rope_bwd, "mla_rope_bwd", [_whole(dq), _whole(dk), _whole(t_cos), _whole(t_s1), _whole(t_s2)], [],
                           [(HEADS * HP, BF16), (HP, BF16)], tm=512)

    def rms_bwd_epi(acc, *rest):
        _, vjp = jax.vjp(_rms, rest[-2].astype(F32), rest[-1])
        return vjp(acc + rest[0] if len(rest) == 3 else acc)

    row_and_sum = [(S5_W, BF16), (S5_W, "sum")]
    dcq, gs["q_norm_g"] = _mm(dq_raw, w_uq_p, "mla_uq_bwd_norm", "nt", epi=rms_bwd_epi, rextras=[(z, Z_CQ, S5_W)],
                              pextras=(gq,), outs=row_and_sum)
    gW["w_uq"] = _unpad_heads(_mm(cqn, dq_raw, "mla_uq_bwd_w", "tn", m=S5_W, n=HEADS * HP), QK)
    dckvn_k = _mm(dk, w_uk_p, "mla_uk_bwd_a", "nt")
    dckv, gs["kv_norm_g"] = _mm(dv, w_uv_p, "mla_uv_bwd_norm", "nt", epi=rms_bwd_epi, extras=(dckvn_k,),
                                rextras=[(z, Z_CKV, S5_W)], pextras=(gkv,), outs=row_and_sum)
    g_uk = _unpad_heads(_mm(ckvn, dk, "mla_uk_bwd_w", "tn", m=S5_W, n=HEADS * HP), NOPE).reshape(S5_W, HEADS, NOPE)
    g_uv = _unpad_heads(_mm(ckvn, dv, "mla_uv_bwd_w", "tn", m=S5_W, n=HEADS * HP), VD).reshape(S5_W, HEADS, VD)
    gW["w_ukv"] = jnp.concatenate([g_uk, g_uv], axis=2).reshape(S5_W, HEADS * (NOPE + VD))

    dy2 = jnp.concatenate([dy2a, dy2b], axis=1)
    dyg = _mm(dy2, W["w_glu"], "glu_bwd_a", "nt")
    gW["w_glu"] = _mm(yg, dy2, "glu_bwd_w", "tn", m=S5_W, n=2 * D_MODEL, tk=2048)

    du, gs["s5_ab"], g_bst, g_cst_t, gs["s5_d"] = _s5_bwd(dyg, yc, z, hs, b_st, c_st, d_skip, tab_rev, "s5_bwd")
    gs["s5_bb_re"] = jnp.swapaxes(_blockdiag_t(g_bst[:, :S5_N], S5_H, S5_P), 1, 2)
    gs["s5_bb_im"] = jnp.swapaxes(_blockdiag_t(g_bst[:, S5_N:], S5_H, S5_P), 1, 2)
    gs["s5_c_re"] = _blockdiag_t(g_cst_t[:, :S5_N], S5_H, S5_P)
    gs["s5_c_im"] = -_blockdiag_t(g_cst_t[:, S5_N:], S5_H, S5_P)

    dz = jnp.concatenate([dgs, dga, du, dcq, dckv, dkr], axis=1)
    g_in_re = _mm(h0, dz, "in_proj_bwd_w", "tn", m=D_MODEL, n=Z_W, tm=512, tn=Z_W)
    gW["w_in"] = jnp.concatenate([g_in_re[:, Z_U:Z_KR], g_in_re[:, Z_KR:Z_KR + ROPE], g_in_re[:, :Z_U]], axis=1)
    dx, gs["ln_in_g"], gs["ln_in_b"] = _mm(dz, w_in_re, "in_proj_bwd_ln_in", "nt", tm=512, tk=Z_W, epi=ln_bwd_epi,
                                           extras=(dr1, x), pextras=(g_in, b_in), outs=row_and_sums)
    return lossv[:, :1], dx, gW, gs, riding


_RAW_SMALL = (("loss", (1, 1)), ("ln_in_g", (1, D_MODEL)), ("ln_in_b", (1, D_MODEL)), ("ln1_g", (1, D_MODEL)),
              ("ln1_b", (1, D_MODEL)), ("ln2_g", (1, D_MODEL)), ("ln2_b", (1, D_MODEL)), ("ln3_g", (1, D_MODEL)),
              ("ln3_b", (1, D_MODEL)), ("q_norm_g", (1, S5_W)), ("kv_norm_g", (1, S5_W)), ("s5_d", (1, S5_W)),
              ("s5_ab", (1, 2 * S5_N)), ("s5_bb_re", (S5_G, S5_P, S5_H)), ("s5_bb_im", (S5_G, S5_P, S5_H)),
              ("s5_c_re", (S5_G, S5_H, S5_P)), ("s5_c_im", (S5_G, S5_H, S5_P)))


def kernel(x, mem, positions, ln_in_g, ln_in_b, w_in, s5_lam_re, s5_lam_im, s5_log_dt, s5_b_re, s5_b_im, s5_c_re, s5_c_im, s5_d, w_glu, q_norm_g, w_uq, kv_norm_g, w_ukv, w_oa, w_o, ln1_g, ln1_b, w_xq, w_xk, w_xv, w_xo, ln2_g, ln2_b, w_up, w_down, ln3_g, ln3_b, loss_target, m_ln_in_g, m_ln_in_b, m_w_in, m_s5_lam_re, m_s5_lam_im, m_s5_log_dt, m_s5_b_re, m_s5_b_im, m_s5_c_re, m_s5_c_im, m_s5_d, m_w_glu, m_q_norm_g, m_w_uq, m_kv_norm_g, m_w_ukv, m_w_oa, m_w_o, m_ln1_g, m_ln1_b, m_w_xq, m_w_xk, m_w_xv, m_w_xo, m_ln2_g, m_ln2_b, m_w_up, m_w_down, m_ln3_g, m_ln3_b, v_ln_in_g, v_ln_in_b, v_w_in, v_s5_lam_re, v_s5_lam_im, v_s5_log_dt, v_s5_b_re, v_s5_b_im, v_s5_c_re, v_s5_c_im, v_s5_d, v_w_glu, v_q_norm_g, v_w_uq, v_kv_norm_g, v_w_ukv, v_w_oa, v_w_o, v_ln1_g, v_ln1_b, v_w_xq, v_w_xk, v_w_xv, v_w_xo, v_ln2_g, v_ln2_b, v_w_up, v_w_down, v_ln3_g, v_ln3_b):
    a = dict(locals())
    wts = {n: a[n] for n in WEIGHTS}
    ms = {n: a["m_" + n] for n in WEIGHTS}
    vs = {n: a["v_" + n] for n in WEIGHTS}
    shard2d = {n: wts[n].reshape(wts[n].shape[-2], wts[n].shape[-1]) for n in BIG}
    nrow = {n: shard2d[n].size // 1024 for n in BIG}
    mine = 2 * lax.axis_index("x") + lax.axis_index("y")
    my_c = lax.axis_index("c")
    cidx = my_c.astype(jnp.int32).reshape(1)

    def group_rows(group):
        rows = sum(nrow[n] for n in group)
        return rows, (-rows) % PACK_ROWS

    def pack_shards(group):
        rows, pad = group_rows(group)
        parts = [shard2d[n].astype(BF16).reshape(nrow[n], 1024) for n in group] + [jnp.zeros((pad, 1024), BF16)]
        return jnp.concatenate(parts, axis=0).reshape(2, (rows + pad) // 2, 1024)

    def unpack_full(group, gathered):
        out, off = {}, 0
        for n in group:
            r, cc = shard2d[n].shape
            piece = lax.optimization_barrier(gathered[:, off:off + nrow[n]])
            out[n] = _full_from_shards(n, piece.reshape(4, r, cc))
            off += nrow[n]
        return out

    def pack_grads(group, gw):
        rows, pad = group_rows(group)
        parts = [_shards_from_full(n, gw[n].astype(BF16)).reshape(4, nrow[n], 1024) for n in group]
        return jnp.concatenate(parts + [jnp.zeros((4, pad, 1024), BF16)], axis=1).reshape(4, 2, (rows + pad) // 2, 1024)

    def by_core(own, other):
        return jnp.where(my_c == 0, jnp.stack([own, other]), jnp.stack([other, own]))

    packed_e, packed_l = pack_shards(BIG_EARLY), pack_shards(BIG_LATE)
    gathered = lax.dynamic_update_slice(_gather_weights(packed_e), packed_e[None], (mine, 0, 0, 0))
    W = unpack_full(BIG_EARLY, gathered.reshape(4, -1, 1024))

    def finish_gather(landed):
        half = packed_l.shape[1]
        own_half = lax.dynamic_index_in_dim(packed_l, my_c, axis=0, keepdims=True)
        landed = lax.dynamic_update_slice(landed, own_half, (mine, 0, 0))
        other = _pair_share(landed.reshape(4 * half, 1024), "gather_late_share").reshape(4, half, 1024)
        out, off = {}, 0
        for n in BIG_LATE:
            r, cc = shard2d[n].shape
            h, lo = divmod(off, half)
            assert lo + nrow[n] <= half, n
            piece = jnp.where(my_c == h, landed[:, lo:lo + nrow[n]], other[:, lo:lo + nrow[n]])
            out[n] = _full_from_shards(n, lax.optimization_barrier(piece).reshape(4, r, cc))
            off += nrow[n]
        return out

    def pair_reduce(group, gw, tag):
        gflat = pack_grads(group, gw)
        return _pair_add(gflat, _pair_exchange(gflat, "grad_pair_exchange_" + tag), cidx, "grad_pair_add_" + tag)

    def finish_reduce(group, sent, landed, tag):
        own = lax.dynamic_slice(sent, (mine, 0, 0), (1,) + sent.shape[1:])
        red = _chip_sum(lax.dynamic_update_slice(landed, own, (mine, 0, 0)), "grad_chip_sum_" + tag)
        gsh = by_core(red, _pair_share(red, "grad_pair_share_" + tag)).reshape(-1, 1024)
        out, off = {}, 0
        for n in group:
            out[n] = lax.optimization_barrier(gsh[off:off + nrow[n]]).reshape(shard2d[n].shape)
            off += nrow[n]
        return out

    sp = {n: wts[n] for n in SMALL}
    sp_local = {n: (sp[n][0] if sp[n].ndim > 1 else sp[n]) for n in SMALL}
    overlap = (packed_l, finish_gather, lambda gw: pair_reduce(BIG_LATE, gw, "late"))
    lossv, dx, gW, gs, riding = _local_step(x[0], mem[0], positions[0], loss_target[0], W, sp_local, overlap)

    g_out = finish_reduce(BIG_LATE, riding[0], riding[1], "late")
    sent_e = pair_reduce(BIG_EARLY, gW, "early")
    g_out.update(finish_reduce(BIG_EARLY, sent_e, _chip_exchange(sent_e, "grad_chip_exchange_early"), "early"))

    gs["loss"] = lossv
    raw = _pack_flat([gs[n].reshape(s) for n, s in _RAW_SMALL], 8 * 1024, F32)
    raw = _allreduce_small(raw.reshape(-1, 1024)).reshape(-1)
    rs = dict(zip([n for n, _ in _RAW_SMALL], _unpack_flat(raw, [s for _, s in _RAW_SMALL])))
    loss = rs["loss"].reshape(())
    _, disc_vjp = jax.vjp(_s5_discretize, sp_local["s5_lam_re"], sp_local["s5_lam_im"], sp_local["s5_log_dt"],
                          sp_local["s5_b_re"], sp_local["s5_b_im"])
    d_ab = rs["s5_ab"].reshape(2, S5_G, S5_P)
    g_lre, g_lim, g_ldt, g_bre, g_bim = disc_vjp((d_ab[0], d_ab[1], rs["s5_bb_re"], rs["s5_bb_im"]))
    small_g = {"s5_lam_re": g_lre, "s5_lam_im": g_lim, "s5_log_dt": g_ldt, "s5_b_re": g_bre, "s5_b_im": g_bim,
               "s5_c_re": rs["s5_c_re"], "s5_c_im": rs["s5_c_im"]}
    for n in ("ln_in_g", "ln_in_b", "ln1_g", "ln1_b", "ln2_g", "ln2_b", "ln3_g", "ln3_b", "q_norm_g", "kv_norm_g", "s5_d"):
        small_g[n] = rs[n]

    grads, deltas, new_m, new_v = {}, {}, {}, {}
    for n in BIG:
        d_, m_, v_ = _adamw(shard2d[n], g_out[n], ms[n].reshape(shard2d[n].shape), vs[n].reshape(shard2d[n].shape), "adamw_" + n)
        grads[n] = g_out[n].reshape(wts[n].shape)
        deltas[n], new_m[n], new_v[n] = (t.reshape(wts[n].shape) for t in (d_, m_, v_))
    as2d = lambda t: t.reshape(-1, t.shape[-1])
    sd, sm, sv = _adamw_many([as2d(wts[n]) for n in SMALL], [as2d(small_g[n].reshape(wts[n].shape)) for n in SMALL],
                             [as2d(ms[n]) for n in SMALL], [as2d(vs[n]) for n in SMALL], "adamw_small")
    for n, d_, m_, v_ in zip(SMALL, sd, sm, sv):
        grads[n] = small_g[n].reshape(wts[n].shape)
        deltas[n], new_m[n], new_v[n] = (t.reshape(wts[n].shape) for t in (d_, m_, v_))

    return (loss, dx[None], *[grads[n] for n in WEIGHTS], *[deltas[n] for n in WEIGHTS],
            *[new_m[n] for n in WEIGHTS], *[new_v[n] for n in WEIGHTS])
```
